```python
import jax, jax.numpy as jnp
from jax import lax
import numpy as np

D_MODEL = 1024
BATCH = 8
SEQ = 16384
DEPTH = 4

PLE_DIM = 256
HG_WIDTH = 512
HG_HEADS = 4
HG_DK = HG_WIDTH // HG_HEADS
HG_CHUNK = 64
AT_WIDTH = D_MODEL - HG_WIDTH
AT_HEAD_DIM = 64
AT_Q_HEADS = AT_WIDTH // AT_HEAD_DIM
AT_KV_HEADS = 2
AT_GROUP = AT_Q_HEADS // AT_KV_HEADS
WINDOW = 128
D_FF = 2816
CONV_W = 3
EPS = 1e-6
MASK_VALUE = -1e30
LB_FLOOR = 1e-30
SPLIT_SIZES = [HG_WIDTH, HG_WIDTH, HG_WIDTH, HG_WIDTH, AT_WIDTH,
               AT_KV_HEADS * AT_HEAD_DIM, AT_KV_HEADS * AT_HEAD_DIM]
IN_WIDTH = sum(SPLIT_SIZES)

kernel_name = "hymba_style_hgrn2_swa_sink_convffn_ple"


def rmsnorm(x, g):
    xf = x.astype(jnp.float32)
    y = xf * lax.rsqrt(jnp.mean(xf * xf, axis=-1, keepdims=True) + EPS)
    return (y * g.astype(jnp.float32)).astype(x.dtype)


def hgrn2_mixer(q_pre, f_pre, i_in, g_pre, lb, norm_g):
    B, S, _ = q_pre.shape
    nc = S // HG_CHUNK
    f32 = jnp.float32
    lbf = lb.astype(f32)
    z = f_pre.astype(f32)
    q = jax.nn.silu(q_pre.astype(f32))
    logf = jnp.logaddexp(jnp.log(jnp.maximum(lbf, LB_FLOOR)),
                         jnp.log1p(-lbf) + jax.nn.log_sigmoid(z))
    k = (1.0 - lbf) * jax.nn.sigmoid(-z)
    v = i_in.astype(f32)

    def to_chunks(t):
        return t.reshape(B, nc, HG_CHUNK, HG_HEADS, HG_DK).transpose(1, 0, 3, 2, 4)

    qc, kc, vc, lc = to_chunks(q), to_chunks(k), to_chunks(v), to_chunks(logf)
    causal = jnp.tril(jnp.ones((HG_CHUNK, HG_CHUNK), dtype=bool))[:, :, None]

    def step(state, inp):
        qi, ki, vi, li = inp
        b = jnp.cumsum(li, axis=2)
        diff = b[:, :, :, None, :] - b[:, :, None, :, :]
        decay = jnp.where(causal, jnp.exp(jnp.where(causal, diff, 0.0)), 0.0)
        att = jnp.einsum('bhtk,bhsk,bhtsk->bhts', qi, ki, decay)
        o = (jnp.einsum('bhts,bhsv->bhtv', att, vi)
             + jnp.einsum('bhtk,bhkv->bhtv', qi * jnp.exp(b), state))
        b_last = b[:, :, -1:, :]
        new_state = (jnp.exp(b_last[:, :, 0, :])[..., None] * state
                     + jnp.einsum('bhsk,bhsv->bhkv', ki * jnp.exp(b_last - b), vi))
        return new_state, o

    s0 = jnp.zeros((B, HG_HEADS, HG_DK, HG_DK), f32)
    _, o = lax.scan(step, s0, (qc, kc, vc, lc))
    o = o.transpose(1, 0, 3, 2, 4).reshape(B, S, HG_HEADS, HG_DK)
    o = o * lax.rsqrt(jnp.mean(o * o, axis=-1, keepdims=True) + EPS)
    o = o * norm_g.astype(f32).reshape(HG_HEADS, HG_DK)
    o = o.reshape(B, S, HG_WIDTH) * jax.nn.sigmoid(g_pre.astype(f32))
    return o.astype(i_in.dtype)


def swa_sink_attention(q, k, v, sinks):
    B, S, _ = q.shape
    nb = S // WINDOW
    q = q.reshape(B, nb, WINDOW, AT_KV_HEADS, AT_GROUP, AT_HEAD_DIM)
    k = k.reshape(B, nb, WINDOW, AT_KV_HEADS, AT_HEAD_DIM)
    v = v.reshape(B, nb, WINDOW, AT_KV_HEADS, AT_HEAD_DIM)
    pad = ((0, 0), (1, 0), (0, 0), (0, 0), (0, 0))
    kk = jnp.concatenate([jnp.pad(k[:, :-1], pad), k], axis=2)
    vv = jnp.concatenate([jnp.pad(v[:, :-1], pad), v], axis=2)
    scale = AT_HEAD_DIM ** -0.5
    scores = jnp.einsum('bnqhgd,bnkhd->bnhgqk', q, kk).astype(jnp.float32) * scale
    qpos = jnp.arange(WINDOW)[:, None] + WINDOW
    kpos = jnp.arange(2 * WINDOW)[None, :]
    rel = qpos - kpos
    band = (rel >= 0) & (rel < WINDOW)
    first = (jnp.arange(nb) == 0)[:, None, None]
    valid = band[None] & ~(first & (kpos < WINDOW)[None])
    scores = jnp.where(valid[None, :, None, None], scores, MASK_VALUE)
    sink = sinks.astype(jnp.float32).reshape(AT_KV_HEADS, AT_GROUP)[None, None, :, :, None, None]
    m = jnp.maximum(jnp.max(scores, axis=-1, keepdims=True), sink)
    e = jnp.exp(scores - m)
    probs = (e / (jnp.sum(e, axis=-1, keepdims=True) + jnp.exp(sink - m))).astype(v.dtype)
    out = jnp.einsum('bnhgqk,bnkhd->bnqhgd', probs, vv)
    return out.reshape(B, S, AT_WIDTH)


def conv_ffn(u, w_up, conv_w, conv_b, w_down):
    h = u @ w_up
    a, b = h[..., :D_FF], h[..., D_FF:]
    a = lax.conv_general_dilated(
        a, conv_w[:, None, :], window_strides=(1,), padding=[(CONV_W - 1, 0)],
        dimension_numbers=('NWC', 'WIO', 'NWC'), feature_group_count=D_FF) + conv_b
    return (jax.nn.silu(a) * b) @ w_down


def _fwd_setup_inputs(seed: int = 0) -> dict:
    key = jax.random.key(seed)
    ks = jax.random.split(key, 20)
    f32 = jnp.float32
    nrm = lambda k, shape, s: jax.random.normal(k, shape, f32) * s
    res_scale = (2.0 * DEPTH) ** -0.5
    return {
        "x": nrm(ks[0], (BATCH, SEQ, D_MODEL), 1.0),
        "p": nrm(ks[1], (DEPTH, BATCH, SEQ, PLE_DIM), 1.0),
        "g_mix": 1.0 + nrm(ks[2], (DEPTH, D_MODEL), 0.02),
        "w_in": nrm(ks[3], (DEPTH, D_MODEL, IN_WIDTH), D_MODEL ** -0.5),
        "lb_logits": nrm(ks[4], (DEPTH, HG_WIDTH), 0.1),
        "hg_norm_g": 1.0 + nrm(ks[5], (DEPTH, HG_WIDTH), 0.02),
        "attn_sinks": nrm(ks[6], (DEPTH, AT_Q_HEADS), 0.5),
        "w_out": nrm(ks[7], (DEPTH, D_MODEL, D_MODEL), D_MODEL ** -0.5 * res_scale),
        "g_ffn": 1.0 + nrm(ks[8], (DEPTH, D_MODEL), 0.02),
        "w_up": nrm(ks[9], (DEPTH, D_MODEL, 2 * D_FF), D_MODEL ** -0.5),
        "conv_w": nrm(ks[10], (DEPTH, CONV_W, D_FF), CONV_W ** -0.5),
        "conv_b": nrm(ks[11], (DEPTH, D_FF), 0.01),
        "w_down": nrm(ks[12], (DEPTH, D_FF, D_MODEL), D_FF ** -0.5 * res_scale),
        "g_ple": 1.0 + nrm(ks[13], (DEPTH, D_MODEL), 0.02),
        "w_ple_gate": nrm(ks[14], (DEPTH, D_MODEL, D_MODEL), D_MODEL ** -0.5),
        "w_ple_up": nrm(ks[15], (DEPTH, PLE_DIM, D_MODEL), PLE_DIM ** -0.5 * res_scale),
        "g_final": 1.0 + nrm(ks[16], (D_MODEL,), 0.02),
    }


def _fwd_reference(x, p, g_mix, w_in, lb_logits, hg_norm_g, attn_sinks, w_out, g_ffn,
              w_up, conv_w, conv_b, w_down, g_ple, w_ple_gate, w_ple_up, g_final):
    lb_p = jax.nn.softmax(lb_logits.astype(jnp.float32), axis=0)
    lb_all = jnp.cumsum(lb_p, axis=0) - lb_p[0]
    cuts = list(np.cumsum(SPLIT_SIZES)[:-1])
    h = x
    for i in range(DEPTH):
        u = rmsnorm(h, g_mix[i])
        proj = u @ w_in[i]
        hq, hf, hi, hg, aq, ak, av = jnp.split(proj, cuts, axis=-1)
        y_hg = hgrn2_mixer(hq, hf, hi, hg, lb_all[i], hg_norm_g[i])
        y_at = swa_sink_attention(aq, ak, av, attn_sinks[i])
        h = h + jnp.concatenate([y_hg, y_at], axis=-1) @ w_out[i]
        h = h + conv_ffn(rmsnorm(h, g_ffn[i]), w_up[i], conv_w[i], conv_b[i], w_down[i])
        gate = jax.nn.sigmoid(rmsnorm(h, g_ple[i]) @ w_ple_gate[i])
        h = h + gate * (p[i] @ w_ple_up[i])
    return rmsnorm(h, g_final)


import jax as _jax
import jax.numpy as _jnp

TWIN_FORMAT = 'train_step'
FWD_PARAMS = ['x', 'p', 'g_mix', 'w_in', 'lb_logits', 'hg_norm_g', 'attn_sinks', 'w_out', 'g_ffn', 'w_up', 'conv_w', 'conv_b', 'w_down', 'g_ple', 'w_ple_gate', 'w_ple_up', 'g_final']
TWIN_WEIGHTS = ['g_mix', 'w_in', 'lb_logits', 'hg_norm_g', 'attn_sinks', 'w_out', 'g_ffn', 'w_up', 'conv_w', 'conv_b', 'w_down', 'g_ple', 'w_ple_gate', 'w_ple_up', 'g_final']
TWIN_DIFF_INPUT = 'x'
TWIN_INPUTS = ['x', 'p', 'g_mix', 'w_in', 'lb_logits', 'hg_norm_g', 'attn_sinks', 'w_out', 'g_ffn', 'w_up', 'conv_w', 'conv_b', 'w_down', 'g_ple', 'w_ple_gate', 'w_ple_up', 'g_final', 'loss_target', 'm_g_mix', 'm_w_in', 'm_lb_logits', 'm_hg_norm_g', 'm_attn_sinks', 'm_w_out', 'm_g_ffn', 'm_w_up', 'm_conv_w', 'm_conv_b', 'm_w_down', 'm_g_ple', 'm_w_ple_gate', 'm_w_ple_up', 'm_g_final', 'v_g_mix', 'v_w_in', 'v_lb_logits', 'v_hg_norm_g', 'v_attn_sinks', 'v_w_out', 'v_g_ffn', 'v_w_up', 'v_conv_w', 'v_conv_b', 'v_w_down', 'v_g_ple', 'v_w_ple_gate', 'v_w_ple_up', 'v_g_final']
TWIN_OUTPUTS = ['loss', 'grad_x', 'grad_g_mix', 'grad_w_in', 'grad_lb_logits', 'grad_hg_norm_g', 'grad_attn_sinks', 'grad_w_out', 'grad_g_ffn', 'grad_w_up', 'grad_conv_w', 'grad_conv_b', 'grad_w_down', 'grad_g_ple', 'grad_w_ple_gate', 'grad_w_ple_up', 'grad_g_final', 'delta_g_mix', 'delta_w_in', 'delta_lb_logits', 'delta_hg_norm_g', 'delta_attn_sinks', 'delta_w_out', 'delta_g_ffn', 'delta_w_up', 'delta_conv_w', 'delta_conv_b', 'delta_w_down', 'delta_g_ple', 'delta_w_ple_gate', 'delta_w_ple_up', 'delta_g_final', 'new_m_g_mix', 'new_m_w_in', 'new_m_lb_logits', 'new_m_hg_norm_g', 'new_m_attn_sinks', 'new_m_w_out', 'new_m_g_ffn', 'new_m_w_up', 'new_m_conv_w', 'new_m_conv_b', 'new_m_w_down', 'new_m_g_ple', 'new_m_w_ple_gate', 'new_m_w_ple_up', 'new_m_g_final', 'new_v_g_mix', 'new_v_w_in', 'new_v_lb_logits', 'new_v_hg_norm_g', 'new_v_attn_sinks', 'new_v_w_out', 'new_v_g_ffn', 'new_v_w_up', 'new_v_conv_w', 'new_v_conv_b', 'new_v_w_down', 'new_v_g_ple', 'new_v_w_ple_gate', 'new_v_w_ple_up', 'new_v_g_final']
TWIN_LEAF_KINDS = {'loss': 'loss', 'grad_x': 'grad_x', 'grad_g_mix': 'grad_w', 'grad_w_in': 'grad_w', 'grad_lb_logits': 'grad_w', 'grad_hg_norm_g': 'grad_w', 'grad_attn_sinks': 'grad_w', 'grad_w_out': 'grad_w', 'grad_g_ffn': 'grad_w', 'grad_w_up': 'grad_w', 'grad_conv_w': 'grad_w', 'grad_conv_b': 'grad_w', 'grad_w_down': 'grad_w', 'grad_g_ple': 'grad_w', 'grad_w_ple_gate': 'grad_w', 'grad_w_ple_up': 'grad_w', 'grad_g_final': 'grad_w', 'delta_g_mix': 'delta_w', 'delta_w_in': 'delta_w', 'delta_lb_logits': 'delta_w', 'delta_hg_norm_g': 'delta_w', 'delta_attn_sinks': 'delta_w', 'delta_w_out': 'delta_w', 'delta_g_ffn': 'delta_w', 'delta_w_up': 'delta_w', 'delta_conv_w': 'delta_w', 'delta_conv_b': 'delta_w', 'delta_w_down': 'delta_w', 'delta_g_ple': 'delta_w', 'delta_w_ple_gate': 'delta_w', 'delta_w_ple_up': 'delta_w', 'delta_g_final': 'delta_w', 'new_m_g_mix': 'new_m', 'new_m_w_in': 'new_m', 'new_m_lb_logits': 'new_m', 'new_m_hg_norm_g': 'new_m', 'new_m_attn_sinks': 'new_m', 'new_m_w_out': 'new_m', 'new_m_g_ffn': 'new_m', 'new_m_w_up': 'new_m', 'new_m_conv_w': 'new_m', 'new_m_conv_b': 'new_m', 'new_m_w_down': 'new_m', 'new_m_g_ple': 'new_m', 'new_m_w_ple_gate': 'new_m', 'new_m_w_ple_up': 'new_m', 'new_m_g_final': 'new_m', 'new_v_g_mix': 'new_v', 'new_v_w_in': 'new_v', 'new_v_lb_logits': 'new_v', 'new_v_hg_norm_g': 'new_v', 'new_v_attn_sinks': 'new_v', 'new_v_w_out': 'new_v', 'new_v_g_ffn': 'new_v', 'new_v_w_up': 'new_v', 'new_v_conv_w': 'new_v', 'new_v_conv_b': 'new_v', 'new_v_w_down': 'new_v', 'new_v_g_ple': 'new_v', 'new_v_w_ple_gate': 'new_v', 'new_v_w_ple_up': 'new_v', 'new_v_g_final': 'new_v'}


def _forward(args):
    return _fwd_reference(*[args[k] for k in FWD_PARAMS])


def _output_shape():
    def fwd():
        inp = _fwd_setup_inputs(0)
        return _fwd_reference(*[inp[k] for k in FWD_PARAMS])
    out = _jax.eval_shape(fwd)
    return out.shape, out.dtype

N_MICROBATCH = 1
ADAM_LR = 0.001
ADAM_B1 = 0.9
ADAM_B2 = 0.999
ADAM_EPS = 1e-08
ADAM_WD = 0.01
ADAM_STEP = 10
PER_EXAMPLE_BATCH_AXIS = {'x': 0, 'p': 1, 'loss_target': 0}
SHARED_INPUTS = []
_WEIGHT_DTYPES = {'g_mix': _jnp.float32, 'w_in': _jnp.float32, 'lb_logits': _jnp.float32, 'hg_norm_g': _jnp.float32, 'attn_sinks': _jnp.float32, 'w_out': _jnp.float32, 'g_ffn': _jnp.float32, 'w_up': _jnp.float32, 'conv_w': _jnp.float32, 'conv_b': _jnp.float32, 'w_down': _jnp.float32, 'g_ple': _jnp.float32, 'w_ple_gate': _jnp.float32, 'w_ple_up': _jnp.float32, 'g_final': _jnp.float32}
MOMENT_SCALE = {'g_mix': 5.772519e-02, 'w_in': 3.474870e-02, 'lb_logits': 5.816036e-03, 'hg_norm_g': 6.651577e-02, 'attn_sinks': 2.120679e-02, 'w_out': 1.408726e-01, 'g_ffn': 1.011614e-01, 'w_up': 4.317987e-02, 'conv_w': 4.369291e-02, 'conv_b': 4.153337e-02, 'w_down': 1.997226e-01, 'g_ple': 2.397062e-02, 'w_ple_gate': 2.416917e-02, 'w_ple_up': 1.752204e-01, 'g_final': 1.281142e+02}


def _to_microbatches(a, axis):
    t = _jnp.moveaxis(a, axis, 0)
    t = t.reshape((N_MICROBATCH, t.shape[0] // N_MICROBATCH) + t.shape[1:])
    return _jnp.moveaxis(t, 1, axis + 1)


def setup_inputs(seed: int = 0) -> dict:
    inp = _fwd_setup_inputs(seed)
    key = _jax.random.fold_in(_jax.random.key(seed), 7919)
    shape, _ = _output_shape()
    out = dict(inp)
    out["loss_target"] = _jax.random.normal(_jax.random.fold_in(key, 0), shape, _jnp.float32)
    for i, name in enumerate(TWIN_WEIGHTS):
        w = inp[name].astype(_jnp.float32)
        if MOMENT_SCALE is None:
            s = _jnp.sqrt(_jnp.mean(_jnp.square(w)) + 1e-30)
        else:
            s = MOMENT_SCALE[name]
        km, kv = _jax.random.split(_jax.random.fold_in(key, i + 1))
        out[name] = w
        out["m_" + name] = s * _jax.random.normal(km, w.shape, _jnp.float32)
        out["v_" + name] = (s * s) * _jax.random.uniform(kv, w.shape, _jnp.float32, 0.5, 1.5)
    if N_MICROBATCH > 1:
        for name, axis in PER_EXAMPLE_BATCH_AXIS.items():
            out[name] = _to_microbatches(out[name], axis)
    return {'x': out['x'], 'p': out['p'], 'g_mix': out['g_mix'], 'w_in': out['w_in'], 'lb_logits': out['lb_logits'], 'hg_norm_g': out['hg_norm_g'], 'attn_sinks': out['attn_sinks'], 'w_out': out['w_out'], 'g_ffn': out['g_ffn'], 'w_up': out['w_up'], 'conv_w': out['conv_w'], 'conv_b': out['conv_b'], 'w_down': out['w_down'], 'g_ple': out['g_ple'], 'w_ple_gate': out['w_ple_gate'], 'w_ple_up': out['w_ple_up'], 'g_final': out['g_final'], 'loss_target': out['loss_target'], 'm_g_mix': out['m_g_mix'], 'm_w_in': out['m_w_in'], 'm_lb_logits': out['m_lb_logits'], 'm_hg_norm_g': out['m_hg_norm_g'], 'm_attn_sinks': out['m_attn_sinks'], 'm_w_out': out['m_w_out'], 'm_g_ffn': out['m_g_ffn'], 'm_w_up': out['m_w_up'], 'm_conv_w': out['m_conv_w'], 'm_conv_b': out['m_conv_b'], 'm_w_down': out['m_w_down'], 'm_g_ple': out['m_g_ple'], 'm_w_ple_gate': out['m_w_ple_gate'], 'm_w_ple_up': out['m_w_ple_up'], 'm_g_final': out['m_g_final'], 'v_g_mix': out['v_g_mix'], 'v_w_in': out['v_w_in'], 'v_lb_logits': out['v_lb_logits'], 'v_hg_norm_g': out['v_hg_norm_g'], 'v_attn_sinks': out['v_attn_sinks'], 'v_w_out': out['v_w_out'], 'v_g_ffn': out['v_g_ffn'], 'v_w_up': out['v_w_up'], 'v_conv_w': out['v_conv_w'], 'v_conv_b': out['v_conv_b'], 'v_w_down': out['v_w_down'], 'v_g_ple': out['v_g_ple'], 'v_w_ple_gate': out['v_w_ple_gate'], 'v_w_ple_up': out['v_w_ple_up'], 'v_g_final': out['v_g_final']}


def _loss(weights, diff, rest, loss_target):
    with _jax.named_scope("forward"):
        args = {**rest, TWIN_DIFF_INPUT: diff, **{k: w.astype(_WEIGHT_DTYPES[k]) for k, w in weights.items()}}
        y = _forward(args)
    with _jax.named_scope("loss_head"):
        err = _jnp.square(y.astype(_jnp.float32) - loss_target)
        return 0.5 * _jnp.sum(_jnp.mean(err, axis=-1)) if err.ndim else 0.5 * err


def _adamw(w, g, m, v):
    m = ADAM_B1 * m + (1.0 - ADAM_B1) * g
    v = ADAM_B2 * v + (1.0 - ADAM_B2) * _jnp.square(g)
    m_hat = m / (1.0 - ADAM_B1 ** ADAM_STEP)
    v_hat = v / (1.0 - ADAM_B2 ** ADAM_STEP)
    delta = -ADAM_LR * (m_hat / (_jnp.sqrt(v_hat) + ADAM_EPS) + ADAM_WD * w)
    return delta, m, v


def reference(x, p, g_mix, w_in, lb_logits, hg_norm_g, attn_sinks, w_out, g_ffn, w_up, conv_w, conv_b, w_down, g_ple, w_ple_gate, w_ple_up, g_final, loss_target, m_g_mix, m_w_in, m_lb_logits, m_hg_norm_g, m_attn_sinks, m_w_out, m_g_ffn, m_w_up, m_conv_w, m_conv_b, m_w_down, m_g_ple, m_w_ple_gate, m_w_ple_up, m_g_final, v_g_mix, v_w_in, v_lb_logits, v_hg_norm_g, v_attn_sinks, v_w_out, v_g_ffn, v_w_up, v_conv_w, v_conv_b, v_w_down, v_g_ple, v_w_ple_gate, v_w_ple_up, v_g_final):
    given = dict(x=x, p=p, g_mix=g_mix, w_in=w_in, lb_logits=lb_logits, hg_norm_g=hg_norm_g, attn_sinks=attn_sinks, w_out=w_out, g_ffn=g_ffn, w_up=w_up, conv_w=conv_w, conv_b=conv_b, w_down=w_down, g_ple=g_ple, w_ple_gate=w_ple_gate, w_ple_up=w_ple_up, g_final=g_final, loss_target=loss_target, m_g_mix=m_g_mix, m_w_in=m_w_in, m_lb_logits=m_lb_logits, m_hg_norm_g=m_hg_norm_g, m_attn_sinks=m_attn_sinks, m_w_out=m_w_out, m_g_ffn=m_g_ffn, m_w_up=m_w_up, m_conv_w=m_conv_w, m_conv_b=m_conv_b, m_w_down=m_w_down, m_g_ple=m_g_ple, m_w_ple_gate=m_w_ple_gate, m_w_ple_up=m_w_ple_up, m_g_final=m_g_final, v_g_mix=v_g_mix, v_w_in=v_w_in, v_lb_logits=v_lb_logits, v_hg_norm_g=v_hg_norm_g, v_attn_sinks=v_attn_sinks, v_w_out=v_w_out, v_g_ffn=v_g_ffn, v_w_up=v_w_up, v_conv_w=v_conv_w, v_conv_b=v_conv_b, v_w_down=v_w_down, v_g_ple=v_g_ple, v_w_ple_gate=v_w_ple_gate, v_w_ple_up=v_w_ple_up, v_g_final=v_g_final)
    weights = {n: given[n] for n in TWIN_WEIGHTS}
    shared = {n: given[n] for n in SHARED_INPUTS}
    per_example = {n: given[n] for n in ['x', 'p']}
    grad_fn = _jax.value_and_grad(_loss, argnums=(0, 1))

    def one_microbatch(ex, loss_target):
        ex = dict(ex)
        diff = ex.pop(TWIN_DIFF_INPUT)
        return grad_fn(weights, diff, {**shared, **ex}, loss_target)

    if N_MICROBATCH == 1:
        loss, (grad_w, grad_x) = one_microbatch(per_example, given["loss_target"])
    else:
        def body(carry, xs):
            loss_sum, grad_sum = carry
            l_k, (gw_k, gx_k) = one_microbatch(xs[0], xs[1])
            with _jax.named_scope("update"):
                return (loss_sum + l_k, _jax.tree.map(_jnp.add, grad_sum, gw_k)), gx_k

        init = (_jnp.zeros((), _jnp.float32), _jax.tree.map(_jnp.zeros_like, weights))
        (loss, grad_w), grad_x = _jax.lax.scan(body, init, (per_example, given["loss_target"]))
    with _jax.named_scope("update"):
        delta_w, new_m, new_v = {}, {}, {}
        for n in TWIN_WEIGHTS:
            delta_w[n], new_m[n], new_v[n] = _adamw(weights[n], grad_w[n], given["m_" + n], given["v_" + n])
    return (loss, grad_x, *[grad_w[n] for n in TWIN_WEIGHTS], *[delta_w[n] for n in TWIN_WEIGHTS],
            *[new_m[n] for n in TWIN_WEIGHTS], *[new_v[n] for n in TWIN_WEIGHTS])
```

```python
import functools

import jax
import jax.numpy as jnp
from jax import lax
from jax.experimental import pallas as pl
from jax.experimental.pallas import tpu as pltpu

F32 = jnp.float32
BF16 = jnp.bfloat16

D_MODEL = 1024
DEPTH = 4
PLE_DIM = 256
HG_WIDTH = 512
HG_HEADS = 4
HG_DK = 128
HG_CHUNK = 64
HG_SUB = 16
AT_WIDTH = 512
AT_HEAD_DIM = 64
AT_Q_HEADS = 8
AT_KV_HEADS = 2
AT_GROUP = 4
WINDOW = 128
D_FF = 2816
IN_WIDTH = 2816
EPS = 1e-6
MASK_VALUE = -1e30
LB_FLOOR = 1e-30

ADAM_LR = 0.001
ADAM_B1 = 0.9
ADAM_B2 = 0.999
ADAM_EPS = 1e-08
ADAM_WD = 0.01
ADAM_STEP = 10

VMEM_LIMIT = 48 * 1024 * 1024


def _params(*sem):
    return pltpu.CompilerParams(dimension_semantics=sem, vmem_limit_bytes=VMEM_LIMIT)


def _dot(a, b, dims=(((1,), (0,)), ((), ()))):
    return lax.dot_general(a.astype(BF16), b.astype(BF16), dims, preferred_element_type=F32)


def _dot_nt(a, b):
    return _dot(a, b, (((1,), (1,)), ((), ())))


def _dot_tn(a, b):
    return _dot(a, b, (((0,), (0,)), ((), ())))


def _dot_f32(a, b, dims=(((1,), (0,)), ((), ()))):
    return lax.dot_general(a, b, dims, preferred_element_type=F32, precision=lax.Precision.HIGHEST)


def _sigmoid(x):
    return 1.0 / (1.0 + jnp.exp(-x))


def _logsig(x):
    return jnp.minimum(x, 0.0) - jnp.log(1.0 + jnp.exp(-jnp.abs(x)))


def _colsum(x):
    return jnp.sum(x, axis=0, keepdims=True)


def _rowsum(x):
    return jnp.sum(x, axis=1, keepdims=True)


def _tri(n):
    r = lax.broadcasted_iota(jnp.int32, (n, n), 0)
    c = lax.broadcasted_iota(jnp.int32, (n, n), 1)
    return (r >= c).astype(F32)


def _hg_gates(qp, z, a, c, oml):
    sq = _sigmoid(qp)
    q = qp * sq
    t = c + _logsig(z)
    mx = jnp.maximum(a, t)
    logf = mx + jnp.log(1.0 + jnp.exp(-jnp.abs(a - t)))
    snz = _sigmoid(-z)
    k = oml * snz
    return q, sq, t, logf, snz, k


def _hg_chunk_fwd(q, k, v, logf, st, b_s, k_s, v_s):
    C, U = HG_CHUNK, HG_SUB
    b = _dot_f32(_tri(C), logf)
    b_s[...] = b
    k_s[...] = k
    v_s[...] = v
    eb = jnp.exp(b)
    o = _dot_nt(q * eb, st)
    rows = lax.broadcasted_iota(jnp.int32, (C, HG_DK), 0)
    trow = lax.broadcasted_iota(jnp.int32, (U, HG_DK), 0)
    outs = []
    for i in range(C // U):
        lo = i * U
        b_i = b[lo:lo + U]
        q_i = q[lo:lo + U]
        o_i = o[lo:lo + U]
        if i > 0:
            r = b_i[0:1]
            qe = q_i * jnp.exp(b_i - r)
            ke = jnp.where(rows < lo, k * jnp.exp(jnp.minimum(r - b, 0.0)), 0.0)
            att = _dot_nt(qe, ke)
            o_i = o_i + _dot(att, v)
        for s in range(U):
            bs = b_s[lo + s:lo + s + 1, :]
            ks = k_s[lo + s:lo + s + 1, :]
            vs = v_s[lo + s:lo + s + 1, :]
            m = trow >= s
            dec = jnp.where(m, jnp.exp(jnp.where(m, b_i - bs, 0.0)), 0.0)
            w = _rowsum(q_i * dec * ks)
            o_i = o_i + w * vs
        outs.append(o_i)
    o = jnp.concatenate(outs, axis=0)
    bl = b[C - 1:C]
    kd = k * jnp.exp(bl - b)
    st_new = st * jnp.exp(bl) + _dot_tn(v, kd)
    return o, st_new, b


def _hg_post(o, gp, ng):
    rs = lax.rsqrt(jnp.mean(o * o, axis=1, keepdims=True) + EPS)
    sg = _sigmoid(gp)
    return o * rs * ng * sg, rs, sg


def hgrn_fwd(proj, lbrows, ng, y_width, *, rows):
    S = proj.shape[0]
    C = HG_CHUNK
    cpb = rows // C
    nb = S // rows

    def body(qp_ref, z_ref, v_ref, gp_ref, lb_ref, ng_ref, y_ref, o_ref, st_ref, st, b_s, k_s, v_s):
        @pl.when(pl.program_id(1) == 0)
        def _():
            st[...] = jnp.zeros_like(st)

        a, c, oml = lb_ref[0:1, :], lb_ref[1:2, :], lb_ref[2:3, :]
        ngr = ng_ref[...]

        def chunk(ci, carry):
            off = pl.multiple_of(ci * C, C)
            sl = pl.ds(off, C)
            st_ref[0, ci] = st[...]
            q, _, _, logf, _, k = _hg_gates(qp_ref[sl, :], z_ref[sl, :], a, c, oml)
            v = v_ref[sl, :]
            o, st_new, _ = _hg_chunk_fwd(q, k, v, logf, st[...], b_s, k_s, v_s)
            y, _, _ = _hg_post(o, gp_ref[sl, :], ngr)
            y_ref[sl, :] = y.astype(y_ref.dtype)
            o_ref[sl, :] = o
            st[...] = st_new
            return carry

        lax.fori_loop(0, cpb, chunk, 0)

    col = lambda kblk: pl.BlockSpec((rows, HG_DK), lambda h, r: (r, h + HG_HEADS * kblk))
    return pl.pallas_call(
        body,
        name="hgrn_fwd",
        grid=(HG_HEADS, nb),
        in_specs=[col(0), col(1), col(2), col(3),
                  pl.BlockSpec((8, HG_DK), lambda h, r: (0, h)),
                  pl.BlockSpec((1, HG_DK), lambda h, r: (0, h))],
        out_specs=[pl.BlockSpec((rows, HG_DK), lambda h, r: (r, h)),
                   pl.BlockSpec((rows, HG_DK), lambda h, r: (r, h)),
                   pl.BlockSpec((1, cpb, HG_DK, HG_DK), lambda h, r: (h, r, 0, 0))],
        out_shape=[jax.ShapeDtypeStruct((S, y_width), BF16),
                   jax.ShapeDtypeStruct((S, HG_WIDTH), F32),
                   jax.ShapeDtypeStruct((HG_HEADS, S // C, HG_DK, HG_DK), F32)],
        scratch_shapes=[pltpu.VMEM((HG_DK, HG_DK), F32)] + [pltpu.VMEM((C, HG_DK), F32)] * 3,
        compiler_params=_params("parallel", "arbitrary"),
    )(proj, proj, proj, proj, lbrows, ng)


def hgrn_bwd(proj, o_raw, states, dy, lbrows, ng, *, rows):
    S = proj.shape[0]
    C, U = HG_CHUNK, HG_SUB
    cpb = rows // C
    nb = S // rows

    def body(qp_ref, z_ref, v_ref, gp_ref, o_ref, st_ref, dy_ref, lb_ref, ng_ref,
             dqp_ref, dz_ref, dv_ref, dgp_ref, dlb_ref, dng_ref,
             dst, b_s, k_s, v_s, dbs, dks, dvs):
        @pl.when(pl.program_id(1) == 0)
        def _():
            dst[...] = jnp.zeros_like(dst)
            dlb_ref[...] = jnp.zeros_like(dlb_ref)
            dng_ref[...] = jnp.zeros_like(dng_ref)

        a, c, oml = lb_ref[0:1, :], lb_ref[1:2, :], lb_ref[2:3, :]
        ngr = ng_ref[...]
        rows_i = lax.broadcasted_iota(jnp.int32, (C, HG_DK), 0)
        trow = lax.broadcasted_iota(jnp.int32, (U, HG_DK), 0)

        def chunk(cj, carry):
            ci = cpb - 1 - cj
            off = pl.multiple_of(ci * C, C)
            sl = pl.ds(off, C)
            qp = qp_ref[sl, :]
            z = z_ref[sl, :]
            v = v_ref[sl, :]
            gp = gp_ref[sl, :]
            st = st_ref[0, ci]
            q, sq, t, logf, snz, k = _hg_gates(qp, z, a, c, oml)
            o = o_ref[sl, :]
            b = _dot_f32(_tri(C), logf)
            b_s[...] = b
            k_s[...] = k
            v_s[...] = v
            dyv = dy_ref[sl, :]
            rs = lax.rsqrt(jnp.mean(o * o, axis=1, keepdims=True) + EPS)
            sg = _sigmoid(gp)
            xh = o * rs
            on = xh * ngr
            dgp_ref[sl, :] = (dyv * on * sg * (1.0 - sg)).astype(dgp_ref.dtype)
            don = dyv * sg
            dng_ref[0:1, :] += _colsum(don * xh)
            dxh = don * ngr
            do = rs * (dxh - xh * jnp.mean(dxh * xh, axis=1, keepdims=True))
            eb = jnp.exp(b)
            qb = q * eb
            dstv = dst[...]
            dqb = _dot(do, st)
            dq = dqb * eb
            db = dqb * qb
            dst_acc = _dot_tn(do, qb)
            bl = b[C - 1:C]
            el = jnp.exp(bl)
            ex = jnp.exp(bl - b)
            kd = k * ex
            dbl = _colsum(dstv * st) * el
            dv = _dot_nt(kd, dstv)
            dkd = _dot(v, dstv)
            dk = dkd * ex
            g2 = dkd * kd
            db = db - g2
            dbl = dbl + _colsum(g2)
            dst[...] = dstv * el + dst_acc
            dbs[...] = db
            dks[...] = dk
            dvs[...] = dv
            dbs[C - 1:C, :] += dbl
            dq_parts = []
            for i in range(C // U):
                lo = i * U
                b_i = b[lo:lo + U]
                q_i = q[lo:lo + U]
                do_i = do[lo:lo + U]
                dq_i = dq[lo:lo + U]
                db_i = jnp.zeros((U, HG_DK), F32)
                if i > 0:
                    r = b_i[0:1]
                    e1 = jnp.exp(b_i - r)
                    qe = q_i * e1
                    e2 = jnp.where(rows_i < lo, jnp.exp(jnp.minimum(r - b, 0.0)), 0.0)
                    ke = k * e2
                    att = _dot_nt(qe, ke)
                    datt = _dot_nt(do_i, v)
                    dvs[...] += _dot_tn(att, do_i)
                    dqe = _dot(datt, ke)
                    dke = _dot_tn(datt, qe)
                    dq_i = dq_i + dqe * e1
                    g = dqe * qe
                    db_i = db_i + g
                    gk = dke * ke
                    dks[...] += dke * e2
                    dbs[...] -= gk
                    dr = _colsum(gk) - _colsum(g)
                    dbs[lo:lo + 1, :] += dr
                for s in range(U):
                    row = slice(lo + s, lo + s + 1)
                    bs = b_s[row, :]
                    ks = k_s[row, :]
                    vs = v_s[row, :]
                    m = trow >= s
                    dec = jnp.where(m, jnp.exp(jnp.where(m, b_i - bs, 0.0)), 0.0)
                    qd = q_i * dec
                    y_ = qd * ks
                    w = _rowsum(y_)
                    dw = _rowsum(do_i * vs)
                    dvs[row, :] += _colsum(w * do_i)
                    dq_i = dq_i + dw * dec * ks
                    dks[row, :] += _colsum(dw * qd)
                    g = dw * y_
                    db_i = db_i + g
                    dbs[row, :] -= _colsum(g)
                dbs[lo:lo + U, :] += db_i
                dq_parts.append(dq_i)
            dq = jnp.concatenate(dq_parts, axis=0)
            db = dbs[...]
            dk = dks[...]
            dlogf = _dot_f32(_tri(C), db, (((0,), (0,)), ((), ())))
            pa = jnp.exp(a - logf)
            pt = jnp.exp(t - logf)
            dt = dlogf * pt
            dlb_ref[0:1, :] += _colsum(dlogf * pa)
            dlb_ref[1:2, :] += _colsum(dt)
            dlb_ref[2:3, :] += _colsum(dk * snz)
            dz = dt * snz - dk * oml * snz * (1.0 - snz)
            dqp = dq * (sq + qp * sq * (1.0 - sq))
            dqp_ref[sl, :] = dqp.astype(dqp_ref.dtype)
            dz_ref[sl, :] = dz.astype(dz_ref.dtype)
            dv_ref[sl, :] = dvs[...].astype(dv_ref.dtype)
            return carry

        lax.fori_loop(0, cpb, chunk, 0)

    rev = lambda r: nb - 1 - r
    col = lambda kblk: pl.BlockSpec((rows, HG_DK), lambda h, r: (rev(r), h + HG_HEADS * kblk))
    acc = pl.BlockSpec((8, HG_DK), lambda h, r: (0, h))
    return pl.pallas_call(
        body,
        name="hgrn_bwd",
        grid=(HG_HEADS, nb),
        in_specs=[col(0), col(1), col(2), col(3), col(0),
                  pl.BlockSpec((1, cpb, HG_DK, HG_DK), lambda h, r: (h, rev(r), 0, 0)),
                  pl.BlockSpec((rows, HG_DK), lambda h, r: (rev(r), h)),
                  pl.BlockSpec((8, HG_DK), lambda h, r: (0, h)),
                  pl.BlockSpec((1, HG_DK), lambda h, r: (0, h))],
        out_specs=[col(0)] * 4 + [acc, acc],
        out_shape=[jax.ShapeDtypeStruct((S, HG_WIDTH), BF16)] * 4
                  + [jax.ShapeDtypeStruct((8, HG_WIDTH), F32)] * 2,
        scratch_shapes=[pltpu.VMEM((HG_DK, HG_DK), F32)] + [pltpu.VMEM((C, HG_DK), F32)] * 6,
        compiler_params=_params("parallel", "arbitrary"),
    )(proj, proj, proj, proj, o_raw, states, dy, lbrows, ng)


def _swa_valid(n):
    W = WINDOW
    qpos = lax.broadcasted_iota(jnp.int32, (W, 2 * W), 0) + W
    kpos = lax.broadcasted_iota(jnp.int32, (W, 2 * W), 1)
    rel = qpos - kpos
    return (rel >= 0) & (rel < W) & jnp.logical_not((n == 0) & (kpos < W))


def _swa_probs(qh, kh, sink, valid):
    s = _dot_nt(qh, kh) * (AT_HEAD_DIM ** -0.5)
    s = jnp.where(valid, s, MASK_VALUE)
    m = jnp.maximum(jnp.max(s, axis=1, keepdims=True), sink)
    e = jnp.exp(s - m)
    es = jnp.exp(sink - m)
    inv = 1.0 / (_rowsum(e) + es)
    return e * inv, es * inv


def swa_fwd(proj, sinks_b, y):
    S = proj.shape[0]
    W, Dh = WINDOW, AT_HEAD_DIM
    nb = S // W

    def body(q_ref, kp_ref, k_ref, vp_ref, v_ref, sk_ref, y_in, y_ref):
        del y_in
        n = pl.program_id(0)
        valid = _swa_valid(n)
        kk = jnp.concatenate([kp_ref[...], k_ref[...]], axis=0)
        vv = jnp.concatenate([vp_ref[...], v_ref[...]], axis=0)
        q = q_ref[...]
        for h in range(AT_Q_HEADS):
            g = h // AT_GROUP
            p, _ = _swa_probs(q[:, h * Dh:(h + 1) * Dh], kk[:, g * Dh:(g + 1) * Dh],
                              sk_ref[h:h + 1, 0:1], valid)
            y_ref[:, h * Dh:(h + 1) * Dh] = _dot(p, vv[:, g * Dh:(g + 1) * Dh]).astype(y_ref.dtype)

    prev = lambda n: jnp.maximum(n - 1, 0)
    return pl.pallas_call(
        body,
        name="swa_fwd",
        grid=(nb,),
        in_specs=[pl.BlockSpec((W, AT_WIDTH), lambda n: (n, 4)),
                  pl.BlockSpec((W, 128), lambda n: (prev(n), 20)),
                  pl.BlockSpec((W, 128), lambda n: (n, 20)),
                  pl.BlockSpec((W, 128), lambda n: (prev(n), 21)),
                  pl.BlockSpec((W, 128), lambda n: (n, 21)),
                  pl.BlockSpec((8, 128), lambda n: (0, 0)),
                  pl.BlockSpec(memory_space=pl.ANY)],
        out_specs=pl.BlockSpec((W, AT_WIDTH), lambda n: (n, 1)),
        out_shape=jax.ShapeDtypeStruct(y.shape, y.dtype),
        input_output_aliases={6: 0},
        compiler_params=_params("parallel"),
    )(proj, proj, proj, proj, proj, sinks_b, y)


def swa_bwd(proj, sinks_b, dy):
    S = proj.shape[0]
    W, Dh = WINDOW, AT_HEAD_DIM
    nb = S // W
    scale = Dh ** -0.5

    def body(q_ref, kp_ref, k_ref, vp_ref, v_ref, sk_ref, dy_ref,
             dq_ref, dko_ref, dkp_ref, dvo_ref, dvp_ref, dsk_ref):
        n = pl.program_id(0)

        @pl.when(n == 0)
        def _():
            dsk_ref[...] = jnp.zeros_like(dsk_ref)

        valid = _swa_valid(n)
        kk = jnp.concatenate([kp_ref[...], k_ref[...]], axis=0)
        vv = jnp.concatenate([vp_ref[...], v_ref[...]], axis=0)
        q = q_ref[...]
        dyv = dy_ref[...]
        for g in range(AT_KV_HEADS):
            kh = kk[:, g * Dh:(g + 1) * Dh]
            vh = vv[:, g * Dh:(g + 1) * Dh]
            dkh = jnp.zeros((2 * W, Dh), F32)
            dvh = jnp.zeros((2 * W, Dh), F32)
            for j in range(AT_GROUP):
                h = g * AT_GROUP + j
                qh = q[:, h * Dh:(h + 1) * Dh]
                doh = dyv[:, h * Dh:(h + 1) * Dh]
                p, ps = _swa_probs(qh, kh, sk_ref[h:h + 1, 0:1], valid)
                dp = _dot_nt(doh, vh)
                delta = _rowsum(dp * p)
                ds = p * (dp - delta) * scale
                dq_ref[:, h * Dh:(h + 1) * Dh] = _dot(ds, kh).astype(dq_ref.dtype)
                dkh = dkh + _dot_tn(ds, qh)
                dvh = dvh + _dot_tn(p, doh)
                dsk_ref[h:h + 1, 0:1] += -_colsum(ps * delta)
            dkp_ref[:, g * Dh:(g + 1) * Dh] = dkh[:W]
            dko_ref[:, g * Dh:(g + 1) * Dh] = dkh[W:]
            dvp_ref[:, g * Dh:(g + 1) * Dh] = dvh[:W]
            dvo_ref[:, g * Dh:(g + 1) * Dh] = dvh[W:]

    prev = lambda n: jnp.maximum(n - 1, 0)
    kv = pl.BlockSpec((W, 128), lambda n: (n, 0))
    return pl.pallas_call(
        body,
        name="swa_bwd",
        grid=(nb,),
        in_specs=[pl.BlockSpec((W, AT_WIDTH), lambda n: (n, 4)),
                  pl.BlockSpec((W, 128), lambda n: (prev(n), 20)),
                  pl.BlockSpec((W, 128), lambda n: (n, 20)),
                  pl.BlockSpec((W, 128), lambda n: (prev(n), 21)),
                  pl.BlockSpec((W, 128), lambda n: (n, 21)),
                  pl.BlockSpec((8, 128), lambda n: (0, 0)),
                  pl.BlockSpec((W, AT_WIDTH), lambda n: (n, 1))],
        out_specs=[pl.BlockSpec((W, AT_WIDTH), lambda n: (n, 0)), kv, kv, kv, kv,
                   pl.BlockSpec((8, 128), lambda n: (0, 0))],
        out_shape=[jax.ShapeDtypeStruct((S, AT_WIDTH), BF16)]
                  + [jax.ShapeDtypeStruct((S, 128), F32)] * 4
                  + [jax.ShapeDtypeStruct((8, 128), F32)],
        compiler_params=_params("arbitrary"),
    )(proj, proj, proj, proj, proj, sinks_b, dy)


def assemble_dproj(hg_grads, dq_at, dko, dkp, dvo, dvp, *, rows):
    S = dq_at.shape[0]
    W = WINDOW
    nb = S // W
    bpr = rows // W

    def body(a0, a1, a2, a3, dq, ko, kp, kpn, vo, vp, vpn, out):
        r = pl.program_id(0)
        for i, a in enumerate((a0, a1, a2, a3)):
            out[:, i * HG_WIDTH:(i + 1) * HG_WIDTH] = a[...]
        base = 4 * HG_WIDTH
        out[:, base:base + AT_WIDTH] = dq[...]
        last = (r == pl.num_programs(0) - 1)
        for off, own, pv, pvn in ((base + AT_WIDTH, ko, kp, kpn), (base + AT_WIDTH + 128, vo, vp, vpn)):
            if bpr > 1:
                out[0:rows - W, off:off + 128] = (own[0:rows - W, :] + pv[W:rows, :]).astype(out.dtype)
            nxt = jnp.where(last, 0.0, pvn[...])
            out[rows - W:rows, off:off + 128] = (own[rows - W:rows, :] + nxt).astype(out.dtype)

    hg = pl.BlockSpec((rows, HG_WIDTH), lambda r: (r, 0))
    blk = pl.BlockSpec((rows, 128), lambda r: (r, 0))
    nxt = pl.BlockSpec((W, 128), lambda r: (jnp.minimum((r + 1) * bpr, nb - 1), 0))
    return pl.pallas_call(
        body,
        name="assemble_dproj",
        grid=(S // rows,),
        in_specs=[hg, hg, hg, hg, pl.BlockSpec((rows, AT_WIDTH), lambda r: (r, 0)),
                  blk, blk, nxt, blk, blk, nxt],
        out_specs=pl.BlockSpec((rows, IN_WIDTH), lambda r: (r, 0)),
        out_shape=jax.ShapeDtypeStruct((S, IN_WIDTH), BF16),
        compiler_params=_params("parallel"),
    )(*hg_grads, dq_at, dko, dkp, dkp, dvo, dvp, dvp)


ROW_TILE = 512
COL_TILE = 1408


def _col_tile(n):
    return n if n <= COL_TILE else COL_TILE


def mm(a, b, *, nt=False, out_dtype=F32, res=None, name):
    M, K = a.shape
    N = b.shape[0] if nt else b.shape[1]
    tm = min(ROW_TILE, M)
    tn = _col_tile(N)
    assert M % tm == 0 and N % tn == 0

    def body(a_ref, b_ref, *rest):
        o_ref = rest[-1]
        acc = _dot_nt(a_ref[...], b_ref[...]) if nt else _dot(a_ref[...], b_ref[...])
        if res is not None:
            acc = acc + rest[0][...]
        o_ref[...] = acc.astype(o_ref.dtype)

    in_specs = [pl.BlockSpec((tm, K), lambda j, i: (i, 0)),
                pl.BlockSpec((tn, K), lambda j, i: (j, 0)) if nt else pl.BlockSpec((K, tn), lambda j, i: (0, j))]
    args = [a, b]
    if res is not None:
        in_specs.append(pl.BlockSpec((tm, tn), lambda j, i: (i, j)))
        args.append(res)
    return pl.pallas_call(
        body,
        name=name,
        grid=(N // tn, M // tm),
        in_specs=in_specs,
        out_specs=pl.BlockSpec((tm, tn), lambda j, i: (i, j)),
        out_shape=jax.ShapeDtypeStruct((M, N), out_dtype),
        compiler_params=_params("parallel", "parallel"),
    )(*args)


def mm_tn(a, b, *, name):
    M, K = a.shape
    N = b.shape[1]
    tm = min(ROW_TILE, M)
    tk = _col_tile(K)
    tn = _col_tile(N)
    assert M % tm == 0 and K % tk == 0 and N % tn == 0

    def body(a_ref, b_ref, o_ref):
        @pl.when(pl.program_id(2) == 0)
        def _():
            o_ref[...] = jnp.zeros_like(o_ref)

        o_ref[...] += _dot_tn(a_ref[...], b_ref[...])

    return pl.pallas_call(
        body,
        name=name,
        grid=(K // tk, N // tn, M // tm),
        in_specs=[pl.BlockSpec((tm, tk), lambda k, j, i: (i, k)),
                  pl.BlockSpec((tm, tn), lambda k, j, i: (i, j))],
        out_specs=pl.BlockSpec((tk, tn), lambda k, j, i: (k, j)),
        out_shape=jax.ShapeDtypeStruct((K, N), F32),
        compiler_params=_params("parallel", "parallel", "arbitrary"),
    )(a, b)


def _row_spec(tm, width):
    return pl.BlockSpec((tm, width), lambda i: (i, 0))


def _const_spec(shape):
    return pl.BlockSpec(shape, lambda *_: (0,) * len(shape))


def rmsnorm_fwd(h, g, *, name):
    S, D = h.shape
    tm = min(ROW_TILE, S)

    def body(h_ref, g_ref, u_ref):
        x = h_ref[...]
        rs = lax.rsqrt(jnp.mean(x * x, axis=1, keepdims=True) + EPS)
        u_ref[...] = (x * rs * g_ref[...]).astype(u_ref.dtype)

    return pl.pallas_call(
        body, name=name, grid=(S // tm,),
        in_specs=[_row_spec(tm, D), _const_spec((1, D))],
        out_specs=_row_spec(tm, D),
        out_shape=jax.ShapeDtypeStruct((S, D), BF16),
        compiler_params=_params("parallel"),
    )(h, g)


def rmsnorm_bwd(du, h, g, dres, *, name):
    S, D = h.shape
    tm = min(ROW_TILE, S)

    def body(du_ref, h_ref, g_ref, dr_ref, dh_ref, dhb_ref, dg_ref):
        @pl.when(pl.program_id(0) == 0)
        def _():
            dg_ref[...] = jnp.zeros_like(dg_ref)

        x = h_ref[...]
        d = du_ref[...]
        rs = lax.rsqrt(jnp.mean(x * x, axis=1, keepdims=True) + EPS)
        xh = x * rs
        dg_ref[0:1, :] += _colsum(d * xh)
        dxh = d * g_ref[...]
        dh = dr_ref[...] + rs * (dxh - xh * jnp.mean(dxh * xh, axis=1, keepdims=True))
        dh_ref[...] = dh
        dhb_ref[...] = dh.astype(BF16)

    return pl.pallas_call(
        body, name=name, grid=(S // tm,),
        in_specs=[_row_spec(tm, D), _row_spec(tm, D), _const_spec((1, D)), _row_spec(tm, D)],
        out_specs=[_row_spec(tm, D), _row_spec(tm, D), _const_spec((8, D))],
        out_shape=[jax.ShapeDtypeStruct((S, D), F32), jax.ShapeDtypeStruct((S, D), BF16),
                   jax.ShapeDtypeStruct((8, D), F32)],
        compiler_params=_params("arbitrary"),
    )(du, h, g, dres)


def _shift_down(x, edge8, s):
    sh = pltpu.roll(x, s, 0)
    er = pltpu.roll(edge8, s, 0)
    row8 = lax.broadcasted_iota(jnp.int32, er.shape, 0)
    top = jnp.where(row8 < s, er, sh[0:8])
    return jnp.concatenate([top, sh[8:]], axis=0)


def _shift_up(x, s):
    return pltpu.roll(x, x.shape[0] - s, 0)


def _conv_pre(a, prev8, w_ref, cb_ref):
    a1 = _shift_down(a, prev8, 1)
    a2 = _shift_down(a, prev8, 2)
    ac = w_ref[2:3, :] * a + w_ref[1:2, :] * a1 + w_ref[0:1, :] * a2 + cb_ref[...]
    return ac, a1, a2


def convffn_fwd(hh, cw8, cb):
    S = hh.shape[0]
    tm = min(ROW_TILE, S)
    tn = _col_tile(D_FF)
    nj = D_FF // tn

    def body(a_ref, ap_ref, b_ref, w_ref, cb_ref, o_ref):
        prev8 = jnp.where(pl.program_id(1) == 0, 0.0, ap_ref[...])
        ac, _, _ = _conv_pre(a_ref[...], prev8, w_ref, cb_ref)
        o_ref[...] = (ac * _sigmoid(ac) * b_ref[...]).astype(o_ref.dtype)

    r8 = tm // 8
    return pl.pallas_call(
        body, name="convffn_fwd", grid=(nj, S // tm),
        in_specs=[pl.BlockSpec((tm, tn), lambda j, i: (i, j)),
                  pl.BlockSpec((8, tn), lambda j, i: (jnp.maximum(i * r8 - 1, 0), j)),
                  pl.BlockSpec((tm, tn), lambda j, i: (i, j + nj)),
                  pl.BlockSpec((8, tn), lambda j, i: (0, j)),
                  pl.BlockSpec((1, tn), lambda j, i: (0, j))],
        out_specs=pl.BlockSpec((tm, tn), lambda j, i: (i, j)),
        out_shape=jax.ShapeDtypeStruct((S, D_FF), BF16),
        compiler_params=_params("parallel", "parallel"),
    )(hh, hh, hh, cw8, cb)


def convffn_bwd(hh, dact, cw8, cb):
    S = hh.shape[0]
    tm = min(ROW_TILE, S)
    tn = _col_tile(D_FF)
    nj = D_FF // tn
    ni = S // tm

    def body(a_ref, ap_ref, an_ref, b_ref, bn_ref, d_ref, dn_ref, w_ref, cb_ref, o_a, o_b, dw_ref):
        i = pl.program_id(1)

        @pl.when(i == 0)
        def _():
            dw_ref[...] = jnp.zeros_like(dw_ref)

        prev8 = jnp.where(i == 0, 0.0, ap_ref[...])
        a = jnp.concatenate([a_ref[...], an_ref[...]], axis=0)
        b = jnp.concatenate([b_ref[...], bn_ref[...]], axis=0)
        d = jnp.concatenate([d_ref[...], jnp.where(i == ni - 1, 0.0, dn_ref[...])], axis=0)
        ac, a1, a2 = _conv_pre(a, prev8, w_ref, cb_ref)
        sa = _sigmoid(ac)
        o_b[...] = (d[0:tm] * ac[0:tm] * sa[0:tm]).astype(o_b.dtype)
        dac = d * b * (sa + ac * sa * (1.0 - sa))
        da = w_ref[2:3, :] * dac + w_ref[1:2, :] * _shift_up(dac, 1) + w_ref[0:1, :] * _shift_up(dac, 2)
        o_a[...] = da[0:tm].astype(o_a.dtype)
        dc = dac[0:tm]
        dw_ref[0:1, :] += _colsum(dc * a2[0:tm])
        dw_ref[1:2, :] += _colsum(dc * a1[0:tm])
        dw_ref[2:3, :] += _colsum(dc * a[0:tm])
        dw_ref[3:4, :] += _colsum(dc)

    r8 = tm // 8
    last8 = S // 8 - 1
    cur = lambda off: pl.BlockSpec((tm, tn), lambda j, i: (i, j + off))
    nxt = lambda off: pl.BlockSpec((8, tn), lambda j, i: (jnp.minimum((i + 1) * r8, last8), j + off))
    return pl.pallas_call(
        body, name="convffn_bwd", grid=(nj, ni),
        in_specs=[cur(0),
                  pl.BlockSpec((8, tn), lambda j, i: (jnp.maximum(i * r8 - 1, 0), j)),
                  nxt(0), cur(nj), nxt(nj), cur(0), nxt(0),
                  pl.BlockSpec((8, tn), lambda j, i: (0, j)),
                  pl.BlockSpec((1, tn), lambda j, i: (0, j))],
        out_specs=[cur(0), cur(0), pl.BlockSpec((8, tn), lambda j, i: (0, j))],
        out_shape=[jax.ShapeDtypeStruct((S, D_FF), BF16), jax.ShapeDtypeStruct((S, D_FF), BF16),
                   jax.ShapeDtypeStruct((8, D_FF), F32)],
        compiler_params=_params("parallel", "arbitrary"),
    )(hh, hh, hh, hh, hh, dact, dact, cw8, cb)


def ple_fwd(h, gpre, p, wpu):
    S, D = h.shape
    tm = min(ROW_TILE, S)

    def body(h_ref, g_ref, p_ref, w_ref, o_ref):
        o_ref[...] = h_ref[...] + _sigmoid(g_ref[...]) * _dot(p_ref[...], w_ref[...])

    return pl.pallas_call(
        body, name="ple_fwd", grid=(S // tm,),
        in_specs=[_row_spec(tm, D), _row_spec(tm, D), _row_spec(tm, PLE_DIM), _const_spec((PLE_DIM, D))],
        out_specs=_row_spec(tm, D),
        out_shape=jax.ShapeDtypeStruct((S, D), F32),
        compiler_params=_params("parallel"),
    )(h, gpre, p, wpu)


def ple_bwd(dh, gpre, p, wpu):
    S, D = dh.shape
    tm = min(ROW_TILE, S)

    def body(d_ref, g_ref, p_ref, w_ref, dpu_ref, dg_ref):
        d = d_ref[...]
        gate = _sigmoid(g_ref[...])
        pu = _dot(p_ref[...], w_ref[...])
        dpu_ref[...] = (d * gate).astype(dpu_ref.dtype)
        dg_ref[...] = (d * pu * gate * (1.0 - gate)).astype(dg_ref.dtype)

    return pl.pallas_call(
        body, name="ple_bwd", grid=(S // tm,),
        in_specs=[_row_spec(tm, D), _row_spec(tm, D), _row_spec(tm, PLE_DIM), _const_spec((PLE_DIM, D))],
        out_specs=[_row_spec(tm, D), _row_spec(tm, D)],
        out_shape=[jax.ShapeDtypeStruct((S, D), BF16)] * 2,
        compiler_params=_params("parallel"),
    )(dh, gpre, p, wpu)


def loss_head(h, g, tgt):
    S, D = h.shape
    tm = min(ROW_TILE, S)

    def body(h_ref, g_ref, t_ref, dh_ref, dhb_ref, l_ref, dg_ref):
        @pl.when(pl.program_id(0) == 0)
        def _():
            l_ref[...] = jnp.zeros_like(l_ref)
            dg_ref[...] = jnp.zeros_like(dg_ref)

        x = h_ref[...]
        gr = g_ref[...]
        rs = lax.rsqrt(jnp.mean(x * x, axis=1, keepdims=True) + EPS)
        xh = x * rs
        err = xh * gr - t_ref[...]
        l_ref[0:1, 0:1] += 0.5 * _colsum(jnp.mean(err * err, axis=1, keepdims=True))
        dy = err * (1.0 / D)
        dg_ref[0:1, :] += _colsum(dy * xh)
        dxh = dy * gr
        dh = rs * (dxh - xh * jnp.mean(dxh * xh, axis=1, keepdims=True))
        dh_ref[...] = dh
        dhb_ref[...] = dh.astype(BF16)

    return pl.pallas_call(
        body, name="loss_head", grid=(S // tm,),
        in_specs=[_row_spec(tm, D), _const_spec((1, D)), _row_spec(tm, D)],
        out_specs=[_row_spec(tm, D), _row_spec(tm, D), _const_spec((8, 128)), _const_spec((8, D))],
        out_shape=[jax.ShapeDtypeStruct((S, D), F32), jax.ShapeDtypeStruct((S, D), BF16),
                   jax.ShapeDtypeStruct((8, 128), F32), jax.ShapeDtypeStruct((8, D), F32)],
        compiler_params=_params("arbitrary"),
    )(h, g, tgt)


def _lb_rows(l_ref):
    l = l_ref[...]
    e = jnp.exp(l - jnp.max(l, axis=0, keepdims=True))
    p = e / _colsum(e)
    lbs, run = [], None
    for i in range(DEPTH):
        run = p[i:i + 1] if i == 0 else run + p[i:i + 1]
        lbs.append(run - p[0:1])
    return p, lbs


def lb_fwd(lb_logits):
    def body(l_ref, o_ref):
        _, lbs = _lb_rows(l_ref)
        o_ref[...] = jnp.zeros_like(o_ref)
        for i, lb in enumerate(lbs):
            o_ref[8 * i:8 * i + 1, :] = jnp.log(jnp.maximum(lb, LB_FLOOR))
            o_ref[8 * i + 1:8 * i + 2, :] = jnp.log1p(-lb)
            o_ref[8 * i + 2:8 * i + 3, :] = 1.0 - lb
            o_ref[8 * i + 3:8 * i + 4, :] = lb

    return pl.pallas_call(
        body, name="lb_fwd",
        out_shape=jax.ShapeDtypeStruct((DEPTH * 8, HG_WIDTH), F32),
    )(lb_logits)


def lb_bwd(dlbrows, lb_logits):
    def body(d_ref, l_ref, o_ref):
        p, lbs = _lb_rows(l_ref)
        dlb = []
        for i, lb in enumerate(lbs):
            da = d_ref[8 * i:8 * i + 1, :]
            dc = d_ref[8 * i + 1:8 * i + 2, :]
            do = d_ref[8 * i + 2:8 * i + 3, :]
            dlb.append(jnp.where(lb > LB_FLOOR, da / jnp.maximum(lb, LB_FLOOR), 0.0) - dc / (1.0 - lb) - do)
        dp = [jnp.zeros_like(dlb[0])]
        for j in range(1, DEPTH):
            acc = dlb[j]
            for i in range(j + 1, DEPTH):
                acc = acc + dlb[i]
            dp.append(acc)
        dot_ = p[0:1] * dp[0]
        for j in range(1, DEPTH):
            dot_ = dot_ + p[j:j + 1] * dp[j]
        o_ref[...] = jnp.zeros_like(o_ref)
        for j in range(DEPTH):
            o_ref[j:j + 1, :] = p[j:j + 1] * (dp[j] - dot_)

    return pl.pallas_call(
        body, name="lb_bwd",
        out_shape=jax.ShapeDtypeStruct((8, HG_WIDTH), F32),
    )(dlbrows, lb_logits)


def adamw(w, g, m, v, *, name):
    R, C = w.shape
    tr = next((t for t in (512, 256, 128, 64, 32, 16, 8) if R % t == 0), R)

    def body(w_ref, g_ref, m_ref, v_ref, d_ref, m2_ref, v2_ref):
        gv = g_ref[...]
        m2 = ADAM_B1 * m_ref[...] + (1.0 - ADAM_B1) * gv
        v2 = ADAM_B2 * v_ref[...] + (1.0 - ADAM_B2) * (gv * gv)
        mh = m2 / (1.0 - ADAM_B1 ** ADAM_STEP)
        vh = v2 / (1.0 - ADAM_B2 ** ADAM_STEP)
        d_ref[...] = -ADAM_LR * (mh / (jnp.sqrt(vh) + ADAM_EPS) + ADAM_WD * w_ref[...])
        m2_ref[...] = m2
        v2_ref[...] = v2

    spec = pl.BlockSpec((tr, C), lambda i: (i, 0))
    return pl.pallas_call(
        body, name=name, grid=(R // tr,),
        in_specs=[spec] * 4, out_specs=[spec] * 3,
        out_shape=[jax.ShapeDtypeStruct((R, C), F32)] * 3,
        compiler_params=_params("parallel"),
    )(w, g, m, v)


def sum_slots(x, *, out_dtype, name):
    n, R, C = x.shape
    tr = 848 if R % 848 == 0 else R

    def body(x_ref, o_ref):
        acc = x_ref[0].astype(F32)
        for k in range(1, n):
            acc = acc + x_ref[k].astype(F32)
        o_ref[...] = acc.astype(o_ref.dtype)

    return pl.pallas_call(
        body, name=name, grid=(R // tr,),
        in_specs=[pl.BlockSpec((n, tr, C), lambda i: (0, i, 0))],
        out_specs=pl.BlockSpec((tr, C), lambda i: (i, 0)),
        out_shape=jax.ShapeDtypeStruct((R, C), out_dtype),
        compiler_params=_params("parallel"),
    )(x)


MESH = pl.DeviceIdType.MESH
ANY = pl.BlockSpec(memory_space=pl.ANY)


def _place():
    return lax.axis_index("x"), lax.axis_index("y"), lax.axis_index("c")


def _other_chips(x, y):
    return [(1 - x, y), (x, 1 - y), (1 - x, 1 - y)]


def small_allgather(buf):
    R, C = buf.shape

    def body(x_ref, out_ref, send_sems, recv_sems, local_sem):
        x, y, c = _place()
        me, sibling = (x, y, c), (x, y, 1 - c)
        chips = _other_chips(x, y)

        def slot(px, py, pc):
            return out_ref.at[4 * px + 2 * py + pc]

        def copy(k, block, to, src=None):
            return pltpu.make_async_remote_copy(
                src_ref=slot(*block) if src is None else src, dst_ref=slot(*block),
                send_sem=send_sems.at[k], recv_sem=recv_sems.at[k],
                device_id=to, device_id_type=MESH)

        mine = pltpu.make_async_copy(x_ref, slot(*me), local_sem)
        mine.start()
        first = [copy(0, me, sibling, src=x_ref)]
        first += [copy(1 + r, me, (*chip, c), src=x_ref) for r, chip in enumerate(chips)]
        for cp in first:
            cp.start()
        passed = [copy(4 + r, (*chip, c), sibling) for r, chip in enumerate(chips)]
        for r, chip in enumerate(chips):
            copy(1 + r, (*chip, c), me).wait_recv()
            passed[r].start()
        copy(0, sibling, me).wait_recv()
        for r, chip in enumerate(chips):
            copy(4 + r, (*chip, 1 - c), me).wait_recv()
        for cp in first + passed:
            cp.wait_send()
        mine.wait()

    return pl.pallas_call(
        body, name="small_allgather",
        out_shape=jax.ShapeDtypeStruct((8, R, C), buf.dtype),
        in_specs=[pl.BlockSpec(memory_space=pltpu.VMEM)],
        out_specs=pl.BlockSpec(memory_space=pltpu.VMEM),
        scratch_shapes=[pltpu.SemaphoreType.DMA((7,)), pltpu.SemaphoreType.DMA((7,)),
                        pltpu.SemaphoreType.DMA],
    )(buf)


def weights_allgather(wp):
    R, C = wp.shape
    half = R // 2

    def body(w_ref, g_ref, send_sems, recv_sems, local_sem):
        x, y, c = _place()
        j = 2 * x + y
        sibling = (x, y, 1 - c)
        chips = _other_chips(x, y)
        mine = pl.ds(pl.multiple_of(c * half, 16), half)
        theirs = pl.ds(pl.multiple_of((1 - c) * half, 16), half)

        def copy(k, chip_block, rows, to, src=None):
            dst = g_ref.at[chip_block, rows]
            return pltpu.make_async_remote_copy(
                src_ref=dst if src is None else src, dst_ref=dst,
                send_sem=send_sems.at[k], recv_sem=recv_sems.at[k],
                device_id=to, device_id_type=MESH)

        own = pltpu.make_async_copy(w_ref, g_ref.at[j], local_sem)
        own.start()
        first = [copy(r, j, mine, (*chip, c), src=w_ref.at[mine]) for r, chip in enumerate(chips)]
        for cp in first:
            cp.start()
        passed = [copy(3 + r, 2 * chip[0] + chip[1], mine, sibling) for r, chip in enumerate(chips)]
        for r, chip in enumerate(chips):
            copy(r, 2 * chip[0] + chip[1], mine, (*chip, c)).wait_recv()
            passed[r].start()
        for r, chip in enumerate(chips):
            copy(3 + r, 2 * chip[0] + chip[1], theirs, sibling).wait_recv()
        for cp in first + passed:
            cp.wait_send()
        own.wait()

    return pl.pallas_call(
        body, name="weights_allgather",
        out_shape=jax.ShapeDtypeStruct((4, R, C), wp.dtype),
        in_specs=[ANY], out_specs=ANY,
        scratch_shapes=[pltpu.SemaphoreType.DMA((6,)), pltpu.SemaphoreType.DMA((6,)),
                        pltpu.SemaphoreType.DMA],
    )(wp)


def sibling_swap(v, *, name):
    def body(v_ref, got_ref, send_sem, recv_sem):
        x, y, c = _place()
        cp = pltpu.make_async_remote_copy(
            src_ref=v_ref, dst_ref=got_ref, send_sem=send_sem, recv_sem=recv_sem,
            device_id=(x, y, 1 - c), device_id_type=MESH)
        cp.start()
        cp.wait()

    return pl.pallas_call(
        body, name=name,
        out_shape=jax.ShapeDtypeStruct(v.shape, v.dtype),
        in_specs=[ANY], out_specs=ANY,
        scratch_shapes=[pltpu.SemaphoreType.DMA, pltpu.SemaphoreType.DMA],
    )(v)


def chip_exchange(q):
    def body(q_ref, r_ref, send_sems, recv_sems, local_sem):
        x, y, c = _place()
        j = 2 * x + y
        chips = _other_chips(x, y)
        own = pltpu.make_async_copy(q_ref.at[j], r_ref.at[j], local_sem)
        own.start()

        def copy(r, src_block, dst_block, chip):
            return pltpu.make_async_remote_copy(
                src_ref=q_ref.at[src_block], dst_ref=r_ref.at[dst_block],
                send_sem=send_sems.at[r], recv_sem=recv_sems.at[r],
                device_id=(*chip, c), device_id_type=MESH)

        sends = [copy(r, 2 * chip[0] + chip[1], j, chip) for r, chip in enumerate(chips)]
        for cp in sends:
            cp.start()
        for r, chip in enumerate(chips):
            jr = 2 * chip[0] + chip[1]
            copy(r, jr, jr, chip).wait_recv()
        for cp in sends:
            cp.wait_send()
        own.wait()

    return pl.pallas_call(
        body, name="chip_exchange",
        out_shape=jax.ShapeDtypeStruct(q.shape, q.dtype),
        in_specs=[ANY], out_specs=ANY,
        scratch_shapes=[pltpu.SemaphoreType.DMA((3,)), pltpu.SemaphoreType.DMA((3,)),
                        pltpu.SemaphoreType.DMA],
    )(q)


N_CHIPS = 4
_PACK = (("w_in", 704), ("w_out", 256), ("w_up", 1408), ("w_down", 704), ("w_ple_gate", 256), ("w_ple_up", 64))
LAYER_ROWS = sum(r for _, r in _PACK)
PACK_ROWS = DEPTH * LAYER_ROWS
HALF_ROWS = PACK_ROWS // 2


def _pack_shards(sh):
    parts = []
    for i in range(DEPTH):
        for name, rows in _PACK:
            parts.append(sh[name][i].reshape(rows, D_MODEL))
    return jnp.concatenate(parts, axis=0)


def _unpack_shards(slab):
    shapes = {"w_in": (D_MODEL, IN_WIDTH // N_CHIPS), "w_out": (D_MODEL // N_CHIPS, D_MODEL),
              "w_up": (D_MODEL, 2 * D_FF // N_CHIPS), "w_down": (D_FF // N_CHIPS, D_MODEL),
              "w_ple_gate": (D_MODEL // N_CHIPS, D_MODEL), "w_ple_up": (PLE_DIM, D_MODEL // N_CHIPS)}
    out = {name: [] for name, _ in _PACK}
    off = 0
    for i in range(DEPTH):
        for name, rows in _PACK:
            out[name].append(slab[off:off + rows].reshape(shapes[name]))
            off += rows
    return {k: jnp.stack(v) for k, v in out.items()}


_COL_SHARDED = ("w_in", "w_up", "w_ple_up")


def _full_from_chips(g, layer):
    per_chip = [_unpack_shards_layer(g[k], layer) for k in range(N_CHIPS)]
    return {name: jnp.concatenate([pc[name] for pc in per_chip], axis=1 if name in _COL_SHARDED else 0)
            for name, _ in _PACK}


def _unpack_shards_layer(slab, layer):
    shapes = {"w_in": (D_MODEL, IN_WIDTH // N_CHIPS), "w_out": (D_MODEL // N_CHIPS, D_MODEL),
              "w_up": (D_MODEL, 2 * D_FF // N_CHIPS), "w_down": (D_FF // N_CHIPS, D_MODEL),
              "w_ple_gate": (D_MODEL // N_CHIPS, D_MODEL), "w_ple_up": (PLE_DIM, D_MODEL // N_CHIPS)}
    out = {}
    off = layer * LAYER_ROWS
    for name, rows in _PACK:
        out[name] = slab[off:off + rows].reshape(shapes[name])
        off += rows
    return out


def _split_to_chips(full, name):
    r, c = full.shape
    if name in _COL_SHARDED:
        full = full.reshape(r, N_CHIPS, c // N_CHIPS).transpose(1, 0, 2)
    return full.reshape(N_CHIPS, -1, D_MODEL)


_SMALL = (("loss", 128), ("g_final", 1024), ("g_mix", 4096), ("lb_logits", 2048), ("hg_norm_g", 2048),
          ("attn_sinks", 128), ("g_ffn", 4096), ("conv_w", 4 * 3 * D_FF), ("conv_b", 4 * D_FF), ("g_ple", 4096))
SMALL_ROWS = 496


def _pack_small(d):
    parts = []
    for name, n in _SMALL:
        v = d[name].reshape(-1).astype(F32)
        parts.append(jnp.pad(v, (0, n - v.shape[0])))
    flat = jnp.concatenate(parts)
    return jnp.pad(flat, (0, SMALL_ROWS * 128 - flat.shape[0])).reshape(SMALL_ROWS, 128)


def _unpack_small(buf, shapes):
    flat = buf.reshape(-1)
    out, off = {}, 0
    for name, n in _SMALL:
        size = 1
        for s in shapes[name]:
            size *= s
        out[name] = flat[off:off + size].reshape(shapes[name])
        off += n
    return out


WEIGHT_ORDER = ('g_mix', 'w_in', 'lb_logits', 'hg_norm_g', 'attn_sinks', 'w_out', 'g_ffn', 'w_up', 'conv_w',
                'conv_b', 'w_down', 'g_ple', 'w_ple_gate', 'w_ple_up', 'g_final')


def kernel(x, p, g_mix, w_in, lb_logits, hg_norm_g, attn_sinks, w_out, g_ffn, w_up, conv_w, conv_b, w_down, g_ple, w_ple_gate, w_ple_up, g_final, loss_target, m_g_mix, m_w_in, m_lb_logits, m_hg_norm_g, m_attn_sinks, m_w_out, m_g_ffn, m_w_up, m_conv_w, m_conv_b, m_w_down, m_g_ple, m_w_ple_gate, m_w_ple_up, m_g_final, v_g_mix, v_w_in, v_lb_logits, v_hg_norm_g, v_attn_sinks, v_w_out, v_g_ffn, v_w_up, v_conv_w, v_conv_b, v_w_down, v_g_ple, v_w_ple_gate, v_w_ple_up, v_g_final):
    W = dict(g_mix=g_mix, w_in=w_in, lb_logits=lb_logits, hg_norm_g=hg_norm_g, attn_sinks=attn_sinks,
             w_out=w_out, g_ffn=g_ffn, w_up=w_up, conv_w=conv_w, conv_b=conv_b, w_down=w_down, g_ple=g_ple,
             w_ple_gate=w_ple_gate, w_ple_up=w_ple_up, g_final=g_final)
    M = dict(g_mix=m_g_mix, w_in=m_w_in, lb_logits=m_lb_logits, hg_norm_g=m_hg_norm_g, attn_sinks=m_attn_sinks,
             w_out=m_w_out, g_ffn=m_g_ffn, w_up=m_w_up, conv_w=m_conv_w, conv_b=m_conv_b, w_down=m_w_down,
             g_ple=m_g_ple, w_ple_gate=m_w_ple_gate, w_ple_up=m_w_ple_up, g_final=m_g_final)
    V = dict(g_mix=v_g_mix, w_in=v_w_in, lb_logits=v_lb_logits, hg_norm_g=v_hg_norm_g, attn_sinks=v_attn_sinks,
             w_out=v_w_out, g_ffn=v_g_ffn, w_up=v_w_up, conv_w=v_conv_w, conv_b=v_conv_b, w_down=v_w_down,
             g_ple=v_g_ple, w_ple_gate=v_w_ple_gate, w_ple_up=v_w_ple_up, g_final=v_g_final)
    S = x.shape[1]
    hg_rows = min(ROW_TILE, S)
    xi, yi, ci = _place()
    chip = 2 * xi + yi

    slab = _pack_shards({n: W[n] for n, _ in _PACK}).astype(BF16)
    gathered = weights_allgather(slab)
    cw_shard = jnp.pad(conv_w.reshape(-1), (0, 72 * 128 - conv_w.size)).reshape(72, 128)
    cw_all = small_allgather(cw_shard)
    cw_full = jnp.concatenate(
        [cw_all[2 * k].reshape(-1)[:conv_w.size].reshape(conv_w.shape) for k in range(N_CHIPS)], axis=2)
    lbrows = lb_fwd(lb_logits)

    h = x[0]
    saved = []
    for i in range(DEPTH):
        wf = _full_from_chips(gathered, i)
        lbr = lbrows[8 * i:8 * i + 8]
        ng = hg_norm_g[i][None]
        sinks_b = jnp.broadcast_to(attn_sinks[i][:, None], (AT_Q_HEADS, 128))
        cw8 = jnp.pad(cw_full[i], ((0, 5), (0, 0)))
        cb = conv_b[i][None]
        u = rmsnorm_fwd(h, g_mix[i][None], name="rmsnorm_fwd")
        proj = mm(u, wf["w_in"], name="mm_in")
        y, o_raw, states = hgrn_fwd(proj, lbr, ng, D_MODEL, rows=hg_rows)
        y = swa_fwd(proj, sinks_b, y)
        h1 = mm(y, wf["w_out"], res=h, name="mm_out")
        u2 = rmsnorm_fwd(h1, g_ffn[i][None], name="rmsnorm_fwd")
        hh = mm(u2, wf["w_up"], name="mm_up")
        act = convffn_fwd(hh, cw8, cb)
        h2 = mm(act, wf["w_down"], res=h1, name="mm_down")
        u3 = rmsnorm_fwd(h2, g_ple[i][None], name="rmsnorm_fwd")
        gpre = mm(u3, wf["w_ple_gate"], name="mm_gate")
        h3 = ple_fwd(h2, gpre, p[i, 0], wf["w_ple_up"])
        saved.append(dict(wf=wf, lbr=lbr, ng=ng, sinks_b=sinks_b, cw8=cw8, cb=cb, h=h, u=u, proj=proj,
                          o_raw=o_raw, states=states, y=y, h1=h1, u2=u2, hh=hh, act=act, h2=h2, u3=u3,
                          gpre=gpre))
        h = h3

    dh, dhb, loss_acc, dg_final = loss_head(h, g_final[None], loss_target[0])

    gfull = {n: [None] * DEPTH for n, _ in _PACK}
    gsmall = {n: [None] * DEPTH for n in ("g_mix", "hg_norm_g", "attn_sinks", "g_ffn", "conv_w", "conv_b", "g_ple")}
    dlbrows = [None] * DEPTH
    for i in reversed(range(DEPTH)):
        s = saved[i]
        wf = s["wf"]
        dpu, dgp = ple_bwd(dh, s["gpre"], p[i, 0], wf["w_ple_up"])
        gfull["w_ple_up"][i] = mm_tn(p[i, 0], dpu, name="mm_tn_pu")
        gfull["w_ple_gate"][i] = mm_tn(s["u3"], dgp, name="mm_tn_gate")
        du3 = mm(dgp, wf["w_ple_gate"], nt=True, name="mm_nt_gate")
        dh2, dh2b, dg = rmsnorm_bwd(du3, s["h2"], g_ple[i][None], dh, name="rmsnorm_bwd")
        gsmall["g_ple"][i] = dg[0]
        gfull["w_down"][i] = mm_tn(s["act"], dh2b, name="mm_tn_down")
        dact = mm(dh2b, wf["w_down"], nt=True, name="mm_nt_down")
        da, db, dcw = convffn_bwd(s["hh"], dact, s["cw8"], s["cb"])
        gsmall["conv_w"][i] = dcw[0:3]
        gsmall["conv_b"][i] = dcw[3]
        dhh = jnp.concatenate([da, db], axis=1)
        gfull["w_up"][i] = mm_tn(s["u2"], dhh, name="mm_tn_up")
        du2 = mm(dhh, wf["w_up"], nt=True, name="mm_nt_up")
        dh1, dh1b, dg = rmsnorm_bwd(du2, s["h1"], g_ffn[i][None], dh2, name="rmsnorm_bwd")
        gsmall["g_ffn"][i] = dg[0]
        gfull["w_out"][i] = mm_tn(s["y"], dh1b, name="mm_tn_out")
        dy = mm(dh1b, wf["w_out"], nt=True, name="mm_nt_out")
        dq_at, dko, dkp, dvo, dvp, dsk = swa_bwd(s["proj"], s["sinks_b"], dy)
        gsmall["attn_sinks"][i] = dsk[:, 0]
        hq, hz, hv, hgp, dlbr, dng = hgrn_bwd(s["proj"], s["o_raw"], s["states"], dy, s["lbr"], s["ng"],
                                              rows=hg_rows)
        dlbrows[i] = dlbr
        gsmall["hg_norm_g"][i] = dng[0]
        dproj = assemble_dproj((hq, hz, hv, hgp), dq_at, dko, dkp, dvo, dvp, rows=hg_rows)
        gfull["w_in"][i] = mm_tn(s["u"], dproj, name="mm_tn_in")
        du = mm(dproj, wf["w_in"], nt=True, name="mm_nt_in")
        dh, dhb, dg = rmsnorm_bwd(du, s["h"], g_mix[i][None], dh1, name="rmsnorm_bwd")
        gsmall["g_mix"][i] = dg[0]
    grad_x = dh[None]
    dlb_logits = lb_bwd(jnp.concatenate(dlbrows, axis=0), lb_logits)[0:DEPTH]

    parts = []
    for i in range(DEPTH):
        for name, _ in _PACK:
            parts.append(_split_to_chips(gfull[name][i], name))
    pk = jnp.concatenate(parts, axis=1).astype(BF16)
    pk = pk.reshape(N_CHIPS, 2, HALF_ROWS, D_MODEL)
    p_mine = lax.dynamic_index_in_dim(pk, ci, axis=1, keepdims=False)
    p_other = lax.dynamic_index_in_dim(pk, 1 - ci, axis=1, keepdims=False)
    from_sib = sibling_swap(p_other, name="sibling_swap_partials")
    pair = sum_slots(jnp.stack([p_mine.reshape(-1, D_MODEL), from_sib.reshape(-1, D_MODEL)]),
                     out_dtype=BF16, name="sum_pair")
    from_chips = chip_exchange(pair.reshape(N_CHIPS, HALF_ROWS, D_MODEL))
    mine_sum = sum_slots(from_chips, out_dtype=F32, name="sum_chips")
    sib_sum = sibling_swap(mine_sum, name="sibling_swap_sums")
    lo = jnp.where(ci == 0, mine_sum, sib_sum)
    hi = jnp.where(ci == 0, sib_sum, mine_sum)
    gshard = _unpack_shards(jnp.concatenate([lo, hi], axis=0))

    small = dict(loss=loss_acc[0, 0:1], g_final=dg_final[0], lb_logits=dlb_logits,
                 **{n: jnp.stack(v) for n, v in gsmall.items()})
    small_sum = sum_slots(small_allgather(_pack_small(small)), out_dtype=F32, name="sum_small")
    shapes = {n: W[n].shape for n in W}
    shapes["loss"] = (1,)
    shapes["conv_w"] = (DEPTH, 3, D_FF)
    gs = _unpack_small(small_sum, shapes)
    loss = gs["loss"][0]
    cshard = conv_w.shape[2]
    grads = dict(gshard)
    for n in ("g_mix", "lb_logits", "hg_norm_g", "attn_sinks", "g_ffn", "conv_b", "g_ple", "g_final"):
        grads[n] = gs[n]
    grads["conv_w"] = lax.dynamic_slice_in_dim(gs["conv_w"], chip * cshard, cshard, axis=2)

    delta, new_m, new_v = {}, {}, {}
    small_names = ("g_final", "g_mix", "lb_logits", "hg_norm_g", "attn_sinks", "g_ffn", "conv_b", "g_ple")
    sshapes = {n: W[n].shape for n in small_names}

    def pack_s(d):
        z = dict(d)
        z["loss"] = jnp.zeros((1,), F32)
        z["conv_w"] = jnp.zeros((1,), F32)
        return _pack_small(z)

    sd, sm, sv = adamw(pack_s(W), pack_s(grads), pack_s(M), pack_s(V), name="adamw_small")
    for out, buf in ((delta, sd), (new_m, sm), (new_v, sv)):
        un = _unpack_small(buf, {**sshapes, "loss": (1,), "conv_w": (1,)})
        for n in small_names:
            out[n] = un[n]
    for n in ("w_in", "w_out", "w_up", "w_down", "w_ple_gate", "w_ple_up", "conv_w"):
        shp = W[n].shape
        two_d = (-1, shp[-1])
        d_, m_, v_ = adamw(W[n].reshape(two_d), grads[n].reshape(two_d), M[n].reshape(two_d),
                           V[n].reshape(two_d), name="adamw_" + n)
        delta[n], new_m[n], new_v[n] = d_.reshape(shp), m_.reshape(shp), v_.reshape(shp)

    return (loss, grad_x, *[grads[n] for n in WEIGHT_ORDER], *[delta[n] for n in WEIGHT_ORDER],
            *[new_m[n] for n in WEIGHT_ORDER], *[new_v[n] for n in WEIGHT_ORDER])
```

```python
import functools

import jax
import jax.numpy as jnp
from jax import lax
from jax.experimental import pallas as pl
from jax.experimental.pallas import tpu as pltpu

F32 = jnp.float32
BF16 = jnp.bfloat16

D_MODEL = 1024
DEPTH = 4
PLE_DIM = 256
HG_WIDTH = 512
HG_HEADS = 4
HG_DK = 128
HG_CHUNK = 64
HG_SUB = 16
AT_WIDTH = 512
AT_HEAD_DIM = 64
AT_Q_HEADS = 8
AT_KV_HEADS = 2
AT_GROUP = 4
WINDOW = 128
D_FF = 2816
IN_WIDTH = 2816
EPS = 1e-6
MASK_VALUE = -1e30
LB_FLOOR = 1e-30

ADAM_LR = 0.001
ADAM_B1 = 0.9
ADAM_B2 = 0.999
ADAM_EPS = 1e-08
ADAM_WD = 0.01
ADAM_STEP = 10

VMEM_LIMIT = 48 * 1024 * 1024


def _params(*sem):
    return pltpu.CompilerParams(dimension_semantics=sem, vmem_limit_bytes=VMEM_LIMIT)


def _dot(a, b, dims=(((1,), (0,)), ((), ()))):
    return lax.dot_general(a.astype(BF16), b.astype(BF16), dims, preferred_element_type=F32)


def _dot_nt(a, b):
    return _dot(a, b, (((1,), (1,)), ((), ())))


def _dot_tn(a, b):
    return _dot(a, b, (((0,), (0,)), ((), ())))


def _dot_f32(a, b, dims=(((1,), (0,)), ((), ()))):
    return lax.dot_general(a, b, dims, preferred_element_type=F32, precision=lax.Precision.HIGHEST)


def _sigmoid(x):
    return 1.0 / (1.0 + jnp.exp(-x))


def _logsig(x):
    return jnp.minimum(x, 0.0) - jnp.log(1.0 + jnp.exp(-jnp.abs(x)))


def _colsum(x):
    return jnp.sum(x, axis=0, keepdims=True)


def _rowsum(x):
    return jnp.sum(x, axis=1, keepdims=True)


def _tri(n):
    r = lax.broadcasted_iota(jnp.int32, (n, n), 0)
    c = lax.broadcasted_iota(jnp.int32, (n, n), 1)
    return (r >= c).astype(F32)


def _hg_gates(qp, z, a, c, oml):
    sq = _sigmoid(qp)
    q = qp * sq
    t = c + _logsig(z)
    mx = jnp.maximum(a, t)
    logf = mx + jnp.log(1.0 + jnp.exp(-jnp.abs(a - t)))
    snz = _sigmoid(-z)
    k = oml * snz
    return q, sq, t, logf, snz, k


_HEADS = range(HG_HEADS)


def _lanes(h):
    return slice(h * HG_DK, (h + 1) * HG_DK)


def _head(x, h):
    return x[:, _lanes(h)]


def _hg_chunk_fwd(q, k, v, logf, st, b_s, k_s, v_s):
    C, U = HG_CHUNK, HG_SUB
    tri = _tri(C)
    b = [_dot_f32(tri, logf[h]) for h in _HEADS]
    for h in _HEADS:
        b_s[h] = b[h]
        k_s[h] = k[h]
        v_s[h] = v[h]
    o = [_dot_nt(q[h] * jnp.exp(b[h]), st[h]) for h in _HEADS]
    bl = [b[h][C - 1:C] for h in _HEADS]
    upd = [_dot_tn(v[h], k[h] * jnp.exp(bl[h] - b[h])) for h in _HEADS]
    rows = lax.broadcasted_iota(jnp.int32, (C, HG_DK), 0)
    trow = lax.broadcasted_iota(jnp.int32, (U, HG_DK), 0)
    outs = [[] for _ in _HEADS]
    for i in range(C // U):
        lo = i * U
        b_i = [b[h][lo:lo + U] for h in _HEADS]
        q_i = [q[h][lo:lo + U] for h in _HEADS]
        o_i = [o[h][lo:lo + U] for h in _HEADS]
        if i > 0:
            qe = [q_i[h] * jnp.exp(b_i[h] - b_i[h][0:1]) for h in _HEADS]
            ke = [jnp.where(rows < lo, k[h] * jnp.exp(jnp.minimum(b_i[h][0:1] - b[h], 0.0)), 0.0) for h in _HEADS]
            att = [_dot_nt(qe[h], ke[h]) for h in _HEADS]
            off = [_dot(att[h], v[h]) for h in _HEADS]
            o_i = [o_i[h] + off[h] for h in _HEADS]
        for s in range(U):
            m = trow >= s
            for h in _HEADS:
                bs = b_s[h, lo + s:lo + s + 1, :]
                ks = k_s[h, lo + s:lo + s + 1, :]
                vs = v_s[h, lo + s:lo + s + 1, :]
                dec = jnp.where(m, jnp.exp(jnp.where(m, b_i[h] - bs, 0.0)), 0.0)
                w = _rowsum(q_i[h] * dec * ks)
                o_i[h] = o_i[h] + w * vs
        for h in _HEADS:
            outs[h].append(o_i[h])
    o = [jnp.concatenate(outs[h], axis=0) for h in _HEADS]
    st_new = [st[h] * jnp.exp(bl[h]) + upd[h] for h in _HEADS]
    return o, st_new, b


def _hg_post(o, gp, ng):
    rs = lax.rsqrt(jnp.mean(o * o, axis=1, keepdims=True) + EPS)
    sg = _sigmoid(gp)
    return o * rs * ng * sg, rs, sg


def hgrn_fwd(proj, lbrows, ng, y_width, *, rows):
    S = proj.shape[0]
    C = HG_CHUNK
    cpb = rows // C
    nb = S // rows

    def body(qp_ref, z_ref, v_ref, gp_ref, lb_ref, ng_ref, y_ref, o_ref, st_ref, st, b_s, k_s, v_s):
        @pl.when(pl.program_id(0) == 0)
        def _():
            st[...] = jnp.zeros_like(st)

        a, c, oml = lb_ref[0:1, :], lb_ref[1:2, :], lb_ref[2:3, :]
        ngr = ng_ref[...]

        def chunk(ci, carry):
            off = pl.multiple_of(ci * C, C)
            sl = pl.ds(off, C)
            for h in _HEADS:
                st_ref[h, ci] = st[h]
            gates = [_hg_gates(qp_ref[sl, _lanes(h)], z_ref[sl, _lanes(h)],
                               _head(a, h), _head(c, h), _head(oml, h)) for h in _HEADS]
            q = [g[0] for g in gates]
            logf = [g[3] for g in gates]
            k = [g[5] for g in gates]
            v = [v_ref[sl, _lanes(h)] for h in _HEADS]
            o, st_new, _ = _hg_chunk_fwd(q, k, v, logf, [st[h] for h in _HEADS], b_s, k_s, v_s)
            for h in _HEADS:
                y, _, _ = _hg_post(o[h], gp_ref[sl, _lanes(h)], _head(ngr, h))
                y_ref[sl, _lanes(h)] = y.astype(y_ref.dtype)
                o_ref[sl, _lanes(h)] = o[h]
                st[h] = st_new[h]
            return carry

        lax.fori_loop(0, cpb, chunk, 0)

    col = lambda kblk: pl.BlockSpec((rows, HG_WIDTH), lambda r: (r, kblk))
    return pl.pallas_call(
        body,
        name="hgrn_fwd",
        grid=(nb,),
        in_specs=[col(0), col(1), col(2), col(3), _const_spec((8, HG_WIDTH)), _const_spec((1, HG_WIDTH))],
        out_specs=[col(0), col(0),
                   pl.BlockSpec((HG_HEADS, cpb, HG_DK, HG_DK), lambda r: (0, r, 0, 0))],
        out_shape=[jax.ShapeDtypeStruct((S, y_width), BF16),
                   jax.ShapeDtypeStruct((S, HG_WIDTH), F32),
                   jax.ShapeDtypeStruct((HG_HEADS, S // C, HG_DK, HG_DK), F32)],
        scratch_shapes=[pltpu.VMEM((HG_HEADS, HG_DK, HG_DK), F32)] + [pltpu.VMEM((HG_HEADS, C, HG_DK), F32)] * 3,
        compiler_params=_params("arbitrary"),
    )(proj, proj, proj, proj, lbrows, ng)


def hgrn_bwd(proj, o_raw, states, dy, lbrows, ng, *, rows):
    S = proj.shape[0]
    C, U = HG_CHUNK, HG_SUB
    cpb = rows // C
    nb = S // rows

    def body(qp_ref, z_ref, v_ref, gp_ref, o_ref, st_ref, dy_ref, lb_ref, ng_ref,
             dqp_ref, dz_ref, dv_ref, dgp_ref, dlb_ref, dng_ref,
             dst, b_s, k_s, v_s, dbs, dks, dvs):
        @pl.when(pl.program_id(0) == 0)
        def _():
            dst[...] = jnp.zeros_like(dst)
            dlb_ref[...] = jnp.zeros_like(dlb_ref)
            dng_ref[...] = jnp.zeros_like(dng_ref)

        a, c, oml = lb_ref[0:1, :], lb_ref[1:2, :], lb_ref[2:3, :]
        ngr = ng_ref[...]
        rows_i = lax.broadcasted_iota(jnp.int32, (C, HG_DK), 0)
        trow = lax.broadcasted_iota(jnp.int32, (U, HG_DK), 0)
        tri = _tri(C)
        H = _HEADS

        def chunk(cj, carry):
            ci = cpb - 1 - cj
            off = pl.multiple_of(ci * C, C)
            sl = pl.ds(off, C)
            qp = [qp_ref[sl, _lanes(h)] for h in H]
            v = [v_ref[sl, _lanes(h)] for h in H]
            st = [st_ref[h, ci] for h in H]
            gates = [_hg_gates(qp[h], z_ref[sl, _lanes(h)], _head(a, h), _head(c, h), _head(oml, h)) for h in H]
            q, sq, t, logf, snz, k = ([g[j] for g in gates] for j in range(6))
            b = [_dot_f32(tri, logf[h]) for h in H]
            for h in H:
                b_s[h] = b[h]
                k_s[h] = k[h]
                v_s[h] = v[h]
            do = []
            for h in H:
                o = o_ref[sl, _lanes(h)]
                dyv = dy_ref[sl, _lanes(h)]
                ngh = _head(ngr, h)
                rs = lax.rsqrt(jnp.mean(o * o, axis=1, keepdims=True) + EPS)
                sg = _sigmoid(gp_ref[sl, _lanes(h)])
                xh = o * rs
                dgp_ref[sl, _lanes(h)] = (dyv * xh * ngh * sg * (1.0 - sg)).astype(dgp_ref.dtype)
                don = dyv * sg
                dng_ref[0:1, _lanes(h)] += _colsum(don * xh)
                dxh = don * ngh
                do.append(rs * (dxh - xh * jnp.mean(dxh * xh, axis=1, keepdims=True)))
            eb = [jnp.exp(b[h]) for h in H]
            qb = [q[h] * eb[h] for h in H]
            dstv = [dst[h] for h in H]
            bl = [b[h][C - 1:C] for h in H]
            el = [jnp.exp(bl[h]) for h in H]
            ex = [jnp.exp(bl[h] - b[h]) for h in H]
            kd = [k[h] * ex[h] for h in H]
            dqb = [_dot(do[h], st[h]) for h in H]
            dst_acc = [_dot_tn(do[h], qb[h]) for h in H]
            dv0 = [_dot_nt(kd[h], dstv[h]) for h in H]
            dkd = [_dot(v[h], dstv[h]) for h in H]
            dq = [dqb[h] * eb[h] for h in H]
            for h in H:
                g2 = dkd[h] * kd[h]
                dbl = _colsum(dstv[h] * st[h]) * el[h] + _colsum(g2)
                dst[h] = dstv[h] * el[h] + dst_acc[h]
                dbs[h] = dqb[h] * qb[h] - g2
                dks[h] = dkd[h] * ex[h]
                dvs[h] = dv0[h]
                dbs[h, C - 1:C, :] += dbl
            dq_parts = [[] for _ in H]
            for i in range(C // U):
                lo = i * U
                b_i = [b[h][lo:lo + U] for h in H]
                q_i = [q[h][lo:lo + U] for h in H]
                do_i = [do[h][lo:lo + U] for h in H]
                dq_i = [dq[h][lo:lo + U] for h in H]
                db_i = [jnp.zeros((U, HG_DK), F32) for _ in H]
                if i > 0:
                    e1 = [jnp.exp(b_i[h] - b_i[h][0:1]) for h in H]
                    qe = [q_i[h] * e1[h] for h in H]
                    e2 = [jnp.where(rows_i < lo, jnp.exp(jnp.minimum(b_i[h][0:1] - b[h], 0.0)), 0.0) for h in H]
                    ke = [k[h] * e2[h] for h in H]
                    att = [_dot_nt(qe[h], ke[h]) for h in H]
                    datt = [_dot_nt(do_i[h], v[h]) for h in H]
                    dv_add = [_dot_tn(att[h], do_i[h]) for h in H]
                    dqe = [_dot(datt[h], ke[h]) for h in H]
                    dke = [_dot_tn(datt[h], qe[h]) for h in H]
                    for h in H:
                        dvs[h] += dv_add[h]
                        dq_i[h] = dq_i[h] + dqe[h] * e1[h]
                        g = dqe[h] * qe[h]
                        db_i[h] = db_i[h] + g
                        gk = dke[h] * ke[h]
                        dks[h] += dke[h] * e2[h]
                        dbs[h] -= gk
                        dbs[h, lo:lo + 1, :] += _colsum(gk) - _colsum(g)
                for s in range(U):
                    row = slice(lo + s, lo + s + 1)
                    m = trow >= s
                    for h in H:
                        bs = b_s[h, row, :]
                        ks = k_s[h, row, :]
                        vs = v_s[h, row, :]
                        dec = jnp.where(m, jnp.exp(jnp.where(m, b_i[h] - bs, 0.0)), 0.0)
                        qd = q_i[h] * dec
                        y_ = qd * ks
                        w = _rowsum(y_)
                        dw = _rowsum(do_i[h] * vs)
                        dvs[h, row, :] += _colsum(w * do_i[h])
                        dq_i[h] = dq_i[h] + dw * dec * ks
                        dks[h, row, :] += _colsum(dw * qd)
                        g = dw * y_
                        db_i[h] = db_i[h] + g
                        dbs[h, row, :] -= _colsum(g)
                for h in H:
                    dbs[h, lo:lo + U, :] += db_i[h]
                    dq_parts[h].append(dq_i[h])
            dlogf = [_dot_f32(tri, dbs[h], (((0,), (0,)), ((), ()))) for h in H]
            for h in H:
                dqh = jnp.concatenate(dq_parts[h], axis=0)
                dk = dks[h]
                ah, omlh = _head(a, h), _head(oml, h)
                pa = jnp.exp(ah - logf[h])
                pt = jnp.exp(t[h] - logf[h])
                dt = dlogf[h] * pt
                dlb_ref[0:1, _lanes(h)] += _colsum(dlogf[h] * pa)
                dlb_ref[1:2, _lanes(h)] += _colsum(dt)
                dlb_ref[2:3, _lanes(h)] += _colsum(dk * snz[h])
                dz = dt * snz[h] - dk * omlh * snz[h] * (1.0 - snz[h])
                dqp = dqh * (sq[h] + qp[h] * sq[h] * (1.0 - sq[h]))
                dqp_ref[sl, _lanes(h)] = dqp.astype(dqp_ref.dtype)
                dz_ref[sl, _lanes(h)] = dz.astype(dz_ref.dtype)
                dv_ref[sl, _lanes(h)] = dvs[h].astype(dv_ref.dtype)
            return carry

        lax.fori_loop(0, cpb, chunk, 0)

    rev = lambda r: nb - 1 - r
    col = lambda kblk: pl.BlockSpec((rows, HG_WIDTH), lambda r: (rev(r), kblk))
    acc = _const_spec((8, HG_WIDTH))
    return pl.pallas_call(
        body,
        name="hgrn_bwd",
        grid=(nb,),
        in_specs=[col(0), col(1), col(2), col(3), col(0),
                  pl.BlockSpec((HG_HEADS, cpb, HG_DK, HG_DK), lambda r: (0, rev(r), 0, 0)),
                  col(0), acc, _const_spec((1, HG_WIDTH))],
        out_specs=[col(0)] * 4 + [acc, acc],
        out_shape=[jax.ShapeDtypeStruct((S, HG_WIDTH), BF16)] * 4
                  + [jax.ShapeDtypeStruct((8, HG_WIDTH), F32)] * 2,
        scratch_shapes=[pltpu.VMEM((HG_HEADS, HG_DK, HG_DK), F32)] + [pltpu.VMEM((HG_HEADS, C, HG_DK), F32)] * 6,
        compiler_params=_params("arbitrary"),
    )(proj, proj, proj, proj, o_raw, states, dy, lbrows, ng)


GROUP_LANES = AT_GROUP * WINDOW


def _swa_valid_t(n):
    W = WINDOW
    kpos = lax.broadcasted_iota(jnp.int32, (2 * W, GROUP_LANES), 0)
    qpos = (lax.broadcasted_iota(jnp.int32, (2 * W, GROUP_LANES), 1) & (W - 1)) + W
    rel = qpos - kpos
    return (rel >= 0) & (rel < W) & jnp.logical_not((n == 0) & (kpos < W))


def _group_lanes(xt, g):
    Dh = AT_HEAD_DIM
    return jnp.concatenate([xt[(g * AT_GROUP + j) * Dh:(g * AT_GROUP + j + 1) * Dh] for j in range(AT_GROUP)],
                           axis=1)


def _swa_probs_t(kg, qg, sink_row, valid):
    s = _dot(kg, qg) * (AT_HEAD_DIM ** -0.5)
    s = jnp.where(valid, s, MASK_VALUE)
    m = jnp.maximum(jnp.max(s, axis=0, keepdims=True), sink_row)
    e = jnp.exp(s - m)
    es = jnp.exp(sink_row - m)
    inv = 1.0 / (_colsum(e) + es)
    return e * inv, es * inv


def swa_fwd(proj, sink_rows, y):
    S = proj.shape[0]
    W, Dh = WINDOW, AT_HEAD_DIM
    nb = S // W

    def body(q_ref, kp_ref, k_ref, vp_ref, v_ref, sk_ref, y_in, y_ref):
        del y_in
        valid = _swa_valid_t(pl.program_id(0))
        kk = jnp.concatenate([kp_ref[...], k_ref[...]], axis=0)
        vt = jnp.concatenate([vp_ref[...], v_ref[...]], axis=0).T
        qt = q_ref[...].T
        outs = []
        for g in range(AT_KV_HEADS):
            p, _ = _swa_probs_t(kk[:, g * Dh:(g + 1) * Dh], _group_lanes(qt, g),
                                sk_ref[8 * g:8 * g + 1, :], valid)
            ot = _dot(vt[g * Dh:(g + 1) * Dh], p)
            outs += [ot[:, j * W:(j + 1) * W] for j in range(AT_GROUP)]
        y_ref[...] = jnp.concatenate(outs, axis=0).T.astype(y_ref.dtype)

    prev = lambda n: jnp.maximum(n - 1, 0)
    return pl.pallas_call(
        body,
        name="swa_fwd",
        grid=(nb,),
        in_specs=[pl.BlockSpec((W, AT_WIDTH), lambda n: (n, 4)),
                  pl.BlockSpec((W, 128), lambda n: (prev(n), 20)),
                  pl.BlockSpec((W, 128), lambda n: (n, 20)),
                  pl.BlockSpec((W, 128), lambda n: (prev(n), 21)),
                  pl.BlockSpec((W, 128), lambda n: (n, 21)),
                  pl.BlockSpec((8 * AT_KV_HEADS, GROUP_LANES), lambda n: (0, 0)),
                  pl.BlockSpec(memory_space=pl.ANY)],
        out_specs=pl.BlockSpec((W, AT_WIDTH), lambda n: (n, 1)),
        out_shape=jax.ShapeDtypeStruct(y.shape, y.dtype),
        input_output_aliases={6: 0},
        compiler_params=_params("parallel"),
    )(proj, proj, proj, proj, proj, sink_rows, y)


def swa_bwd(proj, sink_rows, dy):
    S = proj.shape[0]
    W, Dh = WINDOW, AT_HEAD_DIM
    nb = S // W
    scale = Dh ** -0.5

    def body(q_ref, kp_ref, k_ref, vp_ref, v_ref, sk_ref, dy_ref,
             dq_ref, dko_ref, dkp_ref, dvo_ref, dvp_ref, dsk_ref):
        n = pl.program_id(0)

        @pl.when(n == 0)
        def _():
            dsk_ref[...] = jnp.zeros_like(dsk_ref)

        valid = _swa_valid_t(n)
        kk = jnp.concatenate([kp_ref[...], k_ref[...]], axis=0)
        vv = jnp.concatenate([vp_ref[...], v_ref[...]], axis=0)
        kt = kk.T
        qt = q_ref[...].T
        dot_ = dy_ref[...].T
        dqs, dks, dvs = [], [], []
        for g in range(AT_KV_HEADS):
            qg = _group_lanes(qt, g)
            dog = _group_lanes(dot_, g)
            vg = vv[:, g * Dh:(g + 1) * Dh]
            p, ps = _swa_probs_t(kk[:, g * Dh:(g + 1) * Dh], qg, sk_ref[8 * g:8 * g + 1, :], valid)
            dp = _dot(vg, dog)
            delta = _colsum(dp * p)
            ds = p * (dp - delta) * scale
            dqt = _dot(kt[g * Dh:(g + 1) * Dh], ds)
            dqs += [dqt[:, j * W:(j + 1) * W] for j in range(AT_GROUP)]
            dks.append(_dot_nt(ds, qg))
            dvs.append(_dot_nt(p, dog))
            dsk_ref[8 * g:8 * g + 1, :] += -(ps * delta)
        dq_ref[...] = jnp.concatenate(dqs, axis=0).T.astype(dq_ref.dtype)
        dk = jnp.concatenate(dks, axis=1)
        dv = jnp.concatenate(dvs, axis=1)
        dkp_ref[...] = dk[:W]
        dko_ref[...] = dk[W:]
        dvp_ref[...] = dv[:W]
        dvo_ref[...] = dv[W:]

        @pl.when(n == nb - 1)
        def _():
            for g in range(AT_KV_HEADS):
                for j in range(AT_GROUP):
                    tot = _rowsum(dsk_ref[8 * g:8 * g + 1, j * W:(j + 1) * W])
                    dsk_ref[8 * g + 1 + j:8 * g + 2 + j, :] = jnp.broadcast_to(tot, (1, GROUP_LANES))

    prev = lambda n: jnp.maximum(n - 1, 0)
    kv = pl.BlockSpec((W, 128), lambda n: (n, 0))
    sk = pl.BlockSpec((8 * AT_KV_HEADS, GROUP_LANES), lambda n: (0, 0))
    return pl.pallas_call(
        body,
        name="swa_bwd",
        grid=(nb,),
        in_specs=[pl.BlockSpec((W, AT_WIDTH), lambda n: (n, 4)),
                  pl.BlockSpec((W, 128), lambda n: (prev(n), 20)),
                  pl.BlockSpec((W, 128), lambda n: (n, 20)),
                  pl.BlockSpec((W, 128), lambda n: (prev(n), 21)),
                  pl.BlockSpec((W, 128), lambda n: (n, 21)),
                  sk,
                  pl.BlockSpec((W, AT_WIDTH), lambda n: (n, 1))],
        out_specs=[pl.BlockSpec((W, AT_WIDTH), lambda n: (n, 0)), kv, kv, kv, kv, sk],
        out_shape=[jax.ShapeDtypeStruct((S, AT_WIDTH), BF16)]
                  + [jax.ShapeDtypeStruct((S, 128), F32)] * 4
                  + [jax.ShapeDtypeStruct((8 * AT_KV_HEADS, GROUP_LANES), F32)],
        compiler_params=_params("arbitrary"),
    )(proj, proj, proj, proj, proj, sink_rows, dy)


def assemble_dproj(hg_grads, dq_at, dko, dkp, dvo, dvp, *, rows):
    S = dq_at.shape[0]
    W = WINDOW
    nb = S // W
    bpr = rows // W

    def body(a0, a1, a2, a3, dq, ko, kp, kpn, vo, vp, vpn, out):
        r = pl.program_id(0)
        for i, a in enumerate((a0, a1, a2, a3)):
            out[:, i * HG_WIDTH:(i + 1) * HG_WIDTH] = a[...]
        base = 4 * HG_WIDTH
        out[:, base:base + AT_WIDTH] = dq[...]
        last = (r == pl.num_programs(0) - 1)
        for off, own, pv, pvn in ((base + AT_WIDTH, ko, kp, kpn), (base + AT_WIDTH + 128, vo, vp, vpn)):
            if bpr > 1:
                out[0:rows - W, off:off + 128] = (own[0:rows - W, :] + pv[W:rows, :]).astype(out.dtype)
            nxt = jnp.where(last, 0.0, pvn[...])
            out[rows - W:rows, off:off + 128] = (own[rows - W:rows, :] + nxt).astype(out.dtype)

    hg = pl.BlockSpec((rows, HG_WIDTH), lambda r: (r, 0))
    blk = pl.BlockSpec((rows, 128), lambda r: (r, 0))
    nxt = pl.BlockSpec((W, 128), lambda r: (jnp.minimum((r + 1) * bpr, nb - 1), 0))
    return pl.pallas_call(
        body,
        name="assemble_dproj",
        grid=(S // rows,),
        in_specs=[hg, hg, hg, hg, pl.BlockSpec((rows, AT_WIDTH), lambda r: (r, 0)),
                  blk, blk, nxt, blk, blk, nxt],
        out_specs=pl.BlockSpec((rows, IN_WIDTH), lambda r: (r, 0)),
        out_shape=jax.ShapeDtypeStruct((S, IN_WIDTH), BF16),
        compiler_params=_params("parallel"),
    )(*hg_grads, dq_at, dko, dkp, dkp, dvo, dvp, dvp)


ROW_TILE = 512
COL_TILE = 1408


def _col_tile(n):
    return n if n <= COL_TILE else COL_TILE


def mm(a, b, *, nt=False, out_dtype=F32, res=None, name):
    M, K = a.shape
    N = b.shape[0] if nt else b.shape[1]
    tm = min(ROW_TILE, M)
    tn = _col_tile(N)
    assert M % tm == 0 and N % tn == 0

    def body(a_ref, b_ref, *rest):
        o_ref = rest[-1]
        acc = _dot_nt(a_ref[...], b_ref[...]) if nt else _dot(a_ref[...], b_ref[...])
        if res is not None:
            acc = acc + rest[0][...]
        o_ref[...] = acc.astype(o_ref.dtype)

    in_specs = [pl.BlockSpec((tm, K), lambda j, i: (i, 0)),
                pl.BlockSpec((tn, K), lambda j, i: (j, 0)) if nt else pl.BlockSpec((K, tn), lambda j, i: (0, j))]
    args = [a, b]
    if res is not None:
        in_specs.append(pl.BlockSpec((tm, tn), lambda j, i: (i, j)))
        args.append(res)
    return pl.pallas_call(
        body,
        name=name,
        grid=(N // tn, M // tm),
        in_specs=in_specs,
        out_specs=pl.BlockSpec((tm, tn), lambda j, i: (i, j)),
        out_shape=jax.ShapeDtypeStruct((M, N), out_dtype),
        compiler_params=_params("parallel", "parallel"),
    )(*args)


def mm_tn(a, b, *, name):
    M, K = a.shape
    N = b.shape[1]
    tm = min(ROW_TILE, M)
    tk = _col_tile(K)
    tn = _col_tile(N)
    assert M % tm == 0 and K % tk == 0 and N % tn == 0

    def body(a_ref, b_ref, o_ref):
        @pl.when(pl.program_id(2) == 0)
        def _():
            o_ref[...] = jnp.zeros_like(o_ref)

        o_ref[...] += _dot_tn(a_ref[...], b_ref[...])

    return pl.pallas_call(
        body,
        name=name,
        grid=(K // tk, N // tn, M // tm),
        in_specs=[pl.BlockSpec((tm, tk), lambda k, j, i: (i, k)),
                  pl.BlockSpec((tm, tn), lambda k, j, i: (i, j))],
        out_specs=pl.BlockSpec((tk, tn), lambda k, j, i: (k, j)),
        out_shape=jax.ShapeDtypeStruct((K, N), F32),
        compiler_params=_params("parallel", "parallel", "arbitrary"),
    )(a, b)


def _row_spec(tm, width):
    return pl.BlockSpec((tm, width), lambda i: (i, 0))


def _const_spec(shape):
    return pl.BlockSpec(shape, lambda *_: (0,) * len(shape))


def rmsnorm_fwd(h, g, *, name):
    S, D = h.shape
    tm = min(ROW_TILE, S)

    def body(h_ref, g_ref, u_ref):
        x = h_ref[...]
        rs = lax.rsqrt(jnp.mean(x * x, axis=1, keepdims=True) + EPS)
        u_ref[...] = (x * rs * g_ref[...]).astype(u_ref.dtype)

    return pl.pallas_call(
        body, name=name, grid=(S // tm,),
        in_specs=[_row_spec(tm, D), _const_spec((1, D))],
        out_specs=_row_spec(tm, D),
        out_shape=jax.ShapeDtypeStruct((S, D), BF16),
        compiler_params=_params("parallel"),
    )(h, g)


def rmsnorm_bwd(du, h, g, dres, *, name):
    S, D = h.shape
    tm = min(ROW_TILE, S)

    def body(du_ref, h_ref, g_ref, dr_ref, dh_ref, dhb_ref, dg_ref):
        @pl.when(pl.program_id(0) == 0)
        def _():
            dg_ref[...] = jnp.zeros_like(dg_ref)

        x = h_ref[...]
        d = du_ref[...]
        rs = lax.rsqrt(jnp.mean(x * x, axis=1, keepdims=True) + EPS)
        xh = x * rs
        dg_ref[0:1, :] += _colsum(d * xh)
        dxh = d * g_ref[...]
        dh = dr_ref[...] + rs * (dxh - xh * jnp.mean(dxh * xh, axis=1, keepdims=True))
        dh_ref[...] = dh
        dhb_ref[...] = dh.astype(BF16)

    return pl.pallas_call(
        body, name=name, grid=(S // tm,),
        in_specs=[_row_spec(tm, D), _row_spec(tm, D), _const_spec((1, D)), _row_spec(tm, D)],
        out_specs=[_row_spec(tm, D), _row_spec(tm, D), _const_spec((8, D))],
        out_shape=[jax.ShapeDtypeStruct((S, D), F32), jax.ShapeDtypeStruct((S, D), BF16),
                   jax.ShapeDtypeStruct((8, D), F32)],
        compiler_params=_params("arbitrary"),
    )(du, h, g, dres)


def _shift_down(x, edge8, s):
    sh = pltpu.roll(x, s, 0)
    er = pltpu.roll(edge8, s, 0)
    row8 = lax.broadcasted_iota(jnp.int32, er.shape, 0)
    top = jnp.where(row8 < s, er, sh[0:8])
    return jnp.concatenate([top, sh[8:]], axis=0)


def _shift_up(x, s):
    return pltpu.roll(x, x.shape[0] - s, 0)


def _conv_pre(a, prev8, w_ref, cb_ref):
    a1 = _shift_down(a, prev8, 1)
    a2 = _shift_down(a, prev8, 2)
    ac = w_ref[2:3, :] * a + w_ref[1:2, :] * a1 + w_ref[0:1, :] * a2 + cb_ref[...]
    return ac, a1, a2


def convffn_fwd(hh, cw8, cb):
    S = hh.shape[0]
    tm = min(ROW_TILE, S)
    tn = _col_tile(D_FF)
    nj = D_FF // tn

    def body(a_ref, ap_ref, b_ref, w_ref, cb_ref, o_ref):
        prev8 = jnp.where(pl.program_id(1) == 0, 0.0, ap_ref[...])
        ac, _, _ = _conv_pre(a_ref[...], prev8, w_ref, cb_ref)
        o_ref[...] = (ac * _sigmoid(ac) * b_ref[...]).astype(o_ref.dtype)

    r8 = tm // 8
    return pl.pallas_call(
        body, name="convffn_fwd", grid=(nj, S // tm),
        in_specs=[pl.BlockSpec((tm, tn), lambda j, i: (i, j)),
                  pl.BlockSpec((8, tn), lambda j, i: (jnp.maximum(i * r8 - 1, 0), j)),
                  pl.BlockSpec((tm, tn), lambda j, i: (i, j + nj)),
                  pl.BlockSpec((8, tn), lambda j, i: (0, j)),
                  pl.BlockSpec((1, tn), lambda j, i: (0, j))],
        out_specs=pl.BlockSpec((tm, tn), lambda j, i: (i, j)),
        out_shape=jax.ShapeDtypeStruct((S, D_FF), BF16),
        compiler_params=_params("parallel", "parallel"),
    )(hh, hh, hh, cw8, cb)


def convffn_bwd(hh, dact, cw8, cb):
    S = hh.shape[0]
    tm = min(ROW_TILE, S)
    tn = _col_tile(D_FF)
    nj = D_FF // tn
    ni = S // tm

    def body(a_ref, ap_ref, an_ref, b_ref, bn_ref, d_ref, dn_ref, w_ref, cb_ref, o_a, o_b, dw_ref):
        i = pl.program_id(1)

        @pl.when(i == 0)
        def _():
            dw_ref[...] = jnp.zeros_like(dw_ref)

        prev8 = jnp.where(i == 0, 0.0, ap_ref[...])
        a = jnp.concatenate([a_ref[...], an_ref[...]], axis=0)
        b = jnp.concatenate([b_ref[...], bn_ref[...]], axis=0)
        d = jnp.concatenate([d_ref[...], jnp.where(i == ni - 1, 0.0, dn_ref[...])], axis=0)
        ac, a1, a2 = _conv_pre(a, prev8, w_ref, cb_ref)
        sa = _sigmoid(ac)
        o_b[...] = (d[0:tm] * ac[0:tm] * sa[0:tm]).astype(o_b.dtype)
        dac = d * b * (sa + ac * sa * (1.0 - sa))
        da = w_ref[2:3, :] * dac + w_ref[1:2, :] * _shift_up(dac, 1) + w_ref[0:1, :] * _shift_up(dac, 2)
        o_a[...] = da[0:tm].astype(o_a.dtype)
        dc = dac[0:tm]
        dw_ref[0:1, :] += _colsum(dc * a2[0:tm])
        dw_ref[1:2, :] += _colsum(dc * a1[0:tm])
        dw_ref[2:3, :] += _colsum(dc * a[0:tm])
        dw_ref[3:4, :] += _colsum(dc)

    r8 = tm // 8
    last8 = S // 8 - 1
    cur = lambda off: pl.BlockSpec((tm, tn), lambda j, i: (i, j + off))
    nxt = lambda off: pl.BlockSpec((8, tn), lambda j, i: (jnp.minimum((i + 1) * r8, last8), j + off))
    return pl.pallas_call(
        body, name="convffn_bwd", grid=(nj, ni),
        in_specs=[cur(0),
                  pl.BlockSpec((8, tn), lambda j, i: (jnp.maximum(i * r8 - 1, 0), j)),
                  nxt(0), cur(nj), nxt(nj), cur(0), nxt(0),
                  pl.BlockSpec((8, tn), lambda j, i: (0, j)),
                  pl.BlockSpec((1, tn), lambda j, i: (0, j))],
        out_specs=[cur(0), cur(0), pl.BlockSpec((8, tn), lambda j, i: (0, j))],
        out_shape=[jax.ShapeDtypeStruct((S, D_FF), BF16), jax.ShapeDtypeStruct((S, D_FF), BF16),
                   jax.ShapeDtypeStruct((8, D_FF), F32)],
        compiler_params=_params("parallel", "arbitrary"),
    )(hh, hh, hh, hh, hh, dact, dact, cw8, cb)


def ple_fwd(h, gpre, p, wpu):
    S, D = h.shape
    tm = min(ROW_TILE, S)

    def body(h_ref, g_ref, p_ref, w_ref, o_ref):
        o_ref[...] = h_ref[...] + _sigmoid(g_ref[...]) * _dot(p_ref[...], w_ref[...])

    return pl.pallas_call(
        body, name="ple_fwd", grid=(S // tm,),
        in_specs=[_row_spec(tm, D), _row_spec(tm, D), _row_spec(tm, PLE_DIM), _const_spec((PLE_DIM, D))],
        out_specs=_row_spec(tm, D),
        out_shape=jax.ShapeDtypeStruct((S, D), F32),
        compiler_params=_params("parallel"),
    )(h, gpre, p, wpu)


def ple_bwd(dh, gpre, p, wpu):
    S, D = dh.shape
    tm = min(ROW_TILE, S)

    def body(d_ref, g_ref, p_ref, w_ref, dpu_ref, dg_ref):
        d = d_ref[...]
        gate = _sigmoid(g_ref[...])
        pu = _dot(p_ref[...], w_ref[...])
        dpu_ref[...] = (d * gate).astype(dpu_ref.dtype)
        dg_ref[...] = (d * pu * gate * (1.0 - gate)).astype(dg_ref.dtype)

    return pl.pallas_call(
        body, name="ple_bwd", grid=(S // tm,),
        in_specs=[_row_spec(tm, D), _row_spec(tm, D), _row_spec(tm, PLE_DIM), _const_spec((PLE_DIM, D))],
        out_specs=[_row_spec(tm, D), _row_spec(tm, D)],
        out_shape=[jax.ShapeDtypeStruct((S, D), BF16)] * 2,
        compiler_params=_params("parallel"),
    )(dh, gpre, p, wpu)


def loss_head(h, g, tgt):
    S, D = h.shape
    tm = min(ROW_TILE, S)

    def body(h_ref, g_ref, t_ref, dh_ref, dhb_ref, l_ref, dg_ref):
        @pl.when(pl.program_id(0) == 0)
        def _():
            l_ref[...] = jnp.zeros_like(l_ref)
            dg_ref[...] = jnp.zeros_like(dg_ref)

        x = h_ref[...]
        gr = g_ref[...]
        rs = lax.rsqrt(jnp.mean(x * x, axis=1, keepdims=True) + EPS)
        xh = x * rs
        err = xh * gr - t_ref[...]
        l_ref[0:1, 0:1] += 0.5 * _colsum(jnp.mean(err * err, axis=1, keepdims=True))
        dy = err * (1.0 / D)
        dg_ref[0:1, :] += _colsum(dy * xh)
        dxh = dy * gr
        dh = rs * (dxh - xh * jnp.mean(dxh * xh, axis=1, keepdims=True))
        dh_ref[...] = dh
        dhb_ref[...] = dh.astype(BF16)

    return pl.pallas_call(
        body, name="loss_head", grid=(S // tm,),
        in_specs=[_row_spec(tm, D), _const_spec((1, D)), _row_spec(tm, D)],
        out_specs=[_row_spec(tm, D), _row_spec(tm, D), _const_spec((8, 128)), _const_spec((8, D))],
        out_shape=[jax.ShapeDtypeStruct((S, D), F32), jax.ShapeDtypeStruct((S, D), BF16),
                   jax.ShapeDtypeStruct((8, 128), F32), jax.ShapeDtypeStruct((8, D), F32)],
        compiler_params=_params("arbitrary"),
    )(h, g, tgt)


def _lb_rows(l_ref):
    l = l_ref[...]
    e = jnp.exp(l - jnp.max(l, axis=0, keepdims=True))
    p = e / _colsum(e)
    lbs, run = [], None
    for i in range(DEPTH):
        run = p[i:i + 1] if i == 0 else run + p[i:i + 1]
        lbs.append(run - p[0:1])
    return p, lbs


def lb_fwd(lb_logits):
    def body(l_ref, o_ref):
        _, lbs = _lb_rows(l_ref)
        o_ref[...] = jnp.zeros_like(o_ref)
        for i, lb in enumerate(lbs):
            o_ref[8 * i:8 * i + 1, :] = jnp.log(jnp.maximum(lb, LB_FLOOR))
            o_ref[8 * i + 1:8 * i + 2, :] = jnp.log1p(-lb)
            o_ref[8 * i + 2:8 * i + 3, :] = 1.0 - lb
            o_ref[8 * i + 3:8 * i + 4, :] = lb

    return pl.pallas_call(
        body, name="lb_fwd",
        out_shape=jax.ShapeDtypeStruct((DEPTH * 8, HG_WIDTH), F32),
    )(lb_logits)


def lb_bwd(dlbrows, lb_logits):
    def body(d_ref, l_ref, o_ref):
        p, lbs = _lb_rows(l_ref)
        dlb = []
        for i, lb in enumerate(lbs):
            da = d_ref[8 * i:8 * i + 1, :]
            dc = d_ref[8 * i + 1:8 * i + 2, :]
            do = d_ref[8 * i + 2:8 * i + 3, :]
            dlb.append(jnp.where(lb > LB_FLOOR, da / jnp.maximum(lb, LB_FLOOR), 0.0) - dc / (1.0 - lb) - do)
        dp = [jnp.zeros_like(dlb[0])]
        for j in range(1, DEPTH):
            acc = dlb[j]
            for i in range(j + 1, DEPTH):
                acc = acc + dlb[i]
            dp.append(acc)
        dot_ = p[0:1] * dp[0]
        for j in range(1, DEPTH):
            dot_ = dot_ + p[j:j + 1] * dp[j]
        o_ref[...] = jnp.zeros_like(o_ref)
        for j in range(DEPTH):
            o_ref[j:j + 1, :] = p[j:j + 1] * (dp[j] - dot_)

    return pl.pallas_call(
        body, name="lb_bwd",
        out_shape=jax.ShapeDtypeStruct((8, HG_WIDTH), F32),
    )(dlbrows, lb_logits)


def adamw(w, g, m, v, *, name):
    R, C = w.shape
    tr = next((t for t in (512, 256, 128, 64, 32, 16, 8) if R % t == 0), R)

    def body(w_ref, g_ref, m_ref, v_ref, d_ref, m2_ref, v2_ref):
        gv = g_ref[...]
        m2 = ADAM_B1 * m_ref[...] + (1.0 - ADAM_B1) * gv
        v2 = ADAM_B2 * v_ref[...] + (1.0 - ADAM_B2) * (gv * gv)
        mh = m2 / (1.0 - ADAM_B1 ** ADAM_STEP)
        vh = v2 / (1.0 - ADAM_B2 ** ADAM_STEP)
        d_ref[...] = -ADAM_LR * (mh / (jnp.sqrt(vh) + ADAM_EPS) + ADAM_WD * w_ref[...])
        m2_ref[...] = m2
        v2_ref[...] = v2

    spec = pl.BlockSpec((tr, C), lambda i: (i, 0))
    return pl.pallas_call(
        body, name=name, grid=(R // tr,),
        in_specs=[spec] * 4, out_specs=[spec] * 3,
        out_shape=[jax.ShapeDtypeStruct((R, C), F32)] * 3,
        compiler_params=_params("parallel"),
    )(w, g, m, v)


def sum_slots(x, *, out_dtype, name):
    n, R, C = x.shape
    tr = 848 if R % 848 == 0 else R

    def body(x_ref, o_ref):
        acc = x_ref[0].astype(F32)
        for k in range(1, n):
            acc = acc + x_ref[k].astype(F32)
        o_ref[...] = acc.astype(o_ref.dtype)

    return pl.pallas_call(
        body, name=name, grid=(R // tr,),
        in_specs=[pl.BlockSpec((n, tr, C), lambda i: (0, i, 0))],
        out_specs=pl.BlockSpec((tr, C), lambda i: (i, 0)),
        out_shape=jax.ShapeDtypeStruct((R, C), out_dtype),
        compiler_params=_params("parallel"),
    )(x)


MESH = pl.DeviceIdType.MESH
ANY = pl.BlockSpec(memory_space=pl.ANY)


def _place():
    return lax.axis_index("x"), lax.axis_index("y"), lax.axis_index("c")


def _other_chips(x, y):
    return [(1 - x, y), (x, 1 - y), (1 - x, 1 - y)]


def small_allgather(buf):
    R, C = buf.shape

    def body(x_ref, out_ref, send_sems, recv_sems, local_sem):
        x, y, c = _place()
        me, sibling = (x, y, c), (x, y, 1 - c)
        chips = _other_chips(x, y)

        def slot(px, py, pc):
            return out_ref.at[4 * px + 2 * py + pc]

        def copy(k, block, to, src=None):
            return pltpu.make_async_remote_copy(
                src_ref=slot(*block) if src is None else src, dst_ref=slot(*block),
                send_sem=send_sems.at[k], recv_sem=recv_sems.at[k],
                device_id=to, device_id_type=MESH)

        mine = pltpu.make_async_copy(x_ref, slot(*me), local_sem)
        mine.start()
        first = [copy(0, me, sibling, src=x_ref)]
        first += [copy(1 + r, me, (*chip, c), src=x_ref) for r, chip in enumerate(chips)]
        for cp in first:
            cp.start()
        passed = [copy(4 + r, (*chip, c), sibling) for r, chip in enumerate(chips)]
        for r, chip in enumerate(chips):
            copy(1 + r, (*chip, c), me).wait_recv()
            passed[r].start()
        copy(0, sibling, me).wait_recv()
        for r, chip in enumerate(chips):
            copy(4 + r, (*chip, 1 - c), me).wait_recv()
        for cp in first + passed:
            cp.wait_send()
        mine.wait()

    return pl.pallas_call(
        body, name="small_allgather",
        out_shape=jax.ShapeDtypeStruct((8, R, C), buf.dtype),
        in_specs=[pl.BlockSpec(memory_space=pltpu.VMEM)],
        out_specs=pl.BlockSpec(memory_space=pltpu.VMEM),
        scratch_shapes=[pltpu.SemaphoreType.DMA((7,)), pltpu.SemaphoreType.DMA((7,)),
                        pltpu.SemaphoreType.DMA],
    )(buf)


def weights_allgather(wp):
    R, C = wp.shape
    half = R // 2

    def body(w_ref, g_ref, send_sems, recv_sems, local_sem):
        x, y, c = _place()
        j = 2 * x + y
        sibling = (x, y, 1 - c)
        chips = _other_chips(x, y)
        mine = pl.ds(pl.multiple_of(c * half, 16), half)
        theirs = pl.ds(pl.multiple_of((1 - c) * half, 16), half)

        def copy(k, chip_block, rows, to, src=None):
            dst = g_ref.at[chip_block, rows]
            return pltpu.make_async_remote_copy(
                src_ref=dst if src is None else src, dst_ref=dst,
                send_sem=send_sems.at[k], recv_sem=recv_sems.at[k],
                device_id=to, device_id_type=MESH)

        own = pltpu.make_async_copy(w_ref, g_ref.at[j], local_sem)
        own.start()
        first = [copy(r, j, mine, (*chip, c), src=w_ref.at[mine]) for r, chip in enumerate(chips)]
        for cp in first:
            cp.start()
        passed = [copy(3 + r, 2 * chip[0] + chip[1], mine, sibling) for r, chip in enumerate(chips)]
        for r, chip in enumerate(chips):
            copy(r, 2 * chip[0] + chip[1], mine, (*chip, c)).wait_recv()
            passed[r].start()
        for r, chip in enumerate(chips):
            copy(3 + r, 2 * chip[0] + chip[1], theirs, sibling).wait_recv()
        for cp in first + passed:
            cp.wait_send()
        own.wait()

    return pl.pallas_call(
        body, name="weights_allgather",
        out_shape=jax.ShapeDtypeStruct((4, R, C), wp.dtype),
        in_specs=[ANY], out_specs=ANY,
        scratch_shapes=[pltpu.SemaphoreType.DMA((6,)), pltpu.SemaphoreType.DMA((6,)),
                        pltpu.SemaphoreType.DMA],
    )(wp)


def sibling_swap(v, *, name):
    def body(v_ref, got_ref, send_sem, recv_sem):
        x, y, c = _place()
        cp = pltpu.make_async_remote_copy(
            src_ref=v_ref, dst_ref=got_ref, send_sem=send_sem, recv_sem=recv_sem,
            device_id=(x, y, 1 - c), device_id_type=MESH)
        cp.start()
        cp.wait()

    return pl.pallas_call(
        body, name=name,
        out_shape=jax.ShapeDtypeStruct(v.shape, v.dtype),
        in_specs=[ANY], out_specs=ANY,
        scratch_shapes=[pltpu.SemaphoreType.DMA, pltpu.SemaphoreType.DMA],
    )(v)


def chip_exchange(q):
    def body(q_ref, r_ref, send_sems, recv_sems, local_sem):
        x, y, c = _place()
        j = 2 * x + y
        chips = _other_chips(x, y)
        own = pltpu.make_async_copy(q_ref.at[j], r_ref.at[j], local_sem)
        own.start()

        def copy(r, src_block, dst_block, chip):
            return pltpu.make_async_remote_copy(
                src_ref=q_ref.at[src_block], dst_ref=r_ref.at[dst_block],
                send_sem=send_sems.at[r], recv_sem=recv_sems.at[r],
                device_id=(*chip, c), device_id_type=MESH)

        sends = [copy(r, 2 * chip[0] + chip[1], j, chip) for r, chip in enumerate(chips)]
        for cp in sends:
            cp.start()
        for r, chip in enumerate(chips):
            jr = 2 * chip[0] + chip[1]
            copy(r, jr, jr, chip).wait_recv()
        for cp in sends:
            cp.wait_send()
        own.wait()

    return pl.pallas_call(
        body, name="chip_exchange",
        out_shape=jax.ShapeDtypeStruct(q.shape, q.dtype),
        in_specs=[ANY], out_specs=ANY,
        scratch_shapes=[pltpu.SemaphoreType.DMA((3,)), pltpu.SemaphoreType.DMA((3,)),
                        pltpu.SemaphoreType.DMA],
    )(q)


N_CHIPS = 4
_PACK = (("w_in", 704), ("w_out", 256), ("w_up", 1408), ("w_down", 704), ("w_ple_gate", 256), ("w_ple_up", 64))
LAYER_ROWS = sum(r for _, r in _PACK)
PACK_ROWS = DEPTH * LAYER_ROWS
HALF_ROWS = PACK_ROWS // 2


def _pack_shards(sh):
    parts = []
    for i in range(DEPTH):
        for name, rows in _PACK:
            parts.append(sh[name][i].reshape(rows, D_MODEL))
    return jnp.concatenate(parts, axis=0)


def _unpack_shards(slab):
    shapes = {"w_in": (D_MODEL, IN_WIDTH // N_CHIPS), "w_out": (D_MODEL // N_CHIPS, D_MODEL),
              "w_up": (D_MODEL, 2 * D_FF // N_CHIPS), "w_down": (D_FF // N_CHIPS, D_MODEL),
              "w_ple_gate": (D_MODEL // N_CHIPS, D_MODEL), "w_ple_up": (PLE_DIM, D_MODEL // N_CHIPS)}
    out = {name: [] for name, _ in _PACK}
    off = 0
    for i in range(DEPTH):
        for name, rows in _PACK:
            out[name].append(slab[off:off + rows].reshape(shapes[name]))
            off += rows
    return {k: jnp.stack(v) for k, v in out.items()}


_COL_SHARDED = ("w_in", "w_up", "w_ple_up")


def _full_from_chips(g, layer):
    per_chip = [_unpack_shards_layer(g[k], layer) for k in range(N_CHIPS)]
    return {name: jnp.concatenate([pc[name] for pc in per_chip], axis=1 if name in _COL_SHARDED else 0)
            for name, _ in _PACK}


def _unpack_shards_layer(slab, layer):
    shapes = {"w_in": (D_MODEL, IN_WIDTH // N_CHIPS), "w_out": (D_MODEL // N_CHIPS, D_MODEL),
              "w_up": (D_MODEL, 2 * D_FF // N_CHIPS), "w_down": (D_FF // N_CHIPS, D_MODEL),
              "w_ple_gate": (D_MODEL // N_CHIPS, D_MODEL), "w_ple_up": (PLE_DIM, D_MODEL // N_CHIPS)}
    out = {}
    off = layer * LAYER_ROWS
    for name, rows in _PACK:
        out[name] = slab[off:off + rows].reshape(shapes[name])
        off += rows
    return out


def _split_to_chips(full, name):
    r, c = full.shape
    if name in _COL_SHARDED:
        full = full.reshape(r, N_CHIPS, c // N_CHIPS).transpose(1, 0, 2)
    return full.reshape(N_CHIPS, -1, D_MODEL)


_SMALL = (("loss", 128), ("g_final", 1024), ("g_mix", 4096), ("lb_logits", 2048), ("hg_norm_g", 2048),
          ("attn_sinks", 128), ("g_ffn", 4096), ("conv_w", 4 * 3 * D_FF), ("conv_b", 4 * D_FF), ("g_ple", 4096))
SMALL_ROWS = 496


def _pack_small(d):
    parts = []
    for name, n in _SMALL:
        v = d[name].reshape(-1).astype(F32)
        parts.append(jnp.pad(v, (0, n - v.shape[0])))
    flat = jnp.concatenate(parts)
    return jnp.pad(flat, (0, SMALL_ROWS * 128 - flat.shape[0])).reshape(SMALL_ROWS, 128)


def _unpack_small(buf, shapes):
    flat = buf.reshape(-1)
    out, off = {}, 0
    for name, n in _SMALL:
        size = 1
        for s in shapes[name]:
            size *= s
        out[name] = flat[off:off + size].reshape(shapes[name])
        off += n
    return out


WEIGHT_ORDER = ('g_mix', 'w_in', 'lb_logits', 'hg_norm_g', 'attn_sinks', 'w_out', 'g_ffn', 'w_up', 'conv_w',
                'conv_b', 'w_down', 'g_ple', 'w_ple_gate', 'w_ple_up', 'g_final')


def kernel(x, p, g_mix, w_in, lb_logits, hg_norm_g, attn_sinks, w_out, g_ffn, w_up, conv_w, conv_b, w_down, g_ple, w_ple_gate, w_ple_up, g_final, loss_target, m_g_mix, m_w_in, m_lb_logits, m_hg_norm_g, m_attn_sinks, m_w_out, m_g_ffn, m_w_up, m_conv_w, m_conv_b, m_w_down, m_g_ple, m_w_ple_gate, m_w_ple_up, m_g_final, v_g_mix, v_w_in, v_lb_logits, v_hg_norm_g, v_attn_sinks, v_w_out, v_g_ffn, v_w_up, v_conv_w, v_conv_b, v_w_down, v_g_ple, v_w_ple_gate, v_w_ple_up, v_g_final):
    W = dict(g_mix=g_mix, w_in=w_in, lb_logits=lb_logits, hg_norm_g=hg_norm_g, attn_sinks=attn_sinks,
             w_out=w_out, g_ffn=g_ffn, w_up=w_up, conv_w=conv_w, conv_b=conv_b, w_down=w_down, g_ple=g_ple,
             w_ple_gate=w_ple_gate, w_ple_up=w_ple_up, g_final=g_final)
    M = dict(g_mix=m_g_mix, w_in=m_w_in, lb_logits=m_lb_logits, hg_norm_g=m_hg_norm_g, attn_sinks=m_attn_sinks,
             w_out=m_w_out, g_ffn=m_g_ffn, w_up=m_w_up, conv_w=m_conv_w, conv_b=m_conv_b, w_down=m_w_down,
             g_ple=m_g_ple, w_ple_gate=m_w_ple_gate, w_ple_up=m_w_ple_up, g_final=m_g_final)
    V = dict(g_mix=v_g_mix, w_in=v_w_in, lb_logits=v_lb_logits, hg_norm_g=v_hg_norm_g, attn_sinks=v_attn_sinks,
             w_out=v_w_out, g_ffn=v_g_ffn, w_up=v_w_up, conv_w=v_conv_w, conv_b=v_conv_b, w_down=v_w_down,
             g_ple=v_g_ple, w_ple_gate=v_w_ple_gate, w_ple_up=v_w_ple_up, g_final=v_g_final)
    S = x.shape[1]
    hg_rows = min(ROW_TILE, S)
    xi, yi, ci = _place()
    chip = 2 * xi + yi

    slab = _pack_shards({n: W[n] for n, _ in _PACK}).astype(BF16)
    gathered = weights_allgather(slab)
    cw_shard = jnp.pad(conv_w.reshape(-1), (0, 72 * 128 - conv_w.size)).reshape(72, 128)
    cw_all = small_allgather(cw_shard)
    cw_full = jnp.concatenate(
        [cw_all[2 * k].reshape(-1)[:conv_w.size].reshape(conv_w.shape) for k in range(N_CHIPS)], axis=2)
    lbrows = lb_fwd(lb_logits)

    h = x[0]
    saved = []
    for i in range(DEPTH):
        wf = _full_from_chips(gathered, i)
        lbr = lbrows[8 * i:8 * i + 8]
        ng = hg_norm_g[i][None]
        sinks_b = jnp.pad(jnp.repeat(attn_sinks[i].reshape(AT_KV_HEADS, 1, AT_GROUP), WINDOW, axis=2),
                          ((0, 0), (0, 7), (0, 0))).reshape(8 * AT_KV_HEADS, GROUP_LANES)
        cw8 = jnp.pad(cw_full[i], ((0, 5), (0, 0)))
        cb = conv_b[i][None]
        u = rmsnorm_fwd(h, g_mix[i][None], name="rmsnorm_fwd")
        proj = mm(u, wf["w_in"], name="mm_in")
        y, o_raw, states = hgrn_fwd(proj, lbr, ng, D_MODEL, rows=hg_rows)
        y = swa_fwd(proj, sinks_b, y)
        h1 = mm(y, wf["w_out"], res=h, name="mm_out")
        u2 = rmsnorm_fwd(h1, g_ffn[i][None], name="rmsnorm_fwd")
        hh = mm(u2, wf["w_up"], name="mm_up")
        act = convffn_fwd(hh, cw8, cb)
        h2 = mm(act, wf["w_down"], res=h1, name="mm_down")
        u3 = rmsnorm_fwd(h2, g_ple[i][None], name="rmsnorm_fwd")
        gpre = mm(u3, wf["w_ple_gate"], name="mm_gate")
        h3 = ple_fwd(h2, gpre, p[i, 0], wf["w_ple_up"])
        saved.append(dict(wf=wf, lbr=lbr, ng=ng, sinks_b=sinks_b, cw8=cw8, cb=cb, h=h, u=u, proj=proj,
                          o_raw=o_raw, states=states, y=y, h1=h1, u2=u2, hh=hh, act=act, h2=h2, u3=u3,
                          gpre=gpre))
        h = h3

    dh, dhb, loss_acc, dg_final = loss_head(h, g_final[None], loss_target[0])

    gfull = {n: [None] * DEPTH for n, _ in _PACK}
    gsmall = {n: [None] * DEPTH for n in ("g_mix", "hg_norm_g", "attn_sinks", "g_ffn", "conv_w", "conv_b", "g_ple")}
    dlbrows = [None] * DEPTH
    for i in reversed(range(DEPTH)):
        s = saved[i]
        wf = s["wf"]
        dpu, dgp = ple_bwd(dh, s["gpre"], p[i, 0], wf["w_ple_up"])
        gfull["w_ple_up"][i] = mm_tn(p[i, 0], dpu, name="mm_tn_pu")
        gfull["w_ple_gate"][i] = mm_tn(s["u3"], dgp, name="mm_tn_gate")
        du3 = mm(dgp, wf["w_ple_gate"], nt=True, name="mm_nt_gate")
        dh2, dh2b, dg = rmsnorm_bwd(du3, s["h2"], g_ple[i][None], dh, name="rmsnorm_bwd")
        gsmall["g_ple"][i] = dg[0]
        gfull["w_down"][i] = mm_tn(s["act"], dh2b, name="mm_tn_down")
        dact = mm(dh2b, wf["w_down"], nt=True, name="mm_nt_down")
        da, db, dcw = convffn_bwd(s["hh"], dact, s["cw8"], s["cb"])
        gsmall["conv_w"][i] = dcw[0:3]
        gsmall["conv_b"][i] = dcw[3]
        dhh = jnp.concatenate([da, db], axis=1)
        gfull["w_up"][i] = mm_tn(s["u2"], dhh, name="mm_tn_up")
        du2 = mm(dhh, wf["w_up"], nt=True, name="mm_nt_up")
        dh1, dh1b, dg = rmsnorm_bwd(du2, s["h1"], g_ffn[i][None], dh2, name="rmsnorm_bwd")
        gsmall["g_ffn"][i] = dg[0]
        gfull["w_out"][i] = mm_tn(s["y"], dh1b, name="mm_tn_out")
        dy = mm(dh1b, wf["w_out"], nt=True, name="mm_nt_out")
        dq_at, dko, dkp, dvo, dvp, dsk = swa_bwd(s["proj"], s["sinks_b"], dy)
        gsmall["attn_sinks"][i] = dsk.reshape(AT_KV_HEADS, 8, GROUP_LANES)[:, 1:1 + AT_GROUP, 0].reshape(-1)
        hq, hz, hv, hgp, dlbr, dng = hgrn_bwd(s["proj"], s["o_raw"], s["states"], dy, s["lbr"], s["ng"],
                                              rows=hg_rows)
        dlbrows[i] = dlbr
        gsmall["hg_norm_g"][i] = dng[0]
        dproj = assemble_dproj((hq, hz, hv, hgp), dq_at, dko, dkp, dvo, dvp, rows=hg_rows)
        gfull["w_in"][i] = mm_tn(s["u"], dproj, name="mm_tn_in")
        du = mm(dproj, wf["w_in"], nt=True, name="mm_nt_in")
        dh, dhb, dg = rmsnorm_bwd(du, s["h"], g_mix[i][None], dh1, name="rmsnorm_bwd")
        gsmall["g_mix"][i] = dg[0]
    grad_x = dh[None]
    dlb_logits = lb_bwd(jnp.concatenate(dlbrows, axis=0), lb_logits)[0:DEPTH]

    parts = []
    for i in range(DEPTH):
        for name, _ in _PACK:
            parts.append(_split_to_chips(gfull[name][i], name))
    pk = jnp.concatenate(parts, axis=1).astype(BF16)
    pk = pk.reshape(N_CHIPS, 2, HALF_ROWS, D_MODEL)
    p_mine = lax.dynamic_index_in_dim(pk, ci, axis=1, keepdims=False)
    p_other = lax.dynamic_index_in_dim(pk, 1 - ci, axis=1, keepdims=False)
    from_sib = sibling_swap(p_other, name="sibling_swap_partials")
    pair = sum_slots(jnp.stack([p_mine.reshape(-1, D_MODEL), from_sib.reshape(-1, D_MODEL)]),
                     out_dtype=BF16, name="sum_pair")
    from_chips = chip_exchange(pair.reshape(N_CHIPS, HALF_ROWS, D_MODEL))
    mine_sum = sum_slots(from_chips, out_dtype=F32, name="sum_chips")
    sib_sum = sibling_swap(mine_sum, name="sibling_swap_sums")
    lo = jnp.where(ci == 0, mine_sum, sib_sum)
    hi = jnp.where(ci == 0, sib_sum, mine_sum)
    gshard = _unpack_shards(jnp.concatenate([lo, hi], axis=0))

    small = dict(loss=loss_acc[0, 0:1], g_final=dg_final[0], lb_logits=dlb_logits,
                 **{n: jnp.stack(v) for n, v in gsmall.items()})
    small_sum = sum_slots(small_allgather(_pack_small(small)), out_dtype=F32, name="sum_small")
    shapes = {n: W[n].shape for n in W}
    shapes["loss"] = (1,)
    shapes["conv_w"] = (DEPTH, 3, D_FF)
    gs = _unpack_small(small_sum, shapes)
    loss = gs["loss"][0]
    cshard = conv_w.shape[2]
    grads = dict(gshard)
    for n in ("g_mix", "lb_logits", "hg_norm_g", "attn_sinks", "g_ffn", "conv_b", "g_ple", "g_final"):
        grads[n] = gs[n]
    grads["conv_w"] = lax.dynamic_slice_in_dim(gs["conv_w"], chip * cshard, cshard, axis=2)

    delta, new_m, new_v = {}, {}, {}
    small_names = ("g_final", "g_mix", "lb_logits", "hg_norm_g", "attn_sinks", "g_ffn", "conv_b", "g_ple")
    sshapes = {n: W[n].shape for n in small_names}

    def pack_s(d):
        z = dict(d)
        z["loss"] = jnp.zeros((1,), F32)
        z["conv_w"] = jnp.zeros((1,), F32)
        return _pack_small(z)

    sd, sm, sv = adamw(pack_s(W), pack_s(grads), pack_s(M), pack_s(V), name="adamw_small")
    for out, buf in ((delta, sd), (new_m, sm), (new_v, sv)):
        un = _unpack_small(buf, {**sshapes, "loss": (1,), "conv_w": (1,)})
        for n in small_names:
            out[n] = un[n]
    for n in ("w_in", "w_out", "w_up", "w_down", "w_ple_gate", "w_ple_up", "conv_w"):
        shp = W[n].shape
        two_d = (-1, shp[-1])
        d_, m_, v_ = adamw(W[n].reshape(two_d), grads[n].reshape(two_d), M[n].reshape(two_d),
                           V[n].reshape(two_d), name="adamw_" + n)
        delta[n], new_m[n], new_v[n] = d_.reshape(shp), m_.reshape(shp), v_.reshape(shp)

    return (loss, grad_x, *[grads[n] for n in WEIGHT_ORDER], *[delta[n] for n in WEIGHT_ORDER],
            *[new_m[n] for n in WEIGHT_ORDER], *[new_v[n] for n in WEIGHT_ORDER])
```

```python
import functools

import jax
import jax.numpy as jnp
from jax import lax
from jax.experimental import pallas as pl
from jax.experimental.pallas import tpu as pltpu

F32 = jnp.float32
BF16 = jnp.bfloat16

D_MODEL = 1024
DEPTH = 4
PLE_DIM = 256
HG_WIDTH = 512
HG_HEADS = 4
HG_DK = 128
HG_CHUNK = 64
HG_SUB = 16
AT_WIDTH = 512
AT_HEAD_DIM = 64
AT_Q_HEADS = 8
AT_KV_HEADS = 2
AT_GROUP = 4
WINDOW = 128
D_FF = 2816
IN_WIDTH = 2816
EPS = 1e-6
MASK_VALUE = -1e30
LB_FLOOR = 1e-30

ADAM_LR = 0.001
ADAM_B1 = 0.9
ADAM_B2 = 0.999
ADAM_EPS = 1e-08
ADAM_WD = 0.01
ADAM_STEP = 10

VMEM_LIMIT = 48 * 1024 * 1024


def _params(*sem):
    return pltpu.CompilerParams(dimension_semantics=sem, vmem_limit_bytes=VMEM_LIMIT)


def _dot(a, b, dims=(((1,), (0,)), ((), ()))):
    return lax.dot_general(a.astype(BF16), b.astype(BF16), dims, preferred_element_type=F32)


def _dot_nt(a, b):
    return _dot(a, b, (((1,), (1,)), ((), ())))


def _dot_tn(a, b):
    return _dot(a, b, (((0,), (0,)), ((), ())))


def _dot_f32(a, b, dims=(((1,), (0,)), ((), ()))):
    return lax.dot_general(a, b, dims, preferred_element_type=F32, precision=lax.Precision.HIGHEST)


def _sigmoid(x):
    return 1.0 / (1.0 + jnp.exp(-x))


def _logsig(x):
    return jnp.minimum(x, 0.0) - jnp.log(1.0 + jnp.exp(-jnp.abs(x)))


def _colsum(x):
    return jnp.sum(x, axis=0, keepdims=True)


def _rowsum(x):
    return jnp.sum(x, axis=1, keepdims=True)


def _tri(n):
    r = lax.broadcasted_iota(jnp.int32, (n, n), 0)
    c = lax.broadcasted_iota(jnp.int32, (n, n), 1)
    return (r >= c).astype(F32)


def _hg_gates(qp, z, a, c, oml):
    sq = _sigmoid(qp)
    q = qp * sq
    t = c + _logsig(z)
    mx = jnp.maximum(a, t)
    logf = mx + jnp.log(1.0 + jnp.exp(-jnp.abs(a - t)))
    snz = _sigmoid(-z)
    k = oml * snz
    return q, sq, t, logf, snz, k


_HEADS = range(HG_HEADS)


def _lanes(h):
    return slice(h * HG_DK, (h + 1) * HG_DK)


def _head(x, h):
    return x[:, _lanes(h)]


def _hg_chunk_fwd(q, k, v, logf, st, b_s, k_s, v_s):
    C, U = HG_CHUNK, HG_SUB
    tri = _tri(C)
    b = [_dot_f32(tri, logf[h]) for h in _HEADS]
    for h in _HEADS:
        b_s[h] = b[h]
        k_s[h] = k[h]
        v_s[h] = v[h]
    o = [_dot_nt(q[h] * jnp.exp(b[h]), st[h]) for h in _HEADS]
    bl = [b[h][C - 1:C] for h in _HEADS]
    upd = [_dot_tn(v[h], k[h] * jnp.exp(bl[h] - b[h])) for h in _HEADS]
    rows = lax.broadcasted_iota(jnp.int32, (C, HG_DK), 0)
    trow = lax.broadcasted_iota(jnp.int32, (U, HG_DK), 0)
    outs = [[] for _ in _HEADS]
    for i in range(C // U):
        lo = i * U
        b_i = [b[h][lo:lo + U] for h in _HEADS]
        q_i = [q[h][lo:lo + U] for h in _HEADS]
        o_i = [o[h][lo:lo + U] for h in _HEADS]
        if i > 0:
            qe = [q_i[h] * jnp.exp(b_i[h] - b_i[h][0:1]) for h in _HEADS]
            ke = [jnp.where(rows < lo, k[h] * jnp.exp(jnp.minimum(b_i[h][0:1] - b[h], 0.0)), 0.0) for h in _HEADS]
            att = [_dot_nt(qe[h], ke[h]) for h in _HEADS]
            off = [_dot(att[h], v[h]) for h in _HEADS]
            o_i = [o_i[h] + off[h] for h in _HEADS]
        for s in range(U):
            m = trow >= s
            for h in _HEADS:
                bs = b_s[h, lo + s:lo + s + 1, :]
                ks = k_s[h, lo + s:lo + s + 1, :]
                vs = v_s[h, lo + s:lo + s + 1, :]
                dec = jnp.where(m, jnp.exp(jnp.where(m, b_i[h] - bs, 0.0)), 0.0)
                w = _rowsum(q_i[h] * dec * ks)
                o_i[h] = o_i[h] + w * vs
        for h in _HEADS:
            outs[h].append(o_i[h])
    o = [jnp.concatenate(outs[h], axis=0) for h in _HEADS]
    st_new = [st[h] * jnp.exp(bl[h]) + upd[h] for h in _HEADS]
    return o, st_new, b


def _hg_post(o, gp, ng):
    rs = lax.rsqrt(jnp.mean(o * o, axis=1, keepdims=True) + EPS)
    sg = _sigmoid(gp)
    return o * rs * ng * sg, rs, sg


def hgrn_fwd(proj, lbrows, ng, y_width, *, rows):
    S = proj.shape[0]
    C = HG_CHUNK
    cpb = rows // C
    nb = S // rows

    def body(qp_ref, z_ref, v_ref, gp_ref, lb_ref, ng_ref, y_ref, o_ref, st_ref, st, b_s, k_s, v_s):
        @pl.when(pl.program_id(0) == 0)
        def _():
            st[...] = jnp.zeros_like(st)

        a, c, oml = lb_ref[0:1, :], lb_ref[1:2, :], lb_ref[2:3, :]
        ngr = ng_ref[...]

        def chunk(ci, carry):
            off = pl.multiple_of(ci * C, C)
            sl = pl.ds(off, C)
            for h in _HEADS:
                st_ref[h, ci] = st[h]
            gates = [_hg_gates(qp_ref[sl, _lanes(h)], z_ref[sl, _lanes(h)],
                               _head(a, h), _head(c, h), _head(oml, h)) for h in _HEADS]
            q = [g[0] for g in gates]
            logf = [g[3] for g in gates]
            k = [g[5] for g in gates]
            v = [v_ref[sl, _lanes(h)] for h in _HEADS]
            o, st_new, _ = _hg_chunk_fwd(q, k, v, logf, [st[h] for h in _HEADS], b_s, k_s, v_s)
            for h in _HEADS:
                y, _, _ = _hg_post(o[h], gp_ref[sl, _lanes(h)], _head(ngr, h))
                y_ref[sl, _lanes(h)] = y.astype(y_ref.dtype)
                o_ref[sl, _lanes(h)] = o[h]
                st[h] = st_new[h]
            return carry

        lax.fori_loop(0, cpb, chunk, 0)

    col = lambda kblk: pl.BlockSpec((rows, HG_WIDTH), lambda r: (r, kblk))
    return pl.pallas_call(
        body,
        name="hgrn_fwd",
        grid=(nb,),
        in_specs=[col(0), col(1), col(2), col(3), _const_spec((8, HG_WIDTH)), _const_spec((1, HG_WIDTH))],
        out_specs=[col(0), col(0),
                   pl.BlockSpec((HG_HEADS, cpb, HG_DK, HG_DK), lambda r: (0, r, 0, 0))],
        out_shape=[jax.ShapeDtypeStruct((S, y_width), BF16),
                   jax.ShapeDtypeStruct((S, HG_WIDTH), F32),
                   jax.ShapeDtypeStruct((HG_HEADS, S // C, HG_DK, HG_DK), F32)],
        scratch_shapes=[pltpu.VMEM((HG_HEADS, HG_DK, HG_DK), F32)] + [pltpu.VMEM((HG_HEADS, C, HG_DK), F32)] * 3,
        compiler_params=_params("arbitrary"),
    )(proj, proj, proj, proj, lbrows, ng)


def hgrn_bwd(proj, o_raw, states, dy, lbrows, ng, *, rows):
    S = proj.shape[0]
    C, U = HG_CHUNK, HG_SUB
    cpb = rows // C
    nb = S // rows

    def body(qp_ref, z_ref, v_ref, gp_ref, o_ref, st_ref, dy_ref, lb_ref, ng_ref,
             dqp_ref, dz_ref, dv_ref, dgp_ref, dlb_ref, dng_ref,
             dst, b_s, k_s, v_s, dbs, dks, dvs):
        @pl.when(pl.program_id(0) == 0)
        def _():
            dst[...] = jnp.zeros_like(dst)
            dlb_ref[...] = jnp.zeros_like(dlb_ref)
            dng_ref[...] = jnp.zeros_like(dng_ref)

        a, c, oml = lb_ref[0:1, :], lb_ref[1:2, :], lb_ref[2:3, :]
        ngr = ng_ref[...]
        rows_i = lax.broadcasted_iota(jnp.int32, (C, HG_DK), 0)
        trow = lax.broadcasted_iota(jnp.int32, (U, HG_DK), 0)
        tri = _tri(C)
        H = _HEADS

        def chunk(cj, carry):
            ci = cpb - 1 - cj
            off = pl.multiple_of(ci * C, C)
            sl = pl.ds(off, C)
            qp = [qp_ref[sl, _lanes(h)] for h in H]
            v = [v_ref[sl, _lanes(h)] for h in H]
            st = [st_ref[h, ci] for h in H]
            gates = [_hg_gates(qp[h], z_ref[sl, _lanes(h)], _head(a, h), _head(c, h), _head(oml, h)) for h in H]
            q, sq, t, logf, snz, k = ([g[j] for g in gates] for j in range(6))
            b = [_dot_f32(tri, logf[h]) for h in H]
            for h in H:
                b_s[h] = b[h]
                k_s[h] = k[h]
                v_s[h] = v[h]
            do = []
            for h in H:
                o = o_ref[sl, _lanes(h)]
                dyv = dy_ref[sl, _lanes(h)]
                ngh = _head(ngr, h)
                rs = lax.rsqrt(jnp.mean(o * o, axis=1, keepdims=True) + EPS)
                sg = _sigmoid(gp_ref[sl, _lanes(h)])
                xh = o * rs
                dgp_ref[sl, _lanes(h)] = (dyv * xh * ngh * sg * (1.0 - sg)).astype(dgp_ref.dtype)
                don = dyv * sg
                dng_ref[0:1, _lanes(h)] += _colsum(don * xh)
                dxh = don * ngh
                do.append(rs * (dxh - xh * jnp.mean(dxh * xh, axis=1, keepdims=True)))
            eb = [jnp.exp(b[h]) for h in H]
            qb = [q[h] * eb[h] for h in H]
            dstv = [dst[h] for h in H]
            bl = [b[h][C - 1:C] for h in H]
            el = [jnp.exp(bl[h]) for h in H]
            ex = [jnp.exp(bl[h] - b[h]) for h in H]
            kd = [k[h] * ex[h] for h in H]
            dqb = [_dot(do[h], st[h]) for h in H]
            dst_acc = [_dot_tn(do[h], qb[h]) for h in H]
            dv0 = [_dot_nt(kd[h], dstv[h]) for h in H]
            dkd = [_dot(v[h], dstv[h]) for h in H]
            dq = [dqb[h] * eb[h] for h in H]
            for h in H:
                g2 = dkd[h] * kd[h]
                dbl = _colsum(dstv[h] * st[h]) * el[h] + _colsum(g2)
                dst[h] = dstv[h] * el[h] + dst_acc[h]
                dbs[h] = dqb[h] * qb[h] - g2
                dks[h] = dkd[h] * ex[h]
                dvs[h] = dv0[h]
                dbs[h, C - 1:C, :] += dbl
            dq_parts = [[] for _ in H]
            for i in range(C // U):
                lo = i * U
                b_i = [b[h][lo:lo + U] for h in H]
                q_i = [q[h][lo:lo + U] for h in H]
                do_i = [do[h][lo:lo + U] for h in H]
                dq_i = [dq[h][lo:lo + U] for h in H]
                db_i = [jnp.zeros((U, HG_DK), F32) for _ in H]
                if i > 0:
                    e1 = [jnp.exp(b_i[h] - b_i[h][0:1]) for h in H]
                    qe = [q_i[h] * e1[h] for h in H]
                    e2 = [jnp.where(rows_i < lo, jnp.exp(jnp.minimum(b_i[h][0:1] - b[h], 0.0)), 0.0) for h in H]
                    ke = [k[h] * e2[h] for h in H]
                    att = [_dot_nt(qe[h], ke[h]) for h in H]
                    datt = [_dot_nt(do_i[h], v[h]) for h in H]
                    dv_add = [_dot_tn(att[h], do_i[h]) for h in H]
                    dqe = [_dot(datt[h], ke[h]) for h in H]
                    dke = [_dot_tn(datt[h], qe[h]) for h in H]
                    for h in H:
                        dvs[h] += dv_add[h]
                        dq_i[h] = dq_i[h] + dqe[h] * e1[h]
                        g = dqe[h] * qe[h]
                        db_i[h] = db_i[h] + g
                        gk = dke[h] * ke[h]
                        dks[h] += dke[h] * e2[h]
                        dbs[h] -= gk
                        dbs[h, lo:lo + 1, :] += _colsum(gk) - _colsum(g)
                for s in range(U):
                    row = slice(lo + s, lo + s + 1)
                    m = trow >= s
                    for h in H:
                        bs = b_s[h, row, :]
                        ks = k_s[h, row, :]
                        vs = v_s[h, row, :]
                        dec = jnp.where(m, jnp.exp(jnp.where(m, b_i[h] - bs, 0.0)), 0.0)
                        qd = q_i[h] * dec
                        y_ = qd * ks
                        w = _rowsum(y_)
                        dw = _rowsum(do_i[h] * vs)
                        dvs[h, row, :] += _colsum(w * do_i[h])
                        dq_i[h] = dq_i[h] + dw * dec * ks
                        dks[h, row, :] += _colsum(dw * qd)
                        g = dw * y_
                        db_i[h] = db_i[h] + g
                        dbs[h, row, :] -= _colsum(g)
                for h in H:
                    dbs[h, lo:lo + U, :] += db_i[h]
                    dq_parts[h].append(dq_i[h])
            dlogf = [_dot_f32(tri, dbs[h], (((0,), (0,)), ((), ()))) for h in H]
            for h in H:
                dqh = jnp.concatenate(dq_parts[h], axis=0)
                dk = dks[h]
                ah, omlh = _head(a, h), _head(oml, h)
                pa = jnp.exp(ah - logf[h])
                pt = jnp.exp(t[h] - logf[h])
                dt = dlogf[h] * pt
                dlb_ref[0:1, _lanes(h)] += _colsum(dlogf[h] * pa)
                dlb_ref[1:2, _lanes(h)] += _colsum(dt)
                dlb_ref[2:3, _lanes(h)] += _colsum(dk * snz[h])
                dz = dt * snz[h] - dk * omlh * snz[h] * (1.0 - snz[h])
                dqp = dqh * (sq[h] + qp[h] * sq[h] * (1.0 - sq[h]))
                dqp_ref[sl, _lanes(h)] = dqp.astype(dqp_ref.dtype)
                dz_ref[sl, _lanes(h)] = dz.astype(dz_ref.dtype)
                dv_ref[sl, _lanes(h)] = dvs[h].astype(dv_ref.dtype)
            return carry

        lax.fori_loop(0, cpb, chunk, 0)

    rev = lambda r: nb - 1 - r
    col = lambda kblk: pl.BlockSpec((rows, HG_WIDTH), lambda r: (rev(r), kblk))
    acc = _const_spec((8, HG_WIDTH))
    return pl.pallas_call(
        body,
        name="hgrn_bwd",
        grid=(nb,),
        in_specs=[col(0), col(1), col(2), col(3), col(0),
                  pl.BlockSpec((HG_HEADS, cpb, HG_DK, HG_DK), lambda r: (0, rev(r), 0, 0)),
                  col(0), acc, _const_spec((1, HG_WIDTH))],
        out_specs=[col(0)] * 4 + [acc, acc],
        out_shape=[jax.ShapeDtypeStruct((S, HG_WIDTH), BF16)] * 4
                  + [jax.ShapeDtypeStruct((8, HG_WIDTH), F32)] * 2,
        scratch_shapes=[pltpu.VMEM((HG_HEADS, HG_DK, HG_DK), F32)] + [pltpu.VMEM((HG_HEADS, C, HG_DK), F32)] * 6,
        compiler_params=_params("arbitrary"),
    )(proj, proj, proj, proj, o_raw, states, dy, lbrows, ng)


GROUP_LANES = AT_GROUP * WINDOW


def _swa_valid_t(n):
    W = WINDOW
    kpos = lax.broadcasted_iota(jnp.int32, (2 * W, GROUP_LANES), 0)
    qpos = (lax.broadcasted_iota(jnp.int32, (2 * W, GROUP_LANES), 1) & (W - 1)) + W
    rel = qpos - kpos
    return (rel >= 0) & (rel < W) & jnp.logical_not((n == 0) & (kpos < W))


def _group_lanes(xt, g):
    Dh = AT_HEAD_DIM
    return jnp.concatenate([xt[(g * AT_GROUP + j) * Dh:(g * AT_GROUP + j + 1) * Dh] for j in range(AT_GROUP)],
                           axis=1)


def _swa_probs_t(kg, qg, sink_row, valid):
    s = _dot(kg, qg) * (AT_HEAD_DIM ** -0.5)
    s = jnp.where(valid, s, MASK_VALUE)
    m = jnp.maximum(jnp.max(s, axis=0, keepdims=True), sink_row)
    e = jnp.exp(s - m)
    es = jnp.exp(sink_row - m)
    inv = 1.0 / (_colsum(e) + es)
    return e * inv, es * inv


def swa_fwd(proj, sink_rows, y):
    S = proj.shape[0]
    W, Dh = WINDOW, AT_HEAD_DIM
    nb = S // W

    def body(q_ref, kp_ref, k_ref, vp_ref, v_ref, sk_ref, y_in, y_ref):
        del y_in
        valid = _swa_valid_t(pl.program_id(0))
        kk = jnp.concatenate([kp_ref[...], k_ref[...]], axis=0)
        vt = jnp.concatenate([vp_ref[...], v_ref[...]], axis=0).T
        qt = q_ref[...].T
        outs = []
        for g in range(AT_KV_HEADS):
            p, _ = _swa_probs_t(kk[:, g * Dh:(g + 1) * Dh], _group_lanes(qt, g),
                                sk_ref[8 * g:8 * g + 1, :], valid)
            ot = _dot(vt[g * Dh:(g + 1) * Dh], p)
            outs += [ot[:, j * W:(j + 1) * W] for j in range(AT_GROUP)]
        y_ref[...] = jnp.concatenate(outs, axis=0).T.astype(y_ref.dtype)

    prev = lambda n: jnp.maximum(n - 1, 0)
    return pl.pallas_call(
        body,
        name="swa_fwd",
        grid=(nb,),
        in_specs=[pl.BlockSpec((W, AT_WIDTH), lambda n: (n, 4)),
                  pl.BlockSpec((W, 128), lambda n: (prev(n), 20)),
                  pl.BlockSpec((W, 128), lambda n: (n, 20)),
                  pl.BlockSpec((W, 128), lambda n: (prev(n), 21)),
                  pl.BlockSpec((W, 128), lambda n: (n, 21)),
                  pl.BlockSpec((8 * AT_KV_HEADS, GROUP_LANES), lambda n: (0, 0)),
                  pl.BlockSpec(memory_space=pl.ANY)],
        out_specs=pl.BlockSpec((W, AT_WIDTH), lambda n: (n, 1)),
        out_shape=jax.ShapeDtypeStruct(y.shape, y.dtype),
        input_output_aliases={6: 0},
        compiler_params=_params("parallel"),
    )(proj, proj, proj, proj, proj, sink_rows, y)


def swa_bwd(proj, sink_rows, dy):
    S = proj.shape[0]
    W, Dh = WINDOW, AT_HEAD_DIM
    nb = S // W
    scale = Dh ** -0.5

    def body(q_ref, kp_ref, k_ref, vp_ref, v_ref, sk_ref, dy_ref,
             dq_ref, dko_ref, dkp_ref, dvo_ref, dvp_ref, dsk_ref):
        n = pl.program_id(0)

        @pl.when(n == 0)
        def _():
            dsk_ref[...] = jnp.zeros_like(dsk_ref)

        valid = _swa_valid_t(n)
        kk = jnp.concatenate([kp_ref[...], k_ref[...]], axis=0)
        vv = jnp.concatenate([vp_ref[...], v_ref[...]], axis=0)
        kt = kk.T
        qt = q_ref[...].T
        dot_ = dy_ref[...].T
        dqs, dks, dvs = [], [], []
        for g in range(AT_KV_HEADS):
            qg = _group_lanes(qt, g)
            dog = _group_lanes(dot_, g)
            vg = vv[:, g * Dh:(g + 1) * Dh]
            p, ps = _swa_probs_t(kk[:, g * Dh:(g + 1) * Dh], qg, sk_ref[8 * g:8 * g + 1, :], valid)
            dp = _dot(vg, dog)
            delta = _colsum(dp * p)
            ds = p * (dp - delta) * scale
            dqt = _dot(kt[g * Dh:(g + 1) * Dh], ds)
            dqs += [dqt[:, j * W:(j + 1) * W] for j in range(AT_GROUP)]
            dks.append(_dot_nt(ds, qg))
            dvs.append(_dot_nt(p, dog))
            dsk_ref[8 * g:8 * g + 1, :] += -(ps * delta)
        dq_ref[...] = jnp.concatenate(dqs, axis=0).T.astype(dq_ref.dtype)
        dk = jnp.concatenate(dks, axis=1)
        dv = jnp.concatenate(dvs, axis=1)
        dkp_ref[...] = dk[:W]
        dko_ref[...] = dk[W:]
        dvp_ref[...] = dv[:W]
        dvo_ref[...] = dv[W:]

        @pl.when(n == nb - 1)
        def _():
            for g in range(AT_KV_HEADS):
                for j in range(AT_GROUP):
                    tot = _rowsum(dsk_ref[8 * g:8 * g + 1, j * W:(j + 1) * W])
                    dsk_ref[8 * g + 1 + j:8 * g + 2 + j, :] = jnp.broadcast_to(tot, (1, GROUP_LANES))

    prev = lambda n: jnp.maximum(n - 1, 0)
    kv = pl.BlockSpec((W, 128), lambda n: (n, 0))
    sk = pl.BlockSpec((8 * AT_KV_HEADS, GROUP_LANES), lambda n: (0, 0))
    return pl.pallas_call(
        body,
        name="swa_bwd",
        grid=(nb,),
        in_specs=[pl.BlockSpec((W, AT_WIDTH), lambda n: (n, 4)),
                  pl.BlockSpec((W, 128), lambda n: (prev(n), 20)),
                  pl.BlockSpec((W, 128), lambda n: (n, 20)),
                  pl.BlockSpec((W, 128), lambda n: (prev(n), 21)),
                  pl.BlockSpec((W, 128), lambda n: (n, 21)),
                  sk,
                  pl.BlockSpec((W, AT_WIDTH), lambda n: (n, 1))],
        out_specs=[pl.BlockSpec((W, AT_WIDTH), lambda n: (n, 0)), kv, kv, kv, kv, sk],
        out_shape=[jax.ShapeDtypeStruct((S, AT_WIDTH), BF16)]
                  + [jax.ShapeDtypeStruct((S, 128), F32)] * 4
                  + [jax.ShapeDtypeStruct((8 * AT_KV_HEADS, GROUP_LANES), F32)],
        compiler_params=_params("arbitrary"),
    )(proj, proj, proj, proj, proj, sink_rows, dy)


def assemble_dproj(hg_grads, dq_at, dko, dkp, dvo, dvp, *, rows):
    S = dq_at.shape[0]
    W = WINDOW
    nb = S // W
    bpr = rows // W

    def body(a0, a1, a2, a3, dq, ko, kp, kpn, vo, vp, vpn, out):
        r = pl.program_id(0)
        for i, a in enumerate((a0, a1, a2, a3)):
            out[:, i * HG_WIDTH:(i + 1) * HG_WIDTH] = a[...]
        base = 4 * HG_WIDTH
        out[:, base:base + AT_WIDTH] = dq[...]
        last = (r == pl.num_programs(0) - 1)
        for off, own, pv, pvn in ((base + AT_WIDTH, ko, kp, kpn), (base + AT_WIDTH + 128, vo, vp, vpn)):
            if bpr > 1:
                out[0:rows - W, off:off + 128] = (own[0:rows - W, :] + pv[W:rows, :]).astype(out.dtype)
            nxt = jnp.where(last, 0.0, pvn[...])
            out[rows - W:rows, off:off + 128] = (own[rows - W:rows, :] + nxt).astype(out.dtype)

    hg = pl.BlockSpec((rows, HG_WIDTH), lambda r: (r, 0))
    blk = pl.BlockSpec((rows, 128), lambda r: (r, 0))
    nxt = pl.BlockSpec((W, 128), lambda r: (jnp.minimum((r + 1) * bpr, nb - 1), 0))
    return pl.pallas_call(
        body,
        name="assemble_dproj",
        grid=(S // rows,),
        in_specs=[hg, hg, hg, hg, pl.BlockSpec((rows, AT_WIDTH), lambda r: (r, 0)),
                  blk, blk, nxt, blk, blk, nxt],
        out_specs=pl.BlockSpec((rows, IN_WIDTH), lambda r: (r, 0)),
        out_shape=jax.ShapeDtypeStruct((S, IN_WIDTH), BF16),
        compiler_params=_params("parallel"),
    )(*hg_grads, dq_at, dko, dkp, dkp, dvo, dvp, dvp)


ROW_TILE = 512
COL_TILE = 1408


def _col_tile(n):
    return n if n <= COL_TILE else COL_TILE


def _rms_scale(x):
    return lax.rsqrt(jnp.mean(x * x, axis=1, keepdims=True) + EPS)


def _rms_bwd(d, x, g):
    rs = _rms_scale(x)
    xh = x * rs
    dxh = d * g
    return rs * (dxh - xh * jnp.mean(dxh * xh, axis=1, keepdims=True)), _colsum(d * xh)


def mm(a, b, *, nt=False, out_dtype=F32, res=None, norm_g=None, rms_bwd=None, name):
    parts = a if isinstance(a, tuple) else (a,)
    M, K = parts[0].shape
    N = b.shape[0] if nt else b.shape[1]
    tm = min(ROW_TILE, M)
    tn = _col_tile(N)
    whole_rows = norm_g is not None or rms_bwd is not None
    assert M % tm == 0 and N % tn == 0 and (tn == N or not whole_rows)
    np_ = len(parts)

    def body(*refs):
        a_refs, b_refs, rest = refs[:np_], refs[np_:2 * np_], refs[2 * np_:]
        dot = _dot_nt if nt else _dot
        acc = dot(a_refs[0][...], b_refs[0][...])
        for ar, br in zip(a_refs[1:], b_refs[1:]):
            acc = acc + dot(ar[...], br[...])
        if rms_bwd is not None:
            h_ref, g_ref, dr_ref, dh_ref, dhb_ref, dg_ref = rest

            @pl.when(pl.program_id(1) == 0)
            def _():
                dg_ref[...] = jnp.zeros_like(dg_ref)

            dx, dgp = _rms_bwd(acc, h_ref[...], g_ref[...])
            dg_ref[0:1, :] += dgp
            dh = dr_ref[...] + dx
            dh_ref[...] = dh
            dhb_ref[...] = dh.astype(BF16)
            return
        rest = list(rest)
        if res is not None:
            acc = acc + rest.pop(0)[...]
        if norm_g is not None:
            g_ref = rest.pop(0)
            rest[1][...] = (acc * _rms_scale(acc) * g_ref[...]).astype(BF16)
        rest[0][...] = acc.astype(rest[0].dtype)

    row = pl.BlockSpec((tm, tn), lambda j, i: (i, j))
    in_specs = [pl.BlockSpec((tm, K), lambda j, i: (i, 0)) for _ in parts]
    for kb in range(np_):
        in_specs.append(pl.BlockSpec((tn, K), lambda j, i, kb=kb: (j, kb)) if nt
                        else pl.BlockSpec((K, tn), lambda j, i, kb=kb: (kb, j)))
    args = list(parts) + [b] * np_
    if rms_bwd is not None:
        h, g, dres = rms_bwd
        in_specs += [row, _const_spec((1, N)), row]
        args += [h, g, dres]
        out_specs = [row, row, _const_spec((8, N))]
        out_shape = [jax.ShapeDtypeStruct((M, N), F32), jax.ShapeDtypeStruct((M, N), BF16),
                     jax.ShapeDtypeStruct((8, N), F32)]
        sem = ("arbitrary", "arbitrary")
    else:
        if res is not None:
            in_specs.append(row)
            args.append(res)
        out_specs, out_shape = [row], [jax.ShapeDtypeStruct((M, N), out_dtype)]
        if norm_g is not None:
            in_specs.append(_const_spec((1, N)))
            args.append(norm_g)
            out_specs.append(row)
            out_shape.append(jax.ShapeDtypeStruct((M, N), BF16))
        sem = ("parallel", "parallel")
    out = pl.pallas_call(
        body,
        name=name,
        grid=(N // tn, M // tm),
        in_specs=in_specs,
        out_specs=out_specs,
        out_shape=out_shape,
        compiler_params=_params(*sem),
    )(*args)
    return out[0] if len(out) == 1 else out


def mm_tn(a, b, *, name):
    M, K = a.shape
    N = b.shape[1]
    tm = min(ROW_TILE, M)
    tk = _col_tile(K)
    tn = _col_tile(N)
    assert M % tm == 0 and K % tk == 0 and N % tn == 0

    def body(a_ref, b_ref, o_ref):
        @pl.when(pl.program_id(2) == 0)
        def _():
            o_ref[...] = jnp.zeros_like(o_ref)

        o_ref[...] += _dot_tn(a_ref[...], b_ref[...])

    return pl.pallas_call(
        body,
        name=name,
        grid=(K // tk, N // tn, M // tm),
        in_specs=[pl.BlockSpec((tm, tk), lambda k, j, i: (i, k)),
                  pl.BlockSpec((tm, tn), lambda k, j, i: (i, j))],
        out_specs=pl.BlockSpec((tk, tn), lambda k, j, i: (k, j)),
        out_shape=jax.ShapeDtypeStruct((K, N), F32),
        compiler_params=_params("parallel", "parallel", "arbitrary"),
    )(a, b)


def _row_spec(tm, width):
    return pl.BlockSpec((tm, width), lambda i: (i, 0))


def _const_spec(shape):
    return pl.BlockSpec(shape, lambda *_: (0,) * len(shape))


def rmsnorm_fwd(h, g, *, name):
    S, D = h.shape
    tm = min(ROW_TILE, S)

    def body(h_ref, g_ref, u_ref):
        x = h_ref[...]
        rs = lax.rsqrt(jnp.mean(x * x, axis=1, keepdims=True) + EPS)
        u_ref[...] = (x * rs * g_ref[...]).astype(u_ref.dtype)

    return pl.pallas_call(
        body, name=name, grid=(S // tm,),
        in_specs=[_row_spec(tm, D), _const_spec((1, D))],
        out_specs=_row_spec(tm, D),
        out_shape=jax.ShapeDtypeStruct((S, D), BF16),
        compiler_params=_params("parallel"),
    )(h, g)


def rmsnorm_bwd(du, h, g, dres, *, name):
    S, D = h.shape
    tm = min(ROW_TILE, S)

    def body(du_ref, h_ref, g_ref, dr_ref, dh_ref, dhb_ref, dg_ref):
        @pl.when(pl.program_id(0) == 0)
        def _():
            dg_ref[...] = jnp.zeros_like(dg_ref)

        x = h_ref[...]
        d = du_ref[...]
        rs = lax.rsqrt(jnp.mean(x * x, axis=1, keepdims=True) + EPS)
        xh = x * rs
        dg_ref[0:1, :] += _colsum(d * xh)
        dxh = d * g_ref[...]
        dh = dr_ref[...] + rs * (dxh - xh * jnp.mean(dxh * xh, axis=1, keepdims=True))
        dh_ref[...] = dh
        dhb_ref[...] = dh.astype(BF16)

    return pl.pallas_call(
        body, name=name, grid=(S // tm,),
        in_specs=[_row_spec(tm, D), _row_spec(tm, D), _const_spec((1, D)), _row_spec(tm, D)],
        out_specs=[_row_spec(tm, D), _row_spec(tm, D), _const_spec((8, D))],
        out_shape=[jax.ShapeDtypeStruct((S, D), F32), jax.ShapeDtypeStruct((S, D), BF16),
                   jax.ShapeDtypeStruct((8, D), F32)],
        compiler_params=_params("arbitrary"),
    )(du, h, g, dres)


def _shift_down(x, edge8, s):
    sh = pltpu.roll(x, s, 0)
    er = pltpu.roll(edge8, s, 0)
    row8 = lax.broadcasted_iota(jnp.int32, er.shape, 0)
    top = jnp.where(row8 < s, er, sh[0:8])
    return jnp.concatenate([top, sh[8:]], axis=0)


def _shift_up(x, s):
    return pltpu.roll(x, x.shape[0] - s, 0)


HALO = 16


def _conv_pre(a, prev8, w_ref, cb_ref):
    a1 = _shift_down(a, prev8, 1)
    a2 = _shift_down(a, prev8, 2)
    ac = w_ref[2:3, :] * a + w_ref[1:2, :] * a1 + w_ref[0:1, :] * a2 + cb_ref[...]
    return ac, a1, a2


def convffn_fwd(hh, cw8, cb):
    S = hh.shape[0]
    tm = min(ROW_TILE, S)
    tn = _col_tile(D_FF)
    nj = D_FF // tn

    def body(a_ref, ap_ref, b_ref, w_ref, cb_ref, o_ref):
        prev8 = jnp.where(pl.program_id(1) == 0, 0.0, ap_ref[...].astype(F32)[HALO - 8:HALO])
        ac, _, _ = _conv_pre(a_ref[...].astype(F32), prev8, w_ref, cb_ref)
        o_ref[...] = (ac * _sigmoid(ac) * b_ref[...].astype(F32)).astype(o_ref.dtype)

    rh = tm // HALO
    return pl.pallas_call(
        body, name="convffn_fwd", grid=(nj, S // tm),
        in_specs=[pl.BlockSpec((tm, tn), lambda j, i: (i, j)),
                  pl.BlockSpec((HALO, tn), lambda j, i: (jnp.maximum(i * rh - 1, 0), j)),
                  pl.BlockSpec((tm, tn), lambda j, i: (i, j + nj)),
                  pl.BlockSpec((8, tn), lambda j, i: (0, j)),
                  pl.BlockSpec((1, tn), lambda j, i: (0, j))],
        out_specs=pl.BlockSpec((tm, tn), lambda j, i: (i, j)),
        out_shape=jax.ShapeDtypeStruct((S, D_FF), BF16),
        compiler_params=_params("parallel", "parallel"),
    )(hh, hh, hh, cw8, cb)


def convffn_bwd(hh, dact, cw8, cb):
    S = hh.shape[0]
    tm = min(ROW_TILE, S)
    tn = _col_tile(D_FF)
    nj = D_FF // tn
    ni = S // tm

    def body(a_ref, ap_ref, an_ref, b_ref, bn_ref, d_ref, dn_ref, w_ref, cb_ref, o_a, o_b, dw_ref):
        i = pl.program_id(1)

        @pl.when(i == 0)
        def _():
            dw_ref[...] = jnp.zeros_like(dw_ref)

        up = lambda r: r[...].astype(F32)
        prev8 = jnp.where(i == 0, 0.0, up(ap_ref)[HALO - 8:HALO])
        a = jnp.concatenate([up(a_ref), up(an_ref)[0:8]], axis=0)
        b = jnp.concatenate([up(b_ref), up(bn_ref)[0:8]], axis=0)
        d = jnp.concatenate([up(d_ref), jnp.where(i == ni - 1, 0.0, up(dn_ref)[0:8])], axis=0)
        ac, a1, a2 = _conv_pre(a, prev8, w_ref, cb_ref)
        sa = _sigmoid(ac)
        o_b[...] = (d[0:tm] * ac[0:tm] * sa[0:tm]).astype(o_b.dtype)
        dac = d * b * (sa + ac * sa * (1.0 - sa))
        da = w_ref[2:3, :] * dac + w_ref[1:2, :] * _shift_up(dac, 1) + w_ref[0:1, :] * _shift_up(dac, 2)
        o_a[...] = da[0:tm].astype(o_a.dtype)
        dc = dac[0:tm]
        dw_ref[0:1, :] += _colsum(dc * a2[0:tm])
        dw_ref[1:2, :] += _colsum(dc * a1[0:tm])
        dw_ref[2:3, :] += _colsum(dc * a[0:tm])
        dw_ref[3:4, :] += _colsum(dc)

    rh = tm // HALO
    last = S // HALO - 1
    cur = lambda off: pl.BlockSpec((tm, tn), lambda j, i: (i, j + off))
    nxt = lambda off: pl.BlockSpec((HALO, tn), lambda j, i: (jnp.minimum((i + 1) * rh, last), j + off))
    return pl.pallas_call(
        body, name="convffn_bwd", grid=(nj, ni),
        in_specs=[cur(0),
                  pl.BlockSpec((HALO, tn), lambda j, i: (jnp.maximum(i * rh - 1, 0), j)),
                  nxt(0), cur(nj), nxt(nj), cur(0), nxt(0),
                  pl.BlockSpec((8, tn), lambda j, i: (0, j)),
                  pl.BlockSpec((1, tn), lambda j, i: (0, j))],
        out_specs=[cur(0), cur(0), pl.BlockSpec((8, tn), lambda j, i: (0, j))],
        out_shape=[jax.ShapeDtypeStruct((S, D_FF), BF16), jax.ShapeDtypeStruct((S, D_FF), BF16),
                   jax.ShapeDtypeStruct((8, D_FF), F32)],
        compiler_params=_params("parallel", "arbitrary"),
    )(hh, hh, hh, hh, hh, dact, dact, cw8, cb)


def ple_fwd(h, gpre, p, wpu, norm_g):
    S, D = h.shape
    tm = min(ROW_TILE, S)

    def body(h_ref, g_ref, p_ref, w_ref, ng_ref, o_ref, u_ref):
        out = h_ref[...] + _sigmoid(g_ref[...]) * _dot(p_ref[...], w_ref[...])
        o_ref[...] = out
        u_ref[...] = (out * _rms_scale(out) * ng_ref[...]).astype(BF16)

    return pl.pallas_call(
        body, name="ple_fwd", grid=(S // tm,),
        in_specs=[_row_spec(tm, D), _row_spec(tm, D), _row_spec(tm, PLE_DIM), _const_spec((PLE_DIM, D)),
                  _const_spec((1, D))],
        out_specs=[_row_spec(tm, D), _row_spec(tm, D)],
        out_shape=[jax.ShapeDtypeStruct((S, D), F32), jax.ShapeDtypeStruct((S, D), BF16)],
        compiler_params=_params("parallel"),
    )(h, gpre, p, wpu, norm_g)


def ple_bwd(dh, gpre, p, wpu):
    S, D = dh.shape
    tm = min(ROW_TILE, S)

    def body(d_ref, g_ref, p_ref, w_ref, dpu_ref, dg_ref):
        d = d_ref[...]
        gate = _sigmoid(g_ref[...])
        pu = _dot(p_ref[...], w_ref[...])
        dpu_ref[...] = (d * gate).astype(dpu_ref.dtype)
        dg_ref[...] = (d * pu * gate * (1.0 - gate)).astype(dg_ref.dtype)

    return pl.pallas_call(
        body, name="ple_bwd", grid=(S // tm,),
        in_specs=[_row_spec(tm, D), _row_spec(tm, D), _row_spec(tm, PLE_DIM), _const_spec((PLE_DIM, D))],
        out_specs=[_row_spec(tm, D), _row_spec(tm, D)],
        out_shape=[jax.ShapeDtypeStruct((S, D), BF16)] * 2,
        compiler_params=_params("parallel"),
    )(dh, gpre, p, wpu)


def loss_head(h, g, tgt):
    S, D = h.shape
    tm = min(ROW_TILE, S)

    def body(h_ref, g_ref, t_ref, dh_ref, dhb_ref, l_ref, dg_ref):
        @pl.when(pl.program_id(0) == 0)
        def _():
            l_ref[...] = jnp.zeros_like(l_ref)
            dg_ref[...] = jnp.zeros_like(dg_ref)

        x = h_ref[...]
        gr = g_ref[...]
        rs = lax.rsqrt(jnp.mean(x * x, axis=1, keepdims=True) + EPS)
        xh = x * rs
        err = xh * gr - t_ref[...]
        l_ref[0:1, 0:1] += 0.5 * _colsum(jnp.mean(err * err, axis=1, keepdims=True))
        dy = err * (1.0 / D)
        dg_ref[0:1, :] += _colsum(dy * xh)
        dxh = dy * gr
        dh = rs * (dxh - xh * jnp.mean(dxh * xh, axis=1, keepdims=True))
        dh_ref[...] = dh
        dhb_ref[...] = dh.astype(BF16)

    return pl.pallas_call(
        body, name="loss_head", grid=(S // tm,),
        in_specs=[_row_spec(tm, D), _const_spec((1, D)), _row_spec(tm, D)],
        out_specs=[_row_spec(tm, D), _row_spec(tm, D), _const_spec((8, 128)), _const_spec((8, D))],
        out_shape=[jax.ShapeDtypeStruct((S, D), F32), jax.ShapeDtypeStruct((S, D), BF16),
                   jax.ShapeDtypeStruct((8, 128), F32), jax.ShapeDtypeStruct((8, D), F32)],
        compiler_params=_params("arbitrary"),
    )(h, g, tgt)


def _lb_rows(l_ref):
    l = l_ref[...]
    e = jnp.exp(l - jnp.max(l, axis=0, keepdims=True))
    p = e / _colsum(e)
    lbs, run = [], None
    for i in range(DEPTH):
        run = p[i:i + 1] if i == 0 else run + p[i:i + 1]
        lbs.append(run - p[0:1])
    return p, lbs


def lb_fwd(lb_logits):
    def body(l_ref, o_ref):
        _, lbs = _lb_rows(l_ref)
        o_ref[...] = jnp.zeros_like(o_ref)
        for i, lb in enumerate(lbs):
            o_ref[8 * i:8 * i + 1, :] = jnp.log(jnp.maximum(lb, LB_FLOOR))
            o_ref[8 * i + 1:8 * i + 2, :] = jnp.log1p(-lb)
            o_ref[8 * i + 2:8 * i + 3, :] = 1.0 - lb
            o_ref[8 * i + 3:8 * i + 4, :] = lb

    return pl.pallas_call(
        body, name="lb_fwd",
        out_shape=jax.ShapeDtypeStruct((DEPTH * 8, HG_WIDTH), F32),
    )(lb_logits)


def lb_bwd(dlbrows, lb_logits):
    def body(d_ref, l_ref, o_ref):
        p, lbs = _lb_rows(l_ref)
        dlb = []
        for i, lb in enumerate(lbs):
            da = d_ref[8 * i:8 * i + 1, :]
            dc = d_ref[8 * i + 1:8 * i + 2, :]
            do = d_ref[8 * i + 2:8 * i + 3, :]
            dlb.append(jnp.where(lb > LB_FLOOR, da / jnp.maximum(lb, LB_FLOOR), 0.0) - dc / (1.0 - lb) - do)
        dp = [jnp.zeros_like(dlb[0])]
        for j in range(1, DEPTH):
            acc = dlb[j]
            for i in range(j + 1, DEPTH):
                acc = acc + dlb[i]
            dp.append(acc)
        dot_ = p[0:1] * dp[0]
        for j in range(1, DEPTH):
            dot_ = dot_ + p[j:j + 1] * dp[j]
        o_ref[...] = jnp.zeros_like(o_ref)
        for j in range(DEPTH):
            o_ref[j:j + 1, :] = p[j:j + 1] * (dp[j] - dot_)

    return pl.pallas_call(
        body, name="lb_bwd",
        out_shape=jax.ShapeDtypeStruct((8, HG_WIDTH), F32),
    )(dlbrows, lb_logits)


def adamw(w, g, m, v, *, name):
    R, C = w.shape
    tr = next((t for t in (512, 256, 128, 64, 32, 16, 8) if R % t == 0), R)

    def body(w_ref, g_ref, m_ref, v_ref, d_ref, m2_ref, v2_ref):
        gv = g_ref[...]
        m2 = ADAM_B1 * m_ref[...] + (1.0 - ADAM_B1) * gv
        v2 = ADAM_B2 * v_ref[...] + (1.0 - ADAM_B2) * (gv * gv)
        mh = m2 / (1.0 - ADAM_B1 ** ADAM_STEP)
        vh = v2 / (1.0 - ADAM_B2 ** ADAM_STEP)
        d_ref[...] = -ADAM_LR * (mh / (jnp.sqrt(vh) + ADAM_EPS) + ADAM_WD * w_ref[...])
        m2_ref[...] = m2
        v2_ref[...] = v2

    spec = pl.BlockSpec((tr, C), lambda i: (i, 0))
    return pl.pallas_call(
        body, name=name, grid=(R // tr,),
        in_specs=[spec] * 4, out_specs=[spec] * 3,
        out_shape=[jax.ShapeDtypeStruct((R, C), F32)] * 3,
        compiler_params=_params("parallel"),
    )(w, g, m, v)


def sum_slots(x, *, out_dtype, name):
    n, R, C = x.shape
    tr = 848 if R % 848 == 0 else R

    def body(x_ref, o_ref):
        acc = x_ref[0].astype(F32)
        for k in range(1, n):
            acc = acc + x_ref[k].astype(F32)
        o_ref[...] = acc.astype(o_ref.dtype)

    return pl.pallas_call(
        body, name=name, grid=(R // tr,),
        in_specs=[pl.BlockSpec((n, tr, C), lambda i: (0, i, 0))],
        out_specs=pl.BlockSpec((tr, C), lambda i: (i, 0)),
        out_shape=jax.ShapeDtypeStruct((R, C), out_dtype),
        compiler_params=_params("parallel"),
    )(x)


MESH = pl.DeviceIdType.MESH
ANY = pl.BlockSpec(memory_space=pl.ANY)


def _place():
    return lax.axis_index("x"), lax.axis_index("y"), lax.axis_index("c")


def _other_chips(x, y):
    return [(1 - x, y), (x, 1 - y), (1 - x, 1 - y)]


def small_allgather(buf):
    R, C = buf.shape

    def body(x_ref, out_ref, send_sems, recv_sems, local_sem):
        x, y, c = _place()
        me, sibling = (x, y, c), (x, y, 1 - c)
        chips = _other_chips(x, y)

        def slot(px, py, pc):
            return out_ref.at[4 * px + 2 * py + pc]

        def copy(k, block, to, src=None):
            return pltpu.make_async_remote_copy(
                src_ref=slot(*block) if src is None else src, dst_ref=slot(*block),
                send_sem=send_sems.at[k], recv_sem=recv_sems.at[k],
                device_id=to, device_id_type=MESH)

        mine = pltpu.make_async_copy(x_ref, slot(*me), local_sem)
        mine.start()
        first = [copy(0, me, sibling, src=x_ref)]
        first += [copy(1 + r, me, (*chip, c), src=x_ref) for r, chip in enumerate(chips)]
        for cp in first:
            cp.start()
        passed = [copy(4 + r, (*chip, c), sibling) for r, chip in enumerate(chips)]
        for r, chip in enumerate(chips):
            copy(1 + r, (*chip, c), me).wait_recv()
            passed[r].start()
        copy(0, sibling, me).wait_recv()
        for r, chip in enumerate(chips):
            copy(4 + r, (*chip, 1 - c), me).wait_recv()
        for cp in first + passed:
            cp.wait_send()
        mine.wait()

    return pl.pallas_call(
        body, name="small_allgather",
        out_shape=jax.ShapeDtypeStruct((8, R, C), buf.dtype),
        in_specs=[pl.BlockSpec(memory_space=pltpu.VMEM)],
        out_specs=pl.BlockSpec(memory_space=pltpu.VMEM),
        scratch_shapes=[pltpu.SemaphoreType.DMA((7,)), pltpu.SemaphoreType.DMA((7,)),
                        pltpu.SemaphoreType.DMA],
    )(buf)


def weights_allgather(wp):
    R, C = wp.shape
    half = R // 2

    def body(w_ref, g_ref, send_sems, recv_sems, local_sem):
        x, y, c = _place()
        j = 2 * x + y
        sibling = (x, y, 1 - c)
        chips = _other_chips(x, y)
        mine = pl.ds(pl.multiple_of(c * half, 16), half)
        theirs = pl.ds(pl.multiple_of((1 - c) * half, 16), half)

        def copy(k, chip_block, rows, to, src=None):
            dst = g_ref.at[chip_block, rows]
            return pltpu.make_async_remote_copy(
                src_ref=dst if src is None else src, dst_ref=dst,
                send_sem=send_sems.at[k], recv_sem=recv_sems.at[k],
                device_id=to, device_id_type=MESH)

        own = pltpu.make_async_copy(w_ref, g_ref.at[j], local_sem)
        own.start()
        first = [copy(r, j, mine, (*chip, c), src=w_ref.at[mine]) for r, chip in enumerate(chips)]
        for cp in first:
            cp.start()
        passed = [copy(3 + r, 2 * chip[0] + chip[1], mine, sibling) for r, chip in enumerate(chips)]
        for r, chip in enumerate(chips):
            copy(r, 2 * chip[0] + chip[1], mine, (*chip, c)).wait_recv()
            passed[r].start()
        for r, chip in enumerate(chips):
            copy(3 + r, 2 * chip[0] + chip[1], theirs, sibling).wait_recv()
        for cp in first + passed:
            cp.wait_send()
        own.wait()

    return pl.pallas_call(
        body, name="weights_allgather",
        out_shape=jax.ShapeDtypeStruct((4, R, C), wp.dtype),
        in_specs=[ANY], out_specs=ANY,
        scratch_shapes=[pltpu.SemaphoreType.DMA((6,)), pltpu.SemaphoreType.DMA((6,)),
                        pltpu.SemaphoreType.DMA],
    )(wp)


def sibling_swap(v, *, name):
    def body(v_ref, got_ref, send_sem, recv_sem):
        x, y, c = _place()
        cp = pltpu.make_async_remote_copy(
            src_ref=v_ref, dst_ref=got_ref, send_sem=send_sem, recv_sem=recv_sem,
            device_id=(x, y, 1 - c), device_id_type=MESH)
        cp.start()
        cp.wait()

    return pl.pallas_call(
        body, name=name,
        out_shape=jax.ShapeDtypeStruct(v.shape, v.dtype),
        in_specs=[ANY], out_specs=ANY,
        scratch_shapes=[pltpu.SemaphoreType.DMA, pltpu.SemaphoreType.DMA],
    )(v)


def chip_exchange(q):
    def body(q_ref, r_ref, send_sems, recv_sems, local_sem):
        x, y, c = _place()
        j = 2 * x + y
        chips = _other_chips(x, y)
        own = pltpu.make_async_copy(q_ref.at[j], r_ref.at[j], local_sem)
        own.start()

        def copy(r, src_block, dst_block, chip):
            return pltpu.make_async_remote_copy(
                src_ref=q_ref.at[src_block], dst_ref=r_ref.at[dst_block],
                send_sem=send_sems.at[r], recv_sem=recv_sems.at[r],
                device_id=(*chip, c), device_id_type=MESH)

        sends = [copy(r, 2 * chip[0] + chip[1], j, chip) for r, chip in enumerate(chips)]
        for cp in sends:
            cp.start()
        for r, chip in enumerate(chips):
            jr = 2 * chip[0] + chip[1]
            copy(r, jr, jr, chip).wait_recv()
        for cp in sends:
            cp.wait_send()
        own.wait()

    return pl.pallas_call(
        body, name="chip_exchange",
        out_shape=jax.ShapeDtypeStruct(q.shape, q.dtype),
        in_specs=[ANY], out_specs=ANY,
        scratch_shapes=[pltpu.SemaphoreType.DMA((3,)), pltpu.SemaphoreType.DMA((3,)),
                        pltpu.SemaphoreType.DMA],
    )(q)


N_CHIPS = 4
_PACK = (("w_in", 704), ("w_out", 256), ("w_up", 1408), ("w_down", 704), ("w_ple_gate", 256), ("w_ple_up", 64))
LAYER_ROWS = sum(r for _, r in _PACK)
PACK_ROWS = DEPTH * LAYER_ROWS
HALF_ROWS = PACK_ROWS // 2


def _pack_shards(sh):
    parts = []
    for i in range(DEPTH):
        for name, rows in _PACK:
            parts.append(sh[name][i].reshape(rows, D_MODEL))
    return jnp.concatenate(parts, axis=0)


def _unpack_shards(slab):
    shapes = {"w_in": (D_MODEL, IN_WIDTH // N_CHIPS), "w_out": (D_MODEL // N_CHIPS, D_MODEL),
              "w_up": (D_MODEL, 2 * D_FF // N_CHIPS), "w_down": (D_FF // N_CHIPS, D_MODEL),
              "w_ple_gate": (D_MODEL // N_CHIPS, D_MODEL), "w_ple_up": (PLE_DIM, D_MODEL // N_CHIPS)}
    out = {name: [] for name, _ in _PACK}
    off = 0
    for i in range(DEPTH):
        for name, rows in _PACK:
            out[name].append(slab[off:off + rows].reshape(shapes[name]))
            off += rows
    return {k: jnp.stack(v) for k, v in out.items()}


_COL_SHARDED = ("w_in", "w_up", "w_ple_up")


def _full_from_chips(g, layer):
    per_chip = [_unpack_shards_layer(g[k], layer) for k in range(N_CHIPS)]
    return {name: jnp.concatenate([pc[name] for pc in per_chip], axis=1 if name in _COL_SHARDED else 0)
            for name, _ in _PACK}


def _unpack_shards_layer(slab, layer):
    shapes = {"w_in": (D_MODEL, IN_WIDTH // N_CHIPS), "w_out": (D_MODEL // N_CHIPS, D_MODEL),
              "w_up": (D_MODEL, 2 * D_FF // N_CHIPS), "w_down": (D_FF // N_CHIPS, D_MODEL),
              "w_ple_gate": (D_MODEL // N_CHIPS, D_MODEL), "w_ple_up": (PLE_DIM, D_MODEL // N_CHIPS)}
    out = {}
    off = layer * LAYER_ROWS
    for name, rows in _PACK:
        out[name] = slab[off:off + rows].reshape(shapes[name])
        off += rows
    return out


def _split_to_chips(full, name):
    r, c = full.shape
    if name in _COL_SHARDED:
        full = full.reshape(r, N_CHIPS, c // N_CHIPS).transpose(1, 0, 2)
    return full.reshape(N_CHIPS, -1, D_MODEL)


_SMALL = (("loss", 128), ("g_final", 1024), ("g_mix", 4096), ("lb_logits", 2048), ("hg_norm_g", 2048),
          ("attn_sinks", 128), ("g_ffn", 4096), ("conv_w", 4 * 3 * D_FF), ("conv_b", 4 * D_FF), ("g_ple", 4096))
SMALL_ROWS = 496


def _pack_small(d):
    parts = []
    for name, n in _SMALL:
        v = d[name].reshape(-1).astype(F32)
        parts.append(jnp.pad(v, (0, n - v.shape[0])))
    flat = jnp.concatenate(parts)
    return jnp.pad(flat, (0, SMALL_ROWS * 128 - flat.shape[0])).reshape(SMALL_ROWS, 128)


def _unpack_small(buf, shapes):
    flat = buf.reshape(-1)
    out, off = {}, 0
    for name, n in _SMALL:
        size = 1
        for s in shapes[name]:
            size *= s
        out[name] = flat[off:off + size].reshape(shapes[name])
        off += n
    return out


WEIGHT_ORDER = ('g_mix', 'w_in', 'lb_logits', 'hg_norm_g', 'attn_sinks', 'w_out', 'g_ffn', 'w_up', 'conv_w',
                'conv_b', 'w_down', 'g_ple', 'w_ple_gate', 'w_ple_up', 'g_final')


def kernel(x, p, g_mix, w_in, lb_logits, hg_norm_g, attn_sinks, w_out, g_ffn, w_up, conv_w, conv_b, w_down, g_ple, w_ple_gate, w_ple_up, g_final, loss_target, m_g_mix, m_w_in, m_lb_logits, m_hg_norm_g, m_attn_sinks, m_w_out, m_g_ffn, m_w_up, m_conv_w, m_conv_b, m_w_down, m_g_ple, m_w_ple_gate, m_w_ple_up, m_g_final, v_g_mix, v_w_in, v_lb_logits, v_hg_norm_g, v_attn_sinks, v_w_out, v_g_ffn, v_w_up, v_conv_w, v_conv_b, v_w_down, v_g_ple, v_w_ple_gate, v_w_ple_up, v_g_final):
    W = dict(g_mix=g_mix, w_in=w_in, lb_logits=lb_logits, hg_norm_g=hg_norm_g, attn_sinks=attn_sinks,
             w_out=w_out, g_ffn=g_ffn, w_up=w_up, conv_w=conv_w, conv_b=conv_b, w_down=w_down, g_ple=g_ple,
             w_ple_gate=w_ple_gate, w_ple_up=w_ple_up, g_final=g_final)
    M = dict(g_mix=m_g_mix, w_in=m_w_in, lb_logits=m_lb_logits, hg_norm_g=m_hg_norm_g, attn_sinks=m_attn_sinks,
             w_out=m_w_out, g_ffn=m_g_ffn, w_up=m_w_up, conv_w=m_conv_w, conv_b=m_conv_b, w_down=m_w_down,
             g_ple=m_g_ple, w_ple_gate=m_w_ple_gate, w_ple_up=m_w_ple_up, g_final=m_g_final)
    V = dict(g_mix=v_g_mix, w_in=v_w_in, lb_logits=v_lb_logits, hg_norm_g=v_hg_norm_g, attn_sinks=v_attn_sinks,
             w_out=v_w_out, g_ffn=v_g_ffn, w_up=v_w_up, conv_w=v_conv_w, conv_b=v_conv_b, w_down=v_w_down,
             g_ple=v_g_ple, w_ple_gate=v_w_ple_gate, w_ple_up=v_w_ple_up, g_final=v_g_final)
    S = x.shape[1]
    hg_rows = min(ROW_TILE, S)
    xi, yi, ci = _place()
    chip = 2 * xi + yi

    slab = _pack_shards({n: W[n] for n, _ in _PACK}).astype(BF16)
    gathered = weights_allgather(slab)
    cw_shard = jnp.pad(conv_w.reshape(-1), (0, 72 * 128 - conv_w.size)).reshape(72, 128)
    cw_all = small_allgather(cw_shard)
    cw_full = jnp.concatenate(
        [cw_all[2 * k].reshape(-1)[:conv_w.size].reshape(conv_w.shape) for k in range(N_CHIPS)], axis=2)
    lbrows = lb_fwd(lb_logits)

    h = x[0]
    saved = []
    for i in range(DEPTH):
        wf = _full_from_chips(gathered, i)
        lbr = lbrows[8 * i:8 * i + 8]
        ng = hg_norm_g[i][None]
        sinks_b = jnp.pad(jnp.repeat(attn_sinks[i].reshape(AT_KV_HEADS, 1, AT_GROUP), WINDOW, axis=2),
                          ((0, 0), (0, 7), (0, 0))).reshape(8 * AT_KV_HEADS, GROUP_LANES)
        cw8 = jnp.pad(cw_full[i], ((0, 5), (0, 0)))
        cb = conv_b[i][None]
        if i == 0:
            u = rmsnorm_fwd(h, g_mix[0][None], name="rmsnorm_fwd")
        proj = mm(u, wf["w_in"], name="mm_in")
        y, o_raw, states = hgrn_fwd(proj, lbr, ng, D_MODEL, rows=hg_rows)
        y = swa_fwd(proj, sinks_b, y)
        h1, u2 = mm(y, wf["w_out"], res=h, norm_g=g_ffn[i][None], name="mm_out")
        hh = mm(u2, wf["w_up"], out_dtype=BF16, name="mm_up")
        act = convffn_fwd(hh, cw8, cb)
        h2, u3 = mm(act, wf["w_down"], res=h1, norm_g=g_ple[i][None], name="mm_down")
        gpre = mm(u3, wf["w_ple_gate"], name="mm_gate")
        next_g = g_mix[i + 1] if i + 1 < DEPTH else g_final
        h3, u_next = ple_fwd(h2, gpre, p[i, 0], wf["w_ple_up"], next_g[None])
        saved.append(dict(wf=wf, lbr=lbr, ng=ng, sinks_b=sinks_b, cw8=cw8, cb=cb, h=h, u=u, proj=proj,
                          o_raw=o_raw, states=states, y=y, h1=h1, u2=u2, hh=hh, act=act, h2=h2, u3=u3,
                          gpre=gpre))
        h, u = h3, u_next

    dh, dhb, loss_acc, dg_final = loss_head(h, g_final[None], loss_target[0])

    gfull = {n: [None] * DEPTH for n, _ in _PACK}
    gsmall = {n: [None] * DEPTH for n in ("g_mix", "hg_norm_g", "attn_sinks", "g_ffn", "conv_w", "conv_b", "g_ple")}
    dlbrows = [None] * DEPTH
    for i in reversed(range(DEPTH)):
        s = saved[i]
        wf = s["wf"]
        dpu, dgp = ple_bwd(dh, s["gpre"], p[i, 0], wf["w_ple_up"])
        gfull["w_ple_up"][i] = mm_tn(p[i, 0], dpu, name="mm_tn_pu")
        gfull["w_ple_gate"][i] = mm_tn(s["u3"], dgp, name="mm_tn_gate")
        dh2, dh2b, dg = mm(dgp, wf["w_ple_gate"], nt=True, rms_bwd=(s["h2"], g_ple[i][None], dh),
                           name="mm_nt_gate")
        gsmall["g_ple"][i] = dg[0]
        gfull["w_down"][i] = mm_tn(s["act"], dh2b, name="mm_tn_down")
        dact = mm(dh2b, wf["w_down"], nt=True, out_dtype=BF16, name="mm_nt_down")
        da, db, dcw = convffn_bwd(s["hh"], dact, s["cw8"], s["cb"])
        gsmall["conv_w"][i] = dcw[0:3]
        gsmall["conv_b"][i] = dcw[3]
        gfull["w_up"][i] = jnp.concatenate([mm_tn(s["u2"], da, name="mm_tn_up"),
                                            mm_tn(s["u2"], db, name="mm_tn_up")], axis=1)
        dh1, dh1b, dg = mm((da, db), wf["w_up"], nt=True, rms_bwd=(s["h1"], g_ffn[i][None], dh2),
                           name="mm_nt_up")
        gsmall["g_ffn"][i] = dg[0]
        gfull["w_out"][i] = mm_tn(s["y"], dh1b, name="mm_tn_out")
        dy = mm(dh1b, wf["w_out"], nt=True, name="mm_nt_out")
        dq_at, dko, dkp, dvo, dvp, dsk = swa_bwd(s["proj"], s["sinks_b"], dy)
        gsmall["attn_sinks"][i] = dsk.reshape(AT_KV_HEADS, 8, GROUP_LANES)[:, 1:1 + AT_GROUP, 0].reshape(-1)
        hq, hz, hv, hgp, dlbr, dng = hgrn_bwd(s["proj"], s["o_raw"], s["states"], dy, s["lbr"], s["ng"],
                                              rows=hg_rows)
        dlbrows[i] = dlbr
        gsmall["hg_norm_g"][i] = dng[0]
        dproj = assemble_dproj((hq, hz, hv, hgp), dq_at, dko, dkp, dvo, dvp, rows=hg_rows)
        gfull["w_in"][i] = mm_tn(s["u"], dproj, name="mm_tn_in")
        dh, dhb, dg = mm(dproj, wf["w_in"], nt=True, rms_bwd=(s["h"], g_mix[i][None], dh1), name="mm_nt_in")
        gsmall["g_mix"][i] = dg[0]
    grad_x = dh[None]
    dlb_logits = lb_bwd(jnp.concatenate(dlbrows, axis=0), lb_logits)[0:DEPTH]

    parts = []
    for i in range(DEPTH):
        for name, _ in _PACK:
            parts.append(_split_to_chips(gfull[name][i], name))
    pk = jnp.concatenate(parts, axis=1).astype(BF16)
    pk = pk.reshape(N_CHIPS, 2, HALF_ROWS, D_MODEL)
    p_mine = lax.dynamic_index_in_dim(pk, ci, axis=1, keepdims=False)
    p_other = lax.dynamic_index_in_dim(pk, 1 - ci, axis=1, keepdims=False)
    from_sib = sibling_swap(p_other, name="sibling_swap_partials")
    pair = sum_slots(jnp.stack([p_mine.reshape(-1, D_MODEL), from_sib.reshape(-1, D_MODEL)]),
                     out_dtype=BF16, name="sum_pair")
    from_chips = chip_exchange(pair.reshape(N_CHIPS, HALF_ROWS, D_MODEL))
    mine_sum = sum_slots(from_chips, out_dtype=F32, name="sum_chips")
    sib_sum = sibling_swap(mine_sum, name="sibling_swap_sums")
    lo = jnp.where(ci == 0, mine_sum, sib_sum)
    hi = jnp.where(ci == 0, sib_sum, mine_sum)
    gshard = _unpack_shards(jnp.concatenate([lo, hi], axis=0))

    small = dict(loss=loss_acc[0, 0:1], g_final=dg_final[0], lb_logits=dlb_logits,
                 **{n: jnp.stack(v) for n, v in gsmall.items()})
    small_sum = sum_slots(small_allgather(_pack_small(small)), out_dtype=F32, name="sum_small")
    shapes = {n: W[n].shape for n in W}
    shapes["loss"] = (1,)
    shapes["conv_w"] = (DEPTH, 3, D_FF)
    gs = _unpack_small(small_sum, shapes)
    loss = gs["loss"][0]
    cshard = conv_w.shape[2]
    grads = dict(gshard)
    for n in ("g_mix", "lb_logits", "hg_norm_g", "attn_sinks", "g_ffn", "conv_b", "g_ple", "g_final"):
        grads[n] = gs[n]
    grads["conv_w"] = lax.dynamic_slice_in_dim(gs["conv_w"], chip * cshard, cshard, axis=2)

    delta, new_m, new_v = {}, {}, {}
    small_names = ("g_final", "g_mix", "lb_logits", "hg_norm_g", "attn_sinks", "g_ffn", "conv_b", "g_ple")
    sshapes = {n: W[n].shape for n in small_names}

    def pack_s(d):
        z = dict(d)
        z["loss"] = jnp.zeros((1,), F32)
        z["conv_w"] = jnp.zeros((1,), F32)
        return _pack_small(z)

    sd, sm, sv = adamw(pack_s(W), pack_s(grads), pack_s(M), pack_s(V), name="adamw_small")
    for out, buf in ((delta, sd), (new_m, sm), (new_v, sv)):
        un = _unpack_small(buf, {**sshapes, "loss": (1,), "conv_w": (1,)})
        for n in small_names:
            out[n] = un[n]
    for n in ("w_in", "w_out", "w_up", "w_down", "w_ple_gate", "w_ple_up", "conv_w"):
        shp = W[n].shape
        two_d = (-1, shp[-1])
        d_, m_, v_ = adamw(W[n].reshape(two_d), grads[n].reshape(two_d), M[n].reshape(two_d),
                           V[n].reshape(two_d), name="adamw_" + n)
        delta[n], new_m[n], new_v[n] = d_.reshape(shp), m_.reshape(shp), v_.reshape(shp)

    return (loss, grad_x, *[grads[n] for n in WEIGHT_ORDER], *[delta[n] for n in WEIGHT_ORDER],
            *[new_m[n] for n in WEIGHT_ORDER], *[new_v[n] for n in WEIGHT_ORDER])
```

```python
import functools

import jax
import jax.numpy as jnp
from jax import lax
from jax.experimental import pallas as pl
from jax.experimental.pallas import tpu as pltpu

F32 = jnp.float32
BF16 = jnp.bfloat16

D_MODEL = 1024
DEPTH = 4
PLE_DIM = 256
HG_WIDTH = 512
HG_HEADS = 4
HG_DK = 128
HG_CHUNK = 64
HG_SUB = 16
AT_WIDTH = 512
AT_HEAD_DIM = 64
AT_Q_HEADS = 8
AT_KV_HEADS = 2
AT_GROUP = 4
WINDOW = 128
D_FF = 2816
IN_WIDTH = 2816
EPS = 1e-6
MASK_VALUE = -1e30
LB_FLOOR = 1e-30

ADAM_LR = 0.001
ADAM_B1 = 0.9
ADAM_B2 = 0.999
ADAM_EPS = 1e-08
ADAM_WD = 0.01
ADAM_STEP = 10

VMEM_LIMIT = 48 * 1024 * 1024


def _params(*sem):
    return pltpu.CompilerParams(dimension_semantics=sem, vmem_limit_bytes=VMEM_LIMIT)


def _dot(a, b, dims=(((1,), (0,)), ((), ()))):
    return lax.dot_general(a.astype(BF16), b.astype(BF16), dims, preferred_element_type=F32)


def _dot_nt(a, b):
    return _dot(a, b, (((1,), (1,)), ((), ())))


def _dot_tn(a, b):
    return _dot(a, b, (((0,), (0,)), ((), ())))


def _dot_f32(a, b, dims=(((1,), (0,)), ((), ()))):
    return lax.dot_general(a, b, dims, preferred_element_type=F32, precision=lax.Precision.HIGHEST)


def _sigmoid(x):
    return 0.5 * jnp.tanh(0.5 * x) + 0.5


def _logsig(x):
    return jnp.minimum(x, 0.0) - jnp.log(1.0 + jnp.exp(-jnp.abs(x)))


def _colsum(x):
    return jnp.sum(x, axis=0, keepdims=True)


def _rowsum(x):
    return jnp.sum(x, axis=1, keepdims=True)


def _tri(n):
    r = lax.broadcasted_iota(jnp.int32, (n, n), 0)
    c = lax.broadcasted_iota(jnp.int32, (n, n), 1)
    return (r >= c).astype(F32)


def _hg_gates(qp, z, a, c, oml):
    sq = _sigmoid(qp)
    q = qp * sq
    t = c + _logsig(z)
    mx = jnp.maximum(a, t)
    logf = mx + jnp.log(1.0 + jnp.exp(-jnp.abs(a - t)))
    snz = _sigmoid(-z)
    k = oml * snz
    return q, sq, t, logf, snz, k


_HEADS = range(HG_HEADS)


def _lanes(h):
    return slice(h * HG_DK, (h + 1) * HG_DK)


def _head(x, h):
    return x[:, _lanes(h)]


def _row_masks():
    row8 = lax.broadcasted_iota(jnp.int32, (8, HG_DK), 0)
    return [None] + [jnp.where(row8 >= j, 0.0, MASK_VALUE) for j in range(1, 8)]


def _hg_chunk_fwd(q, k, v, logf, st, b_s, k_s, v_s):
    C, U = HG_CHUNK, HG_SUB
    tri = _tri(C)
    b = [_dot_f32(tri, logf[h]) for h in _HEADS]
    for h in _HEADS:
        b_s[h] = b[h]
        k_s[h] = k[h]
        v_s[h] = v[h]
    o = [_dot_nt(q[h] * jnp.exp(b[h]), st[h]) for h in _HEADS]
    bl = [b[h][C - 1:C] for h in _HEADS]
    upd = [_dot_tn(v[h], k[h] * jnp.exp(bl[h] - b[h])) for h in _HEADS]
    rows = lax.broadcasted_iota(jnp.int32, (C, HG_DK), 0)
    nmask = _row_masks()
    outs = [[] for _ in _HEADS]
    for i in range(C // U):
        lo = i * U
        b_i = [b[h][lo:lo + U] for h in _HEADS]
        q_i = [q[h][lo:lo + U] for h in _HEADS]
        o_i = [o[h][lo:lo + U] for h in _HEADS]
        if i > 0:
            qe = [q_i[h] * jnp.exp(b_i[h] - b_i[h][0:1]) for h in _HEADS]
            ke = [jnp.where(rows < lo, k[h] * jnp.exp(jnp.minimum(b_i[h][0:1] - b[h], 0.0)), 0.0) for h in _HEADS]
            att = [_dot_nt(qe[h], ke[h]) for h in _HEADS]
            off = [_dot(att[h], v[h]) for h in _HEADS]
            o_i = [o_i[h] + off[h] for h in _HEADS]
        pieces = [[o_i[h][8 * f:8 * f + 8] for f in range(U // 8)] for h in _HEADS]
        for s in range(U):
            for f in range(s // 8, U // 8):
                for h in _HEADS:
                    bs = b_s[h, lo + s:lo + s + 1, :]
                    ks = k_s[h, lo + s:lo + s + 1, :]
                    vs = v_s[h, lo + s:lo + s + 1, :]
                    arg = b_i[h][8 * f:8 * f + 8] - bs
                    if s > 8 * f:
                        arg = arg + nmask[s - 8 * f]
                    w = _rowsum(q_i[h][8 * f:8 * f + 8] * jnp.exp(arg) * ks)
                    pieces[h][f] = pieces[h][f] + w * vs
        for h in _HEADS:
            outs[h] += pieces[h]
    o = [jnp.concatenate(outs[h], axis=0) for h in _HEADS]
    st_new = [st[h] * jnp.exp(bl[h]) + upd[h] for h in _HEADS]
    return o, st_new, b


def _hg_post(o, gp, ng):
    rs = lax.rsqrt(jnp.mean(o * o, axis=1, keepdims=True) + EPS)
    sg = _sigmoid(gp)
    return o * rs * ng * sg, rs, sg


def hgrn_fwd(proj, lbrows, ng, y_width, *, rows):
    S = proj.shape[0]
    C = HG_CHUNK
    cpb = rows // C
    nb = S // rows

    def body(qp_ref, z_ref, v_ref, gp_ref, lb_ref, ng_ref, y_ref, o_ref, st_ref, st, b_s, k_s, v_s):
        @pl.when(pl.program_id(0) == 0)
        def _():
            st[...] = jnp.zeros_like(st)

        a, c, oml = lb_ref[0:1, :], lb_ref[1:2, :], lb_ref[2:3, :]
        ngr = ng_ref[...]

        def chunk(ci, carry):
            off = pl.multiple_of(ci * C, C)
            sl = pl.ds(off, C)
            for h in _HEADS:
                st_ref[h, ci] = st[h]
            gates = [_hg_gates(qp_ref[sl, _lanes(h)], z_ref[sl, _lanes(h)],
                               _head(a, h), _head(c, h), _head(oml, h)) for h in _HEADS]
            q = [g[0] for g in gates]
            logf = [g[3] for g in gates]
            k = [g[5] for g in gates]
            v = [v_ref[sl, _lanes(h)] for h in _HEADS]
            o, st_new, _ = _hg_chunk_fwd(q, k, v, logf, [st[h] for h in _HEADS], b_s, k_s, v_s)
            for h in _HEADS:
                y, _, _ = _hg_post(o[h], gp_ref[sl, _lanes(h)], _head(ngr, h))
                y_ref[sl, _lanes(h)] = y.astype(y_ref.dtype)
                o_ref[sl, _lanes(h)] = o[h]
                st[h] = st_new[h]
            return carry

        lax.fori_loop(0, cpb, chunk, 0)

    col = lambda kblk: pl.BlockSpec((rows, HG_WIDTH), lambda r: (r, kblk))
    return pl.pallas_call(
        body,
        name="hgrn_fwd",
        grid=(nb,),
        in_specs=[col(0), col(1), col(2), col(3), _const_spec((8, HG_WIDTH)), _const_spec((1, HG_WIDTH))],
        out_specs=[col(0), col(0),
                   pl.BlockSpec((HG_HEADS, cpb, HG_DK, HG_DK), lambda r: (0, r, 0, 0))],
        out_shape=[jax.ShapeDtypeStruct((S, y_width), BF16),
                   jax.ShapeDtypeStruct((S, HG_WIDTH), F32),
                   jax.ShapeDtypeStruct((HG_HEADS, S // C, HG_DK, HG_DK), F32)],
        scratch_shapes=[pltpu.VMEM((HG_HEADS, HG_DK, HG_DK), F32)] + [pltpu.VMEM((HG_HEADS, C, HG_DK), F32)] * 3,
        compiler_params=_params("arbitrary"),
    )(proj, proj, proj, proj, lbrows, ng)


def hgrn_bwd(proj, o_raw, states, dy, lbrows, ng, *, rows):
    S = proj.shape[0]
    C, U = HG_CHUNK, HG_SUB
    cpb = rows // C
    nb = S // rows

    def body(qp_ref, z_ref, v_ref, gp_ref, o_ref, st_ref, dy_ref, lb_ref, ng_ref,
             dqp_ref, dz_ref, dv_ref, dgp_ref, dlb_ref, dng_ref,
             dst, b_s, k_s, v_s, dbs, dks, dvs):
        @pl.when(pl.program_id(0) == 0)
        def _():
            dst[...] = jnp.zeros_like(dst)
            dlb_ref[...] = jnp.zeros_like(dlb_ref)
            dng_ref[...] = jnp.zeros_like(dng_ref)

        a, c, oml = lb_ref[0:1, :], lb_ref[1:2, :], lb_ref[2:3, :]
        ngr = ng_ref[...]
        rows_i = lax.broadcasted_iota(jnp.int32, (C, HG_DK), 0)
        nmask = _row_masks()
        tri = _tri(C)
        H = _HEADS

        def chunk(cj, carry):
            ci = cpb - 1 - cj
            off = pl.multiple_of(ci * C, C)
            sl = pl.ds(off, C)
            qp = [qp_ref[sl, _lanes(h)] for h in H]
            v = [v_ref[sl, _lanes(h)] for h in H]
            st = [st_ref[h, ci] for h in H]
            gates = [_hg_gates(qp[h], z_ref[sl, _lanes(h)], _head(a, h), _head(c, h), _head(oml, h)) for h in H]
            q, sq, t, logf, snz, k = ([g[j] for g in gates] for j in range(6))
            b = [_dot_f32(tri, logf[h]) for h in H]
            for h in H:
                b_s[h] = b[h]
                k_s[h] = k[h]
                v_s[h] = v[h]
            do = []
            for h in H:
                o = o_ref[sl, _lanes(h)]
                dyv = dy_ref[sl, _lanes(h)]
                ngh = _head(ngr, h)
                rs = lax.rsqrt(jnp.mean(o * o, axis=1, keepdims=True) + EPS)
                sg = _sigmoid(gp_ref[sl, _lanes(h)])
                xh = o * rs
                dgp_ref[sl, _lanes(h)] = (dyv * xh * ngh * sg * (1.0 - sg)).astype(dgp_ref.dtype)
                don = dyv * sg
                dng_ref[0:1, _lanes(h)] += _colsum(don * xh)
                dxh = don * ngh
                do.append(rs * (dxh - xh * jnp.mean(dxh * xh, axis=1, keepdims=True)))
            eb = [jnp.exp(b[h]) for h in H]
            qb = [q[h] * eb[h] for h in H]
            dstv = [dst[h] for h in H]
            bl = [b[h][C - 1:C] for h in H]
            el = [jnp.exp(bl[h]) for h in H]
            ex = [jnp.exp(bl[h] - b[h]) for h in H]
            kd = [k[h] * ex[h] for h in H]
            dqb = [_dot(do[h], st[h]) for h in H]
            dst_acc = [_dot_tn(do[h], qb[h]) for h in H]
            dv0 = [_dot_nt(kd[h], dstv[h]) for h in H]
            dkd = [_dot(v[h], dstv[h]) for h in H]
            dq = [dqb[h] * eb[h] for h in H]
            for h in H:
                g2 = dkd[h] * kd[h]
                dbl = _colsum(dstv[h] * st[h]) * el[h] + _colsum(g2)
                dst[h] = dstv[h] * el[h] + dst_acc[h]
                dbs[h] = dqb[h] * qb[h] - g2
                dks[h] = dkd[h] * ex[h]
                dvs[h] = dv0[h]
                dbs[h, C - 1:C, :] += dbl
            dq_parts = [[] for _ in H]
            for i in range(C // U):
                lo = i * U
                b_i = [b[h][lo:lo + U] for h in H]
                q_i = [q[h][lo:lo + U] for h in H]
                do_i = [do[h][lo:lo + U] for h in H]
                dq_i = [dq[h][lo:lo + U] for h in H]
                db_i = [jnp.zeros((U, HG_DK), F32) for _ in H]
                if i > 0:
                    e1 = [jnp.exp(b_i[h] - b_i[h][0:1]) for h in H]
                    qe = [q_i[h] * e1[h] for h in H]
                    e2 = [jnp.where(rows_i < lo, jnp.exp(jnp.minimum(b_i[h][0:1] - b[h], 0.0)), 0.0) for h in H]
                    ke = [k[h] * e2[h] for h in H]
                    att = [_dot_nt(qe[h], ke[h]) for h in H]
                    datt = [_dot_nt(do_i[h], v[h]) for h in H]
                    dv_add = [_dot_tn(att[h], do_i[h]) for h in H]
                    dqe = [_dot(datt[h], ke[h]) for h in H]
                    dke = [_dot_tn(datt[h], qe[h]) for h in H]
                    for h in H:
                        dvs[h] += dv_add[h]
                        dq_i[h] = dq_i[h] + dqe[h] * e1[h]
                        g = dqe[h] * qe[h]
                        db_i[h] = db_i[h] + g
                        gk = dke[h] * ke[h]
                        dks[h] += dke[h] * e2[h]
                        dbs[h] -= gk
                        dbs[h, lo:lo + 1, :] += _colsum(gk) - _colsum(g)
                nf = U // 8
                dq8 = [[dq_i[h][8 * f:8 * f + 8] for f in range(nf)] for h in H]
                db8 = [[db_i[h][8 * f:8 * f + 8] for f in range(nf)] for h in H]
                for s in range(U):
                    row = slice(lo + s, lo + s + 1)
                    for h in H:
                        bs = b_s[h, row, :]
                        ks = k_s[h, row, :]
                        vs = v_s[h, row, :]
                        tv = tk = tb = None
                        for f in range(s // 8, nf):
                            p8 = slice(8 * f, 8 * f + 8)
                            arg = b_i[h][p8] - bs
                            if s > 8 * f:
                                arg = arg + nmask[s - 8 * f]
                            dec = jnp.exp(arg)
                            qd = q_i[h][p8] * dec
                            y_ = qd * ks
                            w = _rowsum(y_)
                            dw = _rowsum(do_i[h][p8] * vs)
                            g = dw * y_
                            dq8[h][f] = dq8[h][f] + dw * dec * ks
                            db8[h][f] = db8[h][f] + g
                            cv, ck = w * do_i[h][p8], dw * qd
                            tv, tk, tb = (cv, ck, g) if tv is None else (tv + cv, tk + ck, tb + g)
                        dvs[h, row, :] += _colsum(tv)
                        dks[h, row, :] += _colsum(tk)
                        dbs[h, row, :] -= _colsum(tb)
                for h in H:
                    for f in range(nf):
                        dbs[h, lo + 8 * f:lo + 8 * f + 8, :] += db8[h][f]
                    dq_parts[h] += dq8[h]
            dlogf = [_dot_f32(tri, dbs[h], (((0,), (0,)), ((), ()))) for h in H]
            for h in H:
                dqh = jnp.concatenate(dq_parts[h], axis=0)
                dk = dks[h]
                ah, omlh = _head(a, h), _head(oml, h)
                pa = jnp.exp(ah - logf[h])
                pt = jnp.exp(t[h] - logf[h])
                dt = dlogf[h] * pt
                dlb_ref[0:1, _lanes(h)] += _colsum(dlogf[h] * pa)
                dlb_ref[1:2, _lanes(h)] += _colsum(dt)
                dlb_ref[2:3, _lanes(h)] += _colsum(dk * snz[h])
                dz = dt * snz[h] - dk * omlh * snz[h] * (1.0 - snz[h])
                dqp = dqh * (sq[h] + qp[h] * sq[h] * (1.0 - sq[h]))
                dqp_ref[sl, _lanes(h)] = dqp.astype(dqp_ref.dtype)
                dz_ref[sl, _lanes(h)] = dz.astype(dz_ref.dtype)
                dv_ref[sl, _lanes(h)] = dvs[h].astype(dv_ref.dtype)
            return carry

        lax.fori_loop(0, cpb, chunk, 0)

    rev = lambda r: nb - 1 - r
    col = lambda kblk: pl.BlockSpec((rows, HG_WIDTH), lambda r: (rev(r), kblk))
    acc = _const_spec((8, HG_WIDTH))
    return pl.pallas_call(
        body,
        name="hgrn_bwd",
        grid=(nb,),
        in_specs=[col(0), col(1), col(2), col(3), col(0),
                  pl.BlockSpec((HG_HEADS, cpb, HG_DK, HG_DK), lambda r: (0, rev(r), 0, 0)),
                  col(0), acc, _const_spec((1, HG_WIDTH))],
        out_specs=[col(0)] * 4 + [acc, acc],
        out_shape=[jax.ShapeDtypeStruct((S, HG_WIDTH), BF16)] * 4
                  + [jax.ShapeDtypeStruct((8, HG_WIDTH), F32)] * 2,
        scratch_shapes=[pltpu.VMEM((HG_HEADS, HG_DK, HG_DK), F32)] + [pltpu.VMEM((HG_HEADS, C, HG_DK), F32)] * 6,
        compiler_params=_params("arbitrary"),
    )(proj, proj, proj, proj, o_raw, states, dy, lbrows, ng)


GROUP_LANES = AT_GROUP * WINDOW


def _swa_valid_t(n):
    W = WINDOW
    kpos = lax.broadcasted_iota(jnp.int32, (2 * W, GROUP_LANES), 0)
    qpos = (lax.broadcasted_iota(jnp.int32, (2 * W, GROUP_LANES), 1) & (W - 1)) + W
    rel = qpos - kpos
    return (rel >= 0) & (rel < W) & jnp.logical_not((n == 0) & (kpos < W))


def _group_lanes(xt, g):
    Dh = AT_HEAD_DIM
    return jnp.concatenate([xt[(g * AT_GROUP + j) * Dh:(g * AT_GROUP + j + 1) * Dh] for j in range(AT_GROUP)],
                           axis=1)


def _swa_probs_t(kg, qg, sink_row, valid):
    s = _dot(kg, qg) * (AT_HEAD_DIM ** -0.5)
    s = jnp.where(valid, s, MASK_VALUE)
    m = jnp.maximum(jnp.max(s, axis=0, keepdims=True), sink_row)
    e = jnp.exp(s - m)
    es = jnp.exp(sink_row - m)
    inv = 1.0 / (_colsum(e) + es)
    return e * inv, es * inv


def swa_fwd(proj, sink_rows, y):
    S = proj.shape[0]
    W, Dh = WINDOW, AT_HEAD_DIM
    nb = S // W

    def body(q_ref, kp_ref, k_ref, vp_ref, v_ref, sk_ref, y_in, y_ref):
        del y_in
        valid = _swa_valid_t(pl.program_id(0))
        kk = jnp.concatenate([kp_ref[...], k_ref[...]], axis=0)
        vt = jnp.concatenate([vp_ref[...], v_ref[...]], axis=0).T
        qt = q_ref[...].T
        outs = []
        for g in range(AT_KV_HEADS):
            p, _ = _swa_probs_t(kk[:, g * Dh:(g + 1) * Dh], _group_lanes(qt, g),
                                sk_ref[8 * g:8 * g + 1, :], valid)
            ot = _dot(vt[g * Dh:(g + 1) * Dh], p)
            outs += [ot[:, j * W:(j + 1) * W] for j in range(AT_GROUP)]
        y_ref[...] = jnp.concatenate(outs, axis=0).T.astype(y_ref.dtype)

    prev = lambda n: jnp.maximum(n - 1, 0)
    return pl.pallas_call(
        body,
        name="swa_fwd",
        grid=(nb,),
        in_specs=[pl.BlockSpec((W, AT_WIDTH), lambda n: (n, 4)),
                  pl.BlockSpec((W, 128), lambda n: (prev(n), 20)),
                  pl.BlockSpec((W, 128), lambda n: (n, 20)),
                  pl.BlockSpec((W, 128), lambda n: (prev(n), 21)),
                  pl.BlockSpec((W, 128), lambda n: (n, 21)),
                  pl.BlockSpec((8 * AT_KV_HEADS, GROUP_LANES), lambda n: (0, 0)),
                  pl.BlockSpec(memory_space=pl.ANY)],
        out_specs=pl.BlockSpec((W, AT_WIDTH), lambda n: (n, 1)),
        out_shape=jax.ShapeDtypeStruct(y.shape, y.dtype),
        input_output_aliases={6: 0},
        compiler_params=_params("parallel"),
    )(proj, proj, proj, proj, proj, sink_rows, y)


def swa_bwd(proj, sink_rows, dy):
    S = proj.shape[0]
    W, Dh = WINDOW, AT_HEAD_DIM
    nb = S // W
    scale = Dh ** -0.5

    def body(q_ref, kp_ref, k_ref, vp_ref, v_ref, sk_ref, dy_ref,
             dq_ref, dko_ref, dkp_ref, dvo_ref, dvp_ref, dsk_ref):
        n = pl.program_id(0)

        @pl.when(n == 0)
        def _():
            dsk_ref[...] = jnp.zeros_like(dsk_ref)

        valid = _swa_valid_t(n)
        kk = jnp.concatenate([kp_ref[...], k_ref[...]], axis=0)
        vv = jnp.concatenate([vp_ref[...], v_ref[...]], axis=0)
        kt = kk.T
        qt = q_ref[...].T
        dot_ = dy_ref[...].T
        dqs, dks, dvs = [], [], []
        for g in range(AT_KV_HEADS):
            qg = _group_lanes(qt, g)
            dog = _group_lanes(dot_, g)
            vg = vv[:, g * Dh:(g + 1) * Dh]
            p, ps = _swa_probs_t(kk[:, g * Dh:(g + 1) * Dh], qg, sk_ref[8 * g:8 * g + 1, :], valid)
            dp = _dot(vg, dog)
            delta = _colsum(dp * p)
            ds = p * (dp - delta) * scale
            dqt = _dot(kt[g * Dh:(g + 1) * Dh], ds)
            dqs += [dqt[:, j * W:(j + 1) * W] for j in range(AT_GROUP)]
            dks.append(_dot_nt(ds, qg))
            dvs.append(_dot_nt(p, dog))
            dsk_ref[8 * g:8 * g + 1, :] += -(ps * delta)
        dq_ref[...] = jnp.concatenate(dqs, axis=0).T.astype(dq_ref.dtype)
        dk = jnp.concatenate(dks, axis=1)
        dv = jnp.concatenate(dvs, axis=1)
        dkp_ref[...] = dk[:W]
        dko_ref[...] = dk[W:]
        dvp_ref[...] = dv[:W]
        dvo_ref[...] = dv[W:]

        @pl.when(n == nb - 1)
        def _():
            for g in range(AT_KV_HEADS):
                for j in range(AT_GROUP):
                    tot = _rowsum(dsk_ref[8 * g:8 * g + 1, j * W:(j + 1) * W])
                    dsk_ref[8 * g + 1 + j:8 * g + 2 + j, :] = jnp.broadcast_to(tot, (1, GROUP_LANES))

    prev = lambda n: jnp.maximum(n - 1, 0)
    kv = pl.BlockSpec((W, 128), lambda n: (n, 0))
    sk = pl.BlockSpec((8 * AT_KV_HEADS, GROUP_LANES), lambda n: (0, 0))
    return pl.pallas_call(
        body,
        name="swa_bwd",
        grid=(nb,),
        in_specs=[pl.BlockSpec((W, AT_WIDTH), lambda n: (n, 4)),
                  pl.BlockSpec((W, 128), lambda n: (prev(n), 20)),
                  pl.BlockSpec((W, 128), lambda n: (n, 20)),
                  pl.BlockSpec((W, 128), lambda n: (prev(n), 21)),
                  pl.BlockSpec((W, 128), lambda n: (n, 21)),
                  sk,
                  pl.BlockSpec((W, AT_WIDTH), lambda n: (n, 1))],
        out_specs=[pl.BlockSpec((W, AT_WIDTH), lambda n: (n, 0)), kv, kv, kv, kv, sk],
        out_shape=[jax.ShapeDtypeStruct((S, AT_WIDTH), BF16)]
                  + [jax.ShapeDtypeStruct((S, 128), F32)] * 4
                  + [jax.ShapeDtypeStruct((8 * AT_KV_HEADS, GROUP_LANES), F32)],
        compiler_params=_params("arbitrary"),
    )(proj, proj, proj, proj, proj, sink_rows, dy)


def assemble_dproj(hg_grads, dq_at, dko, dkp, dvo, dvp, *, rows):
    S = dq_at.shape[0]
    W = WINDOW
    nb = S // W
    bpr = rows // W

    def body(a0, a1, a2, a3, dq, ko, kp, kpn, vo, vp, vpn, out):
        r = pl.program_id(0)
        for i, a in enumerate((a0, a1, a2, a3)):
            out[:, i * HG_WIDTH:(i + 1) * HG_WIDTH] = a[...]
        base = 4 * HG_WIDTH
        out[:, base:base + AT_WIDTH] = dq[...]
        last = (r == pl.num_programs(0) - 1)
        for off, own, pv, pvn in ((base + AT_WIDTH, ko, kp, kpn), (base + AT_WIDTH + 128, vo, vp, vpn)):
            if bpr > 1:
                out[0:rows - W, off:off + 128] = (own[0:rows - W, :] + pv[W:rows, :]).astype(out.dtype)
            nxt = jnp.where(last, 0.0, pvn[...])
            out[rows - W:rows, off:off + 128] = (own[rows - W:rows, :] + nxt).astype(out.dtype)

    hg = pl.BlockSpec((rows, HG_WIDTH), lambda r: (r, 0))
    blk = pl.BlockSpec((rows, 128), lambda r: (r, 0))
    nxt = pl.BlockSpec((W, 128), lambda r: (jnp.minimum((r + 1) * bpr, nb - 1), 0))
    return pl.pallas_call(
        body,
        name="assemble_dproj",
        grid=(S // rows,),
        in_specs=[hg, hg, hg, hg, pl.BlockSpec((rows, AT_WIDTH), lambda r: (r, 0)),
                  blk, blk, nxt, blk, blk, nxt],
        out_specs=pl.BlockSpec((rows, IN_WIDTH), lambda r: (r, 0)),
        out_shape=jax.ShapeDtypeStruct((S, IN_WIDTH), BF16),
        compiler_params=_params("parallel"),
    )(*hg_grads, dq_at, dko, dkp, dkp, dvo, dvp, dvp)


ROW_TILE = 512
COL_TILE = 1408


def _col_tile(n):
    return n if n <= COL_TILE else COL_TILE


def _rms_scale(x):
    return lax.rsqrt(jnp.mean(x * x, axis=1, keepdims=True) + EPS)


def _rms_bwd(d, x, g):
    rs = _rms_scale(x)
    xh = x * rs
    dxh = d * g
    return rs * (dxh - xh * jnp.mean(dxh * xh, axis=1, keepdims=True)), _colsum(d * xh)


def mm(a, b, *, nt=False, out_dtype=F32, res=None, norm_g=None, rms_bwd=None, name):
    parts = a if isinstance(a, tuple) else (a,)
    M, K = parts[0].shape
    N = b.shape[0] if nt else b.shape[1]
    tall = K <= D_MODEL and rms_bwd is None and len(parts) == 1 and M % (2 * ROW_TILE) == 0
    tm = 2 * ROW_TILE if tall else min(ROW_TILE, M)
    tn = _col_tile(N)
    whole_rows = norm_g is not None or rms_bwd is not None
    assert M % tm == 0 and N % tn == 0 and (tn == N or not whole_rows)
    np_ = len(parts)

    def body(*refs):
        a_refs, b_refs, rest = refs[:np_], refs[np_:2 * np_], refs[2 * np_:]
        dot = _dot_nt if nt else _dot
        acc = dot(a_refs[0][...], b_refs[0][...])
        for ar, br in zip(a_refs[1:], b_refs[1:]):
            acc = acc + dot(ar[...], br[...])
        if rms_bwd is not None:
            h_ref, g_ref, dr_ref, dh_ref, dhb_ref, dg_ref = rest

            @pl.when(pl.program_id(1) == 0)
            def _():
                dg_ref[...] = jnp.zeros_like(dg_ref)

            dx, dgp = _rms_bwd(acc, h_ref[...], g_ref[...])
            dg_ref[0:1, :] += dgp
            dh = dr_ref[...] + dx
            dh_ref[...] = dh
            dhb_ref[...] = dh.astype(BF16)
            return
        rest = list(rest)
        if res is not None:
            acc = acc + rest.pop(0)[...]
        if norm_g is not None:
            g_ref = rest.pop(0)
            rest[1][...] = (acc * _rms_scale(acc) * g_ref[...]).astype(BF16)
        rest[0][...] = acc.astype(rest[0].dtype)

    row = pl.BlockSpec((tm, tn), lambda j, i: (i, j))
    in_specs = [pl.BlockSpec((tm, K), lambda j, i: (i, 0)) for _ in parts]
    for kb in range(np_):
        in_specs.append(pl.BlockSpec((tn, K), lambda j, i, kb=kb: (j, kb)) if nt
                        else pl.BlockSpec((K, tn), lambda j, i, kb=kb: (kb, j)))
    args = list(parts) + [b] * np_
    if rms_bwd is not None:
        h, g, dres = rms_bwd
        in_specs += [row, _const_spec((1, N)), row]
        args += [h, g, dres]
        out_specs = [row, row, _const_spec((8, N))]
        out_shape = [jax.ShapeDtypeStruct((M, N), F32), jax.ShapeDtypeStruct((M, N), BF16),
                     jax.ShapeDtypeStruct((8, N), F32)]
        sem = ("arbitrary", "arbitrary")
    else:
        if res is not None:
            in_specs.append(row)
            args.append(res)
        out_specs, out_shape = [row], [jax.ShapeDtypeStruct((M, N), out_dtype)]
        if norm_g is not None:
            in_specs.append(_const_spec((1, N)))
            args.append(norm_g)
            out_specs.append(row)
            out_shape.append(jax.ShapeDtypeStruct((M, N), BF16))
        sem = ("parallel", "parallel")
    out = pl.pallas_call(
        body,
        name=name,
        grid=(N // tn, M // tm),
        in_specs=in_specs,
        out_specs=out_specs,
        out_shape=out_shape,
        compiler_params=_params(*sem),
    )(*args)
    return out[0] if len(out) == 1 else out


def mm_tn(a, b, *, name):
    M, K = a.shape
    N = b.shape[1]
    tm = next((t for t in (4 * ROW_TILE, 2 * ROW_TILE) if M % t == 0), min(ROW_TILE, M))
    tk = _col_tile(K)
    tn = _col_tile(N)
    assert M % tm == 0 and K % tk == 0 and N % tn == 0

    def body(a_ref, b_ref, o_ref):
        @pl.when(pl.program_id(2) == 0)
        def _():
            o_ref[...] = jnp.zeros_like(o_ref)

        o_ref[...] += _dot_tn(a_ref[...], b_ref[...])

    return pl.pallas_call(
        body,
        name=name,
        grid=(K // tk, N // tn, M // tm),
        in_specs=[pl.BlockSpec((tm, tk), lambda k, j, i: (i, k)),
                  pl.BlockSpec((tm, tn), lambda k, j, i: (i, j))],
        out_specs=pl.BlockSpec((tk, tn), lambda k, j, i: (k, j)),
        out_shape=jax.ShapeDtypeStruct((K, N), F32),
        compiler_params=_params("parallel", "parallel", "arbitrary"),
    )(a, b)


def _row_spec(tm, width):
    return pl.BlockSpec((tm, width), lambda i: (i, 0))


def _const_spec(shape):
    return pl.BlockSpec(shape, lambda *_: (0,) * len(shape))


def rmsnorm_fwd(h, g, *, name):
    S, D = h.shape
    tm = min(ROW_TILE, S)

    def body(h_ref, g_ref, u_ref):
        x = h_ref[...]
        rs = lax.rsqrt(jnp.mean(x * x, axis=1, keepdims=True) + EPS)
        u_ref[...] = (x * rs * g_ref[...]).astype(u_ref.dtype)

    return pl.pallas_call(
        body, name=name, grid=(S // tm,),
        in_specs=[_row_spec(tm, D), _const_spec((1, D))],
        out_specs=_row_spec(tm, D),
        out_shape=jax.ShapeDtypeStruct((S, D), BF16),
        compiler_params=_params("parallel"),
    )(h, g)


def rmsnorm_bwd(du, h, g, dres, *, name):
    S, D = h.shape
    tm = min(ROW_TILE, S)

    def body(du_ref, h_ref, g_ref, dr_ref, dh_ref, dhb_ref, dg_ref):
        @pl.when(pl.program_id(0) == 0)
        def _():
            dg_ref[...] = jnp.zeros_like(dg_ref)

        x = h_ref[...]
        d = du_ref[...]
        rs = lax.rsqrt(jnp.mean(x * x, axis=1, keepdims=True) + EPS)
        xh = x * rs
        dg_ref[0:1, :] += _colsum(d * xh)
        dxh = d * g_ref[...]
        dh = dr_ref[...] + rs * (dxh - xh * jnp.mean(dxh * xh, axis=1, keepdims=True))
        dh_ref[...] = dh
        dhb_ref[...] = dh.astype(BF16)

    return pl.pallas_call(
        body, name=name, grid=(S // tm,),
        in_specs=[_row_spec(tm, D), _row_spec(tm, D), _const_spec((1, D)), _row_spec(tm, D)],
        out_specs=[_row_spec(tm, D), _row_spec(tm, D), _const_spec((8, D))],
        out_shape=[jax.ShapeDtypeStruct((S, D), F32), jax.ShapeDtypeStruct((S, D), BF16),
                   jax.ShapeDtypeStruct((8, D), F32)],
        compiler_params=_params("arbitrary"),
    )(du, h, g, dres)


def _shift_down(x, edge8, s):
    sh = pltpu.roll(x, s, 0)
    er = pltpu.roll(edge8, s, 0)
    row8 = lax.broadcasted_iota(jnp.int32, er.shape, 0)
    top = jnp.where(row8 < s, er, sh[0:8])
    return jnp.concatenate([top, sh[8:]], axis=0)


def _shift_up(x, s):
    return pltpu.roll(x, x.shape[0] - s, 0)


HALO = 16


def _conv_pre(a, prev8, w_ref, cb_ref):
    a1 = _shift_down(a, prev8, 1)
    a2 = _shift_down(a, prev8, 2)
    ac = w_ref[2:3, :] * a + w_ref[1:2, :] * a1 + w_ref[0:1, :] * a2 + cb_ref[...]
    return ac, a1, a2


def convffn_fwd(hh, cw8, cb):
    S = hh.shape[0]
    tm = min(ROW_TILE, S)
    tn = _col_tile(D_FF)
    nj = D_FF // tn

    def body(a_ref, ap_ref, b_ref, w_ref, cb_ref, o_ref):
        prev8 = jnp.where(pl.program_id(1) == 0, 0.0, ap_ref[...].astype(F32)[HALO - 8:HALO])
        ac, _, _ = _conv_pre(a_ref[...].astype(F32), prev8, w_ref, cb_ref)
        o_ref[...] = (ac * _sigmoid(ac) * b_ref[...].astype(F32)).astype(o_ref.dtype)

    rh = tm // HALO
    return pl.pallas_call(
        body, name="convffn_fwd", grid=(nj, S // tm),
        in_specs=[pl.BlockSpec((tm, tn), lambda j, i: (i, j)),
                  pl.BlockSpec((HALO, tn), lambda j, i: (jnp.maximum(i * rh - 1, 0), j)),
                  pl.BlockSpec((tm, tn), lambda j, i: (i, j + nj)),
                  pl.BlockSpec((8, tn), lambda j, i: (0, j)),
                  pl.BlockSpec((1, tn), lambda j, i: (0, j))],
        out_specs=pl.BlockSpec((tm, tn), lambda j, i: (i, j)),
        out_shape=jax.ShapeDtypeStruct((S, D_FF), BF16),
        compiler_params=_params("parallel", "parallel"),
    )(hh, hh, hh, cw8, cb)


def convffn_bwd(hh, dact, cw8, cb):
    S = hh.shape[0]
    tm = min(ROW_TILE, S)
    tn = _col_tile(D_FF)
    nj = D_FF // tn
    ni = S // tm

    def body(a_ref, ap_ref, an_ref, b_ref, bn_ref, d_ref, dn_ref, w_ref, cb_ref, o_a, o_b, dw_ref):
        i = pl.program_id(1)

        @pl.when(i == 0)
        def _():
            dw_ref[...] = jnp.zeros_like(dw_ref)

        up = lambda r: r[...].astype(F32)
        prev8 = jnp.where(i == 0, 0.0, up(ap_ref)[HALO - 8:HALO])
        a = jnp.concatenate([up(a_ref), up(an_ref)[0:8]], axis=0)
        b = jnp.concatenate([up(b_ref), up(bn_ref)[0:8]], axis=0)
        d = jnp.concatenate([up(d_ref), jnp.where(i == ni - 1, 0.0, up(dn_ref)[0:8])], axis=0)
        ac, a1, a2 = _conv_pre(a, prev8, w_ref, cb_ref)
        sa = _sigmoid(ac)
        o_b[...] = (d[0:tm] * ac[0:tm] * sa[0:tm]).astype(o_b.dtype)
        dac = d * b * (sa + ac * sa * (1.0 - sa))
        da = w_ref[2:3, :] * dac + w_ref[1:2, :] * _shift_up(dac, 1) + w_ref[0:1, :] * _shift_up(dac, 2)
        o_a[...] = da[0:tm].astype(o_a.dtype)
        dc = dac[0:tm]
        dw_ref[0:1, :] += _colsum(dc * a2[0:tm])
        dw_ref[1:2, :] += _colsum(dc * a1[0:tm])
        dw_ref[2:3, :] += _colsum(dc * a[0:tm])
        dw_ref[3:4, :] += _colsum(dc)

    rh = tm // HALO
    last = S // HALO - 1
    cur = lambda off: pl.BlockSpec((tm, tn), lambda j, i: (i, j + off))
    nxt = lambda off: pl.BlockSpec((HALO, tn), lambda j, i: (jnp.minimum((i + 1) * rh, last), j + off))
    return pl.pallas_call(
        body, name="convffn_bwd", grid=(nj, ni),
        in_specs=[cur(0),
                  pl.BlockSpec((HALO, tn), lambda j, i: (jnp.maximum(i * rh - 1, 0), j)),
                  nxt(0), cur(nj), nxt(nj), cur(0), nxt(0),
                  pl.BlockSpec((8, tn), lambda j, i: (0, j)),
                  pl.BlockSpec((1, tn), lambda j, i: (0, j))],
        out_specs=[cur(0), cur(0), pl.BlockSpec((8, tn), lambda j, i: (0, j))],
        out_shape=[jax.ShapeDtypeStruct((S, D_FF), BF16), jax.ShapeDtypeStruct((S, D_FF), BF16),
                   jax.ShapeDtypeStruct((8, D_FF), F32)],
        compiler_params=_params("parallel", "arbitrary"),
    )(hh, hh, hh, hh, hh, dact, dact, cw8, cb)


def ple_fwd(h, gpre, p, wpu, norm_g):
    S, D = h.shape
    tm = min(ROW_TILE, S)

    def body(h_ref, g_ref, p_ref, w_ref, ng_ref, o_ref, u_ref):
        out = h_ref[...] + _sigmoid(g_ref[...]) * _dot(p_ref[...], w_ref[...])
        o_ref[...] = out
        u_ref[...] = (out * _rms_scale(out) * ng_ref[...]).astype(BF16)

    return pl.pallas_call(
        body, name="ple_fwd", grid=(S // tm,),
        in_specs=[_row_spec(tm, D), _row_spec(tm, D), _row_spec(tm, PLE_DIM), _const_spec((PLE_DIM, D)),
                  _const_spec((1, D))],
        out_specs=[_row_spec(tm, D), _row_spec(tm, D)],
        out_shape=[jax.ShapeDtypeStruct((S, D), F32), jax.ShapeDtypeStruct((S, D), BF16)],
        compiler_params=_params("parallel"),
    )(h, gpre, p, wpu, norm_g)


def ple_bwd(dh, gpre, p, wpu):
    S, D = dh.shape
    tm = min(ROW_TILE, S)

    def body(d_ref, g_ref, p_ref, w_ref, dpu_ref, dg_ref):
        d = d_ref[...]
        gate = _sigmoid(g_ref[...])
        pu = _dot(p_ref[...], w_ref[...])
        dpu_ref[...] = (d * gate).astype(dpu_ref.dtype)
        dg_ref[...] = (d * pu * gate * (1.0 - gate)).astype(dg_ref.dtype)

    return pl.pallas_call(
        body, name="ple_bwd", grid=(S // tm,),
        in_specs=[_row_spec(tm, D), _row_spec(tm, D), _row_spec(tm, PLE_DIM), _const_spec((PLE_DIM, D))],
        out_specs=[_row_spec(tm, D), _row_spec(tm, D)],
        out_shape=[jax.ShapeDtypeStruct((S, D), BF16)] * 2,
        compiler_params=_params("parallel"),
    )(dh, gpre, p, wpu)


def loss_head(h, g, tgt):
    S, D = h.shape
    tm = min(ROW_TILE, S)

    def body(h_ref, g_ref, t_ref, dh_ref, dhb_ref, l_ref, dg_ref):
        @pl.when(pl.program_id(0) == 0)
        def _():
            l_ref[...] = jnp.zeros_like(l_ref)
            dg_ref[...] = jnp.zeros_like(dg_ref)

        x = h_ref[...]
        gr = g_ref[...]
        rs = lax.rsqrt(jnp.mean(x * x, axis=1, keepdims=True) + EPS)
        xh = x * rs
        err = xh * gr - t_ref[...]
        l_ref[0:1, 0:1] += 0.5 * _colsum(jnp.mean(err * err, axis=1, keepdims=True))
        dy = err * (1.0 / D)
        dg_ref[0:1, :] += _colsum(dy * xh)
        dxh = dy * gr
        dh = rs * (dxh - xh * jnp.mean(dxh * xh, axis=1, keepdims=True))
        dh_ref[...] = dh
        dhb_ref[...] = dh.astype(BF16)

    return pl.pallas_call(
        body, name="loss_head", grid=(S // tm,),
        in_specs=[_row_spec(tm, D), _const_spec((1, D)), _row_spec(tm, D)],
        out_specs=[_row_spec(tm, D), _row_spec(tm, D), _const_spec((8, 128)), _const_spec((8, D))],
        out_shape=[jax.ShapeDtypeStruct((S, D), F32), jax.ShapeDtypeStruct((S, D), BF16),
                   jax.ShapeDtypeStruct((8, 128), F32), jax.ShapeDtypeStruct((8, D), F32)],
        compiler_params=_params("arbitrary"),
    )(h, g, tgt)


def _lb_rows(l_ref):
    l = l_ref[...]
    e = jnp.exp(l - jnp.max(l, axis=0, keepdims=True))
    p = e / _colsum(e)
    lbs, run = [], None
    for i in range(DEPTH):
        run = p[i:i + 1] if i == 0 else run + p[i:i + 1]
        lbs.append(run - p[0:1])
    return p, lbs


def lb_fwd(lb_logits):
    def body(l_ref, o_ref):
        _, lbs = _lb_rows(l_ref)
        o_ref[...] = jnp.zeros_like(o_ref)
        for i, lb in enumerate(lbs):
            o_ref[8 * i:8 * i + 1, :] = jnp.log(jnp.maximum(lb, LB_FLOOR))
            o_ref[8 * i + 1:8 * i + 2, :] = jnp.log1p(-lb)
            o_ref[8 * i + 2:8 * i + 3, :] = 1.0 - lb
            o_ref[8 * i + 3:8 * i + 4, :] = lb

    return pl.pallas_call(
        body, name="lb_fwd",
        out_shape=jax.ShapeDtypeStruct((DEPTH * 8, HG_WIDTH), F32),
    )(lb_logits)


def lb_bwd(dlbrows, lb_logits):
    def body(d_ref, l_ref, o_ref):
        p, lbs = _lb_rows(l_ref)
        dlb = []
        for i, lb in enumerate(lbs):
            da = d_ref[8 * i:8 * i + 1, :]
            dc = d_ref[8 * i + 1:8 * i + 2, :]
            do = d_ref[8 * i + 2:8 * i + 3, :]
            dlb.append(jnp.where(lb > LB_FLOOR, da / jnp.maximum(lb, LB_FLOOR), 0.0) - dc / (1.0 - lb) - do)
        dp = [jnp.zeros_like(dlb[0])]
        for j in range(1, DEPTH):
            acc = dlb[j]
            for i in range(j + 1, DEPTH):
                acc = acc + dlb[i]
            dp.append(acc)
        dot_ = p[0:1] * dp[0]
        for j in range(1, DEPTH):
            dot_ = dot_ + p[j:j + 1] * dp[j]
        o_ref[...] = jnp.zeros_like(o_ref)
        for j in range(DEPTH):
            o_ref[j:j + 1, :] = p[j:j + 1] * (dp[j] - dot_)

    return pl.pallas_call(
        body, name="lb_bwd",
        out_shape=jax.ShapeDtypeStruct((8, HG_WIDTH), F32),
    )(dlbrows, lb_logits)


def adamw(w, g, m, v, *, name):
    R, C = w.shape
    tr = next((t for t in (512, 256, 128, 64, 32, 16, 8) if R % t == 0), R)

    def body(w_ref, g_ref, m_ref, v_ref, d_ref, m2_ref, v2_ref):
        gv = g_ref[...]
        m2 = ADAM_B1 * m_ref[...] + (1.0 - ADAM_B1) * gv
        v2 = ADAM_B2 * v_ref[...] + (1.0 - ADAM_B2) * (gv * gv)
        mh = m2 / (1.0 - ADAM_B1 ** ADAM_STEP)
        vh = v2 / (1.0 - ADAM_B2 ** ADAM_STEP)
        d_ref[...] = -ADAM_LR * (mh / (jnp.sqrt(vh) + ADAM_EPS) + ADAM_WD * w_ref[...])
        m2_ref[...] = m2
        v2_ref[...] = v2

    spec = pl.BlockSpec((tr, C), lambda i: (i, 0))
    return pl.pallas_call(
        body, name=name, grid=(R // tr,),
        in_specs=[spec] * 4, out_specs=[spec] * 3,
        out_shape=[jax.ShapeDtypeStruct((R, C), F32)] * 3,
        compiler_params=_params("parallel"),
    )(w, g, m, v)


def sum_slots(x, *, out_dtype, name):
    n, R, C = x.shape
    tr = 848 if R % 848 == 0 else R

    def body(x_ref, o_ref):
        acc = x_ref[0].astype(F32)
        for k in range(1, n):
            acc = acc + x_ref[k].astype(F32)
        o_ref[...] = acc.astype(o_ref.dtype)

    return pl.pallas_call(
        body, name=name, grid=(R // tr,),
        in_specs=[pl.BlockSpec((n, tr, C), lambda i: (0, i, 0))],
        out_specs=pl.BlockSpec((tr, C), lambda i: (i, 0)),
        out_shape=jax.ShapeDtypeStruct((R, C), out_dtype),
        compiler_params=_params("parallel"),
    )(x)


MESH = pl.DeviceIdType.MESH
ANY = pl.BlockSpec(memory_space=pl.ANY)


def _place():
    return lax.axis_index("x"), lax.axis_index("y"), lax.axis_index("c")


def _other_chips(x, y):
    return [(1 - x, y), (x, 1 - y), (1 - x, 1 - y)]


def small_allgather(buf):
    R, C = buf.shape

    def body(x_ref, out_ref, send_sems, recv_sems, local_sem):
        x, y, c = _place()
        me, sibling = (x, y, c), (x, y, 1 - c)
        chips = _other_chips(x, y)

        def slot(px, py, pc):
            return out_ref.at[4 * px + 2 * py + pc]

        def copy(k, block, to, src=None):
            return pltpu.make_async_remote_copy(
                src_ref=slot(*block) if src is None else src, dst_ref=slot(*block),
                send_sem=send_sems.at[k], recv_sem=recv_sems.at[k],
                device_id=to, device_id_type=MESH)

        mine = pltpu.make_async_copy(x_ref, slot(*me), local_sem)
        mine.start()
        first = [copy(0, me, sibling, src=x_ref)]
        first += [copy(1 + r, me, (*chip, c), src=x_ref) for r, chip in enumerate(chips)]
        for cp in first:
            cp.start()
        passed = [copy(4 + r, (*chip, c), sibling) for r, chip in enumerate(chips)]
        for r, chip in enumerate(chips):
            copy(1 + r, (*chip, c), me).wait_recv()
            passed[r].start()
        copy(0, sibling, me).wait_recv()
        for r, chip in enumerate(chips):
            copy(4 + r, (*chip, 1 - c), me).wait_recv()
        for cp in first + passed:
            cp.wait_send()
        mine.wait()

    return pl.pallas_call(
        body, name="small_allgather",
        out_shape=jax.ShapeDtypeStruct((8, R, C), buf.dtype),
        in_specs=[pl.BlockSpec(memory_space=pltpu.VMEM)],
        out_specs=pl.BlockSpec(memory_space=pltpu.VMEM),
        scratch_shapes=[pltpu.SemaphoreType.DMA((7,)), pltpu.SemaphoreType.DMA((7,)),
                        pltpu.SemaphoreType.DMA],
    )(buf)


def weights_allgather(wp):
    R, C = wp.shape
    half = R // 2

    def body(w_ref, g_ref, send_sems, recv_sems, local_sem):
        x, y, c = _place()
        j = 2 * x + y
        sibling = (x, y, 1 - c)
        chips = _other_chips(x, y)
        mine = pl.ds(pl.multiple_of(c * half, 16), half)
        theirs = pl.ds(pl.multiple_of((1 - c) * half, 16), half)

        def copy(k, chip_block, rows, to, src=None):
            dst = g_ref.at[chip_block, rows]
            return pltpu.make_async_remote_copy(
                src_ref=dst if src is None else src, dst_ref=dst,
                send_sem=send_sems.at[k], recv_sem=recv_sems.at[k],
                device_id=to, device_id_type=MESH)

        own = pltpu.make_async_copy(w_ref, g_ref.at[j], local_sem)
        own.start()
        first = [copy(r, j, mine, (*chip, c), src=w_ref.at[mine]) for r, chip in enumerate(chips)]
        for cp in first:
            cp.start()
        passed = [copy(3 + r, 2 * chip[0] + chip[1], mine, sibling) for r, chip in enumerate(chips)]
        for r, chip in enumerate(chips):
            copy(r, 2 * chip[0] + chip[1], mine, (*chip, c)).wait_recv()
            passed[r].start()
        for r, chip in enumerate(chips):
            copy(3 + r, 2 * chip[0] + chip[1], theirs, sibling).wait_recv()
        for cp in first + passed:
            cp.wait_send()
        own.wait()

    return pl.pallas_call(
        body, name="weights_allgather",
        out_shape=jax.ShapeDtypeStruct((4, R, C), wp.dtype),
        in_specs=[ANY], out_specs=ANY,
        scratch_shapes=[pltpu.SemaphoreType.DMA((6,)), pltpu.SemaphoreType.DMA((6,)),
                        pltpu.SemaphoreType.DMA],
    )(wp)


def sibling_swap(v, *, name):
    def body(v_ref, got_ref, send_sem, recv_sem):
        x, y, c = _place()
        cp = pltpu.make_async_remote_copy(
            src_ref=v_ref, dst_ref=got_ref, send_sem=send_sem, recv_sem=recv_sem,
            device_id=(x, y, 1 - c), device_id_type=MESH)
        cp.start()
        cp.wait()

    return pl.pallas_call(
        body, name=name,
        out_shape=jax.ShapeDtypeStruct(v.shape, v.dtype),
        in_specs=[ANY], out_specs=ANY,
        scratch_shapes=[pltpu.SemaphoreType.DMA, pltpu.SemaphoreType.DMA],
    )(v)


def chip_exchange(q):
    def body(q_ref, r_ref, send_sems, recv_sems, local_sem):
        x, y, c = _place()
        j = 2 * x + y
        chips = _other_chips(x, y)
        own = pltpu.make_async_copy(q_ref.at[j], r_ref.at[j], local_sem)
        own.start()

        def copy(r, src_block, dst_block, chip):
            return pltpu.make_async_remote_copy(
                src_ref=q_ref.at[src_block], dst_ref=r_ref.at[dst_block],
                send_sem=send_sems.at[r], recv_sem=recv_sems.at[r],
                device_id=(*chip, c), device_id_type=MESH)

        sends = [copy(r, 2 * chip[0] + chip[1], j, chip) for r, chip in enumerate(chips)]
        for cp in sends:
            cp.start()
        for r, chip in enumerate(chips):
            jr = 2 * chip[0] + chip[1]
            copy(r, jr, jr, chip).wait_recv()
        for cp in sends:
            cp.wait_send()
        own.wait()

    return pl.pallas_call(
        body, name="chip_exchange",
        out_shape=jax.ShapeDtypeStruct(q.shape, q.dtype),
        in_specs=[ANY], out_specs=ANY,
        scratch_shapes=[pltpu.SemaphoreType.DMA((3,)), pltpu.SemaphoreType.DMA((3,)),
                        pltpu.SemaphoreType.DMA],
    )(q)


N_CHIPS = 4
_PACK = (("w_in", 704), ("w_out", 256), ("w_up", 1408), ("w_down", 704), ("w_ple_gate", 256), ("w_ple_up", 64))
LAYER_ROWS = sum(r for _, r in _PACK)
PACK_ROWS = DEPTH * LAYER_ROWS
HALF_ROWS = PACK_ROWS // 2


def _pack_shards(sh):
    parts = []
    for i in range(DEPTH):
        for name, rows in _PACK:
            parts.append(sh[name][i].reshape(rows, D_MODEL))
    return jnp.concatenate(parts, axis=0)


def _unpack_shards(slab):
    shapes = {"w_in": (D_MODEL, IN_WIDTH // N_CHIPS), "w_out": (D_MODEL // N_CHIPS, D_MODEL),
              "w_up": (D_MODEL, 2 * D_FF // N_CHIPS), "w_down": (D_FF // N_CHIPS, D_MODEL),
              "w_ple_gate": (D_MODEL // N_CHIPS, D_MODEL), "w_ple_up": (PLE_DIM, D_MODEL // N_CHIPS)}
    out = {name: [] for name, _ in _PACK}
    off = 0
    for i in range(DEPTH):
        for name, rows in _PACK:
            out[name].append(slab[off:off + rows].reshape(shapes[name]))
            off += rows
    return {k: jnp.stack(v) for k, v in out.items()}


_COL_SHARDED = ("w_in", "w_up", "w_ple_up")


def _full_from_chips(g, layer):
    per_chip = [_unpack_shards_layer(g[k], layer) for k in range(N_CHIPS)]
    return {name: jnp.concatenate([pc[name] for pc in per_chip], axis=1 if name in _COL_SHARDED else 0)
            for name, _ in _PACK}


def _unpack_shards_layer(slab, layer):
    shapes = {"w_in": (D_MODEL, IN_WIDTH // N_CHIPS), "w_out": (D_MODEL // N_CHIPS, D_MODEL),
              "w_up": (D_MODEL, 2 * D_FF // N_CHIPS), "w_down": (D_FF // N_CHIPS, D_MODEL),
              "w_ple_gate": (D_MODEL // N_CHIPS, D_MODEL), "w_ple_up": (PLE_DIM, D_MODEL // N_CHIPS)}
    out = {}
    off = layer * LAYER_ROWS
    for name, rows in _PACK:
        out[name] = slab[off:off + rows].reshape(shapes[name])
        off += rows
    return out


def _split_to_chips(full, name):
    r, c = full.shape
    if name in _COL_SHARDED:
        full = full.reshape(r, N_CHIPS, c // N_CHIPS).transpose(1, 0, 2)
    return full.reshape(N_CHIPS, -1, D_MODEL)


_SMALL = (("loss", 128), ("g_final", 1024), ("g_mix", 4096), ("lb_logits", 2048), ("hg_norm_g", 2048),
          ("attn_sinks", 128), ("g_ffn", 4096), ("conv_w", 4 * 3 * D_FF), ("conv_b", 4 * D_FF), ("g_ple", 4096))
SMALL_ROWS = 496


def _pack_small(d):
    parts = []
    for name, n in _SMALL:
        v = d[name].reshape(-1).astype(F32)
        parts.append(jnp.pad(v, (0, n - v.shape[0])))
    flat = jnp.concatenate(parts)
    return jnp.pad(flat, (0, SMALL_ROWS * 128 - flat.shape[0])).reshape(SMALL_ROWS, 128)


def _unpack_small(buf, shapes):
    flat = buf.reshape(-1)
    out, off = {}, 0
    for name, n in _SMALL:
        size = 1
        for s in shapes[name]:
            size *= s
        out[name] = flat[off:off + size].reshape(shapes[name])
        off += n
    return out


WEIGHT_ORDER = ('g_mix', 'w_in', 'lb_logits', 'hg_norm_g', 'attn_sinks', 'w_out', 'g_ffn', 'w_up', 'conv_w',
                'conv_b', 'w_down', 'g_ple', 'w_ple_gate', 'w_ple_up', 'g_final')


def kernel(x, p, g_mix, w_in, lb_logits, hg_norm_g, attn_sinks, w_out, g_ffn, w_up, conv_w, conv_b, w_down, g_ple, w_ple_gate, w_ple_up, g_final, loss_target, m_g_mix, m_w_in, m_lb_logits, m_hg_norm_g, m_attn_sinks, m_w_out, m_g_ffn, m_w_up, m_conv_w, m_conv_b, m_w_down, m_g_ple, m_w_ple_gate, m_w_ple_up, m_g_final, v_g_mix, v_w_in, v_lb_logits, v_hg_norm_g, v_attn_sinks, v_w_out, v_g_ffn, v_w_up, v_conv_w, v_conv_b, v_w_down, v_g_ple, v_w_ple_gate, v_w_ple_up, v_g_final):
    W = dict(g_mix=g_mix, w_in=w_in, lb_logits=lb_logits, hg_norm_g=hg_norm_g, attn_sinks=attn_sinks,
             w_out=w_out, g_ffn=g_ffn, w_up=w_up, conv_w=conv_w, conv_b=conv_b, w_down=w_down, g_ple=g_ple,
             w_ple_gate=w_ple_gate, w_ple_up=w_ple_up, g_final=g_final)
    M = dict(g_mix=m_g_mix, w_in=m_w_in, lb_logits=m_lb_logits, hg_norm_g=m_hg_norm_g, attn_sinks=m_attn_sinks,
             w_out=m_w_out, g_ffn=m_g_ffn, w_up=m_w_up, conv_w=m_conv_w, conv_b=m_conv_b, w_down=m_w_down,
             g_ple=m_g_ple, w_ple_gate=m_w_ple_gate, w_ple_up=m_w_ple_up, g_final=m_g_final)
    V = dict(g_mix=v_g_mix, w_in=v_w_in, lb_logits=v_lb_logits, hg_norm_g=v_hg_norm_g, attn_sinks=v_attn_sinks,
             w_out=v_w_out, g_ffn=v_g_ffn, w_up=v_w_up, conv_w=v_conv_w, conv_b=v_conv_b, w_down=v_w_down,
             g_ple=v_g_ple, w_ple_gate=v_w_ple_gate, w_ple_up=v_w_ple_up, g_final=v_g_final)
    S = x.shape[1]
    hg_rows = min(ROW_TILE, S)
    xi, yi, ci = _place()
    chip = 2 * xi + yi

    slab = _pack_shards({n: W[n] for n, _ in _PACK}).astype(BF16)
    gathered = weights_allgather(slab)
    cw_shard = jnp.pad(conv_w.reshape(-1), (0, 72 * 128 - conv_w.size)).reshape(72, 128)
    cw_all = small_allgather(cw_shard)
    cw_full = jnp.concatenate(
        [cw_all[2 * k].reshape(-1)[:conv_w.size].reshape(conv_w.shape) for k in range(N_CHIPS)], axis=2)
    lbrows = lb_fwd(lb_logits)

    h = x[0]
    saved = []
    for i in range(DEPTH):
        wf = _full_from_chips(gathered, i)
        lbr = lbrows[8 * i:8 * i + 8]
        ng = hg_norm_g[i][None]
        sinks_b = jnp.pad(jnp.repeat(attn_sinks[i].reshape(AT_KV_HEADS, 1, AT_GROUP), WINDOW, axis=2),
                          ((0, 0), (0, 7), (0, 0))).reshape(8 * AT_KV_HEADS, GROUP_LANES)
        cw8 = jnp.pad(cw_full[i], ((0, 5), (0, 0)))
        cb = conv_b[i][None]
        if i == 0:
            u = rmsnorm_fwd(h, g_mix[0][None], name="rmsnorm_fwd")
        proj = mm(u, wf["w_in"], name="mm_in")
        y, o_raw, states = hgrn_fwd(proj, lbr, ng, D_MODEL, rows=hg_rows)
        y = swa_fwd(proj, sinks_b, y)
        h1, u2 = mm(y, wf["w_out"], res=h, norm_g=g_ffn[i][None], name="mm_out")
        hh = mm(u2, wf["w_up"], out_dtype=BF16, name="mm_up")
        act = convffn_fwd(hh, cw8, cb)
        h2, u3 = mm(act, wf["w_down"], res=h1, norm_g=g_ple[i][None], name="mm_down")
        gpre = mm(u3, wf["w_ple_gate"], name="mm_gate")
        next_g = g_mix[i + 1] if i + 1 < DEPTH else g_final
        h3, u_next = ple_fwd(h2, gpre, p[i, 0], wf["w_ple_up"], next_g[None])
        saved.append(dict(wf=wf, lbr=lbr, ng=ng, sinks_b=sinks_b, cw8=cw8, cb=cb, h=h, u=u, proj=proj,
                          o_raw=o_raw, states=states, y=y, h1=h1, u2=u2, hh=hh, act=act, h2=h2, u3=u3,
                          gpre=gpre))
        h, u = h3, u_next

    dh, dhb, loss_acc, dg_final = loss_head(h, g_final[None], loss_target[0])

    gfull = {n: [None] * DEPTH for n, _ in _PACK}
    gsmall = {n: [None] * DEPTH for n in ("g_mix", "hg_norm_g", "attn_sinks", "g_ffn", "conv_w", "conv_b", "g_ple")}
    dlbrows = [None] * DEPTH
    for i in reversed(range(DEPTH)):
        s = saved[i]
        wf = s["wf"]
        dpu, dgp = ple_bwd(dh, s["gpre"], p[i, 0], wf["w_ple_up"])
        gfull["w_ple_up"][i] = mm_tn(p[i, 0], dpu, name="mm_tn_pu")
        gfull["w_ple_gate"][i] = mm_tn(s["u3"], dgp, name="mm_tn_gate")
        dh2, dh2b, dg = mm(dgp, wf["w_ple_gate"], nt=True, rms_bwd=(s["h2"], g_ple[i][None], dh),
                           name="mm_nt_gate")
        gsmall["g_ple"][i] = dg[0]
        gfull["w_down"][i] = mm_tn(s["act"], dh2b, name="mm_tn_down")
        dact = mm(dh2b, wf["w_down"], nt=True, out_dtype=BF16, name="mm_nt_down")
        da, db, dcw = convffn_bwd(s["hh"], dact, s["cw8"], s["cb"])
        gsmall["conv_w"][i] = dcw[0:3]
        gsmall["conv_b"][i] = dcw[3]
        gfull["w_up"][i] = jnp.concatenate([mm_tn(s["u2"], da, name="mm_tn_up"),
                                            mm_tn(s["u2"], db, name="mm_tn_up")], axis=1)
        dh1, dh1b, dg = mm((da, db), wf["w_up"], nt=True, rms_bwd=(s["h1"], g_ffn[i][None], dh2),
                           name="mm_nt_up")
        gsmall["g_ffn"][i] = dg[0]
        gfull["w_out"][i] = mm_tn(s["y"], dh1b, name="mm_tn_out")
        dy = mm(dh1b, wf["w_out"], nt=True, name="mm_nt_out")
        dq_at, dko, dkp, dvo, dvp, dsk = swa_bwd(s["proj"], s["sinks_b"], dy)
        gsmall["attn_sinks"][i] = dsk.reshape(AT_KV_HEADS, 8, GROUP_LANES)[:, 1:1 + AT_GROUP, 0].reshape(-1)
        hq, hz, hv, hgp, dlbr, dng = hgrn_bwd(s["proj"], s["o_raw"], s["states"], dy, s["lbr"], s["ng"],
                                              rows=hg_rows)
        dlbrows[i] = dlbr
        gsmall["hg_norm_g"][i] = dng[0]
        dproj = assemble_dproj((hq, hz, hv, hgp), dq_at, dko, dkp, dvo, dvp, rows=hg_rows)
        gfull["w_in"][i] = mm_tn(s["u"], dproj, name="mm_tn_in")
        dh, dhb, dg = mm(dproj, wf["w_in"], nt=True, rms_bwd=(s["h"], g_mix[i][None], dh1), name="mm_nt_in")
        gsmall["g_mix"][i] = dg[0]
    grad_x = dh[None]
    dlb_logits = lb_bwd(jnp.concatenate(dlbrows, axis=0), lb_logits)[0:DEPTH]

    parts = []
    for i in range(DEPTH):
        for name, _ in _PACK:
            parts.append(_split_to_chips(gfull[name][i], name))
    pk = jnp.concatenate(parts, axis=1).astype(BF16)
    pk = pk.reshape(N_CHIPS, 2, HALF_ROWS, D_MODEL)
    p_mine = lax.dynamic_index_in_dim(pk, ci, axis=1, keepdims=False)
    p_other = lax.dynamic_index_in_dim(pk, 1 - ci, axis=1, keepdims=False)
    from_sib = sibling_swap(p_other, name="sibling_swap_partials")
    pair = sum_slots(jnp.stack([p_mine.reshape(-1, D_MODEL), from_sib.reshape(-1, D_MODEL)]),
                     out_dtype=BF16, name="sum_pair")
    from_chips = chip_exchange(pair.reshape(N_CHIPS, HALF_ROWS, D_MODEL))
    mine_sum = sum_slots(from_chips, out_dtype=F32, name="sum_chips")
    sib_sum = sibling_swap(mine_sum, name="sibling_swap_sums")
    lo = jnp.where(ci == 0, mine_sum, sib_sum)
    hi = jnp.where(ci == 0, sib_sum, mine_sum)
    gshard = _unpack_shards(jnp.concatenate([lo, hi], axis=0))

    small = dict(loss=loss_acc[0, 0:1], g_final=dg_final[0], lb_logits=dlb_logits,
                 **{n: jnp.stack(v) for n, v in gsmall.items()})
    small_sum = sum_slots(small_allgather(_pack_small(small)), out_dtype=F32, name="sum_small")
    shapes = {n: W[n].shape for n in W}
    shapes["loss"] = (1,)
    shapes["conv_w"] = (DEPTH, 3, D_FF)
    gs = _unpack_small(small_sum, shapes)
    loss = gs["loss"][0]
    cshard = conv_w.shape[2]
    grads = dict(gshard)
    for n in ("g_mix", "lb_logits", "hg_norm_g", "attn_sinks", "g_ffn", "conv_b", "g_ple", "g_final"):
        grads[n] = gs[n]
    grads["conv_w"] = lax.dynamic_slice_in_dim(gs["conv_w"], chip * cshard, cshard, axis=2)

    delta, new_m, new_v = {}, {}, {}
    small_names = ("g_final", "g_mix", "lb_logits", "hg_norm_g", "attn_sinks", "g_ffn", "conv_b", "g_ple")
    sshapes = {n: W[n].shape for n in small_names}

    def pack_s(d):
        z = dict(d)
        z["loss"] = jnp.zeros((1,), F32)
        z["conv_w"] = jnp.zeros((1,), F32)
        return _pack_small(z)

    sd, sm, sv = adamw(pack_s(W), pack_s(grads), pack_s(M), pack_s(V), name="adamw_small")
    for out, buf in ((delta, sd), (new_m, sm), (new_v, sv)):
        un = _unpack_small(buf, {**sshapes, "loss": (1,), "conv_w": (1,)})
        for n in small_names:
            out[n] = un[n]
    for n in ("w_in", "w_out", "w_up", "w_down", "w_ple_gate", "w_ple_up", "conv_w"):
        shp = W[n].shape
        two_d = (-1, shp[-1])
        d_, m_, v_ = adamw(W[n].reshape(two_d), grads[n].reshape(two_d), M[n].reshape(two_d),
                           V[n].reshape(two_d), name="adamw_" + n)
        delta[n], new_m[n], new_v[n] = d_.reshape(shp), m_.reshape(shp), v_.reshape(shp)

    return (loss, grad_x, *[grads[n] for n in WEIGHT_ORDER], *[delta[n] for n in WEIGHT_ORDER],
            *[new_m[n] for n in WEIGHT_ORDER], *[new_v[n] for n in WEIGHT_ORDER])
```

```python
import functools

import jax
import jax.numpy as jnp
from jax import lax
from jax.experimental import pallas as pl
from jax.experimental.pallas import tpu as pltpu

F32 = jnp.float32
BF16 = jnp.bfloat16

D_MODEL = 1024
DEPTH = 4
PLE_DIM = 256
HG_WIDTH = 512
HG_HEADS = 4
HG_DK = 128
HG_CHUNK = 64
HG_SUB = 16
AT_WIDTH = 512
AT_HEAD_DIM = 64
AT_Q_HEADS = 8
AT_KV_HEADS = 2
AT_GROUP = 4
WINDOW = 128
D_FF = 2816
IN_WIDTH = 2816
EPS = 1e-6
MASK_VALUE = -1e30
LB_FLOOR = 1e-30

ADAM_LR = 0.001
ADAM_B1 = 0.9
ADAM_B2 = 0.999
ADAM_EPS = 1e-08
ADAM_WD = 0.01
ADAM_STEP = 10

VMEM_LIMIT = 48 * 1024 * 1024


def _params(*sem):
    return pltpu.CompilerParams(dimension_semantics=sem, vmem_limit_bytes=VMEM_LIMIT)


def _dot(a, b, dims=(((1,), (0,)), ((), ()))):
    return lax.dot_general(a.astype(BF16), b.astype(BF16), dims, preferred_element_type=F32)


def _dot_nt(a, b):
    return _dot(a, b, (((1,), (1,)), ((), ())))


def _dot_tn(a, b):
    return _dot(a, b, (((0,), (0,)), ((), ())))


def _dot_f32(a, b, dims=(((1,), (0,)), ((), ()))):
    return lax.dot_general(a, b, dims, preferred_element_type=F32, precision=lax.Precision.HIGHEST)


def _sigmoid(x):
    return 0.5 * jnp.tanh(0.5 * x) + 0.5


def _logsig(x):
    return jnp.minimum(x, 0.0) - jnp.log(1.0 + jnp.exp(-jnp.abs(x)))


def _colsum(x):
    return jnp.sum(x, axis=0, keepdims=True)


def _rowsum(x):
    return jnp.sum(x, axis=1, keepdims=True)


def _tri(n):
    r = lax.broadcasted_iota(jnp.int32, (n, n), 0)
    c = lax.broadcasted_iota(jnp.int32, (n, n), 1)
    return (r >= c).astype(F32)


def _hg_gates(qp, z, a, c, oml):
    sq = _sigmoid(qp)
    q = qp * sq
    t = c + _logsig(z)
    mx = jnp.maximum(a, t)
    logf = mx + jnp.log(1.0 + jnp.exp(-jnp.abs(a - t)))
    snz = _sigmoid(-z)
    k = oml * snz
    return q, sq, t, logf, snz, k


_HEADS = range(HG_HEADS)


def _lanes(h):
    return slice(h * HG_DK, (h + 1) * HG_DK)


def _head(x, h):
    return x[:, _lanes(h)]


def _row_masks():
    row8 = lax.broadcasted_iota(jnp.int32, (8, HG_DK), 0)
    return [None] + [jnp.where(row8 >= j, 0.0, MASK_VALUE) for j in range(1, 8)]


def _hg_chunk_fwd(q, k, v, logf, st, b_s, k_s, v_s):
    C, U = HG_CHUNK, HG_SUB
    tri = _tri(C)
    b = [_dot_f32(tri, logf[h]) for h in _HEADS]
    for h in _HEADS:
        b_s[h] = b[h]
        k_s[h] = k[h]
        v_s[h] = v[h]
    o = [_dot_nt(q[h] * jnp.exp(b[h]), st[h]) for h in _HEADS]
    bl = [b[h][C - 1:C] for h in _HEADS]
    upd = [_dot_tn(v[h], k[h] * jnp.exp(bl[h] - b[h])) for h in _HEADS]
    rows = lax.broadcasted_iota(jnp.int32, (C, HG_DK), 0)
    nmask = _row_masks()
    outs = [[] for _ in _HEADS]
    for i in range(C // U):
        lo = i * U
        b_i = [b[h][lo:lo + U] for h in _HEADS]
        q_i = [q[h][lo:lo + U] for h in _HEADS]
        o_i = [o[h][lo:lo + U] for h in _HEADS]
        if i > 0:
            qe = [q_i[h] * jnp.exp(b_i[h] - b_i[h][0:1]) for h in _HEADS]
            ke = [jnp.where(rows < lo, k[h] * jnp.exp(jnp.minimum(b_i[h][0:1] - b[h], 0.0)), 0.0) for h in _HEADS]
            att = [_dot_nt(qe[h], ke[h]) for h in _HEADS]
            off = [_dot(att[h], v[h]) for h in _HEADS]
            o_i = [o_i[h] + off[h] for h in _HEADS]
        pieces = [[o_i[h][8 * f:8 * f + 8] for f in range(U // 8)] for h in _HEADS]
        for s in range(U):
            for f in range(s // 8, U // 8):
                for h in _HEADS:
                    bs = b_s[h, lo + s:lo + s + 1, :]
                    ks = k_s[h, lo + s:lo + s + 1, :]
                    vs = v_s[h, lo + s:lo + s + 1, :]
                    arg = b_i[h][8 * f:8 * f + 8] - bs
                    if s > 8 * f:
                        arg = arg + nmask[s - 8 * f]
                    w = _rowsum(q_i[h][8 * f:8 * f + 8] * jnp.exp(arg) * ks)
                    pieces[h][f] = pieces[h][f] + w * vs
        for h in _HEADS:
            outs[h] += pieces[h]
    o = [jnp.concatenate(outs[h], axis=0) for h in _HEADS]
    st_new = [st[h] * jnp.exp(bl[h]) + upd[h] for h in _HEADS]
    return o, st_new, b


def _hg_post(o, gp, ng):
    rs = lax.rsqrt(jnp.mean(o * o, axis=1, keepdims=True) + EPS)
    sg = _sigmoid(gp)
    return o * rs * ng * sg, rs, sg


def hgrn_fwd(proj, lbrows, ng, y_width, *, rows, gather=None):
    S = proj.shape[0]
    C = HG_CHUNK
    cpb = rows // C
    nb = S // rows

    def body(qp_ref, z_ref, v_ref, gp_ref, lb_ref, ng_ref, *rest):
        if gather is not None:
            w_ref, y_ref, o_ref, st_ref, g_ref, st, b_s, k_s, v_s, *sems = rest
            comm = _Gather(w_ref, g_ref, *sems)
        else:
            y_ref, o_ref, st_ref, st, b_s, k_s, v_s = rest

        @pl.when(pl.program_id(0) == 0)
        def _():
            st[...] = jnp.zeros_like(st)
            if gather is not None:
                comm.start()

        a, c, oml = lb_ref[0:1, :], lb_ref[1:2, :], lb_ref[2:3, :]
        ngr = ng_ref[...]

        def chunk(ci, carry):
            off = pl.multiple_of(ci * C, C)
            sl = pl.ds(off, C)
            for h in _HEADS:
                st_ref[h, ci] = st[h]
            gates = [_hg_gates(qp_ref[sl, _lanes(h)], z_ref[sl, _lanes(h)],
                               _head(a, h), _head(c, h), _head(oml, h)) for h in _HEADS]
            q = [g[0] for g in gates]
            logf = [g[3] for g in gates]
            k = [g[5] for g in gates]
            v = [v_ref[sl, _lanes(h)] for h in _HEADS]
            o, st_new, _ = _hg_chunk_fwd(q, k, v, logf, [st[h] for h in _HEADS], b_s, k_s, v_s)
            for h in _HEADS:
                y, _, _ = _hg_post(o[h], gp_ref[sl, _lanes(h)], _head(ngr, h))
                y_ref[sl, _lanes(h)] = y.astype(y_ref.dtype)
                o_ref[sl, _lanes(h)] = o[h]
                st[h] = st_new[h]
            return carry

        lax.fori_loop(0, cpb, chunk, 0)

        if gather is not None:
            pl.when(pl.program_id(0) == nb - 1)(comm.finish)

    col = lambda kblk: pl.BlockSpec((rows, HG_WIDTH), lambda r: (r, kblk))
    in_specs = [col(0), col(1), col(2), col(3), _const_spec((8, HG_WIDTH)), _const_spec((1, HG_WIDTH))]
    out_specs = [col(0), col(0), pl.BlockSpec((HG_HEADS, cpb, HG_DK, HG_DK), lambda r: (0, r, 0, 0))]
    out_shape = [jax.ShapeDtypeStruct((S, y_width), BF16),
                 jax.ShapeDtypeStruct((S, HG_WIDTH), F32),
                 jax.ShapeDtypeStruct((HG_HEADS, S // C, HG_DK, HG_DK), F32)]
    scratch = [pltpu.VMEM((HG_HEADS, HG_DK, HG_DK), F32)] + [pltpu.VMEM((HG_HEADS, C, HG_DK), F32)] * 3
    args = [proj, proj, proj, proj, lbrows, ng]
    if gather is not None:
        in_specs.append(ANY)
        out_specs.append(ANY)
        out_shape.append(jax.ShapeDtypeStruct((N_CHIPS,) + gather.shape, gather.dtype))
        scratch += _Gather.SCRATCH
        args.append(gather)
    return pl.pallas_call(
        body,
        name="hgrn_fwd" if gather is None else "hgrn_fwd_gather",
        grid=(nb,),
        in_specs=in_specs,
        out_specs=out_specs,
        out_shape=out_shape,
        scratch_shapes=scratch,
        compiler_params=_params("arbitrary"),
    )(*args)


def hgrn_bwd(proj, o_raw, states, dy, lbrows, ng, *, rows, exchange=None):
    S = proj.shape[0]
    C, U = HG_CHUNK, HG_SUB
    cpb = rows // C
    nb = S // rows

    def body(qp_ref, z_ref, v_ref, gp_ref, o_ref, st_ref, dy_ref, lb_ref, ng_ref, *rest):
        if exchange is not None:
            (q_ref, dqp_ref, dz_ref, dv_ref, dgp_ref, dlb_ref, dng_ref, r_ref,
             dst, b_s, k_s, v_s, dbs, dks, dvs, *sems) = rest
            comm = _Exchange(q_ref, r_ref, *sems)
        else:
            dqp_ref, dz_ref, dv_ref, dgp_ref, dlb_ref, dng_ref, dst, b_s, k_s, v_s, dbs, dks, dvs = rest

        @pl.when(pl.program_id(0) == 0)
        def _():
            dst[...] = jnp.zeros_like(dst)
            dlb_ref[...] = jnp.zeros_like(dlb_ref)
            dng_ref[...] = jnp.zeros_like(dng_ref)
            if exchange is not None:
                comm.start()

        a, c, oml = lb_ref[0:1, :], lb_ref[1:2, :], lb_ref[2:3, :]
        ngr = ng_ref[...]
        rows_i = lax.broadcasted_iota(jnp.int32, (C, HG_DK), 0)
        nmask = _row_masks()
        tri = _tri(C)
        H = _HEADS

        def chunk(cj, carry):
            ci = cpb - 1 - cj
            off = pl.multiple_of(ci * C, C)
            sl = pl.ds(off, C)
            qp = [qp_ref[sl, _lanes(h)] for h in H]
            v = [v_ref[sl, _lanes(h)] for h in H]
            st = [st_ref[h, ci] for h in H]
            gates = [_hg_gates(qp[h], z_ref[sl, _lanes(h)], _head(a, h), _head(c, h), _head(oml, h)) for h in H]
            q, sq, t, logf, snz, k = ([g[j] for g in gates] for j in range(6))
            b = [_dot_f32(tri, logf[h]) for h in H]
            for h in H:
                b_s[h] = b[h]
                k_s[h] = k[h]
                v_s[h] = v[h]
            do = []
            for h in H:
                o = o_ref[sl, _lanes(h)]
                dyv = dy_ref[sl, _lanes(h)]
                ngh = _head(ngr, h)
                rs = lax.rsqrt(jnp.mean(o * o, axis=1, keepdims=True) + EPS)
                sg = _sigmoid(gp_ref[sl, _lanes(h)])
                xh = o * rs
                dgp_ref[sl, _lanes(h)] = (dyv * xh * ngh * sg * (1.0 - sg)).astype(dgp_ref.dtype)
                don = dyv * sg
                dng_ref[0:1, _lanes(h)] += _colsum(don * xh)
                dxh = don * ngh
                do.append(rs * (dxh - xh * jnp.mean(dxh * xh, axis=1, keepdims=True)))
            eb = [jnp.exp(b[h]) for h in H]
            qb = [q[h] * eb[h] for h in H]
            dstv = [dst[h] for h in H]
            bl = [b[h][C - 1:C] for h in H]
            el = [jnp.exp(bl[h]) for h in H]
            ex = [jnp.exp(bl[h] - b[h]) for h in H]
            kd = [k[h] * ex[h] for h in H]
            dqb = [_dot(do[h], st[h]) for h in H]
            dst_acc = [_dot_tn(do[h], qb[h]) for h in H]
            dv0 = [_dot_nt(kd[h], dstv[h]) for h in H]
            dkd = [_dot(v[h], dstv[h]) for h in H]
            dq = [dqb[h] * eb[h] for h in H]
            for h in H:
                g2 = dkd[h] * kd[h]
                dbl = _colsum(dstv[h] * st[h]) * el[h] + _colsum(g2)
                dst[h] = dstv[h] * el[h] + dst_acc[h]
                dbs[h] = dqb[h] * qb[h] - g2
                dks[h] = dkd[h] * ex[h]
                dvs[h] = dv0[h]
                dbs[h, C - 1:C, :] += dbl
            dq_parts = [[] for _ in H]
            for i in range(C // U):
                lo = i * U
                b_i = [b[h][lo:lo + U] for h in H]
                q_i = [q[h][lo:lo + U] for h in H]
                do_i = [do[h][lo:lo + U] for h in H]
                dq_i = [dq[h][lo:lo + U] for h in H]
                db_i = [jnp.zeros((U, HG_DK), F32) for _ in H]
                if i > 0:
                    e1 = [jnp.exp(b_i[h] - b_i[h][0:1]) for h in H]
                    qe = [q_i[h] * e1[h] for h in H]
                    e2 = [jnp.where(rows_i < lo, jnp.exp(jnp.minimum(b_i[h][0:1] - b[h], 0.0)), 0.0) for h in H]
                    ke = [k[h] * e2[h] for h in H]
                    att = [_dot_nt(qe[h], ke[h]) for h in H]
                    datt = [_dot_nt(do_i[h], v[h]) for h in H]
                    dv_add = [_dot_tn(att[h], do_i[h]) for h in H]
                    dqe = [_dot(datt[h], ke[h]) for h in H]
                    dke = [_dot_tn(datt[h], qe[h]) for h in H]
                    for h in H:
                        dvs[h] += dv_add[h]
                        dq_i[h] = dq_i[h] + dqe[h] * e1[h]
                        g = dqe[h] * qe[h]
                        db_i[h] = db_i[h] + g
                        gk = dke[h] * ke[h]
                        dks[h] += dke[h] * e2[h]
                        dbs[h] -= gk
                        dbs[h, lo:lo + 1, :] += _colsum(gk) - _colsum(g)
                nf = U // 8
                dq8 = [[dq_i[h][8 * f:8 * f + 8] for f in range(nf)] for h in H]
                db8 = [[db_i[h][8 * f:8 * f + 8] for f in range(nf)] for h in H]
                for s in range(U):
                    row = slice(lo + s, lo + s + 1)
                    for h in H:
                        bs = b_s[h, row, :]
                        ks = k_s[h, row, :]
                        vs = v_s[h, row, :]
                        tv = tk = tb = None
                        for f in range(s // 8, nf):
                            p8 = slice(8 * f, 8 * f + 8)
                            arg = b_i[h][p8] - bs
                            if s > 8 * f:
                                arg = arg + nmask[s - 8 * f]
                            dec = jnp.exp(arg)
                            qd = q_i[h][p8] * dec
                            y_ = qd * ks
                            w = _rowsum(y_)
                            dw = _rowsum(do_i[h][p8] * vs)
                            g = dw * y_
                            dq8[h][f] = dq8[h][f] + dw * dec * ks
                            db8[h][f] = db8[h][f] + g
                            cv, ck = w * do_i[h][p8], dw * qd
                            tv, tk, tb = (cv, ck, g) if tv is None else (tv + cv, tk + ck, tb + g)
                        dvs[h, row, :] += _colsum(tv)
                        dks[h, row, :] += _colsum(tk)
                        dbs[h, row, :] -= _colsum(tb)
                for h in H:
                    for f in range(nf):
                        dbs[h, lo + 8 * f:lo + 8 * f + 8, :] += db8[h][f]
                    dq_parts[h] += dq8[h]
            dlogf = [_dot_f32(tri, dbs[h], (((0,), (0,)), ((), ()))) for h in H]
            for h in H:
                dqh = jnp.concatenate(dq_parts[h], axis=0)
                dk = dks[h]
                ah, omlh = _head(a, h), _head(oml, h)
                pa = jnp.exp(ah - logf[h])
                pt = jnp.exp(t[h] - logf[h])
                dt = dlogf[h] * pt
                dlb_ref[0:1, _lanes(h)] += _colsum(dlogf[h] * pa)
                dlb_ref[1:2, _lanes(h)] += _colsum(dt)
                dlb_ref[2:3, _lanes(h)] += _colsum(dk * snz[h])
                dz = dt * snz[h] - dk * omlh * snz[h] * (1.0 - snz[h])
                dqp = dqh * (sq[h] + qp[h] * sq[h] * (1.0 - sq[h]))
                dqp_ref[sl, _lanes(h)] = dqp.astype(dqp_ref.dtype)
                dz_ref[sl, _lanes(h)] = dz.astype(dz_ref.dtype)
                dv_ref[sl, _lanes(h)] = dvs[h].astype(dv_ref.dtype)
            return carry

        lax.fori_loop(0, cpb, chunk, 0)

        if exchange is not None:
            pl.when(pl.program_id(0) == nb - 1)(comm.finish)

    rev = lambda r: nb - 1 - r
    col = lambda kblk: pl.BlockSpec((rows, HG_WIDTH), lambda r: (rev(r), kblk))
    acc = _const_spec((8, HG_WIDTH))
    in_specs = [col(0), col(1), col(2), col(3), col(0),
                pl.BlockSpec((HG_HEADS, cpb, HG_DK, HG_DK), lambda r: (0, rev(r), 0, 0)),
                col(0), acc, _const_spec((1, HG_WIDTH))]
    out_specs = [col(0)] * 4 + [acc, acc]
    out_shape = [jax.ShapeDtypeStruct((S, HG_WIDTH), BF16)] * 4 + [jax.ShapeDtypeStruct((8, HG_WIDTH), F32)] * 2
    scratch = [pltpu.VMEM((HG_HEADS, HG_DK, HG_DK), F32)] + [pltpu.VMEM((HG_HEADS, C, HG_DK), F32)] * 6
    args = [proj, proj, proj, proj, o_raw, states, dy, lbrows, ng]
    if exchange is not None:
        in_specs.append(ANY)
        out_specs.append(ANY)
        out_shape.append(jax.ShapeDtypeStruct(exchange.shape, exchange.dtype))
        scratch += _Exchange.SCRATCH
        args.append(exchange)
    return pl.pallas_call(
        body,
        name="hgrn_bwd" if exchange is None else "hgrn_bwd_exchange",
        grid=(nb,),
        in_specs=in_specs,
        out_specs=out_specs,
        out_shape=out_shape,
        scratch_shapes=scratch,
        compiler_params=_params("arbitrary"),
    )(*args)


GROUP_LANES = AT_GROUP * WINDOW


def _swa_valid_t(n):
    W = WINDOW
    kpos = lax.broadcasted_iota(jnp.int32, (2 * W, GROUP_LANES), 0)
    qpos = (lax.broadcasted_iota(jnp.int32, (2 * W, GROUP_LANES), 1) & (W - 1)) + W
    rel = qpos - kpos
    return (rel >= 0) & (rel < W) & jnp.logical_not((n == 0) & (kpos < W))


def _group_lanes(xt, g):
    Dh = AT_HEAD_DIM
    return jnp.concatenate([xt[(g * AT_GROUP + j) * Dh:(g * AT_GROUP + j + 1) * Dh] for j in range(AT_GROUP)],
                           axis=1)


def _swa_probs_t(kg, qg, sink_row, valid):
    s = _dot(kg, qg) * (AT_HEAD_DIM ** -0.5)
    s = jnp.where(valid, s, MASK_VALUE)
    m = jnp.maximum(jnp.max(s, axis=0, keepdims=True), sink_row)
    e = jnp.exp(s - m)
    es = jnp.exp(sink_row - m)
    inv = 1.0 / (_colsum(e) + es)
    return e * inv, es * inv


def swa_fwd(proj, sink_rows, y):
    S = proj.shape[0]
    W, Dh = WINDOW, AT_HEAD_DIM
    nb = S // W

    def body(q_ref, kp_ref, k_ref, vp_ref, v_ref, sk_ref, y_in, y_ref):
        del y_in
        valid = _swa_valid_t(pl.program_id(0))
        kk = jnp.concatenate([kp_ref[...], k_ref[...]], axis=0)
        vt = jnp.concatenate([vp_ref[...], v_ref[...]], axis=0).T
        qt = q_ref[...].T
        outs = []
        for g in range(AT_KV_HEADS):
            p, _ = _swa_probs_t(kk[:, g * Dh:(g + 1) * Dh], _group_lanes(qt, g),
                                sk_ref[8 * g:8 * g + 1, :], valid)
            ot = _dot(vt[g * Dh:(g + 1) * Dh], p)
            outs += [ot[:, j * W:(j + 1) * W] for j in range(AT_GROUP)]
        y_ref[...] = jnp.concatenate(outs, axis=0).T.astype(y_ref.dtype)

    prev = lambda n: jnp.maximum(n - 1, 0)
    return pl.pallas_call(
        body,
        name="swa_fwd",
        grid=(nb,),
        in_specs=[pl.BlockSpec((W, AT_WIDTH), lambda n: (n, 4)),
                  pl.BlockSpec((W, 128), lambda n: (prev(n), 20)),
                  pl.BlockSpec((W, 128), lambda n: (n, 20)),
                  pl.BlockSpec((W, 128), lambda n: (prev(n), 21)),
                  pl.BlockSpec((W, 128), lambda n: (n, 21)),
                  pl.BlockSpec((8 * AT_KV_HEADS, GROUP_LANES), lambda n: (0, 0)),
                  pl.BlockSpec(memory_space=pl.ANY)],
        out_specs=pl.BlockSpec((W, AT_WIDTH), lambda n: (n, 1)),
        out_shape=jax.ShapeDtypeStruct(y.shape, y.dtype),
        input_output_aliases={6: 0},
        compiler_params=_params("parallel"),
    )(proj, proj, proj, proj, proj, sink_rows, y)


def swa_bwd(proj, sink_rows, dy):
    S = proj.shape[0]
    W, Dh = WINDOW, AT_HEAD_DIM
    nb = S // W
    scale = Dh ** -0.5

    def body(q_ref, kp_ref, k_ref, vp_ref, v_ref, sk_ref, dy_ref,
             dq_ref, dko_ref, dkp_ref, dvo_ref, dvp_ref, dsk_ref):
        n = pl.program_id(0)

        @pl.when(n == 0)
        def _():
            dsk_ref[...] = jnp.zeros_like(dsk_ref)

        valid = _swa_valid_t(n)
        kk = jnp.concatenate([kp_ref[...], k_ref[...]], axis=0)
        vv = jnp.concatenate([vp_ref[...], v_ref[...]], axis=0)
        kt = kk.T
        qt = q_ref[...].T
        dot_ = dy_ref[...].T
        dqs, dks, dvs = [], [], []
        for g in range(AT_KV_HEADS):
            qg = _group_lanes(qt, g)
            dog = _group_lanes(dot_, g)
            vg = vv[:, g * Dh:(g + 1) * Dh]
            p, ps = _swa_probs_t(kk[:, g * Dh:(g + 1) * Dh], qg, sk_ref[8 * g:8 * g + 1, :], valid)
            dp = _dot(vg, dog)
            delta = _colsum(dp * p)
            ds = p * (dp - delta) * scale
            dqt = _dot(kt[g * Dh:(g + 1) * Dh], ds)
            dqs += [dqt[:, j * W:(j + 1) * W] for j in range(AT_GROUP)]
            dks.append(_dot_nt(ds, qg))
            dvs.append(_dot_nt(p, dog))
            dsk_ref[8 * g:8 * g + 1, :] += -(ps * delta)
        dq_ref[...] = jnp.concatenate(dqs, axis=0).T.astype(dq_ref.dtype)
        dk = jnp.concatenate(dks, axis=1)
        dv = jnp.concatenate(dvs, axis=1)
        dkp_ref[...] = dk[:W]
        dko_ref[...] = dk[W:]
        dvp_ref[...] = dv[:W]
        dvo_ref[...] = dv[W:]

        @pl.when(n == nb - 1)
        def _():
            for g in range(AT_KV_HEADS):
                for j in range(AT_GROUP):
                    tot = _rowsum(dsk_ref[8 * g:8 * g + 1, j * W:(j + 1) * W])
                    dsk_ref[8 * g + 1 + j:8 * g + 2 + j, :] = jnp.broadcast_to(tot, (1, GROUP_LANES))

    prev = lambda n: jnp.maximum(n - 1, 0)
    kv = pl.BlockSpec((W, 128), lambda n: (n, 0))
    sk = pl.BlockSpec((8 * AT_KV_HEADS, GROUP_LANES), lambda n: (0, 0))
    return pl.pallas_call(
        body,
        name="swa_bwd",
        grid=(nb,),
        in_specs=[pl.BlockSpec((W, AT_WIDTH), lambda n: (n, 4)),
                  pl.BlockSpec((W, 128), lambda n: (prev(n), 20)),
                  pl.BlockSpec((W, 128), lambda n: (n, 20)),
                  pl.BlockSpec((W, 128), lambda n: (prev(n), 21)),
                  pl.BlockSpec((W, 128), lambda n: (n, 21)),
                  sk,
                  pl.BlockSpec((W, AT_WIDTH), lambda n: (n, 1))],
        out_specs=[pl.BlockSpec((W, AT_WIDTH), lambda n: (n, 0)), kv, kv, kv, kv, sk],
        out_shape=[jax.ShapeDtypeStruct((S, AT_WIDTH), BF16)]
                  + [jax.ShapeDtypeStruct((S, 128), F32)] * 4
                  + [jax.ShapeDtypeStruct((8 * AT_KV_HEADS, GROUP_LANES), F32)],
        compiler_params=_params("arbitrary"),
    )(proj, proj, proj, proj, proj, sink_rows, dy)


def assemble_dproj(hg_grads, dq_at, dko, dkp, dvo, dvp, *, rows):
    S = dq_at.shape[0]
    W = WINDOW
    nb = S // W
    bpr = rows // W

    def body(a0, a1, a2, a3, dq, ko, kp, kpn, vo, vp, vpn, out):
        r = pl.program_id(0)
        for i, a in enumerate((a0, a1, a2, a3)):
            out[:, i * HG_WIDTH:(i + 1) * HG_WIDTH] = a[...]
        base = 4 * HG_WIDTH
        out[:, base:base + AT_WIDTH] = dq[...]
        last = (r == pl.num_programs(0) - 1)
        for off, own, pv, pvn in ((base + AT_WIDTH, ko, kp, kpn), (base + AT_WIDTH + 128, vo, vp, vpn)):
            if bpr > 1:
                out[0:rows - W, off:off + 128] = (own[0:rows - W, :] + pv[W:rows, :]).astype(out.dtype)
            nxt = jnp.where(last, 0.0, pvn[...])
            out[rows - W:rows, off:off + 128] = (own[rows - W:rows, :] + nxt).astype(out.dtype)

    hg = pl.BlockSpec((rows, HG_WIDTH), lambda r: (r, 0))
    blk = pl.BlockSpec((rows, 128), lambda r: (r, 0))
    nxt = pl.BlockSpec((W, 128), lambda r: (jnp.minimum((r + 1) * bpr, nb - 1), 0))
    return pl.pallas_call(
        body,
        name="assemble_dproj",
        grid=(S // rows,),
        in_specs=[hg, hg, hg, hg, pl.BlockSpec((rows, AT_WIDTH), lambda r: (r, 0)),
                  blk, blk, nxt, blk, blk, nxt],
        out_specs=pl.BlockSpec((rows, IN_WIDTH), lambda r: (r, 0)),
        out_shape=jax.ShapeDtypeStruct((S, IN_WIDTH), BF16),
        compiler_params=_params("parallel"),
    )(*hg_grads, dq_at, dko, dkp, dkp, dvo, dvp, dvp)


ROW_TILE = 512
COL_TILE = 1408


def _col_tile(n):
    return n if n <= COL_TILE else COL_TILE


def _rms_scale(x):
    return lax.rsqrt(jnp.mean(x * x, axis=1, keepdims=True) + EPS)


def _rms_bwd(d, x, g):
    rs = _rms_scale(x)
    xh = x * rs
    dxh = d * g
    return rs * (dxh - xh * jnp.mean(dxh * xh, axis=1, keepdims=True)), _colsum(d * xh)


def mm(a, b, *, nt=False, out_dtype=F32, res=None, norm_g=None, rms_bwd=None, name):
    parts = a if isinstance(a, tuple) else (a,)
    M, K = parts[0].shape
    N = b.shape[0] if nt else b.shape[1]
    tall = K <= D_MODEL and rms_bwd is None and len(parts) == 1 and M % (2 * ROW_TILE) == 0
    tm = 2 * ROW_TILE if tall else min(ROW_TILE, M)
    tn = _col_tile(N)
    whole_rows = norm_g is not None or rms_bwd is not None
    assert M % tm == 0 and N % tn == 0 and (tn == N or not whole_rows)
    np_ = len(parts)

    def body(*refs):
        a_refs, b_refs, rest = refs[:np_], refs[np_:2 * np_], refs[2 * np_:]
        dot = _dot_nt if nt else _dot
        acc = dot(a_refs[0][...], b_refs[0][...])
        for ar, br in zip(a_refs[1:], b_refs[1:]):
            acc = acc + dot(ar[...], br[...])
        if rms_bwd is not None:
            h_ref, g_ref, dr_ref, dh_ref, dhb_ref, dg_ref = rest

            @pl.when(pl.program_id(1) == 0)
            def _():
                dg_ref[...] = jnp.zeros_like(dg_ref)

            dx, dgp = _rms_bwd(acc, h_ref[...], g_ref[...])
            dg_ref[0:1, :] += dgp
            dh = dr_ref[...] + dx
            dh_ref[...] = dh
            dhb_ref[...] = dh.astype(BF16)
            return
        rest = list(rest)
        if res is not None:
            acc = acc + rest.pop(0)[...]
        if norm_g is not None:
            g_ref = rest.pop(0)
            rest[1][...] = (acc * _rms_scale(acc) * g_ref[...]).astype(BF16)
        rest[0][...] = acc.astype(rest[0].dtype)

    row = pl.BlockSpec((tm, tn), lambda j, i: (i, j))
    in_specs = [pl.BlockSpec((tm, K), lambda j, i: (i, 0)) for _ in parts]
    for kb in range(np_):
        in_specs.append(pl.BlockSpec((tn, K), lambda j, i, kb=kb: (j, kb)) if nt
                        else pl.BlockSpec((K, tn), lambda j, i, kb=kb: (kb, j)))
    args = list(parts) + [b] * np_
    if rms_bwd is not None:
        h, g, dres = rms_bwd
        in_specs += [row, _const_spec((1, N)), row]
        args += [h, g, dres]
        out_specs = [row, row, _const_spec((8, N))]
        out_shape = [jax.ShapeDtypeStruct((M, N), F32), jax.ShapeDtypeStruct((M, N), BF16),
                     jax.ShapeDtypeStruct((8, N), F32)]
        sem = ("arbitrary", "arbitrary")
    else:
        if res is not None:
            in_specs.append(row)
            args.append(res)
        out_specs, out_shape = [row], [jax.ShapeDtypeStruct((M, N), out_dtype)]
        if norm_g is not None:
            in_specs.append(_const_spec((1, N)))
            args.append(norm_g)
            out_specs.append(row)
            out_shape.append(jax.ShapeDtypeStruct((M, N), BF16))
        sem = ("parallel", "parallel")
    out = pl.pallas_call(
        body,
        name=name,
        grid=(N // tn, M // tm),
        in_specs=in_specs,
        out_specs=out_specs,
        out_shape=out_shape,
        compiler_params=_params(*sem),
    )(*args)
    return out[0] if len(out) == 1 else out


def mm_tn(a, b, *, name):
    M, K = a.shape
    N = b.shape[1]
    tm = next((t for t in (4 * ROW_TILE, 2 * ROW_TILE) if M % t == 0), min(ROW_TILE, M))
    tk = _col_tile(K)
    tn = _col_tile(N)
    assert M % tm == 0 and K % tk == 0 and N % tn == 0

    def body(a_ref, b_ref, o_ref):
        @pl.when(pl.program_id(2) == 0)
        def _():
            o_ref[...] = jnp.zeros_like(o_ref)

        o_ref[...] += _dot_tn(a_ref[...], b_ref[...])

    return pl.pallas_call(
        body,
        name=name,
        grid=(K // tk, N // tn, M // tm),
        in_specs=[pl.BlockSpec((tm, tk), lambda k, j, i: (i, k)),
                  pl.BlockSpec((tm, tn), lambda k, j, i: (i, j))],
        out_specs=pl.BlockSpec((tk, tn), lambda k, j, i: (k, j)),
        out_shape=jax.ShapeDtypeStruct((K, N), F32),
        compiler_params=_params("parallel", "parallel", "arbitrary"),
    )(a, b)


def _row_spec(tm, width):
    return pl.BlockSpec((tm, width), lambda i: (i, 0))


def _const_spec(shape):
    return pl.BlockSpec(shape, lambda *_: (0,) * len(shape))


def rmsnorm_fwd(h, g, *, name):
    S, D = h.shape
    tm = min(ROW_TILE, S)

    def body(h_ref, g_ref, u_ref):
        x = h_ref[...]
        rs = lax.rsqrt(jnp.mean(x * x, axis=1, keepdims=True) + EPS)
        u_ref[...] = (x * rs * g_ref[...]).astype(u_ref.dtype)

    return pl.pallas_call(
        body, name=name, grid=(S // tm,),
        in_specs=[_row_spec(tm, D), _const_spec((1, D))],
        out_specs=_row_spec(tm, D),
        out_shape=jax.ShapeDtypeStruct((S, D), BF16),
        compiler_params=_params("parallel"),
    )(h, g)


HALO = 16


def _shift_down(x, edge8, s):
    sh = pltpu.roll(x, s, 0)
    er = pltpu.roll(edge8, s, 0)
    row8 = lax.broadcasted_iota(jnp.int32, er.shape, 0)
    top = jnp.where(row8 < s, er, sh[0:8])
    return jnp.concatenate([top, sh[8:]], axis=0)


def _shift_up(x, s):
    return pltpu.roll(x, x.shape[0] - s, 0)


def _conv_pre(a, prev8, w_ref, cb_ref):
    a1 = _shift_down(a, prev8, 1)
    a2 = _shift_down(a, prev8, 2)
    ac = w_ref[2:3, :] * a + w_ref[1:2, :] * a1 + w_ref[0:1, :] * a2 + cb_ref[...]
    return ac, a1, a2


def convffn_fwd(hh, cw8, cb):
    S = hh.shape[0]
    tm = min(ROW_TILE, S)
    tn = _col_tile(D_FF)
    nj = D_FF // tn

    def body(a_ref, ap_ref, b_ref, w_ref, cb_ref, o_ref):
        prev8 = jnp.where(pl.program_id(1) == 0, 0.0, ap_ref[...].astype(F32)[HALO - 8:HALO])
        ac, _, _ = _conv_pre(a_ref[...].astype(F32), prev8, w_ref, cb_ref)
        o_ref[...] = (ac * _sigmoid(ac) * b_ref[...].astype(F32)).astype(o_ref.dtype)

    rh = tm // HALO
    return pl.pallas_call(
        body, name="convffn_fwd", grid=(nj, S // tm),
        in_specs=[pl.BlockSpec((tm, tn), lambda j, i: (i, j)),
                  pl.BlockSpec((HALO, tn), lambda j, i: (jnp.maximum(i * rh - 1, 0), j)),
                  pl.BlockSpec((tm, tn), lambda j, i: (i, j + nj)),
                  pl.BlockSpec((8, tn), lambda j, i: (0, j)),
                  pl.BlockSpec((1, tn), lambda j, i: (0, j))],
        out_specs=pl.BlockSpec((tm, tn), lambda j, i: (i, j)),
        out_shape=jax.ShapeDtypeStruct((S, D_FF), BF16),
        compiler_params=_params("parallel", "parallel"),
    )(hh, hh, hh, cw8, cb)


def convffn_bwd(hh, dact, cw8, cb):
    S = hh.shape[0]
    tm = min(ROW_TILE, S)
    tn = _col_tile(D_FF)
    nj = D_FF // tn
    ni = S // tm

    def body(a_ref, ap_ref, an_ref, b_ref, bn_ref, d_ref, dn_ref, w_ref, cb_ref, o_a, o_b, dw_ref):
        i = pl.program_id(1)

        @pl.when(i == 0)
        def _():
            dw_ref[...] = jnp.zeros_like(dw_ref)

        up = lambda r: r[...].astype(F32)
        prev8 = jnp.where(i == 0, 0.0, up(ap_ref)[HALO - 8:HALO])
        a = jnp.concatenate([up(a_ref), up(an_ref)[0:8]], axis=0)
        b = jnp.concatenate([up(b_ref), up(bn_ref)[0:8]], axis=0)
        d = jnp.concatenate([up(d_ref), jnp.where(i == ni - 1, 0.0, up(dn_ref)[0:8])], axis=0)
        ac, a1, a2 = _conv_pre(a, prev8, w_ref, cb_ref)
        sa = _sigmoid(ac)
        o_b[...] = (d[0:tm] * ac[0:tm] * sa[0:tm]).astype(o_b.dtype)
        dac = d * b * (sa + ac * sa * (1.0 - sa))
        da = w_ref[2:3, :] * dac + w_ref[1:2, :] * _shift_up(dac, 1) + w_ref[0:1, :] * _shift_up(dac, 2)
        o_a[...] = da[0:tm].astype(o_a.dtype)
        dc = dac[0:tm]
        dw_ref[0:1, :] += _colsum(dc * a2[0:tm])
        dw_ref[1:2, :] += _colsum(dc * a1[0:tm])
        dw_ref[2:3, :] += _colsum(dc * a[0:tm])
        dw_ref[3:4, :] += _colsum(dc)

    rh = tm // HALO
    last = S // HALO - 1
    cur = lambda off: pl.BlockSpec((tm, tn), lambda j, i: (i, j + off))
    nxt = lambda off: pl.BlockSpec((HALO, tn), lambda j, i: (jnp.minimum((i + 1) * rh, last), j + off))
    return pl.pallas_call(
        body, name="convffn_bwd", grid=(nj, ni),
        in_specs=[cur(0),
                  pl.BlockSpec((HALO, tn), lambda j, i: (jnp.maximum(i * rh - 1, 0), j)),
                  nxt(0), cur(nj), nxt(nj), cur(0), nxt(0),
                  pl.BlockSpec((8, tn), lambda j, i: (0, j)),
                  pl.BlockSpec((1, tn), lambda j, i: (0, j))],
        out_specs=[cur(0), cur(0), pl.BlockSpec((8, tn), lambda j, i: (0, j))],
        out_shape=[jax.ShapeDtypeStruct((S, D_FF), BF16), jax.ShapeDtypeStruct((S, D_FF), BF16),
                   jax.ShapeDtypeStruct((8, D_FF), F32)],
        compiler_params=_params("parallel", "arbitrary"),
    )(hh, hh, hh, hh, hh, dact, dact, cw8, cb)


def ple_fwd(h, gpre, p, wpu, norm_g):
    S, D = h.shape
    tm = min(ROW_TILE, S)

    def body(h_ref, g_ref, p_ref, w_ref, ng_ref, o_ref, u_ref):
        out = h_ref[...] + _sigmoid(g_ref[...]) * _dot(p_ref[...], w_ref[...])
        o_ref[...] = out
        u_ref[...] = (out * _rms_scale(out) * ng_ref[...]).astype(BF16)

    return pl.pallas_call(
        body, name="ple_fwd", grid=(S // tm,),
        in_specs=[_row_spec(tm, D), _row_spec(tm, D), _row_spec(tm, PLE_DIM), _const_spec((PLE_DIM, D)),
                  _const_spec((1, D))],
        out_specs=[_row_spec(tm, D), _row_spec(tm, D)],
        out_shape=[jax.ShapeDtypeStruct((S, D), F32), jax.ShapeDtypeStruct((S, D), BF16)],
        compiler_params=_params("parallel"),
    )(h, gpre, p, wpu, norm_g)


def ple_bwd(dh, gpre, p, wpu):
    S, D = dh.shape
    tm = min(ROW_TILE, S)

    def body(d_ref, g_ref, p_ref, w_ref, dpu_ref, dg_ref):
        d = d_ref[...]
        gate = _sigmoid(g_ref[...])
        pu = _dot(p_ref[...], w_ref[...])
        dpu_ref[...] = (d * gate).astype(dpu_ref.dtype)
        dg_ref[...] = (d * pu * gate * (1.0 - gate)).astype(dg_ref.dtype)

    return pl.pallas_call(
        body, name="ple_bwd", grid=(S // tm,),
        in_specs=[_row_spec(tm, D), _row_spec(tm, D), _row_spec(tm, PLE_DIM), _const_spec((PLE_DIM, D))],
        out_specs=[_row_spec(tm, D), _row_spec(tm, D)],
        out_shape=[jax.ShapeDtypeStruct((S, D), BF16)] * 2,
        compiler_params=_params("parallel"),
    )(dh, gpre, p, wpu)


def loss_head(h, g, tgt):
    S, D = h.shape
    tm = min(ROW_TILE, S)

    def body(h_ref, g_ref, t_ref, dh_ref, dhb_ref, l_ref, dg_ref):
        @pl.when(pl.program_id(0) == 0)
        def _():
            l_ref[...] = jnp.zeros_like(l_ref)
            dg_ref[...] = jnp.zeros_like(dg_ref)

        x = h_ref[...]
        gr = g_ref[...]
        rs = lax.rsqrt(jnp.mean(x * x, axis=1, keepdims=True) + EPS)
        xh = x * rs
        err = xh * gr - t_ref[...]
        l_ref[0:1, 0:1] += 0.5 * _colsum(jnp.mean(err * err, axis=1, keepdims=True))
        dy = err * (1.0 / D)
        dg_ref[0:1, :] += _colsum(dy * xh)
        dxh = dy * gr
        dh = rs * (dxh - xh * jnp.mean(dxh * xh, axis=1, keepdims=True))
        dh_ref[...] = dh
        dhb_ref[...] = dh.astype(BF16)

    return pl.pallas_call(
        body, name="loss_head", grid=(S // tm,),
        in_specs=[_row_spec(tm, D), _const_spec((1, D)), _row_spec(tm, D)],
        out_specs=[_row_spec(tm, D), _row_spec(tm, D), _const_spec((8, 128)), _const_spec((8, D))],
        out_shape=[jax.ShapeDtypeStruct((S, D), F32), jax.ShapeDtypeStruct((S, D), BF16),
                   jax.ShapeDtypeStruct((8, 128), F32), jax.ShapeDtypeStruct((8, D), F32)],
        compiler_params=_params("arbitrary"),
    )(h, g, tgt)


def _lb_rows(l_ref):
    l = l_ref[...]
    e = jnp.exp(l - jnp.max(l, axis=0, keepdims=True))
    p = e / _colsum(e)
    lbs, run = [], None
    for i in range(DEPTH):
        run = p[i:i + 1] if i == 0 else run + p[i:i + 1]
        lbs.append(run - p[0:1])
    return p, lbs


def lb_fwd(lb_logits):
    def body(l_ref, o_ref):
        _, lbs = _lb_rows(l_ref)
        o_ref[...] = jnp.zeros_like(o_ref)
        for i, lb in enumerate(lbs):
            o_ref[8 * i:8 * i + 1, :] = jnp.log(jnp.maximum(lb, LB_FLOOR))
            o_ref[8 * i + 1:8 * i + 2, :] = jnp.log1p(-lb)
            o_ref[8 * i + 2:8 * i + 3, :] = 1.0 - lb
            o_ref[8 * i + 3:8 * i + 4, :] = lb

    return pl.pallas_call(
        body, name="lb_fwd",
        out_shape=jax.ShapeDtypeStruct((DEPTH * 8, HG_WIDTH), F32),
    )(lb_logits)


def lb_bwd(dlbrows, lb_logits):
    def body(d_ref, l_ref, o_ref):
        p, lbs = _lb_rows(l_ref)
        dlb = []
        for i, lb in enumerate(lbs):
            da = d_ref[8 * i:8 * i + 1, :]
            dc = d_ref[8 * i + 1:8 * i + 2, :]
            do = d_ref[8 * i + 2:8 * i + 3, :]
            dlb.append(jnp.where(lb > LB_FLOOR, da / jnp.maximum(lb, LB_FLOOR), 0.0) - dc / (1.0 - lb) - do)
        dp = [jnp.zeros_like(dlb[0])]
        for j in range(1, DEPTH):
            acc = dlb[j]
            for i in range(j + 1, DEPTH):
                acc = acc + dlb[i]
            dp.append(acc)
        dot_ = p[0:1] * dp[0]
        for j in range(1, DEPTH):
            dot_ = dot_ + p[j:j + 1] * dp[j]
        o_ref[...] = jnp.zeros_like(o_ref)
        for j in range(DEPTH):
            o_ref[j:j + 1, :] = p[j:j + 1] * (dp[j] - dot_)

    return pl.pallas_call(
        body, name="lb_bwd",
        out_shape=jax.ShapeDtypeStruct((8, HG_WIDTH), F32),
    )(dlbrows, lb_logits)


def adamw(w, g, m, v, *, name):
    R, C = w.shape
    tr = next((t for t in (512, 256, 128, 64, 32, 16, 8) if R % t == 0), R)

    def body(w_ref, g_ref, m_ref, v_ref, d_ref, m2_ref, v2_ref):
        gv = g_ref[...]
        m2 = ADAM_B1 * m_ref[...] + (1.0 - ADAM_B1) * gv
        v2 = ADAM_B2 * v_ref[...] + (1.0 - ADAM_B2) * (gv * gv)
        mh = m2 / (1.0 - ADAM_B1 ** ADAM_STEP)
        vh = v2 / (1.0 - ADAM_B2 ** ADAM_STEP)
        d_ref[...] = -ADAM_LR * (mh / (jnp.sqrt(vh) + ADAM_EPS) + ADAM_WD * w_ref[...])
        m2_ref[...] = m2
        v2_ref[...] = v2

    spec = pl.BlockSpec((tr, C), lambda i: (i, 0))
    return pl.pallas_call(
        body, name=name, grid=(R // tr,),
        in_specs=[spec] * 4, out_specs=[spec] * 3,
        out_shape=[jax.ShapeDtypeStruct((R, C), F32)] * 3,
        compiler_params=_params("parallel"),
    )(w, g, m, v)


def sum_slots(x, *, out_dtype, name):
    n, R, C = x.shape
    tr = 848 if R % 848 == 0 else R

    def body(x_ref, o_ref):
        acc = x_ref[0].astype(F32)
        for k in range(1, n):
            acc = acc + x_ref[k].astype(F32)
        o_ref[...] = acc.astype(o_ref.dtype)

    return pl.pallas_call(
        body, name=name, grid=(R // tr,),
        in_specs=[pl.BlockSpec((n, tr, C), lambda i: (0, i, 0))],
        out_specs=pl.BlockSpec((tr, C), lambda i: (i, 0)),
        out_shape=jax.ShapeDtypeStruct((R, C), out_dtype),
        compiler_params=_params("parallel"),
    )(x)


MESH = pl.DeviceIdType.MESH
ANY = pl.BlockSpec(memory_space=pl.ANY)


def _place():
    return lax.axis_index("x"), lax.axis_index("y"), lax.axis_index("c")


def _other_chips(x, y):
    return [(1 - x, y), (x, 1 - y), (1 - x, 1 - y)]


def small_allgather(buf):
    R, C = buf.shape

    def body(x_ref, out_ref, send_sems, recv_sems, local_sem):
        x, y, c = _place()
        me, sibling = (x, y, c), (x, y, 1 - c)
        chips = _other_chips(x, y)

        def slot(px, py, pc):
            return out_ref.at[4 * px + 2 * py + pc]

        def copy(k, block, to, src=None):
            return pltpu.make_async_remote_copy(
                src_ref=slot(*block) if src is None else src, dst_ref=slot(*block),
                send_sem=send_sems.at[k], recv_sem=recv_sems.at[k],
                device_id=to, device_id_type=MESH)

        mine = pltpu.make_async_copy(x_ref, slot(*me), local_sem)
        mine.start()
        first = [copy(0, me, sibling, src=x_ref)]
        first += [copy(1 + r, me, (*chip, c), src=x_ref) for r, chip in enumerate(chips)]
        for cp in first:
            cp.start()
        passed = [copy(4 + r, (*chip, c), sibling) for r, chip in enumerate(chips)]
        for r, chip in enumerate(chips):
            copy(1 + r, (*chip, c), me).wait_recv()
            passed[r].start()
        copy(0, sibling, me).wait_recv()
        for r, chip in enumerate(chips):
            copy(4 + r, (*chip, 1 - c), me).wait_recv()
        for cp in first + passed:
            cp.wait_send()
        mine.wait()

    return pl.pallas_call(
        body, name="small_allgather",
        out_shape=jax.ShapeDtypeStruct((8, R, C), buf.dtype),
        in_specs=[pl.BlockSpec(memory_space=pltpu.VMEM)],
        out_specs=pl.BlockSpec(memory_space=pltpu.VMEM),
        scratch_shapes=[pltpu.SemaphoreType.DMA((7,)), pltpu.SemaphoreType.DMA((7,)),
                        pltpu.SemaphoreType.DMA],
    )(buf)


def weights_allgather(wp):
    def body(w_ref, g_ref, send_sems, recv_sems, local_sem):
        gather = _Gather(w_ref, g_ref, send_sems, recv_sems, local_sem)
        gather.start()
        gather.finish()

    return pl.pallas_call(
        body, name="weights_allgather",
        out_shape=jax.ShapeDtypeStruct((4,) + wp.shape, wp.dtype),
        in_specs=[ANY], out_specs=ANY,
        scratch_shapes=_Gather.SCRATCH,
    )(wp)


class _Gather:
    SCRATCH = [pltpu.SemaphoreType.DMA((6,)), pltpu.SemaphoreType.DMA((6,)), pltpu.SemaphoreType.DMA]

    def __init__(self, w_ref, g_ref, send_sems, recv_sems, local_sem):
        self.w_ref, self.g_ref, self.local_sem = w_ref, g_ref, local_sem
        self.send_sems, self.recv_sems = send_sems, recv_sems
        self.x, self.y, self.c = _place()
        self.chips = _other_chips(self.x, self.y)
        half = w_ref.shape[0] // 2
        self.mine = pl.ds(pl.multiple_of(self.c * half, 16), half)
        self.theirs = pl.ds(pl.multiple_of((1 - self.c) * half, 16), half)

    def _copy(self, k, chip_block, rows, to, src=None):
        dst = self.g_ref.at[chip_block, rows]
        return pltpu.make_async_remote_copy(
            src_ref=dst if src is None else src, dst_ref=dst,
            send_sem=self.send_sems.at[k], recv_sem=self.recv_sems.at[k],
            device_id=to, device_id_type=MESH)

    def _own(self):
        return pltpu.make_async_copy(self.w_ref, self.g_ref.at[2 * self.x + self.y], self.local_sem)

    def _first(self):
        return [self._copy(r, 2 * self.x + self.y, self.mine, (*chip, self.c), src=self.w_ref.at[self.mine])
                for r, chip in enumerate(self.chips)]

    def start(self):
        self._own().start()
        for cp in self._first():
            cp.start()

    def finish(self):
        sibling = (self.x, self.y, 1 - self.c)
        passed = [self._copy(3 + r, 2 * chip[0] + chip[1], self.mine, sibling) for r, chip in enumerate(self.chips)]
        for r, chip in enumerate(self.chips):
            self._copy(r, 2 * chip[0] + chip[1], self.mine, (*chip, self.c)).wait_recv()
            passed[r].start()
        for r, chip in enumerate(self.chips):
            self._copy(3 + r, 2 * chip[0] + chip[1], self.theirs, sibling).wait_recv()
        for cp in self._first() + passed:
            cp.wait_send()
        self._own().wait()


def sibling_swap(v, *, name):
    def body(v_ref, got_ref, send_sem, recv_sem):
        x, y, c = _place()
        cp = pltpu.make_async_remote_copy(
            src_ref=v_ref, dst_ref=got_ref, send_sem=send_sem, recv_sem=recv_sem,
            device_id=(x, y, 1 - c), device_id_type=MESH)
        cp.start()
        cp.wait()

    return pl.pallas_call(
        body, name=name,
        out_shape=jax.ShapeDtypeStruct(v.shape, v.dtype),
        in_specs=[ANY], out_specs=ANY,
        scratch_shapes=[pltpu.SemaphoreType.DMA, pltpu.SemaphoreType.DMA],
    )(v)


def chip_exchange(q):
    def body(q_ref, r_ref, send_sems, recv_sems, local_sem):
        exchange = _Exchange(q_ref, r_ref, send_sems, recv_sems, local_sem)
        exchange.start()
        exchange.finish()

    return pl.pallas_call(
        body, name="chip_exchange",
        out_shape=jax.ShapeDtypeStruct(q.shape, q.dtype),
        in_specs=[ANY], out_specs=ANY,
        scratch_shapes=_Exchange.SCRATCH,
    )(q)


class _Exchange:
    SCRATCH = [pltpu.SemaphoreType.DMA((3,)), pltpu.SemaphoreType.DMA((3,)), pltpu.SemaphoreType.DMA]

    def __init__(self, q_ref, r_ref, send_sems, recv_sems, local_sem):
        self.q_ref, self.r_ref, self.local_sem = q_ref, r_ref, local_sem
        self.send_sems, self.recv_sems = send_sems, recv_sems
        self.x, self.y, self.c = _place()
        self.j = 2 * self.x + self.y
        self.chips = _other_chips(self.x, self.y)

    def _copy(self, r, src_block, dst_block, chip):
        return pltpu.make_async_remote_copy(
            src_ref=self.q_ref.at[src_block], dst_ref=self.r_ref.at[dst_block],
            send_sem=self.send_sems.at[r], recv_sem=self.recv_sems.at[r],
            device_id=(*chip, self.c), device_id_type=MESH)

    def _own(self):
        return pltpu.make_async_copy(self.q_ref.at[self.j], self.r_ref.at[self.j], self.local_sem)

    def _sends(self):
        return [self._copy(r, 2 * chip[0] + chip[1], self.j, chip) for r, chip in enumerate(self.chips)]

    def start(self):
        self._own().start()
        for cp in self._sends():
            cp.start()

    def finish(self):
        for r, chip in enumerate(self.chips):
            jr = 2 * chip[0] + chip[1]
            self._copy(r, jr, jr, chip).wait_recv()
        for cp in self._sends():
            cp.wait_send()
        self._own().wait()


N_CHIPS = 4
_PACK = (("w_in", 704), ("w_out", 256), ("w_up", 1408), ("w_down", 704), ("w_ple_gate", 256), ("w_ple_up", 64))
LAYER_ROWS = sum(r for _, r in _PACK)
PACK_ROWS = DEPTH * LAYER_ROWS
HALF_ROWS = PACK_ROWS // 2


def _pack_shards(sh):
    parts = []
    for i in range(DEPTH):
        for name, rows in _PACK:
            parts.append(sh[name][i].reshape(rows, D_MODEL))
    return jnp.concatenate(parts, axis=0)


def _unpack_shards(slab):
    shapes = {"w_in": (D_MODEL, IN_WIDTH // N_CHIPS), "w_out": (D_MODEL // N_CHIPS, D_MODEL),
              "w_up": (D_MODEL, 2 * D_FF // N_CHIPS), "w_down": (D_FF // N_CHIPS, D_MODEL),
              "w_ple_gate": (D_MODEL // N_CHIPS, D_MODEL), "w_ple_up": (PLE_DIM, D_MODEL // N_CHIPS)}
    out = {name: [] for name, _ in _PACK}
    off = 0
    for i in range(DEPTH):
        for name, rows in _PACK:
            out[name].append(slab[off:off + rows].reshape(shapes[name]))
            off += rows
    return {k: jnp.stack(v) for k, v in out.items()}


_COL_SHARDED = ("w_in", "w_up", "w_ple_up")


def _full_from_chips(g, layer):
    per_chip = [_unpack_shards_layer(g[k], layer) for k in range(N_CHIPS)]
    return {name: jnp.concatenate([pc[name] for pc in per_chip], axis=1 if name in _COL_SHARDED else 0)
            for name, _ in _PACK}


def _unpack_shards_layer(slab, layer):
    shapes = {"w_in": (D_MODEL, IN_WIDTH // N_CHIPS), "w_out": (D_MODEL // N_CHIPS, D_MODEL),
              "w_up": (D_MODEL, 2 * D_FF // N_CHIPS), "w_down": (D_FF // N_CHIPS, D_MODEL),
              "w_ple_gate": (D_MODEL // N_CHIPS, D_MODEL), "w_ple_up": (PLE_DIM, D_MODEL // N_CHIPS)}
    out = {}
    off = layer * LAYER_ROWS
    for name, rows in _PACK:
        out[name] = slab[off:off + rows].reshape(shapes[name])
        off += rows
    return out


def _split_to_chips(full, name):
    r, c = full.shape
    if name in _COL_SHARDED:
        full = full.reshape(r, N_CHIPS, c // N_CHIPS).transpose(1, 0, 2)
    return full.reshape(N_CHIPS, -1, D_MODEL)


_SMALL = (("loss", 128), ("g_final", 1024), ("g_mix", 4096), ("lb_logits", 2048), ("hg_norm_g", 2048),
          ("attn_sinks", 128), ("g_ffn", 4096), ("conv_w", 4 * 3 * D_FF), ("conv_b", 4 * D_FF), ("g_ple", 4096))
SMALL_ROWS = 496


def _pack_small(d):
    parts = []
    for name, n in _SMALL:
        v = d[name].reshape(-1).astype(F32)
        parts.append(jnp.pad(v, (0, n - v.shape[0])))
    flat = jnp.concatenate(parts)
    return jnp.pad(flat, (0, SMALL_ROWS * 128 - flat.shape[0])).reshape(SMALL_ROWS, 128)


def _unpack_small(buf, shapes):
    flat = buf.reshape(-1)
    out, off = {}, 0
    for name, n in _SMALL:
        size = 1
        for s in shapes[name]:
            size *= s
        out[name] = flat[off:off + size].reshape(shapes[name])
        off += n
    return out


WEIGHT_ORDER = ('g_mix', 'w_in', 'lb_logits', 'hg_norm_g', 'attn_sinks', 'w_out', 'g_ffn', 'w_up', 'conv_w',
                'conv_b', 'w_down', 'g_ple', 'w_ple_gate', 'w_ple_up', 'g_final')


def kernel(x, p, g_mix, w_in, lb_logits, hg_norm_g, attn_sinks, w_out, g_ffn, w_up, conv_w, conv_b, w_down, g_ple, w_ple_gate, w_ple_up, g_final, loss_target, m_g_mix, m_w_in, m_lb_logits, m_hg_norm_g, m_attn_sinks, m_w_out, m_g_ffn, m_w_up, m_conv_w, m_conv_b, m_w_down, m_g_ple, m_w_ple_gate, m_w_ple_up, m_g_final, v_g_mix, v_w_in, v_lb_logits, v_hg_norm_g, v_attn_sinks, v_w_out, v_g_ffn, v_w_up, v_conv_w, v_conv_b, v_w_down, v_g_ple, v_w_ple_gate, v_w_ple_up, v_g_final):
    W = dict(g_mix=g_mix, w_in=w_in, lb_logits=lb_logits, hg_norm_g=hg_norm_g, attn_sinks=attn_sinks,
             w_out=w_out, g_ffn=g_ffn, w_up=w_up, conv_w=conv_w, conv_b=conv_b, w_down=w_down, g_ple=g_ple,
             w_ple_gate=w_ple_gate, w_ple_up=w_ple_up, g_final=g_final)
    M = dict(g_mix=m_g_mix, w_in=m_w_in, lb_logits=m_lb_logits, hg_norm_g=m_hg_norm_g, attn_sinks=m_attn_sinks,
             w_out=m_w_out, g_ffn=m_g_ffn, w_up=m_w_up, conv_w=m_conv_w, conv_b=m_conv_b, w_down=m_w_down,
             g_ple=m_g_ple, w_ple_gate=m_w_ple_gate, w_ple_up=m_w_ple_up, g_final=m_g_final)
    V = dict(g_mix=v_g_mix, w_in=v_w_in, lb_logits=v_lb_logits, hg_norm_g=v_hg_norm_g, attn_sinks=v_attn_sinks,
             w_out=v_w_out, g_ffn=v_g_ffn, w_up=v_w_up, conv_w=v_conv_w, conv_b=v_conv_b, w_down=v_w_down,
             g_ple=v_g_ple, w_ple_gate=v_w_ple_gate, w_ple_up=v_w_ple_up, g_final=v_g_final)
    S = x.shape[1]
    hg_rows = min(ROW_TILE, S)
    xi, yi, ci = _place()
    chip = 2 * xi + yi

    slab = _pack_shards({n: W[n] for n, _ in _PACK}).astype(BF16).reshape(DEPTH, LAYER_ROWS, D_MODEL)
    gathered = weights_allgather(slab[0])
    cw_shard = jnp.pad(conv_w.reshape(-1), (0, 72 * 128 - conv_w.size)).reshape(72, 128)
    cw_all = small_allgather(cw_shard)
    cw_full = jnp.concatenate(
        [cw_all[2 * k].reshape(-1)[:conv_w.size].reshape(conv_w.shape) for k in range(N_CHIPS)], axis=2)
    lbrows = lb_fwd(lb_logits)

    h = x[0]
    saved = []
    for i in range(DEPTH):
        wf = _full_from_chips(gathered, 0)
        lbr = lbrows[8 * i:8 * i + 8]
        ng = hg_norm_g[i][None]
        sinks_b = jnp.pad(jnp.repeat(attn_sinks[i].reshape(AT_KV_HEADS, 1, AT_GROUP), WINDOW, axis=2),
                          ((0, 0), (0, 7), (0, 0))).reshape(8 * AT_KV_HEADS, GROUP_LANES)
        cw8 = jnp.pad(cw_full[i], ((0, 5), (0, 0)))
        cb = conv_b[i][None]
        if i == 0:
            u = rmsnorm_fwd(h, g_mix[0][None], name="rmsnorm_fwd")
        proj = mm(u, wf["w_in"], name="mm_in")
        if i + 1 < DEPTH:
            y, o_raw, states, gathered = hgrn_fwd(proj, lbr, ng, D_MODEL, rows=hg_rows, gather=slab[i + 1])
        else:
            y, o_raw, states = hgrn_fwd(proj, lbr, ng, D_MODEL, rows=hg_rows)
        y = swa_fwd(proj, sinks_b, y)
        h1, u2 = mm(y, wf["w_out"], res=h, norm_g=g_ffn[i][None], name="mm_out")
        hh = mm(u2, wf["w_up"], out_dtype=BF16, name="mm_up")
        act = convffn_fwd(hh, cw8, cb)
        h2, u3 = mm(act, wf["w_down"], res=h1, norm_g=g_ple[i][None], name="mm_down")
        gpre = mm(u3, wf["w_ple_gate"], name="mm_gate")
        next_g = g_mix[i + 1] if i + 1 < DEPTH else g_final
        h3, u_next = ple_fwd(h2, gpre, p[i, 0], wf["w_ple_up"], next_g[None])
        saved.append(dict(wf=wf, lbr=lbr, ng=ng, sinks_b=sinks_b, cw8=cw8, cb=cb, h=h, u=u, proj=proj,
                          o_raw=o_raw, states=states, y=y, h1=h1, u2=u2, hh=hh, act=act, h2=h2, u3=u3,
                          gpre=gpre))
        h, u = h3, u_next

    dh, dhb, loss_acc, dg_final = loss_head(h, g_final[None], loss_target[0])

    gfull = {n: [None] * DEPTH for n, _ in _PACK}
    gsmall = {n: [None] * DEPTH for n in ("g_mix", "hg_norm_g", "attn_sinks", "g_ffn", "conv_w", "conv_b", "g_ple")}
    dlbrows = [None] * DEPTH
    half_rows = LAYER_ROWS // 2
    from_chips = [None] * DEPTH
    pending = None
    for i in reversed(range(DEPTH)):
        s = saved[i]
        wf = s["wf"]
        dpu, dgp = ple_bwd(dh, s["gpre"], p[i, 0], wf["w_ple_up"])
        gfull["w_ple_up"][i] = mm_tn(p[i, 0], dpu, name="mm_tn_pu")
        gfull["w_ple_gate"][i] = mm_tn(s["u3"], dgp, name="mm_tn_gate")
        dh2, dh2b, dg = mm(dgp, wf["w_ple_gate"], nt=True, rms_bwd=(s["h2"], g_ple[i][None], dh),
                           name="mm_nt_gate")
        gsmall["g_ple"][i] = dg[0]
        gfull["w_down"][i] = mm_tn(s["act"], dh2b, name="mm_tn_down")
        dact = mm(dh2b, wf["w_down"], nt=True, out_dtype=BF16, name="mm_nt_down")
        da, db, dcw = convffn_bwd(s["hh"], dact, s["cw8"], s["cb"])
        gsmall["conv_w"][i] = dcw[0:3]
        gsmall["conv_b"][i] = dcw[3]
        gfull["w_up"][i] = jnp.concatenate([mm_tn(s["u2"], da, name="mm_tn_up"),
                                            mm_tn(s["u2"], db, name="mm_tn_up")], axis=1)
        dh1, dh1b, dg = mm((da, db), wf["w_up"], nt=True, rms_bwd=(s["h1"], g_ffn[i][None], dh2),
                           name="mm_nt_up")
        gsmall["g_ffn"][i] = dg[0]
        gfull["w_out"][i] = mm_tn(s["y"], dh1b, name="mm_tn_out")
        dy = mm(dh1b, wf["w_out"], nt=True, name="mm_nt_out")
        dq_at, dko, dkp, dvo, dvp, dsk = swa_bwd(s["proj"], s["sinks_b"], dy)
        gsmall["attn_sinks"][i] = dsk.reshape(AT_KV_HEADS, 8, GROUP_LANES)[:, 1:1 + AT_GROUP, 0].reshape(-1)
        hg_args = (s["proj"], s["o_raw"], s["states"], dy, s["lbr"], s["ng"])
        if pending is None:
            hq, hz, hv, hgp, dlbr, dng = hgrn_bwd(*hg_args, rows=hg_rows)
        else:
            hq, hz, hv, hgp, dlbr, dng, got = hgrn_bwd(*hg_args, rows=hg_rows, exchange=pending[1])
            from_chips[pending[0]] = got
        dlbrows[i] = dlbr
        gsmall["hg_norm_g"][i] = dng[0]
        dproj = assemble_dproj((hq, hz, hv, hgp), dq_at, dko, dkp, dvo, dvp, rows=hg_rows)
        gfull["w_in"][i] = mm_tn(s["u"], dproj, name="mm_tn_in")
        dh, dhb, dg = mm(dproj, wf["w_in"], nt=True, rms_bwd=(s["h"], g_mix[i][None], dh1), name="mm_nt_in")
        gsmall["g_mix"][i] = dg[0]
        pk = jnp.concatenate([_split_to_chips(gfull[name][i], name) for name, _ in _PACK], axis=1).astype(BF16)
        pk = pk.reshape(N_CHIPS, 2, half_rows, D_MODEL)
        p_mine = lax.dynamic_index_in_dim(pk, ci, axis=1, keepdims=False)
        p_other = lax.dynamic_index_in_dim(pk, 1 - ci, axis=1, keepdims=False)
        from_sib = sibling_swap(p_other, name="sibling_swap_partials")
        pair = sum_slots(jnp.stack([p_mine.reshape(-1, D_MODEL), from_sib.reshape(-1, D_MODEL)]),
                         out_dtype=BF16, name="sum_pair")
        pending = (i, pair.reshape(N_CHIPS, half_rows, D_MODEL))
    from_chips[0] = chip_exchange(pending[1])
    grad_x = dh[None]
    dlb_logits = lb_bwd(jnp.concatenate(dlbrows, axis=0), lb_logits)[0:DEPTH]

    mine_sum = jnp.concatenate([sum_slots(from_chips[i], out_dtype=F32, name="sum_chips") for i in range(DEPTH)],
                               axis=0)
    sib_sum = sibling_swap(mine_sum, name="sibling_swap_sums")
    mine_sum = mine_sum.reshape(DEPTH, half_rows, D_MODEL)
    sib_sum = sib_sum.reshape(DEPTH, half_rows, D_MODEL)
    lo = jnp.where(ci == 0, mine_sum, sib_sum)
    hi = jnp.where(ci == 0, sib_sum, mine_sum)
    gshard = _unpack_shards(jnp.concatenate([lo, hi], axis=1).reshape(PACK_ROWS, D_MODEL))

    small = dict(loss=loss_acc[0, 0:1], g_final=dg_final[0], lb_logits=dlb_logits,
                 **{n: jnp.stack(v) for n, v in gsmall.items()})
    small_sum = sum_slots(small_allgather(_pack_small(small)), out_dtype=F32, name="sum_small")
    shapes = {n: W[n].shape for n in W}
    shapes["loss"] = (1,)
    shapes["conv_w"] = (DEPTH, 3, D_FF)
    gs = _unpack_small(small_sum, shapes)
    loss = gs["loss"][0]
    cshard = conv_w.shape[2]
    grads = dict(gshard)
    for n in ("g_mix", "lb_logits", "hg_norm_g", "attn_sinks", "g_ffn", "conv_b", "g_ple", "g_final"):
        grads[n] = gs[n]
    grads["conv_w"] = lax.dynamic_slice_in_dim(gs["conv_w"], chip * cshard, cshard, axis=2)

    delta, new_m, new_v = {}, {}, {}
    small_names = ("g_final", "g_mix", "lb_logits", "hg_norm_g", "attn_sinks", "g_ffn", "conv_b", "g_ple")
    sshapes = {n: W[n].shape for n in small_names}

    def pack_s(d):
        z = dict(d)
        z["loss"] = jnp.zeros((1,), F32)
        z["conv_w"] = jnp.zeros((1,), F32)
        return _pack_small(z)

    sd, sm, sv = adamw(pack_s(W), pack_s(grads), pack_s(M), pack_s(V), name="adamw_small")
    for out, buf in ((delta, sd), (new_m, sm), (new_v, sv)):
        un = _unpack_small(buf, {**sshapes, "loss": (1,), "conv_w": (1,)})
        for n in small_names:
            out[n] = un[n]
    for n in ("w_in", "w_out", "w_up", "w_down", "w_ple_gate", "w_ple_up", "conv_w"):
        shp = W[n].shape
        two_d = (-1, shp[-1])
        d_, m_, v_ = adamw(W[n].reshape(two_d), grads[n].reshape(two_d), M[n].reshape(two_d),
                           V[n].reshape(two_d), name="adamw_" + n)
        delta[n], new_m[n], new_v[n] = d_.reshape(shp), m_.reshape(shp), v_.reshape(shp)

    return (loss, grad_x, *[grads[n] for n in WEIGHT_ORDER], *[delta[n] for n in WEIGHT_ORDER],
            *[new_m[n] for n in WEIGHT_ORDER], *[new_v[n] for n in WEIGHT_ORDER])
```

```python
import functools

import jax
import jax.numpy as jnp
from jax import lax
from jax.experimental import pallas as pl
from jax.experimental.pallas import tpu as pltpu

F32 = jnp.float32
BF16 = jnp.bfloat16

D_MODEL = 1024
DEPTH = 4
PLE_DIM = 256
HG_WIDTH = 512
HG_HEADS = 4
HG_DK = 128
HG_CHUNK = 64
HG_SUB = 16
AT_WIDTH = 512
AT_HEAD_DIM = 64
AT_Q_HEADS = 8
AT_KV_HEADS = 2
AT_GROUP = 4
WINDOW = 128
D_FF = 2816
IN_WIDTH = 2816
EPS = 1e-6
MASK_VALUE = -1e30
LB_FLOOR = 1e-30

ADAM_LR = 0.001
ADAM_B1 = 0.9
ADAM_B2 = 0.999
ADAM_EPS = 1e-08
ADAM_WD = 0.01
ADAM_STEP = 10

VMEM_LIMIT = 48 * 1024 * 1024


def _params(*sem):
    return pltpu.CompilerParams(dimension_semantics=sem, vmem_limit_bytes=VMEM_LIMIT)


def _dot(a, b, dims=(((1,), (0,)), ((), ()))):
    return lax.dot_general(a.astype(BF16), b.astype(BF16), dims, preferred_element_type=F32)


def _dot_nt(a, b):
    return _dot(a, b, (((1,), (1,)), ((), ())))


def _dot_tn(a, b):
    return _dot(a, b, (((0,), (0,)), ((), ())))


def _dot_f32(a, b, dims=(((1,), (0,)), ((), ()))):
    return lax.dot_general(a, b, dims, preferred_element_type=F32, precision=lax.Precision.HIGHEST)


def _sigmoid(x):
    return 0.5 * jnp.tanh(0.5 * x) + 0.5


def _logsig(x):
    return jnp.minimum(x, 0.0) - jnp.log(1.0 + jnp.exp(-jnp.abs(x)))


def _colsum(x):
    return jnp.sum(x, axis=0, keepdims=True)


def _rowsum(x):
    return jnp.sum(x, axis=1, keepdims=True)


def _tri(n):
    r = lax.broadcasted_iota(jnp.int32, (n, n), 0)
    c = lax.broadcasted_iota(jnp.int32, (n, n), 1)
    return (r >= c).astype(F32)


def _hg_gates(qp, z, a, c, oml):
    sq = _sigmoid(qp)
    q = qp * sq
    t = c + _logsig(z)
    mx = jnp.maximum(a, t)
    logf = mx + jnp.log(1.0 + jnp.exp(-jnp.abs(a - t)))
    snz = _sigmoid(-z)
    k = oml * snz
    return q, sq, t, logf, snz, k


_HEADS = range(HG_HEADS)


def _lanes(h):
    return slice(h * HG_DK, (h + 1) * HG_DK)


def _head(x, h):
    return x[:, _lanes(h)]


def _row_masks():
    row8 = lax.broadcasted_iota(jnp.int32, (8, HG_DK), 0)
    return [None] + [jnp.where(row8 >= j, 0.0, MASK_VALUE) for j in range(1, 8)]


def _hg_chunk_fwd(q, k, v, logf, st, b_s, k_s, v_s):
    C, U = HG_CHUNK, HG_SUB
    tri = _tri(C)
    b = [_dot_f32(tri, logf[h]) for h in _HEADS]
    for h in _HEADS:
        b_s[h] = b[h]
        k_s[h] = k[h]
        v_s[h] = v[h]
    o = [_dot_nt(q[h] * jnp.exp(b[h]), st[h]) for h in _HEADS]
    bl = [b[h][C - 1:C] for h in _HEADS]
    upd = [_dot_tn(v[h], k[h] * jnp.exp(bl[h] - b[h])) for h in _HEADS]
    rows = lax.broadcasted_iota(jnp.int32, (C, HG_DK), 0)
    nmask = _row_masks()
    outs = [[] for _ in _HEADS]
    for i in range(C // U):
        lo = i * U
        b_i = [b[h][lo:lo + U] for h in _HEADS]
        q_i = [q[h][lo:lo + U] for h in _HEADS]
        o_i = [o[h][lo:lo + U] for h in _HEADS]
        if i > 0:
            qe = [q_i[h] * jnp.exp(b_i[h] - b_i[h][0:1]) for h in _HEADS]
            ke = [jnp.where(rows < lo, k[h] * jnp.exp(jnp.minimum(b_i[h][0:1] - b[h], 0.0)), 0.0) for h in _HEADS]
            att = [_dot_nt(qe[h], ke[h]) for h in _HEADS]
            off = [_dot(att[h], v[h]) for h in _HEADS]
            o_i = [o_i[h] + off[h] for h in _HEADS]
        pieces = [[o_i[h][8 * f:8 * f + 8] for f in range(U // 8)] for h in _HEADS]
        for s in range(U):
            for f in range(s // 8, U // 8):
                for h in _HEADS:
                    bs = b_s[h, lo + s:lo + s + 1, :]
                    ks = k_s[h, lo + s:lo + s + 1, :]
                    vs = v_s[h, lo + s:lo + s + 1, :]
                    arg = b_i[h][8 * f:8 * f + 8] - bs
                    if s > 8 * f:
                        arg = arg + nmask[s - 8 * f]
                    w = _rowsum(q_i[h][8 * f:8 * f + 8] * jnp.exp(arg) * ks)
                    pieces[h][f] = pieces[h][f] + w * vs
        for h in _HEADS:
            outs[h] += pieces[h]
    o = [jnp.concatenate(outs[h], axis=0) for h in _HEADS]
    st_new = [st[h] * jnp.exp(bl[h]) + upd[h] for h in _HEADS]
    return o, st_new, b


def _hg_post(o, gp, ng):
    rs = lax.rsqrt(jnp.mean(o * o, axis=1, keepdims=True) + EPS)
    sg = _sigmoid(gp)
    return o * rs * ng * sg, rs, sg


def hgrn_fwd(proj, lbrows, ng, y_width, *, rows, gather=None):
    S = proj.shape[0]
    C = HG_CHUNK
    cpb = rows // C
    nb = S // rows

    def body(qp_ref, z_ref, v_ref, gp_ref, lb_ref, ng_ref, *rest):
        if gather is not None:
            w_ref, y_ref, o_ref, st_ref, g_ref, st, b_s, k_s, v_s, *sems = rest
            comm = _Gather(w_ref, g_ref, *sems)
        else:
            y_ref, o_ref, st_ref, st, b_s, k_s, v_s = rest

        @pl.when(pl.program_id(0) == 0)
        def _():
            st[...] = jnp.zeros_like(st)
            if gather is not None:
                comm.start()

        a, c, oml = lb_ref[0:1, :], lb_ref[1:2, :], lb_ref[2:3, :]
        ngr = ng_ref[...]

        def chunk(ci, carry):
            off = pl.multiple_of(ci * C, C)
            sl = pl.ds(off, C)
            for h in _HEADS:
                st_ref[h, ci] = st[h]
            gates = [_hg_gates(qp_ref[sl, _lanes(h)], z_ref[sl, _lanes(h)],
                               _head(a, h), _head(c, h), _head(oml, h)) for h in _HEADS]
            q = [g[0] for g in gates]
            logf = [g[3] for g in gates]
            k = [g[5] for g in gates]
            v = [v_ref[sl, _lanes(h)] for h in _HEADS]
            o, st_new, _ = _hg_chunk_fwd(q, k, v, logf, [st[h] for h in _HEADS], b_s, k_s, v_s)
            for h in _HEADS:
                y, _, _ = _hg_post(o[h], gp_ref[sl, _lanes(h)], _head(ngr, h))
                y_ref[sl, _lanes(h)] = y.astype(y_ref.dtype)
                o_ref[sl, _lanes(h)] = o[h]
                st[h] = st_new[h]
            return carry

        lax.fori_loop(0, cpb, chunk, 0)

        if gather is not None:
            pl.when(pl.program_id(0) == nb - 1)(comm.finish)

    col = lambda kblk: pl.BlockSpec((rows, HG_WIDTH), lambda r: (r, kblk))
    in_specs = [col(0), col(1), col(2), col(3), _const_spec((8, HG_WIDTH)), _const_spec((1, HG_WIDTH))]
    out_specs = [col(0), col(0), pl.BlockSpec((HG_HEADS, cpb, HG_DK, HG_DK), lambda r: (0, r, 0, 0))]
    out_shape = [jax.ShapeDtypeStruct((S, y_width), BF16),
                 jax.ShapeDtypeStruct((S, HG_WIDTH), F32),
                 jax.ShapeDtypeStruct((HG_HEADS, S // C, HG_DK, HG_DK), F32)]
    scratch = [pltpu.VMEM((HG_HEADS, HG_DK, HG_DK), F32)] + [pltpu.VMEM((HG_HEADS, C, HG_DK), F32)] * 3
    args = [proj, proj, proj, proj, lbrows, ng]
    if gather is not None:
        in_specs.append(ANY)
        out_specs.append(ANY)
        out_shape.append(jax.ShapeDtypeStruct((N_CHIPS,) + gather.shape, gather.dtype))
        scratch += _Gather.SCRATCH
        args.append(gather)
    return pl.pallas_call(
        body,
        name="hgrn_fwd" if gather is None else "hgrn_fwd_gather",
        grid=(nb,),
        in_specs=in_specs,
        out_specs=out_specs,
        out_shape=out_shape,
        scratch_shapes=scratch,
        compiler_params=_params("arbitrary"),
    )(*args)


def hgrn_bwd(proj, o_raw, states, dy, lbrows, ng, *, rows, exchange=None):
    S = proj.shape[0]
    C, U = HG_CHUNK, HG_SUB
    cpb = rows // C
    nb = S // rows

    def body(qp_ref, z_ref, v_ref, gp_ref, o_ref, st_ref, dy_ref, lb_ref, ng_ref, *rest):
        if exchange is not None:
            (q_ref, dqp_ref, dz_ref, dv_ref, dgp_ref, dlb_ref, dng_ref, r_ref,
             dst, b_s, k_s, v_s, dbs, dks, dvs, *sems) = rest
            comm = _Exchange(q_ref, r_ref, *sems)
        else:
            dqp_ref, dz_ref, dv_ref, dgp_ref, dlb_ref, dng_ref, dst, b_s, k_s, v_s, dbs, dks, dvs = rest

        @pl.when(pl.program_id(0) == 0)
        def _():
            dst[...] = jnp.zeros_like(dst)
            dlb_ref[...] = jnp.zeros_like(dlb_ref)
            dng_ref[...] = jnp.zeros_like(dng_ref)
            if exchange is not None:
                comm.start()

        a, c, oml = lb_ref[0:1, :], lb_ref[1:2, :], lb_ref[2:3, :]
        ngr = ng_ref[...]
        rows_i = lax.broadcasted_iota(jnp.int32, (C, HG_DK), 0)
        nmask = _row_masks()
        tri = _tri(C)
        H = _HEADS

        def chunk(cj, carry):
            ci = cpb - 1 - cj
            off = pl.multiple_of(ci * C, C)
            sl = pl.ds(off, C)
            qp = [qp_ref[sl, _lanes(h)] for h in H]
            v = [v_ref[sl, _lanes(h)] for h in H]
            st = [st_ref[h, ci] for h in H]
            gates = [_hg_gates(qp[h], z_ref[sl, _lanes(h)], _head(a, h), _head(c, h), _head(oml, h)) for h in H]
            q, sq, t, logf, snz, k = ([g[j] for g in gates] for j in range(6))
            b = [_dot_f32(tri, logf[h]) for h in H]
            for h in H:
                b_s[h] = b[h]
                k_s[h] = k[h]
                v_s[h] = v[h]
            do = []
            for h in H:
                o = o_ref[sl, _lanes(h)]
                dyv = dy_ref[sl, _lanes(h)]
                ngh = _head(ngr, h)
                rs = lax.rsqrt(jnp.mean(o * o, axis=1, keepdims=True) + EPS)
                sg = _sigmoid(gp_ref[sl, _lanes(h)])
                xh = o * rs
                dgp_ref[sl, _lanes(h)] = (dyv * xh * ngh * sg * (1.0 - sg)).astype(dgp_ref.dtype)
                don = dyv * sg
                dng_ref[0:1, _lanes(h)] += _colsum(don * xh)
                dxh = don * ngh
                do.append(rs * (dxh - xh * jnp.mean(dxh * xh, axis=1, keepdims=True)))
            eb = [jnp.exp(b[h]) for h in H]
            qb = [q[h] * eb[h] for h in H]
            dstv = [dst[h] for h in H]
            bl = [b[h][C - 1:C] for h in H]
            el = [jnp.exp(bl[h]) for h in H]
            ex = [jnp.exp(bl[h] - b[h]) for h in H]
            kd = [k[h] * ex[h] for h in H]
            dqb = [_dot(do[h], st[h]) for h in H]
            dst_acc = [_dot_tn(do[h], qb[h]) for h in H]
            dv0 = [_dot_nt(kd[h], dstv[h]) for h in H]
            dkd = [_dot(v[h], dstv[h]) for h in H]
            dq = [dqb[h] * eb[h] for h in H]
            for h in H:
                g2 = dkd[h] * kd[h]
                dbl = _colsum(dstv[h] * st[h]) * el[h] + _colsum(g2)
                dst[h] = dstv[h] * el[h] + dst_acc[h]
                dbs[h] = dqb[h] * qb[h] - g2
                dks[h] = dkd[h] * ex[h]
                dvs[h] = dv0[h]
                dbs[h, C - 1:C, :] += dbl
            dq_parts = [[] for _ in H]
            for i in range(C // U):
                lo = i * U
                b_i = [b[h][lo:lo + U] for h in H]
                q_i = [q[h][lo:lo + U] for h in H]
                do_i = [do[h][lo:lo + U] for h in H]
                dq_i = [dq[h][lo:lo + U] for h in H]
                db_i = [jnp.zeros((U, HG_DK), F32) for _ in H]
                if i > 0:
                    e1 = [jnp.exp(b_i[h] - b_i[h][0:1]) for h in H]
                    qe = [q_i[h] * e1[h] for h in H]
                    e2 = [jnp.where(rows_i < lo, jnp.exp(jnp.minimum(b_i[h][0:1] - b[h], 0.0)), 0.0) for h in H]
                    ke = [k[h] * e2[h] for h in H]
                    att = [_dot_nt(qe[h], ke[h]) for h in H]
                    datt = [_dot_nt(do_i[h], v[h]) for h in H]
                    dv_add = [_dot_tn(att[h], do_i[h]) for h in H]
                    dqe = [_dot(datt[h], ke[h]) for h in H]
                    dke = [_dot_tn(datt[h], qe[h]) for h in H]
                    for h in H:
                        dvs[h] += dv_add[h]
                        dq_i[h] = dq_i[h] + dqe[h] * e1[h]
                        g = dqe[h] * qe[h]
                        db_i[h] = db_i[h] + g
                        gk = dke[h] * ke[h]
                        dks[h] += dke[h] * e2[h]
                        dbs[h] -= gk
                        dbs[h, lo:lo + 1, :] += _colsum(gk) - _colsum(g)
                nf = U // 8
                dq8 = [[dq_i[h][8 * f:8 * f + 8] for f in range(nf)] for h in H]
                db8 = [[db_i[h][8 * f:8 * f + 8] for f in range(nf)] for h in H]
                for s in range(U):
                    row = slice(lo + s, lo + s + 1)
                    for h in H:
                        bs = b_s[h, row, :]
                        ks = k_s[h, row, :]
                        vs = v_s[h, row, :]
                        tv = tk = tb = None
                        for f in range(s // 8, nf):
                            p8 = slice(8 * f, 8 * f + 8)
                            arg = b_i[h][p8] - bs
                            if s > 8 * f:
                                arg = arg + nmask[s - 8 * f]
                            dec = jnp.exp(arg)
                            qd = q_i[h][p8] * dec
                            y_ = qd * ks
                            w = _rowsum(y_)
                            dw = _rowsum(do_i[h][p8] * vs)
                            g = dw * y_
                            dq8[h][f] = dq8[h][f] + dw * dec * ks
                            db8[h][f] = db8[h][f] + g
                            cv, ck = w * do_i[h][p8], dw * qd
                            tv, tk, tb = (cv, ck, g) if tv is None else (tv + cv, tk + ck, tb + g)
                        dvs[h, row, :] += _colsum(tv)
                        dks[h, row, :] += _colsum(tk)
                        dbs[h, row, :] -= _colsum(tb)
                for h in H:
                    for f in range(nf):
                        dbs[h, lo + 8 * f:lo + 8 * f + 8, :] += db8[h][f]
                    dq_parts[h] += dq8[h]
            dlogf = [_dot_f32(tri, dbs[h], (((0,), (0,)), ((), ()))) for h in H]
            for h in H:
                dqh = jnp.concatenate(dq_parts[h], axis=0)
                dk = dks[h]
                ah, omlh = _head(a, h), _head(oml, h)
                pa = jnp.exp(ah - logf[h])
                pt = jnp.exp(t[h] - logf[h])
                dt = dlogf[h] * pt
                dlb_ref[0:1, _lanes(h)] += _colsum(dlogf[h] * pa)
                dlb_ref[1:2, _lanes(h)] += _colsum(dt)
                dlb_ref[2:3, _lanes(h)] += _colsum(dk * snz[h])
                dz = dt * snz[h] - dk * omlh * snz[h] * (1.0 - snz[h])
                dqp = dqh * (sq[h] + qp[h] * sq[h] * (1.0 - sq[h]))
                dqp_ref[sl, _lanes(h)] = dqp.astype(dqp_ref.dtype)
                dz_ref[sl, _lanes(h)] = dz.astype(dz_ref.dtype)
                dv_ref[sl, _lanes(h)] = dvs[h].astype(dv_ref.dtype)
            return carry

        lax.fori_loop(0, cpb, chunk, 0)

        if exchange is not None:
            pl.when(pl.program_id(0) == nb - 1)(comm.finish)

    rev = lambda r: nb - 1 - r
    col = lambda kblk: pl.BlockSpec((rows, HG_WIDTH), lambda r: (rev(r), kblk))
    acc = _const_spec((8, HG_WIDTH))
    in_specs = [col(0), col(1), col(2), col(3), col(0),
                pl.BlockSpec((HG_HEADS, cpb, HG_DK, HG_DK), lambda r: (0, rev(r), 0, 0)),
                col(0), acc, _const_spec((1, HG_WIDTH))]
    out_specs = [col(0)] * 4 + [acc, acc]
    out_shape = [jax.ShapeDtypeStruct((S, HG_WIDTH), BF16)] * 4 + [jax.ShapeDtypeStruct((8, HG_WIDTH), F32)] * 2
    scratch = [pltpu.VMEM((HG_HEADS, HG_DK, HG_DK), F32)] + [pltpu.VMEM((HG_HEADS, C, HG_DK), F32)] * 6
    args = [proj, proj, proj, proj, o_raw, states, dy, lbrows, ng]
    if exchange is not None:
        in_specs.append(ANY)
        out_specs.append(ANY)
        out_shape.append(jax.ShapeDtypeStruct(exchange.shape, exchange.dtype))
        scratch += _Exchange.SCRATCH
        args.append(exchange)
    return pl.pallas_call(
        body,
        name="hgrn_bwd" if exchange is None else "hgrn_bwd_exchange",
        grid=(nb,),
        in_specs=in_specs,
        out_specs=out_specs,
        out_shape=out_shape,
        scratch_shapes=scratch,
        compiler_params=_params("arbitrary"),
    )(*args)


GROUP_LANES = AT_GROUP * WINDOW


def swa_mask():
    W = WINDOW
    kpos = lax.broadcasted_iota(jnp.int32, (2, 2 * W, GROUP_LANES), 1)
    qpos = (lax.broadcasted_iota(jnp.int32, (2, 2 * W, GROUP_LANES), 2) & (W - 1)) + W
    first = lax.broadcasted_iota(jnp.int32, (2, 2 * W, GROUP_LANES), 0) == 0
    rel = qpos - kpos
    valid = (rel >= 0) & (rel < W) & jnp.logical_not(first & (kpos < W))
    return jnp.where(valid, 0.0, MASK_VALUE).astype(F32)


def _group_lanes(xt, g):
    Dh = AT_HEAD_DIM
    return jnp.concatenate([xt[(g * AT_GROUP + j) * Dh:(g * AT_GROUP + j + 1) * Dh] for j in range(AT_GROUP)],
                           axis=1)


SWA_SCALE = AT_HEAD_DIM ** -0.5


def _swa_softmax_t(s, sink_row):
    m = jnp.maximum(jnp.max(s, axis=0, keepdims=True), sink_row)
    e = jnp.exp(s - m)
    es = jnp.exp(sink_row - m)
    inv = 1.0 / (_colsum(e) + es)
    return e * inv, es * inv


SWA_BLOCKS = 4
_BG = [(b, g) for b in range(SWA_BLOCKS) for g in range(AT_KV_HEADS)]


def _swa_specs(col_q):
    W = WINDOW
    rows = SWA_BLOCKS * W
    prev = lambda n: jnp.maximum(SWA_BLOCKS * n - 1, 0)
    return [pl.BlockSpec((rows, AT_WIDTH), lambda n: (n, col_q)),
            pl.BlockSpec((W, 128), lambda n: (prev(n), 20)),
            pl.BlockSpec((rows, 128), lambda n: (n, 20)),
            pl.BlockSpec((W, 128), lambda n: (prev(n), 21)),
            pl.BlockSpec((rows, 128), lambda n: (n, 21)),
            _const_spec((8 * AT_KV_HEADS, GROUP_LANES)),
            _const_spec((2, 2 * W, GROUP_LANES))]


def _swa_operands(n, q_ref, kp_ref, k_ref, vp_ref, v_ref, sk_ref, mask_ref):
    W, Dh = WINDOW, AT_HEAD_DIM
    k_all = jnp.concatenate([kp_ref[...], k_ref[...]], axis=0)
    v_all = jnp.concatenate([vp_ref[...], v_ref[...]], axis=0)
    kk = [k_all[b * W:(b + 2) * W] for b in range(SWA_BLOCKS)]
    vv = [v_all[b * W:(b + 2) * W] for b in range(SWA_BLOCKS)]
    masks = [mask_ref[jnp.minimum(n, 1)]] + [mask_ref[1]] * (SWA_BLOCKS - 1)
    qt = [(q_ref[b * W:(b + 1) * W, :] * SWA_SCALE).T for b in range(SWA_BLOCKS)]
    kg = {(b, g): kk[b][:, g * Dh:(g + 1) * Dh] for b, g in _BG}
    qg = {(b, g): _group_lanes(qt[b], g) for b, g in _BG}
    s = {bg: _dot(kg[bg], qg[bg]) + masks[bg[0]] for bg in _BG}
    sink = {(b, g): sk_ref[8 * g:8 * g + 1, :] for b, g in _BG}
    return kk, vv, kg, qg, s, sink


def swa_fwd(proj, sink_rows, mask, y):
    S = proj.shape[0]
    W, Dh = WINDOW, AT_HEAD_DIM
    rows = SWA_BLOCKS * W

    def body(q_ref, kp_ref, k_ref, vp_ref, v_ref, sk_ref, mask_ref, y_in, y_ref):
        del y_in
        _, vv, _, _, s, sink = _swa_operands(pl.program_id(0), q_ref, kp_ref, k_ref, vp_ref, v_ref,
                                             sk_ref, mask_ref)
        vt = [v.T for v in vv]
        p = {bg: _swa_softmax_t(s[bg], sink[bg])[0] for bg in _BG}
        ot = {(b, g): _dot(vt[b][g * Dh:(g + 1) * Dh], p[b, g]) for b, g in _BG}
        for b in range(SWA_BLOCKS):
            outs = [ot[b, g][:, j * W:(j + 1) * W] for g in range(AT_KV_HEADS) for j in range(AT_GROUP)]
            y_ref[b * W:(b + 1) * W, :] = jnp.concatenate(outs, axis=0).T.astype(y_ref.dtype)

    return pl.pallas_call(
        body,
        name="swa_fwd",
        grid=(S // rows,),
        in_specs=_swa_specs(4) + [pl.BlockSpec(memory_space=pl.ANY)],
        out_specs=pl.BlockSpec((rows, AT_WIDTH), lambda n: (n, 1)),
        out_shape=jax.ShapeDtypeStruct(y.shape, y.dtype),
        input_output_aliases={7: 0},
        compiler_params=_params("parallel"),
    )(proj, proj, proj, proj, proj, sink_rows, mask, y)


def swa_bwd(proj, sink_rows, mask, dy):
    S = proj.shape[0]
    W, Dh = WINDOW, AT_HEAD_DIM
    rows = SWA_BLOCKS * W
    nsteps = S // rows

    def body(q_ref, kp_ref, k_ref, vp_ref, v_ref, sk_ref, mask_ref, dy_ref,
             dq_ref, dko_ref, dkp_ref, dvo_ref, dvp_ref, dsk_ref):
        n = pl.program_id(0)

        @pl.when(n == 0)
        def _():
            dsk_ref[...] = jnp.zeros_like(dsk_ref)

        kk, vv, _, qg, s, sink = _swa_operands(n, q_ref, kp_ref, k_ref, vp_ref, v_ref, sk_ref, mask_ref)
        kt = [k.T for k in kk]
        dot_ = [dy_ref[b * W:(b + 1) * W, :].T for b in range(SWA_BLOCKS)]
        dog = {(b, g): _group_lanes(dot_[b], g) for b, g in _BG}
        dp = {(b, g): _dot(vv[b][:, g * Dh:(g + 1) * Dh], dog[b, g]) for b, g in _BG}
        pp = {bg: _swa_softmax_t(s[bg], sink[bg]) for bg in _BG}
        delta = {bg: _colsum(dp[bg] * pp[bg][0]) for bg in _BG}
        ds = {bg: pp[bg][0] * (dp[bg] - delta[bg]) for bg in _BG}
        dqt = {(b, g): _dot(kt[b][g * Dh:(g + 1) * Dh], ds[b, g]) * SWA_SCALE for b, g in _BG}
        dk = {bg: _dot_nt(ds[bg], qg[bg]) for bg in _BG}
        dv = {bg: _dot_nt(pp[bg][0], dog[bg]) for bg in _BG}
        for g in range(AT_KV_HEADS):
            tot = -(pp[0, g][1] * delta[0, g])
            for b in range(1, SWA_BLOCKS):
                tot = tot - pp[b, g][1] * delta[b, g]
            dsk_ref[8 * g:8 * g + 1, :] += tot
        for b in range(SWA_BLOCKS):
            r = slice(b * W, (b + 1) * W)
            dqs = [dqt[b, g][:, j * W:(j + 1) * W] for g in range(AT_KV_HEADS) for j in range(AT_GROUP)]
            dq_ref[r, :] = jnp.concatenate(dqs, axis=0).T.astype(dq_ref.dtype)
            dkb = jnp.concatenate([dk[b, g] for g in range(AT_KV_HEADS)], axis=1)
            dvb = jnp.concatenate([dv[b, g] for g in range(AT_KV_HEADS)], axis=1)
            dkp_ref[r, :] = dkb[:W]
            dko_ref[r, :] = dkb[W:]
            dvp_ref[r, :] = dvb[:W]
            dvo_ref[r, :] = dvb[W:]

        @pl.when(n == nsteps - 1)
        def _():
            for g in range(AT_KV_HEADS):
                for j in range(AT_GROUP):
                    tot = _rowsum(dsk_ref[8 * g:8 * g + 1, j * W:(j + 1) * W])
                    dsk_ref[8 * g + 1 + j:8 * g + 2 + j, :] = jnp.broadcast_to(tot, (1, GROUP_LANES))

    kv = pl.BlockSpec((rows, 128), lambda n: (n, 0))
    sk = _const_spec((8 * AT_KV_HEADS, GROUP_LANES))
    return pl.pallas_call(
        body,
        name="swa_bwd",
        grid=(nsteps,),
        in_specs=_swa_specs(4) + [pl.BlockSpec((rows, AT_WIDTH), lambda n: (n, 1))],
        out_specs=[pl.BlockSpec((rows, AT_WIDTH), lambda n: (n, 0)), kv, kv, kv, kv, sk],
        out_shape=[jax.ShapeDtypeStruct((S, AT_WIDTH), BF16)]
                  + [jax.ShapeDtypeStruct((S, 128), F32)] * 4
                  + [jax.ShapeDtypeStruct((8 * AT_KV_HEADS, GROUP_LANES), F32)],
        compiler_params=_params("arbitrary"),
    )(proj, proj, proj, proj, proj, sink_rows, mask, dy)


def assemble_dproj(hg_grads, dq_at, dko, dkp, dvo, dvp, *, rows):
    S = dq_at.shape[0]
    W = WINDOW
    nb = S // W
    bpr = rows // W

    def body(a0, a1, a2, a3, dq, ko, kp, kpn, vo, vp, vpn, out):
        r = pl.program_id(0)
        for i, a in enumerate((a0, a1, a2, a3)):
            out[:, i * HG_WIDTH:(i + 1) * HG_WIDTH] = a[...]
        base = 4 * HG_WIDTH
        out[:, base:base + AT_WIDTH] = dq[...]
        last = (r == pl.num_programs(0) - 1)
        for off, own, pv, pvn in ((base + AT_WIDTH, ko, kp, kpn), (base + AT_WIDTH + 128, vo, vp, vpn)):
            if bpr > 1:
                out[0:rows - W, off:off + 128] = (own[0:rows - W, :] + pv[W:rows, :]).astype(out.dtype)
            nxt = jnp.where(last, 0.0, pvn[...])
            out[rows - W:rows, off:off + 128] = (own[rows - W:rows, :] + nxt).astype(out.dtype)

    hg = pl.BlockSpec((rows, HG_WIDTH), lambda r: (r, 0))
    blk = pl.BlockSpec((rows, 128), lambda r: (r, 0))
    nxt = pl.BlockSpec((W, 128), lambda r: (jnp.minimum((r + 1) * bpr, nb - 1), 0))
    return pl.pallas_call(
        body,
        name="assemble_dproj",
        grid=(S // rows,),
        in_specs=[hg, hg, hg, hg, pl.BlockSpec((rows, AT_WIDTH), lambda r: (r, 0)),
                  blk, blk, nxt, blk, blk, nxt],
        out_specs=pl.BlockSpec((rows, IN_WIDTH), lambda r: (r, 0)),
        out_shape=jax.ShapeDtypeStruct((S, IN_WIDTH), BF16),
        compiler_params=_params("parallel"),
    )(*hg_grads, dq_at, dko, dkp, dkp, dvo, dvp, dvp)


ROW_TILE = 512
COL_TILE = 1408


def _col_tile(n):
    return n if n <= COL_TILE else COL_TILE


def _rms_scale(x):
    return lax.rsqrt(jnp.mean(x * x, axis=1, keepdims=True) + EPS)


def _rms_bwd(d, x, g):
    rs = _rms_scale(x)
    xh = x * rs
    dxh = d * g
    return rs * (dxh - xh * jnp.mean(dxh * xh, axis=1, keepdims=True)), _colsum(d * xh)


def mm(a, b, *, nt=False, out_dtype=F32, res=None, norm_g=None, rms_bwd=None, name):
    parts = a if isinstance(a, tuple) else (a,)
    M, K = parts[0].shape
    N = b.shape[0] if nt else b.shape[1]
    tall = K <= D_MODEL and rms_bwd is None and len(parts) == 1 and M % (2 * ROW_TILE) == 0
    tm = 2 * ROW_TILE if tall else min(ROW_TILE, M)
    tn = _col_tile(N)
    whole_rows = norm_g is not None or rms_bwd is not None
    assert M % tm == 0 and N % tn == 0 and (tn == N or not whole_rows)
    np_ = len(parts)

    def body(*refs):
        a_refs, b_refs, rest = refs[:np_], refs[np_:2 * np_], refs[2 * np_:]
        dot = _dot_nt if nt else _dot
        acc = dot(a_refs[0][...], b_refs[0][...])
        for ar, br in zip(a_refs[1:], b_refs[1:]):
            acc = acc + dot(ar[...], br[...])
        if rms_bwd is not None:
            h_ref, g_ref, dr_ref, dh_ref, dhb_ref, dg_ref = rest

            @pl.when(pl.program_id(1) == 0)
            def _():
                dg_ref[...] = jnp.zeros_like(dg_ref)

            dx, dgp = _rms_bwd(acc, h_ref[...], g_ref[...])
            dg_ref[0:1, :] += dgp
            dh = dr_ref[...] + dx
            dh_ref[...] = dh
            dhb_ref[...] = dh.astype(BF16)
            return
        rest = list(rest)
        if res is not None:
            acc = acc + rest.pop(0)[...]
        if norm_g is not None:
            g_ref = rest.pop(0)
            rest[1][...] = (acc * _rms_scale(acc) * g_ref[...]).astype(BF16)
        rest[0][...] = acc.astype(rest[0].dtype)

    row = pl.BlockSpec((tm, tn), lambda j, i: (i, j))
    in_specs = [pl.BlockSpec((tm, K), lambda j, i: (i, 0)) for _ in parts]
    for kb in range(np_):
        in_specs.append(pl.BlockSpec((tn, K), lambda j, i, kb=kb: (j, kb)) if nt
                        else pl.BlockSpec((K, tn), lambda j, i, kb=kb: (kb, j)))
    args = list(parts) + [b] * np_
    if rms_bwd is not None:
        h, g, dres = rms_bwd
        in_specs += [row, _const_spec((1, N)), row]
        args += [h, g, dres]
        out_specs = [row, row, _const_spec((8, N))]
        out_shape = [jax.ShapeDtypeStruct((M, N), F32), jax.ShapeDtypeStruct((M, N), BF16),
                     jax.ShapeDtypeStruct((8, N), F32)]
        sem = ("arbitrary", "arbitrary")
    else:
        if res is not None:
            in_specs.append(row)
            args.append(res)
        out_specs, out_shape = [row], [jax.ShapeDtypeStruct((M, N), out_dtype)]
        if norm_g is not None:
            in_specs.append(_const_spec((1, N)))
            args.append(norm_g)
            out_specs.append(row)
            out_shape.append(jax.ShapeDtypeStruct((M, N), BF16))
        sem = ("parallel", "parallel")
    out = pl.pallas_call(
        body,
        name=name,
        grid=(N // tn, M // tm),
        in_specs=in_specs,
        out_specs=out_specs,
        out_shape=out_shape,
        compiler_params=_params(*sem),
    )(*args)
    return out[0] if len(out) == 1 else out


def mm_tn(a, b, *, name):
    M, K = a.shape
    N = b.shape[1]
    tm = next((t for t in (4 * ROW_TILE, 2 * ROW_TILE) if M % t == 0), min(ROW_TILE, M))
    tk = _col_tile(K)
    tn = _col_tile(N)
    assert M % tm == 0 and K % tk == 0 and N % tn == 0

    def body(a_ref, b_ref, o_ref):
        @pl.when(pl.program_id(2) == 0)
        def _():
            o_ref[...] = jnp.zeros_like(o_ref)

        o_ref[...] += _dot_tn(a_ref[...], b_ref[...])

    return pl.pallas_call(
        body,
        name=name,
        grid=(K // tk, N // tn, M // tm),
        in_specs=[pl.BlockSpec((tm, tk), lambda k, j, i: (i, k)),
                  pl.BlockSpec((tm, tn), lambda k, j, i: (i, j))],
        out_specs=pl.BlockSpec((tk, tn), lambda k, j, i: (k, j)),
        out_shape=jax.ShapeDtypeStruct((K, N), F32),
        compiler_params=_params("parallel", "parallel", "arbitrary"),
    )(a, b)


def _row_spec(tm, width):
    return pl.BlockSpec((tm, width), lambda i: (i, 0))


def _const_spec(shape):
    return pl.BlockSpec(shape, lambda *_: (0,) * len(shape))


def rmsnorm_fwd(h, g, *, name):
    S, D = h.shape
    tm = min(ROW_TILE, S)

    def body(h_ref, g_ref, u_ref):
        x = h_ref[...]
        rs = lax.rsqrt(jnp.mean(x * x, axis=1, keepdims=True) + EPS)
        u_ref[...] = (x * rs * g_ref[...]).astype(u_ref.dtype)

    return pl.pallas_call(
        body, name=name, grid=(S // tm,),
        in_specs=[_row_spec(tm, D), _const_spec((1, D))],
        out_specs=_row_spec(tm, D),
        out_shape=jax.ShapeDtypeStruct((S, D), BF16),
        compiler_params=_params("parallel"),
    )(h, g)


HALO = 16


def _shift_down(x, edge8, s):
    sh = pltpu.roll(x, s, 0)
    er = pltpu.roll(edge8, s, 0)
    row8 = lax.broadcasted_iota(jnp.int32, er.shape, 0)
    top = jnp.where(row8 < s, er, sh[0:8])
    return jnp.concatenate([top, sh[8:]], axis=0)


def _shift_up(x, s):
    return pltpu.roll(x, x.shape[0] - s, 0)


def _conv_pre(a, prev8, w_ref, cb_ref):
    a1 = _shift_down(a, prev8, 1)
    a2 = _shift_down(a, prev8, 2)
    ac = w_ref[2:3, :] * a + w_ref[1:2, :] * a1 + w_ref[0:1, :] * a2 + cb_ref[...]
    return ac, a1, a2


def convffn_fwd(hh, cw8, cb):
    S = hh.shape[0]
    tm = min(ROW_TILE, S)
    tn = _col_tile(D_FF)
    nj = D_FF // tn

    def body(a_ref, ap_ref, b_ref, w_ref, cb_ref, o_ref):
        prev8 = jnp.where(pl.program_id(1) == 0, 0.0, ap_ref[...].astype(F32)[HALO - 8:HALO])
        ac, _, _ = _conv_pre(a_ref[...].astype(F32), prev8, w_ref, cb_ref)
        o_ref[...] = (ac * _sigmoid(ac) * b_ref[...].astype(F32)).astype(o_ref.dtype)

    rh = tm // HALO
    return pl.pallas_call(
        body, name="convffn_fwd", grid=(nj, S // tm),
        in_specs=[pl.BlockSpec((tm, tn), lambda j, i: (i, j)),
                  pl.BlockSpec((HALO, tn), lambda j, i: (jnp.maximum(i * rh - 1, 0), j)),
                  pl.BlockSpec((tm, tn), lambda j, i: (i, j + nj)),
                  pl.BlockSpec((8, tn), lambda j, i: (0, j)),
                  pl.BlockSpec((1, tn), lambda j, i: (0, j))],
        out_specs=pl.BlockSpec((tm, tn), lambda j, i: (i, j)),
        out_shape=jax.ShapeDtypeStruct((S, D_FF), BF16),
        compiler_params=_params("parallel", "parallel"),
    )(hh, hh, hh, cw8, cb)


def convffn_bwd(hh, dact, cw8, cb):
    S = hh.shape[0]
    tm = min(ROW_TILE, S)
    tn = _col_tile(D_FF)
    nj = D_FF // tn
    ni = S // tm

    def body(a_ref, ap_ref, an_ref, b_ref, bn_ref, d_ref, dn_ref, w_ref, cb_ref, o_a, o_b, dw_ref):
        i = pl.program_id(1)

        @pl.when(i == 0)
        def _():
            dw_ref[...] = jnp.zeros_like(dw_ref)

        up = lambda r: r[...].astype(F32)
        prev8 = jnp.where(i == 0, 0.0, up(ap_ref)[HALO - 8:HALO])
        a = jnp.concatenate([up(a_ref), up(an_ref)[0:8]], axis=0)
        b = jnp.concatenate([up(b_ref), up(bn_ref)[0:8]], axis=0)
        d = jnp.concatenate([up(d_ref), jnp.where(i == ni - 1, 0.0, up(dn_ref)[0:8])], axis=0)
        ac, a1, a2 = _conv_pre(a, prev8, w_ref, cb_ref)
        sa = _sigmoid(ac)
        o_b[...] = (d[0:tm] * ac[0:tm] * sa[0:tm]).astype(o_b.dtype)
        dac = d * b * (sa + ac * sa * (1.0 - sa))
        da = w_ref[2:3, :] * dac + w_ref[1:2, :] * _shift_up(dac, 1) + w_ref[0:1, :] * _shift_up(dac, 2)
        o_a[...] = da[0:tm].astype(o_a.dtype)
        dc = dac[0:tm]
        dw_ref[0:1, :] += _colsum(dc * a2[0:tm])
        dw_ref[1:2, :] += _colsum(dc * a1[0:tm])
        dw_ref[2:3, :] += _colsum(dc * a[0:tm])
        dw_ref[3:4, :] += _colsum(dc)

    rh = tm // HALO
    last = S // HALO - 1
    cur = lambda off: pl.BlockSpec((tm, tn), lambda j, i: (i, j + off))
    nxt = lambda off: pl.BlockSpec((HALO, tn), lambda j, i: (jnp.minimum((i + 1) * rh, last), j + off))
    return pl.pallas_call(
        body, name="convffn_bwd", grid=(nj, ni),
        in_specs=[cur(0),
                  pl.BlockSpec((HALO, tn), lambda j, i: (jnp.maximum(i * rh - 1, 0), j)),
                  nxt(0), cur(nj), nxt(nj), cur(0), nxt(0),
                  pl.BlockSpec((8, tn), lambda j, i: (0, j)),
                  pl.BlockSpec((1, tn), lambda j, i: (0, j))],
        out_specs=[cur(0), cur(0), pl.BlockSpec((8, tn), lambda j, i: (0, j))],
        out_shape=[jax.ShapeDtypeStruct((S, D_FF), BF16), jax.ShapeDtypeStruct((S, D_FF), BF16),
                   jax.ShapeDtypeStruct((8, D_FF), F32)],
        compiler_params=_params("parallel", "arbitrary"),
    )(hh, hh, hh, hh, hh, dact, dact, cw8, cb)


def ple_fwd(h, gpre, p, wpu, norm_g):
    S, D = h.shape
    tm = min(ROW_TILE, S)

    def body(h_ref, g_ref, p_ref, w_ref, ng_ref, o_ref, u_ref):
        out = h_ref[...] + _sigmoid(g_ref[...]) * _dot(p_ref[...], w_ref[...])
        o_ref[...] = out
        u_ref[...] = (out * _rms_scale(out) * ng_ref[...]).astype(BF16)

    return pl.pallas_call(
        body, name="ple_fwd", grid=(S // tm,),
        in_specs=[_row_spec(tm, D), _row_spec(tm, D), _row_spec(tm, PLE_DIM), _const_spec((PLE_DIM, D)),
                  _const_spec((1, D))],
        out_specs=[_row_spec(tm, D), _row_spec(tm, D)],
        out_shape=[jax.ShapeDtypeStruct((S, D), F32), jax.ShapeDtypeStruct((S, D), BF16)],
        compiler_params=_params("parallel"),
    )(h, gpre, p, wpu, norm_g)


def ple_bwd(dh, gpre, p, wpu):
    S, D = dh.shape
    tm = min(ROW_TILE, S)

    def body(d_ref, g_ref, p_ref, w_ref, dpu_ref, dg_ref):
        d = d_ref[...]
        gate = _sigmoid(g_ref[...])
        pu = _dot(p_ref[...], w_ref[...])
        dpu_ref[...] = (d * gate).astype(dpu_ref.dtype)
        dg_ref[...] = (d * pu * gate * (1.0 - gate)).astype(dg_ref.dtype)

    return pl.pallas_call(
        body, name="ple_bwd", grid=(S // tm,),
        in_specs=[_row_spec(tm, D), _row_spec(tm, D), _row_spec(tm, PLE_DIM), _const_spec((PLE_DIM, D))],
        out_specs=[_row_spec(tm, D), _row_spec(tm, D)],
        out_shape=[jax.ShapeDtypeStruct((S, D), BF16)] * 2,
        compiler_params=_params("parallel"),
    )(dh, gpre, p, wpu)


def loss_head(h, g, tgt):
    S, D = h.shape
    tm = min(ROW_TILE, S)

    def body(h_ref, g_ref, t_ref, dh_ref, dhb_ref, l_ref, dg_ref):
        @pl.when(pl.program_id(0) == 0)
        def _():
            l_ref[...] = jnp.zeros_like(l_ref)
            dg_ref[...] = jnp.zeros_like(dg_ref)

        x = h_ref[...]
        gr = g_ref[...]
        rs = lax.rsqrt(jnp.mean(x * x, axis=1, keepdims=True) + EPS)
        xh = x * rs
        err = xh * gr - t_ref[...]
        l_ref[0:1, 0:1] += 0.5 * _colsum(jnp.mean(err * err, axis=1, keepdims=True))
        dy = err * (1.0 / D)
        dg_ref[0:1, :] += _colsum(dy * xh)
        dxh = dy * gr
        dh = rs * (dxh - xh * jnp.mean(dxh * xh, axis=1, keepdims=True))
        dh_ref[...] = dh
        dhb_ref[...] = dh.astype(BF16)

    return pl.pallas_call(
        body, name="loss_head", grid=(S // tm,),
        in_specs=[_row_spec(tm, D), _const_spec((1, D)), _row_spec(tm, D)],
        out_specs=[_row_spec(tm, D), _row_spec(tm, D), _const_spec((8, 128)), _const_spec((8, D))],
        out_shape=[jax.ShapeDtypeStruct((S, D), F32), jax.ShapeDtypeStruct((S, D), BF16),
                   jax.ShapeDtypeStruct((8, 128), F32), jax.ShapeDtypeStruct((8, D), F32)],
        compiler_params=_params("arbitrary"),
    )(h, g, tgt)


def _lb_rows(l_ref):
    l = l_ref[...]
    e = jnp.exp(l - jnp.max(l, axis=0, keepdims=True))
    p = e / _colsum(e)
    lbs, run = [], None
    for i in range(DEPTH):
        run = p[i:i + 1] if i == 0 else run + p[i:i + 1]
        lbs.append(run - p[0:1])
    return p, lbs


def lb_fwd(lb_logits):
    def body(l_ref, o_ref):
        _, lbs = _lb_rows(l_ref)
        o_ref[...] = jnp.zeros_like(o_ref)
        for i, lb in enumerate(lbs):
            o_ref[8 * i:8 * i + 1, :] = jnp.log(jnp.maximum(lb, LB_FLOOR))
            o_ref[8 * i + 1:8 * i + 2, :] = jnp.log1p(-lb)
            o_ref[8 * i + 2:8 * i + 3, :] = 1.0 - lb
            o_ref[8 * i + 3:8 * i + 4, :] = lb

    return pl.pallas_call(
        body, name="lb_fwd",
        out_shape=jax.ShapeDtypeStruct((DEPTH * 8, HG_WIDTH), F32),
    )(lb_logits)


def lb_bwd(dlbrows, lb_logits):
    def body(d_ref, l_ref, o_ref):
        p, lbs = _lb_rows(l_ref)
        dlb = []
        for i, lb in enumerate(lbs):
            da = d_ref[8 * i:8 * i + 1, :]
            dc = d_ref[8 * i + 1:8 * i + 2, :]
            do = d_ref[8 * i + 2:8 * i + 3, :]
            dlb.append(jnp.where(lb > LB_FLOOR, da / jnp.maximum(lb, LB_FLOOR), 0.0) - dc / (1.0 - lb) - do)
        dp = [jnp.zeros_like(dlb[0])]
        for j in range(1, DEPTH):
            acc = dlb[j]
            for i in range(j + 1, DEPTH):
                acc = acc + dlb[i]
            dp.append(acc)
        dot_ = p[0:1] * dp[0]
        for j in range(1, DEPTH):
            dot_ = dot_ + p[j:j + 1] * dp[j]
        o_ref[...] = jnp.zeros_like(o_ref)
        for j in range(DEPTH):
            o_ref[j:j + 1, :] = p[j:j + 1] * (dp[j] - dot_)

    return pl.pallas_call(
        body, name="lb_bwd",
        out_shape=jax.ShapeDtypeStruct((8, HG_WIDTH), F32),
    )(dlbrows, lb_logits)


def adamw(w, g, m, v, *, name):
    R, C = w.shape
    tr = next((t for t in (512, 256, 128, 64, 32, 16, 8) if R % t == 0), R)

    def body(w_ref, g_ref, m_ref, v_ref, d_ref, m2_ref, v2_ref):
        gv = g_ref[...]
        m2 = ADAM_B1 * m_ref[...] + (1.0 - ADAM_B1) * gv
        v2 = ADAM_B2 * v_ref[...] + (1.0 - ADAM_B2) * (gv * gv)
        mh = m2 / (1.0 - ADAM_B1 ** ADAM_STEP)
        vh = v2 / (1.0 - ADAM_B2 ** ADAM_STEP)
        d_ref[...] = -ADAM_LR * (mh / (jnp.sqrt(vh) + ADAM_EPS) + ADAM_WD * w_ref[...])
        m2_ref[...] = m2
        v2_ref[...] = v2

    spec = pl.BlockSpec((tr, C), lambda i: (i, 0))
    return pl.pallas_call(
        body, name=name, grid=(R // tr,),
        in_specs=[spec] * 4, out_specs=[spec] * 3,
        out_shape=[jax.ShapeDtypeStruct((R, C), F32)] * 3,
        compiler_params=_params("parallel"),
    )(w, g, m, v)


def sum_slots(x, *, out_dtype, name):
    n, R, C = x.shape
    tr = 848 if R % 848 == 0 else R

    def body(x_ref, o_ref):
        acc = x_ref[0].astype(F32)
        for k in range(1, n):
            acc = acc + x_ref[k].astype(F32)
        o_ref[...] = acc.astype(o_ref.dtype)

    return pl.pallas_call(
        body, name=name, grid=(R // tr,),
        in_specs=[pl.BlockSpec((n, tr, C), lambda i: (0, i, 0))],
        out_specs=pl.BlockSpec((tr, C), lambda i: (i, 0)),
        out_shape=jax.ShapeDtypeStruct((R, C), out_dtype),
        compiler_params=_params("parallel"),
    )(x)


MESH = pl.DeviceIdType.MESH
ANY = pl.BlockSpec(memory_space=pl.ANY)


def _place():
    return lax.axis_index("x"), lax.axis_index("y"), lax.axis_index("c")


def _other_chips(x, y):
    return [(1 - x, y), (x, 1 - y), (1 - x, 1 - y)]


def small_allgather(buf):
    R, C = buf.shape

    def body(x_ref, out_ref, send_sems, recv_sems, local_sem):
        x, y, c = _place()
        me, sibling = (x, y, c), (x, y, 1 - c)
        chips = _other_chips(x, y)

        def slot(px, py, pc):
            return out_ref.at[4 * px + 2 * py + pc]

        def copy(k, block, to, src=None):
            return pltpu.make_async_remote_copy(
                src_ref=slot(*block) if src is None else src, dst_ref=slot(*block),
                send_sem=send_sems.at[k], recv_sem=recv_sems.at[k],
                device_id=to, device_id_type=MESH)

        mine = pltpu.make_async_copy(x_ref, slot(*me), local_sem)
        mine.start()
        first = [copy(0, me, sibling, src=x_ref)]
        first += [copy(1 + r, me, (*chip, c), src=x_ref) for r, chip in enumerate(chips)]
        for cp in first:
            cp.start()
        passed = [copy(4 + r, (*chip, c), sibling) for r, chip in enumerate(chips)]
        for r, chip in enumerate(chips):
            copy(1 + r, (*chip, c), me).wait_recv()
            passed[r].start()
        copy(0, sibling, me).wait_recv()
        for r, chip in enumerate(chips):
            copy(4 + r, (*chip, 1 - c), me).wait_recv()
        for cp in first + passed:
            cp.wait_send()
        mine.wait()

    return pl.pallas_call(
        body, name="small_allgather",
        out_shape=jax.ShapeDtypeStruct((8, R, C), buf.dtype),
        in_specs=[pl.BlockSpec(memory_space=pltpu.VMEM)],
        out_specs=pl.BlockSpec(memory_space=pltpu.VMEM),
        scratch_shapes=[pltpu.SemaphoreType.DMA((7,)), pltpu.SemaphoreType.DMA((7,)),
                        pltpu.SemaphoreType.DMA],
    )(buf)


def weights_allgather(wp):
    def body(w_ref, g_ref, send_sems, recv_sems, local_sem):
        gather = _Gather(w_ref, g_ref, send_sems, recv_sems, local_sem)
        gather.start()
        gather.finish()

    return pl.pallas_call(
        body, name="weights_allgather",
        out_shape=jax.ShapeDtypeStruct((4,) + wp.shape, wp.dtype),
        in_specs=[ANY], out_specs=ANY,
        scratch_shapes=_Gather.SCRATCH,
    )(wp)


class _Gather:
    SCRATCH = [pltpu.SemaphoreType.DMA((6,)), pltpu.SemaphoreType.DMA((6,)), pltpu.SemaphoreType.DMA]

    def __init__(self, w_ref, g_ref, send_sems, recv_sems, local_sem):
        self.w_ref, self.g_ref, self.local_sem = w_ref, g_ref, local_sem
        self.send_sems, self.recv_sems = send_sems, recv_sems
        self.x, self.y, self.c = _place()
        self.chips = _other_chips(self.x, self.y)
        half = w_ref.shape[0] // 2
        self.mine = pl.ds(pl.multiple_of(self.c * half, 16), half)
        self.theirs = pl.ds(pl.multiple_of((1 - self.c) * half, 16), half)

    def _copy(self, k, chip_block, rows, to, src=None):
        dst = self.g_ref.at[chip_block, rows]
        return pltpu.make_async_remote_copy(
            src_ref=dst if src is None else src, dst_ref=dst,
            send_sem=self.send_sems.at[k], recv_sem=self.recv_sems.at[k],
            device_id=to, device_id_type=MESH)

    def _own(self):
        return pltpu.make_async_copy(self.w_ref, self.g_ref.at[2 * self.x + self.y], self.local_sem)

    def _first(self):
        return [self._copy(r, 2 * self.x + self.y, self.mine, (*chip, self.c), src=self.w_ref.at[self.mine])
                for r, chip in enumerate(self.chips)]

    def start(self):
        self._own().start()
        for cp in self._first():
            cp.start()

    def finish(self):
        sibling = (self.x, self.y, 1 - self.c)
        passed = [self._copy(3 + r, 2 * chip[0] + chip[1], self.mine, sibling) for r, chip in enumerate(self.chips)]
        for r, chip in enumerate(self.chips):
            self._copy(r, 2 * chip[0] + chip[1], self.mine, (*chip, self.c)).wait_recv()
            passed[r].start()
        for r, chip in enumerate(self.chips):
            self._copy(3 + r, 2 * chip[0] + chip[1], self.theirs, sibling).wait_recv()
        for cp in self._first() + passed:
            cp.wait_send()
        self._own().wait()


def sibling_swap(v, *, name):
    def body(v_ref, got_ref, send_sem, recv_sem):
        x, y, c = _place()
        cp = pltpu.make_async_remote_copy(
            src_ref=v_ref, dst_ref=got_ref, send_sem=send_sem, recv_sem=recv_sem,
            device_id=(x, y, 1 - c), device_id_type=MESH)
        cp.start()
        cp.wait()

    return pl.pallas_call(
        body, name=name,
        out_shape=jax.ShapeDtypeStruct(v.shape, v.dtype),
        in_specs=[ANY], out_specs=ANY,
        scratch_shapes=[pltpu.SemaphoreType.DMA, pltpu.SemaphoreType.DMA],
    )(v)


def chip_exchange(q):
    def body(q_ref, r_ref, send_sems, recv_sems, local_sem):
        exchange = _Exchange(q_ref, r_ref, send_sems, recv_sems, local_sem)
        exchange.start()
        exchange.finish()

    return pl.pallas_call(
        body, name="chip_exchange",
        out_shape=jax.ShapeDtypeStruct(q.shape, q.dtype),
        in_specs=[ANY], out_specs=ANY,
        scratch_shapes=_Exchange.SCRATCH,
    )(q)


class _Exchange:
    SCRATCH = [pltpu.SemaphoreType.DMA((3,)), pltpu.SemaphoreType.DMA((3,)), pltpu.SemaphoreType.DMA]

    def __init__(self, q_ref, r_ref, send_sems, recv_sems, local_sem):
        self.q_ref, self.r_ref, self.local_sem = q_ref, r_ref, local_sem
        self.send_sems, self.recv_sems = send_sems, recv_sems
        self.x, self.y, self.c = _place()
        self.j = 2 * self.x + self.y
        self.chips = _other_chips(self.x, self.y)

    def _copy(self, r, src_block, dst_block, chip):
        return pltpu.make_async_remote_copy(
            src_ref=self.q_ref.at[src_block], dst_ref=self.r_ref.at[dst_block],
            send_sem=self.send_sems.at[r], recv_sem=self.recv_sems.at[r],
            device_id=(*chip, self.c), device_id_type=MESH)

    def _own(self):
        return pltpu.make_async_copy(self.q_ref.at[self.j], self.r_ref.at[self.j], self.local_sem)

    def _sends(self):
        return [self._copy(r, 2 * chip[0] + chip[1], self.j, chip) for r, chip in enumerate(self.chips)]

    def start(self):
        self._own().start()
        for cp in self._sends():
            cp.start()

    def finish(self):
        for r, chip in enumerate(self.chips):
            jr = 2 * chip[0] + chip[1]
            self._copy(r, jr, jr, chip).wait_recv()
        for cp in self._sends():
            cp.wait_send()
        self._own().wait()


N_CHIPS = 4
_PACK = (("w_in", 704), ("w_out", 256), ("w_up", 1408), ("w_down", 704), ("w_ple_gate", 256), ("w_ple_up", 64))
LAYER_ROWS = sum(r for _, r in _PACK)
PACK_ROWS = DEPTH * LAYER_ROWS
HALF_ROWS = PACK_ROWS // 2


def _pack_shards(sh):
    parts = []
    for i in range(DEPTH):
        for name, rows in _PACK:
            parts.append(sh[name][i].reshape(rows, D_MODEL))
    return jnp.concatenate(parts, axis=0)


def _unpack_shards(slab):
    shapes = {"w_in": (D_MODEL, IN_WIDTH // N_CHIPS), "w_out": (D_MODEL // N_CHIPS, D_MODEL),
              "w_up": (D_MODEL, 2 * D_FF // N_CHIPS), "w_down": (D_FF // N_CHIPS, D_MODEL),
              "w_ple_gate": (D_MODEL // N_CHIPS, D_MODEL), "w_ple_up": (PLE_DIM, D_MODEL // N_CHIPS)}
    out = {name: [] for name, _ in _PACK}
    off = 0
    for i in range(DEPTH):
        for name, rows in _PACK:
            out[name].append(slab[off:off + rows].reshape(shapes[name]))
            off += rows
    return {k: jnp.stack(v) for k, v in out.items()}


_COL_SHARDED = ("w_in", "w_up", "w_ple_up")


def _full_from_chips(g, layer):
    per_chip = [_unpack_shards_layer(g[k], layer) for k in range(N_CHIPS)]
    return {name: jnp.concatenate([pc[name] for pc in per_chip], axis=1 if name in _COL_SHARDED else 0)
            for name, _ in _PACK}


def _unpack_shards_layer(slab, layer):
    shapes = {"w_in": (D_MODEL, IN_WIDTH // N_CHIPS), "w_out": (D_MODEL // N_CHIPS, D_MODEL),
              "w_up": (D_MODEL, 2 * D_FF // N_CHIPS), "w_down": (D_FF // N_CHIPS, D_MODEL),
              "w_ple_gate": (D_MODEL // N_CHIPS, D_MODEL), "w_ple_up": (PLE_DIM, D_MODEL // N_CHIPS)}
    out = {}
    off = layer * LAYER_ROWS
    for name, rows in _PACK:
        out[name] = slab[off:off + rows].reshape(shapes[name])
        off += rows
    return out


def _split_to_chips(full, name):
    r, c = full.shape
    if name in _COL_SHARDED:
        full = full.reshape(r, N_CHIPS, c // N_CHIPS).transpose(1, 0, 2)
    return full.reshape(N_CHIPS, -1, D_MODEL)


_SMALL = (("loss", 128), ("g_final", 1024), ("g_mix", 4096), ("lb_logits", 2048), ("hg_norm_g", 2048),
          ("attn_sinks", 128), ("g_ffn", 4096), ("conv_w", 4 * 3 * D_FF), ("conv_b", 4 * D_FF), ("g_ple", 4096))
SMALL_ROWS = 496


def _pack_small(d):
    parts = []
    for name, n in _SMALL:
        v = d[name].reshape(-1).astype(F32)
        parts.append(jnp.pad(v, (0, n - v.shape[0])))
    flat = jnp.concatenate(parts)
    return jnp.pad(flat, (0, SMALL_ROWS * 128 - flat.shape[0])).reshape(SMALL_ROWS, 128)


def _unpack_small(buf, shapes):
    flat = buf.reshape(-1)
    out, off = {}, 0
    for name, n in _SMALL:
        size = 1
        for s in shapes[name]:
            size *= s
        out[name] = flat[off:off + size].reshape(shapes[name])
        off += n
    return out


WEIGHT_ORDER = ('g_mix', 'w_in', 'lb_logits', 'hg_norm_g', 'attn_sinks', 'w_out', 'g_ffn', 'w_up', 'conv_w',
                'conv_b', 'w_down', 'g_ple', 'w_ple_gate', 'w_ple_up', 'g_final')


def kernel(x, p, g_mix, w_in, lb_logits, hg_norm_g, attn_sinks, w_out, g_ffn, w_up, conv_w, conv_b, w_down, g_ple, w_ple_gate, w_ple_up, g_final, loss_target, m_g_mix, m_w_in, m_lb_logits, m_hg_norm_g, m_attn_sinks, m_w_out, m_g_ffn, m_w_up, m_conv_w, m_conv_b, m_w_down, m_g_ple, m_w_ple_gate, m_w_ple_up, m_g_final, v_g_mix, v_w_in, v_lb_logits, v_hg_norm_g, v_attn_sinks, v_w_out, v_g_ffn, v_w_up, v_conv_w, v_conv_b, v_w_down, v_g_ple, v_w_ple_gate, v_w_ple_up, v_g_final):
    W = dict(g_mix=g_mix, w_in=w_in, lb_logits=lb_logits, hg_norm_g=hg_norm_g, attn_sinks=attn_sinks,
             w_out=w_out, g_ffn=g_ffn, w_up=w_up, conv_w=conv_w, conv_b=conv_b, w_down=w_down, g_ple=g_ple,
             w_ple_gate=w_ple_gate, w_ple_up=w_ple_up, g_final=g_final)
    M = dict(g_mix=m_g_mix, w_in=m_w_in, lb_logits=m_lb_logits, hg_norm_g=m_hg_norm_g, attn_sinks=m_attn_sinks,
             w_out=m_w_out, g_ffn=m_g_ffn, w_up=m_w_up, conv_w=m_conv_w, conv_b=m_conv_b, w_down=m_w_down,
             g_ple=m_g_ple, w_ple_gate=m_w_ple_gate, w_ple_up=m_w_ple_up, g_final=m_g_final)
    V = dict(g_mix=v_g_mix, w_in=v_w_in, lb_logits=v_lb_logits, hg_norm_g=v_hg_norm_g, attn_sinks=v_attn_sinks,
             w_out=v_w_out, g_ffn=v_g_ffn, w_up=v_w_up, conv_w=v_conv_w, conv_b=v_conv_b, w_down=v_w_down,
             g_ple=v_g_ple, w_ple_gate=v_w_ple_gate, w_ple_up=v_w_ple_up, g_final=v_g_final)
    S = x.shape[1]
    hg_rows = min(ROW_TILE, S)
    xi, yi, ci = _place()
    chip = 2 * xi + yi

    slab = _pack_shards({n: W[n] for n, _ in _PACK}).astype(BF16).reshape(DEPTH, LAYER_ROWS, D_MODEL)
    gathered = weights_allgather(slab[0])
    cw_shard = jnp.pad(conv_w.reshape(-1), (0, 72 * 128 - conv_w.size)).reshape(72, 128)
    cw_all = small_allgather(cw_shard)
    cw_full = jnp.concatenate(
        [cw_all[2 * k].reshape(-1)[:conv_w.size].reshape(conv_w.shape) for k in range(N_CHIPS)], axis=2)
    lbrows = lb_fwd(lb_logits)
    at_mask = swa_mask()

    h = x[0]
    saved = []
    for i in range(DEPTH):
        wf = _full_from_chips(gathered, 0)
        lbr = lbrows[8 * i:8 * i + 8]
        ng = hg_norm_g[i][None]
        sinks_b = jnp.pad(jnp.repeat(attn_sinks[i].reshape(AT_KV_HEADS, 1, AT_GROUP), WINDOW, axis=2),
                          ((0, 0), (0, 7), (0, 0))).reshape(8 * AT_KV_HEADS, GROUP_LANES)
        cw8 = jnp.pad(cw_full[i], ((0, 5), (0, 0)))
        cb = conv_b[i][None]
        if i == 0:
            u = rmsnorm_fwd(h, g_mix[0][None], name="rmsnorm_fwd")
        proj = mm(u, wf["w_in"], name="mm_in")
        if i + 1 < DEPTH:
            y, o_raw, states, gathered = hgrn_fwd(proj, lbr, ng, D_MODEL, rows=hg_rows, gather=slab[i + 1])
        else:
            y, o_raw, states = hgrn_fwd(proj, lbr, ng, D_MODEL, rows=hg_rows)
        y = swa_fwd(proj, sinks_b, at_mask, y)
        h1, u2 = mm(y, wf["w_out"], res=h, norm_g=g_ffn[i][None], name="mm_out")
        hh = mm(u2, wf["w_up"], out_dtype=BF16, name="mm_up")
        act = convffn_fwd(hh, cw8, cb)
        h2, u3 = mm(act, wf["w_down"], res=h1, norm_g=g_ple[i][None], name="mm_down")
        gpre = mm(u3, wf["w_ple_gate"], name="mm_gate")
        next_g = g_mix[i + 1] if i + 1 < DEPTH else g_final
        h3, u_next = ple_fwd(h2, gpre, p[i, 0], wf["w_ple_up"], next_g[None])
        saved.append(dict(wf=wf, lbr=lbr, ng=ng, sinks_b=sinks_b, cw8=cw8, cb=cb, h=h, u=u, proj=proj,
                          o_raw=o_raw, states=states, y=y, h1=h1, u2=u2, hh=hh, act=act, h2=h2, u3=u3,
                          gpre=gpre))
        h, u = h3, u_next

    dh, dhb, loss_acc, dg_final = loss_head(h, g_final[None], loss_target[0])

    gfull = {n: [None] * DEPTH for n, _ in _PACK}
    gsmall = {n: [None] * DEPTH for n in ("g_mix", "hg_norm_g", "attn_sinks", "g_ffn", "conv_w", "conv_b", "g_ple")}
    dlbrows = [None] * DEPTH
    half_rows = LAYER_ROWS // 2
    from_chips = [None] * DEPTH
    pending = None
    for i in reversed(range(DEPTH)):
        s = saved[i]
        wf = s["wf"]
        dpu, dgp = ple_bwd(dh, s["gpre"], p[i, 0], wf["w_ple_up"])
        gfull["w_ple_up"][i] = mm_tn(p[i, 0], dpu, name="mm_tn_pu")
        gfull["w_ple_gate"][i] = mm_tn(s["u3"], dgp, name="mm_tn_gate")
        dh2, dh2b, dg = mm(dgp, wf["w_ple_gate"], nt=True, rms_bwd=(s["h2"], g_ple[i][None], dh),
                           name="mm_nt_gate")
        gsmall["g_ple"][i] = dg[0]
        gfull["w_down"][i] = mm_tn(s["act"], dh2b, name="mm_tn_down")
        dact = mm(dh2b, wf["w_down"], nt=True, out_dtype=BF16, name="mm_nt_down")
        da, db, dcw = convffn_bwd(s["hh"], dact, s["cw8"], s["cb"])
        gsmall["conv_w"][i] = dcw[0:3]
        gsmall["conv_b"][i] = dcw[3]
        gfull["w_up"][i] = jnp.concatenate([mm_tn(s["u2"], da, name="mm_tn_up"),
                                            mm_tn(s["u2"], db, name="mm_tn_up")], axis=1)
        dh1, dh1b, dg = mm((da, db), wf["w_up"], nt=True, rms_bwd=(s["h1"], g_ffn[i][None], dh2),
                           name="mm_nt_up")
        gsmall["g_ffn"][i] = dg[0]
        gfull["w_out"][i] = mm_tn(s["y"], dh1b, name="mm_tn_out")
        dy = mm(dh1b, wf["w_out"], nt=True, name="mm_nt_out")
        dq_at, dko, dkp, dvo, dvp, dsk = swa_bwd(s["proj"], s["sinks_b"], at_mask, dy)
        gsmall["attn_sinks"][i] = dsk.reshape(AT_KV_HEADS, 8, GROUP_LANES)[:, 1:1 + AT_GROUP, 0].reshape(-1)
        hg_args = (s["proj"], s["o_raw"], s["states"], dy, s["lbr"], s["ng"])
        if pending is None:
            hq, hz, hv, hgp, dlbr, dng = hgrn_bwd(*hg_args, rows=hg_rows)
        else:
            hq, hz, hv, hgp, dlbr, dng, got = hgrn_bwd(*hg_args, rows=hg_rows, exchange=pending[1])
            from_chips[pending[0]] = got
        dlbrows[i] = dlbr
        gsmall["hg_norm_g"][i] = dng[0]
        dproj = assemble_dproj((hq, hz, hv, hgp), dq_at, dko, dkp, dvo, dvp, rows=hg_rows)
        gfull["w_in"][i] = mm_tn(s["u"], dproj, name="mm_tn_in")
        dh, dhb, dg = mm(dproj, wf["w_in"], nt=True, rms_bwd=(s["h"], g_mix[i][None], dh1), name="mm_nt_in")
        gsmall["g_mix"][i] = dg[0]
        pk = jnp.concatenate([_split_to_chips(gfull[name][i], name) for name, _ in _PACK], axis=1).astype(BF16)
        pk = pk.reshape(N_CHIPS, 2, half_rows, D_MODEL)
        p_mine = lax.dynamic_index_in_dim(pk, ci, axis=1, keepdims=False)
        p_other = lax.dynamic_index_in_dim(pk, 1 - ci, axis=1, keepdims=False)
        from_sib = sibling_swap(p_other, name="sibling_swap_partials")
        pair = sum_slots(jnp.stack([p_mine.reshape(-1, D_MODEL), from_sib.reshape(-1, D_MODEL)]),
                         out_dtype=BF16, name="sum_pair")
        pending = (i, pair.reshape(N_CHIPS, half_rows, D_MODEL))
    from_chips[0] = chip_exchange(pending[1])
    grad_x = dh[None]
    dlb_logits = lb_bwd(jnp.concatenate(dlbrows, axis=0), lb_logits)[0:DEPTH]

    mine_sum = jnp.concatenate([sum_slots(from_chips[i], out_dtype=F32, name="sum_chips") for i in range(DEPTH)],
                               axis=0)
    sib_sum = sibling_swap(mine_sum, name="sibling_swap_sums")
    mine_sum = mine_sum.reshape(DEPTH, half_rows, D_MODEL)
    sib_sum = sib_sum.reshape(DEPTH, half_rows, D_MODEL)
    lo = jnp.where(ci == 0, mine_sum, sib_sum)
    hi = jnp.where(ci == 0, sib_sum, mine_sum)
    gshard = _unpack_shards(jnp.concatenate([lo, hi], axis=1).reshape(PACK_ROWS, D_MODEL))

    small = dict(loss=loss_acc[0, 0:1], g_final=dg_final[0], lb_logits=dlb_logits,
                 **{n: jnp.stack(v) for n, v in gsmall.items()})
    small_sum = sum_slots(small_allgather(_pack_small(small)), out_dtype=F32, name="sum_small")
    shapes = {n: W[n].shape for n in W}
    shapes["loss"] = (1,)
    shapes["conv_w"] = (DEPTH, 3, D_FF)
    gs = _unpack_small(small_sum, shapes)
    loss = gs["loss"][0]
    cshard = conv_w.shape[2]
    grads = dict(gshard)
    for n in ("g_mix", "lb_logits", "hg_norm_g", "attn_sinks", "g_ffn", "conv_b", "g_ple", "g_final"):
        grads[n] = gs[n]
    grads["conv_w"] = lax.dynamic_slice_in_dim(gs["conv_w"], chip * cshard, cshard, axis=2)

    delta, new_m, new_v = {}, {}, {}
    small_names = ("g_final", "g_mix", "lb_logits", "hg_norm_g", "attn_sinks", "g_ffn", "conv_b", "g_ple")
    sshapes = {n: W[n].shape for n in small_names}

    def pack_s(d):
        z = dict(d)
        z["loss"] = jnp.zeros((1,), F32)
        z["conv_w"] = jnp.zeros((1,), F32)
        return _pack_small(z)

    sd, sm, sv = adamw(pack_s(W), pack_s(grads), pack_s(M), pack_s(V), name="adamw_small")
    for out, buf in ((delta, sd), (new_m, sm), (new_v, sv)):
        un = _unpack_small(buf, {**sshapes, "loss": (1,), "conv_w": (1,)})
        for n in small_names:
            out[n] = un[n]
    for n in ("w_in", "w_out", "w_up", "w_down", "w_ple_gate", "w_ple_up", "conv_w"):
        shp = W[n].shape
        two_d = (-1, shp[-1])
        d_, m_, v_ = adamw(W[n].reshape(two_d), grads[n].reshape(two_d), M[n].reshape(two_d),
                           V[n].reshape(two_d), name="adamw_" + n)
        delta[n], new_m[n], new_v[n] = d_.reshape(shp), m_.reshape(shp), v_.reshape(shp)

    return (loss, grad_x, *[grads[n] for n in WEIGHT_ORDER], *[delta[n] for n in WEIGHT_ORDER],
            *[new_m[n] for n in WEIGHT_ORDER], *[new_v[n] for n in WEIGHT_ORDER])
```

```python
import functools

import jax
import jax.numpy as jnp
from jax import lax
from jax.experimental import pallas as pl
from jax.experimental.pallas import tpu as pltpu

F32 = jnp.float32
BF16 = jnp.bfloat16

D_MODEL = 1024
DEPTH = 4
PLE_DIM = 256
HG_WIDTH = 512
HG_HEADS = 4
HG_DK = 128
HG_CHUNK = 64
HG_SUB = 16
AT_WIDTH = 512
AT_HEAD_DIM = 64
AT_Q_HEADS = 8
AT_KV_HEADS = 2
AT_GROUP = 4
WINDOW = 128
D_FF = 2816
IN_WIDTH = 2816
EPS = 1e-6
MASK_VALUE = -1e30
LB_FLOOR = 1e-30

ADAM_LR = 0.001
ADAM_B1 = 0.9
ADAM_B2 = 0.999
ADAM_EPS = 1e-08
ADAM_WD = 0.01
ADAM_STEP = 10

VMEM_LIMIT = 48 * 1024 * 1024


def _params(*sem):
    return pltpu.CompilerParams(dimension_semantics=sem, vmem_limit_bytes=VMEM_LIMIT)


def _dot(a, b, dims=(((1,), (0,)), ((), ()))):
    return lax.dot_general(a.astype(BF16), b.astype(BF16), dims, preferred_element_type=F32)


def _dot_nt(a, b):
    return _dot(a, b, (((1,), (1,)), ((), ())))


def _dot_tn(a, b):
    return _dot(a, b, (((0,), (0,)), ((), ())))


def _dot_exact(sel, x, dims=(((1,), (0,)), ((), ()))):
    hi = x.astype(BF16)
    r1 = x - hi.astype(F32)
    mid = r1.astype(BF16)
    lo = (r1 - mid.astype(F32)).astype(BF16)
    s = sel.astype(BF16)
    one = lambda p: lax.dot_general(s, p, dims, preferred_element_type=F32)
    return one(hi) + one(mid) + one(lo)


def _sigmoid(x):
    return 0.5 * jnp.tanh(0.5 * x) + 0.5


def _logsig(x):
    return jnp.minimum(x, 0.0) - jnp.log(1.0 + jnp.exp(-jnp.abs(x)))


def _colsum(x):
    return jnp.sum(x, axis=0, keepdims=True)


def _rowsum(x):
    return jnp.sum(x, axis=1, keepdims=True)


def _colsum8(xs):
    row = lax.broadcasted_iota(jnp.int32, xs[0].shape, 0)

    def merge(a, b, keep_a, step):
        return jnp.where(keep_a, a + pltpu.roll(a, 8 - step, 0), b + pltpu.roll(b, step, 0))

    c = [merge(xs[j], xs[j + 4], row < 4, 4) for j in range(4)]
    d = [merge(c[j], c[j + 2], (row & 3) < 2, 2) for j in range(2)]
    return merge(d[0], d[1], (row & 1) == 0, 1)


def _tri(n):
    r = lax.broadcasted_iota(jnp.int32, (n, n), 0)
    c = lax.broadcasted_iota(jnp.int32, (n, n), 1)
    return (r >= c).astype(F32)


def _hg_gates(qp, z, a, c, oml):
    sq = _sigmoid(qp)
    q = qp * sq
    t = c + _logsig(z)
    mx = jnp.maximum(a, t)
    logf = mx + jnp.log(1.0 + jnp.exp(-jnp.abs(a - t)))
    snz = _sigmoid(-z)
    k = oml * snz
    return q, sq, t, logf, snz, k


_HEADS = range(HG_HEADS)


def _lanes(h):
    return slice(h * HG_DK, (h + 1) * HG_DK)


def _head(x, h):
    return x[:, _lanes(h)]


def _row_masks():
    row8 = lax.broadcasted_iota(jnp.int32, (8, HG_DK), 0)
    return [None] + [jnp.where(row8 >= j, 0.0, MASK_VALUE) for j in range(1, 8)]


def _hg_chunk_fwd(q, k, v, logf, st, b_s, k_s, v_s):
    C, U = HG_CHUNK, HG_SUB
    tri = _tri(C)
    b = [_dot_exact(tri, logf[h]) for h in _HEADS]
    for h in _HEADS:
        b_s[h] = b[h]
        k_s[h] = k[h]
        v_s[h] = v[h]
    o = [_dot_nt(q[h] * jnp.exp(b[h]), st[h]) for h in _HEADS]
    bl = [b[h][C - 1:C] for h in _HEADS]
    upd = [_dot_tn(v[h], k[h] * jnp.exp(bl[h] - b[h])) for h in _HEADS]
    rows = lax.broadcasted_iota(jnp.int32, (C, HG_DK), 0)
    nmask = _row_masks()
    outs = [[] for _ in _HEADS]
    for i in range(C // U):
        lo = i * U
        b_i = [b[h][lo:lo + U] for h in _HEADS]
        q_i = [q[h][lo:lo + U] for h in _HEADS]
        o_i = [o[h][lo:lo + U] for h in _HEADS]
        if i > 0:
            qe = [q_i[h] * jnp.exp(b_i[h] - b_i[h][0:1]) for h in _HEADS]
            ke = [jnp.where(rows < lo, k[h] * jnp.exp(jnp.minimum(b_i[h][0:1] - b[h], 0.0)), 0.0) for h in _HEADS]
            att = [_dot_nt(qe[h], ke[h]) for h in _HEADS]
            off = [_dot(att[h], v[h]) for h in _HEADS]
            o_i = [o_i[h] + off[h] for h in _HEADS]
        pieces = [[o_i[h][8 * f:8 * f + 8] for f in range(U // 8)] for h in _HEADS]
        for s in range(U):
            for f in range(s // 8, U // 8):
                for h in _HEADS:
                    bs = b_s[h, lo + s:lo + s + 1, :]
                    ks = k_s[h, lo + s:lo + s + 1, :]
                    vs = v_s[h, lo + s:lo + s + 1, :]
                    arg = b_i[h][8 * f:8 * f + 8] - bs
                    if s > 8 * f:
                        arg = arg + nmask[s - 8 * f]
                    w = _rowsum(q_i[h][8 * f:8 * f + 8] * jnp.exp(arg) * ks)
                    pieces[h][f] = pieces[h][f] + w * vs
        for h in _HEADS:
            outs[h] += pieces[h]
    o = [jnp.concatenate(outs[h], axis=0) for h in _HEADS]
    st_new = [st[h] * jnp.exp(bl[h]) + upd[h] for h in _HEADS]
    return o, st_new, b


def _hg_post(o, gp, ng):
    rs = lax.rsqrt(jnp.mean(o * o, axis=1, keepdims=True) + EPS)
    sg = _sigmoid(gp)
    return o * rs * ng * sg, rs, sg


def hgrn_fwd(proj, lbrows, ng, y_width, *, rows, gather=None):
    S = proj.shape[0]
    C = HG_CHUNK
    cpb = rows // C
    nb = S // rows

    def body(qp_ref, z_ref, v_ref, gp_ref, lb_ref, ng_ref, *rest):
        if gather is not None:
            w_ref, y_ref, o_ref, st_ref, g_ref, st, b_s, k_s, v_s, *sems = rest
            comm = _Gather(w_ref, g_ref, *sems)
        else:
            y_ref, o_ref, st_ref, st, b_s, k_s, v_s = rest

        @pl.when(pl.program_id(0) == 0)
        def _():
            st[...] = jnp.zeros_like(st)
            if gather is not None:
                comm.start()

        a, c, oml = lb_ref[0:1, :], lb_ref[1:2, :], lb_ref[2:3, :]
        ngr = ng_ref[...]

        def chunk(ci, carry):
            off = pl.multiple_of(ci * C, C)
            sl = pl.ds(off, C)
            for h in _HEADS:
                st_ref[h, ci] = st[h]
            gates = [_hg_gates(qp_ref[sl, _lanes(h)], z_ref[sl, _lanes(h)],
                               _head(a, h), _head(c, h), _head(oml, h)) for h in _HEADS]
            q = [g[0] for g in gates]
            logf = [g[3] for g in gates]
            k = [g[5] for g in gates]
            v = [v_ref[sl, _lanes(h)] for h in _HEADS]
            o, st_new, _ = _hg_chunk_fwd(q, k, v, logf, [st[h] for h in _HEADS], b_s, k_s, v_s)
            for h in _HEADS:
                y, _, _ = _hg_post(o[h], gp_ref[sl, _lanes(h)], _head(ngr, h))
                y_ref[sl, _lanes(h)] = y.astype(y_ref.dtype)
                o_ref[sl, _lanes(h)] = o[h]
                st[h] = st_new[h]
            return carry

        lax.fori_loop(0, cpb, chunk, 0)

        if gather is not None:
            pl.when(pl.program_id(0) == nb - 1)(comm.finish)

    col = lambda kblk: pl.BlockSpec((rows, HG_WIDTH), lambda r: (r, kblk))
    in_specs = [col(0), col(1), col(2), col(3), _const_spec((8, HG_WIDTH)), _const_spec((1, HG_WIDTH))]
    out_specs = [col(0), col(0), pl.BlockSpec((HG_HEADS, cpb, HG_DK, HG_DK), lambda r: (0, r, 0, 0))]
    out_shape = [jax.ShapeDtypeStruct((S, y_width), BF16),
                 jax.ShapeDtypeStruct((S, HG_WIDTH), F32),
                 jax.ShapeDtypeStruct((HG_HEADS, S // C, HG_DK, HG_DK), F32)]
    scratch = [pltpu.VMEM((HG_HEADS, HG_DK, HG_DK), F32)] + [pltpu.VMEM((HG_HEADS, C, HG_DK), F32)] * 3
    args = [proj, proj, proj, proj, lbrows, ng]
    if gather is not None:
        in_specs.append(ANY)
        out_specs.append(ANY)
        out_shape.append(jax.ShapeDtypeStruct((N_CHIPS,) + gather.shape, gather.dtype))
        scratch += _Gather.SCRATCH
        args.append(gather)
    return pl.pallas_call(
        body,
        name="hgrn_fwd" if gather is None else "hgrn_fwd_gather",
        grid=(nb,),
        in_specs=in_specs,
        out_specs=out_specs,
        out_shape=out_shape,
        scratch_shapes=scratch,
        compiler_params=_params("arbitrary"),
    )(*args)


def hgrn_bwd(proj, o_raw, states, dy, lbrows, ng, *, rows, exchange=None):
    S = proj.shape[0]
    C, U = HG_CHUNK, HG_SUB
    cpb = rows // C
    nb = S // rows

    def body(qp_ref, z_ref, v_ref, gp_ref, o_ref, st_ref, dy_ref, lb_ref, ng_ref, *rest):
        if exchange is not None:
            (q_ref, dqp_ref, dz_ref, dv_ref, dgp_ref, dlb_ref, dng_ref, r_ref,
             dst, b_s, k_s, v_s, dbs, dks, dvs, *sems) = rest
            comm = _Exchange(q_ref, r_ref, *sems)
        else:
            dqp_ref, dz_ref, dv_ref, dgp_ref, dlb_ref, dng_ref, dst, b_s, k_s, v_s, dbs, dks, dvs = rest

        @pl.when(pl.program_id(0) == 0)
        def _():
            dst[...] = jnp.zeros_like(dst)
            dlb_ref[...] = jnp.zeros_like(dlb_ref)
            dng_ref[...] = jnp.zeros_like(dng_ref)
            if exchange is not None:
                comm.start()

        a, c, oml = lb_ref[0:1, :], lb_ref[1:2, :], lb_ref[2:3, :]
        ngr = ng_ref[...]
        rows_i = lax.broadcasted_iota(jnp.int32, (C, HG_DK), 0)
        nmask = _row_masks()
        tri = _tri(C)
        H = _HEADS

        def chunk(cj, carry):
            ci = cpb - 1 - cj
            off = pl.multiple_of(ci * C, C)
            sl = pl.ds(off, C)
            qp = [qp_ref[sl, _lanes(h)] for h in H]
            v = [v_ref[sl, _lanes(h)] for h in H]
            st = [st_ref[h, ci] for h in H]
            gates = [_hg_gates(qp[h], z_ref[sl, _lanes(h)], _head(a, h), _head(c, h), _head(oml, h)) for h in H]
            q, sq, t, logf, snz, k = ([g[j] for g in gates] for j in range(6))
            b = [_dot_exact(tri, logf[h]) for h in H]
            for h in H:
                b_s[h] = b[h]
                k_s[h] = k[h]
                v_s[h] = v[h]
            do = []
            for h in H:
                o = o_ref[sl, _lanes(h)]
                dyv = dy_ref[sl, _lanes(h)]
                ngh = _head(ngr, h)
                rs = lax.rsqrt(jnp.mean(o * o, axis=1, keepdims=True) + EPS)
                sg = _sigmoid(gp_ref[sl, _lanes(h)])
                xh = o * rs
                dgp_ref[sl, _lanes(h)] = (dyv * xh * ngh * sg * (1.0 - sg)).astype(dgp_ref.dtype)
                don = dyv * sg
                dng_ref[0:1, _lanes(h)] += _colsum(don * xh)
                dxh = don * ngh
                do.append(rs * (dxh - xh * jnp.mean(dxh * xh, axis=1, keepdims=True)))
            eb = [jnp.exp(b[h]) for h in H]
            qb = [q[h] * eb[h] for h in H]
            dstv = [dst[h] for h in H]
            bl = [b[h][C - 1:C] for h in H]
            el = [jnp.exp(bl[h]) for h in H]
            ex = [jnp.exp(bl[h] - b[h]) for h in H]
            kd = [k[h] * ex[h] for h in H]
            dqb = [_dot(do[h], st[h]) for h in H]
            dst_acc = [_dot_tn(do[h], qb[h]) for h in H]
            dv0 = [_dot_nt(kd[h], dstv[h]) for h in H]
            dkd = [_dot(v[h], dstv[h]) for h in H]
            dq = [dqb[h] * eb[h] for h in H]
            for h in H:
                g2 = dkd[h] * kd[h]
                dbl = _colsum(dstv[h] * st[h]) * el[h] + _colsum(g2)
                dst[h] = dstv[h] * el[h] + dst_acc[h]
                dbs[h] = dqb[h] * qb[h] - g2
                dks[h] = dkd[h] * ex[h]
                dvs[h] = dv0[h]
                dbs[h, C - 1:C, :] += dbl
            dq_parts = [[] for _ in H]
            for i in range(C // U):
                lo = i * U
                b_i = [b[h][lo:lo + U] for h in H]
                q_i = [q[h][lo:lo + U] for h in H]
                do_i = [do[h][lo:lo + U] for h in H]
                dq_i = [dq[h][lo:lo + U] for h in H]
                db_i = [jnp.zeros((U, HG_DK), F32) for _ in H]
                if i > 0:
                    e1 = [jnp.exp(b_i[h] - b_i[h][0:1]) for h in H]
                    qe = [q_i[h] * e1[h] for h in H]
                    e2 = [jnp.where(rows_i < lo, jnp.exp(jnp.minimum(b_i[h][0:1] - b[h], 0.0)), 0.0) for h in H]
                    ke = [k[h] * e2[h] for h in H]
                    att = [_dot_nt(qe[h], ke[h]) for h in H]
                    datt = [_dot_nt(do_i[h], v[h]) for h in H]
                    dv_add = [_dot_tn(att[h], do_i[h]) for h in H]
                    dqe = [_dot(datt[h], ke[h]) for h in H]
                    dke = [_dot_tn(datt[h], qe[h]) for h in H]
                    for h in H:
                        dvs[h] += dv_add[h]
                        dq_i[h] = dq_i[h] + dqe[h] * e1[h]
                        g = dqe[h] * qe[h]
                        db_i[h] = db_i[h] + g
                        gk = dke[h] * ke[h]
                        dks[h] += dke[h] * e2[h]
                        dbs[h] -= gk
                        dbs[h, lo:lo + 1, :] += _colsum(gk) - _colsum(g)
                nf = U // 8
                dq8 = [[dq_i[h][8 * f:8 * f + 8] for f in range(nf)] for h in H]
                db8 = [[db_i[h][8 * f:8 * f + 8] for f in range(nf)] for h in H]
                key_v = [[] for _ in H]
                key_k = [[] for _ in H]
                key_b = [[] for _ in H]
                for s in range(U):
                    row = slice(lo + s, lo + s + 1)
                    for h in H:
                        bs = b_s[h, row, :]
                        ks = k_s[h, row, :]
                        vs = v_s[h, row, :]
                        tv = tk = tb = None
                        for f in range(s // 8, nf):
                            p8 = slice(8 * f, 8 * f + 8)
                            arg = b_i[h][p8] - bs
                            if s > 8 * f:
                                arg = arg + nmask[s - 8 * f]
                            dec = jnp.exp(arg)
                            qd = q_i[h][p8] * dec
                            y_ = qd * ks
                            w = _rowsum(y_)
                            dw = _rowsum(do_i[h][p8] * vs)
                            g = dw * y_
                            dq8[h][f] = dq8[h][f] + dw * dec * ks
                            db8[h][f] = db8[h][f] + g
                            cv, ck = w * do_i[h][p8], dw * qd
                            tv, tk, tb = (cv, ck, g) if tv is None else (tv + cv, tk + ck, tb + g)
                        key_v[h].append(tv)
                        key_k[h].append(tk)
                        key_b[h].append(tb)
                for h in H:
                    for f in range(nf):
                        r8 = slice(lo + 8 * f, lo + 8 * f + 8)
                        dvs[h, r8, :] += _colsum8(key_v[h][8 * f:8 * f + 8])
                        dks[h, r8, :] += _colsum8(key_k[h][8 * f:8 * f + 8])
                        dbs[h, r8, :] += db8[h][f] - _colsum8(key_b[h][8 * f:8 * f + 8])
                    dq_parts[h] += dq8[h]
            dlogf = [_dot_exact(tri, dbs[h], (((0,), (0,)), ((), ()))) for h in H]
            for h in H:
                dqh = jnp.concatenate(dq_parts[h], axis=0)
                dk = dks[h]
                ah, omlh = _head(a, h), _head(oml, h)
                pa = jnp.exp(ah - logf[h])
                pt = jnp.exp(t[h] - logf[h])
                dt = dlogf[h] * pt
                dlb_ref[0:1, _lanes(h)] += _colsum(dlogf[h] * pa)
                dlb_ref[1:2, _lanes(h)] += _colsum(dt)
                dlb_ref[2:3, _lanes(h)] += _colsum(dk * snz[h])
                dz = dt * snz[h] - dk * omlh * snz[h] * (1.0 - snz[h])
                dqp = dqh * (sq[h] + qp[h] * sq[h] * (1.0 - sq[h]))
                dqp_ref[sl, _lanes(h)] = dqp.astype(dqp_ref.dtype)
                dz_ref[sl, _lanes(h)] = dz.astype(dz_ref.dtype)
                dv_ref[sl, _lanes(h)] = dvs[h].astype(dv_ref.dtype)
            return carry

        lax.fori_loop(0, cpb, chunk, 0)

        if exchange is not None:
            pl.when(pl.program_id(0) == nb - 1)(comm.finish)

    rev = lambda r: nb - 1 - r
    col = lambda kblk: pl.BlockSpec((rows, HG_WIDTH), lambda r: (rev(r), kblk))
    acc = _const_spec((8, HG_WIDTH))
    in_specs = [col(0), col(1), col(2), col(3), col(0),
                pl.BlockSpec((HG_HEADS, cpb, HG_DK, HG_DK), lambda r: (0, rev(r), 0, 0)),
                col(0), acc, _const_spec((1, HG_WIDTH))]
    out_specs = [col(0)] * 4 + [acc, acc]
    out_shape = [jax.ShapeDtypeStruct((S, HG_WIDTH), BF16)] * 4 + [jax.ShapeDtypeStruct((8, HG_WIDTH), F32)] * 2
    scratch = [pltpu.VMEM((HG_HEADS, HG_DK, HG_DK), F32)] + [pltpu.VMEM((HG_HEADS, C, HG_DK), F32)] * 6
    args = [proj, proj, proj, proj, o_raw, states, dy, lbrows, ng]
    if exchange is not None:
        in_specs.append(ANY)
        out_specs.append(ANY)
        out_shape.append(jax.ShapeDtypeStruct(exchange.shape, exchange.dtype))
        scratch += _Exchange.SCRATCH
        args.append(exchange)
    return pl.pallas_call(
        body,
        name="hgrn_bwd" if exchange is None else "hgrn_bwd_exchange",
        grid=(nb,),
        in_specs=in_specs,
        out_specs=out_specs,
        out_shape=out_shape,
        scratch_shapes=scratch,
        compiler_params=_params("arbitrary"),
    )(*args)


GROUP_LANES = AT_GROUP * WINDOW


def swa_mask():
    W = WINDOW
    kpos = lax.broadcasted_iota(jnp.int32, (2, 2 * W, GROUP_LANES), 1)
    qpos = (lax.broadcasted_iota(jnp.int32, (2, 2 * W, GROUP_LANES), 2) & (W - 1)) + W
    first = lax.broadcasted_iota(jnp.int32, (2, 2 * W, GROUP_LANES), 0) == 0
    rel = qpos - kpos
    valid = (rel >= 0) & (rel < W) & jnp.logical_not(first & (kpos < W))
    return jnp.where(valid, 0.0, MASK_VALUE).astype(F32)


def _group_lanes(xt, g):
    Dh = AT_HEAD_DIM
    return jnp.concatenate([xt[(g * AT_GROUP + j) * Dh:(g * AT_GROUP + j + 1) * Dh] for j in range(AT_GROUP)],
                           axis=1)


SWA_SCALE = AT_HEAD_DIM ** -0.5


def _swa_softmax_t(s, sink_row):
    m = jnp.maximum(jnp.max(s, axis=0, keepdims=True), sink_row)
    e = jnp.exp(s - m)
    es = jnp.exp(sink_row - m)
    inv = 1.0 / (_colsum(e) + es)
    return e * inv, es * inv


SWA_BLOCKS = 4
_BG = [(b, g) for b in range(SWA_BLOCKS) for g in range(AT_KV_HEADS)]


def _swa_specs(col_q):
    W = WINDOW
    rows = SWA_BLOCKS * W
    prev = lambda n: jnp.maximum(SWA_BLOCKS * n - 1, 0)
    return [pl.BlockSpec((rows, AT_WIDTH), lambda n: (n, col_q)),
            pl.BlockSpec((W, 128), lambda n: (prev(n), 20)),
            pl.BlockSpec((rows, 128), lambda n: (n, 20)),
            pl.BlockSpec((W, 128), lambda n: (prev(n), 21)),
            pl.BlockSpec((rows, 128), lambda n: (n, 21)),
            _const_spec((8 * AT_KV_HEADS, GROUP_LANES)),
            _const_spec((2, 2 * W, GROUP_LANES))]


def _swa_operands(n, q_ref, kp_ref, k_ref, vp_ref, v_ref, sk_ref, mask_ref):
    W, Dh = WINDOW, AT_HEAD_DIM
    k_all = jnp.concatenate([kp_ref[...], k_ref[...]], axis=0)
    v_all = jnp.concatenate([vp_ref[...], v_ref[...]], axis=0)
    kk = [k_all[b * W:(b + 2) * W] for b in range(SWA_BLOCKS)]
    vv = [v_all[b * W:(b + 2) * W] for b in range(SWA_BLOCKS)]
    masks = [mask_ref[jnp.minimum(n, 1)]] + [mask_ref[1]] * (SWA_BLOCKS - 1)
    qt = [(q_ref[b * W:(b + 1) * W, :] * SWA_SCALE).T for b in range(SWA_BLOCKS)]
    kg = {(b, g): kk[b][:, g * Dh:(g + 1) * Dh] for b, g in _BG}
    qg = {(b, g): _group_lanes(qt[b], g) for b, g in _BG}
    s = {bg: _dot(kg[bg], qg[bg]) + masks[bg[0]] for bg in _BG}
    sink = {(b, g): sk_ref[8 * g:8 * g + 1, :] for b, g in _BG}
    return kk, vv, kg, qg, s, sink


def swa_fwd(proj, sink_rows, mask, y):
    S = proj.shape[0]
    W, Dh = WINDOW, AT_HEAD_DIM
    rows = SWA_BLOCKS * W

    def body(q_ref, kp_ref, k_ref, vp_ref, v_ref, sk_ref, mask_ref, y_in, y_ref):
        del y_in
        _, vv, _, _, s, sink = _swa_operands(pl.program_id(0), q_ref, kp_ref, k_ref, vp_ref, v_ref,
                                             sk_ref, mask_ref)
        vt = [v.T for v in vv]
        p = {bg: _swa_softmax_t(s[bg], sink[bg])[0] for bg in _BG}
        ot = {(b, g): _dot(vt[b][g * Dh:(g + 1) * Dh], p[b, g]) for b, g in _BG}
        for b in range(SWA_BLOCKS):
            outs = [ot[b, g][:, j * W:(j + 1) * W] for g in range(AT_KV_HEADS) for j in range(AT_GROUP)]
            y_ref[b * W:(b + 1) * W, :] = jnp.concatenate(outs, axis=0).T.astype(y_ref.dtype)

    return pl.pallas_call(
        body,
        name="swa_fwd",
        grid=(S // rows,),
        in_specs=_swa_specs(4) + [pl.BlockSpec(memory_space=pl.ANY)],
        out_specs=pl.BlockSpec((rows, AT_WIDTH), lambda n: (n, 1)),
        out_shape=jax.ShapeDtypeStruct(y.shape, y.dtype),
        input_output_aliases={7: 0},
        compiler_params=_params("parallel"),
    )(proj, proj, proj, proj, proj, sink_rows, mask, y)


def swa_bwd(proj, sink_rows, mask, dy):
    S = proj.shape[0]
    W, Dh = WINDOW, AT_HEAD_DIM
    rows = SWA_BLOCKS * W
    nsteps = S // rows

    def body(q_ref, kp_ref, k_ref, vp_ref, v_ref, sk_ref, mask_ref, dy_ref,
             dq_ref, dko_ref, dkp_ref, dvo_ref, dvp_ref, dsk_ref):
        n = pl.program_id(0)

        @pl.when(n == 0)
        def _():
            dsk_ref[...] = jnp.zeros_like(dsk_ref)

        kk, vv, _, qg, s, sink = _swa_operands(n, q_ref, kp_ref, k_ref, vp_ref, v_ref, sk_ref, mask_ref)
        kt = [k.T for k in kk]
        dot_ = [dy_ref[b * W:(b + 1) * W, :].T for b in range(SWA_BLOCKS)]
        dog = {(b, g): _group_lanes(dot_[b], g) for b, g in _BG}
        dp = {(b, g): _dot(vv[b][:, g * Dh:(g + 1) * Dh], dog[b, g]) for b, g in _BG}
        pp = {bg: _swa_softmax_t(s[bg], sink[bg]) for bg in _BG}
        delta = {bg: _colsum(dp[bg] * pp[bg][0]) for bg in _BG}
        ds = {bg: pp[bg][0] * (dp[bg] - delta[bg]) for bg in _BG}
        dqt = {(b, g): _dot(kt[b][g * Dh:(g + 1) * Dh], ds[b, g]) * SWA_SCALE for b, g in _BG}
        dk = {bg: _dot_nt(ds[bg], qg[bg]) for bg in _BG}
        dv = {bg: _dot_nt(pp[bg][0], dog[bg]) for bg in _BG}
        for g in range(AT_KV_HEADS):
            tot = -(pp[0, g][1] * delta[0, g])
            for b in range(1, SWA_BLOCKS):
                tot = tot - pp[b, g][1] * delta[b, g]
            dsk_ref[8 * g:8 * g + 1, :] += tot
        for b in range(SWA_BLOCKS):
            r = slice(b * W, (b + 1) * W)
            dqs = [dqt[b, g][:, j * W:(j + 1) * W] for g in range(AT_KV_HEADS) for j in range(AT_GROUP)]
            dq_ref[r, :] = jnp.concatenate(dqs, axis=0).T.astype(dq_ref.dtype)
            dkb = jnp.concatenate([dk[b, g] for g in range(AT_KV_HEADS)], axis=1)
            dvb = jnp.concatenate([dv[b, g] for g in range(AT_KV_HEADS)], axis=1)
            dkp_ref[r, :] = dkb[:W]
            dko_ref[r, :] = dkb[W:]
            dvp_ref[r, :] = dvb[:W]
            dvo_ref[r, :] = dvb[W:]

        @pl.when(n == nsteps - 1)
        def _():
            for g in range(AT_KV_HEADS):
                for j in range(AT_GROUP):
                    tot = _rowsum(dsk_ref[8 * g:8 * g + 1, j * W:(j + 1) * W])
                    dsk_ref[8 * g + 1 + j:8 * g + 2 + j, :] = jnp.broadcast_to(tot, (1, GROUP_LANES))

    kv = pl.BlockSpec((rows, 128), lambda n: (n, 0))
    sk = _const_spec((8 * AT_KV_HEADS, GROUP_LANES))
    return pl.pallas_call(
        body,
        name="swa_bwd",
        grid=(nsteps,),
        in_specs=_swa_specs(4) + [pl.BlockSpec((rows, AT_WIDTH), lambda n: (n, 1))],
        out_specs=[pl.BlockSpec((rows, AT_WIDTH), lambda n: (n, 0)), kv, kv, kv, kv, sk],
        out_shape=[jax.ShapeDtypeStruct((S, AT_WIDTH), BF16)]
                  + [jax.ShapeDtypeStruct((S, 128), F32)] * 4
                  + [jax.ShapeDtypeStruct((8 * AT_KV_HEADS, GROUP_LANES), F32)],
        compiler_params=_params("arbitrary"),
    )(proj, proj, proj, proj, proj, sink_rows, mask, dy)


def assemble_dproj(hg_grads, dq_at, dko, dkp, dvo, dvp, *, rows):
    S = dq_at.shape[0]
    W = WINDOW
    nb = S // W
    bpr = rows // W

    def body(a0, a1, a2, a3, dq, ko, kp, kpn, vo, vp, vpn, out):
        r = pl.program_id(0)
        for i, a in enumerate((a0, a1, a2, a3)):
            out[:, i * HG_WIDTH:(i + 1) * HG_WIDTH] = a[...]
        base = 4 * HG_WIDTH
        out[:, base:base + AT_WIDTH] = dq[...]
        last = (r == pl.num_programs(0) - 1)
        for off, own, pv, pvn in ((base + AT_WIDTH, ko, kp, kpn), (base + AT_WIDTH + 128, vo, vp, vpn)):
            if bpr > 1:
                out[0:rows - W, off:off + 128] = (own[0:rows - W, :] + pv[W:rows, :]).astype(out.dtype)
            nxt = jnp.where(last, 0.0, pvn[...])
            out[rows - W:rows, off:off + 128] = (own[rows - W:rows, :] + nxt).astype(out.dtype)

    hg = pl.BlockSpec((rows, HG_WIDTH), lambda r: (r, 0))
    blk = pl.BlockSpec((rows, 128), lambda r: (r, 0))
    nxt = pl.BlockSpec((W, 128), lambda r: (jnp.minimum((r + 1) * bpr, nb - 1), 0))
    return pl.pallas_call(
        body,
        name="assemble_dproj",
        grid=(S // rows,),
        in_specs=[hg, hg, hg, hg, pl.BlockSpec((rows, AT_WIDTH), lambda r: (r, 0)),
                  blk, blk, nxt, blk, blk, nxt],
        out_specs=pl.BlockSpec((rows, IN_WIDTH), lambda r: (r, 0)),
        out_shape=jax.ShapeDtypeStruct((S, IN_WIDTH), BF16),
        compiler_params=_params("parallel"),
    )(*hg_grads, dq_at, dko, dkp, dkp, dvo, dvp, dvp)


ROW_TILE = 512
COL_TILE = 1408


def _col_tile(n):
    return n if n <= COL_TILE else COL_TILE


def _rms_scale(x):
    return lax.rsqrt(jnp.mean(x * x, axis=1, keepdims=True) + EPS)


def _rms_bwd(d, x, g):
    rs = _rms_scale(x)
    xh = x * rs
    dxh = d * g
    return rs * (dxh - xh * jnp.mean(dxh * xh, axis=1, keepdims=True)), _colsum(d * xh)


def mm(a, b, *, nt=False, out_dtype=F32, res=None, norm_g=None, rms_bwd=None, name):
    parts = a if isinstance(a, tuple) else (a,)
    M, K = parts[0].shape
    N = b.shape[0] if nt else b.shape[1]
    tall = K <= D_MODEL and rms_bwd is None and len(parts) == 1 and M % (2 * ROW_TILE) == 0
    tm = 2 * ROW_TILE if tall else min(ROW_TILE, M)
    tn = _col_tile(N)
    whole_rows = norm_g is not None or rms_bwd is not None
    assert M % tm == 0 and N % tn == 0 and (tn == N or not whole_rows)
    np_ = len(parts)

    def body(*refs):
        a_refs, b_refs, rest = refs[:np_], refs[np_:2 * np_], refs[2 * np_:]
        dot = _dot_nt if nt else _dot
        acc = dot(a_refs[0][...], b_refs[0][...])
        for ar, br in zip(a_refs[1:], b_refs[1:]):
            acc = acc + dot(ar[...], br[...])
        if rms_bwd is not None:
            h_ref, g_ref, dr_ref, dh_ref, dhb_ref, dg_ref = rest

            @pl.when(pl.program_id(1) == 0)
            def _():
                dg_ref[...] = jnp.zeros_like(dg_ref)

            dx, dgp = _rms_bwd(acc, h_ref[...], g_ref[...])
            dg_ref[0:1, :] += dgp
            dh = dr_ref[...] + dx
            dh_ref[...] = dh
            dhb_ref[...] = dh.astype(BF16)
            return
        rest = list(rest)
        if res is not None:
            acc = acc + rest.pop(0)[...]
        if norm_g is not None:
            g_ref = rest.pop(0)
            rest[1][...] = (acc * _rms_scale(acc) * g_ref[...]).astype(BF16)
        rest[0][...] = acc.astype(rest[0].dtype)

    row = pl.BlockSpec((tm, tn), lambda j, i: (i, j))
    in_specs = [pl.BlockSpec((tm, K), lambda j, i: (i, 0)) for _ in parts]
    for kb in range(np_):
        in_specs.append(pl.BlockSpec((tn, K), lambda j, i, kb=kb: (j, kb)) if nt
                        else pl.BlockSpec((K, tn), lambda j, i, kb=kb: (kb, j)))
    args = list(parts) + [b] * np_
    if rms_bwd is not None:
        h, g, dres = rms_bwd
        in_specs += [row, _const_spec((1, N)), row]
        args += [h, g, dres]
        out_specs = [row, row, _const_spec((8, N))]
        out_shape = [jax.ShapeDtypeStruct((M, N), F32), jax.ShapeDtypeStruct((M, N), BF16),
                     jax.ShapeDtypeStruct((8, N), F32)]
        sem = ("arbitrary", "arbitrary")
    else:
        if res is not None:
            in_specs.append(row)
            args.append(res)
        out_specs, out_shape = [row], [jax.ShapeDtypeStruct((M, N), out_dtype)]
        if norm_g is not None:
            in_specs.append(_const_spec((1, N)))
            args.append(norm_g)
            out_specs.append(row)
            out_shape.append(jax.ShapeDtypeStruct((M, N), BF16))
        sem = ("parallel", "parallel")
    out = pl.pallas_call(
        body,
        name=name,
        grid=(N // tn, M // tm),
        in_specs=in_specs,
        out_specs=out_specs,
        out_shape=out_shape,
        compiler_params=_params(*sem),
    )(*args)
    return out[0] if len(out) == 1 else out


def mm_tn(a, b, *, name):
    M, K = a.shape
    N = b.shape[1]
    tm = next((t for t in (4 * ROW_TILE, 2 * ROW_TILE) if M % t == 0), min(ROW_TILE, M))
    tk = _col_tile(K)
    tn = _col_tile(N)
    assert M % tm == 0 and K % tk == 0 and N % tn == 0

    def body(a_ref, b_ref, o_ref):
        @pl.when(pl.program_id(2) == 0)
        def _():
            o_ref[...] = jnp.zeros_like(o_ref)

        o_ref[...] += _dot_tn(a_ref[...], b_ref[...])

    return pl.pallas_call(
        body,
        name=name,
        grid=(K // tk, N // tn, M // tm),
        in_specs=[pl.BlockSpec((tm, tk), lambda k, j, i: (i, k)),
                  pl.BlockSpec((tm, tn), lambda k, j, i: (i, j))],
        out_specs=pl.BlockSpec((tk, tn), lambda k, j, i: (k, j)),
        out_shape=jax.ShapeDtypeStruct((K, N), F32),
        compiler_params=_params("parallel", "parallel", "arbitrary"),
    )(a, b)


def _row_spec(tm, width):
    return pl.BlockSpec((tm, width), lambda i: (i, 0))


def _const_spec(shape):
    return pl.BlockSpec(shape, lambda *_: (0,) * len(shape))


def rmsnorm_fwd(h, g, *, name):
    S, D = h.shape
    tm = min(ROW_TILE, S)

    def body(h_ref, g_ref, u_ref):
        x = h_ref[...]
        rs = lax.rsqrt(jnp.mean(x * x, axis=1, keepdims=True) + EPS)
        u_ref[...] = (x * rs * g_ref[...]).astype(u_ref.dtype)

    return pl.pallas_call(
        body, name=name, grid=(S // tm,),
        in_specs=[_row_spec(tm, D), _const_spec((1, D))],
        out_specs=_row_spec(tm, D),
        out_shape=jax.ShapeDtypeStruct((S, D), BF16),
        compiler_params=_params("parallel"),
    )(h, g)


HALO = 16


def _shift_down(x, edge8, s):
    sh = pltpu.roll(x, s, 0)
    er = pltpu.roll(edge8, s, 0)
    row8 = lax.broadcasted_iota(jnp.int32, er.shape, 0)
    top = jnp.where(row8 < s, er, sh[0:8])
    return jnp.concatenate([top, sh[8:]], axis=0)


def _shift_up(x, s):
    return pltpu.roll(x, x.shape[0] - s, 0)


def _conv_pre(a, prev8, w_ref, cb_ref):
    a1 = _shift_down(a, prev8, 1)
    a2 = _shift_down(a, prev8, 2)
    ac = w_ref[2:3, :] * a + w_ref[1:2, :] * a1 + w_ref[0:1, :] * a2 + cb_ref[...]
    return ac, a1, a2


def convffn_fwd(hh, cw8, cb):
    S = hh.shape[0]
    tm = min(ROW_TILE, S)
    tn = _col_tile(D_FF)
    nj = D_FF // tn

    def body(a_ref, ap_ref, b_ref, w_ref, cb_ref, o_ref):
        prev8 = jnp.where(pl.program_id(1) == 0, 0.0, ap_ref[...].astype(F32)[HALO - 8:HALO])
        ac, _, _ = _conv_pre(a_ref[...].astype(F32), prev8, w_ref, cb_ref)
        o_ref[...] = (ac * _sigmoid(ac) * b_ref[...].astype(F32)).astype(o_ref.dtype)

    rh = tm // HALO
    return pl.pallas_call(
        body, name="convffn_fwd", grid=(nj, S // tm),
        in_specs=[pl.BlockSpec((tm, tn), lambda j, i: (i, j)),
                  pl.BlockSpec((HALO, tn), lambda j, i: (jnp.maximum(i * rh - 1, 0), j)),
                  pl.BlockSpec((tm, tn), lambda j, i: (i, j + nj)),
                  pl.BlockSpec((8, tn), lambda j, i: (0, j)),
                  pl.BlockSpec((1, tn), lambda j, i: (0, j))],
        out_specs=pl.BlockSpec((tm, tn), lambda j, i: (i, j)),
        out_shape=jax.ShapeDtypeStruct((S, D_FF), BF16),
        compiler_params=_params("parallel", "parallel"),
    )(hh, hh, hh, cw8, cb)


def convffn_bwd(hh, dact, cw8, cb):
    S = hh.shape[0]
    tm = min(ROW_TILE, S)
    tn = _col_tile(D_FF)
    nj = D_FF // tn
    ni = S // tm

    def body(a_ref, ap_ref, an_ref, b_ref, bn_ref, d_ref, dn_ref, w_ref, cb_ref, o_a, o_b, dw_ref):
        i = pl.program_id(1)

        @pl.when(i == 0)
        def _():
            dw_ref[...] = jnp.zeros_like(dw_ref)

        up = lambda r: r[...].astype(F32)
        prev8 = jnp.where(i == 0, 0.0, up(ap_ref)[HALO - 8:HALO])
        a = jnp.concatenate([up(a_ref), up(an_ref)[0:8]], axis=0)
        b = jnp.concatenate([up(b_ref), up(bn_ref)[0:8]], axis=0)
        d = jnp.concatenate([up(d_ref), jnp.where(i == ni - 1, 0.0, up(dn_ref)[0:8])], axis=0)
        ac, a1, a2 = _conv_pre(a, prev8, w_ref, cb_ref)
        sa = _sigmoid(ac)
        o_b[...] = (d[0:tm] * ac[0:tm] * sa[0:tm]).astype(o_b.dtype)
        dac = d * b * (sa + ac * sa * (1.0 - sa))
        da = w_ref[2:3, :] * dac + w_ref[1:2, :] * _shift_up(dac, 1) + w_ref[0:1, :] * _shift_up(dac, 2)
        o_a[...] = da[0:tm].astype(o_a.dtype)
        dc = dac[0:tm]
        dw_ref[0:1, :] += _colsum(dc * a2[0:tm])
        dw_ref[1:2, :] += _colsum(dc * a1[0:tm])
        dw_ref[2:3, :] += _colsum(dc * a[0:tm])
        dw_ref[3:4, :] += _colsum(dc)

    rh = tm // HALO
    last = S // HALO - 1
    cur = lambda off: pl.BlockSpec((tm, tn), lambda j, i: (i, j + off))
    nxt = lambda off: pl.BlockSpec((HALO, tn), lambda j, i: (jnp.minimum((i + 1) * rh, last), j + off))
    return pl.pallas_call(
        body, name="convffn_bwd", grid=(nj, ni),
        in_specs=[cur(0),
                  pl.BlockSpec((HALO, tn), lambda j, i: (jnp.maximum(i * rh - 1, 0), j)),
                  nxt(0), cur(nj), nxt(nj), cur(0), nxt(0),
                  pl.BlockSpec((8, tn), lambda j, i: (0, j)),
                  pl.BlockSpec((1, tn), lambda j, i: (0, j))],
        out_specs=[cur(0), cur(0), pl.BlockSpec((8, tn), lambda j, i: (0, j))],
        out_shape=[jax.ShapeDtypeStruct((S, D_FF), BF16), jax.ShapeDtypeStruct((S, D_FF), BF16),
                   jax.ShapeDtypeStruct((8, D_FF), F32)],
        compiler_params=_params("parallel", "arbitrary"),
    )(hh, hh, hh, hh, hh, dact, dact, cw8, cb)


def ple_fwd(h, gpre, p, wpu, norm_g):
    S, D = h.shape
    tm = min(ROW_TILE, S)

    def body(h_ref, g_ref, p_ref, w_ref, ng_ref, o_ref, u_ref):
        out = h_ref[...] + _sigmoid(g_ref[...]) * _dot(p_ref[...], w_ref[...])
        o_ref[...] = out
        u_ref[...] = (out * _rms_scale(out) * ng_ref[...]).astype(BF16)

    return pl.pallas_call(
        body, name="ple_fwd", grid=(S // tm,),
        in_specs=[_row_spec(tm, D), _row_spec(tm, D), _row_spec(tm, PLE_DIM), _const_spec((PLE_DIM, D)),
                  _const_spec((1, D))],
        out_specs=[_row_spec(tm, D), _row_spec(tm, D)],
        out_shape=[jax.ShapeDtypeStruct((S, D), F32), jax.ShapeDtypeStruct((S, D), BF16)],
        compiler_params=_params("parallel"),
    )(h, gpre, p, wpu, norm_g)


def ple_bwd(dh, gpre, p, wpu):
    S, D = dh.shape
    tm = min(ROW_TILE, S)

    def body(d_ref, g_ref, p_ref, w_ref, dpu_ref, dg_ref):
        d = d_ref[...]
        gate = _sigmoid(g_ref[...])
        pu = _dot(p_ref[...], w_ref[...])
        dpu_ref[...] = (d * gate).astype(dpu_ref.dtype)
        dg_ref[...] = (d * pu * gate * (1.0 - gate)).astype(dg_ref.dtype)

    return pl.pallas_call(
        body, name="ple_bwd", grid=(S // tm,),
        in_specs=[_row_spec(tm, D), _row_spec(tm, D), _row_spec(tm, PLE_DIM), _const_spec((PLE_DIM, D))],
        out_specs=[_row_spec(tm, D), _row_spec(tm, D)],
        out_shape=[jax.ShapeDtypeStruct((S, D), BF16)] * 2,
        compiler_params=_params("parallel"),
    )(dh, gpre, p, wpu)


def loss_head(h, g, tgt):
    S, D = h.shape
    tm = min(ROW_TILE, S)

    def body(h_ref, g_ref, t_ref, dh_ref, dhb_ref, l_ref, dg_ref):
        @pl.when(pl.program_id(0) == 0)
        def _():
            l_ref[...] = jnp.zeros_like(l_ref)
            dg_ref[...] = jnp.zeros_like(dg_ref)

        x = h_ref[...]
        gr = g_ref[...]
        rs = lax.rsqrt(jnp.mean(x * x, axis=1, keepdims=True) + EPS)
        xh = x * rs
        err = xh * gr - t_ref[...]
        l_ref[0:1, 0:1] += 0.5 * _colsum(jnp.mean(err * err, axis=1, keepdims=True))
        dy = err * (1.0 / D)
        dg_ref[0:1, :] += _colsum(dy * xh)
        dxh = dy * gr
        dh = rs * (dxh - xh * jnp.mean(dxh * xh, axis=1, keepdims=True))
        dh_ref[...] = dh
        dhb_ref[...] = dh.astype(BF16)

    return pl.pallas_call(
        body, name="loss_head", grid=(S // tm,),
        in_specs=[_row_spec(tm, D), _const_spec((1, D)), _row_spec(tm, D)],
        out_specs=[_row_spec(tm, D), _row_spec(tm, D), _const_spec((8, 128)), _const_spec((8, D))],
        out_shape=[jax.ShapeDtypeStruct((S, D), F32), jax.ShapeDtypeStruct((S, D), BF16),
                   jax.ShapeDtypeStruct((8, 128), F32), jax.ShapeDtypeStruct((8, D), F32)],
        compiler_params=_params("arbitrary"),
    )(h, g, tgt)


def _lb_rows(l_ref):
    l = l_ref[...]
    e = jnp.exp(l - jnp.max(l, axis=0, keepdims=True))
    p = e / _colsum(e)
    lbs, run = [], None
    for i in range(DEPTH):
        run = p[i:i + 1] if i == 0 else run + p[i:i + 1]
        lbs.append(run - p[0:1])
    return p, lbs


def lb_fwd(lb_logits):
    def body(l_ref, o_ref):
        _, lbs = _lb_rows(l_ref)
        o_ref[...] = jnp.zeros_like(o_ref)
        for i, lb in enumerate(lbs):
            o_ref[8 * i:8 * i + 1, :] = jnp.log(jnp.maximum(lb, LB_FLOOR))
            o_ref[8 * i + 1:8 * i + 2, :] = jnp.log1p(-lb)
            o_ref[8 * i + 2:8 * i + 3, :] = 1.0 - lb
            o_ref[8 * i + 3:8 * i + 4, :] = lb

    return pl.pallas_call(
        body, name="lb_fwd",
        out_shape=jax.ShapeDtypeStruct((DEPTH * 8, HG_WIDTH), F32),
    )(lb_logits)


def lb_bwd(dlbrows, lb_logits):
    def body(d_ref, l_ref, o_ref):
        p, lbs = _lb_rows(l_ref)
        dlb = []
        for i, lb in enumerate(lbs):
            da = d_ref[8 * i:8 * i + 1, :]
            dc = d_ref[8 * i + 1:8 * i + 2, :]
            do = d_ref[8 * i + 2:8 * i + 3, :]
            dlb.append(jnp.where(lb > LB_FLOOR, da / jnp.maximum(lb, LB_FLOOR), 0.0) - dc / (1.0 - lb) - do)
        dp = [jnp.zeros_like(dlb[0])]
        for j in range(1, DEPTH):
            acc = dlb[j]
            for i in range(j + 1, DEPTH):
                acc = acc + dlb[i]
            dp.append(acc)
        dot_ = p[0:1] * dp[0]
        for j in range(1, DEPTH):
            dot_ = dot_ + p[j:j + 1] * dp[j]
        o_ref[...] = jnp.zeros_like(o_ref)
        for j in range(DEPTH):
            o_ref[j:j + 1, :] = p[j:j + 1] * (dp[j] - dot_)

    return pl.pallas_call(
        body, name="lb_bwd",
        out_shape=jax.ShapeDtypeStruct((8, HG_WIDTH), F32),
    )(dlbrows, lb_logits)


def adamw(w, g, m, v, *, name):
    R, C = w.shape
    tr = next((t for t in (512, 256, 128, 64, 32, 16, 8) if R % t == 0), R)

    def body(w_ref, g_ref, m_ref, v_ref, d_ref, m2_ref, v2_ref):
        gv = g_ref[...]
        m2 = ADAM_B1 * m_ref[...] + (1.0 - ADAM_B1) * gv
        v2 = ADAM_B2 * v_ref[...] + (1.0 - ADAM_B2) * (gv * gv)
        mh = m2 / (1.0 - ADAM_B1 ** ADAM_STEP)
        vh = v2 / (1.0 - ADAM_B2 ** ADAM_STEP)
        d_ref[...] = -ADAM_LR * (mh / (jnp.sqrt(vh) + ADAM_EPS) + ADAM_WD * w_ref[...])
        m2_ref[...] = m2
        v2_ref[...] = v2

    spec = pl.BlockSpec((tr, C), lambda i: (i, 0))
    return pl.pallas_call(
        body, name=name, grid=(R // tr,),
        in_specs=[spec] * 4, out_specs=[spec] * 3,
        out_shape=[jax.ShapeDtypeStruct((R, C), F32)] * 3,
        compiler_params=_params("parallel"),
    )(w, g, m, v)


def sum_slots(x, *, out_dtype, name):
    n, R, C = x.shape
    tr = 848 if R % 848 == 0 else R

    def body(x_ref, o_ref):
        acc = x_ref[0].astype(F32)
        for k in range(1, n):
            acc = acc + x_ref[k].astype(F32)
        o_ref[...] = acc.astype(o_ref.dtype)

    return pl.pallas_call(
        body, name=name, grid=(R // tr,),
        in_specs=[pl.BlockSpec((n, tr, C), lambda i: (0, i, 0))],
        out_specs=pl.BlockSpec((tr, C), lambda i: (i, 0)),
        out_shape=jax.ShapeDtypeStruct((R, C), out_dtype),
        compiler_params=_params("parallel"),
    )(x)


MESH = pl.DeviceIdType.MESH
ANY = pl.BlockSpec(memory_space=pl.ANY)


def _place():
    return lax.axis_index("x"), lax.axis_index("y"), lax.axis_index("c")


def _other_chips(x, y):
    return [(1 - x, y), (x, 1 - y), (1 - x, 1 - y)]


def small_allgather(buf):
    R, C = buf.shape

    def body(x_ref, out_ref, send_sems, recv_sems, local_sem):
        x, y, c = _place()
        me, sibling = (x, y, c), (x, y, 1 - c)
        chips = _other_chips(x, y)

        def slot(px, py, pc):
            return out_ref.at[4 * px + 2 * py + pc]

        def copy(k, block, to, src=None):
            return pltpu.make_async_remote_copy(
                src_ref=slot(*block) if src is None else src, dst_ref=slot(*block),
                send_sem=send_sems.at[k], recv_sem=recv_sems.at[k],
                device_id=to, device_id_type=MESH)

        mine = pltpu.make_async_copy(x_ref, slot(*me), local_sem)
        mine.start()
        first = [copy(0, me, sibling, src=x_ref)]
        first += [copy(1 + r, me, (*chip, c), src=x_ref) for r, chip in enumerate(chips)]
        for cp in first:
            cp.start()
        passed = [copy(4 + r, (*chip, c), sibling) for r, chip in enumerate(chips)]
        for r, chip in enumerate(chips):
            copy(1 + r, (*chip, c), me).wait_recv()
            passed[r].start()
        copy(0, sibling, me).wait_recv()
        for r, chip in enumerate(chips):
            copy(4 + r, (*chip, 1 - c), me).wait_recv()
        for cp in first + passed:
            cp.wait_send()
        mine.wait()

    return pl.pallas_call(
        body, name="small_allgather",
        out_shape=jax.ShapeDtypeStruct((8, R, C), buf.dtype),
        in_specs=[pl.BlockSpec(memory_space=pltpu.VMEM)],
        out_specs=pl.BlockSpec(memory_space=pltpu.VMEM),
        scratch_shapes=[pltpu.SemaphoreType.DMA((7,)), pltpu.SemaphoreType.DMA((7,)),
                        pltpu.SemaphoreType.DMA],
    )(buf)


def weights_allgather(wp):
    def body(w_ref, g_ref, send_sems, recv_sems, local_sem):
        gather = _Gather(w_ref, g_ref, send_sems, recv_sems, local_sem)
        gather.start()
        gather.finish()

    return pl.pallas_call(
        body, name="weights_allgather",
        out_shape=jax.ShapeDtypeStruct((4,) + wp.shape, wp.dtype),
        in_specs=[ANY], out_specs=ANY,
        scratch_shapes=_Gather.SCRATCH,
    )(wp)


class _Gather:
    SCRATCH = [pltpu.SemaphoreType.DMA((6,)), pltpu.SemaphoreType.DMA((6,)), pltpu.SemaphoreType.DMA]

    def __init__(self, w_ref, g_ref, send_sems, recv_sems, local_sem):
        self.w_ref, self.g_ref, self.local_sem = w_ref, g_ref, local_sem
        self.send_sems, self.recv_sems = send_sems, recv_sems
        self.x, self.y, self.c = _place()
        self.chips = _other_chips(self.x, self.y)
        half = w_ref.shape[0] // 2
        self.mine = pl.ds(pl.multiple_of(self.c * half, 16), half)
        self.theirs = pl.ds(pl.multiple_of((1 - self.c) * half, 16), half)

    def _copy(self, k, chip_block, rows, to, src=None):
        dst = self.g_ref.at[chip_block, rows]
        return pltpu.make_async_remote_copy(
            src_ref=dst if src is None else src, dst_ref=dst,
            send_sem=self.send_sems.at[k], recv_sem=self.recv_sems.at[k],
            device_id=to, device_id_type=MESH)

    def _own(self):
        return pltpu.make_async_copy(self.w_ref, self.g_ref.at[2 * self.x + self.y], self.local_sem)

    def _first(self):
        return [self._copy(r, 2 * self.x + self.y, self.mine, (*chip, self.c), src=self.w_ref.at[self.mine])
                for r, chip in enumerate(self.chips)]

    def start(self):
        self._own().start()
        for cp in self._first():
            cp.start()

    def finish(self):
        sibling = (self.x, self.y, 1 - self.c)
        passed = [self._copy(3 + r, 2 * chip[0] + chip[1], self.mine, sibling) for r, chip in enumerate(self.chips)]
        for r, chip in enumerate(self.chips):
            self._copy(r, 2 * chip[0] + chip[1], self.mine, (*chip, self.c)).wait_recv()
            passed[r].start()
        for r, chip in enumerate(self.chips):
            self._copy(3 + r, 2 * chip[0] + chip[1], self.theirs, sibling).wait_recv()
        for cp in self._first() + passed:
            cp.wait_send()
        self._own().wait()


def sibling_swap(v, *, name):
    def body(v_ref, got_ref, send_sem, recv_sem):
        x, y, c = _place()
        cp = pltpu.make_async_remote_copy(
            src_ref=v_ref, dst_ref=got_ref, send_sem=send_sem, recv_sem=recv_sem,
            device_id=(x, y, 1 - c), device_id_type=MESH)
        cp.start()
        cp.wait()

    return pl.pallas_call(
        body, name=name,
        out_shape=jax.ShapeDtypeStruct(v.shape, v.dtype),
        in_specs=[ANY], out_specs=ANY,
        scratch_shapes=[pltpu.SemaphoreType.DMA, pltpu.SemaphoreType.DMA],
    )(v)


def chip_exchange(q):
    def body(q_ref, r_ref, send_sems, recv_sems, local_sem):
        exchange = _Exchange(q_ref, r_ref, send_sems, recv_sems, local_sem)
        exchange.start()
        exchange.finish()

    return pl.pallas_call(
        body, name="chip_exchange",
        out_shape=jax.ShapeDtypeStruct(q.shape, q.dtype),
        in_specs=[ANY], out_specs=ANY,
        scratch_shapes=_Exchange.SCRATCH,
    )(q)


class _Exchange:
    SCRATCH = [pltpu.SemaphoreType.DMA((3,)), pltpu.SemaphoreType.DMA((3,)), pltpu.SemaphoreType.DMA]

    def __init__(self, q_ref, r_ref, send_sems, recv_sems, local_sem):
        self.q_ref, self.r_ref, self.local_sem = q_ref, r_ref, local_sem
        self.send_sems, self.recv_sems = send_sems, recv_sems
        self.x, self.y, self.c = _place()
        self.j = 2 * self.x + self.y
        self.chips = _other_chips(self.x, self.y)

    def _copy(self, r, src_block, dst_block, chip):
        return pltpu.make_async_remote_copy(
            src_ref=self.q_ref.at[src_block], dst_ref=self.r_ref.at[dst_block],
            send_sem=self.send_sems.at[r], recv_sem=self.recv_sems.at[r],
            device_id=(*chip, self.c), device_id_type=MESH)

    def _own(self):
        return pltpu.make_async_copy(self.q_ref.at[self.j], self.r_ref.at[self.j], self.local_sem)

    def _sends(self):
        return [self._copy(r, 2 * chip[0] + chip[1], self.j, chip) for r, chip in enumerate(self.chips)]

    def start(self):
        self._own().start()
        for cp in self._sends():
            cp.start()

    def finish(self):
        for r, chip in enumerate(self.chips):
            jr = 2 * chip[0] + chip[1]
            self._copy(r, jr, jr, chip).wait_recv()
        for cp in self._sends():
            cp.wait_send()
        self._own().wait()


N_CHIPS = 4
_PACK = (("w_in", 704), ("w_out", 256), ("w_up", 1408), ("w_down", 704), ("w_ple_gate", 256), ("w_ple_up", 64))
LAYER_ROWS = sum(r for _, r in _PACK)
PACK_ROWS = DEPTH * LAYER_ROWS
HALF_ROWS = PACK_ROWS // 2


def _pack_shards(sh):
    parts = []
    for i in range(DEPTH):
        for name, rows in _PACK:
            parts.append(sh[name][i].reshape(rows, D_MODEL))
    return jnp.concatenate(parts, axis=0)


def _unpack_shards(slab):
    shapes = {"w_in": (D_MODEL, IN_WIDTH // N_CHIPS), "w_out": (D_MODEL // N_CHIPS, D_MODEL),
              "w_up": (D_MODEL, 2 * D_FF // N_CHIPS), "w_down": (D_FF // N_CHIPS, D_MODEL),
              "w_ple_gate": (D_MODEL // N_CHIPS, D_MODEL), "w_ple_up": (PLE_DIM, D_MODEL // N_CHIPS)}
    out = {name: [] for name, _ in _PACK}
    off = 0
    for i in range(DEPTH):
        for name, rows in _PACK:
            out[name].append(slab[off:off + rows].reshape(shapes[name]))
            off += rows
    return {k: jnp.stack(v) for k, v in out.items()}


_COL_SHARDED = ("w_in", "w_up", "w_ple_up")


def _full_from_chips(g, layer):
    per_chip = [_unpack_shards_layer(g[k], layer) for k in range(N_CHIPS)]
    return {name: jnp.concatenate([pc[name] for pc in per_chip], axis=1 if name in _COL_SHARDED else 0)
            for name, _ in _PACK}


def _unpack_shards_layer(slab, layer):
    shapes = {"w_in": (D_MODEL, IN_WIDTH // N_CHIPS), "w_out": (D_MODEL // N_CHIPS, D_MODEL),
              "w_up": (D_MODEL, 2 * D_FF // N_CHIPS), "w_down": (D_FF // N_CHIPS, D_MODEL),
              "w_ple_gate": (D_MODEL // N_CHIPS, D_MODEL), "w_ple_up": (PLE_DIM, D_MODEL // N_CHIPS)}
    out = {}
    off = layer * LAYER_ROWS
    for name, rows in _PACK:
        out[name] = slab[off:off + rows].reshape(shapes[name])
        off += rows
    return out


def _split_to_chips(full, name):
    r, c = full.shape
    if name in _COL_SHARDED:
        full = full.reshape(r, N_CHIPS, c // N_CHIPS).transpose(1, 0, 2)
    return full.reshape(N_CHIPS, -1, D_MODEL)


_SMALL = (("loss", 128), ("g_final", 1024), ("g_mix", 4096), ("lb_logits", 2048), ("hg_norm_g", 2048),
          ("attn_sinks", 128), ("g_ffn", 4096), ("conv_w", 4 * 3 * D_FF), ("conv_b", 4 * D_FF), ("g_ple", 4096))
SMALL_ROWS = 496


def _pack_small(d):
    parts = []
    for name, n in _SMALL:
        v = d[name].reshape(-1).astype(F32)
        parts.append(jnp.pad(v, (0, n - v.shape[0])))
    flat = jnp.concatenate(parts)
    return jnp.pad(flat, (0, SMALL_ROWS * 128 - flat.shape[0])).reshape(SMALL_ROWS, 128)


def _unpack_small(buf, shapes):
    flat = buf.reshape(-1)
    out, off = {}, 0
    for name, n in _SMALL:
        size = 1
        for s in shapes[name]:
            size *= s
        out[name] = flat[off:off + size].reshape(shapes[name])
        off += n
    return out


WEIGHT_ORDER = ('g_mix', 'w_in', 'lb_logits', 'hg_norm_g', 'attn_sinks', 'w_out', 'g_ffn', 'w_up', 'conv_w',
                'conv_b', 'w_down', 'g_ple', 'w_ple_gate', 'w_ple_up', 'g_final')


def kernel(x, p, g_mix, w_in, lb_logits, hg_norm_g, attn_sinks, w_out, g_ffn, w_up, conv_w, conv_b, w_down, g_ple, w_ple_gate, w_ple_up, g_final, loss_target, m_g_mix, m_w_in, m_lb_logits, m_hg_norm_g, m_attn_sinks, m_w_out, m_g_ffn, m_w_up, m_conv_w, m_conv_b, m_w_down, m_g_ple, m_w_ple_gate, m_w_ple_up, m_g_final, v_g_mix, v_w_in, v_lb_logits, v_hg_norm_g, v_attn_sinks, v_w_out, v_g_ffn, v_w_up, v_conv_w, v_conv_b, v_w_down, v_g_ple, v_w_ple_gate, v_w_ple_up, v_g_final):
    W = dict(g_mix=g_mix, w_in=w_in, lb_logits=lb_logits, hg_norm_g=hg_norm_g, attn_sinks=attn_sinks,
             w_out=w_out, g_ffn=g_ffn, w_up=w_up, conv_w=conv_w, conv_b=conv_b, w_down=w_down, g_ple=g_ple,
             w_ple_gate=w_ple_gate, w_ple_up=w_ple_up, g_final=g_final)
    M = dict(g_mix=m_g_mix, w_in=m_w_in, lb_logits=m_lb_logits, hg_norm_g=m_hg_norm_g, attn_sinks=m_attn_sinks,
             w_out=m_w_out, g_ffn=m_g_ffn, w_up=m_w_up, conv_w=m_conv_w, conv_b=m_conv_b, w_down=m_w_down,
             g_ple=m_g_ple, w_ple_gate=m_w_ple_gate, w_ple_up=m_w_ple_up, g_final=m_g_final)
    V = dict(g_mix=v_g_mix, w_in=v_w_in, lb_logits=v_lb_logits, hg_norm_g=v_hg_norm_g, attn_sinks=v_attn_sinks,
             w_out=v_w_out, g_ffn=v_g_ffn, w_up=v_w_up, conv_w=v_conv_w, conv_b=v_conv_b, w_down=v_w_down,
             g_ple=v_g_ple, w_ple_gate=v_w_ple_gate, w_ple_up=v_w_ple_up, g_final=v_g_final)
    S = x.shape[1]
    hg_rows = min(ROW_TILE, S)
    xi, yi, ci = _place()
    chip = 2 * xi + yi

    slab = _pack_shards({n: W[n] for n, _ in _PACK}).astype(BF16).reshape(DEPTH, LAYER_ROWS, D_MODEL)
    gathered = weights_allgather(slab[0])
    cw_shard = jnp.pad(conv_w.reshape(-1), (0, 72 * 128 - conv_w.size)).reshape(72, 128)
    cw_all = small_allgather(cw_shard)
    cw_full = jnp.concatenate(
        [cw_all[2 * k].reshape(-1)[:conv_w.size].reshape(conv_w.shape) for k in range(N_CHIPS)], axis=2)
    lbrows = lb_fwd(lb_logits)
    at_mask = swa_mask()

    h = x[0]
    saved = []
    for i in range(DEPTH):
        wf = _full_from_chips(gathered, 0)
        lbr = lbrows[8 * i:8 * i + 8]
        ng = hg_norm_g[i][None]
        sinks_b = jnp.pad(jnp.repeat(attn_sinks[i].reshape(AT_KV_HEADS, 1, AT_GROUP), WINDOW, axis=2),
                          ((0, 0), (0, 7), (0, 0))).reshape(8 * AT_KV_HEADS, GROUP_LANES)
        cw8 = jnp.pad(cw_full[i], ((0, 5), (0, 0)))
        cb = conv_b[i][None]
        if i == 0:
            u = rmsnorm_fwd(h, g_mix[0][None], name="rmsnorm_fwd")
        proj = mm(u, wf["w_in"], name="mm_in")
        if i + 1 < DEPTH:
            y, o_raw, states, gathered = hgrn_fwd(proj, lbr, ng, D_MODEL, rows=hg_rows, gather=slab[i + 1])
        else:
            y, o_raw, states = hgrn_fwd(proj, lbr, ng, D_MODEL, rows=hg_rows)
        y = swa_fwd(proj, sinks_b, at_mask, y)
        h1, u2 = mm(y, wf["w_out"], res=h, norm_g=g_ffn[i][None], name="mm_out")
        hh = mm(u2, wf["w_up"], out_dtype=BF16, name="mm_up")
        act = convffn_fwd(hh, cw8, cb)
        h2, u3 = mm(act, wf["w_down"], res=h1, norm_g=g_ple[i][None], name="mm_down")
        gpre = mm(u3, wf["w_ple_gate"], name="mm_gate")
        next_g = g_mix[i + 1] if i + 1 < DEPTH else g_final
        h3, u_next = ple_fwd(h2, gpre, p[i, 0], wf["w_ple_up"], next_g[None])
        saved.append(dict(wf=wf, lbr=lbr, ng=ng, sinks_b=sinks_b, cw8=cw8, cb=cb, h=h, u=u, proj=proj,
                          o_raw=o_raw, states=states, y=y, h1=h1, u2=u2, hh=hh, act=act, h2=h2, u3=u3,
                          gpre=gpre))
        h, u = h3, u_next

    dh, dhb, loss_acc, dg_final = loss_head(h, g_final[None], loss_target[0])

    gfull = {n: [None] * DEPTH for n, _ in _PACK}
    gsmall = {n: [None] * DEPTH for n in ("g_mix", "hg_norm_g", "attn_sinks", "g_ffn", "conv_w", "conv_b", "g_ple")}
    dlbrows = [None] * DEPTH
    half_rows = LAYER_ROWS // 2
    from_chips = [None] * DEPTH
    pending = None
    for i in reversed(range(DEPTH)):
        s = saved[i]
        wf = s["wf"]
        dpu, dgp = ple_bwd(dh, s["gpre"], p[i, 0], wf["w_ple_up"])
        gfull["w_ple_up"][i] = mm_tn(p[i, 0], dpu, name="mm_tn_pu")
        gfull["w_ple_gate"][i] = mm_tn(s["u3"], dgp, name="mm_tn_gate")
        dh2, dh2b, dg = mm(dgp, wf["w_ple_gate"], nt=True, rms_bwd=(s["h2"], g_ple[i][None], dh),
                           name="mm_nt_gate")
        gsmall["g_ple"][i] = dg[0]
        gfull["w_down"][i] = mm_tn(s["act"], dh2b, name="mm_tn_down")
        dact = mm(dh2b, wf["w_down"], nt=True, out_dtype=BF16, name="mm_nt_down")
        da, db, dcw = convffn_bwd(s["hh"], dact, s["cw8"], s["cb"])
        gsmall["conv_w"][i] = dcw[0:3]
        gsmall["conv_b"][i] = dcw[3]
        gfull["w_up"][i] = jnp.concatenate([mm_tn(s["u2"], da, name="mm_tn_up"),
                                            mm_tn(s["u2"], db, name="mm_tn_up")], axis=1)
        dh1, dh1b, dg = mm((da, db), wf["w_up"], nt=True, rms_bwd=(s["h1"], g_ffn[i][None], dh2),
                           name="mm_nt_up")
        gsmall["g_ffn"][i] = dg[0]
        gfull["w_out"][i] = mm_tn(s["y"], dh1b, name="mm_tn_out")
        dy = mm(dh1b, wf["w_out"], nt=True, name="mm_nt_out")
        dq_at, dko, dkp, dvo, dvp, dsk = swa_bwd(s["proj"], s["sinks_b"], at_mask, dy)
        gsmall["attn_sinks"][i] = dsk.reshape(AT_KV_HEADS, 8, GROUP_LANES)[:, 1:1 + AT_GROUP, 0].reshape(-1)
        hg_args = (s["proj"], s["o_raw"], s["states"], dy, s["lbr"], s["ng"])
        if pending is None:
            hq, hz, hv, hgp, dlbr, dng = hgrn_bwd(*hg_args, rows=hg_rows)
        else:
            hq, hz, hv, hgp, dlbr, dng, got = hgrn_bwd(*hg_args, rows=hg_rows, exchange=pending[1])
            from_chips[pending[0]] = got
        dlbrows[i] = dlbr
        gsmall["hg_norm_g"][i] = dng[0]
        dproj = assemble_dproj((hq, hz, hv, hgp), dq_at, dko, dkp, dvo, dvp, rows=hg_rows)
        gfull["w_in"][i] = mm_tn(s["u"], dproj, name="mm_tn_in")
        dh, dhb, dg = mm(dproj, wf["w_in"], nt=True, rms_bwd=(s["h"], g_mix[i][None], dh1), name="mm_nt_in")
        gsmall["g_mix"][i] = dg[0]
        pk = jnp.concatenate([_split_to_chips(gfull[name][i], name) for name, _ in _PACK], axis=1).astype(BF16)
        pk = pk.reshape(N_CHIPS, 2, half_rows, D_MODEL)
        p_mine = lax.dynamic_index_in_dim(pk, ci, axis=1, keepdims=False)
        p_other = lax.dynamic_index_in_dim(pk, 1 - ci, axis=1, keepdims=False)
        from_sib = sibling_swap(p_other, name="sibling_swap_partials")
        pair = sum_slots(jnp.stack([p_mine.reshape(-1, D_MODEL), from_sib.reshape(-1, D_MODEL)]),
                         out_dtype=BF16, name="sum_pair")
        pending = (i, pair.reshape(N_CHIPS, half_rows, D_MODEL))
    from_chips[0] = chip_exchange(pending[1])
    grad_x = dh[None]
    dlb_logits = lb_bwd(jnp.concatenate(dlbrows, axis=0), lb_logits)[0:DEPTH]

    mine_sum = jnp.concatenate([sum_slots(from_chips[i], out_dtype=F32, name="sum_chips") for i in range(DEPTH)],
                               axis=0)
    sib_sum = sibling_swap(mine_sum, name="sibling_swap_sums")
    mine_sum = mine_sum.reshape(DEPTH, half_rows, D_MODEL)
    sib_sum = sib_sum.reshape(DEPTH, half_rows, D_MODEL)
    lo = jnp.where(ci == 0, mine_sum, sib_sum)
    hi = jnp.where(ci == 0, sib_sum, mine_sum)
    gshard = _unpack_shards(jnp.concatenate([lo, hi], axis=1).reshape(PACK_ROWS, D_MODEL))

    small = dict(loss=loss_acc[0, 0:1], g_final=dg_final[0], lb_logits=dlb_logits,
                 **{n: jnp.stack(v) for n, v in gsmall.items()})
    small_sum = sum_slots(small_allgather(_pack_small(small)), out_dtype=F32, name="sum_small")
    shapes = {n: W[n].shape for n in W}
    shapes["loss"] = (1,)
    shapes["conv_w"] = (DEPTH, 3, D_FF)
    gs = _unpack_small(small_sum, shapes)
    loss = gs["loss"][0]
    cshard = conv_w.shape[2]
    grads = dict(gshard)
    for n in ("g_mix", "lb_logits", "hg_norm_g", "attn_sinks", "g_ffn", "conv_b", "g_ple", "g_final"):
        grads[n] = gs[n]
    grads["conv_w"] = lax.dynamic_slice_in_dim(gs["conv_w"], chip * cshard, cshard, axis=2)

    delta, new_m, new_v = {}, {}, {}
    small_names = ("g_final", "g_mix", "lb_logits", "hg_norm_g", "attn_sinks", "g_ffn", "conv_b", "g_ple")
    sshapes = {n: W[n].shape for n in small_names}

    def pack_s(d):
        z = dict(d)
        z["loss"] = jnp.zeros((1,), F32)
        z["conv_w"] = jnp.zeros((1,), F32)
        return _pack_small(z)

    sd, sm, sv = adamw(pack_s(W), pack_s(grads), pack_s(M), pack_s(V), name="adamw_small")
    for out, buf in ((delta, sd), (new_m, sm), (new_v, sv)):
        un = _unpack_small(buf, {**sshapes, "loss": (1,), "conv_w": (1,)})
        for n in small_names:
            out[n] = un[n]
    for n in ("w_in", "w_out", "w_up", "w_down", "w_ple_gate", "w_ple_up", "conv_w"):
        shp = W[n].shape
        two_d = (-1, shp[-1])
        d_, m_, v_ = adamw(W[n].reshape(two_d), grads[n].reshape(two_d), M[n].reshape(two_d),
                           V[n].reshape(two_d), name="adamw_" + n)
        delta[n], new_m[n], new_v[n] = d_.reshape(shp), m_.reshape(shp), v_.reshape(shp)

    return (loss, grad_x, *[grads[n] for n in WEIGHT_ORDER], *[delta[n] for n in WEIGHT_ORDER],
            *[new_m[n] for n in WEIGHT_ORDER], *[new_v[n] for n in WEIGHT_ORDER])
```

```python
import functools

import jax
import jax.numpy as jnp
from jax import lax
from jax.experimental import pallas as pl
from jax.experimental.pallas import tpu as pltpu

F32 = jnp.float32
BF16 = jnp.bfloat16

D_MODEL = 1024
DEPTH = 4
PLE_DIM = 256
HG_WIDTH = 512
HG_HEADS = 4
HG_DK = 128
HG_CHUNK = 64
HG_SUB = 16
AT_WIDTH = 512
AT_HEAD_DIM = 64
AT_Q_HEADS = 8
AT_KV_HEADS = 2
AT_GROUP = 4
WINDOW = 128
D_FF = 2816
IN_WIDTH = 2816
EPS = 1e-6
MASK_VALUE = -1e30
LB_FLOOR = 1e-30

ADAM_LR = 0.001
ADAM_B1 = 0.9
ADAM_B2 = 0.999
ADAM_EPS = 1e-08
ADAM_WD = 0.01
ADAM_STEP = 10

VMEM_LIMIT = 48 * 1024 * 1024


def _params(*sem):
    return pltpu.CompilerParams(dimension_semantics=sem, vmem_limit_bytes=VMEM_LIMIT)


def _dot(a, b, dims=(((1,), (0,)), ((), ()))):
    return lax.dot_general(a.astype(BF16), b.astype(BF16), dims, preferred_element_type=F32)


def _dot_nt(a, b):
    return _dot(a, b, (((1,), (1,)), ((), ())))


def _dot_tn(a, b):
    return _dot(a, b, (((0,), (0,)), ((), ())))


def _dot_exact(sel, x, dims=(((1,), (0,)), ((), ()))):
    hi = x.astype(BF16)
    r1 = x - hi.astype(F32)
    mid = r1.astype(BF16)
    lo = (r1 - mid.astype(F32)).astype(BF16)
    s = sel.astype(BF16)
    one = lambda p: lax.dot_general(s, p, dims, preferred_element_type=F32)
    return one(hi) + one(mid) + one(lo)


def _sigmoid(x):
    return 0.5 * jnp.tanh(0.5 * x) + 0.5


def _logsig(x):
    return jnp.minimum(x, 0.0) - jnp.log(1.0 + jnp.exp(-jnp.abs(x)))


def _colsum(x):
    return jnp.sum(x, axis=0, keepdims=True)


def _rowsum(x):
    return jnp.sum(x, axis=1, keepdims=True)


def _colsum8(xs):
    row = lax.broadcasted_iota(jnp.int32, xs[0].shape, 0)

    def merge(a, b, keep_a, step):
        return jnp.where(keep_a, a + pltpu.roll(a, 8 - step, 0), b + pltpu.roll(b, step, 0))

    c = [merge(xs[j], xs[j + 4], row < 4, 4) for j in range(4)]
    d = [merge(c[j], c[j + 2], (row & 3) < 2, 2) for j in range(2)]
    return merge(d[0], d[1], (row & 1) == 0, 1)


def _tri(n):
    r = lax.broadcasted_iota(jnp.int32, (n, n), 0)
    c = lax.broadcasted_iota(jnp.int32, (n, n), 1)
    return (r >= c).astype(F32)


def _hg_gates(qp, z, a, c, oml):
    sq = _sigmoid(qp)
    q = qp * sq
    t = c + _logsig(z)
    mx = jnp.maximum(a, t)
    logf = mx + jnp.log(1.0 + jnp.exp(-jnp.abs(a - t)))
    snz = _sigmoid(-z)
    k = oml * snz
    return q, sq, t, logf, snz, k


_HEADS = range(HG_HEADS)


def _lanes(h):
    return slice(h * HG_DK, (h + 1) * HG_DK)


def _head(x, h):
    return x[:, _lanes(h)]


def _row_masks():
    row8 = lax.broadcasted_iota(jnp.int32, (8, HG_DK), 0)
    return [None] + [jnp.where(row8 >= j, 0.0, MASK_VALUE) for j in range(1, 8)]


def _hg_chunk_fwd(q, k, v, logf, st, b_s, k_s, v_s):
    C, U = HG_CHUNK, HG_SUB
    tri = _tri(C)
    b = [_dot_exact(tri, logf[h]) for h in _HEADS]
    for h in _HEADS:
        b_s[h] = b[h]
        k_s[h] = k[h]
        v_s[h] = v[h]
    o = [_dot_nt(q[h] * jnp.exp(b[h]), st[h]) for h in _HEADS]
    bl = [b[h][C - 1:C] for h in _HEADS]
    upd = [_dot_tn(v[h], k[h] * jnp.exp(bl[h] - b[h])) for h in _HEADS]
    rows = lax.broadcasted_iota(jnp.int32, (C, HG_DK), 0)
    nmask = _row_masks()
    outs = [[] for _ in _HEADS]
    for i in range(C // U):
        lo = i * U
        b_i = [b[h][lo:lo + U] for h in _HEADS]
        q_i = [q[h][lo:lo + U] for h in _HEADS]
        o_i = [o[h][lo:lo + U] for h in _HEADS]
        if i > 0:
            qe = [q_i[h] * jnp.exp(b_i[h] - b_i[h][0:1]) for h in _HEADS]
            ke = [jnp.where(rows < lo, k[h] * jnp.exp(jnp.minimum(b_i[h][0:1] - b[h], 0.0)), 0.0) for h in _HEADS]
            att = [_dot_nt(qe[h], ke[h]) for h in _HEADS]
            off = [_dot(att[h], v[h]) for h in _HEADS]
            o_i = [o_i[h] + off[h] for h in _HEADS]
        pieces = [[o_i[h][8 * f:8 * f + 8] for f in range(U // 8)] for h in _HEADS]
        for s in range(U):
            for f in range(s // 8, U // 8):
                for h in _HEADS:
                    bs = b_s[h, lo + s:lo + s + 1, :]
                    ks = k_s[h, lo + s:lo + s + 1, :]
                    vs = v_s[h, lo + s:lo + s + 1, :]
                    arg = b_i[h][8 * f:8 * f + 8] - bs
                    if s > 8 * f:
                        arg = arg + nmask[s - 8 * f]
                    w = _rowsum(q_i[h][8 * f:8 * f + 8] * jnp.exp(arg) * ks)
                    pieces[h][f] = pieces[h][f] + w * vs
        for h in _HEADS:
            outs[h] += pieces[h]
    o = [jnp.concatenate(outs[h], axis=0) for h in _HEADS]
    st_new = [st[h] * jnp.exp(bl[h]) + upd[h] for h in _HEADS]
    return o, st_new, b


def _hg_post(o, gp, ng):
    rs = lax.rsqrt(jnp.mean(o * o, axis=1, keepdims=True) + EPS)
    sg = _sigmoid(gp)
    return o * rs * ng * sg, rs, sg


def hgrn_fwd(proj, lbrows, ng, y_width, *, rows, gather=None):
    S = proj.shape[0]
    C = HG_CHUNK
    cpb = rows // C
    nb = S // rows

    def body(qp_ref, z_ref, v_ref, gp_ref, lb_ref, ng_ref, *rest):
        if gather is not None:
            w_ref, y_ref, o_ref, st_ref, g_ref, st, b_s, k_s, v_s, *sems = rest
            comm = _Gather(w_ref, g_ref, *sems)
        else:
            y_ref, o_ref, st_ref, st, b_s, k_s, v_s = rest

        @pl.when(pl.program_id(0) == 0)
        def _():
            st[...] = jnp.zeros_like(st)
            if gather is not None:
                comm.start()

        a, c, oml = lb_ref[0:1, :], lb_ref[1:2, :], lb_ref[2:3, :]
        ngr = ng_ref[...]

        def chunk(ci, carry):
            off = pl.multiple_of(ci * C, C)
            sl = pl.ds(off, C)
            for h in _HEADS:
                st_ref[h, ci] = st[h]
            gates = [_hg_gates(qp_ref[sl, _lanes(h)], z_ref[sl, _lanes(h)],
                               _head(a, h), _head(c, h), _head(oml, h)) for h in _HEADS]
            q = [g[0] for g in gates]
            logf = [g[3] for g in gates]
            k = [g[5] for g in gates]
            v = [v_ref[sl, _lanes(h)] for h in _HEADS]
            o, st_new, _ = _hg_chunk_fwd(q, k, v, logf, [st[h] for h in _HEADS], b_s, k_s, v_s)
            for h in _HEADS:
                y, _, _ = _hg_post(o[h], gp_ref[sl, _lanes(h)], _head(ngr, h))
                y_ref[sl, _lanes(h)] = y.astype(y_ref.dtype)
                o_ref[sl, _lanes(h)] = o[h]
                st[h] = st_new[h]
            return carry

        lax.fori_loop(0, cpb, chunk, 0)

        if gather is not None:
            pl.when(pl.program_id(0) == nb - 1)(comm.finish)

    col = lambda kblk: pl.BlockSpec((rows, HG_WIDTH), lambda r: (r, kblk))
    in_specs = [col(0), col(1), col(2), col(3), _const_spec((8, HG_WIDTH)), _const_spec((1, HG_WIDTH))]
    out_specs = [col(0), col(0), pl.BlockSpec((HG_HEADS, cpb, HG_DK, HG_DK), lambda r: (0, r, 0, 0))]
    out_shape = [jax.ShapeDtypeStruct((S, y_width), BF16),
                 jax.ShapeDtypeStruct((S, HG_WIDTH), F32),
                 jax.ShapeDtypeStruct((HG_HEADS, S // C, HG_DK, HG_DK), F32)]
    scratch = [pltpu.VMEM((HG_HEADS, HG_DK, HG_DK), F32)] + [pltpu.VMEM((HG_HEADS, C, HG_DK), F32)] * 3
    args = [proj, proj, proj, proj, lbrows, ng]
    if gather is not None:
        in_specs.append(ANY)
        out_specs.append(ANY)
        out_shape.append(jax.ShapeDtypeStruct((N_CHIPS,) + gather.shape, gather.dtype))
        scratch += _Gather.SCRATCH
        args.append(gather)
    return pl.pallas_call(
        body,
        name="hgrn_fwd" if gather is None else "hgrn_fwd_gather",
        grid=(nb,),
        in_specs=in_specs,
        out_specs=out_specs,
        out_shape=out_shape,
        scratch_shapes=scratch,
        compiler_params=_params("arbitrary"),
    )(*args)


def hgrn_bwd(proj, o_raw, states, dy, lbrows, ng, *, rows, exchange=None):
    S = proj.shape[0]
    C, U = HG_CHUNK, HG_SUB
    cpb = rows // C
    nb = S // rows

    def body(qp_ref, z_ref, v_ref, gp_ref, o_ref, st_ref, dy_ref, lb_ref, ng_ref, *rest):
        if exchange is not None:
            (q_ref, dqp_ref, dz_ref, dv_ref, dgp_ref, dlb_ref, dng_ref, r_ref,
             dst, b_s, k_s, v_s, dbs, dks, dvs, *sems) = rest
            comm = _Exchange(q_ref, r_ref, *sems)
        else:
            dqp_ref, dz_ref, dv_ref, dgp_ref, dlb_ref, dng_ref, dst, b_s, k_s, v_s, dbs, dks, dvs = rest

        @pl.when(pl.program_id(0) == 0)
        def _():
            dst[...] = jnp.zeros_like(dst)
            dlb_ref[...] = jnp.zeros_like(dlb_ref)
            dng_ref[...] = jnp.zeros_like(dng_ref)
            if exchange is not None:
                comm.start()

        a, c, oml = lb_ref[0:1, :], lb_ref[1:2, :], lb_ref[2:3, :]
        ngr = ng_ref[...]
        rows_i = lax.broadcasted_iota(jnp.int32, (C, HG_DK), 0)
        nmask = _row_masks()
        tri = _tri(C)
        H = _HEADS

        def chunk(cj, carry):
            ci = cpb - 1 - cj
            off = pl.multiple_of(ci * C, C)
            sl = pl.ds(off, C)
            qp = [qp_ref[sl, _lanes(h)] for h in H]
            v = [v_ref[sl, _lanes(h)] for h in H]
            st = [st_ref[h, ci] for h in H]
            gates = [_hg_gates(qp[h], z_ref[sl, _lanes(h)], _head(a, h), _head(c, h), _head(oml, h)) for h in H]
            q, sq, t, logf, snz, k = ([g[j] for g in gates] for j in range(6))
            b = [_dot_exact(tri, logf[h]) for h in H]
            for h in H:
                b_s[h] = b[h]
                k_s[h] = k[h]
                v_s[h] = v[h]
            do = []
            for h in H:
                o = o_ref[sl, _lanes(h)]
                dyv = dy_ref[sl, _lanes(h)]
                ngh = _head(ngr, h)
                rs = lax.rsqrt(jnp.mean(o * o, axis=1, keepdims=True) + EPS)
                sg = _sigmoid(gp_ref[sl, _lanes(h)])
                xh = o * rs
                dgp_ref[sl, _lanes(h)] = (dyv * xh * ngh * sg * (1.0 - sg)).astype(dgp_ref.dtype)
                don = dyv * sg
                dng_ref[0:1, _lanes(h)] += _colsum(don * xh)
                dxh = don * ngh
                do.append(rs * (dxh - xh * jnp.mean(dxh * xh, axis=1, keepdims=True)))
            eb = [jnp.exp(b[h]) for h in H]
            qb = [q[h] * eb[h] for h in H]
            dstv = [dst[h] for h in H]
            bl = [b[h][C - 1:C] for h in H]
            el = [jnp.exp(bl[h]) for h in H]
            ex = [jnp.exp(bl[h] - b[h]) for h in H]
            kd = [k[h] * ex[h] for h in H]
            dqb = [_dot(do[h], st[h]) for h in H]
            dst_acc = [_dot_tn(do[h], qb[h]) for h in H]
            dv0 = [_dot_nt(kd[h], dstv[h]) for h in H]
            dkd = [_dot(v[h], dstv[h]) for h in H]
            dq = [dqb[h] * eb[h] for h in H]
            for h in H:
                g2 = dkd[h] * kd[h]
                dbl = _colsum(dstv[h] * st[h]) * el[h] + _colsum(g2)
                dst[h] = dstv[h] * el[h] + dst_acc[h]
                dbs[h] = dqb[h] * qb[h] - g2
                dks[h] = dkd[h] * ex[h]
                dvs[h] = dv0[h]
                dbs[h, C - 1:C, :] += dbl
            dq_parts = [[] for _ in H]
            for i in range(C // U):
                lo = i * U
                b_i = [b[h][lo:lo + U] for h in H]
                q_i = [q[h][lo:lo + U] for h in H]
                do_i = [do[h][lo:lo + U] for h in H]
                dq_i = [dq[h][lo:lo + U] for h in H]
                db_i = [jnp.zeros((U, HG_DK), F32) for _ in H]
                if i > 0:
                    e1 = [jnp.exp(b_i[h] - b_i[h][0:1]) for h in H]
                    qe = [q_i[h] * e1[h] for h in H]
                    e2 = [jnp.where(rows_i < lo, jnp.exp(jnp.minimum(b_i[h][0:1] - b[h], 0.0)), 0.0) for h in H]
                    ke = [k[h] * e2[h] for h in H]
                    att = [_dot_nt(qe[h], ke[h]) for h in H]
                    datt = [_dot_nt(do_i[h], v[h]) for h in H]
                    dv_add = [_dot_tn(att[h], do_i[h]) for h in H]
                    dqe = [_dot(datt[h], ke[h]) for h in H]
                    dke = [_dot_tn(datt[h], qe[h]) for h in H]
                    for h in H:
                        dvs[h] += dv_add[h]
                        dq_i[h] = dq_i[h] + dqe[h] * e1[h]
                        g = dqe[h] * qe[h]
                        db_i[h] = db_i[h] + g
                        gk = dke[h] * ke[h]
                        dks[h] += dke[h] * e2[h]
                        dbs[h] -= gk
                        dbs[h, lo:lo + 1, :] += _colsum(gk) - _colsum(g)
                nf = U // 8
                dq8 = [[dq_i[h][8 * f:8 * f + 8] for f in range(nf)] for h in H]
                db8 = [[db_i[h][8 * f:8 * f + 8] for f in range(nf)] for h in H]
                key_v = [[] for _ in H]
                key_k = [[] for _ in H]
                key_b = [[] for _ in H]
                for s in range(U):
                    row = slice(lo + s, lo + s + 1)
                    for h in H:
                        bs = b_s[h, row, :]
                        ks = k_s[h, row, :]
                        vs = v_s[h, row, :]
                        tv = tk = tb = None
                        for f in range(s // 8, nf):
                            p8 = slice(8 * f, 8 * f + 8)
                            arg = b_i[h][p8] - bs
                            if s > 8 * f:
                                arg = arg + nmask[s - 8 * f]
                            dec = jnp.exp(arg)
                            qd = q_i[h][p8] * dec
                            y_ = qd * ks
                            w = _rowsum(y_)
                            dw = _rowsum(do_i[h][p8] * vs)
                            g = dw * y_
                            dq8[h][f] = dq8[h][f] + dw * dec * ks
                            db8[h][f] = db8[h][f] + g
                            cv, ck = w * do_i[h][p8], dw * qd
                            tv, tk, tb = (cv, ck, g) if tv is None else (tv + cv, tk + ck, tb + g)
                        key_v[h].append(tv)
                        key_k[h].append(tk)
                        key_b[h].append(tb)
                for h in H:
                    for f in range(nf):
                        r8 = slice(lo + 8 * f, lo + 8 * f + 8)
                        dvs[h, r8, :] += _colsum8(key_v[h][8 * f:8 * f + 8])
                        dks[h, r8, :] += _colsum8(key_k[h][8 * f:8 * f + 8])
                        dbs[h, r8, :] += db8[h][f] - _colsum8(key_b[h][8 * f:8 * f + 8])
                    dq_parts[h] += dq8[h]
            dlogf = [_dot_exact(tri, dbs[h], (((0,), (0,)), ((), ()))) for h in H]
            for h in H:
                dqh = jnp.concatenate(dq_parts[h], axis=0)
                dk = dks[h]
                ah, omlh = _head(a, h), _head(oml, h)
                pa = jnp.exp(ah - logf[h])
                pt = jnp.exp(t[h] - logf[h])
                dt = dlogf[h] * pt
                dlb_ref[0:1, _lanes(h)] += _colsum(dlogf[h] * pa)
                dlb_ref[1:2, _lanes(h)] += _colsum(dt)
                dlb_ref[2:3, _lanes(h)] += _colsum(dk * snz[h])
                dz = dt * snz[h] - dk * omlh * snz[h] * (1.0 - snz[h])
                dqp = dqh * (sq[h] + qp[h] * sq[h] * (1.0 - sq[h]))
                dqp_ref[sl, _lanes(h)] = dqp.astype(dqp_ref.dtype)
                dz_ref[sl, _lanes(h)] = dz.astype(dz_ref.dtype)
                dv_ref[sl, _lanes(h)] = dvs[h].astype(dv_ref.dtype)
            return carry

        lax.fori_loop(0, cpb, chunk, 0)

        if exchange is not None:
            pl.when(pl.program_id(0) == nb - 1)(comm.finish)

    rev = lambda r: nb - 1 - r
    col = lambda kblk: pl.BlockSpec((rows, HG_WIDTH), lambda r: (rev(r), kblk))
    acc = _const_spec((8, HG_WIDTH))
    in_specs = [col(0), col(1), col(2), col(3), col(0),
                pl.BlockSpec((HG_HEADS, cpb, HG_DK, HG_DK), lambda r: (0, rev(r), 0, 0)),
                col(0), acc, _const_spec((1, HG_WIDTH))]
    out_specs = [col(0)] * 4 + [acc, acc]
    out_shape = [jax.ShapeDtypeStruct((S, HG_WIDTH), BF16)] * 4 + [jax.ShapeDtypeStruct((8, HG_WIDTH), F32)] * 2
    scratch = [pltpu.VMEM((HG_HEADS, HG_DK, HG_DK), F32)] + [pltpu.VMEM((HG_HEADS, C, HG_DK), F32)] * 6
    args = [proj, proj, proj, proj, o_raw, states, dy, lbrows, ng]
    if exchange is not None:
        in_specs.append(ANY)
        out_specs.append(ANY)
        out_shape.append(jax.ShapeDtypeStruct(exchange.shape, exchange.dtype))
        scratch += _Exchange.SCRATCH
        args.append(exchange)
    return pl.pallas_call(
        body,
        name="hgrn_bwd" if exchange is None else "hgrn_bwd_exchange",
        grid=(nb,),
        in_specs=in_specs,
        out_specs=out_specs,
        out_shape=out_shape,
        scratch_shapes=scratch,
        compiler_params=_params("arbitrary"),
    )(*args)


GROUP_LANES = AT_GROUP * WINDOW


def swa_mask():
    W = WINDOW
    kpos = lax.broadcasted_iota(jnp.int32, (2, 2 * W, GROUP_LANES), 1)
    qpos = (lax.broadcasted_iota(jnp.int32, (2, 2 * W, GROUP_LANES), 2) & (W - 1)) + W
    first = lax.broadcasted_iota(jnp.int32, (2, 2 * W, GROUP_LANES), 0) == 0
    rel = qpos - kpos
    valid = (rel >= 0) & (rel < W) & jnp.logical_not(first & (kpos < W))
    return jnp.where(valid, 0.0, MASK_VALUE).astype(F32)


def _group_lanes(xt, g):
    Dh = AT_HEAD_DIM
    return jnp.concatenate([xt[(g * AT_GROUP + j) * Dh:(g * AT_GROUP + j + 1) * Dh] for j in range(AT_GROUP)],
                           axis=1)


SWA_SCALE = AT_HEAD_DIM ** -0.5


def _swa_softmax_t(s, sink_row):
    m = jnp.maximum(jnp.max(s, axis=0, keepdims=True), sink_row)
    e = jnp.exp(s - m)
    es = jnp.exp(sink_row - m)
    inv = 1.0 / (_colsum(e) + es)
    return e * inv, es * inv


SWA_BLOCKS = 4
_BG = [(b, g) for b in range(SWA_BLOCKS) for g in range(AT_KV_HEADS)]


def _swa_specs(col_q):
    W = WINDOW
    rows = SWA_BLOCKS * W
    prev = lambda n: jnp.maximum(SWA_BLOCKS * n - 1, 0)
    return [pl.BlockSpec((rows, AT_WIDTH), lambda n: (n, col_q)),
            pl.BlockSpec((W, 128), lambda n: (prev(n), 20)),
            pl.BlockSpec((rows, 128), lambda n: (n, 20)),
            pl.BlockSpec((W, 128), lambda n: (prev(n), 21)),
            pl.BlockSpec((rows, 128), lambda n: (n, 21)),
            _const_spec((8 * AT_KV_HEADS, GROUP_LANES)),
            _const_spec((2, 2 * W, GROUP_LANES))]


def _swa_operands(n, q_ref, kp_ref, k_ref, vp_ref, v_ref, sk_ref, mask_ref):
    W, Dh = WINDOW, AT_HEAD_DIM
    k_all = jnp.concatenate([kp_ref[...], k_ref[...]], axis=0)
    v_all = jnp.concatenate([vp_ref[...], v_ref[...]], axis=0)
    kk = [k_all[b * W:(b + 2) * W] for b in range(SWA_BLOCKS)]
    vv = [v_all[b * W:(b + 2) * W] for b in range(SWA_BLOCKS)]
    masks = [mask_ref[jnp.minimum(n, 1)]] + [mask_ref[1]] * (SWA_BLOCKS - 1)
    qt = [(q_ref[b * W:(b + 1) * W, :] * SWA_SCALE).T for b in range(SWA_BLOCKS)]
    kg = {(b, g): kk[b][:, g * Dh:(g + 1) * Dh] for b, g in _BG}
    qg = {(b, g): _group_lanes(qt[b], g) for b, g in _BG}
    s = {bg: _dot(kg[bg], qg[bg]) + masks[bg[0]] for bg in _BG}
    sink = {(b, g): sk_ref[8 * g:8 * g + 1, :] for b, g in _BG}
    return kk, vv, kg, qg, s, sink


def swa_fwd(proj, sink_rows, mask, y):
    S = proj.shape[0]
    W, Dh = WINDOW, AT_HEAD_DIM
    rows = SWA_BLOCKS * W

    def body(q_ref, kp_ref, k_ref, vp_ref, v_ref, sk_ref, mask_ref, y_in, y_ref):
        del y_in
        _, vv, _, _, s, sink = _swa_operands(pl.program_id(0), q_ref, kp_ref, k_ref, vp_ref, v_ref,
                                             sk_ref, mask_ref)
        vt = [v.T for v in vv]
        p = {bg: _swa_softmax_t(s[bg], sink[bg])[0] for bg in _BG}
        ot = {(b, g): _dot(vt[b][g * Dh:(g + 1) * Dh], p[b, g]) for b, g in _BG}
        for b in range(SWA_BLOCKS):
            outs = [ot[b, g][:, j * W:(j + 1) * W] for g in range(AT_KV_HEADS) for j in range(AT_GROUP)]
            y_ref[b * W:(b + 1) * W, :] = jnp.concatenate(outs, axis=0).T.astype(y_ref.dtype)

    return pl.pallas_call(
        body,
        name="swa_fwd",
        grid=(S // rows,),
        in_specs=_swa_specs(4) + [pl.BlockSpec(memory_space=pl.ANY)],
        out_specs=pl.BlockSpec((rows, AT_WIDTH), lambda n: (n, 1)),
        out_shape=jax.ShapeDtypeStruct(y.shape, y.dtype),
        input_output_aliases={7: 0},
        compiler_params=_params("parallel"),
    )(proj, proj, proj, proj, proj, sink_rows, mask, y)


def swa_bwd(proj, sink_rows, mask, dy):
    S = proj.shape[0]
    W, Dh = WINDOW, AT_HEAD_DIM
    rows = SWA_BLOCKS * W
    nsteps = S // rows

    def body(q_ref, kp_ref, k_ref, vp_ref, v_ref, sk_ref, mask_ref, dy_ref,
             dq_ref, dko_ref, dkp_ref, dvo_ref, dvp_ref, dsk_ref):
        n = pl.program_id(0)

        @pl.when(n == 0)
        def _():
            dsk_ref[...] = jnp.zeros_like(dsk_ref)

        kk, vv, _, qg, s, sink = _swa_operands(n, q_ref, kp_ref, k_ref, vp_ref, v_ref, sk_ref, mask_ref)
        kt = [k.T for k in kk]
        dot_ = [dy_ref[b * W:(b + 1) * W, :].T for b in range(SWA_BLOCKS)]
        dog = {(b, g): _group_lanes(dot_[b], g) for b, g in _BG}
        dp = {(b, g): _dot(vv[b][:, g * Dh:(g + 1) * Dh], dog[b, g]) for b, g in _BG}
        pp = {bg: _swa_softmax_t(s[bg], sink[bg]) for bg in _BG}
        delta = {bg: _colsum(dp[bg] * pp[bg][0]) for bg in _BG}
        ds = {bg: pp[bg][0] * (dp[bg] - delta[bg]) for bg in _BG}
        dqt = {(b, g): _dot(kt[b][g * Dh:(g + 1) * Dh], ds[b, g]) * SWA_SCALE for b, g in _BG}
        dk = {bg: _dot_nt(ds[bg], qg[bg]) for bg in _BG}
        dv = {bg: _dot_nt(pp[bg][0], dog[bg]) for bg in _BG}
        for g in range(AT_KV_HEADS):
            tot = -(pp[0, g][1] * delta[0, g])
            for b in range(1, SWA_BLOCKS):
                tot = tot - pp[b, g][1] * delta[b, g]
            dsk_ref[8 * g:8 * g + 1, :] += tot
        for b in range(SWA_BLOCKS):
            r = slice(b * W, (b + 1) * W)
            dqs = [dqt[b, g][:, j * W:(j + 1) * W] for g in range(AT_KV_HEADS) for j in range(AT_GROUP)]
            dq_ref[r, :] = jnp.concatenate(dqs, axis=0).T.astype(dq_ref.dtype)
            dkb = jnp.concatenate([dk[b, g] for g in range(AT_KV_HEADS)], axis=1)
            dvb = jnp.concatenate([dv[b, g] for g in range(AT_KV_HEADS)], axis=1)
            dkp_ref[r, :] = dkb[:W]
            dko_ref[r, :] = dkb[W:]
            dvp_ref[r, :] = dvb[:W]
            dvo_ref[r, :] = dvb[W:]

        @pl.when(n == nsteps - 1)
        def _():
            for g in range(AT_KV_HEADS):
                for j in range(AT_GROUP):
                    tot = _rowsum(dsk_ref[8 * g:8 * g + 1, j * W:(j + 1) * W])
                    dsk_ref[8 * g + 1 + j:8 * g + 2 + j, :] = jnp.broadcast_to(tot, (1, GROUP_LANES))

    kv = pl.BlockSpec((rows, 128), lambda n: (n, 0))
    sk = _const_spec((8 * AT_KV_HEADS, GROUP_LANES))
    return pl.pallas_call(
        body,
        name="swa_bwd",
        grid=(nsteps,),
        in_specs=_swa_specs(4) + [pl.BlockSpec((rows, AT_WIDTH), lambda n: (n, 1))],
        out_specs=[pl.BlockSpec((rows, AT_WIDTH), lambda n: (n, 0)), kv, kv, kv, kv, sk],
        out_shape=[jax.ShapeDtypeStruct((S, AT_WIDTH), BF16)]
                  + [jax.ShapeDtypeStruct((S, 128), F32)] * 4
                  + [jax.ShapeDtypeStruct((8 * AT_KV_HEADS, GROUP_LANES), F32)],
        compiler_params=_params("arbitrary"),
    )(proj, proj, proj, proj, proj, sink_rows, mask, dy)


def assemble_dproj(hg_grads, dq_at, dko, dkp, dvo, dvp, *, rows):
    S = dq_at.shape[0]
    W = WINDOW
    nb = S // W
    bpr = rows // W

    def body(a0, a1, a2, a3, dq, ko, kp, kpn, vo, vp, vpn, out):
        r = pl.program_id(0)
        for i, a in enumerate((a0, a1, a2, a3)):
            out[:, i * HG_WIDTH:(i + 1) * HG_WIDTH] = a[...]
        base = 4 * HG_WIDTH
        out[:, base:base + AT_WIDTH] = dq[...]
        last = (r == pl.num_programs(0) - 1)
        for off, own, pv, pvn in ((base + AT_WIDTH, ko, kp, kpn), (base + AT_WIDTH + 128, vo, vp, vpn)):
            if bpr > 1:
                out[0:rows - W, off:off + 128] = (own[0:rows - W, :] + pv[W:rows, :]).astype(out.dtype)
            nxt = jnp.where(last, 0.0, pvn[...])
            out[rows - W:rows, off:off + 128] = (own[rows - W:rows, :] + nxt).astype(out.dtype)

    hg = pl.BlockSpec((rows, HG_WIDTH), lambda r: (r, 0))
    blk = pl.BlockSpec((rows, 128), lambda r: (r, 0))
    nxt = pl.BlockSpec((W, 128), lambda r: (jnp.minimum((r + 1) * bpr, nb - 1), 0))
    return pl.pallas_call(
        body,
        name="assemble_dproj",
        grid=(S // rows,),
        in_specs=[hg, hg, hg, hg, pl.BlockSpec((rows, AT_WIDTH), lambda r: (r, 0)),
                  blk, blk, nxt, blk, blk, nxt],
        out_specs=pl.BlockSpec((rows, IN_WIDTH), lambda r: (r, 0)),
        out_shape=jax.ShapeDtypeStruct((S, IN_WIDTH), BF16),
        compiler_params=_params("parallel"),
    )(*hg_grads, dq_at, dko, dkp, dkp, dvo, dvp, dvp)


ROW_TILE = 512
COL_TILE = 1408


def _col_tile(n):
    return n if n <= COL_TILE else COL_TILE


def _rms_scale(x):
    return lax.rsqrt(jnp.mean(x * x, axis=1, keepdims=True) + EPS)


def _rms_bwd(d, x, g):
    rs = _rms_scale(x)
    xh = x * rs
    dxh = d * g
    return rs * (dxh - xh * jnp.mean(dxh * xh, axis=1, keepdims=True)), _colsum(d * xh)


def mm(a, b, *, nt=False, out_dtype=F32, res=None, norm_g=None, rms_bwd=None, name):
    parts = a if isinstance(a, tuple) else (a,)
    M, K = parts[0].shape
    N = b.shape[0] if nt else b.shape[1]
    tall = K <= D_MODEL and rms_bwd is None and len(parts) == 1 and M % (2 * ROW_TILE) == 0
    tm = 2 * ROW_TILE if tall else min(ROW_TILE, M)
    tn = _col_tile(N)
    whole_rows = norm_g is not None or rms_bwd is not None
    assert M % tm == 0 and N % tn == 0 and (tn == N or not whole_rows)
    np_ = len(parts)

    def body(*refs):
        a_refs, b_refs, rest = refs[:np_], refs[np_:2 * np_], refs[2 * np_:]
        dot = _dot_nt if nt else _dot
        acc = dot(a_refs[0][...], b_refs[0][...])
        for ar, br in zip(a_refs[1:], b_refs[1:]):
            acc = acc + dot(ar[...], br[...])
        if rms_bwd is not None:
            h_ref, g_ref, dr_ref, dh_ref, dhb_ref, dg_ref = rest

            @pl.when(pl.program_id(1) == 0)
            def _():
                dg_ref[...] = jnp.zeros_like(dg_ref)

            dx, dgp = _rms_bwd(acc, h_ref[...], g_ref[...])
            dg_ref[0:1, :] += dgp
            dh = dr_ref[...] + dx
            dh_ref[...] = dh
            dhb_ref[...] = dh.astype(BF16)
            return
        rest = list(rest)
        if res is not None:
            acc = acc + rest.pop(0)[...]
        if norm_g is not None:
            g_ref = rest.pop(0)
            rest[1][...] = (acc * _rms_scale(acc) * g_ref[...]).astype(BF16)
        rest[0][...] = acc.astype(rest[0].dtype)

    row = pl.BlockSpec((tm, tn), lambda j, i: (i, j))
    in_specs = [pl.BlockSpec((tm, K), lambda j, i: (i, 0)) for _ in parts]
    for kb in range(np_):
        in_specs.append(pl.BlockSpec((tn, K), lambda j, i, kb=kb: (j, kb)) if nt
                        else pl.BlockSpec((K, tn), lambda j, i, kb=kb: (kb, j)))
    args = list(parts) + [b] * np_
    if rms_bwd is not None:
        h, g, dres = rms_bwd
        in_specs += [row, _const_spec((1, N)), row]
        args += [h, g, dres]
        out_specs = [row, row, _const_spec((8, N))]
        out_shape = [jax.ShapeDtypeStruct((M, N), F32), jax.ShapeDtypeStruct((M, N), BF16),
                     jax.ShapeDtypeStruct((8, N), F32)]
        sem = ("arbitrary", "arbitrary")
    else:
        if res is not None:
            in_specs.append(row)
            args.append(res)
        out_specs, out_shape = [row], [jax.ShapeDtypeStruct((M, N), out_dtype)]
        if norm_g is not None:
            in_specs.append(_const_spec((1, N)))
            args.append(norm_g)
            out_specs.append(row)
            out_shape.append(jax.ShapeDtypeStruct((M, N), BF16))
        sem = ("parallel", "parallel")
    out = pl.pallas_call(
        body,
        name=name,
        grid=(N // tn, M // tm),
        in_specs=in_specs,
        out_specs=out_specs,
        out_shape=out_shape,
        compiler_params=_params(*sem),
    )(*args)
    return out[0] if len(out) == 1 else out


def mm_tn(a, b, *, name):
    M, K = a.shape
    N = b.shape[1]
    tm = next((t for t in (4 * ROW_TILE, 2 * ROW_TILE) if M % t == 0), min(ROW_TILE, M))
    tk = _col_tile(K)
    tn = _col_tile(N)
    assert M % tm == 0 and K % tk == 0 and N % tn == 0

    def body(a_ref, b_ref, o_ref):
        @pl.when(pl.program_id(2) == 0)
        def _():
            o_ref[...] = jnp.zeros_like(o_ref)

        o_ref[...] += _dot_tn(a_ref[...], b_ref[...])

    return pl.pallas_call(
        body,
        name=name,
        grid=(K // tk, N // tn, M // tm),
        in_specs=[pl.BlockSpec((tm, tk), lambda k, j, i: (i, k)),
                  pl.BlockSpec((tm, tn), lambda k, j, i: (i, j))],
        out_specs=pl.BlockSpec((tk, tn), lambda k, j, i: (k, j)),
        out_shape=jax.ShapeDtypeStruct((K, N), F32),
        compiler_params=_params("parallel", "parallel", "arbitrary"),
    )(a, b)


def _row_spec(tm, width):
    return pl.BlockSpec((tm, width), lambda i: (i, 0))


def _const_spec(shape):
    return pl.BlockSpec(shape, lambda *_: (0,) * len(shape))


def rmsnorm_fwd(h, g, *, name):
    S, D = h.shape
    tm = min(ROW_TILE, S)

    def body(h_ref, g_ref, u_ref):
        x = h_ref[...]
        rs = lax.rsqrt(jnp.mean(x * x, axis=1, keepdims=True) + EPS)
        u_ref[...] = (x * rs * g_ref[...]).astype(u_ref.dtype)

    return pl.pallas_call(
        body, name=name, grid=(S // tm,),
        in_specs=[_row_spec(tm, D), _const_spec((1, D))],
        out_specs=_row_spec(tm, D),
        out_shape=jax.ShapeDtypeStruct((S, D), BF16),
        compiler_params=_params("parallel"),
    )(h, g)


HALO = 16


def _shift_down(x, edge8, s):
    sh = pltpu.roll(x, s, 0)
    er = pltpu.roll(edge8, s, 0)
    row8 = lax.broadcasted_iota(jnp.int32, er.shape, 0)
    top = jnp.where(row8 < s, er, sh[0:8])
    return jnp.concatenate([top, sh[8:]], axis=0)


def _shift_up(x, s):
    return pltpu.roll(x, x.shape[0] - s, 0)


def _conv_pre(a, prev8, w_ref, cb_ref):
    a1 = _shift_down(a, prev8, 1)
    a2 = _shift_down(a, prev8, 2)
    return w_ref[2:3, :] * a + w_ref[1:2, :] * a1 + w_ref[0:1, :] * a2 + cb_ref[...]


def convffn_fwd(hh, cw8, cb):
    S = hh.shape[0]
    tm = min(ROW_TILE, S)
    tn = _col_tile(D_FF)
    nj = D_FF // tn

    def body(a_ref, ap_ref, b_ref, w_ref, cb_ref, o_ref, ac_ref):
        prev8 = jnp.where(pl.program_id(1) == 0, 0.0, ap_ref[...].astype(F32)[HALO - 8:HALO])
        ac = _conv_pre(a_ref[...].astype(F32), prev8, w_ref, cb_ref)
        ac_ref[...] = ac.astype(ac_ref.dtype)
        o_ref[...] = (ac * _sigmoid(ac) * b_ref[...].astype(F32)).astype(o_ref.dtype)

    rh = tm // HALO
    return pl.pallas_call(
        body, name="convffn_fwd", grid=(nj, S // tm),
        in_specs=[pl.BlockSpec((tm, tn), lambda j, i: (i, j)),
                  pl.BlockSpec((HALO, tn), lambda j, i: (jnp.maximum(i * rh - 1, 0), j)),
                  pl.BlockSpec((tm, tn), lambda j, i: (i, j + nj)),
                  pl.BlockSpec((8, tn), lambda j, i: (0, j)),
                  pl.BlockSpec((1, tn), lambda j, i: (0, j))],
        out_specs=[pl.BlockSpec((tm, tn), lambda j, i: (i, j))] * 2,
        out_shape=[jax.ShapeDtypeStruct((S, D_FF), BF16)] * 2,
        compiler_params=_params("parallel", "parallel"),
    )(hh, hh, hh, cw8, cb)


def convffn_bwd(hh, conv, dact, cw8):
    S = hh.shape[0]
    tm = min(ROW_TILE, S)
    tn = _col_tile(D_FF)
    nj = D_FF // tn
    ni = S // tm

    def body(a_ref, b_ref, bn_ref, c_ref, cn_ref, d_ref, dn_ref, w_ref, o_a, o_b, dw_ref):
        i = pl.program_id(1)

        @pl.when(i == 0)
        def _():
            dw_ref[...] = jnp.zeros_like(dw_ref)

        up = lambda r: r[...].astype(F32)
        ext = lambda cur, nxt: jnp.concatenate([up(cur), up(nxt)[0:8]], axis=0)
        b = ext(b_ref, bn_ref)
        ac = ext(c_ref, cn_ref)
        d = jnp.concatenate([up(d_ref), jnp.where(i == ni - 1, 0.0, up(dn_ref)[0:8])], axis=0)
        sa = _sigmoid(ac)
        silu = ac * sa
        o_b[...] = (d[0:tm] * silu[0:tm]).astype(o_b.dtype)
        dac = d * b * (sa + silu * (1.0 - sa))
        dc0 = dac[0:tm]
        dc1 = _shift_up(dac, 1)[0:tm]
        dc2 = _shift_up(dac, 2)[0:tm]
        o_a[...] = (w_ref[2:3, :] * dc0 + w_ref[1:2, :] * dc1 + w_ref[0:1, :] * dc2).astype(o_a.dtype)
        a = up(a_ref)
        dw_ref[0:1, :] += _colsum(dc2 * a)
        dw_ref[1:2, :] += _colsum(dc1 * a)
        dw_ref[2:3, :] += _colsum(dc0 * a)
        dw_ref[3:4, :] += _colsum(dc0)

    rh = tm // HALO
    last = S // HALO - 1
    cur = lambda off: pl.BlockSpec((tm, tn), lambda j, i: (i, j + off))
    nxt = lambda off: pl.BlockSpec((HALO, tn), lambda j, i: (jnp.minimum((i + 1) * rh, last), j + off))
    return pl.pallas_call(
        body, name="convffn_bwd", grid=(nj, ni),
        in_specs=[cur(0), cur(nj), nxt(nj), cur(0), nxt(0), cur(0), nxt(0),
                  pl.BlockSpec((8, tn), lambda j, i: (0, j))],
        out_specs=[cur(0), cur(0), pl.BlockSpec((8, tn), lambda j, i: (0, j))],
        out_shape=[jax.ShapeDtypeStruct((S, D_FF), BF16), jax.ShapeDtypeStruct((S, D_FF), BF16),
                   jax.ShapeDtypeStruct((8, D_FF), F32)],
        compiler_params=_params("parallel", "arbitrary"),
    )(hh, hh, hh, conv, conv, dact, dact, cw8)


def ple_fwd(h, gpre, p, wpu, norm_g):
    S, D = h.shape
    tm = min(ROW_TILE, S)

    def body(h_ref, g_ref, p_ref, w_ref, ng_ref, o_ref, u_ref):
        out = h_ref[...] + _sigmoid(g_ref[...].astype(F32)) * _dot(p_ref[...], w_ref[...])
        o_ref[...] = out
        u_ref[...] = (out * _rms_scale(out) * ng_ref[...]).astype(BF16)

    return pl.pallas_call(
        body, name="ple_fwd", grid=(S // tm,),
        in_specs=[_row_spec(tm, D), _row_spec(tm, D), _row_spec(tm, PLE_DIM), _const_spec((PLE_DIM, D)),
                  _const_spec((1, D))],
        out_specs=[_row_spec(tm, D), _row_spec(tm, D)],
        out_shape=[jax.ShapeDtypeStruct((S, D), F32), jax.ShapeDtypeStruct((S, D), BF16)],
        compiler_params=_params("parallel"),
    )(h, gpre, p, wpu, norm_g)


def ple_bwd(dh, gpre, p, wpu):
    S, D = dh.shape
    tm = min(ROW_TILE, S)

    def body(d_ref, g_ref, p_ref, w_ref, dpu_ref, dg_ref):
        d = d_ref[...]
        gate = _sigmoid(g_ref[...].astype(F32))
        pu = _dot(p_ref[...], w_ref[...])
        dpu_ref[...] = (d * gate).astype(dpu_ref.dtype)
        dg_ref[...] = (d * pu * gate * (1.0 - gate)).astype(dg_ref.dtype)

    return pl.pallas_call(
        body, name="ple_bwd", grid=(S // tm,),
        in_specs=[_row_spec(tm, D), _row_spec(tm, D), _row_spec(tm, PLE_DIM), _const_spec((PLE_DIM, D))],
        out_specs=[_row_spec(tm, D), _row_spec(tm, D)],
        out_shape=[jax.ShapeDtypeStruct((S, D), BF16)] * 2,
        compiler_params=_params("parallel"),
    )(dh, gpre, p, wpu)


def loss_head(h, g, tgt):
    S, D = h.shape
    tm = min(ROW_TILE, S)

    def body(h_ref, g_ref, t_ref, dh_ref, dhb_ref, l_ref, dg_ref):
        @pl.when(pl.program_id(0) == 0)
        def _():
            l_ref[...] = jnp.zeros_like(l_ref)
            dg_ref[...] = jnp.zeros_like(dg_ref)

        x = h_ref[...]
        gr = g_ref[...]
        rs = lax.rsqrt(jnp.mean(x * x, axis=1, keepdims=True) + EPS)
        xh = x * rs
        err = xh * gr - t_ref[...]
        l_ref[0:1, 0:1] += 0.5 * _colsum(jnp.mean(err * err, axis=1, keepdims=True))
        dy = err * (1.0 / D)
        dg_ref[0:1, :] += _colsum(dy * xh)
        dxh = dy * gr
        dh = rs * (dxh - xh * jnp.mean(dxh * xh, axis=1, keepdims=True))
        dh_ref[...] = dh
        dhb_ref[...] = dh.astype(BF16)

    return pl.pallas_call(
        body, name="loss_head", grid=(S // tm,),
        in_specs=[_row_spec(tm, D), _const_spec((1, D)), _row_spec(tm, D)],
        out_specs=[_row_spec(tm, D), _row_spec(tm, D), _const_spec((8, 128)), _const_spec((8, D))],
        out_shape=[jax.ShapeDtypeStruct((S, D), F32), jax.ShapeDtypeStruct((S, D), BF16),
                   jax.ShapeDtypeStruct((8, 128), F32), jax.ShapeDtypeStruct((8, D), F32)],
        compiler_params=_params("arbitrary"),
    )(h, g, tgt)


def _lb_rows(l_ref):
    l = l_ref[...]
    e = jnp.exp(l - jnp.max(l, axis=0, keepdims=True))
    p = e / _colsum(e)
    lbs, run = [], None
    for i in range(DEPTH):
        run = p[i:i + 1] if i == 0 else run + p[i:i + 1]
        lbs.append(run - p[0:1])
    return p, lbs


def lb_fwd(lb_logits):
    def body(l_ref, o_ref):
        _, lbs = _lb_rows(l_ref)
        o_ref[...] = jnp.zeros_like(o_ref)
        for i, lb in enumerate(lbs):
            o_ref[8 * i:8 * i + 1, :] = jnp.log(jnp.maximum(lb, LB_FLOOR))
            o_ref[8 * i + 1:8 * i + 2, :] = jnp.log1p(-lb)
            o_ref[8 * i + 2:8 * i + 3, :] = 1.0 - lb
            o_ref[8 * i + 3:8 * i + 4, :] = lb

    return pl.pallas_call(
        body, name="lb_fwd",
        out_shape=jax.ShapeDtypeStruct((DEPTH * 8, HG_WIDTH), F32),
    )(lb_logits)


def lb_bwd(dlbrows, lb_logits):
    def body(d_ref, l_ref, o_ref):
        p, lbs = _lb_rows(l_ref)
        dlb = []
        for i, lb in enumerate(lbs):
            da = d_ref[8 * i:8 * i + 1, :]
            dc = d_ref[8 * i + 1:8 * i + 2, :]
            do = d_ref[8 * i + 2:8 * i + 3, :]
            dlb.append(jnp.where(lb > LB_FLOOR, da / jnp.maximum(lb, LB_FLOOR), 0.0) - dc / (1.0 - lb) - do)
        dp = [jnp.zeros_like(dlb[0])]
        for j in range(1, DEPTH):
            acc = dlb[j]
            for i in range(j + 1, DEPTH):
                acc = acc + dlb[i]
            dp.append(acc)
        dot_ = p[0:1] * dp[0]
        for j in range(1, DEPTH):
            dot_ = dot_ + p[j:j + 1] * dp[j]
        o_ref[...] = jnp.zeros_like(o_ref)
        for j in range(DEPTH):
            o_ref[j:j + 1, :] = p[j:j + 1] * (dp[j] - dot_)

    return pl.pallas_call(
        body, name="lb_bwd",
        out_shape=jax.ShapeDtypeStruct((8, HG_WIDTH), F32),
    )(dlbrows, lb_logits)


def adamw(w, g, m, v, *, name):
    R, C = w.shape
    tr = next((t for t in (512, 256, 128, 64, 32, 16, 8) if R % t == 0), R)

    def body(w_ref, g_ref, m_ref, v_ref, d_ref, m2_ref, v2_ref):
        gv = g_ref[...]
        m2 = ADAM_B1 * m_ref[...] + (1.0 - ADAM_B1) * gv
        v2 = ADAM_B2 * v_ref[...] + (1.0 - ADAM_B2) * (gv * gv)
        mh = m2 / (1.0 - ADAM_B1 ** ADAM_STEP)
        vh = v2 / (1.0 - ADAM_B2 ** ADAM_STEP)
        d_ref[...] = -ADAM_LR * (mh / (jnp.sqrt(vh) + ADAM_EPS) + ADAM_WD * w_ref[...])
        m2_ref[...] = m2
        v2_ref[...] = v2

    spec = pl.BlockSpec((tr, C), lambda i: (i, 0))
    return pl.pallas_call(
        body, name=name, grid=(R // tr,),
        in_specs=[spec] * 4, out_specs=[spec] * 3,
        out_shape=[jax.ShapeDtypeStruct((R, C), F32)] * 3,
        compiler_params=_params("parallel"),
    )(w, g, m, v)


def sum_slots(x, *, out_dtype, name):
    n, R, C = x.shape
    tr = 848 if R % 848 == 0 else R

    def body(x_ref, o_ref):
        acc = x_ref[0].astype(F32)
        for k in range(1, n):
            acc = acc + x_ref[k].astype(F32)
        o_ref[...] = acc.astype(o_ref.dtype)

    return pl.pallas_call(
        body, name=name, grid=(R // tr,),
        in_specs=[pl.BlockSpec((n, tr, C), lambda i: (0, i, 0))],
        out_specs=pl.BlockSpec((tr, C), lambda i: (i, 0)),
        out_shape=jax.ShapeDtypeStruct((R, C), out_dtype),
        compiler_params=_params("parallel"),
    )(x)


MESH = pl.DeviceIdType.MESH
ANY = pl.BlockSpec(memory_space=pl.ANY)


def _place():
    return lax.axis_index("x"), lax.axis_index("y"), lax.axis_index("c")


def _other_chips(x, y):
    return [(1 - x, y), (x, 1 - y), (1 - x, 1 - y)]


def small_allgather(buf):
    R, C = buf.shape

    def body(x_ref, out_ref, send_sems, recv_sems, local_sem):
        x, y, c = _place()
        me, sibling = (x, y, c), (x, y, 1 - c)
        chips = _other_chips(x, y)

        def slot(px, py, pc):
            return out_ref.at[4 * px + 2 * py + pc]

        def copy(k, block, to, src=None):
            return pltpu.make_async_remote_copy(
                src_ref=slot(*block) if src is None else src, dst_ref=slot(*block),
                send_sem=send_sems.at[k], recv_sem=recv_sems.at[k],
                device_id=to, device_id_type=MESH)

        mine = pltpu.make_async_copy(x_ref, slot(*me), local_sem)
        mine.start()
        first = [copy(0, me, sibling, src=x_ref)]
        first += [copy(1 + r, me, (*chip, c), src=x_ref) for r, chip in enumerate(chips)]
        for cp in first:
            cp.start()
        passed = [copy(4 + r, (*chip, c), sibling) for r, chip in enumerate(chips)]
        for r, chip in enumerate(chips):
            copy(1 + r, (*chip, c), me).wait_recv()
            passed[r].start()
        copy(0, sibling, me).wait_recv()
        for r, chip in enumerate(chips):
            copy(4 + r, (*chip, 1 - c), me).wait_recv()
        for cp in first + passed:
            cp.wait_send()
        mine.wait()

    return pl.pallas_call(
        body, name="small_allgather",
        out_shape=jax.ShapeDtypeStruct((8, R, C), buf.dtype),
        in_specs=[pl.BlockSpec(memory_space=pltpu.VMEM)],
        out_specs=pl.BlockSpec(memory_space=pltpu.VMEM),
        scratch_shapes=[pltpu.SemaphoreType.DMA((7,)), pltpu.SemaphoreType.DMA((7,)),
                        pltpu.SemaphoreType.DMA],
    )(buf)


def weights_allgather(wp):
    def body(w_ref, g_ref, send_sems, recv_sems, local_sem):
        gather = _Gather(w_ref, g_ref, send_sems, recv_sems, local_sem)
        gather.start()
        gather.finish()

    return pl.pallas_call(
        body, name="weights_allgather",
        out_shape=jax.ShapeDtypeStruct((4,) + wp.shape, wp.dtype),
        in_specs=[ANY], out_specs=ANY,
        scratch_shapes=_Gather.SCRATCH,
    )(wp)


class _Gather:
    SCRATCH = [pltpu.SemaphoreType.DMA((6,)), pltpu.SemaphoreType.DMA((6,)), pltpu.SemaphoreType.DMA]

    def __init__(self, w_ref, g_ref, send_sems, recv_sems, local_sem):
        self.w_ref, self.g_ref, self.local_sem = w_ref, g_ref, local_sem
        self.send_sems, self.recv_sems = send_sems, recv_sems
        self.x, self.y, self.c = _place()
        self.chips = _other_chips(self.x, self.y)
        half = w_ref.shape[0] // 2
        self.mine = pl.ds(pl.multiple_of(self.c * half, 16), half)
        self.theirs = pl.ds(pl.multiple_of((1 - self.c) * half, 16), half)

    def _copy(self, k, chip_block, rows, to, src=None):
        dst = self.g_ref.at[chip_block, rows]
        return pltpu.make_async_remote_copy(
            src_ref=dst if src is None else src, dst_ref=dst,
            send_sem=self.send_sems.at[k], recv_sem=self.recv_sems.at[k],
            device_id=to, device_id_type=MESH)

    def _own(self):
        return pltpu.make_async_copy(self.w_ref, self.g_ref.at[2 * self.x + self.y], self.local_sem)

    def _first(self):
        return [self._copy(r, 2 * self.x + self.y, self.mine, (*chip, self.c), src=self.w_ref.at[self.mine])
                for r, chip in enumerate(self.chips)]

    def start(self):
        self._own().start()
        for cp in self._first():
            cp.start()

    def finish(self):
        sibling = (self.x, self.y, 1 - self.c)
        passed = [self._copy(3 + r, 2 * chip[0] + chip[1], self.mine, sibling) for r, chip in enumerate(self.chips)]
        for r, chip in enumerate(self.chips):
            self._copy(r, 2 * chip[0] + chip[1], self.mine, (*chip, self.c)).wait_recv()
            passed[r].start()
        for r, chip in enumerate(self.chips):
            self._copy(3 + r, 2 * chip[0] + chip[1], self.theirs, sibling).wait_recv()
        for cp in self._first() + passed:
            cp.wait_send()
        self._own().wait()


def sibling_swap(v, *, name):
    def body(v_ref, got_ref, send_sem, recv_sem):
        x, y, c = _place()
        cp = pltpu.make_async_remote_copy(
            src_ref=v_ref, dst_ref=got_ref, send_sem=send_sem, recv_sem=recv_sem,
            device_id=(x, y, 1 - c), device_id_type=MESH)
        cp.start()
        cp.wait()

    return pl.pallas_call(
        body, name=name,
        out_shape=jax.ShapeDtypeStruct(v.shape, v.dtype),
        in_specs=[ANY], out_specs=ANY,
        scratch_shapes=[pltpu.SemaphoreType.DMA, pltpu.SemaphoreType.DMA],
    )(v)


def chip_exchange(q):
    def body(q_ref, r_ref, send_sems, recv_sems, local_sem):
        exchange = _Exchange(q_ref, r_ref, send_sems, recv_sems, local_sem)
        exchange.start()
        exchange.finish()

    return pl.pallas_call(
        body, name="chip_exchange",
        out_shape=jax.ShapeDtypeStruct(q.shape, q.dtype),
        in_specs=[ANY], out_specs=ANY,
        scratch_shapes=_Exchange.SCRATCH,
    )(q)


class _Exchange:
    SCRATCH = [pltpu.SemaphoreType.DMA((3,)), pltpu.SemaphoreType.DMA((3,)), pltpu.SemaphoreType.DMA]

    def __init__(self, q_ref, r_ref, send_sems, recv_sems, local_sem):
        self.q_ref, self.r_ref, self.local_sem = q_ref, r_ref, local_sem
        self.send_sems, self.recv_sems = send_sems, recv_sems
        self.x, self.y, self.c = _place()
        self.j = 2 * self.x + self.y
        self.chips = _other_chips(self.x, self.y)

    def _copy(self, r, src_block, dst_block, chip):
        return pltpu.make_async_remote_copy(
            src_ref=self.q_ref.at[src_block], dst_ref=self.r_ref.at[dst_block],
            send_sem=self.send_sems.at[r], recv_sem=self.recv_sems.at[r],
            device_id=(*chip, self.c), device_id_type=MESH)

    def _own(self):
        return pltpu.make_async_copy(self.q_ref.at[self.j], self.r_ref.at[self.j], self.local_sem)

    def _sends(self):
        return [self._copy(r, 2 * chip[0] + chip[1], self.j, chip) for r, chip in enumerate(self.chips)]

    def start(self):
        self._own().start()
        for cp in self._sends():
            cp.start()

    def finish(self):
        for r, chip in enumerate(self.chips):
            jr = 2 * chip[0] + chip[1]
            self._copy(r, jr, jr, chip).wait_recv()
        for cp in self._sends():
            cp.wait_send()
        self._own().wait()


N_CHIPS = 4
_PACK = (("w_in", 704), ("w_out", 256), ("w_up", 1408), ("w_down", 704), ("w_ple_gate", 256), ("w_ple_up", 64))
LAYER_ROWS = sum(r for _, r in _PACK)
PACK_ROWS = DEPTH * LAYER_ROWS
HALF_ROWS = PACK_ROWS // 2


def _pack_shards(sh):
    parts = []
    for i in range(DEPTH):
        for name, rows in _PACK:
            parts.append(sh[name][i].reshape(rows, D_MODEL))
    return jnp.concatenate(parts, axis=0)


def _unpack_shards(slab):
    shapes = {"w_in": (D_MODEL, IN_WIDTH // N_CHIPS), "w_out": (D_MODEL // N_CHIPS, D_MODEL),
              "w_up": (D_MODEL, 2 * D_FF // N_CHIPS), "w_down": (D_FF // N_CHIPS, D_MODEL),
              "w_ple_gate": (D_MODEL // N_CHIPS, D_MODEL), "w_ple_up": (PLE_DIM, D_MODEL // N_CHIPS)}
    out = {name: [] for name, _ in _PACK}
    off = 0
    for i in range(DEPTH):
        for name, rows in _PACK:
            out[name].append(slab[off:off + rows].reshape(shapes[name]))
            off += rows
    return {k: jnp.stack(v) for k, v in out.items()}


_COL_SHARDED = ("w_in", "w_up", "w_ple_up")


def _full_from_chips(g, layer):
    per_chip = [_unpack_shards_layer(g[k], layer) for k in range(N_CHIPS)]
    return {name: jnp.concatenate([pc[name] for pc in per_chip], axis=1 if name in _COL_SHARDED else 0)
            for name, _ in _PACK}


def _unpack_shards_layer(slab, layer):
    shapes = {"w_in": (D_MODEL, IN_WIDTH // N_CHIPS), "w_out": (D_MODEL // N_CHIPS, D_MODEL),
              "w_up": (D_MODEL, 2 * D_FF // N_CHIPS), "w_down": (D_FF // N_CHIPS, D_MODEL),
              "w_ple_gate": (D_MODEL // N_CHIPS, D_MODEL), "w_ple_up": (PLE_DIM, D_MODEL // N_CHIPS)}
    out = {}
    off = layer * LAYER_ROWS
    for name, rows in _PACK:
        out[name] = slab[off:off + rows].reshape(shapes[name])
        off += rows
    return out


def _split_to_chips(full, name):
    r, c = full.shape
    if name in _COL_SHARDED:
        full = full.reshape(r, N_CHIPS, c // N_CHIPS).transpose(1, 0, 2)
    return full.reshape(N_CHIPS, -1, D_MODEL)


_SMALL = (("loss", 128), ("g_final", 1024), ("g_mix", 4096), ("lb_logits", 2048), ("hg_norm_g", 2048),
          ("attn_sinks", 128), ("g_ffn", 4096), ("conv_w", 4 * 3 * D_FF), ("conv_b", 4 * D_FF), ("g_ple", 4096))
SMALL_ROWS = 496


def _pack_small(d):
    parts = []
    for name, n in _SMALL:
        v = d[name].reshape(-1).astype(F32)
        parts.append(jnp.pad(v, (0, n - v.shape[0])))
    flat = jnp.concatenate(parts)
    return jnp.pad(flat, (0, SMALL_ROWS * 128 - flat.shape[0])).reshape(SMALL_ROWS, 128)


def _unpack_small(buf, shapes):
    flat = buf.reshape(-1)
    out, off = {}, 0
    for name, n in _SMALL:
        size = 1
        for s in shapes[name]:
            size *= s
        out[name] = flat[off:off + size].reshape(shapes[name])
        off += n
    return out


WEIGHT_ORDER = ('g_mix', 'w_in', 'lb_logits', 'hg_norm_g', 'attn_sinks', 'w_out', 'g_ffn', 'w_up', 'conv_w',
                'conv_b', 'w_down', 'g_ple', 'w_ple_gate', 'w_ple_up', 'g_final')


def kernel(x, p, g_mix, w_in, lb_logits, hg_norm_g, attn_sinks, w_out, g_ffn, w_up, conv_w, conv_b, w_down, g_ple, w_ple_gate, w_ple_up, g_final, loss_target, m_g_mix, m_w_in, m_lb_logits, m_hg_norm_g, m_attn_sinks, m_w_out, m_g_ffn, m_w_up, m_conv_w, m_conv_b, m_w_down, m_g_ple, m_w_ple_gate, m_w_ple_up, m_g_final, v_g_mix, v_w_in, v_lb_logits, v_hg_norm_g, v_attn_sinks, v_w_out, v_g_ffn, v_w_up, v_conv_w, v_conv_b, v_w_down, v_g_ple, v_w_ple_gate, v_w_ple_up, v_g_final):
    W = dict(g_mix=g_mix, w_in=w_in, lb_logits=lb_logits, hg_norm_g=hg_norm_g, attn_sinks=attn_sinks,
             w_out=w_out, g_ffn=g_ffn, w_up=w_up, conv_w=conv_w, conv_b=conv_b, w_down=w_down, g_ple=g_ple,
             w_ple_gate=w_ple_gate, w_ple_up=w_ple_up, g_final=g_final)
    M = dict(g_mix=m_g_mix, w_in=m_w_in, lb_logits=m_lb_logits, hg_norm_g=m_hg_norm_g, attn_sinks=m_attn_sinks,
             w_out=m_w_out, g_ffn=m_g_ffn, w_up=m_w_up, conv_w=m_conv_w, conv_b=m_conv_b, w_down=m_w_down,
             g_ple=m_g_ple, w_ple_gate=m_w_ple_gate, w_ple_up=m_w_ple_up, g_final=m_g_final)
    V = dict(g_mix=v_g_mix, w_in=v_w_in, lb_logits=v_lb_logits, hg_norm_g=v_hg_norm_g, attn_sinks=v_attn_sinks,
             w_out=v_w_out, g_ffn=v_g_ffn, w_up=v_w_up, conv_w=v_conv_w, conv_b=v_conv_b, w_down=v_w_down,
             g_ple=v_g_ple, w_ple_gate=v_w_ple_gate, w_ple_up=v_w_ple_up, g_final=v_g_final)
    S = x.shape[1]
    hg_rows = min(ROW_TILE, S)
    xi, yi, ci = _place()
    chip = 2 * xi + yi

    slab = _pack_shards({n: W[n] for n, _ in _PACK}).astype(BF16).reshape(DEPTH, LAYER_ROWS, D_MODEL)
    gathered = weights_allgather(slab[0])
    cw_shard = jnp.pad(conv_w.reshape(-1), (0, 72 * 128 - conv_w.size)).reshape(72, 128)
    cw_all = small_allgather(cw_shard)
    cw_full = jnp.concatenate(
        [cw_all[2 * k].reshape(-1)[:conv_w.size].reshape(conv_w.shape) for k in range(N_CHIPS)], axis=2)
    lbrows = lb_fwd(lb_logits)
    at_mask = swa_mask()

    h = x[0]
    saved = []
    for i in range(DEPTH):
        wf = _full_from_chips(gathered, 0)
        lbr = lbrows[8 * i:8 * i + 8]
        ng = hg_norm_g[i][None]
        sinks_b = jnp.pad(jnp.repeat(attn_sinks[i].reshape(AT_KV_HEADS, 1, AT_GROUP), WINDOW, axis=2),
                          ((0, 0), (0, 7), (0, 0))).reshape(8 * AT_KV_HEADS, GROUP_LANES)
        cw8 = jnp.pad(cw_full[i], ((0, 5), (0, 0)))
        cb = conv_b[i][None]
        if i == 0:
            u = rmsnorm_fwd(h, g_mix[0][None], name="rmsnorm_fwd")
        proj = mm(u, wf["w_in"], name="mm_in")
        if i + 1 < DEPTH:
            y, o_raw, states, gathered = hgrn_fwd(proj, lbr, ng, D_MODEL, rows=hg_rows, gather=slab[i + 1])
        else:
            y, o_raw, states = hgrn_fwd(proj, lbr, ng, D_MODEL, rows=hg_rows)
        y = swa_fwd(proj, sinks_b, at_mask, y)
        h1, u2 = mm(y, wf["w_out"], res=h, norm_g=g_ffn[i][None], name="mm_out")
        hh = mm(u2, wf["w_up"], out_dtype=BF16, name="mm_up")
        act, conv = convffn_fwd(hh, cw8, cb)
        h2, u3 = mm(act, wf["w_down"], res=h1, norm_g=g_ple[i][None], name="mm_down")
        gpre = mm(u3, wf["w_ple_gate"], out_dtype=BF16, name="mm_gate")
        next_g = g_mix[i + 1] if i + 1 < DEPTH else g_final
        h3, u_next = ple_fwd(h2, gpre, p[i, 0], wf["w_ple_up"], next_g[None])
        saved.append(dict(wf=wf, lbr=lbr, ng=ng, sinks_b=sinks_b, cw8=cw8, cb=cb, h=h, u=u, proj=proj,
                          o_raw=o_raw, states=states, y=y, h1=h1, u2=u2, hh=hh, conv=conv, act=act, h2=h2, u3=u3,
                          gpre=gpre))
        h, u = h3, u_next

    dh, dhb, loss_acc, dg_final = loss_head(h, g_final[None], loss_target[0])

    gfull = {n: [None] * DEPTH for n, _ in _PACK}
    gsmall = {n: [None] * DEPTH for n in ("g_mix", "hg_norm_g", "attn_sinks", "g_ffn", "conv_w", "conv_b", "g_ple")}
    dlbrows = [None] * DEPTH
    half_rows = LAYER_ROWS // 2
    from_chips = [None] * DEPTH
    pending = None
    for i in reversed(range(DEPTH)):
        s = saved[i]
        wf = s["wf"]
        dpu, dgp = ple_bwd(dh, s["gpre"], p[i, 0], wf["w_ple_up"])
        gfull["w_ple_up"][i] = mm_tn(p[i, 0], dpu, name="mm_tn_pu")
        gfull["w_ple_gate"][i] = mm_tn(s["u3"], dgp, name="mm_tn_gate")
        dh2, dh2b, dg = mm(dgp, wf["w_ple_gate"], nt=True, rms_bwd=(s["h2"], g_ple[i][None], dh),
                           name="mm_nt_gate")
        gsmall["g_ple"][i] = dg[0]
        gfull["w_down"][i] = mm_tn(s["act"], dh2b, name="mm_tn_down")
        dact = mm(dh2b, wf["w_down"], nt=True, out_dtype=BF16, name="mm_nt_down")
        da, db, dcw = convffn_bwd(s["hh"], s["conv"], dact, s["cw8"])
        gsmall["conv_w"][i] = dcw[0:3]
        gsmall["conv_b"][i] = dcw[3]
        gfull["w_up"][i] = jnp.concatenate([mm_tn(s["u2"], da, name="mm_tn_up"),
                                            mm_tn(s["u2"], db, name="mm_tn_up")], axis=1)
        dh1, dh1b, dg = mm((da, db), wf["w_up"], nt=True, rms_bwd=(s["h1"], g_ffn[i][None], dh2),
                           name="mm_nt_up")
        gsmall["g_ffn"][i] = dg[0]
        gfull["w_out"][i] = mm_tn(s["y"], dh1b, name="mm_tn_out")
        dy = mm(dh1b, wf["w_out"], nt=True, name="mm_nt_out")
        dq_at, dko, dkp, dvo, dvp, dsk = swa_bwd(s["proj"], s["sinks_b"], at_mask, dy)
        gsmall["attn_sinks"][i] = dsk.reshape(AT_KV_HEADS, 8, GROUP_LANES)[:, 1:1 + AT_GROUP, 0].reshape(-1)
        hg_args = (s["proj"], s["o_raw"], s["states"], dy, s["lbr"], s["ng"])
        if pending is None:
            hq, hz, hv, hgp, dlbr, dng = hgrn_bwd(*hg_args, rows=hg_rows)
        else:
            hq, hz, hv, hgp, dlbr, dng, got = hgrn_bwd(*hg_args, rows=hg_rows, exchange=pending[1])
            from_chips[pending[0]] = got
        dlbrows[i] = dlbr
        gsmall["hg_norm_g"][i] = dng[0]
        dproj = assemble_dproj((hq, hz, hv, hgp), dq_at, dko, dkp, dvo, dvp, rows=hg_rows)
        gfull["w_in"][i] = mm_tn(s["u"], dproj, name="mm_tn_in")
        dh, dhb, dg = mm(dproj, wf["w_in"], nt=True, rms_bwd=(s["h"], g_mix[i][None], dh1), name="mm_nt_in")
        gsmall["g_mix"][i] = dg[0]
        pk = jnp.concatenate([_split_to_chips(gfull[name][i], name) for name, _ in _PACK], axis=1).astype(BF16)
        pk = pk.reshape(N_CHIPS, 2, half_rows, D_MODEL)
        p_mine = lax.dynamic_index_in_dim(pk, ci, axis=1, keepdims=False)
        p_other = lax.dynamic_index_in_dim(pk, 1 - ci, axis=1, keepdims=False)
        from_sib = sibling_swap(p_other, name="sibling_swap_partials")
        pair = sum_slots(jnp.stack([p_mine.reshape(-1, D_MODEL), from_sib.reshape(-1, D_MODEL)]),
                         out_dtype=BF16, name="sum_pair")
        pending = (i, pair.reshape(N_CHIPS, half_rows, D_MODEL))
    from_chips[0] = chip_exchange(pending[1])
    grad_x = dh[None]
    dlb_logits = lb_bwd(jnp.concatenate(dlbrows, axis=0), lb_logits)[0:DEPTH]

    mine_sum = jnp.concatenate([sum_slots(from_chips[i], out_dtype=F32, name="sum_chips") for i in range(DEPTH)],
                               axis=0)
    sib_sum = sibling_swap(mine_sum, name="sibling_swap_sums")
    mine_sum = mine_sum.reshape(DEPTH, half_rows, D_MODEL)
    sib_sum = sib_sum.reshape(DEPTH, half_rows, D_MODEL)
    lo = jnp.where(ci == 0, mine_sum, sib_sum)
    hi = jnp.where(ci == 0, sib_sum, mine_sum)
    gshard = _unpack_shards(jnp.concatenate([lo, hi], axis=1).reshape(PACK_ROWS, D_MODEL))

    small = dict(loss=loss_acc[0, 0:1], g_final=dg_final[0], lb_logits=dlb_logits,
                 **{n: jnp.stack(v) for n, v in gsmall.items()})
    small_sum = sum_slots(small_allgather(_pack_small(small)), out_dtype=F32, name="sum_small")
    shapes = {n: W[n].shape for n in W}
    shapes["loss"] = (1,)
    shapes["conv_w"] = (DEPTH, 3, D_FF)
    gs = _unpack_small(small_sum, shapes)
    loss = gs["loss"][0]
    cshard = conv_w.shape[2]
    grads = dict(gshard)
    for n in ("g_mix", "lb_logits", "hg_norm_g", "attn_sinks", "g_ffn", "conv_b", "g_ple", "g_final"):
        grads[n] = gs[n]
    grads["conv_w"] = lax.dynamic_slice_in_dim(gs["conv_w"], chip * cshard, cshard, axis=2)

    delta, new_m, new_v = {}, {}, {}
    small_names = ("g_final", "g_mix", "lb_logits", "hg_norm_g", "attn_sinks", "g_ffn", "conv_b", "g_ple")
    sshapes = {n: W[n].shape for n in small_names}

    def pack_s(d):
        z = dict(d)
        z["loss"] = jnp.zeros((1,), F32)
        z["conv_w"] = jnp.zeros((1,), F32)
        return _pack_small(z)

    sd, sm, sv = adamw(pack_s(W), pack_s(grads), pack_s(M), pack_s(V), name="adamw_small")
    for out, buf in ((delta, sd), (new_m, sm), (new_v, sv)):
        un = _unpack_small(buf, {**sshapes, "loss": (1,), "conv_w": (1,)})
        for n in small_names:
            out[n] = un[n]
    for n in ("w_in", "w_out", "w_up", "w_down", "w_ple_gate", "w_ple_up", "conv_w"):
        shp = W[n].shape
        two_d = (-1, shp[-1])
        d_, m_, v_ = adamw(W[n].reshape(two_d), grads[n].reshape(two_d), M[n].reshape(two_d),
                           V[n].reshape(two_d), name="adamw_" + n)
        delta[n], new_m[n], new_v[n] = d_.reshape(shp), m_.reshape(shp), v_.reshape(shp)

    return (loss, grad_x, *[grads[n] for n in WEIGHT_ORDER], *[delta[n] for n in WEIGHT_ORDER],
            *[new_m[n] for n in WEIGHT_ORDER], *[new_v[n] for n in WEIGHT_ORDER])
```

```python
import functools

import jax
import jax.numpy as jnp
from jax import lax
from jax.experimental import pallas as pl
from jax.experimental.pallas import tpu as pltpu

F32 = jnp.float32
BF16 = jnp.bfloat16

D_MODEL = 1024
DEPTH = 4
PLE_DIM = 256
HG_WIDTH = 512
HG_HEADS = 4
HG_DK = 128
HG_CHUNK = 64
HG_SUB = 16
AT_WIDTH = 512
AT_HEAD_DIM = 64
AT_Q_HEADS = 8
AT_KV_HEADS = 2
AT_GROUP = 4
WINDOW = 128
D_FF = 2816
IN_WIDTH = 2816
EPS = 1e-6
MASK_VALUE = -1e30
LB_FLOOR = 1e-30

ADAM_LR = 0.001
ADAM_B1 = 0.9
ADAM_B2 = 0.999
ADAM_EPS = 1e-08
ADAM_WD = 0.01
ADAM_STEP = 10

VMEM_LIMIT = 48 * 1024 * 1024


def _params(*sem):
    return pltpu.CompilerParams(dimension_semantics=sem, vmem_limit_bytes=VMEM_LIMIT)


def _dot(a, b, dims=(((1,), (0,)), ((), ()))):
    return lax.dot_general(a.astype(BF16), b.astype(BF16), dims, preferred_element_type=F32)


def _dot_nt(a, b):
    return _dot(a, b, (((1,), (1,)), ((), ())))


def _dot_tn(a, b):
    return _dot(a, b, (((0,), (0,)), ((), ())))


def _dot_exact(sel, x, dims=(((1,), (0,)), ((), ()))):
    hi = x.astype(BF16)
    r1 = x - hi.astype(F32)
    mid = r1.astype(BF16)
    lo = (r1 - mid.astype(F32)).astype(BF16)
    s = sel.astype(BF16)
    one = lambda p: lax.dot_general(s, p, dims, preferred_element_type=F32)
    return one(hi) + one(mid) + one(lo)


def _sigmoid(x):
    return 0.5 * jnp.tanh(0.5 * x) + 0.5


def _logsig(x):
    return jnp.minimum(x, 0.0) - jnp.log(1.0 + jnp.exp(-jnp.abs(x)))


def _colsum(x):
    return jnp.sum(x, axis=0, keepdims=True)


def _rowsum(x):
    return jnp.sum(x, axis=1, keepdims=True)


def _colsum8(xs):
    row = lax.broadcasted_iota(jnp.int32, xs[0].shape, 0)

    def merge(a, b, keep_a, step):
        return jnp.where(keep_a, a + pltpu.roll(a, 8 - step, 0), b + pltpu.roll(b, step, 0))

    c = [merge(xs[j], xs[j + 4], row < 4, 4) for j in range(4)]
    d = [merge(c[j], c[j + 2], (row & 3) < 2, 2) for j in range(2)]
    return merge(d[0], d[1], (row & 1) == 0, 1)


def _tri(n):
    r = lax.broadcasted_iota(jnp.int32, (n, n), 0)
    c = lax.broadcasted_iota(jnp.int32, (n, n), 1)
    return (r >= c).astype(F32)


def _hg_gates(qp, z, a, c, oml):
    sq = _sigmoid(qp)
    q = qp * sq
    t = c + _logsig(z)
    mx = jnp.maximum(a, t)
    logf = mx + jnp.log(1.0 + jnp.exp(-jnp.abs(a - t)))
    snz = _sigmoid(-z)
    k = oml * snz
    return q, sq, t, logf, snz, k


_HEADS = range(HG_HEADS)


def _lanes(h):
    return slice(h * HG_DK, (h + 1) * HG_DK)


def _head(x, h):
    return x[:, _lanes(h)]


def _row_masks():
    row8 = lax.broadcasted_iota(jnp.int32, (8, HG_DK), 0)
    return [None] + [jnp.where(row8 >= j, 0.0, MASK_VALUE) for j in range(1, 8)]


def _hg_chunk_fwd(q, k, v, logf, st, b_s, k_s, v_s):
    C, U = HG_CHUNK, HG_SUB
    tri = _tri(C)
    b = [_dot_exact(tri, logf[h]) for h in _HEADS]
    for h in _HEADS:
        b_s[h] = b[h]
        k_s[h] = k[h]
        v_s[h] = v[h]
    o = [_dot_nt(q[h] * jnp.exp(b[h]), st[h]) for h in _HEADS]
    bl = [b[h][C - 1:C] for h in _HEADS]
    upd = [_dot_tn(v[h], k[h] * jnp.exp(bl[h] - b[h])) for h in _HEADS]
    rows = lax.broadcasted_iota(jnp.int32, (C, HG_DK), 0)
    nmask = _row_masks()
    outs = [[] for _ in _HEADS]
    for i in range(C // U):
        lo = i * U
        b_i = [b[h][lo:lo + U] for h in _HEADS]
        q_i = [q[h][lo:lo + U] for h in _HEADS]
        o_i = [o[h][lo:lo + U] for h in _HEADS]
        if i > 0:
            qe = [q_i[h] * jnp.exp(b_i[h] - b_i[h][0:1]) for h in _HEADS]
            ke = [jnp.where(rows < lo, k[h] * jnp.exp(jnp.minimum(b_i[h][0:1] - b[h], 0.0)), 0.0) for h in _HEADS]
            att = [_dot_nt(qe[h], ke[h]) for h in _HEADS]
            off = [_dot(att[h], v[h]) for h in _HEADS]
            o_i = [o_i[h] + off[h] for h in _HEADS]
        pieces = [[o_i[h][8 * f:8 * f + 8] for f in range(U // 8)] for h in _HEADS]
        for s in range(U):
            for f in range(s // 8, U // 8):
                for h in _HEADS:
                    bs = b_s[h, lo + s:lo + s + 1, :]
                    ks = k_s[h, lo + s:lo + s + 1, :]
                    vs = v_s[h, lo + s:lo + s + 1, :]
                    arg = b_i[h][8 * f:8 * f + 8] - bs
                    if s > 8 * f:
                        arg = arg + nmask[s - 8 * f]
                    w = _rowsum(q_i[h][8 * f:8 * f + 8] * jnp.exp(arg) * ks)
                    pieces[h][f] = pieces[h][f] + w * vs
        for h in _HEADS:
            outs[h] += pieces[h]
    o = [jnp.concatenate(outs[h], axis=0) for h in _HEADS]
    st_new = [st[h] * jnp.exp(bl[h]) + upd[h] for h in _HEADS]
    return o, st_new, b


def _hg_post(o, gp, ng):
    rs = lax.rsqrt(jnp.mean(o * o, axis=1, keepdims=True) + EPS)
    sg = _sigmoid(gp)
    return o * rs * ng * sg, rs, sg


def hgrn_fwd(proj, lbrows, ng, y_width, *, rows, gather=()):
    S = proj.shape[0]
    C = HG_CHUNK
    cpb = rows // C
    nb = S // rows
    ng_ = len(gather)

    def body(qp_ref, z_ref, v_ref, gp_ref, lb_ref, ng_ref, *rest):
        w_refs, rest = rest[:ng_], rest[ng_:]
        y_ref, o_ref, st_ref = rest[:3]
        g_refs, rest = rest[3:3 + ng_], rest[3 + ng_:]
        st, b_s, k_s, v_s = rest[:4]
        sems = rest[4:]
        comms = [_Gather(w_refs[i], g_refs[i], *sems[3 * i:3 * i + 3]) for i in range(ng_)]

        @pl.when(pl.program_id(0) == 0)
        def _():
            st[...] = jnp.zeros_like(st)
            for comm in comms:
                comm.start()

        a, c, oml = lb_ref[0:1, :], lb_ref[1:2, :], lb_ref[2:3, :]
        ngr = ng_ref[...]

        def chunk(ci, carry):
            off = pl.multiple_of(ci * C, C)
            sl = pl.ds(off, C)
            for h in _HEADS:
                st_ref[h, ci] = st[h]
            gates = [_hg_gates(qp_ref[sl, _lanes(h)], z_ref[sl, _lanes(h)],
                               _head(a, h), _head(c, h), _head(oml, h)) for h in _HEADS]
            q = [g[0] for g in gates]
            logf = [g[3] for g in gates]
            k = [g[5] for g in gates]
            v = [v_ref[sl, _lanes(h)] for h in _HEADS]
            o, st_new, _ = _hg_chunk_fwd(q, k, v, logf, [st[h] for h in _HEADS], b_s, k_s, v_s)
            for h in _HEADS:
                y, _, _ = _hg_post(o[h], gp_ref[sl, _lanes(h)], _head(ngr, h))
                y_ref[sl, _lanes(h)] = y.astype(y_ref.dtype)
                o_ref[sl, _lanes(h)] = o[h]
                st[h] = st_new[h]
            return carry

        lax.fori_loop(0, cpb, chunk, 0)

        for comm in comms:
            pl.when(pl.program_id(0) == nb - 1)(comm.finish)

    col = lambda kblk: pl.BlockSpec((rows, HG_WIDTH), lambda r: (r, kblk))
    in_specs = [col(0), col(1), col(2), col(3), _const_spec((8, HG_WIDTH)), _const_spec((1, HG_WIDTH))]
    out_specs = [col(0), col(0), pl.BlockSpec((HG_HEADS, cpb, HG_DK, HG_DK), lambda r: (0, r, 0, 0))]
    out_shape = [jax.ShapeDtypeStruct((S, y_width), BF16),
                 jax.ShapeDtypeStruct((S, HG_WIDTH), F32),
                 jax.ShapeDtypeStruct((HG_HEADS, S // C, HG_DK, HG_DK), F32)]
    scratch = [pltpu.VMEM((HG_HEADS, HG_DK, HG_DK), F32)] + [pltpu.VMEM((HG_HEADS, C, HG_DK), F32)] * 3
    args = [proj, proj, proj, proj, lbrows, ng]
    for w in gather:
        in_specs.append(ANY)
        out_specs.append(ANY)
        out_shape.append(jax.ShapeDtypeStruct((N_CHIPS,) + w.shape, w.dtype))
        scratch += _Gather.SCRATCH
        args.append(w)
    return pl.pallas_call(
        body,
        name="hgrn_fwd" if not gather else "hgrn_fwd_gather%d" % ng_,
        grid=(nb,),
        in_specs=in_specs,
        out_specs=out_specs,
        out_shape=out_shape,
        scratch_shapes=scratch,
        compiler_params=_params("arbitrary"),
    )(*args)


def hgrn_bwd(proj, o_raw, states, dy, lbrows, ng, *, rows, exchange=()):
    S = proj.shape[0]
    C, U = HG_CHUNK, HG_SUB
    cpb = rows // C
    nb = S // rows
    nx = len(exchange)

    def body(qp_ref, z_ref, v_ref, gp_ref, o_ref, st_ref, dy_ref, lb_ref, ng_ref, *rest):
        q_refs, rest = rest[:nx], rest[nx:]
        dqp_ref, dz_ref, dv_ref, dgp_ref, dlb_ref, dng_ref = rest[:6]
        r_refs, rest = rest[6:6 + nx], rest[6 + nx:]
        dst, b_s, k_s, v_s, dbs, dks, dvs = rest[:7]
        sems = rest[7:]
        comms = [_Exchange(q_refs[i], r_refs[i], *sems[3 * i:3 * i + 3]) for i in range(nx)]

        @pl.when(pl.program_id(0) == 0)
        def _():
            dst[...] = jnp.zeros_like(dst)
            dlb_ref[...] = jnp.zeros_like(dlb_ref)
            dng_ref[...] = jnp.zeros_like(dng_ref)
            for comm in comms:
                comm.start()

        a, c, oml = lb_ref[0:1, :], lb_ref[1:2, :], lb_ref[2:3, :]
        ngr = ng_ref[...]
        rows_i = lax.broadcasted_iota(jnp.int32, (C, HG_DK), 0)
        nmask = _row_masks()
        tri = _tri(C)
        H = _HEADS

        def chunk(cj, carry):
            ci = cpb - 1 - cj
            off = pl.multiple_of(ci * C, C)
            sl = pl.ds(off, C)
            qp = [qp_ref[sl, _lanes(h)] for h in H]
            v = [v_ref[sl, _lanes(h)] for h in H]
            st = [st_ref[h, ci] for h in H]
            gates = [_hg_gates(qp[h], z_ref[sl, _lanes(h)], _head(a, h), _head(c, h), _head(oml, h)) for h in H]
            q, sq, t, logf, snz, k = ([g[j] for g in gates] for j in range(6))
            b = [_dot_exact(tri, logf[h]) for h in H]
            for h in H:
                b_s[h] = b[h]
                k_s[h] = k[h]
                v_s[h] = v[h]
            do = []
            for h in H:
                o = o_ref[sl, _lanes(h)]
                dyv = dy_ref[sl, _lanes(h)]
                ngh = _head(ngr, h)
                rs = lax.rsqrt(jnp.mean(o * o, axis=1, keepdims=True) + EPS)
                sg = _sigmoid(gp_ref[sl, _lanes(h)])
                xh = o * rs
                dgp_ref[sl, _lanes(h)] = (dyv * xh * ngh * sg * (1.0 - sg)).astype(dgp_ref.dtype)
                don = dyv * sg
                dng_ref[0:1, _lanes(h)] += _colsum(don * xh)
                dxh = don * ngh
                do.append(rs * (dxh - xh * jnp.mean(dxh * xh, axis=1, keepdims=True)))
            eb = [jnp.exp(b[h]) for h in H]
            qb = [q[h] * eb[h] for h in H]
            dstv = [dst[h] for h in H]
            bl = [b[h][C - 1:C] for h in H]
            el = [jnp.exp(bl[h]) for h in H]
            ex = [jnp.exp(bl[h] - b[h]) for h in H]
            kd = [k[h] * ex[h] for h in H]
            dqb = [_dot(do[h], st[h]) for h in H]
            dst_acc = [_dot_tn(do[h], qb[h]) for h in H]
            dv0 = [_dot_nt(kd[h], dstv[h]) for h in H]
            dkd = [_dot(v[h], dstv[h]) for h in H]
            dq = [dqb[h] * eb[h] for h in H]
            for h in H:
                g2 = dkd[h] * kd[h]
                dbl = _colsum(dstv[h] * st[h]) * el[h] + _colsum(g2)
                dst[h] = dstv[h] * el[h] + dst_acc[h]
                dbs[h] = dqb[h] * qb[h] - g2
                dks[h] = dkd[h] * ex[h]
                dvs[h] = dv0[h]
                dbs[h, C - 1:C, :] += dbl
            dq_parts = [[] for _ in H]
            for i in range(C // U):
                lo = i * U
                b_i = [b[h][lo:lo + U] for h in H]
                q_i = [q[h][lo:lo + U] for h in H]
                do_i = [do[h][lo:lo + U] for h in H]
                dq_i = [dq[h][lo:lo + U] for h in H]
                db_i = [jnp.zeros((U, HG_DK), F32) for _ in H]
                if i > 0:
                    e1 = [jnp.exp(b_i[h] - b_i[h][0:1]) for h in H]
                    qe = [q_i[h] * e1[h] for h in H]
                    e2 = [jnp.where(rows_i < lo, jnp.exp(jnp.minimum(b_i[h][0:1] - b[h], 0.0)), 0.0) for h in H]
                    ke = [k[h] * e2[h] for h in H]
                    att = [_dot_nt(qe[h], ke[h]) for h in H]
                    datt = [_dot_nt(do_i[h], v[h]) for h in H]
                    dv_add = [_dot_tn(att[h], do_i[h]) for h in H]
                    dqe = [_dot(datt[h], ke[h]) for h in H]
                    dke = [_dot_tn(datt[h], qe[h]) for h in H]
                    for h in H:
                        dvs[h] += dv_add[h]
                        dq_i[h] = dq_i[h] + dqe[h] * e1[h]
                        g = dqe[h] * qe[h]
                        db_i[h] = db_i[h] + g
                        gk = dke[h] * ke[h]
                        dks[h] += dke[h] * e2[h]
                        dbs[h] -= gk
                        dbs[h, lo:lo + 1, :] += _colsum(gk) - _colsum(g)
                nf = U // 8
                dq8 = [[dq_i[h][8 * f:8 * f + 8] for f in range(nf)] for h in H]
                db8 = [[db_i[h][8 * f:8 * f + 8] for f in range(nf)] for h in H]
                key_v = [[] for _ in H]
                key_k = [[] for _ in H]
                key_b = [[] for _ in H]
                for s in range(U):
                    row = slice(lo + s, lo + s + 1)
                    for h in H:
                        bs = b_s[h, row, :]
                        ks = k_s[h, row, :]
                        vs = v_s[h, row, :]
                        tv = tk = tb = None
                        for f in range(s // 8, nf):
                            p8 = slice(8 * f, 8 * f + 8)
                            arg = b_i[h][p8] - bs
                            if s > 8 * f:
                                arg = arg + nmask[s - 8 * f]
                            dec = jnp.exp(arg)
                            qd = q_i[h][p8] * dec
                            y_ = qd * ks
                            w = _rowsum(y_)
                            dw = _rowsum(do_i[h][p8] * vs)
                            g = dw * y_
                            dq8[h][f] = dq8[h][f] + dw * dec * ks
                            db8[h][f] = db8[h][f] + g
                            cv, ck = w * do_i[h][p8], dw * qd
                            tv, tk, tb = (cv, ck, g) if tv is None else (tv + cv, tk + ck, tb + g)
                        key_v[h].append(tv)
                        key_k[h].append(tk)
                        key_b[h].append(tb)
                for h in H:
                    for f in range(nf):
                        r8 = slice(lo + 8 * f, lo + 8 * f + 8)
                        dvs[h, r8, :] += _colsum8(key_v[h][8 * f:8 * f + 8])
                        dks[h, r8, :] += _colsum8(key_k[h][8 * f:8 * f + 8])
                        dbs[h, r8, :] += db8[h][f] - _colsum8(key_b[h][8 * f:8 * f + 8])
                    dq_parts[h] += dq8[h]
            dlogf = [_dot_exact(tri, dbs[h], (((0,), (0,)), ((), ()))) for h in H]
            for h in H:
                dqh = jnp.concatenate(dq_parts[h], axis=0)
                dk = dks[h]
                ah, omlh = _head(a, h), _head(oml, h)
                pa = jnp.exp(ah - logf[h])
                pt = jnp.exp(t[h] - logf[h])
                dt = dlogf[h] * pt
                dlb_ref[0:1, _lanes(h)] += _colsum(dlogf[h] * pa)
                dlb_ref[1:2, _lanes(h)] += _colsum(dt)
                dlb_ref[2:3, _lanes(h)] += _colsum(dk * snz[h])
                dz = dt * snz[h] - dk * omlh * snz[h] * (1.0 - snz[h])
                dqp = dqh * (sq[h] + qp[h] * sq[h] * (1.0 - sq[h]))
                dqp_ref[sl, _lanes(h)] = dqp.astype(dqp_ref.dtype)
                dz_ref[sl, _lanes(h)] = dz.astype(dz_ref.dtype)
                dv_ref[sl, _lanes(h)] = dvs[h].astype(dv_ref.dtype)
            return carry

        lax.fori_loop(0, cpb, chunk, 0)

        for comm in comms:
            pl.when(pl.program_id(0) == nb - 1)(comm.finish)

    rev = lambda r: nb - 1 - r
    col = lambda kblk: pl.BlockSpec((rows, HG_WIDTH), lambda r: (rev(r), kblk))
    acc = _const_spec((8, HG_WIDTH))
    in_specs = [col(0), col(1), col(2), col(3), col(0),
                pl.BlockSpec((HG_HEADS, cpb, HG_DK, HG_DK), lambda r: (0, rev(r), 0, 0)),
                col(0), acc, _const_spec((1, HG_WIDTH))]
    out_specs = [col(0)] * 4 + [acc, acc]
    out_shape = [jax.ShapeDtypeStruct((S, HG_WIDTH), BF16)] * 4 + [jax.ShapeDtypeStruct((8, HG_WIDTH), F32)] * 2
    scratch = [pltpu.VMEM((HG_HEADS, HG_DK, HG_DK), F32)] + [pltpu.VMEM((HG_HEADS, C, HG_DK), F32)] * 6
    args = [proj, proj, proj, proj, o_raw, states, dy, lbrows, ng]
    for q in exchange:
        in_specs.append(ANY)
        out_specs.append(ANY)
        out_shape.append(jax.ShapeDtypeStruct(q.shape, q.dtype))
        scratch += _Exchange.SCRATCH
        args.append(q)
    return pl.pallas_call(
        body,
        name="hgrn_bwd" if not exchange else "hgrn_bwd_exchange%d" % nx,
        grid=(nb,),
        in_specs=in_specs,
        out_specs=out_specs,
        out_shape=out_shape,
        scratch_shapes=scratch,
        compiler_params=_params("arbitrary"),
    )(*args)


GROUP_LANES = AT_GROUP * WINDOW


def swa_mask():
    W = WINDOW
    kpos = lax.broadcasted_iota(jnp.int32, (2, 2 * W, GROUP_LANES), 1)
    qpos = (lax.broadcasted_iota(jnp.int32, (2, 2 * W, GROUP_LANES), 2) & (W - 1)) + W
    first = lax.broadcasted_iota(jnp.int32, (2, 2 * W, GROUP_LANES), 0) == 0
    rel = qpos - kpos
    valid = (rel >= 0) & (rel < W) & jnp.logical_not(first & (kpos < W))
    return jnp.where(valid, 0.0, MASK_VALUE).astype(F32)


def _group_lanes(xt, g):
    Dh = AT_HEAD_DIM
    return jnp.concatenate([xt[(g * AT_GROUP + j) * Dh:(g * AT_GROUP + j + 1) * Dh] for j in range(AT_GROUP)],
                           axis=1)


SWA_SCALE = AT_HEAD_DIM ** -0.5


def _swa_softmax_t(s, sink_row):
    m = jnp.maximum(jnp.max(s, axis=0, keepdims=True), sink_row)
    e = jnp.exp(s - m)
    es = jnp.exp(sink_row - m)
    inv = 1.0 / (_colsum(e) + es)
    return e * inv, es * inv


SWA_BLOCKS = 4
_BG = [(b, g) for b in range(SWA_BLOCKS) for g in range(AT_KV_HEADS)]


def _swa_specs(col_q):
    W = WINDOW
    rows = SWA_BLOCKS * W
    prev = lambda n: jnp.maximum(SWA_BLOCKS * n - 1, 0)
    return [pl.BlockSpec((rows, AT_WIDTH), lambda n: (n, col_q)),
            pl.BlockSpec((W, 128), lambda n: (prev(n), 20)),
            pl.BlockSpec((rows, 128), lambda n: (n, 20)),
            pl.BlockSpec((W, 128), lambda n: (prev(n), 21)),
            pl.BlockSpec((rows, 128), lambda n: (n, 21)),
            _const_spec((8 * AT_KV_HEADS, GROUP_LANES)),
            _const_spec((2, 2 * W, GROUP_LANES))]


def _swa_operands(n, q_ref, kp_ref, k_ref, vp_ref, v_ref, sk_ref, mask_ref):
    W, Dh = WINDOW, AT_HEAD_DIM
    k_all = jnp.concatenate([kp_ref[...], k_ref[...]], axis=0)
    v_all = jnp.concatenate([vp_ref[...], v_ref[...]], axis=0)
    kk = [k_all[b * W:(b + 2) * W] for b in range(SWA_BLOCKS)]
    vv = [v_all[b * W:(b + 2) * W] for b in range(SWA_BLOCKS)]
    masks = [mask_ref[jnp.minimum(n, 1)]] + [mask_ref[1]] * (SWA_BLOCKS - 1)
    qt = [(q_ref[b * W:(b + 1) * W, :] * SWA_SCALE).T for b in range(SWA_BLOCKS)]
    kg = {(b, g): kk[b][:, g * Dh:(g + 1) * Dh] for b, g in _BG}
    qg = {(b, g): _group_lanes(qt[b], g) for b, g in _BG}
    s = {bg: _dot(kg[bg], qg[bg]) + masks[bg[0]] for bg in _BG}
    sink = {(b, g): sk_ref[8 * g:8 * g + 1, :] for b, g in _BG}
    return kk, vv, kg, qg, s, sink


def swa_fwd(proj, sink_rows, mask, y):
    S = proj.shape[0]
    W, Dh = WINDOW, AT_HEAD_DIM
    rows = SWA_BLOCKS * W

    def body(q_ref, kp_ref, k_ref, vp_ref, v_ref, sk_ref, mask_ref, y_in, y_ref):
        del y_in
        _, vv, _, _, s, sink = _swa_operands(pl.program_id(0), q_ref, kp_ref, k_ref, vp_ref, v_ref,
                                             sk_ref, mask_ref)
        vt = [v.T for v in vv]
        p = {bg: _swa_softmax_t(s[bg], sink[bg])[0] for bg in _BG}
        ot = {(b, g): _dot(vt[b][g * Dh:(g + 1) * Dh], p[b, g]) for b, g in _BG}
        for b in range(SWA_BLOCKS):
            outs = [ot[b, g][:, j * W:(j + 1) * W] for g in range(AT_KV_HEADS) for j in range(AT_GROUP)]
            y_ref[b * W:(b + 1) * W, :] = jnp.concatenate(outs, axis=0).T.astype(y_ref.dtype)

    return pl.pallas_call(
        body,
        name="swa_fwd",
        grid=(S // rows,),
        in_specs=_swa_specs(4) + [pl.BlockSpec(memory_space=pl.ANY)],
        out_specs=pl.BlockSpec((rows, AT_WIDTH), lambda n: (n, 1)),
        out_shape=jax.ShapeDtypeStruct(y.shape, y.dtype),
        input_output_aliases={7: 0},
        compiler_params=_params("parallel"),
    )(proj, proj, proj, proj, proj, sink_rows, mask, y)


def swa_bwd(proj, sink_rows, mask, dy):
    S = proj.shape[0]
    W, Dh = WINDOW, AT_HEAD_DIM
    rows = SWA_BLOCKS * W
    nsteps = S // rows

    def body(q_ref, kp_ref, k_ref, vp_ref, v_ref, sk_ref, mask_ref, dy_ref,
             dq_ref, dko_ref, dkp_ref, dvo_ref, dvp_ref, dsk_ref):
        n = pl.program_id(0)

        @pl.when(n == 0)
        def _():
            dsk_ref[...] = jnp.zeros_like(dsk_ref)

        kk, vv, _, qg, s, sink = _swa_operands(n, q_ref, kp_ref, k_ref, vp_ref, v_ref, sk_ref, mask_ref)
        kt = [k.T for k in kk]
        dot_ = [dy_ref[b * W:(b + 1) * W, :].T for b in range(SWA_BLOCKS)]
        dog = {(b, g): _group_lanes(dot_[b], g) for b, g in _BG}
        dp = {(b, g): _dot(vv[b][:, g * Dh:(g + 1) * Dh], dog[b, g]) for b, g in _BG}
        pp = {bg: _swa_softmax_t(s[bg], sink[bg]) for bg in _BG}
        delta = {bg: _colsum(dp[bg] * pp[bg][0]) for bg in _BG}
        ds = {bg: pp[bg][0] * (dp[bg] - delta[bg]) for bg in _BG}
        dqt = {(b, g): _dot(kt[b][g * Dh:(g + 1) * Dh], ds[b, g]) * SWA_SCALE for b, g in _BG}
        dk = {bg: _dot_nt(ds[bg], qg[bg]) for bg in _BG}
        dv = {bg: _dot_nt(pp[bg][0], dog[bg]) for bg in _BG}
        for g in range(AT_KV_HEADS):
            tot = -(pp[0, g][1] * delta[0, g])
            for b in range(1, SWA_BLOCKS):
                tot = tot - pp[b, g][1] * delta[b, g]
            dsk_ref[8 * g:8 * g + 1, :] += tot
        for b in range(SWA_BLOCKS):
            r = slice(b * W, (b + 1) * W)
            dqs = [dqt[b, g][:, j * W:(j + 1) * W] for g in range(AT_KV_HEADS) for j in range(AT_GROUP)]
            dq_ref[r, :] = jnp.concatenate(dqs, axis=0).T.astype(dq_ref.dtype)
            dkb = jnp.concatenate([dk[b, g] for g in range(AT_KV_HEADS)], axis=1)
            dvb = jnp.concatenate([dv[b, g] for g in range(AT_KV_HEADS)], axis=1)
            dkp_ref[r, :] = dkb[:W]
            dko_ref[r, :] = dkb[W:]
            dvp_ref[r, :] = dvb[:W]
            dvo_ref[r, :] = dvb[W:]

        @pl.when(n == nsteps - 1)
        def _():
            for g in range(AT_KV_HEADS):
                for j in range(AT_GROUP):
                    tot = _rowsum(dsk_ref[8 * g:8 * g + 1, j * W:(j + 1) * W])
                    dsk_ref[8 * g + 1 + j:8 * g + 2 + j, :] = jnp.broadcast_to(tot, (1, GROUP_LANES))

    kv = pl.BlockSpec((rows, 128), lambda n: (n, 0))
    sk = _const_spec((8 * AT_KV_HEADS, GROUP_LANES))
    return pl.pallas_call(
        body,
        name="swa_bwd",
        grid=(nsteps,),
        in_specs=_swa_specs(4) + [pl.BlockSpec((rows, AT_WIDTH), lambda n: (n, 1))],
        out_specs=[pl.BlockSpec((rows, AT_WIDTH), lambda n: (n, 0)), kv, kv, kv, kv, sk],
        out_shape=[jax.ShapeDtypeStruct((S, AT_WIDTH), BF16)]
                  + [jax.ShapeDtypeStruct((S, 128), F32)] * 4
                  + [jax.ShapeDtypeStruct((8 * AT_KV_HEADS, GROUP_LANES), F32)],
        compiler_params=_params("arbitrary"),
    )(proj, proj, proj, proj, proj, sink_rows, mask, dy)


def assemble_dproj(hg_grads, dq_at, dko, dkp, dvo, dvp, *, rows):
    S = dq_at.shape[0]
    W = WINDOW
    nb = S // W
    bpr = rows // W

    def body(a0, a1, a2, a3, dq, ko, kp, kpn, vo, vp, vpn, out):
        r = pl.program_id(0)
        for i, a in enumerate((a0, a1, a2, a3)):
            out[:, i * HG_WIDTH:(i + 1) * HG_WIDTH] = a[...]
        base = 4 * HG_WIDTH
        out[:, base:base + AT_WIDTH] = dq[...]
        last = (r == pl.num_programs(0) - 1)
        for off, own, pv, pvn in ((base + AT_WIDTH, ko, kp, kpn), (base + AT_WIDTH + 128, vo, vp, vpn)):
            if bpr > 1:
                out[0:rows - W, off:off + 128] = (own[0:rows - W, :] + pv[W:rows, :]).astype(out.dtype)
            nxt = jnp.where(last, 0.0, pvn[...])
            out[rows - W:rows, off:off + 128] = (own[rows - W:rows, :] + nxt).astype(out.dtype)

    hg = pl.BlockSpec((rows, HG_WIDTH), lambda r: (r, 0))
    blk = pl.BlockSpec((rows, 128), lambda r: (r, 0))
    nxt = pl.BlockSpec((W, 128), lambda r: (jnp.minimum((r + 1) * bpr, nb - 1), 0))
    return pl.pallas_call(
        body,
        name="assemble_dproj",
        grid=(S // rows,),
        in_specs=[hg, hg, hg, hg, pl.BlockSpec((rows, AT_WIDTH), lambda r: (r, 0)),
                  blk, blk, nxt, blk, blk, nxt],
        out_specs=pl.BlockSpec((rows, IN_WIDTH), lambda r: (r, 0)),
        out_shape=jax.ShapeDtypeStruct((S, IN_WIDTH), BF16),
        compiler_params=_params("parallel"),
    )(*hg_grads, dq_at, dko, dkp, dkp, dvo, dvp, dvp)


ROW_TILE = 512
COL_TILE = 1408


def _col_tile(n):
    return n if n <= COL_TILE else COL_TILE


def _rms_scale(x):
    return lax.rsqrt(jnp.mean(x * x, axis=1, keepdims=True) + EPS)


def _rms_bwd(d, x, g):
    rs = _rms_scale(x)
    xh = x * rs
    dxh = d * g
    return rs * (dxh - xh * jnp.mean(dxh * xh, axis=1, keepdims=True)), _colsum(d * xh)


def mm(a, b, *, nt=False, out_dtype=F32, res=None, norm_g=None, rms_bwd=None, name):
    parts = a if isinstance(a, tuple) else (a,)
    M, K = parts[0].shape
    N = b.shape[0] if nt else b.shape[1]
    tall = K <= D_MODEL and rms_bwd is None and len(parts) == 1 and M % (2 * ROW_TILE) == 0
    tm = 2 * ROW_TILE if tall else min(ROW_TILE, M)
    tn = _col_tile(N)
    whole_rows = norm_g is not None or rms_bwd is not None
    assert M % tm == 0 and N % tn == 0 and (tn == N or not whole_rows)
    np_ = len(parts)

    def body(*refs):
        a_refs, b_refs, rest = refs[:np_], refs[np_:2 * np_], refs[2 * np_:]
        dot = _dot_nt if nt else _dot
        acc = dot(a_refs[0][...], b_refs[0][...])
        for ar, br in zip(a_refs[1:], b_refs[1:]):
            acc = acc + dot(ar[...], br[...])
        if rms_bwd is not None:
            h_ref, g_ref, dr_ref, dh_ref, dhb_ref, dg_ref = rest

            @pl.when(pl.program_id(1) == 0)
            def _():
                dg_ref[...] = jnp.zeros_like(dg_ref)

            dx, dgp = _rms_bwd(acc, h_ref[...], g_ref[...])
            dg_ref[0:1, :] += dgp
            dh = dr_ref[...] + dx
            dh_ref[...] = dh
            dhb_ref[...] = dh.astype(BF16)
            return
        rest = list(rest)
        if res is not None:
            acc = acc + rest.pop(0)[...]
        if norm_g is not None:
            g_ref = rest.pop(0)
            rest[1][...] = (acc * _rms_scale(acc) * g_ref[...]).astype(BF16)
        rest[0][...] = acc.astype(rest[0].dtype)

    row = pl.BlockSpec((tm, tn), lambda j, i: (i, j))
    in_specs = [pl.BlockSpec((tm, K), lambda j, i: (i, 0)) for _ in parts]
    for kb in range(np_):
        in_specs.append(pl.BlockSpec((tn, K), lambda j, i, kb=kb: (j, kb)) if nt
                        else pl.BlockSpec((K, tn), lambda j, i, kb=kb: (kb, j)))
    args = list(parts) + [b] * np_
    if rms_bwd is not None:
        h, g, dres = rms_bwd
        in_specs += [row, _const_spec((1, N)), row]
        args += [h, g, dres]
        out_specs = [row, row, _const_spec((8, N))]
        out_shape = [jax.ShapeDtypeStruct((M, N), F32), jax.ShapeDtypeStruct((M, N), BF16),
                     jax.ShapeDtypeStruct((8, N), F32)]
        sem = ("arbitrary", "arbitrary")
    else:
        if res is not None:
            in_specs.append(row)
            args.append(res)
        out_specs, out_shape = [row], [jax.ShapeDtypeStruct((M, N), out_dtype)]
        if norm_g is not None:
            in_specs.append(_const_spec((1, N)))
            args.append(norm_g)
            out_specs.append(row)
            out_shape.append(jax.ShapeDtypeStruct((M, N), BF16))
        sem = ("parallel", "parallel")
    out = pl.pallas_call(
        body,
        name=name,
        grid=(N // tn, M // tm),
        in_specs=in_specs,
        out_specs=out_specs,
        out_shape=out_shape,
        compiler_params=_params(*sem),
    )(*args)
    return out[0] if len(out) == 1 else out


def mm_tn(a, b, *, name):
    M, K = a.shape
    N = b.shape[1]
    tm = next((t for t in (4 * ROW_TILE, 2 * ROW_TILE) if M % t == 0), min(ROW_TILE, M))
    tk = _col_tile(K)
    tn = _col_tile(N)
    assert M % tm == 0 and K % tk == 0 and N % tn == 0

    def body(a_ref, b_ref, o_ref):
        @pl.when(pl.program_id(2) == 0)
        def _():
            o_ref[...] = jnp.zeros_like(o_ref)

        o_ref[...] += _dot_tn(a_ref[...], b_ref[...])

    return pl.pallas_call(
        body,
        name=name,
        grid=(K // tk, N // tn, M // tm),
        in_specs=[pl.BlockSpec((tm, tk), lambda k, j, i: (i, k)),
                  pl.BlockSpec((tm, tn), lambda k, j, i: (i, j))],
        out_specs=pl.BlockSpec((tk, tn), lambda k, j, i: (k, j)),
        out_shape=jax.ShapeDtypeStruct((K, N), F32),
        compiler_params=_params("parallel", "parallel", "arbitrary"),
    )(a, b)


def _row_spec(tm, width):
    return pl.BlockSpec((tm, width), lambda i: (i, 0))


def _const_spec(shape):
    return pl.BlockSpec(shape, lambda *_: (0,) * len(shape))


def rmsnorm_fwd(h, g, *, name):
    S, D = h.shape
    tm = min(ROW_TILE, S)

    def body(h_ref, g_ref, u_ref):
        x = h_ref[...]
        rs = lax.rsqrt(jnp.mean(x * x, axis=1, keepdims=True) + EPS)
        u_ref[...] = (x * rs * g_ref[...]).astype(u_ref.dtype)

    return pl.pallas_call(
        body, name=name, grid=(S // tm,),
        in_specs=[_row_spec(tm, D), _const_spec((1, D))],
        out_specs=_row_spec(tm, D),
        out_shape=jax.ShapeDtypeStruct((S, D), BF16),
        compiler_params=_params("parallel"),
    )(h, g)


HALO = 16


def _shift_down(x, edge8, s):
    sh = pltpu.roll(x, s, 0)
    er = pltpu.roll(edge8, s, 0)
    row8 = lax.broadcasted_iota(jnp.int32, er.shape, 0)
    top = jnp.where(row8 < s, er, sh[0:8])
    return jnp.concatenate([top, sh[8:]], axis=0)


def _shift_up(x, s):
    return pltpu.roll(x, x.shape[0] - s, 0)


def _conv_pre(a, prev8, w_ref, cb_ref):
    a1 = _shift_down(a, prev8, 1)
    a2 = _shift_down(a, prev8, 2)
    return w_ref[2:3, :] * a + w_ref[1:2, :] * a1 + w_ref[0:1, :] * a2 + cb_ref[...]


def convffn_fwd(hh, cw8, cb):
    S = hh.shape[0]
    tm = min(ROW_TILE, S)
    tn = _col_tile(D_FF)
    nj = D_FF // tn

    def body(a_ref, ap_ref, b_ref, w_ref, cb_ref, o_ref, ac_ref):
        prev8 = jnp.where(pl.program_id(1) == 0, 0.0, ap_ref[...].astype(F32)[HALO - 8:HALO])
        ac = _conv_pre(a_ref[...].astype(F32), prev8, w_ref, cb_ref)
        ac_ref[...] = ac.astype(ac_ref.dtype)
        o_ref[...] = (ac * _sigmoid(ac) * b_ref[...].astype(F32)).astype(o_ref.dtype)

    rh = tm // HALO
    return pl.pallas_call(
        body, name="convffn_fwd", grid=(nj, S // tm),
        in_specs=[pl.BlockSpec((tm, tn), lambda j, i: (i, j)),
                  pl.BlockSpec((HALO, tn), lambda j, i: (jnp.maximum(i * rh - 1, 0), j)),
                  pl.BlockSpec((tm, tn), lambda j, i: (i, j + nj)),
                  pl.BlockSpec((8, tn), lambda j, i: (0, j)),
                  pl.BlockSpec((1, tn), lambda j, i: (0, j))],
        out_specs=[pl.BlockSpec((tm, tn), lambda j, i: (i, j))] * 2,
        out_shape=[jax.ShapeDtypeStruct((S, D_FF), BF16)] * 2,
        compiler_params=_params("parallel", "parallel"),
    )(hh, hh, hh, cw8, cb)


def convffn_bwd(hh, conv, dact, cw8):
    S = hh.shape[0]
    tm = min(ROW_TILE, S)
    tn = _col_tile(D_FF)
    nj = D_FF // tn
    ni = S // tm

    def body(a_ref, b_ref, bn_ref, c_ref, cn_ref, d_ref, dn_ref, w_ref, o_a, o_b, dw_ref):
        i = pl.program_id(1)

        @pl.when(i == 0)
        def _():
            dw_ref[...] = jnp.zeros_like(dw_ref)

        up = lambda r: r[...].astype(F32)
        ext = lambda cur, nxt: jnp.concatenate([up(cur), up(nxt)[0:8]], axis=0)
        b = ext(b_ref, bn_ref)
        ac = ext(c_ref, cn_ref)
        d = jnp.concatenate([up(d_ref), jnp.where(i == ni - 1, 0.0, up(dn_ref)[0:8])], axis=0)
        sa = _sigmoid(ac)
        silu = ac * sa
        o_b[...] = (d[0:tm] * silu[0:tm]).astype(o_b.dtype)
        dac = d * b * (sa + silu * (1.0 - sa))
        dc0 = dac[0:tm]
        dc1 = _shift_up(dac, 1)[0:tm]
        dc2 = _shift_up(dac, 2)[0:tm]
        o_a[...] = (w_ref[2:3, :] * dc0 + w_ref[1:2, :] * dc1 + w_ref[0:1, :] * dc2).astype(o_a.dtype)
        a = up(a_ref)
        dw_ref[0:1, :] += _colsum(dc2 * a)
        dw_ref[1:2, :] += _colsum(dc1 * a)
        dw_ref[2:3, :] += _colsum(dc0 * a)
        dw_ref[3:4, :] += _colsum(dc0)

    rh = tm // HALO
    last = S // HALO - 1
    cur = lambda off: pl.BlockSpec((tm, tn), lambda j, i: (i, j + off))
    nxt = lambda off: pl.BlockSpec((HALO, tn), lambda j, i: (jnp.minimum((i + 1) * rh, last), j + off))
    return pl.pallas_call(
        body, name="convffn_bwd", grid=(nj, ni),
        in_specs=[cur(0), cur(nj), nxt(nj), cur(0), nxt(0), cur(0), nxt(0),
                  pl.BlockSpec((8, tn), lambda j, i: (0, j))],
        out_specs=[cur(0), cur(0), pl.BlockSpec((8, tn), lambda j, i: (0, j))],
        out_shape=[jax.ShapeDtypeStruct((S, D_FF), BF16), jax.ShapeDtypeStruct((S, D_FF), BF16),
                   jax.ShapeDtypeStruct((8, D_FF), F32)],
        compiler_params=_params("parallel", "arbitrary"),
    )(hh, hh, hh, conv, conv, dact, dact, cw8)


def ple_fwd(h, gpre, p, wpu, norm_g):
    S, D = h.shape
    tm = min(ROW_TILE, S)

    def body(h_ref, g_ref, p_ref, w_ref, ng_ref, o_ref, u_ref):
        out = h_ref[...] + _sigmoid(g_ref[...].astype(F32)) * _dot(p_ref[...], w_ref[...])
        o_ref[...] = out
        u_ref[...] = (out * _rms_scale(out) * ng_ref[...]).astype(BF16)

    return pl.pallas_call(
        body, name="ple_fwd", grid=(S // tm,),
        in_specs=[_row_spec(tm, D), _row_spec(tm, D), _row_spec(tm, PLE_DIM), _const_spec((PLE_DIM, D)),
                  _const_spec((1, D))],
        out_specs=[_row_spec(tm, D), _row_spec(tm, D)],
        out_shape=[jax.ShapeDtypeStruct((S, D), F32), jax.ShapeDtypeStruct((S, D), BF16)],
        compiler_params=_params("parallel"),
    )(h, gpre, p, wpu, norm_g)


def ple_bwd(dh, gpre, p, wpu):
    S, D = dh.shape
    tm = min(ROW_TILE, S)

    def body(d_ref, g_ref, p_ref, w_ref, dpu_ref, dg_ref):
        d = d_ref[...]
        gate = _sigmoid(g_ref[...].astype(F32))
        pu = _dot(p_ref[...], w_ref[...])
        dpu_ref[...] = (d * gate).astype(dpu_ref.dtype)
        dg_ref[...] = (d * pu * gate * (1.0 - gate)).astype(dg_ref.dtype)

    return pl.pallas_call(
        body, name="ple_bwd", grid=(S // tm,),
        in_specs=[_row_spec(tm, D), _row_spec(tm, D), _row_spec(tm, PLE_DIM), _const_spec((PLE_DIM, D))],
        out_specs=[_row_spec(tm, D), _row_spec(tm, D)],
        out_shape=[jax.ShapeDtypeStruct((S, D), BF16)] * 2,
        compiler_params=_params("parallel"),
    )(dh, gpre, p, wpu)


def loss_head(h, g, tgt):
    S, D = h.shape
    tm = min(ROW_TILE, S)

    def body(h_ref, g_ref, t_ref, dh_ref, dhb_ref, l_ref, dg_ref):
        @pl.when(pl.program_id(0) == 0)
        def _():
            l_ref[...] = jnp.zeros_like(l_ref)
            dg_ref[...] = jnp.zeros_like(dg_ref)

        x = h_ref[...]
        gr = g_ref[...]
        rs = lax.rsqrt(jnp.mean(x * x, axis=1, keepdims=True) + EPS)
        xh = x * rs
        err = xh * gr - t_ref[...]
        l_ref[0:1, 0:1] += 0.5 * _colsum(jnp.mean(err * err, axis=1, keepdims=True))
        dy = err * (1.0 / D)
        dg_ref[0:1, :] += _colsum(dy * xh)
        dxh = dy * gr
        dh = rs * (dxh - xh * jnp.mean(dxh * xh, axis=1, keepdims=True))
        dh_ref[...] = dh
        dhb_ref[...] = dh.astype(BF16)

    return pl.pallas_call(
        body, name="loss_head", grid=(S // tm,),
        in_specs=[_row_spec(tm, D), _const_spec((1, D)), _row_spec(tm, D)],
        out_specs=[_row_spec(tm, D), _row_spec(tm, D), _const_spec((8, 128)), _const_spec((8, D))],
        out_shape=[jax.ShapeDtypeStruct((S, D), F32), jax.ShapeDtypeStruct((S, D), BF16),
                   jax.ShapeDtypeStruct((8, 128), F32), jax.ShapeDtypeStruct((8, D), F32)],
        compiler_params=_params("arbitrary"),
    )(h, g, tgt)


def _lb_rows(l_ref):
    l = l_ref[...]
    e = jnp.exp(l - jnp.max(l, axis=0, keepdims=True))
    p = e / _colsum(e)
    lbs, run = [], None
    for i in range(DEPTH):
        run = p[i:i + 1] if i == 0 else run + p[i:i + 1]
        lbs.append(run - p[0:1])
    return p, lbs


def lb_fwd(lb_logits):
    def body(l_ref, o_ref):
        _, lbs = _lb_rows(l_ref)
        o_ref[...] = jnp.zeros_like(o_ref)
        for i, lb in enumerate(lbs):
            o_ref[8 * i:8 * i + 1, :] = jnp.log(jnp.maximum(lb, LB_FLOOR))
            o_ref[8 * i + 1:8 * i + 2, :] = jnp.log1p(-lb)
            o_ref[8 * i + 2:8 * i + 3, :] = 1.0 - lb
            o_ref[8 * i + 3:8 * i + 4, :] = lb

    return pl.pallas_call(
        body, name="lb_fwd",
        out_shape=jax.ShapeDtypeStruct((DEPTH * 8, HG_WIDTH), F32),
    )(lb_logits)


def lb_bwd(dlbrows, lb_logits):
    def body(d_ref, l_ref, o_ref):
        p, lbs = _lb_rows(l_ref)
        dlb = []
        for i, lb in enumerate(lbs):
            da = d_ref[8 * i:8 * i + 1, :]
            dc = d_ref[8 * i + 1:8 * i + 2, :]
            do = d_ref[8 * i + 2:8 * i + 3, :]
            dlb.append(jnp.where(lb > LB_FLOOR, da / jnp.maximum(lb, LB_FLOOR), 0.0) - dc / (1.0 - lb) - do)
        dp = [jnp.zeros_like(dlb[0])]
        for j in range(1, DEPTH):
            acc = dlb[j]
            for i in range(j + 1, DEPTH):
                acc = acc + dlb[i]
            dp.append(acc)
        dot_ = p[0:1] * dp[0]
        for j in range(1, DEPTH):
            dot_ = dot_ + p[j:j + 1] * dp[j]
        o_ref[...] = jnp.zeros_like(o_ref)
        for j in range(DEPTH):
            o_ref[j:j + 1, :] = p[j:j + 1] * (dp[j] - dot_)

    return pl.pallas_call(
        body, name="lb_bwd",
        out_shape=jax.ShapeDtypeStruct((8, HG_WIDTH), F32),
    )(dlbrows, lb_logits)


def adamw(w, g, m, v, *, name):
    R, C = w.shape
    tr = next((t for t in (512, 256, 128, 64, 32, 16, 8) if R % t == 0), R)

    def body(w_ref, g_ref, m_ref, v_ref, d_ref, m2_ref, v2_ref):
        gv = g_ref[...]
        m2 = ADAM_B1 * m_ref[...] + (1.0 - ADAM_B1) * gv
        v2 = ADAM_B2 * v_ref[...] + (1.0 - ADAM_B2) * (gv * gv)
        mh = m2 / (1.0 - ADAM_B1 ** ADAM_STEP)
        vh = v2 / (1.0 - ADAM_B2 ** ADAM_STEP)
        d_ref[...] = -ADAM_LR * (mh / (jnp.sqrt(vh) + ADAM_EPS) + ADAM_WD * w_ref[...])
        m2_ref[...] = m2
        v2_ref[...] = v2

    spec = pl.BlockSpec((tr, C), lambda i: (i, 0))
    return pl.pallas_call(
        body, name=name, grid=(R // tr,),
        in_specs=[spec] * 4, out_specs=[spec] * 3,
        out_shape=[jax.ShapeDtypeStruct((R, C), F32)] * 3,
        compiler_params=_params("parallel"),
    )(w, g, m, v)


def sum_slots(x, *, out_dtype, name):
    n, R, C = x.shape
    tr = R if R <= 1024 else next((t for t in (848, 768, 704, 672, 512, 448, 352, 256, 128, 64, 16) if R % t == 0), R)

    def body(x_ref, o_ref):
        acc = x_ref[0].astype(F32)
        for k in range(1, n):
            acc = acc + x_ref[k].astype(F32)
        o_ref[...] = acc.astype(o_ref.dtype)

    return pl.pallas_call(
        body, name=name, grid=(R // tr,),
        in_specs=[pl.BlockSpec((n, tr, C), lambda i: (0, i, 0))],
        out_specs=pl.BlockSpec((tr, C), lambda i: (i, 0)),
        out_shape=jax.ShapeDtypeStruct((R, C), out_dtype),
        compiler_params=_params("parallel"),
    )(x)


MESH = pl.DeviceIdType.MESH
ANY = pl.BlockSpec(memory_space=pl.ANY)


def _place():
    return lax.axis_index("x"), lax.axis_index("y"), lax.axis_index("c")


def _other_chips(x, y):
    return [(1 - x, y), (x, 1 - y), (1 - x, 1 - y)]


def small_allgather(buf):
    R, C = buf.shape

    def body(x_ref, out_ref, send_sems, recv_sems, local_sem):
        x, y, c = _place()
        me, sibling = (x, y, c), (x, y, 1 - c)
        chips = _other_chips(x, y)

        def slot(px, py, pc):
            return out_ref.at[4 * px + 2 * py + pc]

        def copy(k, block, to, src=None):
            return pltpu.make_async_remote_copy(
                src_ref=slot(*block) if src is None else src, dst_ref=slot(*block),
                send_sem=send_sems.at[k], recv_sem=recv_sems.at[k],
                device_id=to, device_id_type=MESH)

        mine = pltpu.make_async_copy(x_ref, slot(*me), local_sem)
        mine.start()
        first = [copy(0, me, sibling, src=x_ref)]
        first += [copy(1 + r, me, (*chip, c), src=x_ref) for r, chip in enumerate(chips)]
        for cp in first:
            cp.start()
        passed = [copy(4 + r, (*chip, c), sibling) for r, chip in enumerate(chips)]
        for r, chip in enumerate(chips):
            copy(1 + r, (*chip, c), me).wait_recv()
            passed[r].start()
        copy(0, sibling, me).wait_recv()
        for r, chip in enumerate(chips):
            copy(4 + r, (*chip, 1 - c), me).wait_recv()
        for cp in first + passed:
            cp.wait_send()
        mine.wait()

    return pl.pallas_call(
        body, name="small_allgather",
        out_shape=jax.ShapeDtypeStruct((8, R, C), buf.dtype),
        in_specs=[pl.BlockSpec(memory_space=pltpu.VMEM)],
        out_specs=pl.BlockSpec(memory_space=pltpu.VMEM),
        scratch_shapes=[pltpu.SemaphoreType.DMA((7,)), pltpu.SemaphoreType.DMA((7,)),
                        pltpu.SemaphoreType.DMA],
    )(buf)


def weights_allgather(wp):
    def body(w_ref, g_ref, send_sems, recv_sems, local_sem):
        gather = _Gather(w_ref, g_ref, send_sems, recv_sems, local_sem)
        gather.start()
        gather.finish()

    return pl.pallas_call(
        body, name="weights_allgather",
        out_shape=jax.ShapeDtypeStruct((4,) + wp.shape, wp.dtype),
        in_specs=[ANY], out_specs=ANY,
        scratch_shapes=_Gather.SCRATCH,
    )(wp)


class _Gather:
    SCRATCH = [pltpu.SemaphoreType.DMA((6,)), pltpu.SemaphoreType.DMA((6,)), pltpu.SemaphoreType.DMA]

    def __init__(self, w_ref, g_ref, send_sems, recv_sems, local_sem):
        self.w_ref, self.g_ref, self.local_sem = w_ref, g_ref, local_sem
        self.send_sems, self.recv_sems = send_sems, recv_sems
        self.x, self.y, self.c = _place()
        self.chips = _other_chips(self.x, self.y)
        half = w_ref.shape[0] // 2
        self.mine = pl.ds(pl.multiple_of(self.c * half, 16), half)
        self.theirs = pl.ds(pl.multiple_of((1 - self.c) * half, 16), half)

    def _copy(self, k, chip_block, rows, to, src=None):
        dst = self.g_ref.at[chip_block, rows]
        return pltpu.make_async_remote_copy(
            src_ref=dst if src is None else src, dst_ref=dst,
            send_sem=self.send_sems.at[k], recv_sem=self.recv_sems.at[k],
            device_id=to, device_id_type=MESH)

    def _own(self):
        return pltpu.make_async_copy(self.w_ref, self.g_ref.at[2 * self.x + self.y], self.local_sem)

    def _first(self):
        return [self._copy(r, 2 * self.x + self.y, self.mine, (*chip, self.c), src=self.w_ref.at[self.mine])
                for r, chip in enumerate(self.chips)]

    def start(self):
        self._own().start()
        for cp in self._first():
            cp.start()

    def finish(self):
        sibling = (self.x, self.y, 1 - self.c)
        passed = [self._copy(3 + r, 2 * chip[0] + chip[1], self.mine, sibling) for r, chip in enumerate(self.chips)]
        for r, chip in enumerate(self.chips):
            self._copy(r, 2 * chip[0] + chip[1], self.mine, (*chip, self.c)).wait_recv()
            passed[r].start()
        for r, chip in enumerate(self.chips):
            self._copy(3 + r, 2 * chip[0] + chip[1], self.theirs, sibling).wait_recv()
        for cp in self._first() + passed:
            cp.wait_send()
        self._own().wait()


def sibling_swap(v, *, name):
    def body(v_ref, got_ref, send_sem, recv_sem):
        x, y, c = _place()
        cp = pltpu.make_async_remote_copy(
            src_ref=v_ref, dst_ref=got_ref, send_sem=send_sem, recv_sem=recv_sem,
            device_id=(x, y, 1 - c), device_id_type=MESH)
        cp.start()
        cp.wait()

    return pl.pallas_call(
        body, name=name,
        out_shape=jax.ShapeDtypeStruct(v.shape, v.dtype),
        in_specs=[ANY], out_specs=ANY,
        scratch_shapes=[pltpu.SemaphoreType.DMA, pltpu.SemaphoreType.DMA],
    )(v)


def chip_exchange(q):
    def body(q_ref, r_ref, send_sems, recv_sems, local_sem):
        exchange = _Exchange(q_ref, r_ref, send_sems, recv_sems, local_sem)
        exchange.start()
        exchange.finish()

    return pl.pallas_call(
        body, name="chip_exchange",
        out_shape=jax.ShapeDtypeStruct(q.shape, q.dtype),
        in_specs=[ANY], out_specs=ANY,
        scratch_shapes=_Exchange.SCRATCH,
    )(q)


class _Exchange:
    SCRATCH = [pltpu.SemaphoreType.DMA((3,)), pltpu.SemaphoreType.DMA((3,)), pltpu.SemaphoreType.DMA]

    def __init__(self, q_ref, r_ref, send_sems, recv_sems, local_sem):
        self.q_ref, self.r_ref, self.local_sem = q_ref, r_ref, local_sem
        self.send_sems, self.recv_sems = send_sems, recv_sems
        self.x, self.y, self.c = _place()
        self.j = 2 * self.x + self.y
        self.chips = _other_chips(self.x, self.y)

    def _copy(self, r, src_block, dst_block, chip):
        return pltpu.make_async_remote_copy(
            src_ref=self.q_ref.at[src_block], dst_ref=self.r_ref.at[dst_block],
            send_sem=self.send_sems.at[r], recv_sem=self.recv_sems.at[r],
            device_id=(*chip, self.c), device_id_type=MESH)

    def _own(self):
        return pltpu.make_async_copy(self.q_ref.at[self.j], self.r_ref.at[self.j], self.local_sem)

    def _sends(self):
        return [self._copy(r, 2 * chip[0] + chip[1], self.j, chip) for r, chip in enumerate(self.chips)]

    def start(self):
        self._own().start()
        for cp in self._sends():
            cp.start()

    def finish(self):
        for r, chip in enumerate(self.chips):
            jr = 2 * chip[0] + chip[1]
            self._copy(r, jr, jr, chip).wait_recv()
        for cp in self._sends():
            cp.wait_send()
        self._own().wait()


N_CHIPS = 4
_PACK = (("w_in", 704), ("w_out", 256), ("w_up", 1408), ("w_down", 704), ("w_ple_gate", 256), ("w_ple_up", 64))
LAYER_ROWS = sum(r for _, r in _PACK)
PACK_ROWS = DEPTH * LAYER_ROWS
HALF_ROWS = PACK_ROWS // 2


def _pack_shards(sh):
    parts = []
    for i in range(DEPTH):
        for name, rows in _PACK:
            parts.append(sh[name][i].reshape(rows, D_MODEL))
    return jnp.concatenate(parts, axis=0)


def _unpack_shards(slab):
    shapes = {"w_in": (D_MODEL, IN_WIDTH // N_CHIPS), "w_out": (D_MODEL // N_CHIPS, D_MODEL),
              "w_up": (D_MODEL, 2 * D_FF // N_CHIPS), "w_down": (D_FF // N_CHIPS, D_MODEL),
              "w_ple_gate": (D_MODEL // N_CHIPS, D_MODEL), "w_ple_up": (PLE_DIM, D_MODEL // N_CHIPS)}
    out = {name: [] for name, _ in _PACK}
    off = 0
    for i in range(DEPTH):
        for name, rows in _PACK:
            out[name].append(slab[off:off + rows].reshape(shapes[name]))
            off += rows
    return {k: jnp.stack(v) for k, v in out.items()}


_COL_SHARDED = ("w_in", "w_up", "w_ple_up")


def _full_from_chips(g, pack=_PACK):
    per_chip = [_unpack_shards_layer(g[k], pack) for k in range(N_CHIPS)]
    return {name: jnp.concatenate([pc[name] for pc in per_chip], axis=1 if name in _COL_SHARDED else 0)
            for name, _ in pack}


def _unpack_shards_layer(slab, pack):
    shapes = {"w_in": (D_MODEL, IN_WIDTH // N_CHIPS), "w_out": (D_MODEL // N_CHIPS, D_MODEL),
              "w_up": (D_MODEL, 2 * D_FF // N_CHIPS), "w_down": (D_FF // N_CHIPS, D_MODEL),
              "w_ple_gate": (D_MODEL // N_CHIPS, D_MODEL), "w_ple_up": (PLE_DIM, D_MODEL // N_CHIPS)}
    out = {}
    off = 0
    for name, rows in pack:
        out[name] = slab[off:off + rows].reshape(shapes[name])
        off += rows
    return out


def _split_to_chips(full, name):
    r, c = full.shape
    if name in _COL_SHARDED:
        full = full.reshape(r, N_CHIPS, c // N_CHIPS).transpose(1, 0, 2)
    return full.reshape(N_CHIPS, -1, D_MODEL)


_SMALL = (("loss", 128), ("g_final", 1024), ("g_mix", 4096), ("lb_logits", 2048), ("hg_norm_g", 2048),
          ("attn_sinks", 128), ("g_ffn", 4096), ("conv_w", 4 * 3 * D_FF), ("conv_b", 4 * D_FF), ("g_ple", 4096))
SMALL_ROWS = 496


def _pack_small(d):
    parts = []
    for name, n in _SMALL:
        v = d[name].reshape(-1).astype(F32)
        parts.append(jnp.pad(v, (0, n - v.shape[0])))
    flat = jnp.concatenate(parts)
    return jnp.pad(flat, (0, SMALL_ROWS * 128 - flat.shape[0])).reshape(SMALL_ROWS, 128)


def _unpack_small(buf, shapes):
    flat = buf.reshape(-1)
    out, off = {}, 0
    for name, n in _SMALL:
        size = 1
        for s in shapes[name]:
            size *= s
        out[name] = flat[off:off + size].reshape(shapes[name])
        off += n
    return out


WEIGHT_ORDER = ('g_mix', 'w_in', 'lb_logits', 'hg_norm_g', 'attn_sinks', 'w_out', 'g_ffn', 'w_up', 'conv_w',
                'conv_b', 'w_down', 'g_ple', 'w_ple_gate', 'w_ple_up', 'g_final')


def kernel(x, p, g_mix, w_in, lb_logits, hg_norm_g, attn_sinks, w_out, g_ffn, w_up, conv_w, conv_b, w_down, g_ple, w_ple_gate, w_ple_up, g_final, loss_target, m_g_mix, m_w_in, m_lb_logits, m_hg_norm_g, m_attn_sinks, m_w_out, m_g_ffn, m_w_up, m_conv_w, m_conv_b, m_w_down, m_g_ple, m_w_ple_gate, m_w_ple_up, m_g_final, v_g_mix, v_w_in, v_lb_logits, v_hg_norm_g, v_attn_sinks, v_w_out, v_g_ffn, v_w_up, v_conv_w, v_conv_b, v_w_down, v_g_ple, v_w_ple_gate, v_w_ple_up, v_g_final):
    W = dict(g_mix=g_mix, w_in=w_in, lb_logits=lb_logits, hg_norm_g=hg_norm_g, attn_sinks=attn_sinks,
             w_out=w_out, g_ffn=g_ffn, w_up=w_up, conv_w=conv_w, conv_b=conv_b, w_down=w_down, g_ple=g_ple,
             w_ple_gate=w_ple_gate, w_ple_up=w_ple_up, g_final=g_final)
    M = dict(g_mix=m_g_mix, w_in=m_w_in, lb_logits=m_lb_logits, hg_norm_g=m_hg_norm_g, attn_sinks=m_attn_sinks,
             w_out=m_w_out, g_ffn=m_g_ffn, w_up=m_w_up, conv_w=m_conv_w, conv_b=m_conv_b, w_down=m_w_down,
             g_ple=m_g_ple, w_ple_gate=m_w_ple_gate, w_ple_up=m_w_ple_up, g_final=m_g_final)
    V = dict(g_mix=v_g_mix, w_in=v_w_in, lb_logits=v_lb_logits, hg_norm_g=v_hg_norm_g, attn_sinks=v_attn_sinks,
             w_out=v_w_out, g_ffn=v_g_ffn, w_up=v_w_up, conv_w=v_conv_w, conv_b=v_conv_b, w_down=v_w_down,
             g_ple=v_g_ple, w_ple_gate=v_w_ple_gate, w_ple_up=v_w_ple_up, g_final=v_g_final)
    S = x.shape[1]
    hg_rows = min(ROW_TILE, S)
    xi, yi, ci = _place()
    chip = 2 * xi + yi

    slab = _pack_shards({n: W[n] for n, _ in _PACK}).astype(BF16).reshape(DEPTH, LAYER_ROWS, D_MODEL)
    first_rows = _PACK[0][1]
    gathered = (weights_allgather(slab[0, :first_rows]),)
    cw_shard = jnp.pad(conv_w.reshape(-1), (0, 72 * 128 - conv_w.size)).reshape(72, 128)
    cw_all = small_allgather(cw_shard)
    cw_full = jnp.concatenate(
        [cw_all[2 * k].reshape(-1)[:conv_w.size].reshape(conv_w.shape) for k in range(N_CHIPS)], axis=2)
    lbrows = lb_fwd(lb_logits)
    at_mask = swa_mask()

    h = x[0]
    saved = []
    for i in range(DEPTH):
        wf = _full_from_chips(gathered[0], _PACK[:1] if i == 0 else _PACK)
        lbr = lbrows[8 * i:8 * i + 8]
        ng = hg_norm_g[i][None]
        sinks_b = jnp.pad(jnp.repeat(attn_sinks[i].reshape(AT_KV_HEADS, 1, AT_GROUP), WINDOW, axis=2),
                          ((0, 0), (0, 7), (0, 0))).reshape(8 * AT_KV_HEADS, GROUP_LANES)
        cw8 = jnp.pad(cw_full[i], ((0, 5), (0, 0)))
        cb = conv_b[i][None]
        if i == 0:
            u = rmsnorm_fwd(h, g_mix[0][None], name="rmsnorm_fwd")
        proj = mm(u, wf["w_in"], name="mm_in")
        riders = ([slab[0, first_rows:]] if i == 0 else []) + ([slab[i + 1]] if i + 1 < DEPTH else [])
        y, o_raw, states, *got = hgrn_fwd(proj, lbr, ng, D_MODEL, rows=hg_rows, gather=riders)
        if i == 0:
            wf.update(_full_from_chips(got[0], _PACK[1:]))
        gathered = got[-1:]
        y = swa_fwd(proj, sinks_b, at_mask, y)
        h1, u2 = mm(y, wf["w_out"], res=h, norm_g=g_ffn[i][None], name="mm_out")
        hh = mm(u2, wf["w_up"], out_dtype=BF16, name="mm_up")
        act, conv = convffn_fwd(hh, cw8, cb)
        h2, u3 = mm(act, wf["w_down"], res=h1, norm_g=g_ple[i][None], name="mm_down")
        gpre = mm(u3, wf["w_ple_gate"], out_dtype=BF16, name="mm_gate")
        next_g = g_mix[i + 1] if i + 1 < DEPTH else g_final
        h3, u_next = ple_fwd(h2, gpre, p[i, 0], wf["w_ple_up"], next_g[None])
        saved.append(dict(wf=wf, lbr=lbr, ng=ng, sinks_b=sinks_b, cw8=cw8, cb=cb, h=h, u=u, proj=proj,
                          o_raw=o_raw, states=states, y=y, h1=h1, u2=u2, hh=hh, conv=conv, act=act, h2=h2, u3=u3,
                          gpre=gpre))
        h, u = h3, u_next

    dh, dhb, loss_acc, dg_final = loss_head(h, g_final[None], loss_target[0])

    gfull = {n: [None] * DEPTH for n, _ in _PACK}
    gsmall = {n: [None] * DEPTH for n in ("g_mix", "hg_norm_g", "attn_sinks", "g_ffn", "conv_w", "conv_b", "g_ple")}
    dlbrows = [None] * DEPTH
    def pair_sums(i, pack):
        pk = jnp.concatenate([_split_to_chips(gfull[name][i], name) for name, _ in pack], axis=1).astype(BF16)
        half = pk.shape[1] // 2
        pk = pk.reshape(N_CHIPS, 2, half, D_MODEL)
        p_mine = lax.dynamic_index_in_dim(pk, ci, axis=1, keepdims=False)
        p_other = lax.dynamic_index_in_dim(pk, 1 - ci, axis=1, keepdims=False)
        from_sib = sibling_swap(p_other, name="sibling_swap_partials")
        pair = sum_slots(jnp.stack([p_mine.reshape(-1, D_MODEL), from_sib.reshape(-1, D_MODEL)]),
                         out_dtype=BF16, name="sum_pair")
        return pair.reshape(N_CHIPS, half, D_MODEL)

    from_chips = {}
    pending = []
    for i in reversed(range(DEPTH)):
        s = saved[i]
        wf = s["wf"]
        dpu, dgp = ple_bwd(dh, s["gpre"], p[i, 0], wf["w_ple_up"])
        gfull["w_ple_up"][i] = mm_tn(p[i, 0], dpu, name="mm_tn_pu")
        gfull["w_ple_gate"][i] = mm_tn(s["u3"], dgp, name="mm_tn_gate")
        dh2, dh2b, dg = mm(dgp, wf["w_ple_gate"], nt=True, rms_bwd=(s["h2"], g_ple[i][None], dh),
                           name="mm_nt_gate")
        gsmall["g_ple"][i] = dg[0]
        gfull["w_down"][i] = mm_tn(s["act"], dh2b, name="mm_tn_down")
        dact = mm(dh2b, wf["w_down"], nt=True, out_dtype=BF16, name="mm_nt_down")
        da, db, dcw = convffn_bwd(s["hh"], s["conv"], dact, s["cw8"])
        gsmall["conv_w"][i] = dcw[0:3]
        gsmall["conv_b"][i] = dcw[3]
        gfull["w_up"][i] = jnp.concatenate([mm_tn(s["u2"], da, name="mm_tn_up"),
                                            mm_tn(s["u2"], db, name="mm_tn_up")], axis=1)
        dh1, dh1b, dg = mm((da, db), wf["w_up"], nt=True, rms_bwd=(s["h1"], g_ffn[i][None], dh2),
                           name="mm_nt_up")
        gsmall["g_ffn"][i] = dg[0]
        gfull["w_out"][i] = mm_tn(s["y"], dh1b, name="mm_tn_out")
        dy = mm(dh1b, wf["w_out"], nt=True, name="mm_nt_out")
        dq_at, dko, dkp, dvo, dvp, dsk = swa_bwd(s["proj"], s["sinks_b"], at_mask, dy)
        gsmall["attn_sinks"][i] = dsk.reshape(AT_KV_HEADS, 8, GROUP_LANES)[:, 1:1 + AT_GROUP, 0].reshape(-1)
        hg_args = (s["proj"], s["o_raw"], s["states"], dy, s["lbr"], s["ng"])
        riders = pending + ([("0 rest", pair_sums(0, _PACK[1:]))] if i == 0 else [])
        hq, hz, hv, hgp, dlbr, dng, *got = hgrn_bwd(*hg_args, rows=hg_rows, exchange=[q for _, q in riders])
        from_chips.update({unit: g for (unit, _), g in zip(riders, got)})
        dlbrows[i] = dlbr
        gsmall["hg_norm_g"][i] = dng[0]
        dproj = assemble_dproj((hq, hz, hv, hgp), dq_at, dko, dkp, dvo, dvp, rows=hg_rows)
        gfull["w_in"][i] = mm_tn(s["u"], dproj, name="mm_tn_in")
        dh, dhb, dg = mm(dproj, wf["w_in"], nt=True, rms_bwd=(s["h"], g_mix[i][None], dh1), name="mm_nt_in")
        gsmall["g_mix"][i] = dg[0]
        pending = [("%d" % i, pair_sums(i, _PACK))] if i > 0 else [("0 w_in", pair_sums(0, _PACK[:1]))]
    from_chips["0 w_in"] = chip_exchange(pending[0][1])
    grad_x = dh[None]
    dlb_logits = lb_bwd(jnp.concatenate(dlbrows, axis=0), lb_logits)[0:DEPTH]

    units = ["0 w_in", "0 rest"] + ["%d" % i for i in range(1, DEPTH)]
    sums = [sum_slots(from_chips[unit], out_dtype=F32, name="sum_chips") for unit in units]
    mine_sum = jnp.concatenate(sums, axis=0)
    sib_sum = sibling_swap(mine_sum, name="sibling_swap_sums")
    lo = jnp.where(ci == 0, mine_sum, sib_sum)
    hi = jnp.where(ci == 0, sib_sum, mine_sum)
    parts, off = [], 0
    for part in sums:
        n = part.shape[0]
        parts += [lo[off:off + n], hi[off:off + n]]
        off += n
    gshard = _unpack_shards(jnp.concatenate(parts, axis=0))

    small = dict(loss=loss_acc[0, 0:1], g_final=dg_final[0], lb_logits=dlb_logits,
                 **{n: jnp.stack(v) for n, v in gsmall.items()})
    small_sum = sum_slots(small_allgather(_pack_small(small)), out_dtype=F32, name="sum_small")
    shapes = {n: W[n].shape for n in W}
    shapes["loss"] = (1,)
    shapes["conv_w"] = (DEPTH, 3, D_FF)
    gs = _unpack_small(small_sum, shapes)
    loss = gs["loss"][0]
    cshard = conv_w.shape[2]
    grads = dict(gshard)
    for n in ("g_mix", "lb_logits", "hg_norm_g", "attn_sinks", "g_ffn", "conv_b", "g_ple", "g_final"):
        grads[n] = gs[n]
    grads["conv_w"] = lax.dynamic_slice_in_dim(gs["conv_w"], chip * cshard, cshard, axis=2)

    delta, new_m, new_v = {}, {}, {}
    small_names = ("g_final", "g_mix", "lb_logits", "hg_norm_g", "attn_sinks", "g_ffn", "conv_b", "g_ple")
    sshapes = {n: W[n].shape for n in small_names}

    def pack_s(d):
        z = dict(d)
        z["loss"] = jnp.zeros((1,), F32)
        z["conv_w"] = jnp.zeros((1,), F32)
        return _pack_small(z)

    sd, sm, sv = adamw(pack_s(W), pack_s(grads), pack_s(M), pack_s(V), name="adamw_small")
    for out, buf in ((delta, sd), (new_m, sm), (new_v, sv)):
        un = _unpack_small(buf, {**sshapes, "loss": (1,), "conv_w": (1,)})
        for n in small_names:
            out[n] = un[n]
    for n in ("w_in", "w_out", "w_up", "w_down", "w_ple_gate", "w_ple_up", "conv_w"):
        shp = W[n].shape
        two_d = (-1, shp[-1])
        d_, m_, v_ = adamw(W[n].reshape(two_d), grads[n].reshape(two_d), M[n].reshape(two_d),
                           V[n].reshape(two_d), name="adamw_" + n)
        delta[n], new_m[n], new_v[n] = d_.reshape(shp), m_.reshape(shp), v_.reshape(shp)

    return (loss, grad_x, *[grads[n] for n in WEIGHT_ORDER], *[delta[n] for n in WEIGHT_ORDER],
            *[new_m[n] for n in WEIGHT_ORDER], *[new_v[n] for n in WEIGHT_ORDER])
```

```python
import jax
import jax.numpy as jnp
from jax import lax
from jax.experimental import pallas as pl
from jax.experimental.pallas import tpu as pltpu

F32 = jnp.float32
BF16 = jnp.bfloat16

D_MODEL = 1024
DEPTH = 4
PLE_DIM = 256
HG_WIDTH = 512
HG_HEADS = 4
HG_DK = 128
HG_CHUNK = 64
HG_SUB = 16
AT_WIDTH = 512
AT_HEAD_DIM = 64
AT_KV_HEADS = 2
AT_GROUP = 4
WINDOW = 128
D_FF = 2816
IN_WIDTH = 2816
EPS = 1e-6
MASK_VALUE = -1e30
LB_FLOOR = 1e-30

ADAM_LR = 0.001
ADAM_B1 = 0.9
ADAM_B2 = 0.999
ADAM_EPS = 1e-08
ADAM_WD = 0.01
ADAM_STEP = 10

VMEM_LIMIT = 48 * 1024 * 1024


def _params(*sem):
    return pltpu.CompilerParams(dimension_semantics=sem, vmem_limit_bytes=VMEM_LIMIT)


def _dot(a, b, dims=(((1,), (0,)), ((), ()))):
    return lax.dot_general(a.astype(BF16), b.astype(BF16), dims, preferred_element_type=F32)


def _dot_nt(a, b):
    return _dot(a, b, (((1,), (1,)), ((), ())))


def _dot_tn(a, b):
    return _dot(a, b, (((0,), (0,)), ((), ())))


def _dot_exact(sel, x, dims=(((1,), (0,)), ((), ()))):
    hi = x.astype(BF16)
    r1 = x - hi.astype(F32)
    mid = r1.astype(BF16)
    lo = (r1 - mid.astype(F32)).astype(BF16)
    s = sel.astype(BF16)
    one = lambda p: lax.dot_general(s, p, dims, preferred_element_type=F32)
    return one(hi) + one(mid) + one(lo)


def _sigmoid(x):
    return 0.5 * jnp.tanh(0.5 * x) + 0.5


def _logsig(x):
    return jnp.minimum(x, 0.0) - jnp.log(1.0 + jnp.exp(-jnp.abs(x)))


def _colsum(x):
    return jnp.sum(x, axis=0, keepdims=True)


def _rowsum(x):
    return jnp.sum(x, axis=1, keepdims=True)


def _colsum8(xs):
    row = lax.broadcasted_iota(jnp.int32, xs[0].shape, 0)

    def merge(a, b, keep_a, step):
        return jnp.where(keep_a, a + pltpu.roll(a, 8 - step, 0), b + pltpu.roll(b, step, 0))

    c = [merge(xs[j], xs[j + 4], row < 4, 4) for j in range(4)]
    d = [merge(c[j], c[j + 2], (row & 3) < 2, 2) for j in range(2)]
    return merge(d[0], d[1], (row & 1) == 0, 1)


def _tri(n):
    r = lax.broadcasted_iota(jnp.int32, (n, n), 0)
    c = lax.broadcasted_iota(jnp.int32, (n, n), 1)
    return (r >= c).astype(F32)


def _hg_gates(qp, z, a, c, oml):
    sq = _sigmoid(qp)
    q = qp * sq
    t = c + _logsig(z)
    mx = jnp.maximum(a, t)
    logf = mx + jnp.log(1.0 + jnp.exp(-jnp.abs(a - t)))
    snz = _sigmoid(-z)
    k = oml * snz
    return q, sq, t, logf, snz, k


_HEADS = range(HG_HEADS)


def _lanes(h):
    return slice(h * HG_DK, (h + 1) * HG_DK)


def _head(x, h):
    return x[:, _lanes(h)]


def _row_masks():
    row8 = lax.broadcasted_iota(jnp.int32, (8, HG_DK), 0)
    return [None] + [jnp.where(row8 >= j, 0.0, MASK_VALUE) for j in range(1, 8)]


def _hg_chunk_fwd(q, k, v, logf, st, b_s, k_s, v_s):
    C, U = HG_CHUNK, HG_SUB
    tri = _tri(C)
    b = [_dot_exact(tri, logf[h]) for h in _HEADS]
    for h in _HEADS:
        b_s[h] = b[h]
        k_s[h] = k[h]
        v_s[h] = v[h]
    o = [_dot_nt(q[h] * jnp.exp(b[h]), st[h]) for h in _HEADS]
    bl = [b[h][C - 1:C] for h in _HEADS]
    upd = [_dot_tn(v[h], k[h] * jnp.exp(bl[h] - b[h])) for h in _HEADS]
    rows = lax.broadcasted_iota(jnp.int32, (C, HG_DK), 0)
    nmask = _row_masks()
    outs = [[] for _ in _HEADS]
    for i in range(C // U):
        lo = i * U
        b_i = [b[h][lo:lo + U] for h in _HEADS]
        q_i = [q[h][lo:lo + U] for h in _HEADS]
        o_i = [o[h][lo:lo + U] for h in _HEADS]
        if i > 0:
            qe = [q_i[h] * jnp.exp(b_i[h] - b_i[h][0:1]) for h in _HEADS]
            ke = [jnp.where(rows < lo, k[h] * jnp.exp(jnp.minimum(b_i[h][0:1] - b[h], 0.0)), 0.0) for h in _HEADS]
            att = [_dot_nt(qe[h], ke[h]) for h in _HEADS]
            off = [_dot(att[h], v[h]) for h in _HEADS]
            o_i = [o_i[h] + off[h] for h in _HEADS]
        pieces = [[o_i[h][8 * f:8 * f + 8] for f in range(U // 8)] for h in _HEADS]
        for s in range(U):
            for f in range(s // 8, U // 8):
                for h in _HEADS:
                    bs = b_s[h, lo + s:lo + s + 1, :]
                    ks = k_s[h, lo + s:lo + s + 1, :]
                    vs = v_s[h, lo + s:lo + s + 1, :]
                    arg = b_i[h][8 * f:8 * f + 8] - bs
                    if s > 8 * f:
                        arg = arg + nmask[s - 8 * f]
                    w = _rowsum(q_i[h][8 * f:8 * f + 8] * jnp.exp(arg) * ks)
                    pieces[h][f] = pieces[h][f] + w * vs
        for h in _HEADS:
            outs[h] += pieces[h]
    o = [jnp.concatenate(outs[h], axis=0) for h in _HEADS]
    st_new = [st[h] * jnp.exp(bl[h]) + upd[h] for h in _HEADS]
    return o, st_new, b


def _hg_post(o, gp, ng):
    rs = lax.rsqrt(jnp.mean(o * o, axis=1, keepdims=True) + EPS)
    sg = _sigmoid(gp)
    return o * rs * ng * sg, rs, sg


def hgrn_fwd(proj, lbrows, ng, y_width, *, rows, gather=()):
    S = proj.shape[0]
    C = HG_CHUNK
    cpb = rows // C
    nb = S // rows
    ng_ = len(gather)

    def body(qp_ref, z_ref, v_ref, gp_ref, lb_ref, ng_ref, *rest):
        w_refs, rest = rest[:ng_], rest[ng_:]
        y_ref, o_ref, st_ref = rest[:3]
        g_refs, rest = rest[3:3 + ng_], rest[3 + ng_:]
        st, b_s, k_s, v_s = rest[:4]
        sems = rest[4:]
        comms = [_Gather(w_refs[i], g_refs[i], *sems[3 * i:3 * i + 3]) for i in range(ng_)]

        @pl.when(pl.program_id(0) == 0)
        def _():
            st[...] = jnp.zeros_like(st)
            for comm in comms:
                comm.start()

        a, c, oml = lb_ref[0:1, :], lb_ref[1:2, :], lb_ref[2:3, :]
        ngr = ng_ref[...]

        def chunk(ci, carry):
            off = pl.multiple_of(ci * C, C)
            sl = pl.ds(off, C)
            for h in _HEADS:
                st_ref[h, ci] = st[h]
            gates = [_hg_gates(qp_ref[sl, _lanes(h)], z_ref[sl, _lanes(h)],
                               _head(a, h), _head(c, h), _head(oml, h)) for h in _HEADS]
            q = [g[0] for g in gates]
            logf = [g[3] for g in gates]
            k = [g[5] for g in gates]
            v = [v_ref[sl, _lanes(h)] for h in _HEADS]
            o, st_new, _ = _hg_chunk_fwd(q, k, v, logf, [st[h] for h in _HEADS], b_s, k_s, v_s)
            for h in _HEADS:
                y, _, _ = _hg_post(o[h], gp_ref[sl, _lanes(h)], _head(ngr, h))
                y_ref[sl, _lanes(h)] = y.astype(y_ref.dtype)
                o_ref[sl, _lanes(h)] = o[h]
                st[h] = st_new[h]
            return carry

        lax.fori_loop(0, cpb, chunk, 0)

        for comm in comms:
            pl.when(pl.program_id(0) == nb - 1)(comm.finish)

    col = lambda kblk: pl.BlockSpec((rows, HG_WIDTH), lambda r: (r, kblk))
    in_specs = [col(0), col(1), col(2), col(3), _const_spec((8, HG_WIDTH)), _const_spec((1, HG_WIDTH))]
    out_specs = [col(0), col(0), pl.BlockSpec((HG_HEADS, cpb, HG_DK, HG_DK), lambda r: (0, r, 0, 0))]
    out_shape = [jax.ShapeDtypeStruct((S, y_width), BF16),
                 jax.ShapeDtypeStruct((S, HG_WIDTH), F32),
                 jax.ShapeDtypeStruct((HG_HEADS, S // C, HG_DK, HG_DK), F32)]
    scratch = [pltpu.VMEM((HG_HEADS, HG_DK, HG_DK), F32)] + [pltpu.VMEM((HG_HEADS, C, HG_DK), F32)] * 3
    args = [proj, proj, proj, proj, lbrows, ng]
    for w in gather:
        in_specs.append(ANY)
        out_specs.append(ANY)
        out_shape.append(jax.ShapeDtypeStruct((N_CHIPS,) + w.shape, w.dtype))
        scratch += _Gather.SCRATCH
        args.append(w)
    return pl.pallas_call(
        body,
        name="hgrn_fwd" if not gather else "hgrn_fwd_gather%d" % ng_,
        grid=(nb,),
        in_specs=in_specs,
        out_specs=out_specs,
        out_shape=out_shape,
        scratch_shapes=scratch,
        compiler_params=_params("arbitrary"),
    )(*args)


def hgrn_bwd(proj, o_raw, states, dy, lbrows, ng, *, rows, exchange=()):
    S = proj.shape[0]
    C, U = HG_CHUNK, HG_SUB
    cpb = rows // C
    nb = S // rows
    nx = len(exchange)

    def body(qp_ref, z_ref, v_ref, gp_ref, o_ref, st_ref, dy_ref, lb_ref, ng_ref, *rest):
        q_refs, rest = rest[:nx], rest[nx:]
        dqp_ref, dz_ref, dv_ref, dgp_ref, dlb_ref, dng_ref = rest[:6]
        r_refs, rest = rest[6:6 + nx], rest[6 + nx:]
        dst, b_s, k_s, v_s, dbs, dks, dvs = rest[:7]
        sems = rest[7:]
        comms = [_Exchange(q_refs[i], r_refs[i], *sems[3 * i:3 * i + 3]) for i in range(nx)]

        @pl.when(pl.program_id(0) == 0)
        def _():
            dst[...] = jnp.zeros_like(dst)
            dlb_ref[...] = jnp.zeros_like(dlb_ref)
            dng_ref[...] = jnp.zeros_like(dng_ref)
            for comm in comms:
                comm.start()

        a, c, oml = lb_ref[0:1, :], lb_ref[1:2, :], lb_ref[2:3, :]
        ngr = ng_ref[...]
        rows_i = lax.broadcasted_iota(jnp.int32, (C, HG_DK), 0)
        nmask = _row_masks()
        tri = _tri(C)
        H = _HEADS

        def chunk(cj, carry):
            ci = cpb - 1 - cj
            off = pl.multiple_of(ci * C, C)
            sl = pl.ds(off, C)
            qp = [qp_ref[sl, _lanes(h)] for h in H]
            v = [v_ref[sl, _lanes(h)] for h in H]
            st = [st_ref[h, ci] for h in H]
            gates = [_hg_gates(qp[h], z_ref[sl, _lanes(h)], _head(a, h), _head(c, h), _head(oml, h)) for h in H]
            q, sq, t, logf, snz, k = ([g[j] for g in gates] for j in range(6))
            b = [_dot_exact(tri, logf[h]) for h in H]
            for h in H:
                b_s[h] = b[h]
                k_s[h] = k[h]
                v_s[h] = v[h]
            do = []
            for h in H:
                o = o_ref[sl, _lanes(h)]
                dyv = dy_ref[sl, _lanes(h)]
                ngh = _head(ngr, h)
                rs = lax.rsqrt(jnp.mean(o * o, axis=1, keepdims=True) + EPS)
                sg = _sigmoid(gp_ref[sl, _lanes(h)])
                xh = o * rs
                dgp_ref[sl, _lanes(h)] = (dyv * xh * ngh * sg * (1.0 - sg)).astype(dgp_ref.dtype)
                don = dyv * sg
                dng_ref[0:1, _lanes(h)] += _colsum(don * xh)
                dxh = don * ngh
                do.append(rs * (dxh - xh * jnp.mean(dxh * xh, axis=1, keepdims=True)))
            eb = [jnp.exp(b[h]) for h in H]
            qb = [q[h] * eb[h] for h in H]
            dstv = [dst[h] for h in H]
            bl = [b[h][C - 1:C] for h in H]
            el = [jnp.exp(bl[h]) for h in H]
            ex = [jnp.exp(bl[h] - b[h]) for h in H]
            kd = [k[h] * ex[h] for h in H]
            dqb = [_dot(do[h], st[h]) for h in H]
            dst_acc = [_dot_tn(do[h], qb[h]) for h in H]
            dv0 = [_dot_nt(kd[h], dstv[h]) for h in H]
            dkd = [_dot(v[h], dstv[h]) for h in H]
            dq = [dqb[h] * eb[h] for h in H]
            for h in H:
                g2 = dkd[h] * kd[h]
                dbl = _colsum(dstv[h] * st[h]) * el[h] + _colsum(g2)
                dst[h] = dstv[h] * el[h] + dst_acc[h]
                dbs[h] = dqb[h] * qb[h] - g2
                dks[h] = dkd[h] * ex[h]
                dvs[h] = dv0[h]
                dbs[h, C - 1:C, :] += dbl
            dq_parts = [[] for _ in H]
            for i in range(C // U):
                lo = i * U
                b_i = [b[h][lo:lo + U] for h in H]
                q_i = [q[h][lo:lo + U] for h in H]
                do_i = [do[h][lo:lo + U] for h in H]
                dq_i = [dq[h][lo:lo + U] for h in H]
                db_i = [jnp.zeros((U, HG_DK), F32) for _ in H]
                if i > 0:
                    e1 = [jnp.exp(b_i[h] - b_i[h][0:1]) for h in H]
                    qe = [q_i[h] * e1[h] for h in H]
                    e2 = [jnp.where(rows_i < lo, jnp.exp(jnp.minimum(b_i[h][0:1] - b[h], 0.0)), 0.0) for h in H]
                    ke = [k[h] * e2[h] for h in H]
                    att = [_dot_nt(qe[h], ke[h]) for h in H]
                    datt = [_dot_nt(do_i[h], v[h]) for h in H]
                    dv_add = [_dot_tn(att[h], do_i[h]) for h in H]
                    dqe = [_dot(datt[h], ke[h]) for h in H]
                    dke = [_dot_tn(datt[h], qe[h]) for h in H]
                    for h in H:
                        dvs[h] += dv_add[h]
                        dq_i[h] = dq_i[h] + dqe[h] * e1[h]
                        g = dqe[h] * qe[h]
                        db_i[h] = db_i[h] + g
                        gk = dke[h] * ke[h]
                        dks[h] += dke[h] * e2[h]
                        dbs[h] -= gk
                        dbs[h, lo:lo + 1, :] += _colsum(gk) - _colsum(g)
                nf = U // 8
                dq8 = [[dq_i[h][8 * f:8 * f + 8] for f in range(nf)] for h in H]
                db8 = [[db_i[h][8 * f:8 * f + 8] for f in range(nf)] for h in H]
                key_v = [[] for _ in H]
                key_k = [[] for _ in H]
                key_b = [[] for _ in H]
                for s in range(U):
                    row = slice(lo + s, lo + s + 1)
                    for h in H:
                        bs = b_s[h, row, :]
                        ks = k_s[h, row, :]
                        vs = v_s[h, row, :]
                        tv = tk = tb = None
                        for f in range(s // 8, nf):
                            p8 = slice(8 * f, 8 * f + 8)
                            arg = b_i[h][p8] - bs
                            if s > 8 * f:
                                arg = arg + nmask[s - 8 * f]
                            dec = jnp.exp(arg)
                            qd = q_i[h][p8] * dec
                            y_ = qd * ks
                            w = _rowsum(y_)
                            dw = _rowsum(do_i[h][p8] * vs)
                            g = dw * y_
                            dq8[h][f] = dq8[h][f] + dw * dec * ks
                            db8[h][f] = db8[h][f] + g
                            cv, ck = w * do_i[h][p8], dw * qd
                            tv, tk, tb = (cv, ck, g) if tv is None else (tv + cv, tk + ck, tb + g)
                        key_v[h].append(tv)
                        key_k[h].append(tk)
                        key_b[h].append(tb)
                for h in H:
                    for f in range(nf):
                        r8 = slice(lo + 8 * f, lo + 8 * f + 8)
                        dvs[h, r8, :] += _colsum8(key_v[h][8 * f:8 * f + 8])
                        dks[h, r8, :] += _colsum8(key_k[h][8 * f:8 * f + 8])
                        dbs[h, r8, :] += db8[h][f] - _colsum8(key_b[h][8 * f:8 * f + 8])
                    dq_parts[h] += dq8[h]
            dlogf = [_dot_exact(tri, dbs[h], (((0,), (0,)), ((), ()))) for h in H]
            for h in H:
                dqh = jnp.concatenate(dq_parts[h], axis=0)
                dk = dks[h]
                ah, omlh = _head(a, h), _head(oml, h)
                pa = jnp.exp(ah - logf[h])
                pt = jnp.exp(t[h] - logf[h])
                dt = dlogf[h] * pt
                dlb_ref[0:1, _lanes(h)] += _colsum(dlogf[h] * pa)
                dlb_ref[1:2, _lanes(h)] += _colsum(dt)
                dlb_ref[2:3, _lanes(h)] += _colsum(dk * snz[h])
                dz = dt * snz[h] - dk * omlh * snz[h] * (1.0 - snz[h])
                dqp = dqh * (sq[h] + qp[h] * sq[h] * (1.0 - sq[h]))
                dqp_ref[sl, _lanes(h)] = dqp.astype(dqp_ref.dtype)
                dz_ref[sl, _lanes(h)] = dz.astype(dz_ref.dtype)
                dv_ref[sl, _lanes(h)] = dvs[h].astype(dv_ref.dtype)
            return carry

        lax.fori_loop(0, cpb, chunk, 0)

        for comm in comms:
            pl.when(pl.program_id(0) == nb - 1)(comm.finish)

    rev = lambda r: nb - 1 - r
    col = lambda kblk: pl.BlockSpec((rows, HG_WIDTH), lambda r: (rev(r), kblk))
    acc = _const_spec((8, HG_WIDTH))
    in_specs = [col(0), col(1), col(2), col(3), col(0),
                pl.BlockSpec((HG_HEADS, cpb, HG_DK, HG_DK), lambda r: (0, rev(r), 0, 0)),
                col(0), acc, _const_spec((1, HG_WIDTH))]
    out_specs = [col(0)] * 4 + [acc, acc]
    out_shape = [jax.ShapeDtypeStruct((S, HG_WIDTH), BF16)] * 4 + [jax.ShapeDtypeStruct((8, HG_WIDTH), F32)] * 2
    scratch = [pltpu.VMEM((HG_HEADS, HG_DK, HG_DK), F32)] + [pltpu.VMEM((HG_HEADS, C, HG_DK), F32)] * 6
    args = [proj, proj, proj, proj, o_raw, states, dy, lbrows, ng]
    for q in exchange:
        in_specs.append(ANY)
        out_specs.append(ANY)
        out_shape.append(jax.ShapeDtypeStruct(q.shape, q.dtype))
        scratch += _Exchange.SCRATCH
        args.append(q)
    return pl.pallas_call(
        body,
        name="hgrn_bwd" if not exchange else "hgrn_bwd_exchange%d" % nx,
        grid=(nb,),
        in_specs=in_specs,
        out_specs=out_specs,
        out_shape=out_shape,
        scratch_shapes=scratch,
        compiler_params=_params("arbitrary"),
    )(*args)


GROUP_LANES = AT_GROUP * WINDOW


def swa_mask():
    W = WINDOW
    kpos = lax.broadcasted_iota(jnp.int32, (2, 2 * W, GROUP_LANES), 1)
    qpos = (lax.broadcasted_iota(jnp.int32, (2, 2 * W, GROUP_LANES), 2) & (W - 1)) + W
    first = lax.broadcasted_iota(jnp.int32, (2, 2 * W, GROUP_LANES), 0) == 0
    rel = qpos - kpos
    valid = (rel >= 0) & (rel < W) & jnp.logical_not(first & (kpos < W))
    return jnp.where(valid, 0.0, MASK_VALUE).astype(F32)


def _group_lanes(xt, g):
    Dh = AT_HEAD_DIM
    return jnp.concatenate([xt[(g * AT_GROUP + j) * Dh:(g * AT_GROUP + j + 1) * Dh] for j in range(AT_GROUP)],
                           axis=1)


SWA_SCALE = AT_HEAD_DIM ** -0.5


def _swa_softmax_t(s, sink_row):
    m = jnp.maximum(jnp.max(s, axis=0, keepdims=True), sink_row)
    e = jnp.exp(s - m)
    es = jnp.exp(sink_row - m)
    inv = 1.0 / (_colsum(e) + es)
    return e * inv, es * inv


SWA_BLOCKS = 8
_BG = [(b, g) for b in range(SWA_BLOCKS) for g in range(AT_KV_HEADS)]


def _swa_specs(col_q):
    W = WINDOW
    rows = SWA_BLOCKS * W
    prev = lambda n: jnp.maximum(SWA_BLOCKS * n - 1, 0)
    return [pl.BlockSpec((rows, AT_WIDTH), lambda n: (n, col_q)),
            pl.BlockSpec((W, 128), lambda n: (prev(n), 20)),
            pl.BlockSpec((rows, 128), lambda n: (n, 20)),
            pl.BlockSpec((W, 128), lambda n: (prev(n), 21)),
            pl.BlockSpec((rows, 128), lambda n: (n, 21)),
            _const_spec((8 * AT_KV_HEADS, GROUP_LANES)),
            _const_spec((2, 2 * W, GROUP_LANES))]


def _swa_operands(n, q_ref, kp_ref, k_ref, vp_ref, v_ref, sk_ref, mask_ref):
    W, Dh = WINDOW, AT_HEAD_DIM
    k_all = jnp.concatenate([kp_ref[...], k_ref[...]], axis=0)
    v_all = jnp.concatenate([vp_ref[...], v_ref[...]], axis=0)
    kk = [k_all[b * W:(b + 2) * W] for b in range(SWA_BLOCKS)]
    vv = [v_all[b * W:(b + 2) * W] for b in range(SWA_BLOCKS)]
    masks = [mask_ref[jnp.minimum(n, 1)]] + [mask_ref[1]] * (SWA_BLOCKS - 1)
    qt = [(q_ref[b * W:(b + 1) * W, :] * SWA_SCALE).T for b in range(SWA_BLOCKS)]
    kg = {(b, g): kk[b][:, g * Dh:(g + 1) * Dh] for b, g in _BG}
    qg = {(b, g): _group_lanes(qt[b], g) for b, g in _BG}
    s = {bg: _dot(kg[bg], qg[bg]) + masks[bg[0]] for bg in _BG}
    sink = {(b, g): sk_ref[8 * g:8 * g + 1, :] for b, g in _BG}
    return kk, vv, kg, qg, s, sink


def swa_fwd(proj, sink_rows, mask, y):
    S = proj.shape[0]
    W, Dh = WINDOW, AT_HEAD_DIM
    rows = SWA_BLOCKS * W

    def body(q_ref, kp_ref, k_ref, vp_ref, v_ref, sk_ref, mask_ref, y_in, y_ref):
        del y_in
        _, vv, _, _, s, sink = _swa_operands(pl.program_id(0), q_ref, kp_ref, k_ref, vp_ref, v_ref,
                                             sk_ref, mask_ref)
        vt = [v.T for v in vv]
        p = {bg: _swa_softmax_t(s[bg], sink[bg])[0] for bg in _BG}
        ot = {(b, g): _dot(vt[b][g * Dh:(g + 1) * Dh], p[b, g]) for b, g in _BG}
        for b in range(SWA_BLOCKS):
            outs = [ot[b, g][:, j * W:(j + 1) * W] for g in range(AT_KV_HEADS) for j in range(AT_GROUP)]
            y_ref[b * W:(b + 1) * W, :] = jnp.concatenate(outs, axis=0).T.astype(y_ref.dtype)

    return pl.pallas_call(
        body,
        name="swa_fwd",
        grid=(S // rows,),
        in_specs=_swa_specs(4) + [pl.BlockSpec(memory_space=pl.ANY)],
        out_specs=pl.BlockSpec((rows, AT_WIDTH), lambda n: (n, 1)),
        out_shape=jax.ShapeDtypeStruct(y.shape, y.dtype),
        input_output_aliases={7: 0},
        compiler_params=_params("parallel"),
    )(proj, proj, proj, proj, proj, sink_rows, mask, y)


def swa_bwd(proj, sink_rows, mask, dy):
    S = proj.shape[0]
    W, Dh = WINDOW, AT_HEAD_DIM
    rows = SWA_BLOCKS * W
    nsteps = S // rows

    def body(q_ref, kp_ref, k_ref, vp_ref, v_ref, sk_ref, mask_ref, dy_ref,
             dq_ref, dko_ref, dkp_ref, dvo_ref, dvp_ref, dsk_ref):
        n = pl.program_id(0)

        @pl.when(n == 0)
        def _():
            dsk_ref[...] = jnp.zeros_like(dsk_ref)

        kk, vv, _, qg, s, sink = _swa_operands(n, q_ref, kp_ref, k_ref, vp_ref, v_ref, sk_ref, mask_ref)
        kt = [k.T for k in kk]
        dot_ = [dy_ref[b * W:(b + 1) * W, :].T for b in range(SWA_BLOCKS)]
        dog = {(b, g): _group_lanes(dot_[b], g) for b, g in _BG}
        dp = {(b, g): _dot(vv[b][:, g * Dh:(g + 1) * Dh], dog[b, g]) for b, g in _BG}
        pp = {bg: _swa_softmax_t(s[bg], sink[bg]) for bg in _BG}
        delta = {bg: _colsum(dp[bg] * pp[bg][0]) for bg in _BG}
        ds = {bg: pp[bg][0] * (dp[bg] - delta[bg]) for bg in _BG}
        dqt = {(b, g): _dot(kt[b][g * Dh:(g + 1) * Dh], ds[b, g]) * SWA_SCALE for b, g in _BG}
        dk = {bg: _dot_nt(ds[bg], qg[bg]) for bg in _BG}
        dv = {bg: _dot_nt(pp[bg][0], dog[bg]) for bg in _BG}
        for g in range(AT_KV_HEADS):
            tot = -(pp[0, g][1] * delta[0, g])
            for b in range(1, SWA_BLOCKS):
                tot = tot - pp[b, g][1] * delta[b, g]
            dsk_ref[8 * g:8 * g + 1, :] += tot
        for b in range(SWA_BLOCKS):
            r = slice(b * W, (b + 1) * W)
            dqs = [dqt[b, g][:, j * W:(j + 1) * W] for g in range(AT_KV_HEADS) for j in range(AT_GROUP)]
            dq_ref[r, :] = jnp.concatenate(dqs, axis=0).T.astype(dq_ref.dtype)
            dkb = jnp.concatenate([dk[b, g] for g in range(AT_KV_HEADS)], axis=1)
            dvb = jnp.concatenate([dv[b, g] for g in range(AT_KV_HEADS)], axis=1)
            dkp_ref[r, :] = dkb[:W]
            dko_ref[r, :] = dkb[W:]
            dvp_ref[r, :] = dvb[:W]
            dvo_ref[r, :] = dvb[W:]

        @pl.when(n == nsteps - 1)
        def _():
            for g in range(AT_KV_HEADS):
                for j in range(AT_GROUP):
                    tot = _rowsum(dsk_ref[8 * g:8 * g + 1, j * W:(j + 1) * W])
                    dsk_ref[8 * g + 1 + j:8 * g + 2 + j, :] = jnp.broadcast_to(tot, (1, GROUP_LANES))

    kv = pl.BlockSpec((rows, 128), lambda n: (n, 0))
    sk = _const_spec((8 * AT_KV_HEADS, GROUP_LANES))
    return pl.pallas_call(
        body,
        name="swa_bwd",
        grid=(nsteps,),
        in_specs=_swa_specs(4) + [pl.BlockSpec((rows, AT_WIDTH), lambda n: (n, 1))],
        out_specs=[pl.BlockSpec((rows, AT_WIDTH), lambda n: (n, 0)), kv, kv, kv, kv, sk],
        out_shape=[jax.ShapeDtypeStruct((S, AT_WIDTH), BF16)]
                  + [jax.ShapeDtypeStruct((S, 128), F32)] * 4
                  + [jax.ShapeDtypeStruct((8 * AT_KV_HEADS, GROUP_LANES), F32)],
        compiler_params=_params("arbitrary"),
    )(proj, proj, proj, proj, proj, sink_rows, mask, dy)


def assemble_dproj(hg_grads, dq_at, dko, dkp, dvo, dvp, *, rows):
    S = dq_at.shape[0]
    W = WINDOW
    nb = S // W
    bpr = rows // W

    def body(a0, a1, a2, a3, dq, ko, kp, kpn, vo, vp, vpn, out):
        r = pl.program_id(0)
        for i, a in enumerate((a0, a1, a2, a3)):
            out[:, i * HG_WIDTH:(i + 1) * HG_WIDTH] = a[...]
        base = 4 * HG_WIDTH
        out[:, base:base + AT_WIDTH] = dq[...]
        last = (r == pl.num_programs(0) - 1)
        for off, own, pv, pvn in ((base + AT_WIDTH, ko, kp, kpn), (base + AT_WIDTH + 128, vo, vp, vpn)):
            if bpr > 1:
                out[0:rows - W, off:off + 128] = (own[0:rows - W, :] + pv[W:rows, :]).astype(out.dtype)
            nxt = jnp.where(last, 0.0, pvn[...])
            out[rows - W:rows, off:off + 128] = (own[rows - W:rows, :] + nxt).astype(out.dtype)

    hg = pl.BlockSpec((rows, HG_WIDTH), lambda r: (r, 0))
    blk = pl.BlockSpec((rows, 128), lambda r: (r, 0))
    nxt = pl.BlockSpec((W, 128), lambda r: (jnp.minimum((r + 1) * bpr, nb - 1), 0))
    return pl.pallas_call(
        body,
        name="assemble_dproj",
        grid=(S // rows,),
        in_specs=[hg, hg, hg, hg, pl.BlockSpec((rows, AT_WIDTH), lambda r: (r, 0)),
                  blk, blk, nxt, blk, blk, nxt],
        out_specs=pl.BlockSpec((rows, IN_WIDTH), lambda r: (r, 0)),
        out_shape=jax.ShapeDtypeStruct((S, IN_WIDTH), BF16),
        compiler_params=_params("parallel"),
    )(*hg_grads, dq_at, dko, dkp, dkp, dvo, dvp, dvp)


ROW_TILE = 512
COL_TILE = 1408


def _col_tile(n):
    return n if n <= COL_TILE else COL_TILE


def _rms_scale(x):
    return lax.rsqrt(jnp.mean(x * x, axis=1, keepdims=True) + EPS)


def _rms_bwd(d, x, g):
    rs = _rms_scale(x)
    xh = x * rs
    dxh = d * g
    return rs * (dxh - xh * jnp.mean(dxh * xh, axis=1, keepdims=True)), _colsum(d * xh)


def mm(a, b, *, nt=False, out_dtype=F32, res=None, norm_g=None, rms_bwd=None, name):
    parts = a if isinstance(a, tuple) else (a,)
    M, K = parts[0].shape
    N = b.shape[0] if nt else b.shape[1]
    tall = K <= D_MODEL and rms_bwd is None and len(parts) == 1 and M % (2 * ROW_TILE) == 0
    tm = 2 * ROW_TILE if tall else min(ROW_TILE, M)
    tn = _col_tile(N)
    whole_rows = norm_g is not None or rms_bwd is not None
    assert M % tm == 0 and N % tn == 0 and (tn == N or not whole_rows)
    np_ = len(parts)

    def body(*refs):
        a_refs, b_refs, rest = refs[:np_], refs[np_:2 * np_], refs[2 * np_:]
        dot = _dot_nt if nt else _dot
        acc = dot(a_refs[0][...], b_refs[0][...])
        for ar, br in zip(a_refs[1:], b_refs[1:]):
            acc = acc + dot(ar[...], br[...])
        if rms_bwd is not None:
            h_ref, g_ref, dr_ref, dh_ref, dhb_ref, dg_ref = rest

            @pl.when(pl.program_id(1) == 0)
            def _():
                dg_ref[...] = jnp.zeros_like(dg_ref)

            dx, dgp = _rms_bwd(acc, h_ref[...], g_ref[...])
            dg_ref[0:1, :] += dgp
            dh = dr_ref[...] + dx
            dh_ref[...] = dh
            dhb_ref[...] = dh.astype(BF16)
            return
        rest = list(rest)
        if res is not None:
            acc = acc + rest.pop(0)[...]
        if norm_g is not None:
            g_ref = rest.pop(0)
            rest[1][...] = (acc * _rms_scale(acc) * g_ref[...]).astype(BF16)
        rest[0][...] = acc.astype(rest[0].dtype)

    row = pl.BlockSpec((tm, tn), lambda j, i: (i, j))
    in_specs = [pl.BlockSpec((tm, K), lambda j, i: (i, 0)) for _ in parts]
    for kb in range(np_):
        in_specs.append(pl.BlockSpec((tn, K), lambda j, i, kb=kb: (j, kb)) if nt
                        else pl.BlockSpec((K, tn), lambda j, i, kb=kb: (kb, j)))
    args = list(parts) + [b] * np_
    if rms_bwd is not None:
        h, g, dres = rms_bwd
        in_specs += [row, _const_spec((1, N)), row]
        args += [h, g, dres]
        out_specs = [row, row, _const_spec((8, N))]
        out_shape = [jax.ShapeDtypeStruct((M, N), F32), jax.ShapeDtypeStruct((M, N), BF16),
                     jax.ShapeDtypeStruct((8, N), F32)]
        sem = ("arbitrary", "arbitrary")
    else:
        if res is not None:
            in_specs.append(row)
            args.append(res)
        out_specs, out_shape = [row], [jax.ShapeDtypeStruct((M, N), out_dtype)]
        if norm_g is not None:
            in_specs.append(_const_spec((1, N)))
            args.append(norm_g)
            out_specs.append(row)
            out_shape.append(jax.ShapeDtypeStruct((M, N), BF16))
        sem = ("parallel", "parallel")
    out = pl.pallas_call(
        body,
        name=name,
        grid=(N // tn, M // tm),
        in_specs=in_specs,
        out_specs=out_specs,
        out_shape=out_shape,
        compiler_params=_params(*sem),
    )(*args)
    return out[0] if len(out) == 1 else out


def mm_tn(a, b, *, name):
    M, K = a.shape
    N = b.shape[1]
    tm = next((t for t in (4 * ROW_TILE, 2 * ROW_TILE) if M % t == 0), min(ROW_TILE, M))
    tk = _col_tile(K)
    tn = _col_tile(N)
    assert M % tm == 0 and K % tk == 0 and N % tn == 0

    def body(a_ref, b_ref, o_ref):
        @pl.when(pl.program_id(2) == 0)
        def _():
            o_ref[...] = jnp.zeros_like(o_ref)

        o_ref[...] += _dot_tn(a_ref[...], b_ref[...])

    return pl.pallas_call(
        body,
        name=name,
        grid=(K // tk, N // tn, M // tm),
        in_specs=[pl.BlockSpec((tm, tk), lambda k, j, i: (i, k)),
                  pl.BlockSpec((tm, tn), lambda k, j, i: (i, j))],
        out_specs=pl.BlockSpec((tk, tn), lambda k, j, i: (k, j)),
        out_shape=jax.ShapeDtypeStruct((K, N), F32),
        compiler_params=_params("parallel", "parallel", "arbitrary"),
    )(a, b)


def _row_spec(tm, width):
    return pl.BlockSpec((tm, width), lambda i: (i, 0))


def _const_spec(shape):
    return pl.BlockSpec(shape, lambda *_: (0,) * len(shape))


def rmsnorm_fwd(h, g, *, name):
    S, D = h.shape
    tm = min(ROW_TILE, S)

    def body(h_ref, g_ref, u_ref):
        x = h_ref[...]
        rs = lax.rsqrt(jnp.mean(x * x, axis=1, keepdims=True) + EPS)
        u_ref[...] = (x * rs * g_ref[...]).astype(u_ref.dtype)

    return pl.pallas_call(
        body, name=name, grid=(S // tm,),
        in_specs=[_row_spec(tm, D), _const_spec((1, D))],
        out_specs=_row_spec(tm, D),
        out_shape=jax.ShapeDtypeStruct((S, D), BF16),
        compiler_params=_params("parallel"),
    )(h, g)


HALO = 16


def _shift_down(x, edge8, s):
    sh = pltpu.roll(x, s, 0)
    er = pltpu.roll(edge8, s, 0)
    row8 = lax.broadcasted_iota(jnp.int32, er.shape, 0)
    top = jnp.where(row8 < s, er, sh[0:8])
    return jnp.concatenate([top, sh[8:]], axis=0)


def _shift_up(x, s):
    return pltpu.roll(x, x.shape[0] - s, 0)


def _conv_pre(a, prev8, w_ref, cb_ref):
    a1 = _shift_down(a, prev8, 1)
    a2 = _shift_down(a, prev8, 2)
    return w_ref[2:3, :] * a + w_ref[1:2, :] * a1 + w_ref[0:1, :] * a2 + cb_ref[...]


def convffn_fwd(hh, cw8, cb):
    S = hh.shape[0]
    tm = min(ROW_TILE, S)
    tn = _col_tile(D_FF)
    nj = D_FF // tn

    def body(a_ref, ap_ref, b_ref, w_ref, cb_ref, o_ref, ac_ref):
        prev8 = jnp.where(pl.program_id(1) == 0, 0.0, ap_ref[...].astype(F32)[HALO - 8:HALO])
        ac = _conv_pre(a_ref[...].astype(F32), prev8, w_ref, cb_ref)
        ac_ref[...] = ac.astype(ac_ref.dtype)
        o_ref[...] = (ac * _sigmoid(ac) * b_ref[...].astype(F32)).astype(o_ref.dtype)

    rh = tm // HALO
    return pl.pallas_call(
        body, name="convffn_fwd", grid=(nj, S // tm),
        in_specs=[pl.BlockSpec((tm, tn), lambda j, i: (i, j)),
                  pl.BlockSpec((HALO, tn), lambda j, i: (jnp.maximum(i * rh - 1, 0), j)),
                  pl.BlockSpec((tm, tn), lambda j, i: (i, j + nj)),
                  pl.BlockSpec((8, tn), lambda j, i: (0, j)),
                  pl.BlockSpec((1, tn), lambda j, i: (0, j))],
        out_specs=[pl.BlockSpec((tm, tn), lambda j, i: (i, j))] * 2,
        out_shape=[jax.ShapeDtypeStruct((S, D_FF), BF16)] * 2,
        compiler_params=_params("parallel", "parallel"),
    )(hh, hh, hh, cw8, cb)


def convffn_bwd(hh, conv, dact, cw8):
    S = hh.shape[0]
    tm = min(ROW_TILE, S)
    tn = _col_tile(D_FF)
    nj = D_FF // tn
    ni = S // tm

    def body(a_ref, b_ref, bn_ref, c_ref, cn_ref, d_ref, dn_ref, w_ref, o_a, o_b, dw_ref):
        i = pl.program_id(1)

        @pl.when(i == 0)
        def _():
            dw_ref[...] = jnp.zeros_like(dw_ref)

        up = lambda r: r[...].astype(F32)
        ext = lambda cur, nxt: jnp.concatenate([up(cur), up(nxt)[0:8]], axis=0)
        b = ext(b_ref, bn_ref)
        ac = ext(c_ref, cn_ref)
        d = jnp.concatenate([up(d_ref), jnp.where(i == ni - 1, 0.0, up(dn_ref)[0:8])], axis=0)
        sa = _sigmoid(ac)
        silu = ac * sa
        o_b[...] = (d[0:tm] * silu[0:tm]).astype(o_b.dtype)
        dac = d * b * (sa + silu * (1.0 - sa))
        dc0 = dac[0:tm]
        dc1 = _shift_up(dac, 1)[0:tm]
        dc2 = _shift_up(dac, 2)[0:tm]
        o_a[...] = (w_ref[2:3, :] * dc0 + w_ref[1:2, :] * dc1 + w_ref[0:1, :] * dc2).astype(o_a.dtype)
        a = up(a_ref)
        dw_ref[0:1, :] += _colsum(dc2 * a)
        dw_ref[1:2, :] += _colsum(dc1 * a)
        dw_ref[2:3, :] += _colsum(dc0 * a)
        dw_ref[3:4, :] += _colsum(dc0)

    rh = tm // HALO
    last = S // HALO - 1
    cur = lambda off: pl.BlockSpec((tm, tn), lambda j, i: (i, j + off))
    nxt = lambda off: pl.BlockSpec((HALO, tn), lambda j, i: (jnp.minimum((i + 1) * rh, last), j + off))
    return pl.pallas_call(
        body, name="convffn_bwd", grid=(nj, ni),
        in_specs=[cur(0), cur(nj), nxt(nj), cur(0), nxt(0), cur(0), nxt(0),
                  pl.BlockSpec((8, tn), lambda j, i: (0, j))],
        out_specs=[cur(0), cur(0), pl.BlockSpec((8, tn), lambda j, i: (0, j))],
        out_shape=[jax.ShapeDtypeStruct((S, D_FF), BF16), jax.ShapeDtypeStruct((S, D_FF), BF16),
                   jax.ShapeDtypeStruct((8, D_FF), F32)],
        compiler_params=_params("parallel", "arbitrary"),
    )(hh, hh, hh, conv, conv, dact, dact, cw8)


def ple_fwd(h, gpre, p, wpu, norm_g):
    S, D = h.shape
    tm = min(ROW_TILE, S)

    def body(h_ref, g_ref, p_ref, w_ref, ng_ref, o_ref, u_ref):
        out = h_ref[...] + _sigmoid(g_ref[...].astype(F32)) * _dot(p_ref[...], w_ref[...])
        o_ref[...] = out
        u_ref[...] = (out * _rms_scale(out) * ng_ref[...]).astype(BF16)

    return pl.pallas_call(
        body, name="ple_fwd", grid=(S // tm,),
        in_specs=[_row_spec(tm, D), _row_spec(tm, D), _row_spec(tm, PLE_DIM), _const_spec((PLE_DIM, D)),
                  _const_spec((1, D))],
        out_specs=[_row_spec(tm, D), _row_spec(tm, D)],
        out_shape=[jax.ShapeDtypeStruct((S, D), F32), jax.ShapeDtypeStruct((S, D), BF16)],
        compiler_params=_params("parallel"),
    )(h, gpre, p, wpu, norm_g)


def ple_bwd(dh, gpre, p, wpu):
    S, D = dh.shape
    tm = min(ROW_TILE, S)

    def body(d_ref, g_ref, p_ref, w_ref, dpu_ref, dg_ref):
        d = d_ref[...]
        gate = _sigmoid(g_ref[...].astype(F32))
        pu = _dot(p_ref[...], w_ref[...])
        dpu_ref[...] = (d * gate).astype(dpu_ref.dtype)
        dg_ref[...] = (d * pu * gate * (1.0 - gate)).astype(dg_ref.dtype)

    return pl.pallas_call(
        body, name="ple_bwd", grid=(S // tm,),
        in_specs=[_row_spec(tm, D), _row_spec(tm, D), _row_spec(tm, PLE_DIM), _const_spec((PLE_DIM, D))],
        out_specs=[_row_spec(tm, D), _row_spec(tm, D)],
        out_shape=[jax.ShapeDtypeStruct((S, D), BF16)] * 2,
        compiler_params=_params("parallel"),
    )(dh, gpre, p, wpu)


def loss_head(h, g, tgt):
    S, D = h.shape
    tm = min(ROW_TILE, S)

    def body(h_ref, g_ref, t_ref, dh_ref, dhb_ref, l_ref, dg_ref):
        @pl.when(pl.program_id(0) == 0)
        def _():
            l_ref[...] = jnp.zeros_like(l_ref)
            dg_ref[...] = jnp.zeros_like(dg_ref)

        x = h_ref[...]
        gr = g_ref[...]
        rs = lax.rsqrt(jnp.mean(x * x, axis=1, keepdims=True) + EPS)
        xh = x * rs
        err = xh * gr - t_ref[...]
        l_ref[0:1, 0:1] += 0.5 * _colsum(jnp.mean(err * err, axis=1, keepdims=True))
        dy = err * (1.0 / D)
        dg_ref[0:1, :] += _colsum(dy * xh)
        dxh = dy * gr
        dh = rs * (dxh - xh * jnp.mean(dxh * xh, axis=1, keepdims=True))
        dh_ref[...] = dh
        dhb_ref[...] = dh.astype(BF16)

    return pl.pallas_call(
        body, name="loss_head", grid=(S // tm,),
        in_specs=[_row_spec(tm, D), _const_spec((1, D)), _row_spec(tm, D)],
        out_specs=[_row_spec(tm, D), _row_spec(tm, D), _const_spec((8, 128)), _const_spec((8, D))],
        out_shape=[jax.ShapeDtypeStruct((S, D), F32), jax.ShapeDtypeStruct((S, D), BF16),
                   jax.ShapeDtypeStruct((8, 128), F32), jax.ShapeDtypeStruct((8, D), F32)],
        compiler_params=_params("arbitrary"),
    )(h, g, tgt)


def _lb_rows(l_ref):
    l = l_ref[...]
    e = jnp.exp(l - jnp.max(l, axis=0, keepdims=True))
    p = e / _colsum(e)
    lbs, run = [], None
    for i in range(DEPTH):
        run = p[i:i + 1] if i == 0 else run + p[i:i + 1]
        lbs.append(run - p[0:1])
    return p, lbs


def lb_fwd(lb_logits):
    def body(l_ref, o_ref):
        _, lbs = _lb_rows(l_ref)
        o_ref[...] = jnp.zeros_like(o_ref)
        for i, lb in enumerate(lbs):
            o_ref[8 * i:8 * i + 1, :] = jnp.log(jnp.maximum(lb, LB_FLOOR))
            o_ref[8 * i + 1:8 * i + 2, :] = jnp.log1p(-lb)
            o_ref[8 * i + 2:8 * i + 3, :] = 1.0 - lb
            o_ref[8 * i + 3:8 * i + 4, :] = lb

    return pl.pallas_call(
        body, name="lb_fwd",
        out_shape=jax.ShapeDtypeStruct((DEPTH * 8, HG_WIDTH), F32),
    )(lb_logits)


def lb_bwd(dlbrows, lb_logits):
    def body(d_ref, l_ref, o_ref):
        p, lbs = _lb_rows(l_ref)
        dlb = []
        for i, lb in enumerate(lbs):
            da = d_ref[8 * i:8 * i + 1, :]
            dc = d_ref[8 * i + 1:8 * i + 2, :]
            do = d_ref[8 * i + 2:8 * i + 3, :]
            dlb.append(jnp.where(lb > LB_FLOOR, da / jnp.maximum(lb, LB_FLOOR), 0.0) - dc / (1.0 - lb) - do)
        dp = [jnp.zeros_like(dlb[0])]
        for j in range(1, DEPTH):
            acc = dlb[j]
            for i in range(j + 1, DEPTH):
                acc = acc + dlb[i]
            dp.append(acc)
        dot_ = p[0:1] * dp[0]
        for j in range(1, DEPTH):
            dot_ = dot_ + p[j:j + 1] * dp[j]
        o_ref[...] = jnp.zeros_like(o_ref)
        for j in range(DEPTH):
            o_ref[j:j + 1, :] = p[j:j + 1] * (dp[j] - dot_)

    return pl.pallas_call(
        body, name="lb_bwd",
        out_shape=jax.ShapeDtypeStruct((8, HG_WIDTH), F32),
    )(dlbrows, lb_logits)


def adamw(w, g, m, v, *, name):
    R, C = w.shape
    tr = next((t for t in (512, 256, 128, 64, 32, 16, 8) if R % t == 0), R)

    def body(w_ref, g_ref, m_ref, v_ref, d_ref, m2_ref, v2_ref):
        gv = g_ref[...]
        m2 = ADAM_B1 * m_ref[...] + (1.0 - ADAM_B1) * gv
        v2 = ADAM_B2 * v_ref[...] + (1.0 - ADAM_B2) * (gv * gv)
        mh = m2 / (1.0 - ADAM_B1 ** ADAM_STEP)
        vh = v2 / (1.0 - ADAM_B2 ** ADAM_STEP)
        d_ref[...] = -ADAM_LR * (mh / (jnp.sqrt(vh) + ADAM_EPS) + ADAM_WD * w_ref[...])
        m2_ref[...] = m2
        v2_ref[...] = v2

    spec = pl.BlockSpec((tr, C), lambda i: (i, 0))
    return pl.pallas_call(
        body, name=name, grid=(R // tr,),
        in_specs=[spec] * 4, out_specs=[spec] * 3,
        out_shape=[jax.ShapeDtypeStruct((R, C), F32)] * 3,
        compiler_params=_params("parallel"),
    )(w, g, m, v)


def _slot_rows(R):
    return R if R <= 1024 else next((t for t in (848, 768, 704, 672, 512, 448, 352, 256, 128, 64, 16) if R % t == 0), R)


def add_pair(a, b):
    R, C = a.shape
    tr = _slot_rows(R)

    def body(a_ref, b_ref, o_ref):
        o_ref[...] = (a_ref[...].astype(F32) + b_ref[...].astype(F32)).astype(o_ref.dtype)

    spec = pl.BlockSpec((tr, C), lambda i: (i, 0))
    return pl.pallas_call(
        body, name="sum_pair", grid=(R // tr,),
        in_specs=[spec, spec], out_specs=spec,
        out_shape=jax.ShapeDtypeStruct((R, C), a.dtype),
        compiler_params=_params("parallel"),
    )(a, b)


def sum_slots(x, *, out_dtype, name):
    n, R, C = x.shape
    tr = _slot_rows(R)

    def body(x_ref, o_ref):
        acc = x_ref[0].astype(F32)
        for k in range(1, n):
            acc = acc + x_ref[k].astype(F32)
        o_ref[...] = acc.astype(o_ref.dtype)

    return pl.pallas_call(
        body, name=name, grid=(R // tr,),
        in_specs=[pl.BlockSpec((n, tr, C), lambda i: (0, i, 0))],
        out_specs=pl.BlockSpec((tr, C), lambda i: (i, 0)),
        out_shape=jax.ShapeDtypeStruct((R, C), out_dtype),
        compiler_params=_params("parallel"),
    )(x)


MESH = pl.DeviceIdType.MESH
ANY = pl.BlockSpec(memory_space=pl.ANY)


def _place():
    return lax.axis_index("x"), lax.axis_index("y"), lax.axis_index("c")


def _other_chips(x, y):
    return [(1 - x, y), (x, 1 - y), (1 - x, 1 - y)]


def small_allgather(buf):
    R, C = buf.shape

    def body(x_ref, out_ref, send_sems, recv_sems, local_sem):
        x, y, c = _place()
        me, sibling = (x, y, c), (x, y, 1 - c)
        chips = _other_chips(x, y)

        def slot(px, py, pc):
            return out_ref.at[4 * px + 2 * py + pc]

        def copy(k, block, to, src=None):
            return pltpu.make_async_remote_copy(
                src_ref=slot(*block) if src is None else src, dst_ref=slot(*block),
                send_sem=send_sems.at[k], recv_sem=recv_sems.at[k],
                device_id=to, device_id_type=MESH)

        mine = pltpu.make_async_copy(x_ref, slot(*me), local_sem)
        mine.start()
        first = [copy(0, me, sibling, src=x_ref)]
        first += [copy(1 + r, me, (*chip, c), src=x_ref) for r, chip in enumerate(chips)]
        for cp in first:
            cp.start()
        passed = [copy(4 + r, (*chip, c), sibling) for r, chip in enumerate(chips)]
        for r, chip in enumerate(chips):
            copy(1 + r, (*chip, c), me).wait_recv()
            passed[r].start()
        copy(0, sibling, me).wait_recv()
        for r, chip in enumerate(chips):
            copy(4 + r, (*chip, 1 - c), me).wait_recv()
        for cp in first + passed:
            cp.wait_send()
        mine.wait()

    return pl.pallas_call(
        body, name="small_allgather",
        out_shape=jax.ShapeDtypeStruct((8, R, C), buf.dtype),
        in_specs=[pl.BlockSpec(memory_space=pltpu.VMEM)],
        out_specs=pl.BlockSpec(memory_space=pltpu.VMEM),
        scratch_shapes=[pltpu.SemaphoreType.DMA((7,)), pltpu.SemaphoreType.DMA((7,)),
                        pltpu.SemaphoreType.DMA],
    )(buf)


def weights_allgather(wp):
    def body(w_ref, g_ref, send_sems, recv_sems, local_sem):
        gather = _Gather(w_ref, g_ref, send_sems, recv_sems, local_sem)
        gather.start()
        gather.finish()

    return pl.pallas_call(
        body, name="weights_allgather",
        out_shape=jax.ShapeDtypeStruct((4,) + wp.shape, wp.dtype),
        in_specs=[ANY], out_specs=ANY,
        scratch_shapes=_Gather.SCRATCH,
    )(wp)


class _Gather:
    SCRATCH = [pltpu.SemaphoreType.DMA((6,)), pltpu.SemaphoreType.DMA((6,)), pltpu.SemaphoreType.DMA]

    def __init__(self, w_ref, g_ref, send_sems, recv_sems, local_sem):
        self.w_ref, self.g_ref, self.local_sem = w_ref, g_ref, local_sem
        self.send_sems, self.recv_sems = send_sems, recv_sems
        self.x, self.y, self.c = _place()
        self.chips = _other_chips(self.x, self.y)
        half = w_ref.shape[0] // 2
        self.mine = pl.ds(pl.multiple_of(self.c * half, 16), half)
        self.theirs = pl.ds(pl.multiple_of((1 - self.c) * half, 16), half)

    def _copy(self, k, chip_block, rows, to, src=None):
        dst = self.g_ref.at[chip_block, rows]
        return pltpu.make_async_remote_copy(
            src_ref=dst if src is None else src, dst_ref=dst,
            send_sem=self.send_sems.at[k], recv_sem=self.recv_sems.at[k],
            device_id=to, device_id_type=MESH)

    def _own(self):
        return pltpu.make_async_copy(self.w_ref, self.g_ref.at[2 * self.x + self.y], self.local_sem)

    def _first(self):
        return [self._copy(r, 2 * self.x + self.y, self.mine, (*chip, self.c), src=self.w_ref.at[self.mine])
                for r, chip in enumerate(self.chips)]

    def start(self):
        self._own().start()
        for cp in self._first():
            cp.start()

    def finish(self):
        sibling = (self.x, self.y, 1 - self.c)
        passed = [self._copy(3 + r, 2 * chip[0] + chip[1], self.mine, sibling) for r, chip in enumerate(self.chips)]
        for r, chip in enumerate(self.chips):
            self._copy(r, 2 * chip[0] + chip[1], self.mine, (*chip, self.c)).wait_recv()
            passed[r].start()
        for r, chip in enumerate(self.chips):
            self._copy(3 + r, 2 * chip[0] + chip[1], self.theirs, sibling).wait_recv()
        for cp in self._first() + passed:
            cp.wait_send()
        self._own().wait()


def sibling_swap(v, *, name):
    def body(v_ref, got_ref, send_sem, recv_sem):
        x, y, c = _place()
        cp = pltpu.make_async_remote_copy(
            src_ref=v_ref, dst_ref=got_ref, send_sem=send_sem, recv_sem=recv_sem,
            device_id=(x, y, 1 - c), device_id_type=MESH)
        cp.start()
        cp.wait()

    return pl.pallas_call(
        body, name=name,
        out_shape=jax.ShapeDtypeStruct(v.shape, v.dtype),
        in_specs=[ANY], out_specs=ANY,
        scratch_shapes=[pltpu.SemaphoreType.DMA, pltpu.SemaphoreType.DMA],
    )(v)


def chip_exchange(q):
    def body(q_ref, r_ref, send_sems, recv_sems, local_sem):
        exchange = _Exchange(q_ref, r_ref, send_sems, recv_sems, local_sem)
        exchange.start()
        exchange.finish()

    return pl.pallas_call(
        body, name="chip_exchange",
        out_shape=jax.ShapeDtypeStruct(q.shape, q.dtype),
        in_specs=[ANY], out_specs=ANY,
        scratch_shapes=_Exchange.SCRATCH,
    )(q)


class _Exchange:
    SCRATCH = [pltpu.SemaphoreType.DMA((3,)), pltpu.SemaphoreType.DMA((3,)), pltpu.SemaphoreType.DMA]

    def __init__(self, q_ref, r_ref, send_sems, recv_sems, local_sem):
        self.q_ref, self.r_ref, self.local_sem = q_ref, r_ref, local_sem
        self.send_sems, self.recv_sems = send_sems, recv_sems
        self.x, self.y, self.c = _place()
        self.j = 2 * self.x + self.y
        self.chips = _other_chips(self.x, self.y)

    def _copy(self, r, src_block, dst_block, chip):
        return pltpu.make_async_remote_copy(
            src_ref=self.q_ref.at[src_block], dst_ref=self.r_ref.at[dst_block],
            send_sem=self.send_sems.at[r], recv_sem=self.recv_sems.at[r],
            device_id=(*chip, self.c), device_id_type=MESH)

    def _own(self):
        return pltpu.make_async_copy(self.q_ref.at[self.j], self.r_ref.at[self.j], self.local_sem)

    def _sends(self):
        return [self._copy(r, 2 * chip[0] + chip[1], self.j, chip) for r, chip in enumerate(self.chips)]

    def start(self):
        self._own().start()
        for cp in self._sends():
            cp.start()

    def finish(self):
        for r, chip in enumerate(self.chips):
            jr = 2 * chip[0] + chip[1]
            self._copy(r, jr, jr, chip).wait_recv()
        for cp in self._sends():
            cp.wait_send()
        self._own().wait()


N_CHIPS = 4
_PACK = (("w_in", 704), ("w_out", 256), ("w_up", 1408), ("w_down", 704), ("w_ple_gate", 256), ("w_ple_up", 64))
LAYER_ROWS = sum(r for _, r in _PACK)
PACK_ROWS = DEPTH * LAYER_ROWS


def _pack_shards(sh):
    parts = []
    for i in range(DEPTH):
        for name, rows in _PACK:
            parts.append(sh[name][i].reshape(rows, D_MODEL))
    return jnp.concatenate(parts, axis=0)


def _unpack_shards(slab):
    shapes = {"w_in": (D_MODEL, IN_WIDTH // N_CHIPS), "w_out": (D_MODEL // N_CHIPS, D_MODEL),
              "w_up": (D_MODEL, 2 * D_FF // N_CHIPS), "w_down": (D_FF // N_CHIPS, D_MODEL),
              "w_ple_gate": (D_MODEL // N_CHIPS, D_MODEL), "w_ple_up": (PLE_DIM, D_MODEL // N_CHIPS)}
    out = {name: [] for name, _ in _PACK}
    off = 0
    for i in range(DEPTH):
        for name, rows in _PACK:
            out[name].append(slab[off:off + rows].reshape(shapes[name]))
            off += rows
    return {k: jnp.stack(v) for k, v in out.items()}


_COL_SHARDED = ("w_in", "w_up", "w_ple_up")


def _full_from_chips(g, pack=_PACK):
    per_chip = [_unpack_shards_layer(g[k], pack) for k in range(N_CHIPS)]
    return {name: jnp.concatenate([pc[name] for pc in per_chip], axis=1 if name in _COL_SHARDED else 0)
            for name, _ in pack}


def _unpack_shards_layer(slab, pack):
    shapes = {"w_in": (D_MODEL, IN_WIDTH // N_CHIPS), "w_out": (D_MODEL // N_CHIPS, D_MODEL),
              "w_up": (D_MODEL, 2 * D_FF // N_CHIPS), "w_down": (D_FF // N_CHIPS, D_MODEL),
              "w_ple_gate": (D_MODEL // N_CHIPS, D_MODEL), "w_ple_up": (PLE_DIM, D_MODEL // N_CHIPS)}
    out = {}
    off = 0
    for name, rows in pack:
        out[name] = slab[off:off + rows].reshape(shapes[name])
        off += rows
    return out


def _split_to_chips(full, name):
    r, c = full.shape
    if name in _COL_SHARDED:
        full = full.reshape(r, N_CHIPS, c // N_CHIPS).transpose(1, 0, 2)
    return full.reshape(N_CHIPS, -1, D_MODEL)


_SMALL = (("loss", 128), ("g_final", 1024), ("g_mix", 4096), ("lb_logits", 2048), ("hg_norm_g", 2048),
          ("attn_sinks", 128), ("g_ffn", 4096), ("conv_w", 4 * 3 * D_FF), ("conv_b", 4 * D_FF), ("g_ple", 4096))
SMALL_ROWS = 496


def _pack_small(d):
    parts = []
    for name, n in _SMALL:
        v = d[name].reshape(-1).astype(F32)
        parts.append(jnp.pad(v, (0, n - v.shape[0])))
    flat = jnp.concatenate(parts)
    return jnp.pad(flat, (0, SMALL_ROWS * 128 - flat.shape[0])).reshape(SMALL_ROWS, 128)


def _unpack_small(buf, shapes):
    flat = buf.reshape(-1)
    out, off = {}, 0
    for name, n in _SMALL:
        size = 1
        for s in shapes[name]:
            size *= s
        out[name] = flat[off:off + size].reshape(shapes[name])
        off += n
    return out


WEIGHT_ORDER = ('g_mix', 'w_in', 'lb_logits', 'hg_norm_g', 'attn_sinks', 'w_out', 'g_ffn', 'w_up', 'conv_w',
                'conv_b', 'w_down', 'g_ple', 'w_ple_gate', 'w_ple_up', 'g_final')


def kernel(x, p, g_mix, w_in, lb_logits, hg_norm_g, attn_sinks, w_out, g_ffn, w_up, conv_w, conv_b, w_down, g_ple, w_ple_gate, w_ple_up, g_final, loss_target, m_g_mix, m_w_in, m_lb_logits, m_hg_norm_g, m_attn_sinks, m_w_out, m_g_ffn, m_w_up, m_conv_w, m_conv_b, m_w_down, m_g_ple, m_w_ple_gate, m_w_ple_up, m_g_final, v_g_mix, v_w_in, v_lb_logits, v_hg_norm_g, v_attn_sinks, v_w_out, v_g_ffn, v_w_up, v_conv_w, v_conv_b, v_w_down, v_g_ple, v_w_ple_gate, v_w_ple_up, v_g_final):
    W = dict(g_mix=g_mix, w_in=w_in, lb_logits=lb_logits, hg_norm_g=hg_norm_g, attn_sinks=attn_sinks,
             w_out=w_out, g_ffn=g_ffn, w_up=w_up, conv_w=conv_w, conv_b=conv_b, w_down=w_down, g_ple=g_ple,
             w_ple_gate=w_ple_gate, w_ple_up=w_ple_up, g_final=g_final)
    M = dict(g_mix=m_g_mix, w_in=m_w_in, lb_logits=m_lb_logits, hg_norm_g=m_hg_norm_g, attn_sinks=m_attn_sinks,
             w_out=m_w_out, g_ffn=m_g_ffn, w_up=m_w_up, conv_w=m_conv_w, conv_b=m_conv_b, w_down=m_w_down,
             g_ple=m_g_ple, w_ple_gate=m_w_ple_gate, w_ple_up=m_w_ple_up, g_final=m_g_final)
    V = dict(g_mix=v_g_mix, w_in=v_w_in, lb_logits=v_lb_logits, hg_norm_g=v_hg_norm_g, attn_sinks=v_attn_sinks,
             w_out=v_w_out, g_ffn=v_g_ffn, w_up=v_w_up, conv_w=v_conv_w, conv_b=v_conv_b, w_down=v_w_down,
             g_ple=v_g_ple, w_ple_gate=v_w_ple_gate, w_ple_up=v_w_ple_up, g_final=v_g_final)
    S = x.shape[1]
    hg_rows = min(ROW_TILE, S)
    xi, yi, ci = _place()
    chip = 2 * xi + yi

    slab = _pack_shards({n: W[n] for n, _ in _PACK}).astype(BF16).reshape(DEPTH, LAYER_ROWS, D_MODEL)
    first_rows = _PACK[0][1]
    gathered = (weights_allgather(slab[0, :first_rows]),)
    cw_shard = jnp.pad(conv_w.reshape(-1), (0, 72 * 128 - conv_w.size)).reshape(72, 128)
    cw_all = small_allgather(cw_shard)
    cw_full = jnp.concatenate(
        [cw_all[2 * k].reshape(-1)[:conv_w.size].reshape(conv_w.shape) for k in range(N_CHIPS)], axis=2)
    lbrows = lb_fwd(lb_logits)
    at_mask = swa_mask()

    h = x[0]
    saved = []
    for i in range(DEPTH):
        wf = _full_from_chips(gathered[0], _PACK[:1] if i == 0 else _PACK)
        lbr = lbrows[8 * i:8 * i + 8]
        ng = hg_norm_g[i][None]
        sinks_b = jnp.pad(jnp.repeat(attn_sinks[i].reshape(AT_KV_HEADS, 1, AT_GROUP), WINDOW, axis=2),
                          ((0, 0), (0, 7), (0, 0))).reshape(8 * AT_KV_HEADS, GROUP_LANES)
        cw8 = jnp.pad(cw_full[i], ((0, 5), (0, 0)))
        cb = conv_b[i][None]
        if i == 0:
            u = rmsnorm_fwd(h, g_mix[0][None], name="rmsnorm_fwd")
        proj = mm(u, wf["w_in"], name="mm_in")
        riders = ([slab[0, first_rows:]] if i == 0 else []) + ([slab[i + 1]] if i + 1 < DEPTH else [])
        y, o_raw, states, *got = hgrn_fwd(proj, lbr, ng, D_MODEL, rows=hg_rows, gather=riders)
        if i == 0:
            wf.update(_full_from_chips(got[0], _PACK[1:]))
        gathered = got[-1:]
        y = swa_fwd(proj, sinks_b, at_mask, y)
        h1, u2 = mm(y, wf["w_out"], res=h, norm_g=g_ffn[i][None], name="mm_out")
        hh = mm(u2, wf["w_up"], out_dtype=BF16, name="mm_up")
        act, conv = convffn_fwd(hh, cw8, cb)
        h2, u3 = mm(act, wf["w_down"], res=h1, norm_g=g_ple[i][None], name="mm_down")
        gpre = mm(u3, wf["w_ple_gate"], out_dtype=BF16, name="mm_gate")
        next_g = g_mix[i + 1] if i + 1 < DEPTH else g_final
        h3, u_next = ple_fwd(h2, gpre, p[i, 0], wf["w_ple_up"], next_g[None])
        saved.append(dict(wf=wf, lbr=lbr, ng=ng, sinks_b=sinks_b, cw8=cw8, cb=cb, h=h, u=u, proj=proj,
                          o_raw=o_raw, states=states, y=y, h1=h1, u2=u2, hh=hh, conv=conv, act=act, h2=h2, u3=u3,
                          gpre=gpre))
        h, u = h3, u_next

    dh, dhb, loss_acc, dg_final = loss_head(h, g_final[None], loss_target[0])

    gfull = {n: [None] * DEPTH for n, _ in _PACK}
    gsmall = {n: [None] * DEPTH for n in ("g_mix", "hg_norm_g", "attn_sinks", "g_ffn", "conv_w", "conv_b", "g_ple")}
    dlbrows = [None] * DEPTH
    def pair_sums(i, pack):
        pk = jnp.concatenate([_split_to_chips(gfull[name][i], name) for name, _ in pack], axis=1).astype(BF16)
        half = pk.shape[1] // 2
        pk = pk.reshape(N_CHIPS, 2, half, D_MODEL)
        p_mine = lax.dynamic_index_in_dim(pk, ci, axis=1, keepdims=False)
        p_other = lax.dynamic_index_in_dim(pk, 1 - ci, axis=1, keepdims=False)
        from_sib = sibling_swap(p_other, name="sibling_swap_partials")
        pair = add_pair(p_mine.reshape(-1, D_MODEL), from_sib.reshape(-1, D_MODEL))
        return pair.reshape(N_CHIPS, half, D_MODEL)

    from_chips = {}
    pending = []
    for i in reversed(range(DEPTH)):
        s = saved[i]
        wf = s["wf"]
        dpu, dgp = ple_bwd(dh, s["gpre"], p[i, 0], wf["w_ple_up"])
        gfull["w_ple_up"][i] = mm_tn(p[i, 0], dpu, name="mm_tn_pu")
        gfull["w_ple_gate"][i] = mm_tn(s["u3"], dgp, name="mm_tn_gate")
        dh2, dh2b, dg = mm(dgp, wf["w_ple_gate"], nt=True, rms_bwd=(s["h2"], g_ple[i][None], dh),
                           name="mm_nt_gate")
        gsmall["g_ple"][i] = dg[0]
        gfull["w_down"][i] = mm_tn(s["act"], dh2b, name="mm_tn_down")
        dact = mm(dh2b, wf["w_down"], nt=True, out_dtype=BF16, name="mm_nt_down")
        da, db, dcw = convffn_bwd(s["hh"], s["conv"], dact, s["cw8"])
        gsmall["conv_w"][i] = dcw[0:3]
        gsmall["conv_b"][i] = dcw[3]
        gfull["w_up"][i] = jnp.concatenate([mm_tn(s["u2"], da, name="mm_tn_up"),
                                            mm_tn(s["u2"], db, name="mm_tn_up")], axis=1)
        dh1, dh1b, dg = mm((da, db), wf["w_up"], nt=True, rms_bwd=(s["h1"], g_ffn[i][None], dh2),
                           name="mm_nt_up")
        gsmall["g_ffn"][i] = dg[0]
        gfull["w_out"][i] = mm_tn(s["y"], dh1b, name="mm_tn_out")
        dy = mm(dh1b, wf["w_out"], nt=True, name="mm_nt_out")
        dq_at, dko, dkp, dvo, dvp, dsk = swa_bwd(s["proj"], s["sinks_b"], at_mask, dy)
        gsmall["attn_sinks"][i] = dsk.reshape(AT_KV_HEADS, 8, GROUP_LANES)[:, 1:1 + AT_GROUP, 0].reshape(-1)
        hg_args = (s["proj"], s["o_raw"], s["states"], dy, s["lbr"], s["ng"])
        riders = pending + ([("0 rest", pair_sums(0, _PACK[1:]))] if i == 0 else [])
        hq, hz, hv, hgp, dlbr, dng, *got = hgrn_bwd(*hg_args, rows=hg_rows, exchange=[q for _, q in riders])
        from_chips.update({unit: g for (unit, _), g in zip(riders, got)})
        dlbrows[i] = dlbr
        gsmall["hg_norm_g"][i] = dng[0]
        dproj = assemble_dproj((hq, hz, hv, hgp), dq_at, dko, dkp, dvo, dvp, rows=hg_rows)
        gfull["w_in"][i] = mm_tn(s["u"], dproj, name="mm_tn_in")
        dh, dhb, dg = mm(dproj, wf["w_in"], nt=True, rms_bwd=(s["h"], g_mix[i][None], dh1), name="mm_nt_in")
        gsmall["g_mix"][i] = dg[0]
        pending = [("%d" % i, pair_sums(i, _PACK))] if i > 0 else [("0 w_in", pair_sums(0, _PACK[:1]))]
    from_chips["0 w_in"] = chip_exchange(pending[0][1])
    grad_x = dh[None]
    dlb_logits = lb_bwd(jnp.concatenate(dlbrows, axis=0), lb_logits)[0:DEPTH]

    units = ["0 w_in", "0 rest"] + ["%d" % i for i in range(1, DEPTH)]
    sums = [sum_slots(from_chips[unit], out_dtype=F32, name="sum_chips") for unit in units]
    mine_sum = jnp.concatenate(sums, axis=0)
    sib_sum = sibling_swap(mine_sum, name="sibling_swap_sums")
    lo = jnp.where(ci == 0, mine_sum, sib_sum)
    hi = jnp.where(ci == 0, sib_sum, mine_sum)
    parts, off = [], 0
    for part in sums:
        n = part.shape[0]
        parts += [lo[off:off + n], hi[off:off + n]]
        off += n
    gshard = _unpack_shards(jnp.concatenate(parts, axis=0))

    small = dict(loss=loss_acc[0, 0:1], g_final=dg_final[0], lb_logits=dlb_logits,
                 **{n: jnp.stack(v) for n, v in gsmall.items()})
    small_sum = sum_slots(small_allgather(_pack_small(small)), out_dtype=F32, name="sum_small")
    shapes = {n: W[n].shape for n in W}
    shapes["loss"] = (1,)
    shapes["conv_w"] = (DEPTH, 3, D_FF)
    gs = _unpack_small(small_sum, shapes)
    loss = gs["loss"][0]
    cshard = conv_w.shape[2]
    grads = dict(gshard)
    for n in ("g_mix", "lb_logits", "hg_norm_g", "attn_sinks", "g_ffn", "conv_b", "g_ple", "g_final"):
        grads[n] = gs[n]
    grads["conv_w"] = lax.dynamic_slice_in_dim(gs["conv_w"], chip * cshard, cshard, axis=2)

    delta, new_m, new_v = {}, {}, {}
    small_names = ("g_final", "g_mix", "lb_logits", "hg_norm_g", "attn_sinks", "g_ffn", "conv_b", "g_ple")
    sshapes = {n: W[n].shape for n in small_names}

    def pack_s(d):
        z = dict(d)
        z["loss"] = jnp.zeros((1,), F32)
        z["conv_w"] = jnp.zeros((1,), F32)
        return _pack_small(z)

    sd, sm, sv = adamw(pack_s(W), pack_s(grads), pack_s(M), pack_s(V), name="adamw_small")
    for out, buf in ((delta, sd), (new_m, sm), (new_v, sv)):
        un = _unpack_small(buf, {**sshapes, "loss": (1,), "conv_w": (1,)})
        for n in small_names:
            out[n] = un[n]
    for n in ("w_in", "w_out", "w_up", "w_down", "w_ple_gate", "w_ple_up", "conv_w"):
        shp = W[n].shape
        two_d = (-1, shp[-1])
        d_, m_, v_ = adamw(W[n].reshape(two_d), grads[n].reshape(two_d), M[n].reshape(two_d),
                           V[n].reshape(two_d), name="adamw_" + n)
        delta[n], new_m[n], new_v[n] = d_.reshape(shp), m_.reshape(shp), v_.reshape(shp)

    return (loss, grad_x, *[grads[n] for n in WEIGHT_ORDER], *[delta[n] for n in WEIGHT_ORDER],
            *[new_m[n] for n in WEIGHT_ORDER], *[new_v[n] for n in WEIGHT_ORDER])
```

```python
import jax
import jax.numpy as jnp
from jax import lax
from jax.experimental import pallas as pl
from jax.experimental.pallas import tpu as pltpu

F32 = jnp.float32
BF16 = jnp.bfloat16

D_MODEL = 1024
DEPTH = 4
PLE_DIM = 256
HG_WIDTH = 512
HG_HEADS = 4
HG_DK = 128
HG_CHUNK = 64
HG_SUB = 16
AT_WIDTH = 512
AT_HEAD_DIM = 64
AT_KV_HEADS = 2
AT_GROUP = 4
WINDOW = 128
D_FF = 2816
IN_WIDTH = 2816
EPS = 1e-6
MASK_VALUE = -1e30
LB_FLOOR = 1e-30

ADAM_LR = 0.001
ADAM_B1 = 0.9
ADAM_B2 = 0.999
ADAM_EPS = 1e-08
ADAM_WD = 0.01
ADAM_STEP = 10

VMEM_LIMIT = 48 * 1024 * 1024


def _params(*sem):
    return pltpu.CompilerParams(dimension_semantics=sem, vmem_limit_bytes=VMEM_LIMIT)


def _dot(a, b, dims=(((1,), (0,)), ((), ()))):
    return lax.dot_general(a.astype(BF16), b.astype(BF16), dims, preferred_element_type=F32)


def _dot_nt(a, b):
    return _dot(a, b, (((1,), (1,)), ((), ())))


def _dot_tn(a, b):
    return _dot(a, b, (((0,), (0,)), ((), ())))


def _dot_exact(sel, x, dims=(((1,), (0,)), ((), ()))):
    hi = x.astype(BF16)
    r1 = x - hi.astype(F32)
    mid = r1.astype(BF16)
    lo = (r1 - mid.astype(F32)).astype(BF16)
    s = sel.astype(BF16)
    one = lambda p: lax.dot_general(s, p, dims, preferred_element_type=F32)
    return one(hi) + one(mid) + one(lo)


def _sigmoid(x):
    return 0.5 * jnp.tanh(0.5 * x) + 0.5


def _logsig(x):
    return jnp.minimum(x, 0.0) - jnp.log(1.0 + jnp.exp(-jnp.abs(x)))


def _colsum(x):
    return jnp.sum(x, axis=0, keepdims=True)


def _rowsum(x):
    return jnp.sum(x, axis=1, keepdims=True)


def _colsum8(xs):
    row = lax.broadcasted_iota(jnp.int32, xs[0].shape, 0)

    def merge(a, b, keep_a, step):
        return jnp.where(keep_a, a + pltpu.roll(a, 8 - step, 0), b + pltpu.roll(b, step, 0))

    c = [merge(xs[j], xs[j + 4], row < 4, 4) for j in range(4)]
    d = [merge(c[j], c[j + 2], (row & 3) < 2, 2) for j in range(2)]
    return merge(d[0], d[1], (row & 1) == 0, 1)


def _tri(n):
    r = lax.broadcasted_iota(jnp.int32, (n, n), 0)
    c = lax.broadcasted_iota(jnp.int32, (n, n), 1)
    return (r >= c).astype(F32)


def _hg_gates(qp, z, a, c, oml):
    sq = _sigmoid(qp)
    q = qp * sq
    t = c + _logsig(z)
    mx = jnp.maximum(a, t)
    logf = mx + jnp.log(1.0 + jnp.exp(-jnp.abs(a - t)))
    snz = _sigmoid(-z)
    k = oml * snz
    return q, sq, t, logf, snz, k


_HEADS = range(HG_HEADS)


def _lanes(h):
    return slice(h * HG_DK, (h + 1) * HG_DK)


def _head(x, h):
    return x[:, _lanes(h)]


def _row_masks():
    row8 = lax.broadcasted_iota(jnp.int32, (8, HG_DK), 0)
    return [None] + [jnp.where(row8 >= j, 0.0, MASK_VALUE) for j in range(1, 8)]


def _hg_chunk_fwd(q, k, v, logf, st, b_s, k_s, v_s):
    C, U = HG_CHUNK, HG_SUB
    tri = _tri(C)
    b = [_dot_exact(tri, logf[h]) for h in _HEADS]
    for h in _HEADS:
        b_s[h] = b[h]
        k_s[h] = k[h]
        v_s[h] = v[h]
    o = [_dot_nt(q[h] * jnp.exp(b[h]), st[h]) for h in _HEADS]
    bl = [b[h][C - 1:C] for h in _HEADS]
    upd = [_dot_tn(v[h], k[h] * jnp.exp(bl[h] - b[h])) for h in _HEADS]
    rows = lax.broadcasted_iota(jnp.int32, (C, HG_DK), 0)
    nmask = _row_masks()
    outs = [[] for _ in _HEADS]
    for i in range(C // U):
        lo = i * U
        b_i = [b[h][lo:lo + U] for h in _HEADS]
        q_i = [q[h][lo:lo + U] for h in _HEADS]
        o_i = [o[h][lo:lo + U] for h in _HEADS]
        if i > 0:
            qe = [q_i[h] * jnp.exp(b_i[h] - b_i[h][0:1]) for h in _HEADS]
            ke = [jnp.where(rows < lo, k[h] * jnp.exp(jnp.minimum(b_i[h][0:1] - b[h], 0.0)), 0.0) for h in _HEADS]
            att = [_dot_nt(qe[h], ke[h]) for h in _HEADS]
            off = [_dot(att[h], v[h]) for h in _HEADS]
            o_i = [o_i[h] + off[h] for h in _HEADS]
        pieces = [[o_i[h][8 * f:8 * f + 8] for f in range(U // 8)] for h in _HEADS]
        for s in range(U):
            for f in range(s // 8, U // 8):
                for h in _HEADS:
                    bs = b_s[h, lo + s:lo + s + 1, :]
                    ks = k_s[h, lo + s:lo + s + 1, :]
                    vs = v_s[h, lo + s:lo + s + 1, :]
                    arg = b_i[h][8 * f:8 * f + 8] - bs
                    if s > 8 * f:
                        arg = arg + nmask[s - 8 * f]
                    w = _rowsum(q_i[h][8 * f:8 * f + 8] * jnp.exp(arg) * ks)
                    pieces[h][f] = pieces[h][f] + w * vs
        for h in _HEADS:
            outs[h] += pieces[h]
    o = [jnp.concatenate(outs[h], axis=0) for h in _HEADS]
    st_new = [st[h] * jnp.exp(bl[h]) + upd[h] for h in _HEADS]
    return o, st_new, b


def _hg_post(o, gp, ng):
    rs = lax.rsqrt(jnp.mean(o * o, axis=1, keepdims=True) + EPS)
    sg = _sigmoid(gp)
    return o * rs * ng * sg, rs, sg


def hgrn_fwd(proj, lbrows, ng, y_width, *, rows, gather=()):
    S = proj.shape[0]
    C = HG_CHUNK
    cpb = rows // C
    nb = S // rows
    ng_ = len(gather)

    def body(qp_ref, z_ref, v_ref, gp_ref, lb_ref, ng_ref, *rest):
        w_refs, rest = rest[:ng_], rest[ng_:]
        y_ref, o_ref, st_ref = rest[:3]
        g_refs, rest = rest[3:3 + ng_], rest[3 + ng_:]
        st, b_s, k_s, v_s = rest[:4]
        sems = rest[4:]
        comms = [_Gather(w_refs[i], g_refs[i], *sems[3 * i:3 * i + 3]) for i in range(ng_)]

        @pl.when(pl.program_id(0) == 0)
        def _():
            st[...] = jnp.zeros_like(st)
            for comm in comms:
                comm.start()

        a, c, oml = lb_ref[0:1, :], lb_ref[1:2, :], lb_ref[2:3, :]
        ngr = ng_ref[...]

        def chunk(ci, carry):
            off = pl.multiple_of(ci * C, C)
            sl = pl.ds(off, C)
            for h in _HEADS:
                st_ref[h, ci] = st[h]
            gates = [_hg_gates(qp_ref[sl, _lanes(h)], z_ref[sl, _lanes(h)],
                               _head(a, h), _head(c, h), _head(oml, h)) for h in _HEADS]
            q = [g[0] for g in gates]
            logf = [g[3] for g in gates]
            k = [g[5] for g in gates]
            v = [v_ref[sl, _lanes(h)] for h in _HEADS]
            o, st_new, _ = _hg_chunk_fwd(q, k, v, logf, [st[h] for h in _HEADS], b_s, k_s, v_s)
            for h in _HEADS:
                y, _, _ = _hg_post(o[h], gp_ref[sl, _lanes(h)], _head(ngr, h))
                y_ref[sl, _lanes(h)] = y.astype(y_ref.dtype)
                o_ref[sl, _lanes(h)] = o[h]
                st[h] = st_new[h]
            return carry

        lax.fori_loop(0, cpb, chunk, 0)

        for comm in comms:
            pl.when(pl.program_id(0) == nb - 1)(comm.finish)

    col = lambda kblk: pl.BlockSpec((rows, HG_WIDTH), lambda r: (r, kblk))
    in_specs = [col(0), col(1), col(2), col(3), _const_spec((8, HG_WIDTH)), _const_spec((1, HG_WIDTH))]
    out_specs = [col(0), col(0), pl.BlockSpec((HG_HEADS, cpb, HG_DK, HG_DK), lambda r: (0, r, 0, 0))]
    out_shape = [jax.ShapeDtypeStruct((S, y_width), BF16),
                 jax.ShapeDtypeStruct((S, HG_WIDTH), F32),
                 jax.ShapeDtypeStruct((HG_HEADS, S // C, HG_DK, HG_DK), F32)]
    scratch = [pltpu.VMEM((HG_HEADS, HG_DK, HG_DK), F32)] + [pltpu.VMEM((HG_HEADS, C, HG_DK), F32)] * 3
    args = [proj, proj, proj, proj, lbrows, ng]
    for w in gather:
        in_specs.append(ANY)
        out_specs.append(ANY)
        out_shape.append(jax.ShapeDtypeStruct((N_CHIPS,) + w.shape, w.dtype))
        scratch += _Gather.SCRATCH
        args.append(w)
    return pl.pallas_call(
        body,
        name="hgrn_fwd" if not gather else "hgrn_fwd_gather%d" % ng_,
        grid=(nb,),
        in_specs=in_specs,
        out_specs=out_specs,
        out_shape=out_shape,
        scratch_shapes=scratch,
        compiler_params=_params("arbitrary"),
    )(*args)


def hgrn_bwd(proj, o_raw, states, dy, lbrows, ng, *, rows, exchange=()):
    S = proj.shape[0]
    C, U = HG_CHUNK, HG_SUB
    cpb = rows // C
    nb = S // rows
    nx = len(exchange)

    def body(qp_ref, z_ref, v_ref, gp_ref, o_ref, st_ref, dy_ref, lb_ref, ng_ref, *rest):
        q_refs, rest = rest[:nx], rest[nx:]
        dqp_ref, dz_ref, dv_ref, dgp_ref, dlb_ref, dng_ref = rest[:6]
        r_refs, rest = rest[6:6 + nx], rest[6 + nx:]
        dst, b_s, k_s, v_s, dbs, dks, dvs = rest[:7]
        sems = rest[7:]
        comms = [_Exchange(q_refs[i], r_refs[i], *sems[3 * i:3 * i + 3]) for i in range(nx)]

        @pl.when(pl.program_id(0) == 0)
        def _():
            dst[...] = jnp.zeros_like(dst)
            dlb_ref[...] = jnp.zeros_like(dlb_ref)
            dng_ref[...] = jnp.zeros_like(dng_ref)
            for comm in comms:
                comm.start()

        a, c, oml = lb_ref[0:1, :], lb_ref[1:2, :], lb_ref[2:3, :]
        ngr = ng_ref[...]
        rows_i = lax.broadcasted_iota(jnp.int32, (C, HG_DK), 0)
        nmask = _row_masks()
        tri = _tri(C)
        H = _HEADS

        def chunk(cj, carry):
            ci = cpb - 1 - cj
            off = pl.multiple_of(ci * C, C)
            sl = pl.ds(off, C)
            qp = [qp_ref[sl, _lanes(h)] for h in H]
            v = [v_ref[sl, _lanes(h)] for h in H]
            st = [st_ref[h, ci] for h in H]
            gates = [_hg_gates(qp[h], z_ref[sl, _lanes(h)], _head(a, h), _head(c, h), _head(oml, h)) for h in H]
            q, sq, t, logf, snz, k = ([g[j] for g in gates] for j in range(6))
            b = [_dot_exact(tri, logf[h]) for h in H]
            for h in H:
                b_s[h] = b[h]
                k_s[h] = k[h]
                v_s[h] = v[h]
            do = []
            for h in H:
                o = o_ref[sl, _lanes(h)]
                dyv = dy_ref[sl, _lanes(h)]
                ngh = _head(ngr, h)
                rs = lax.rsqrt(jnp.mean(o * o, axis=1, keepdims=True) + EPS)
                sg = _sigmoid(gp_ref[sl, _lanes(h)])
                xh = o * rs
                dgp_ref[sl, _lanes(h)] = (dyv * xh * ngh * sg * (1.0 - sg)).astype(dgp_ref.dtype)
                don = dyv * sg
                dng_ref[0:1, _lanes(h)] += _colsum(don * xh)
                dxh = don * ngh
                do.append(rs * (dxh - xh * jnp.mean(dxh * xh, axis=1, keepdims=True)))
            eb = [jnp.exp(b[h]) for h in H]
            qb = [q[h] * eb[h] for h in H]
            dstv = [dst[h] for h in H]
            bl = [b[h][C - 1:C] for h in H]
            el = [jnp.exp(bl[h]) for h in H]
            ex = [jnp.exp(bl[h] - b[h]) for h in H]
            kd = [k[h] * ex[h] for h in H]
            dqb = [_dot(do[h], st[h]) for h in H]
            dst_acc = [_dot_tn(do[h], qb[h]) for h in H]
            dv0 = [_dot_nt(kd[h], dstv[h]) for h in H]
            dkd = [_dot(v[h], dstv[h]) for h in H]
            dq = [dqb[h] * eb[h] for h in H]
            for h in H:
                g2 = dkd[h] * kd[h]
                dbl = _colsum(dstv[h] * st[h]) * el[h] + _colsum(g2)
                dst[h] = dstv[h] * el[h] + dst_acc[h]
                dbs[h] = dqb[h] * qb[h] - g2
                dks[h] = dkd[h] * ex[h]
                dvs[h] = dv0[h]
                dbs[h, C - 1:C, :] += dbl
            dq_parts = [[] for _ in H]
            for i in range(C // U):
                lo = i * U
                b_i = [b[h][lo:lo + U] for h in H]
                q_i = [q[h][lo:lo + U] for h in H]
                do_i = [do[h][lo:lo + U] for h in H]
                dq_i = [dq[h][lo:lo + U] for h in H]
                db_i = [jnp.zeros((U, HG_DK), F32) for _ in H]
                if i > 0:
                    e1 = [jnp.exp(b_i[h] - b_i[h][0:1]) for h in H]
                    qe = [q_i[h] * e1[h] for h in H]
                    e2 = [jnp.where(rows_i < lo, jnp.exp(jnp.minimum(b_i[h][0:1] - b[h], 0.0)), 0.0) for h in H]
                    ke = [k[h] * e2[h] for h in H]
                    att = [_dot_nt(qe[h], ke[h]) for h in H]
                    datt = [_dot_nt(do_i[h], v[h]) for h in H]
                    dv_add = [_dot_tn(att[h], do_i[h]) for h in H]
                    dqe = [_dot(datt[h], ke[h]) for h in H]
                    dke = [_dot_tn(datt[h], qe[h]) for h in H]
                    for h in H:
                        dvs[h] += dv_add[h]
                        dq_i[h] = dq_i[h] + dqe[h] * e1[h]
                        g = dqe[h] * qe[h]
                        db_i[h] = db_i[h] + g
                        gk = dke[h] * ke[h]
                        dks[h] += dke[h] * e2[h]
                        dbs[h] -= gk
                        dbs[h, lo:lo + 1, :] += _colsum(gk) - _colsum(g)
                nf = U // 8
                dq8 = [[dq_i[h][8 * f:8 * f + 8] for f in range(nf)] for h in H]
                db8 = [[db_i[h][8 * f:8 * f + 8] for f in range(nf)] for h in H]
                key_v = [[] for _ in H]
                key_k = [[] for _ in H]
                key_b = [[] for _ in H]
                for s in range(U):
                    row = slice(lo + s, lo + s + 1)
                    for h in H:
                        bs = b_s[h, row, :]
                        ks = k_s[h, row, :]
                        vs = v_s[h, row, :]
                        tv = tk = tb = None
                        for f in range(s // 8, nf):
                            p8 = slice(8 * f, 8 * f + 8)
                            arg = b_i[h][p8] - bs
                            if s > 8 * f:
                                arg = arg + nmask[s - 8 * f]
                            dec = jnp.exp(arg)
                            qd = q_i[h][p8] * dec
                            y_ = qd * ks
                            w = _rowsum(y_)
                            dw = _rowsum(do_i[h][p8] * vs)
                            g = dw * y_
                            dq8[h][f] = dq8[h][f] + dw * dec * ks
                            db8[h][f] = db8[h][f] + g
                            cv, ck = w * do_i[h][p8], dw * qd
                            tv, tk, tb = (cv, ck, g) if tv is None else (tv + cv, tk + ck, tb + g)
                        key_v[h].append(tv)
                        key_k[h].append(tk)
                        key_b[h].append(tb)
                for h in H:
                    for f in range(nf):
                        r8 = slice(lo + 8 * f, lo + 8 * f + 8)
                        dvs[h, r8, :] += _colsum8(key_v[h][8 * f:8 * f + 8])
                        dks[h, r8, :] += _colsum8(key_k[h][8 * f:8 * f + 8])
                        dbs[h, r8, :] += db8[h][f] - _colsum8(key_b[h][8 * f:8 * f + 8])
                    dq_parts[h] += dq8[h]
            dlogf = [_dot_exact(tri, dbs[h], (((0,), (0,)), ((), ()))) for h in H]
            for h in H:
                dqh = jnp.concatenate(dq_parts[h], axis=0)
                dk = dks[h]
                ah, omlh = _head(a, h), _head(oml, h)
                pa = jnp.exp(ah - logf[h])
                pt = jnp.exp(t[h] - logf[h])
                dt = dlogf[h] * pt
                dlb_ref[0:1, _lanes(h)] += _colsum(dlogf[h] * pa)
                dlb_ref[1:2, _lanes(h)] += _colsum(dt)
                dlb_ref[2:3, _lanes(h)] += _colsum(dk * snz[h])
                dz = dt * snz[h] - dk * omlh * snz[h] * (1.0 - snz[h])
                dqp = dqh * (sq[h] + qp[h] * sq[h] * (1.0 - sq[h]))
                dqp_ref[sl, _lanes(h)] = dqp.astype(dqp_ref.dtype)
                dz_ref[sl, _lanes(h)] = dz.astype(dz_ref.dtype)
                dv_ref[sl, _lanes(h)] = dvs[h].astype(dv_ref.dtype)
            return carry

        lax.fori_loop(0, cpb, chunk, 0)

        for comm in comms:
            pl.when(pl.program_id(0) == nb - 1)(comm.finish)

    rev = lambda r: nb - 1 - r
    col = lambda kblk: pl.BlockSpec((rows, HG_WIDTH), lambda r: (rev(r), kblk))
    acc = _const_spec((8, HG_WIDTH))
    in_specs = [col(0), col(1), col(2), col(3), col(0),
                pl.BlockSpec((HG_HEADS, cpb, HG_DK, HG_DK), lambda r: (0, rev(r), 0, 0)),
                col(0), acc, _const_spec((1, HG_WIDTH))]
    out_specs = [col(0)] * 4 + [acc, acc]
    out_shape = [jax.ShapeDtypeStruct((S, HG_WIDTH), BF16)] * 4 + [jax.ShapeDtypeStruct((8, HG_WIDTH), F32)] * 2
    scratch = [pltpu.VMEM((HG_HEADS, HG_DK, HG_DK), F32)] + [pltpu.VMEM((HG_HEADS, C, HG_DK), F32)] * 6
    args = [proj, proj, proj, proj, o_raw, states, dy, lbrows, ng]
    for q in exchange:
        in_specs.append(ANY)
        out_specs.append(ANY)
        out_shape.append(jax.ShapeDtypeStruct(q.shape, q.dtype))
        scratch += _Exchange.SCRATCH
        args.append(q)
    return pl.pallas_call(
        body,
        name="hgrn_bwd" if not exchange else "hgrn_bwd_exchange%d" % nx,
        grid=(nb,),
        in_specs=in_specs,
        out_specs=out_specs,
        out_shape=out_shape,
        scratch_shapes=scratch,
        compiler_params=_params("arbitrary"),
    )(*args)


GROUP_LANES = AT_GROUP * WINDOW


def swa_mask():
    W = WINDOW
    kpos = lax.broadcasted_iota(jnp.int32, (2, 2 * W, GROUP_LANES), 1)
    qpos = (lax.broadcasted_iota(jnp.int32, (2, 2 * W, GROUP_LANES), 2) & (W - 1)) + W
    first = lax.broadcasted_iota(jnp.int32, (2, 2 * W, GROUP_LANES), 0) == 0
    rel = qpos - kpos
    valid = (rel >= 0) & (rel < W) & jnp.logical_not(first & (kpos < W))
    return jnp.where(valid, 0.0, MASK_VALUE).astype(F32)


def _group_lanes(xt, g):
    Dh = AT_HEAD_DIM
    return jnp.concatenate([xt[(g * AT_GROUP + j) * Dh:(g * AT_GROUP + j + 1) * Dh] for j in range(AT_GROUP)],
                           axis=1)


SWA_SCALE = AT_HEAD_DIM ** -0.5


def _swa_softmax_t(s, sink_row):
    m = jnp.maximum(jnp.max(s, axis=0, keepdims=True), sink_row)
    e = jnp.exp(s - m)
    es = jnp.exp(sink_row - m)
    inv = 1.0 / (_colsum(e) + es)
    return e * inv, es * inv


SWA_BLOCKS = 8
_BG = [(b, g) for b in range(SWA_BLOCKS) for g in range(AT_KV_HEADS)]


def _swa_specs(col_q):
    W = WINDOW
    rows = SWA_BLOCKS * W
    prev = lambda n: jnp.maximum(SWA_BLOCKS * n - 1, 0)
    return [pl.BlockSpec((rows, AT_WIDTH), lambda n: (n, col_q)),
            pl.BlockSpec((W, 128), lambda n: (prev(n), 20)),
            pl.BlockSpec((rows, 128), lambda n: (n, 20)),
            pl.BlockSpec((W, 128), lambda n: (prev(n), 21)),
            pl.BlockSpec((rows, 128), lambda n: (n, 21)),
            _const_spec((8 * AT_KV_HEADS, GROUP_LANES)),
            _const_spec((2, 2 * W, GROUP_LANES))]


def _swa_operands(n, q_ref, kp_ref, k_ref, vp_ref, v_ref, sk_ref, mask_ref):
    W, Dh = WINDOW, AT_HEAD_DIM
    k_all = jnp.concatenate([kp_ref[...], k_ref[...]], axis=0)
    v_all = jnp.concatenate([vp_ref[...], v_ref[...]], axis=0)
    kk = [k_all[b * W:(b + 2) * W] for b in range(SWA_BLOCKS)]
    vv = [v_all[b * W:(b + 2) * W] for b in range(SWA_BLOCKS)]
    masks = [mask_ref[jnp.minimum(n, 1)]] + [mask_ref[1]] * (SWA_BLOCKS - 1)
    qt = [(q_ref[b * W:(b + 1) * W, :] * SWA_SCALE).T for b in range(SWA_BLOCKS)]
    kg = {(b, g): kk[b][:, g * Dh:(g + 1) * Dh] for b, g in _BG}
    qg = {(b, g): _group_lanes(qt[b], g) for b, g in _BG}
    s = {bg: _dot(kg[bg], qg[bg]) + masks[bg[0]] for bg in _BG}
    sink = {(b, g): sk_ref[8 * g:8 * g + 1, :] for b, g in _BG}
    return kk, vv, kg, qg, s, sink


def swa_fwd(proj, sink_rows, mask, y):
    S = proj.shape[0]
    W, Dh = WINDOW, AT_HEAD_DIM
    rows = SWA_BLOCKS * W

    def body(q_ref, kp_ref, k_ref, vp_ref, v_ref, sk_ref, mask_ref, y_in, y_ref):
        del y_in
        _, vv, _, _, s, sink = _swa_operands(pl.program_id(0), q_ref, kp_ref, k_ref, vp_ref, v_ref,
                                             sk_ref, mask_ref)
        vt = [v.T for v in vv]
        p = {bg: _swa_softmax_t(s[bg], sink[bg])[0] for bg in _BG}
        ot = {(b, g): _dot(vt[b][g * Dh:(g + 1) * Dh], p[b, g]) for b, g in _BG}
        for b in range(SWA_BLOCKS):
            outs = [ot[b, g][:, j * W:(j + 1) * W] for g in range(AT_KV_HEADS) for j in range(AT_GROUP)]
            y_ref[b * W:(b + 1) * W, :] = jnp.concatenate(outs, axis=0).T.astype(y_ref.dtype)

    return pl.pallas_call(
        body,
        name="swa_fwd",
        grid=(S // rows,),
        in_specs=_swa_specs(4) + [pl.BlockSpec(memory_space=pl.ANY)],
        out_specs=pl.BlockSpec((rows, AT_WIDTH), lambda n: (n, 1)),
        out_shape=jax.ShapeDtypeStruct(y.shape, y.dtype),
        input_output_aliases={7: 0},
        compiler_params=_params("parallel"),
    )(proj, proj, proj, proj, proj, sink_rows, mask, y)


def swa_bwd(proj, sink_rows, mask, dy):
    S = proj.shape[0]
    W, Dh = WINDOW, AT_HEAD_DIM
    rows = SWA_BLOCKS * W
    nsteps = S // rows

    def body(q_ref, kp_ref, k_ref, vp_ref, v_ref, sk_ref, mask_ref, dy_ref,
             dq_ref, dko_ref, dkp_ref, dvo_ref, dvp_ref, dsk_ref):
        n = pl.program_id(0)

        @pl.when(n == 0)
        def _():
            dsk_ref[...] = jnp.zeros_like(dsk_ref)

        kk, vv, _, qg, s, sink = _swa_operands(n, q_ref, kp_ref, k_ref, vp_ref, v_ref, sk_ref, mask_ref)
        kt = [k.T for k in kk]
        dot_ = [dy_ref[b * W:(b + 1) * W, :].T for b in range(SWA_BLOCKS)]
        dog = {(b, g): _group_lanes(dot_[b], g) for b, g in _BG}
        dp = {(b, g): _dot(vv[b][:, g * Dh:(g + 1) * Dh], dog[b, g]) for b, g in _BG}
        pp = {bg: _swa_softmax_t(s[bg], sink[bg]) for bg in _BG}
        delta = {bg: _colsum(dp[bg] * pp[bg][0]) for bg in _BG}
        ds = {bg: pp[bg][0] * (dp[bg] - delta[bg]) for bg in _BG}
        dqt = {(b, g): _dot(kt[b][g * Dh:(g + 1) * Dh], ds[b, g]) * SWA_SCALE for b, g in _BG}
        dk = {bg: _dot_nt(ds[bg], qg[bg]) for bg in _BG}
        dv = {bg: _dot_nt(pp[bg][0], dog[bg]) for bg in _BG}
        for g in range(AT_KV_HEADS):
            tot = -(pp[0, g][1] * delta[0, g])
            for b in range(1, SWA_BLOCKS):
                tot = tot - pp[b, g][1] * delta[b, g]
            dsk_ref[8 * g:8 * g + 1, :] += tot
        for b in range(SWA_BLOCKS):
            r = slice(b * W, (b + 1) * W)
            dqs = [dqt[b, g][:, j * W:(j + 1) * W] for g in range(AT_KV_HEADS) for j in range(AT_GROUP)]
            dq_ref[r, :] = jnp.concatenate(dqs, axis=0).T.astype(dq_ref.dtype)
            dkb = jnp.concatenate([dk[b, g] for g in range(AT_KV_HEADS)], axis=1)
            dvb = jnp.concatenate([dv[b, g] for g in range(AT_KV_HEADS)], axis=1)
            dkp_ref[r, :] = dkb[:W]
            dko_ref[r, :] = dkb[W:]
            dvp_ref[r, :] = dvb[:W]
            dvo_ref[r, :] = dvb[W:]

        @pl.when(n == nsteps - 1)
        def _():
            for g in range(AT_KV_HEADS):
                for j in range(AT_GROUP):
                    tot = _rowsum(dsk_ref[8 * g:8 * g + 1, j * W:(j + 1) * W])
                    dsk_ref[8 * g + 1 + j:8 * g + 2 + j, :] = jnp.broadcast_to(tot, (1, GROUP_LANES))

    kv = pl.BlockSpec((rows, 128), lambda n: (n, 0))
    sk = _const_spec((8 * AT_KV_HEADS, GROUP_LANES))
    return pl.pallas_call(
        body,
        name="swa_bwd",
        grid=(nsteps,),
        in_specs=_swa_specs(4) + [pl.BlockSpec((rows, AT_WIDTH), lambda n: (n, 1))],
        out_specs=[pl.BlockSpec((rows, AT_WIDTH), lambda n: (n, 0)), kv, kv, kv, kv, sk],
        out_shape=[jax.ShapeDtypeStruct((S, AT_WIDTH), BF16)]
                  + [jax.ShapeDtypeStruct((S, 128), F32)] * 4
                  + [jax.ShapeDtypeStruct((8 * AT_KV_HEADS, GROUP_LANES), F32)],
        compiler_params=_params("arbitrary"),
    )(proj, proj, proj, proj, proj, sink_rows, mask, dy)


def assemble_dproj(hg_grads, dq_at, dko, dkp, dvo, dvp, *, rows):
    S = dq_at.shape[0]
    W = WINDOW
    nb = S // W
    bpr = rows // W

    def body(a0, a1, a2, a3, dq, ko, kp, kpn, vo, vp, vpn, out):
        r = pl.program_id(0)
        for i, a in enumerate((a0, a1, a2, a3)):
            out[:, i * HG_WIDTH:(i + 1) * HG_WIDTH] = a[...]
        base = 4 * HG_WIDTH
        out[:, base:base + AT_WIDTH] = dq[...]
        last = (r == pl.num_programs(0) - 1)
        for off, own, pv, pvn in ((base + AT_WIDTH, ko, kp, kpn), (base + AT_WIDTH + 128, vo, vp, vpn)):
            if bpr > 1:
                out[0:rows - W, off:off + 128] = (own[0:rows - W, :] + pv[W:rows, :]).astype(out.dtype)
            nxt = jnp.where(last, 0.0, pvn[...])
            out[rows - W:rows, off:off + 128] = (own[rows - W:rows, :] + nxt).astype(out.dtype)

    hg = pl.BlockSpec((rows, HG_WIDTH), lambda r: (r, 0))
    blk = pl.BlockSpec((rows, 128), lambda r: (r, 0))
    nxt = pl.BlockSpec((W, 128), lambda r: (jnp.minimum((r + 1) * bpr, nb - 1), 0))
    return pl.pallas_call(
        body,
        name="assemble_dproj",
        grid=(S // rows,),
        in_specs=[hg, hg, hg, hg, pl.BlockSpec((rows, AT_WIDTH), lambda r: (r, 0)),
                  blk, blk, nxt, blk, blk, nxt],
        out_specs=pl.BlockSpec((rows, IN_WIDTH), lambda r: (r, 0)),
        out_shape=jax.ShapeDtypeStruct((S, IN_WIDTH), BF16),
        compiler_params=_params("parallel"),
    )(*hg_grads, dq_at, dko, dkp, dkp, dvo, dvp, dvp)


ROW_TILE = 512
COL_TILE = 1408


def _col_tile(n):
    return n if n <= COL_TILE else COL_TILE


def _rms_scale(x):
    return lax.rsqrt(jnp.mean(x * x, axis=1, keepdims=True) + EPS)


def _rms_bwd(d, x, g):
    rs = _rms_scale(x)
    xh = x * rs
    dxh = d * g
    return rs * (dxh - xh * jnp.mean(dxh * xh, axis=1, keepdims=True)), _colsum(d * xh)


def mm(a, b, *, nt=False, out_dtype=F32, res=None, norm_g=None, rms_bwd=None, name):
    parts = a if isinstance(a, tuple) else (a,)
    M, K = parts[0].shape
    N = b.shape[0] if nt else b.shape[1]
    tall = K <= D_MODEL and rms_bwd is None and len(parts) == 1 and M % (2 * ROW_TILE) == 0
    tm = 2 * ROW_TILE if tall else min(ROW_TILE, M)
    tn = _col_tile(N)
    whole_rows = norm_g is not None or rms_bwd is not None
    assert M % tm == 0 and N % tn == 0 and (tn == N or not whole_rows)
    np_ = len(parts)

    def body(*refs):
        a_refs, b_refs, rest = refs[:np_], refs[np_:2 * np_], refs[2 * np_:]
        dot = _dot_nt if nt else _dot
        acc = dot(a_refs[0][...], b_refs[0][...])
        for ar, br in zip(a_refs[1:], b_refs[1:]):
            acc = acc + dot(ar[...], br[...])
        if rms_bwd is not None:
            h_ref, g_ref, dr_ref, dh_ref, dhb_ref, dg_ref = rest

            @pl.when(pl.program_id(1) == 0)
            def _():
                dg_ref[...] = jnp.zeros_like(dg_ref)

            dx, dgp = _rms_bwd(acc, h_ref[...], g_ref[...])
            dg_ref[0:1, :] += dgp
            dh = dr_ref[...] + dx
            dh_ref[...] = dh
            dhb_ref[...] = dh.astype(BF16)
            return
        rest = list(rest)
        if res is not None:
            acc = acc + rest.pop(0)[...]
        if norm_g is not None:
            g_ref = rest.pop(0)
            rest[1][...] = (acc * _rms_scale(acc) * g_ref[...]).astype(BF16)
        rest[0][...] = acc.astype(rest[0].dtype)

    row = pl.BlockSpec((tm, tn), lambda j, i: (i, j))
    in_specs = [pl.BlockSpec((tm, K), lambda j, i: (i, 0)) for _ in parts]
    for kb in range(np_):
        in_specs.append(pl.BlockSpec((tn, K), lambda j, i, kb=kb: (j, kb)) if nt
                        else pl.BlockSpec((K, tn), lambda j, i, kb=kb: (kb, j)))
    args = list(parts) + [b] * np_
    if rms_bwd is not None:
        h, g, dres = rms_bwd
        in_specs += [row, _const_spec((1, N)), row]
        args += [h, g, dres]
        out_specs = [row, row, _const_spec((8, N))]
        out_shape = [jax.ShapeDtypeStruct((M, N), F32), jax.ShapeDtypeStruct((M, N), BF16),
                     jax.ShapeDtypeStruct((8, N), F32)]
        sem = ("arbitrary", "arbitrary")
    else:
        if res is not None:
            in_specs.append(row)
            args.append(res)
        out_specs, out_shape = [row], [jax.ShapeDtypeStruct((M, N), out_dtype)]
        if norm_g is not None:
            in_specs.append(_const_spec((1, N)))
            args.append(norm_g)
            out_specs.append(row)
            out_shape.append(jax.ShapeDtypeStruct((M, N), BF16))
        sem = ("parallel", "parallel")
    out = pl.pallas_call(
        body,
        name=name,
        grid=(N // tn, M // tm),
        in_specs=in_specs,
        out_specs=out_specs,
        out_shape=out_shape,
        compiler_params=_params(*sem),
    )(*args)
    return out[0] if len(out) == 1 else out


def mm_tn(a, b, *, name):
    M, K = a.shape
    N = b.shape[1]
    tm = next((t for t in (4 * ROW_TILE, 2 * ROW_TILE) if M % t == 0), min(ROW_TILE, M))
    tk = _col_tile(K)
    tn = _col_tile(N)
    assert M % tm == 0 and K % tk == 0 and N % tn == 0
    steps = M // tm

    def body(a_ref, b_ref, o_ref, acc):
        @pl.when(pl.program_id(2) == 0)
        def _():
            acc[...] = jnp.zeros_like(acc)

        acc[...] += _dot_tn(a_ref[...], b_ref[...])

        @pl.when(pl.program_id(2) == steps - 1)
        def _():
            o_ref[...] = acc[...].astype(o_ref.dtype)

    return pl.pallas_call(
        body,
        name=name,
        grid=(K // tk, N // tn, steps),
        in_specs=[pl.BlockSpec((tm, tk), lambda k, j, i: (i, k)),
                  pl.BlockSpec((tm, tn), lambda k, j, i: (i, j))],
        out_specs=pl.BlockSpec((tk, tn), lambda k, j, i: (k, j)),
        out_shape=jax.ShapeDtypeStruct((K, N), BF16),
        scratch_shapes=[pltpu.VMEM((tk, tn), F32)],
        compiler_params=_params("parallel", "parallel", "arbitrary"),
    )(a, b)


def _row_spec(tm, width):
    return pl.BlockSpec((tm, width), lambda i: (i, 0))


def _const_spec(shape):
    return pl.BlockSpec(shape, lambda *_: (0,) * len(shape))


def rmsnorm_fwd(h, g, *, name):
    S, D = h.shape
    tm = min(ROW_TILE, S)

    def body(h_ref, g_ref, u_ref):
        x = h_ref[...]
        rs = lax.rsqrt(jnp.mean(x * x, axis=1, keepdims=True) + EPS)
        u_ref[...] = (x * rs * g_ref[...]).astype(u_ref.dtype)

    return pl.pallas_call(
        body, name=name, grid=(S // tm,),
        in_specs=[_row_spec(tm, D), _const_spec((1, D))],
        out_specs=_row_spec(tm, D),
        out_shape=jax.ShapeDtypeStruct((S, D), BF16),
        compiler_params=_params("parallel"),
    )(h, g)


HALO = 16


def _shift_down(x, edge8, s):
    sh = pltpu.roll(x, s, 0)
    er = pltpu.roll(edge8, s, 0)
    row8 = lax.broadcasted_iota(jnp.int32, er.shape, 0)
    top = jnp.where(row8 < s, er, sh[0:8])
    return jnp.concatenate([top, sh[8:]], axis=0)


def _shift_up(x, s):
    return pltpu.roll(x, x.shape[0] - s, 0)


def _conv_pre(a, prev8, w_ref, cb_ref):
    a1 = _shift_down(a, prev8, 1)
    a2 = _shift_down(a, prev8, 2)
    return w_ref[2:3, :] * a + w_ref[1:2, :] * a1 + w_ref[0:1, :] * a2 + cb_ref[...]


def convffn_fwd(hh, cw8, cb):
    S = hh.shape[0]
    tm = min(ROW_TILE, S)
    tn = _col_tile(D_FF)
    nj = D_FF // tn

    def body(a_ref, ap_ref, b_ref, w_ref, cb_ref, o_ref, ac_ref):
        prev8 = jnp.where(pl.program_id(1) == 0, 0.0, ap_ref[...].astype(F32)[HALO - 8:HALO])
        ac = _conv_pre(a_ref[...].astype(F32), prev8, w_ref, cb_ref)
        ac_ref[...] = ac.astype(ac_ref.dtype)
        o_ref[...] = (ac * _sigmoid(ac) * b_ref[...].astype(F32)).astype(o_ref.dtype)

    rh = tm // HALO
    return pl.pallas_call(
        body, name="convffn_fwd", grid=(nj, S // tm),
        in_specs=[pl.BlockSpec((tm, tn), lambda j, i: (i, j)),
                  pl.BlockSpec((HALO, tn), lambda j, i: (jnp.maximum(i * rh - 1, 0), j)),
                  pl.BlockSpec((tm, tn), lambda j, i: (i, j + nj)),
                  pl.BlockSpec((8, tn), lambda j, i: (0, j)),
                  pl.BlockSpec((1, tn), lambda j, i: (0, j))],
        out_specs=[pl.BlockSpec((tm, tn), lambda j, i: (i, j))] * 2,
        out_shape=[jax.ShapeDtypeStruct((S, D_FF), BF16)] * 2,
        compiler_params=_params("parallel", "parallel"),
    )(hh, hh, hh, cw8, cb)


def convffn_bwd(hh, conv, dact, cw8):
    S = hh.shape[0]
    tm = min(ROW_TILE, S)
    tn = _col_tile(D_FF)
    nj = D_FF // tn
    ni = S // tm

    def body(a_ref, b_ref, bn_ref, c_ref, cn_ref, d_ref, dn_ref, w_ref, o_a, o_b, dw_ref):
        i = pl.program_id(1)

        @pl.when(i == 0)
        def _():
            dw_ref[...] = jnp.zeros_like(dw_ref)

        up = lambda r: r[...].astype(F32)
        ext = lambda cur, nxt: jnp.concatenate([up(cur), up(nxt)[0:8]], axis=0)
        b = ext(b_ref, bn_ref)
        ac = ext(c_ref, cn_ref)
        d = jnp.concatenate([up(d_ref), jnp.where(i == ni - 1, 0.0, up(dn_ref)[0:8])], axis=0)
        sa = _sigmoid(ac)
        silu = ac * sa
        o_b[...] = (d[0:tm] * silu[0:tm]).astype(o_b.dtype)
        dac = d * b * (sa + silu * (1.0 - sa))
        dc0 = dac[0:tm]
        dc1 = _shift_up(dac, 1)[0:tm]
        dc2 = _shift_up(dac, 2)[0:tm]
        o_a[...] = (w_ref[2:3, :] * dc0 + w_ref[1:2, :] * dc1 + w_ref[0:1, :] * dc2).astype(o_a.dtype)
        a = up(a_ref)
        dw_ref[0:1, :] += _colsum(dc2 * a)
        dw_ref[1:2, :] += _colsum(dc1 * a)
        dw_ref[2:3, :] += _colsum(dc0 * a)
        dw_ref[3:4, :] += _colsum(dc0)

    rh = tm // HALO
    last = S // HALO - 1
    cur = lambda off: pl.BlockSpec((tm, tn), lambda j, i: (i, j + off))
    nxt = lambda off: pl.BlockSpec((HALO, tn), lambda j, i: (jnp.minimum((i + 1) * rh, last), j + off))
    return pl.pallas_call(
        body, name="convffn_bwd", grid=(nj, ni),
        in_specs=[cur(0), cur(nj), nxt(nj), cur(0), nxt(0), cur(0), nxt(0),
                  pl.BlockSpec((8, tn), lambda j, i: (0, j))],
        out_specs=[cur(0), cur(0), pl.BlockSpec((8, tn), lambda j, i: (0, j))],
        out_shape=[jax.ShapeDtypeStruct((S, D_FF), BF16), jax.ShapeDtypeStruct((S, D_FF), BF16),
                   jax.ShapeDtypeStruct((8, D_FF), F32)],
        compiler_params=_params("parallel", "arbitrary"),
    )(hh, hh, hh, conv, conv, dact, dact, cw8)


def ple_fwd(h, gpre, p, wpu, norm_g):
    S, D = h.shape
    tm = min(ROW_TILE, S)

    def body(h_ref, g_ref, p_ref, w_ref, ng_ref, o_ref, u_ref):
        out = h_ref[...] + _sigmoid(g_ref[...].astype(F32)) * _dot(p_ref[...], w_ref[...])
        o_ref[...] = out
        u_ref[...] = (out * _rms_scale(out) * ng_ref[...]).astype(BF16)

    return pl.pallas_call(
        body, name="ple_fwd", grid=(S // tm,),
        in_specs=[_row_spec(tm, D), _row_spec(tm, D), _row_spec(tm, PLE_DIM), _const_spec((PLE_DIM, D)),
                  _const_spec((1, D))],
        out_specs=[_row_spec(tm, D), _row_spec(tm, D)],
        out_shape=[jax.ShapeDtypeStruct((S, D), F32), jax.ShapeDtypeStruct((S, D), BF16)],
        compiler_params=_params("parallel"),
    )(h, gpre, p, wpu, norm_g)


def ple_bwd(dh, gpre, p, wpu):
    S, D = dh.shape
    tm = min(ROW_TILE, S)

    def body(d_ref, g_ref, p_ref, w_ref, dpu_ref, dg_ref):
        d = d_ref[...]
        gate = _sigmoid(g_ref[...].astype(F32))
        pu = _dot(p_ref[...], w_ref[...])
        dpu_ref[...] = (d * gate).astype(dpu_ref.dtype)
        dg_ref[...] = (d * pu * gate * (1.0 - gate)).astype(dg_ref.dtype)

    return pl.pallas_call(
        body, name="ple_bwd", grid=(S // tm,),
        in_specs=[_row_spec(tm, D), _row_spec(tm, D), _row_spec(tm, PLE_DIM), _const_spec((PLE_DIM, D))],
        out_specs=[_row_spec(tm, D), _row_spec(tm, D)],
        out_shape=[jax.ShapeDtypeStruct((S, D), BF16)] * 2,
        compiler_params=_params("parallel"),
    )(dh, gpre, p, wpu)


def loss_head(h, g, tgt):
    S, D = h.shape
    tm = min(ROW_TILE, S)

    def body(h_ref, g_ref, t_ref, dh_ref, dhb_ref, l_ref, dg_ref):
        @pl.when(pl.program_id(0) == 0)
        def _():
            l_ref[...] = jnp.zeros_like(l_ref)
            dg_ref[...] = jnp.zeros_like(dg_ref)

        x = h_ref[...]
        gr = g_ref[...]
        rs = lax.rsqrt(jnp.mean(x * x, axis=1, keepdims=True) + EPS)
        xh = x * rs
        err = xh * gr - t_ref[...]
        l_ref[0:1, 0:1] += 0.5 * _colsum(jnp.mean(err * err, axis=1, keepdims=True))
        dy = err * (1.0 / D)
        dg_ref[0:1, :] += _colsum(dy * xh)
        dxh = dy * gr
        dh = rs * (dxh - xh * jnp.mean(dxh * xh, axis=1, keepdims=True))
        dh_ref[...] = dh
        dhb_ref[...] = dh.astype(BF16)

    return pl.pallas_call(
        body, name="loss_head", grid=(S // tm,),
        in_specs=[_row_spec(tm, D), _const_spec((1, D)), _row_spec(tm, D)],
        out_specs=[_row_spec(tm, D), _row_spec(tm, D), _const_spec((8, 128)), _const_spec((8, D))],
        out_shape=[jax.ShapeDtypeStruct((S, D), F32), jax.ShapeDtypeStruct((S, D), BF16),
                   jax.ShapeDtypeStruct((8, 128), F32), jax.ShapeDtypeStruct((8, D), F32)],
        compiler_params=_params("arbitrary"),
    )(h, g, tgt)


def _lb_rows(l_ref):
    l = l_ref[...]
    e = jnp.exp(l - jnp.max(l, axis=0, keepdims=True))
    p = e / _colsum(e)
    lbs, run = [], None
    for i in range(DEPTH):
        run = p[i:i + 1] if i == 0 else run + p[i:i + 1]
        lbs.append(run - p[0:1])
    return p, lbs


def lb_fwd(lb_logits):
    def body(l_ref, o_ref):
        _, lbs = _lb_rows(l_ref)
        o_ref[...] = jnp.zeros_like(o_ref)
        for i, lb in enumerate(lbs):
            o_ref[8 * i:8 * i + 1, :] = jnp.log(jnp.maximum(lb, LB_FLOOR))
            o_ref[8 * i + 1:8 * i + 2, :] = jnp.log1p(-lb)
            o_ref[8 * i + 2:8 * i + 3, :] = 1.0 - lb
            o_ref[8 * i + 3:8 * i + 4, :] = lb

    return pl.pallas_call(
        body, name="lb_fwd",
        out_shape=jax.ShapeDtypeStruct((DEPTH * 8, HG_WIDTH), F32),
    )(lb_logits)


def lb_bwd(dlbrows, lb_logits):
    def body(d_ref, l_ref, o_ref):
        p, lbs = _lb_rows(l_ref)
        dlb = []
        for i, lb in enumerate(lbs):
            da = d_ref[8 * i:8 * i + 1, :]
            dc = d_ref[8 * i + 1:8 * i + 2, :]
            do = d_ref[8 * i + 2:8 * i + 3, :]
            dlb.append(jnp.where(lb > LB_FLOOR, da / jnp.maximum(lb, LB_FLOOR), 0.0) - dc / (1.0 - lb) - do)
        dp = [jnp.zeros_like(dlb[0])]
        for j in range(1, DEPTH):
            acc = dlb[j]
            for i in range(j + 1, DEPTH):
                acc = acc + dlb[i]
            dp.append(acc)
        dot_ = p[0:1] * dp[0]
        for j in range(1, DEPTH):
            dot_ = dot_ + p[j:j + 1] * dp[j]
        o_ref[...] = jnp.zeros_like(o_ref)
        for j in range(DEPTH):
            o_ref[j:j + 1, :] = p[j:j + 1] * (dp[j] - dot_)

    return pl.pallas_call(
        body, name="lb_bwd",
        out_shape=jax.ShapeDtypeStruct((8, HG_WIDTH), F32),
    )(dlbrows, lb_logits)


def adamw(w, g, m, v, *, name):
    R, C = w.shape
    tr = next((t for t in (512, 256, 128, 64, 32, 16, 8) if R % t == 0), R)

    def body(w_ref, g_ref, m_ref, v_ref, d_ref, m2_ref, v2_ref):
        gv = g_ref[...]
        m2 = ADAM_B1 * m_ref[...] + (1.0 - ADAM_B1) * gv
        v2 = ADAM_B2 * v_ref[...] + (1.0 - ADAM_B2) * (gv * gv)
        mh = m2 / (1.0 - ADAM_B1 ** ADAM_STEP)
        vh = v2 / (1.0 - ADAM_B2 ** ADAM_STEP)
        d_ref[...] = -ADAM_LR * (mh / (jnp.sqrt(vh) + ADAM_EPS) + ADAM_WD * w_ref[...])
        m2_ref[...] = m2
        v2_ref[...] = v2

    spec = pl.BlockSpec((tr, C), lambda i: (i, 0))
    return pl.pallas_call(
        body, name=name, grid=(R // tr,),
        in_specs=[spec] * 4, out_specs=[spec] * 3,
        out_shape=[jax.ShapeDtypeStruct((R, C), F32)] * 3,
        compiler_params=_params("parallel"),
    )(w, g, m, v)


def _slot_rows(R):
    return R if R <= 1024 else next((t for t in (848, 768, 704, 672, 512, 448, 352, 256, 128, 64, 16) if R % t == 0), R)


def add_pair(a, b):
    R, C = a.shape
    tr = _slot_rows(R)

    def body(a_ref, b_ref, o_ref):
        o_ref[...] = (a_ref[...].astype(F32) + b_ref[...].astype(F32)).astype(o_ref.dtype)

    spec = pl.BlockSpec((tr, C), lambda i: (i, 0))
    return pl.pallas_call(
        body, name="sum_pair", grid=(R // tr,),
        in_specs=[spec, spec], out_specs=spec,
        out_shape=jax.ShapeDtypeStruct((R, C), a.dtype),
        compiler_params=_params("parallel"),
    )(a, b)


def sum_slots(x, *, out_dtype, name):
    n, R, C = x.shape
    tr = _slot_rows(R)

    def body(x_ref, o_ref):
        acc = x_ref[0].astype(F32)
        for k in range(1, n):
            acc = acc + x_ref[k].astype(F32)
        o_ref[...] = acc.astype(o_ref.dtype)

    return pl.pallas_call(
        body, name=name, grid=(R // tr,),
        in_specs=[pl.BlockSpec((n, tr, C), lambda i: (0, i, 0))],
        out_specs=pl.BlockSpec((tr, C), lambda i: (i, 0)),
        out_shape=jax.ShapeDtypeStruct((R, C), out_dtype),
        compiler_params=_params("parallel"),
    )(x)


MESH = pl.DeviceIdType.MESH
ANY = pl.BlockSpec(memory_space=pl.ANY)


def _place():
    return lax.axis_index("x"), lax.axis_index("y"), lax.axis_index("c")


def _other_chips(x, y):
    return [(1 - x, y), (x, 1 - y), (1 - x, 1 - y)]


def small_allgather(buf):
    R, C = buf.shape

    def body(x_ref, out_ref, send_sems, recv_sems, local_sem):
        x, y, c = _place()
        me, sibling = (x, y, c), (x, y, 1 - c)
        chips = _other_chips(x, y)

        def slot(px, py, pc):
            return out_ref.at[4 * px + 2 * py + pc]

        def copy(k, block, to, src=None):
            return pltpu.make_async_remote_copy(
                src_ref=slot(*block) if src is None else src, dst_ref=slot(*block),
                send_sem=send_sems.at[k], recv_sem=recv_sems.at[k],
                device_id=to, device_id_type=MESH)

        mine = pltpu.make_async_copy(x_ref, slot(*me), local_sem)
        mine.start()
        first = [copy(0, me, sibling, src=x_ref)]
        first += [copy(1 + r, me, (*chip, c), src=x_ref) for r, chip in enumerate(chips)]
        for cp in first:
            cp.start()
        passed = [copy(4 + r, (*chip, c), sibling) for r, chip in enumerate(chips)]
        for r, chip in enumerate(chips):
            copy(1 + r, (*chip, c), me).wait_recv()
            passed[r].start()
        copy(0, sibling, me).wait_recv()
        for r, chip in enumerate(chips):
            copy(4 + r, (*chip, 1 - c), me).wait_recv()
        for cp in first + passed:
            cp.wait_send()
        mine.wait()

    return pl.pallas_call(
        body, name="small_allgather",
        out_shape=jax.ShapeDtypeStruct((8, R, C), buf.dtype),
        in_specs=[pl.BlockSpec(memory_space=pltpu.VMEM)],
        out_specs=pl.BlockSpec(memory_space=pltpu.VMEM),
        scratch_shapes=[pltpu.SemaphoreType.DMA((7,)), pltpu.SemaphoreType.DMA((7,)),
                        pltpu.SemaphoreType.DMA],
    )(buf)


def weights_allgather(wp):
    def body(w_ref, g_ref, send_sems, recv_sems, local_sem):
        gather = _Gather(w_ref, g_ref, send_sems, recv_sems, local_sem)
        gather.start()
        gather.finish()

    return pl.pallas_call(
        body, name="weights_allgather",
        out_shape=jax.ShapeDtypeStruct((4,) + wp.shape, wp.dtype),
        in_specs=[ANY], out_specs=ANY,
        scratch_shapes=_Gather.SCRATCH,
    )(wp)


class _Gather:
    SCRATCH = [pltpu.SemaphoreType.DMA((6,)), pltpu.SemaphoreType.DMA((6,)), pltpu.SemaphoreType.DMA]

    def __init__(self, w_ref, g_ref, send_sems, recv_sems, local_sem):
        self.w_ref, self.g_ref, self.local_sem = w_ref, g_ref, local_sem
        self.send_sems, self.recv_sems = send_sems, recv_sems
        self.x, self.y, self.c = _place()
        self.chips = _other_chips(self.x, self.y)
        half = w_ref.shape[0] // 2
        self.mine = pl.ds(pl.multiple_of(self.c * half, 16), half)
        self.theirs = pl.ds(pl.multiple_of((1 - self.c) * half, 16), half)

    def _copy(self, k, chip_block, rows, to, src=None):
        dst = self.g_ref.at[chip_block, rows]
        return pltpu.make_async_remote_copy(
            src_ref=dst if src is None else src, dst_ref=dst,
            send_sem=self.send_sems.at[k], recv_sem=self.recv_sems.at[k],
            device_id=to, device_id_type=MESH)

    def _own(self):
        return pltpu.make_async_copy(self.w_ref, self.g_ref.at[2 * self.x + self.y], self.local_sem)

    def _first(self):
        return [self._copy(r, 2 * self.x + self.y, self.mine, (*chip, self.c), src=self.w_ref.at[self.mine])
                for r, chip in enumerate(self.chips)]

    def start(self):
        self._own().start()
        for cp in self._first():
            cp.start()

    def finish(self):
        sibling = (self.x, self.y, 1 - self.c)
        passed = [self._copy(3 + r, 2 * chip[0] + chip[1], self.mine, sibling) for r, chip in enumerate(self.chips)]
        for r, chip in enumerate(self.chips):
            self._copy(r, 2 * chip[0] + chip[1], self.mine, (*chip, self.c)).wait_recv()
            passed[r].start()
        for r, chip in enumerate(self.chips):
            self._copy(3 + r, 2 * chip[0] + chip[1], self.theirs, sibling).wait_recv()
        for cp in self._first() + passed:
            cp.wait_send()
        self._own().wait()


def sibling_swap(v, *, name):
    def body(v_ref, got_ref, send_sem, recv_sem):
        x, y, c = _place()
        cp = pltpu.make_async_remote_copy(
            src_ref=v_ref, dst_ref=got_ref, send_sem=send_sem, recv_sem=recv_sem,
            device_id=(x, y, 1 - c), device_id_type=MESH)
        cp.start()
        cp.wait()

    return pl.pallas_call(
        body, name=name,
        out_shape=jax.ShapeDtypeStruct(v.shape, v.dtype),
        in_specs=[ANY], out_specs=ANY,
        scratch_shapes=[pltpu.SemaphoreType.DMA, pltpu.SemaphoreType.DMA],
    )(v)


def chip_exchange(q):
    def body(q_ref, r_ref, send_sems, recv_sems, local_sem):
        exchange = _Exchange(q_ref, r_ref, send_sems, recv_sems, local_sem)
        exchange.start()
        exchange.finish()

    return pl.pallas_call(
        body, name="chip_exchange",
        out_shape=jax.ShapeDtypeStruct(q.shape, q.dtype),
        in_specs=[ANY], out_specs=ANY,
        scratch_shapes=_Exchange.SCRATCH,
    )(q)


class _Exchange:
    SCRATCH = [pltpu.SemaphoreType.DMA((3,)), pltpu.SemaphoreType.DMA((3,)), pltpu.SemaphoreType.DMA]

    def __init__(self, q_ref, r_ref, send_sems, recv_sems, local_sem):
        self.q_ref, self.r_ref, self.local_sem = q_ref, r_ref, local_sem
        self.send_sems, self.recv_sems = send_sems, recv_sems
        self.x, self.y, self.c = _place()
        self.j = 2 * self.x + self.y
        self.chips = _other_chips(self.x, self.y)

    def _copy(self, r, src_block, dst_block, chip):
        return pltpu.make_async_remote_copy(
            src_ref=self.q_ref.at[src_block], dst_ref=self.r_ref.at[dst_block],
            send_sem=self.send_sems.at[r], recv_sem=self.recv_sems.at[r],
            device_id=(*chip, self.c), device_id_type=MESH)

    def _own(self):
        return pltpu.make_async_copy(self.q_ref.at[self.j], self.r_ref.at[self.j], self.local_sem)

    def _sends(self):
        return [self._copy(r, 2 * chip[0] + chip[1], self.j, chip) for r, chip in enumerate(self.chips)]

    def start(self):
        self._own().start()
        for cp in self._sends():
            cp.start()

    def finish(self):
        for r, chip in enumerate(self.chips):
            jr = 2 * chip[0] + chip[1]
            self._copy(r, jr, jr, chip).wait_recv()
        for cp in self._sends():
            cp.wait_send()
        self._own().wait()


N_CHIPS = 4
_PACK = (("w_in", 704), ("w_out", 256), ("w_up", 1408), ("w_down", 704), ("w_ple_gate", 256), ("w_ple_up", 64))
LAYER_ROWS = sum(r for _, r in _PACK)
PACK_ROWS = DEPTH * LAYER_ROWS


def _pack_shards(sh):
    parts = []
    for i in range(DEPTH):
        for name, rows in _PACK:
            parts.append(sh[name][i].reshape(rows, D_MODEL))
    return jnp.concatenate(parts, axis=0)


def _unpack_shards(slab):
    shapes = {"w_in": (D_MODEL, IN_WIDTH // N_CHIPS), "w_out": (D_MODEL // N_CHIPS, D_MODEL),
              "w_up": (D_MODEL, 2 * D_FF // N_CHIPS), "w_down": (D_FF // N_CHIPS, D_MODEL),
              "w_ple_gate": (D_MODEL // N_CHIPS, D_MODEL), "w_ple_up": (PLE_DIM, D_MODEL // N_CHIPS)}
    out = {name: [] for name, _ in _PACK}
    off = 0
    for i in range(DEPTH):
        for name, rows in _PACK:
            out[name].append(slab[off:off + rows].reshape(shapes[name]))
            off += rows
    return {k: jnp.stack(v) for k, v in out.items()}


_COL_SHARDED = ("w_in", "w_up", "w_ple_up")


def _full_from_chips(g, pack=_PACK):
    per_chip = [_unpack_shards_layer(g[k], pack) for k in range(N_CHIPS)]
    return {name: jnp.concatenate([pc[name] for pc in per_chip], axis=1 if name in _COL_SHARDED else 0)
            for name, _ in pack}


def _unpack_shards_layer(slab, pack):
    shapes = {"w_in": (D_MODEL, IN_WIDTH // N_CHIPS), "w_out": (D_MODEL // N_CHIPS, D_MODEL),
              "w_up": (D_MODEL, 2 * D_FF // N_CHIPS), "w_down": (D_FF // N_CHIPS, D_MODEL),
              "w_ple_gate": (D_MODEL // N_CHIPS, D_MODEL), "w_ple_up": (PLE_DIM, D_MODEL // N_CHIPS)}
    out = {}
    off = 0
    for name, rows in pack:
        out[name] = slab[off:off + rows].reshape(shapes[name])
        off += rows
    return out


def _split_to_chips(full, name):
    r, c = full.shape
    if name in _COL_SHARDED:
        full = full.reshape(r, N_CHIPS, c // N_CHIPS).transpose(1, 0, 2)
    return full.reshape(N_CHIPS, -1, D_MODEL)


_SMALL = (("loss", 128), ("g_final", 1024), ("g_mix", 4096), ("lb_logits", 2048), ("hg_norm_g", 2048),
          ("attn_sinks", 128), ("g_ffn", 4096), ("conv_w", 4 * 3 * D_FF), ("conv_b", 4 * D_FF), ("g_ple", 4096))
SMALL_ROWS = 496


def _pack_small(d):
    parts = []
    for name, n in _SMALL:
        v = d[name].reshape(-1).astype(F32)
        parts.append(jnp.pad(v, (0, n - v.shape[0])))
    flat = jnp.concatenate(parts)
    return jnp.pad(flat, (0, SMALL_ROWS * 128 - flat.shape[0])).reshape(SMALL_ROWS, 128)


def _unpack_small(buf, shapes):
    flat = buf.reshape(-1)
    out, off = {}, 0
    for name, n in _SMALL:
        size = 1
        for s in shapes[name]:
            size *= s
        out[name] = flat[off:off + size].reshape(shapes[name])
        off += n
    return out


WEIGHT_ORDER = ('g_mix', 'w_in', 'lb_logits', 'hg_norm_g', 'attn_sinks', 'w_out', 'g_ffn', 'w_up', 'conv_w',
                'conv_b', 'w_down', 'g_ple', 'w_ple_gate', 'w_ple_up', 'g_final')


def kernel(x, p, g_mix, w_in, lb_logits, hg_norm_g, attn_sinks, w_out, g_ffn, w_up, conv_w, conv_b, w_down, g_ple, w_ple_gate, w_ple_up, g_final, loss_target, m_g_mix, m_w_in, m_lb_logits, m_hg_norm_g, m_attn_sinks, m_w_out, m_g_ffn, m_w_up, m_conv_w, m_conv_b, m_w_down, m_g_ple, m_w_ple_gate, m_w_ple_up, m_g_final, v_g_mix, v_w_in, v_lb_logits, v_hg_norm_g, v_attn_sinks, v_w_out, v_g_ffn, v_w_up, v_conv_w, v_conv_b, v_w_down, v_g_ple, v_w_ple_gate, v_w_ple_up, v_g_final):
    W = dict(g_mix=g_mix, w_in=w_in, lb_logits=lb_logits, hg_norm_g=hg_norm_g, attn_sinks=attn_sinks,
             w_out=w_out, g_ffn=g_ffn, w_up=w_up, conv_w=conv_w, conv_b=conv_b, w_down=w_down, g_ple=g_ple,
             w_ple_gate=w_ple_gate, w_ple_up=w_ple_up, g_final=g_final)
    M = dict(g_mix=m_g_mix, w_in=m_w_in, lb_logits=m_lb_logits, hg_norm_g=m_hg_norm_g, attn_sinks=m_attn_sinks,
             w_out=m_w_out, g_ffn=m_g_ffn, w_up=m_w_up, conv_w=m_conv_w, conv_b=m_conv_b, w_down=m_w_down,
             g_ple=m_g_ple, w_ple_gate=m_w_ple_gate, w_ple_up=m_w_ple_up, g_final=m_g_final)
    V = dict(g_mix=v_g_mix, w_in=v_w_in, lb_logits=v_lb_logits, hg_norm_g=v_hg_norm_g, attn_sinks=v_attn_sinks,
             w_out=v_w_out, g_ffn=v_g_ffn, w_up=v_w_up, conv_w=v_conv_w, conv_b=v_conv_b, w_down=v_w_down,
             g_ple=v_g_ple, w_ple_gate=v_w_ple_gate, w_ple_up=v_w_ple_up, g_final=v_g_final)
    S = x.shape[1]
    hg_rows = min(ROW_TILE, S)
    xi, yi, ci = _place()
    chip = 2 * xi + yi

    slab = _pack_shards({n: W[n] for n, _ in _PACK}).astype(BF16).reshape(DEPTH, LAYER_ROWS, D_MODEL)
    first_rows = _PACK[0][1]
    gathered = (weights_allgather(slab[0, :first_rows]),)
    cw_shard = jnp.pad(conv_w.reshape(-1), (0, 72 * 128 - conv_w.size)).reshape(72, 128)
    cw_all = small_allgather(cw_shard)
    cw_full = jnp.concatenate(
        [cw_all[2 * k].reshape(-1)[:conv_w.size].reshape(conv_w.shape) for k in range(N_CHIPS)], axis=2)
    lbrows = lb_fwd(lb_logits)
    at_mask = swa_mask()

    h = x[0]
    saved = []
    for i in range(DEPTH):
        wf = _full_from_chips(gathered[0], _PACK[:1] if i == 0 else _PACK)
        lbr = lbrows[8 * i:8 * i + 8]
        ng = hg_norm_g[i][None]
        sinks_b = jnp.pad(jnp.repeat(attn_sinks[i].reshape(AT_KV_HEADS, 1, AT_GROUP), WINDOW, axis=2),
                          ((0, 0), (0, 7), (0, 0))).reshape(8 * AT_KV_HEADS, GROUP_LANES)
        cw8 = jnp.pad(cw_full[i], ((0, 5), (0, 0)))
        cb = conv_b[i][None]
        if i == 0:
            u = rmsnorm_fwd(h, g_mix[0][None], name="rmsnorm_fwd")
        proj = mm(u, wf["w_in"], name="mm_in")
        riders = ([slab[0, first_rows:]] if i == 0 else []) + ([slab[i + 1]] if i + 1 < DEPTH else [])
        y, o_raw, states, *got = hgrn_fwd(proj, lbr, ng, D_MODEL, rows=hg_rows, gather=riders)
        if i == 0:
            wf.update(_full_from_chips(got[0], _PACK[1:]))
        gathered = got[-1:]
        y = swa_fwd(proj, sinks_b, at_mask, y)
        h1, u2 = mm(y, wf["w_out"], res=h, norm_g=g_ffn[i][None], name="mm_out")
        hh = mm(u2, wf["w_up"], out_dtype=BF16, name="mm_up")
        act, conv = convffn_fwd(hh, cw8, cb)
        h2, u3 = mm(act, wf["w_down"], res=h1, norm_g=g_ple[i][None], name="mm_down")
        gpre = mm(u3, wf["w_ple_gate"], out_dtype=BF16, name="mm_gate")
        next_g = g_mix[i + 1] if i + 1 < DEPTH else g_final
        h3, u_next = ple_fwd(h2, gpre, p[i, 0], wf["w_ple_up"], next_g[None])
        saved.append(dict(wf=wf, lbr=lbr, ng=ng, sinks_b=sinks_b, cw8=cw8, cb=cb, h=h, u=u, proj=proj,
                          o_raw=o_raw, states=states, y=y, h1=h1, u2=u2, hh=hh, conv=conv, act=act, h2=h2, u3=u3,
                          gpre=gpre))
        h, u = h3, u_next

    dh, dhb, loss_acc, dg_final = loss_head(h, g_final[None], loss_target[0])

    gfull = {n: [None] * DEPTH for n, _ in _PACK}
    gsmall = {n: [None] * DEPTH for n in ("g_mix", "hg_norm_g", "attn_sinks", "g_ffn", "conv_w", "conv_b", "g_ple")}
    dlbrows = [None] * DEPTH
    def pair_sums(i, pack):
        pk = jnp.concatenate([_split_to_chips(gfull[name][i], name) for name, _ in pack], axis=1).astype(BF16)
        half = pk.shape[1] // 2
        pk = pk.reshape(N_CHIPS, 2, half, D_MODEL)
        p_mine = lax.dynamic_index_in_dim(pk, ci, axis=1, keepdims=False)
        p_other = lax.dynamic_index_in_dim(pk, 1 - ci, axis=1, keepdims=False)
        from_sib = sibling_swap(p_other, name="sibling_swap_partials")
        pair = add_pair(p_mine.reshape(-1, D_MODEL), from_sib.reshape(-1, D_MODEL))
        return pair.reshape(N_CHIPS, half, D_MODEL)

    from_chips = {}
    pending = []
    for i in reversed(range(DEPTH)):
        s = saved[i]
        wf = s["wf"]
        dpu, dgp = ple_bwd(dh, s["gpre"], p[i, 0], wf["w_ple_up"])
        gfull["w_ple_up"][i] = mm_tn(p[i, 0], dpu, name="mm_tn_pu")
        gfull["w_ple_gate"][i] = mm_tn(s["u3"], dgp, name="mm_tn_gate")
        dh2, dh2b, dg = mm(dgp, wf["w_ple_gate"], nt=True, rms_bwd=(s["h2"], g_ple[i][None], dh),
                           name="mm_nt_gate")
        gsmall["g_ple"][i] = dg[0]
        gfull["w_down"][i] = mm_tn(s["act"], dh2b, name="mm_tn_down")
        dact = mm(dh2b, wf["w_down"], nt=True, out_dtype=BF16, name="mm_nt_down")
        da, db, dcw = convffn_bwd(s["hh"], s["conv"], dact, s["cw8"])
        gsmall["conv_w"][i] = dcw[0:3]
        gsmall["conv_b"][i] = dcw[3]
        gfull["w_up"][i] = jnp.concatenate([mm_tn(s["u2"], da, name="mm_tn_up"),
                                            mm_tn(s["u2"], db, name="mm_tn_up")], axis=1)
        dh1, dh1b, dg = mm((da, db), wf["w_up"], nt=True, rms_bwd=(s["h1"], g_ffn[i][None], dh2),
                           name="mm_nt_up")
        gsmall["g_ffn"][i] = dg[0]
        gfull["w_out"][i] = mm_tn(s["y"], dh1b, name="mm_tn_out")
        dy = mm(dh1b, wf["w_out"], nt=True, name="mm_nt_out")
        dq_at, dko, dkp, dvo, dvp, dsk = swa_bwd(s["proj"], s["sinks_b"], at_mask, dy)
        gsmall["attn_sinks"][i] = dsk.reshape(AT_KV_HEADS, 8, GROUP_LANES)[:, 1:1 + AT_GROUP, 0].reshape(-1)
        hg_args = (s["proj"], s["o_raw"], s["states"], dy, s["lbr"], s["ng"])
        riders = pending + ([("0 rest", pair_sums(0, _PACK[1:]))] if i == 0 else [])
        hq, hz, hv, hgp, dlbr, dng, *got = hgrn_bwd(*hg_args, rows=hg_rows, exchange=[q for _, q in riders])
        from_chips.update({unit: g for (unit, _), g in zip(riders, got)})
        dlbrows[i] = dlbr
        gsmall["hg_norm_g"][i] = dng[0]
        dproj = assemble_dproj((hq, hz, hv, hgp), dq_at, dko, dkp, dvo, dvp, rows=hg_rows)
        gfull["w_in"][i] = mm_tn(s["u"], dproj, name="mm_tn_in")
        dh, dhb, dg = mm(dproj, wf["w_in"], nt=True, rms_bwd=(s["h"], g_mix[i][None], dh1), name="mm_nt_in")
        gsmall["g_mix"][i] = dg[0]
        pending = [("%d" % i, pair_sums(i, _PACK))] if i > 0 else [("0 w_in", pair_sums(0, _PACK[:1]))]
    from_chips["0 w_in"] = chip_exchange(pending[0][1])
    grad_x = dh[None]
    dlb_logits = lb_bwd(jnp.concatenate(dlbrows, axis=0), lb_logits)[0:DEPTH]

    units = ["0 w_in", "0 rest"] + ["%d" % i for i in range(1, DEPTH)]
    sums = [sum_slots(from_chips[unit], out_dtype=F32, name="sum_chips") for unit in units]
    mine_sum = jnp.concatenate(sums, axis=0)
    sib_sum = sibling_swap(mine_sum, name="sibling_swap_sums")
    lo = jnp.where(ci == 0, mine_sum, sib_sum)
    hi = jnp.where(ci == 0, sib_sum, mine_sum)
    parts, off = [], 0
    for part in sums:
        n = part.shape[0]
        parts += [lo[off:off + n], hi[off:off + n]]
        off += n
    gshard = _unpack_shards(jnp.concatenate(parts, axis=0))

    small = dict(loss=loss_acc[0, 0:1], g_final=dg_final[0], lb_logits=dlb_logits,
                 **{n: jnp.stack(v) for n, v in gsmall.items()})
    small_sum = sum_slots(small_allgather(_pack_small(small)), out_dtype=F32, name="sum_small")
    shapes = {n: W[n].shape for n in W}
    shapes["loss"] = (1,)
    shapes["conv_w"] = (DEPTH, 3, D_FF)
    gs = _unpack_small(small_sum, shapes)
    loss = gs["loss"][0]
    cshard = conv_w.shape[2]
    grads = dict(gshard)
    for n in ("g_mix", "lb_logits", "hg_norm_g", "attn_sinks", "g_ffn", "conv_b", "g_ple", "g_final"):
        grads[n] = gs[n]
    grads["conv_w"] = lax.dynamic_slice_in_dim(gs["conv_w"], chip * cshard, cshard, axis=2)

    delta, new_m, new_v = {}, {}, {}
    small_names = ("g_final", "g_mix", "lb_logits", "hg_norm_g", "attn_sinks", "g_ffn", "conv_b", "g_ple")
    sshapes = {n: W[n].shape for n in small_names}

    def pack_s(d):
        z = dict(d)
        z["loss"] = jnp.zeros((1,), F32)
        z["conv_w"] = jnp.zeros((1,), F32)
        return _pack_small(z)

    sd, sm, sv = adamw(pack_s(W), pack_s(grads), pack_s(M), pack_s(V), name="adamw_small")
    for out, buf in ((delta, sd), (new_m, sm), (new_v, sv)):
        un = _unpack_small(buf, {**sshapes, "loss": (1,), "conv_w": (1,)})
        for n in small_names:
            out[n] = un[n]
    for n in ("w_in", "w_out", "w_up", "w_down", "w_ple_gate", "w_ple_up", "conv_w"):
        shp = W[n].shape
        two_d = (-1, shp[-1])
        d_, m_, v_ = adamw(W[n].reshape(two_d), grads[n].reshape(two_d), M[n].reshape(two_d),
                           V[n].reshape(two_d), name="adamw_" + n)
        delta[n], new_m[n], new_v[n] = d_.reshape(shp), m_.reshape(shp), v_.reshape(shp)

    return (loss, grad_x, *[grads[n] for n in WEIGHT_ORDER], *[delta[n] for n in WEIGHT_ORDER],
            *[new_m[n] for n in WEIGHT_ORDER], *[new_v[n] for n in WEIGHT_ORDER])
```

```python
import jax
import jax.numpy as jnp
from jax import lax
from jax.experimental import pallas as pl
from jax.experimental.pallas import tpu as pltpu

F32 = jnp.float32
BF16 = jnp.bfloat16

D_MODEL = 1024
DEPTH = 4
PLE_DIM = 256
HG_WIDTH = 512
HG_HEADS = 4
HG_DK = 128
HG_CHUNK = 64
HG_SUB = 16
AT_WIDTH = 512
AT_HEAD_DIM = 64
AT_KV_HEADS = 2
AT_GROUP = 4
WINDOW = 128
D_FF = 2816
IN_WIDTH = 2816
EPS = 1e-6
MASK_VALUE = -1e30
LB_FLOOR = 1e-30

ADAM_LR = 0.001
ADAM_B1 = 0.9
ADAM_B2 = 0.999
ADAM_EPS = 1e-08
ADAM_WD = 0.01
ADAM_STEP = 10

VMEM_LIMIT = 48 * 1024 * 1024


def _params(*sem):
    return pltpu.CompilerParams(dimension_semantics=sem, vmem_limit_bytes=VMEM_LIMIT)


def _dot(a, b, dims=(((1,), (0,)), ((), ()))):
    return lax.dot_general(a.astype(BF16), b.astype(BF16), dims, preferred_element_type=F32)


def _dot_nt(a, b):
    return _dot(a, b, (((1,), (1,)), ((), ())))


def _dot_tn(a, b):
    return _dot(a, b, (((0,), (0,)), ((), ())))


def _dot_exact(sel, x, dims=(((1,), (0,)), ((), ()))):
    hi = x.astype(BF16)
    r1 = x - hi.astype(F32)
    mid = r1.astype(BF16)
    lo = (r1 - mid.astype(F32)).astype(BF16)
    s = sel.astype(BF16)
    one = lambda p: lax.dot_general(s, p, dims, preferred_element_type=F32)
    return one(hi) + one(mid) + one(lo)


def _sigmoid(x):
    return 0.5 * jnp.tanh(0.5 * x) + 0.5


def _logsig(x):
    return jnp.minimum(x, 0.0) - jnp.log(1.0 + jnp.exp(-jnp.abs(x)))


def _colsum(x):
    return jnp.sum(x, axis=0, keepdims=True)


def _rowsum(x):
    return jnp.sum(x, axis=1, keepdims=True)


def _colsum8(xs):
    row = lax.broadcasted_iota(jnp.int32, xs[0].shape, 0)

    def merge(a, b, keep_a, step):
        return jnp.where(keep_a, a + pltpu.roll(a, 8 - step, 0), b + pltpu.roll(b, step, 0))

    c = [merge(xs[j], xs[j + 4], row < 4, 4) for j in range(4)]
    d = [merge(c[j], c[j + 2], (row & 3) < 2, 2) for j in range(2)]
    return merge(d[0], d[1], (row & 1) == 0, 1)


def _tri(n):
    r = lax.broadcasted_iota(jnp.int32, (n, n), 0)
    c = lax.broadcasted_iota(jnp.int32, (n, n), 1)
    return (r >= c).astype(F32)


def _hg_gates(qp, z, a, c, oml):
    sq = _sigmoid(qp)
    q = qp * sq
    t = c + _logsig(z)
    mx = jnp.maximum(a, t)
    logf = mx + jnp.log(1.0 + jnp.exp(-jnp.abs(a - t)))
    snz = _sigmoid(-z)
    k = oml * snz
    return q, sq, t, logf, snz, k


_HEADS = range(HG_HEADS)


def _lanes(h):
    return slice(h * HG_DK, (h + 1) * HG_DK)


def _head(x, h):
    return x[:, _lanes(h)]


def _row_masks():
    row8 = lax.broadcasted_iota(jnp.int32, (8, HG_DK), 0)
    return [None] + [jnp.where(row8 >= j, 0.0, MASK_VALUE) for j in range(1, 8)]


def _hg_chunk_fwd(q, k, v, logf, st, b_s, k_s, v_s):
    C, U = HG_CHUNK, HG_SUB
    tri = _tri(C)
    b = [_dot_exact(tri, logf[h]) for h in _HEADS]
    for h in _HEADS:
        b_s[h] = b[h]
        k_s[h] = k[h]
        v_s[h] = v[h]
    o = [_dot_nt(q[h] * jnp.exp(b[h]), st[h]) for h in _HEADS]
    bl = [b[h][C - 1:C] for h in _HEADS]
    upd = [_dot_tn(v[h], k[h] * jnp.exp(bl[h] - b[h])) for h in _HEADS]
    rows = lax.broadcasted_iota(jnp.int32, (C, HG_DK), 0)
    nmask = _row_masks()
    outs = [[] for _ in _HEADS]
    for i in range(C // U):
        lo = i * U
        b_i = [b[h][lo:lo + U] for h in _HEADS]
        q_i = [q[h][lo:lo + U] for h in _HEADS]
        o_i = [o[h][lo:lo + U] for h in _HEADS]
        if i > 0:
            qe = [q_i[h] * jnp.exp(b_i[h] - b_i[h][0:1]) for h in _HEADS]
            ke = [jnp.where(rows < lo, k[h] * jnp.exp(jnp.minimum(b_i[h][0:1] - b[h], 0.0)), 0.0) for h in _HEADS]
            att = [_dot_nt(qe[h], ke[h]) for h in _HEADS]
            off = [_dot(att[h], v[h]) for h in _HEADS]
            o_i = [o_i[h] + off[h] for h in _HEADS]
        pieces = [[o_i[h][8 * f:8 * f + 8] for f in range(U // 8)] for h in _HEADS]
        for s in range(U):
            for f in range(s // 8, U // 8):
                for h in _HEADS:
                    bs = b_s[h, lo + s:lo + s + 1, :]
                    ks = k_s[h, lo + s:lo + s + 1, :]
                    vs = v_s[h, lo + s:lo + s + 1, :]
                    arg = b_i[h][8 * f:8 * f + 8] - bs
                    if s > 8 * f:
                        arg = arg + nmask[s - 8 * f]
                    w = _rowsum(q_i[h][8 * f:8 * f + 8] * jnp.exp(arg) * ks)
                    pieces[h][f] = pieces[h][f] + w * vs
        for h in _HEADS:
            outs[h] += pieces[h]
    o = [jnp.concatenate(outs[h], axis=0) for h in _HEADS]
    st_new = [st[h] * jnp.exp(bl[h]) + upd[h] for h in _HEADS]
    return o, st_new, b


def _hg_post(o, gp, ng):
    rs = lax.rsqrt(jnp.mean(o * o, axis=1, keepdims=True) + EPS)
    sg = _sigmoid(gp)
    return o * rs * ng * sg, rs, sg


def hgrn_fwd(proj, lbrows, ng, y_width, *, rows, gather=()):
    S = proj.shape[0]
    C = HG_CHUNK
    cpb = rows // C
    nb = S // rows
    ng_ = len(gather)

    def body(qp_ref, z_ref, v_ref, gp_ref, lb_ref, ng_ref, *rest):
        w_refs, rest = rest[:ng_], rest[ng_:]
        y_ref, o_ref, st_ref = rest[:3]
        g_refs, rest = rest[3:3 + ng_], rest[3 + ng_:]
        st, b_s, k_s, v_s = rest[:4]
        sems = rest[4:]
        comms = [_Gather(w_refs[i], g_refs[i], *sems[3 * i:3 * i + 3]) for i in range(ng_)]

        @pl.when(pl.program_id(0) == 0)
        def _():
            st[...] = jnp.zeros_like(st)
            for comm in comms:
                comm.start()

        a, c, oml = lb_ref[0:1, :], lb_ref[1:2, :], lb_ref[2:3, :]
        ngr = ng_ref[...]

        def chunk(ci, carry):
            off = pl.multiple_of(ci * C, C)
            sl = pl.ds(off, C)
            for h in _HEADS:
                st_ref[h, ci] = st[h]
            gates = [_hg_gates(qp_ref[sl, _lanes(h)], z_ref[sl, _lanes(h)],
                               _head(a, h), _head(c, h), _head(oml, h)) for h in _HEADS]
            q = [g[0] for g in gates]
            logf = [g[3] for g in gates]
            k = [g[5] for g in gates]
            v = [v_ref[sl, _lanes(h)] for h in _HEADS]
            o, st_new, _ = _hg_chunk_fwd(q, k, v, logf, [st[h] for h in _HEADS], b_s, k_s, v_s)
            for h in _HEADS:
                y, _, _ = _hg_post(o[h], gp_ref[sl, _lanes(h)], _head(ngr, h))
                y_ref[sl, _lanes(h)] = y.astype(y_ref.dtype)
                o_ref[sl, _lanes(h)] = o[h]
                st[h] = st_new[h]
            return carry

        lax.fori_loop(0, cpb, chunk, 0)

        for comm in comms:
            pl.when(pl.program_id(0) == nb - 1)(comm.finish)

    col = lambda kblk: pl.BlockSpec((rows, HG_WIDTH), lambda r: (r, kblk))
    in_specs = [col(0), col(1), col(2), col(3), _const_spec((8, HG_WIDTH)), _const_spec((1, HG_WIDTH))]
    out_specs = [col(0), col(0), pl.BlockSpec((HG_HEADS, cpb, HG_DK, HG_DK), lambda r: (0, r, 0, 0))]
    out_shape = [jax.ShapeDtypeStruct((S, y_width), BF16),
                 jax.ShapeDtypeStruct((S, HG_WIDTH), F32),
                 jax.ShapeDtypeStruct((HG_HEADS, S // C, HG_DK, HG_DK), F32)]
    scratch = [pltpu.VMEM((HG_HEADS, HG_DK, HG_DK), F32)] + [pltpu.VMEM((HG_HEADS, C, HG_DK), F32)] * 3
    args = [proj, proj, proj, proj, lbrows, ng]
    for w in gather:
        in_specs.append(ANY)
        out_specs.append(ANY)
        out_shape.append(jax.ShapeDtypeStruct((N_CHIPS,) + w.shape, w.dtype))
        scratch += _Gather.SCRATCH
        args.append(w)
    return pl.pallas_call(
        body,
        name="hgrn_fwd" if not gather else "hgrn_fwd_gather%d" % ng_,
        grid=(nb,),
        in_specs=in_specs,
        out_specs=out_specs,
        out_shape=out_shape,
        scratch_shapes=scratch,
        compiler_params=_params("arbitrary"),
    )(*args)


def hgrn_bwd(proj, o_raw, states, dy, lbrows, ng, *, rows, exchange=()):
    S = proj.shape[0]
    C, U = HG_CHUNK, HG_SUB
    cpb = rows // C
    nb = S // rows
    nx = len(exchange)

    def body(qp_ref, z_ref, v_ref, gp_ref, o_ref, st_ref, dy_ref, lb_ref, ng_ref, *rest):
        q_refs, rest = rest[:nx], rest[nx:]
        dqp_ref, dz_ref, dv_ref, dgp_ref, dlb_ref, dng_ref = rest[:6]
        r_refs, rest = rest[6:6 + nx], rest[6 + nx:]
        dst, b_s, k_s, v_s, dbs, dks, dvs = rest[:7]
        sems = rest[7:]
        comms = [_Exchange(q_refs[i], r_refs[i], *sems[3 * i:3 * i + 3]) for i in range(nx)]

        @pl.when(pl.program_id(0) == 0)
        def _():
            dst[...] = jnp.zeros_like(dst)
            dlb_ref[...] = jnp.zeros_like(dlb_ref)
            dng_ref[...] = jnp.zeros_like(dng_ref)
            for comm in comms:
                comm.start()

        a, c, oml = lb_ref[0:1, :], lb_ref[1:2, :], lb_ref[2:3, :]
        ngr = ng_ref[...]
        rows_i = lax.broadcasted_iota(jnp.int32, (C, HG_DK), 0)
        nmask = _row_masks()
        tri = _tri(C)
        H = _HEADS

        def chunk(cj, carry):
            ci = cpb - 1 - cj
            off = pl.multiple_of(ci * C, C)
            sl = pl.ds(off, C)
            qp = [qp_ref[sl, _lanes(h)] for h in H]
            v = [v_ref[sl, _lanes(h)] for h in H]
            st = [st_ref[h, ci] for h in H]
            gates = [_hg_gates(qp[h], z_ref[sl, _lanes(h)], _head(a, h), _head(c, h), _head(oml, h)) for h in H]
            q, sq, t, logf, snz, k = ([g[j] for g in gates] for j in range(6))
            b = [_dot_exact(tri, logf[h]) for h in H]
            for h in H:
                b_s[h] = b[h]
                k_s[h] = k[h]
                v_s[h] = v[h]
            do = []
            for h in H:
                o = o_ref[sl, _lanes(h)]
                dyv = dy_ref[sl, _lanes(h)]
                ngh = _head(ngr, h)
                rs = lax.rsqrt(jnp.mean(o * o, axis=1, keepdims=True) + EPS)
                sg = _sigmoid(gp_ref[sl, _lanes(h)])
                xh = o * rs
                dgp_ref[sl, _lanes(h)] = (dyv * xh * ngh * sg * (1.0 - sg)).astype(dgp_ref.dtype)
                don = dyv * sg
                dng_ref[0:1, _lanes(h)] += _colsum(don * xh)
                dxh = don * ngh
                do.append(rs * (dxh - xh * jnp.mean(dxh * xh, axis=1, keepdims=True)))
            eb = [jnp.exp(b[h]) for h in H]
            qb = [q[h] * eb[h] for h in H]
            dstv = [dst[h] for h in H]
            bl = [b[h][C - 1:C] for h in H]
            el = [jnp.exp(bl[h]) for h in H]
            ex = [jnp.exp(bl[h] - b[h]) for h in H]
            kd = [k[h] * ex[h] for h in H]
            dqb = [_dot(do[h], st[h]) for h in H]
            dst_acc = [_dot_tn(do[h], qb[h]) for h in H]
            dv0 = [_dot_nt(kd[h], dstv[h]) for h in H]
            dkd = [_dot(v[h], dstv[h]) for h in H]
            dq = [dqb[h] * eb[h] for h in H]
            for h in H:
                g2 = dkd[h] * kd[h]
                dbl = _colsum(dstv[h] * st[h]) * el[h] + _colsum(g2)
                dst[h] = dstv[h] * el[h] + dst_acc[h]
                dbs[h] = dqb[h] * qb[h] - g2
                dks[h] = dkd[h] * ex[h]
                dvs[h] = dv0[h]
                dbs[h, C - 1:C, :] += dbl
            dq_parts = [[] for _ in H]
            for i in range(C // U):
                lo = i * U
                b_i = [b[h][lo:lo + U] for h in H]
                q_i = [q[h][lo:lo + U] for h in H]
                do_i = [do[h][lo:lo + U] for h in H]
                dq_i = [dq[h][lo:lo + U] for h in H]
                db_i = [jnp.zeros((U, HG_DK), F32) for _ in H]
                if i > 0:
                    e1 = [jnp.exp(b_i[h] - b_i[h][0:1]) for h in H]
                    qe = [q_i[h] * e1[h] for h in H]
                    e2 = [jnp.where(rows_i < lo, jnp.exp(jnp.minimum(b_i[h][0:1] - b[h], 0.0)), 0.0) for h in H]
                    ke = [k[h] * e2[h] for h in H]
                    att = [_dot_nt(qe[h], ke[h]) for h in H]
                    datt = [_dot_nt(do_i[h], v[h]) for h in H]
                    dv_add = [_dot_tn(att[h], do_i[h]) for h in H]
                    dqe = [_dot(datt[h], ke[h]) for h in H]
                    dke = [_dot_tn(datt[h], qe[h]) for h in H]
                    for h in H:
                        dvs[h] += dv_add[h]
                        dq_i[h] = dq_i[h] + dqe[h] * e1[h]
                        g = dqe[h] * qe[h]
                        db_i[h] = db_i[h] + g
                        gk = dke[h] * ke[h]
                        dks[h] += dke[h] * e2[h]
                        dbs[h] -= gk
                        dbs[h, lo:lo + 1, :] += _colsum(gk) - _colsum(g)
                nf = U // 8
                dq8 = [[dq_i[h][8 * f:8 * f + 8] for f in range(nf)] for h in H]
                db8 = [[db_i[h][8 * f:8 * f + 8] for f in range(nf)] for h in H]
                key_v = [[] for _ in H]
                key_k = [[] for _ in H]
                key_b = [[] for _ in H]
                for s in range(U):
                    row = slice(lo + s, lo + s + 1)
                    for h in H:
                        bs = b_s[h, row, :]
                        ks = k_s[h, row, :]
                        vs = v_s[h, row, :]
                        tv = tk = tb = None
                        for f in range(s // 8, nf):
                            p8 = slice(8 * f, 8 * f + 8)
                            arg = b_i[h][p8] - bs
                            if s > 8 * f:
                                arg = arg + nmask[s - 8 * f]
                            dec = jnp.exp(arg)
                            qd = q_i[h][p8] * dec
                            y_ = qd * ks
                            w = _rowsum(y_)
                            dw = _rowsum(do_i[h][p8] * vs)
                            g = dw * y_
                            dq8[h][f] = dq8[h][f] + dw * dec * ks
                            db8[h][f] = db8[h][f] + g
                            cv, ck = w * do_i[h][p8], dw * qd
                            tv, tk, tb = (cv, ck, g) if tv is None else (tv + cv, tk + ck, tb + g)
                        key_v[h].append(tv)
                        key_k[h].append(tk)
                        key_b[h].append(tb)
                for h in H:
                    for f in range(nf):
                        r8 = slice(lo + 8 * f, lo + 8 * f + 8)
                        dvs[h, r8, :] += _colsum8(key_v[h][8 * f:8 * f + 8])
                        dks[h, r8, :] += _colsum8(key_k[h][8 * f:8 * f + 8])
                        dbs[h, r8, :] += db8[h][f] - _colsum8(key_b[h][8 * f:8 * f + 8])
                    dq_parts[h] += dq8[h]
            dlogf = [_dot_exact(tri, dbs[h], (((0,), (0,)), ((), ()))) for h in H]
            for h in H:
                dqh = jnp.concatenate(dq_parts[h], axis=0)
                dk = dks[h]
                ah, omlh = _head(a, h), _head(oml, h)
                pa = jnp.exp(ah - logf[h])
                pt = jnp.exp(t[h] - logf[h])
                dt = dlogf[h] * pt
                dlb_ref[0:1, _lanes(h)] += _colsum(dlogf[h] * pa)
                dlb_ref[1:2, _lanes(h)] += _colsum(dt)
                dlb_ref[2:3, _lanes(h)] += _colsum(dk * snz[h])
                dz = dt * snz[h] - dk * omlh * snz[h] * (1.0 - snz[h])
                dqp = dqh * (sq[h] + qp[h] * sq[h] * (1.0 - sq[h]))
                dqp_ref[sl, _lanes(h)] = dqp.astype(dqp_ref.dtype)
                dz_ref[sl, _lanes(h)] = dz.astype(dz_ref.dtype)
                dv_ref[sl, _lanes(h)] = dvs[h].astype(dv_ref.dtype)
            return carry

        lax.fori_loop(0, cpb, chunk, 0)

        for comm in comms:
            pl.when(pl.program_id(0) == nb - 1)(comm.finish)

    rev = lambda r: nb - 1 - r
    col = lambda kblk: pl.BlockSpec((rows, HG_WIDTH), lambda r: (rev(r), kblk))
    acc = _const_spec((8, HG_WIDTH))
    in_specs = [col(0), col(1), col(2), col(3), col(0),
                pl.BlockSpec((HG_HEADS, cpb, HG_DK, HG_DK), lambda r: (0, rev(r), 0, 0)),
                col(0), acc, _const_spec((1, HG_WIDTH))]
    out_specs = [col(0)] * 4 + [acc, acc]
    out_shape = [jax.ShapeDtypeStruct((S, HG_WIDTH), BF16)] * 4 + [jax.ShapeDtypeStruct((8, HG_WIDTH), F32)] * 2
    scratch = [pltpu.VMEM((HG_HEADS, HG_DK, HG_DK), F32)] + [pltpu.VMEM((HG_HEADS, C, HG_DK), F32)] * 6
    args = [proj, proj, proj, proj, o_raw, states, dy, lbrows, ng]
    for q in exchange:
        in_specs.append(ANY)
        out_specs.append(ANY)
        out_shape.append(jax.ShapeDtypeStruct(q.shape, q.dtype))
        scratch += _Exchange.SCRATCH
        args.append(q)
    return pl.pallas_call(
        body,
        name="hgrn_bwd" if not exchange else "hgrn_bwd_exchange%d" % nx,
        grid=(nb,),
        in_specs=in_specs,
        out_specs=out_specs,
        out_shape=out_shape,
        scratch_shapes=scratch,
        compiler_params=_params("arbitrary"),
    )(*args)


GROUP_LANES = AT_GROUP * WINDOW


def swa_mask():
    W = WINDOW
    kpos = lax.broadcasted_iota(jnp.int32, (2, 2 * W, GROUP_LANES), 1)
    qpos = (lax.broadcasted_iota(jnp.int32, (2, 2 * W, GROUP_LANES), 2) & (W - 1)) + W
    first = lax.broadcasted_iota(jnp.int32, (2, 2 * W, GROUP_LANES), 0) == 0
    rel = qpos - kpos
    valid = (rel >= 0) & (rel < W) & jnp.logical_not(first & (kpos < W))
    return jnp.where(valid, 0.0, MASK_VALUE).astype(F32)


def _group_lanes(xt, g):
    Dh = AT_HEAD_DIM
    return jnp.concatenate([xt[(g * AT_GROUP + j) * Dh:(g * AT_GROUP + j + 1) * Dh] for j in range(AT_GROUP)],
                           axis=1)


SWA_SCALE = AT_HEAD_DIM ** -0.5


def _swa_softmax_t(s, sink_row):
    m = jnp.maximum(jnp.max(s, axis=0, keepdims=True), sink_row)
    e = jnp.exp(s - m)
    es = jnp.exp(sink_row - m)
    inv = 1.0 / (_colsum(e) + es)
    return e * inv, es * inv


SWA_BLOCKS = 8
_BG = [(b, g) for b in range(SWA_BLOCKS) for g in range(AT_KV_HEADS)]


def _swa_specs(col_q):
    W = WINDOW
    rows = SWA_BLOCKS * W
    prev = lambda n: jnp.maximum(SWA_BLOCKS * n - 1, 0)
    return [pl.BlockSpec((rows, AT_WIDTH), lambda n: (n, col_q)),
            pl.BlockSpec((W, 128), lambda n: (prev(n), 20)),
            pl.BlockSpec((rows, 128), lambda n: (n, 20)),
            pl.BlockSpec((W, 128), lambda n: (prev(n), 21)),
            pl.BlockSpec((rows, 128), lambda n: (n, 21)),
            _const_spec((8 * AT_KV_HEADS, GROUP_LANES)),
            _const_spec((2, 2 * W, GROUP_LANES))]


def _swa_operands(n, q_ref, kp_ref, k_ref, vp_ref, v_ref, sk_ref, mask_ref):
    W, Dh = WINDOW, AT_HEAD_DIM
    k_all = jnp.concatenate([kp_ref[...], k_ref[...]], axis=0)
    v_all = jnp.concatenate([vp_ref[...], v_ref[...]], axis=0)
    kk = [k_all[b * W:(b + 2) * W] for b in range(SWA_BLOCKS)]
    vv = [v_all[b * W:(b + 2) * W] for b in range(SWA_BLOCKS)]
    masks = [mask_ref[jnp.minimum(n, 1)]] + [mask_ref[1]] * (SWA_BLOCKS - 1)
    qt = [(q_ref[b * W:(b + 1) * W, :] * SWA_SCALE).T for b in range(SWA_BLOCKS)]
    kg = {(b, g): kk[b][:, g * Dh:(g + 1) * Dh] for b, g in _BG}
    qg = {(b, g): _group_lanes(qt[b], g) for b, g in _BG}
    s = {bg: _dot(kg[bg], qg[bg]) + masks[bg[0]] for bg in _BG}
    sink = {(b, g): sk_ref[8 * g:8 * g + 1, :] for b, g in _BG}
    return kk, vv, kg, qg, s, sink


def swa_fwd(proj, sink_rows, mask, y):
    S = proj.shape[0]
    W, Dh = WINDOW, AT_HEAD_DIM
    rows = SWA_BLOCKS * W

    def body(q_ref, kp_ref, k_ref, vp_ref, v_ref, sk_ref, mask_ref, y_in, y_ref):
        del y_in
        _, vv, _, _, s, sink = _swa_operands(pl.program_id(0), q_ref, kp_ref, k_ref, vp_ref, v_ref,
                                             sk_ref, mask_ref)
        vt = [v.T for v in vv]
        p = {bg: _swa_softmax_t(s[bg], sink[bg])[0] for bg in _BG}
        ot = {(b, g): _dot(vt[b][g * Dh:(g + 1) * Dh], p[b, g]) for b, g in _BG}
        for b in range(SWA_BLOCKS):
            outs = [ot[b, g][:, j * W:(j + 1) * W] for g in range(AT_KV_HEADS) for j in range(AT_GROUP)]
            y_ref[b * W:(b + 1) * W, :] = jnp.concatenate(outs, axis=0).T.astype(y_ref.dtype)

    return pl.pallas_call(
        body,
        name="swa_fwd",
        grid=(S // rows,),
        in_specs=_swa_specs(4) + [pl.BlockSpec(memory_space=pl.ANY)],
        out_specs=pl.BlockSpec((rows, AT_WIDTH), lambda n: (n, 1)),
        out_shape=jax.ShapeDtypeStruct(y.shape, y.dtype),
        input_output_aliases={7: 0},
        compiler_params=_params("parallel"),
    )(proj, proj, proj, proj, proj, sink_rows, mask, y)


def swa_bwd(proj, sink_rows, mask, dy):
    S = proj.shape[0]
    W, Dh = WINDOW, AT_HEAD_DIM
    rows = SWA_BLOCKS * W
    nsteps = S // rows

    def body(q_ref, kp_ref, k_ref, vp_ref, v_ref, sk_ref, mask_ref, dy_ref,
             dq_ref, dko_ref, dkp_ref, dvo_ref, dvp_ref, dsk_ref):
        n = pl.program_id(0)

        @pl.when(n == 0)
        def _():
            dsk_ref[...] = jnp.zeros_like(dsk_ref)

        kk, vv, _, qg, s, sink = _swa_operands(n, q_ref, kp_ref, k_ref, vp_ref, v_ref, sk_ref, mask_ref)
        kt = [k.T for k in kk]
        dot_ = [dy_ref[b * W:(b + 1) * W, :].T for b in range(SWA_BLOCKS)]
        dog = {(b, g): _group_lanes(dot_[b], g) for b, g in _BG}
        dp = {(b, g): _dot(vv[b][:, g * Dh:(g + 1) * Dh], dog[b, g]) for b, g in _BG}
        pp = {bg: _swa_softmax_t(s[bg], sink[bg]) for bg in _BG}
        delta = {bg: _colsum(dp[bg] * pp[bg][0]) for bg in _BG}
        ds = {bg: pp[bg][0] * (dp[bg] - delta[bg]) for bg in _BG}
        dqt = {(b, g): _dot(kt[b][g * Dh:(g + 1) * Dh], ds[b, g]) * SWA_SCALE for b, g in _BG}
        dk = {bg: _dot_nt(ds[bg], qg[bg]) for bg in _BG}
        dv = {bg: _dot_nt(pp[bg][0], dog[bg]) for bg in _BG}
        for g in range(AT_KV_HEADS):
            tot = -(pp[0, g][1] * delta[0, g])
            for b in range(1, SWA_BLOCKS):
                tot = tot - pp[b, g][1] * delta[b, g]
            dsk_ref[8 * g:8 * g + 1, :] += tot
        for b in range(SWA_BLOCKS):
            r = slice(b * W, (b + 1) * W)
            dqs = [dqt[b, g][:, j * W:(j + 1) * W] for g in range(AT_KV_HEADS) for j in range(AT_GROUP)]
            dq_ref[r, :] = jnp.concatenate(dqs, axis=0).T.astype(dq_ref.dtype)
            dkb = jnp.concatenate([dk[b, g] for g in range(AT_KV_HEADS)], axis=1)
            dvb = jnp.concatenate([dv[b, g] for g in range(AT_KV_HEADS)], axis=1)
            dkp_ref[r, :] = dkb[:W]
            dko_ref[r, :] = dkb[W:]
            dvp_ref[r, :] = dvb[:W]
            dvo_ref[r, :] = dvb[W:]

        @pl.when(n == nsteps - 1)
        def _():
            for g in range(AT_KV_HEADS):
                for j in range(AT_GROUP):
                    tot = _rowsum(dsk_ref[8 * g:8 * g + 1, j * W:(j + 1) * W])
                    dsk_ref[8 * g + 1 + j:8 * g + 2 + j, :] = jnp.broadcast_to(tot, (1, GROUP_LANES))

    kv = pl.BlockSpec((rows, 128), lambda n: (n, 0))
    sk = _const_spec((8 * AT_KV_HEADS, GROUP_LANES))
    return pl.pallas_call(
        body,
        name="swa_bwd",
        grid=(nsteps,),
        in_specs=_swa_specs(4) + [pl.BlockSpec((rows, AT_WIDTH), lambda n: (n, 1))],
        out_specs=[pl.BlockSpec((rows, AT_WIDTH), lambda n: (n, 0)), kv, kv, kv, kv, sk],
        out_shape=[jax.ShapeDtypeStruct((S, AT_WIDTH), BF16)]
                  + [jax.ShapeDtypeStruct((S, 128), F32)] * 4
                  + [jax.ShapeDtypeStruct((8 * AT_KV_HEADS, GROUP_LANES), F32)],
        compiler_params=_params("arbitrary"),
    )(proj, proj, proj, proj, proj, sink_rows, mask, dy)


def assemble_dproj(hg_grads, dq_at, dko, dkp, dvo, dvp, *, rows):
    S = dq_at.shape[0]
    W = WINDOW
    nb = S // W
    bpr = rows // W

    def body(a0, a1, a2, a3, dq, ko, kp, kpn, vo, vp, vpn, out):
        r = pl.program_id(0)
        for i, a in enumerate((a0, a1, a2, a3)):
            out[:, i * HG_WIDTH:(i + 1) * HG_WIDTH] = a[...]
        base = 4 * HG_WIDTH
        out[:, base:base + AT_WIDTH] = dq[...]
        last = (r == pl.num_programs(0) - 1)
        for off, own, pv, pvn in ((base + AT_WIDTH, ko, kp, kpn), (base + AT_WIDTH + 128, vo, vp, vpn)):
            if bpr > 1:
                out[0:rows - W, off:off + 128] = (own[0:rows - W, :] + pv[W:rows, :]).astype(out.dtype)
            nxt = jnp.where(last, 0.0, pvn[...])
            out[rows - W:rows, off:off + 128] = (own[rows - W:rows, :] + nxt).astype(out.dtype)

    hg = pl.BlockSpec((rows, HG_WIDTH), lambda r: (r, 0))
    blk = pl.BlockSpec((rows, 128), lambda r: (r, 0))
    nxt = pl.BlockSpec((W, 128), lambda r: (jnp.minimum((r + 1) * bpr, nb - 1), 0))
    return pl.pallas_call(
        body,
        name="assemble_dproj",
        grid=(S // rows,),
        in_specs=[hg, hg, hg, hg, pl.BlockSpec((rows, AT_WIDTH), lambda r: (r, 0)),
                  blk, blk, nxt, blk, blk, nxt],
        out_specs=pl.BlockSpec((rows, IN_WIDTH), lambda r: (r, 0)),
        out_shape=jax.ShapeDtypeStruct((S, IN_WIDTH), BF16),
        compiler_params=_params("parallel"),
    )(*hg_grads, dq_at, dko, dkp, dkp, dvo, dvp, dvp)


ROW_TILE = 512
COL_TILE = 1408


def _col_tile(n):
    return n if n <= COL_TILE else COL_TILE


def _rms_scale(x):
    return lax.rsqrt(jnp.mean(x * x, axis=1, keepdims=True) + EPS)


def _rms_bwd(d, x, g):
    rs = _rms_scale(x)
    xh = x * rs
    dxh = d * g
    return rs * (dxh - xh * jnp.mean(dxh * xh, axis=1, keepdims=True)), _colsum(d * xh)


def mm(a, b, *, nt=False, out_dtype=F32, res=None, norm_g=None, rms_bwd=None, name):
    parts = a if isinstance(a, tuple) else (a,)
    M, K = parts[0].shape
    N = b.shape[0] if nt else b.shape[1]
    tall = K <= D_MODEL and len(parts) == 1 and M % (2 * ROW_TILE) == 0
    tm = 2 * ROW_TILE if tall else min(ROW_TILE, M)
    tn = _col_tile(N)
    whole_rows = norm_g is not None or rms_bwd is not None
    assert M % tm == 0 and N % tn == 0 and (tn == N or not whole_rows)
    np_ = len(parts)

    def body(*refs):
        a_refs, b_refs, rest = refs[:np_], refs[np_:2 * np_], refs[2 * np_:]
        dot = _dot_nt if nt else _dot
        acc = dot(a_refs[0][...], b_refs[0][...])
        for ar, br in zip(a_refs[1:], b_refs[1:]):
            acc = acc + dot(ar[...], br[...])
        if rms_bwd is not None:
            h_ref, g_ref, dr_ref, dh_ref, dhb_ref, dg_ref = rest

            @pl.when(pl.program_id(1) == 0)
            def _():
                dg_ref[...] = jnp.zeros_like(dg_ref)

            dx, dgp = _rms_bwd(acc, h_ref[...], g_ref[...])
            dg_ref[0:1, :] += dgp
            dh = dr_ref[...] + dx
            dh_ref[...] = dh
            dhb_ref[...] = dh.astype(BF16)
            return
        rest = list(rest)
        if res is not None:
            acc = acc + rest.pop(0)[...]
        if norm_g is not None:
            g_ref = rest.pop(0)
            rest[1][...] = (acc * _rms_scale(acc) * g_ref[...]).astype(BF16)
        rest[0][...] = acc.astype(rest[0].dtype)

    row = pl.BlockSpec((tm, tn), lambda j, i: (i, j))
    in_specs = [pl.BlockSpec((tm, K), lambda j, i: (i, 0)) for _ in parts]
    for kb in range(np_):
        in_specs.append(pl.BlockSpec((tn, K), lambda j, i, kb=kb: (j, kb)) if nt
                        else pl.BlockSpec((K, tn), lambda j, i, kb=kb: (kb, j)))
    args = list(parts) + [b] * np_
    if rms_bwd is not None:
        h, g, dres = rms_bwd
        in_specs += [row, _const_spec((1, N)), row]
        args += [h, g, dres]
        out_specs = [row, row, _const_spec((8, N))]
        out_shape = [jax.ShapeDtypeStruct((M, N), F32), jax.ShapeDtypeStruct((M, N), BF16),
                     jax.ShapeDtypeStruct((8, N), F32)]
        sem = ("arbitrary", "arbitrary")
    else:
        if res is not None:
            in_specs.append(row)
            args.append(res)
        out_specs, out_shape = [row], [jax.ShapeDtypeStruct((M, N), out_dtype)]
        if norm_g is not None:
            in_specs.append(_const_spec((1, N)))
            args.append(norm_g)
            out_specs.append(row)
            out_shape.append(jax.ShapeDtypeStruct((M, N), BF16))
        sem = ("parallel", "parallel")
    out = pl.pallas_call(
        body,
        name=name,
        grid=(N // tn, M // tm),
        in_specs=in_specs,
        out_specs=out_specs,
        out_shape=out_shape,
        compiler_params=_params(*sem),
    )(*args)
    return out[0] if len(out) == 1 else out


def mm_tn(a, b, *, name):
    M, K = a.shape
    N = b.shape[1]
    tm = next((t for t in (4 * ROW_TILE, 2 * ROW_TILE) if M % t == 0), min(ROW_TILE, M))
    tk = _col_tile(K)
    tn = _col_tile(N)
    assert M % tm == 0 and K % tk == 0 and N % tn == 0
    steps = M // tm

    def body(a_ref, b_ref, o_ref, acc):
        @pl.when(pl.program_id(2) == 0)
        def _():
            acc[...] = jnp.zeros_like(acc)

        acc[...] += _dot_tn(a_ref[...], b_ref[...])

        @pl.when(pl.program_id(2) == steps - 1)
        def _():
            o_ref[...] = acc[...].astype(o_ref.dtype)

    return pl.pallas_call(
        body,
        name=name,
        grid=(K // tk, N // tn, steps),
        in_specs=[pl.BlockSpec((tm, tk), lambda k, j, i: (i, k)),
                  pl.BlockSpec((tm, tn), lambda k, j, i: (i, j))],
        out_specs=pl.BlockSpec((tk, tn), lambda k, j, i: (k, j)),
        out_shape=jax.ShapeDtypeStruct((K, N), BF16),
        scratch_shapes=[pltpu.VMEM((tk, tn), F32)],
        compiler_params=_params("parallel", "parallel", "arbitrary"),
    )(a, b)


def _row_spec(tm, width):
    return pl.BlockSpec((tm, width), lambda i: (i, 0))


def _const_spec(shape):
    return pl.BlockSpec(shape, lambda *_: (0,) * len(shape))


def rmsnorm_fwd(h, g, *, name):
    S, D = h.shape
    tm = min(ROW_TILE, S)

    def body(h_ref, g_ref, u_ref):
        x = h_ref[...]
        rs = lax.rsqrt(jnp.mean(x * x, axis=1, keepdims=True) + EPS)
        u_ref[...] = (x * rs * g_ref[...]).astype(u_ref.dtype)

    return pl.pallas_call(
        body, name=name, grid=(S // tm,),
        in_specs=[_row_spec(tm, D), _const_spec((1, D))],
        out_specs=_row_spec(tm, D),
        out_shape=jax.ShapeDtypeStruct((S, D), BF16),
        compiler_params=_params("parallel"),
    )(h, g)


HALO = 16


def _shift_down(x, edge8, s):
    sh = pltpu.roll(x, s, 0)
    er = pltpu.roll(edge8, s, 0)
    row8 = lax.broadcasted_iota(jnp.int32, er.shape, 0)
    top = jnp.where(row8 < s, er, sh[0:8])
    return jnp.concatenate([top, sh[8:]], axis=0)


def _shift_up(x, s):
    return pltpu.roll(x, x.shape[0] - s, 0)


def _conv_pre(a, prev8, w_ref, cb_ref):
    a1 = _shift_down(a, prev8, 1)
    a2 = _shift_down(a, prev8, 2)
    return w_ref[2:3, :] * a + w_ref[1:2, :] * a1 + w_ref[0:1, :] * a2 + cb_ref[...]


def convffn_fwd(hh, cw8, cb):
    S = hh.shape[0]
    tm = min(ROW_TILE, S)
    tn = _col_tile(D_FF)
    nj = D_FF // tn

    def body(a_ref, ap_ref, b_ref, w_ref, cb_ref, o_ref, ac_ref):
        prev8 = jnp.where(pl.program_id(1) == 0, 0.0, ap_ref[...].astype(F32)[HALO - 8:HALO])
        ac = _conv_pre(a_ref[...].astype(F32), prev8, w_ref, cb_ref)
        ac_ref[...] = ac.astype(ac_ref.dtype)
        o_ref[...] = (ac * _sigmoid(ac) * b_ref[...].astype(F32)).astype(o_ref.dtype)

    rh = tm // HALO
    return pl.pallas_call(
        body, name="convffn_fwd", grid=(nj, S // tm),
        in_specs=[pl.BlockSpec((tm, tn), lambda j, i: (i, j)),
                  pl.BlockSpec((HALO, tn), lambda j, i: (jnp.maximum(i * rh - 1, 0), j)),
                  pl.BlockSpec((tm, tn), lambda j, i: (i, j + nj)),
                  pl.BlockSpec((8, tn), lambda j, i: (0, j)),
                  pl.BlockSpec((1, tn), lambda j, i: (0, j))],
        out_specs=[pl.BlockSpec((tm, tn), lambda j, i: (i, j))] * 2,
        out_shape=[jax.ShapeDtypeStruct((S, D_FF), BF16)] * 2,
        compiler_params=_params("parallel", "parallel"),
    )(hh, hh, hh, cw8, cb)


def convffn_bwd(hh, conv, dact, cw8):
    S = hh.shape[0]
    tm = min(ROW_TILE, S)
    tn = _col_tile(D_FF)
    nj = D_FF // tn
    ni = S // tm

    def body(a_ref, b_ref, bn_ref, c_ref, cn_ref, d_ref, dn_ref, w_ref, o_a, o_b, dw_ref):
        i = pl.program_id(1)

        @pl.when(i == 0)
        def _():
            dw_ref[...] = jnp.zeros_like(dw_ref)

        up = lambda r: r[...].astype(F32)
        ext = lambda cur, nxt: jnp.concatenate([up(cur), up(nxt)[0:8]], axis=0)
        b = ext(b_ref, bn_ref)
        ac = ext(c_ref, cn_ref)
        d = jnp.concatenate([up(d_ref), jnp.where(i == ni - 1, 0.0, up(dn_ref)[0:8])], axis=0)
        sa = _sigmoid(ac)
        silu = ac * sa
        o_b[...] = (d[0:tm] * silu[0:tm]).astype(o_b.dtype)
        dac = d * b * (sa + silu * (1.0 - sa))
        dc0 = dac[0:tm]
        dc1 = _shift_up(dac, 1)[0:tm]
        dc2 = _shift_up(dac, 2)[0:tm]
        o_a[...] = (w_ref[2:3, :] * dc0 + w_ref[1:2, :] * dc1 + w_ref[0:1, :] * dc2).astype(o_a.dtype)
        a = up(a_ref)
        dw_ref[0:1, :] += _colsum(dc2 * a)
        dw_ref[1:2, :] += _colsum(dc1 * a)
        dw_ref[2:3, :] += _colsum(dc0 * a)
        dw_ref[3:4, :] += _colsum(dc0)

    rh = tm // HALO
    last = S // HALO - 1
    cur = lambda off: pl.BlockSpec((tm, tn), lambda j, i: (i, j + off))
    nxt = lambda off: pl.BlockSpec((HALO, tn), lambda j, i: (jnp.minimum((i + 1) * rh, last), j + off))
    return pl.pallas_call(
        body, name="convffn_bwd", grid=(nj, ni),
        in_specs=[cur(0), cur(nj), nxt(nj), cur(0), nxt(0), cur(0), nxt(0),
                  pl.BlockSpec((8, tn), lambda j, i: (0, j))],
        out_specs=[cur(0), cur(0), pl.BlockSpec((8, tn), lambda j, i: (0, j))],
        out_shape=[jax.ShapeDtypeStruct((S, D_FF), BF16), jax.ShapeDtypeStruct((S, D_FF), BF16),
                   jax.ShapeDtypeStruct((8, D_FF), F32)],
        compiler_params=_params("parallel", "arbitrary"),
    )(hh, hh, hh, conv, conv, dact, dact, cw8)


def ple_fwd(h, gpre, p, wpu, norm_g):
    S, D = h.shape
    tm = min(ROW_TILE, S)

    def body(h_ref, g_ref, p_ref, w_ref, ng_ref, o_ref, u_ref):
        out = h_ref[...] + _sigmoid(g_ref[...].astype(F32)) * _dot(p_ref[...], w_ref[...])
        o_ref[...] = out
        u_ref[...] = (out * _rms_scale(out) * ng_ref[...]).astype(BF16)

    return pl.pallas_call(
        body, name="ple_fwd", grid=(S // tm,),
        in_specs=[_row_spec(tm, D), _row_spec(tm, D), _row_spec(tm, PLE_DIM), _const_spec((PLE_DIM, D)),
                  _const_spec((1, D))],
        out_specs=[_row_spec(tm, D), _row_spec(tm, D)],
        out_shape=[jax.ShapeDtypeStruct((S, D), F32), jax.ShapeDtypeStruct((S, D), BF16)],
        compiler_params=_params("parallel"),
    )(h, gpre, p, wpu, norm_g)


def ple_bwd(dh, gpre, p, wpu):
    S, D = dh.shape
    tm = min(ROW_TILE, S)

    def body(d_ref, g_ref, p_ref, w_ref, dpu_ref, dg_ref):
        d = d_ref[...]
        gate = _sigmoid(g_ref[...].astype(F32))
        pu = _dot(p_ref[...], w_ref[...])
        dpu_ref[...] = (d * gate).astype(dpu_ref.dtype)
        dg_ref[...] = (d * pu * gate * (1.0 - gate)).astype(dg_ref.dtype)

    return pl.pallas_call(
        body, name="ple_bwd", grid=(S // tm,),
        in_specs=[_row_spec(tm, D), _row_spec(tm, D), _row_spec(tm, PLE_DIM), _const_spec((PLE_DIM, D))],
        out_specs=[_row_spec(tm, D), _row_spec(tm, D)],
        out_shape=[jax.ShapeDtypeStruct((S, D), BF16)] * 2,
        compiler_params=_params("parallel"),
    )(dh, gpre, p, wpu)


def loss_head(h, g, tgt):
    S, D = h.shape
    tm = min(ROW_TILE, S)

    def body(h_ref, g_ref, t_ref, dh_ref, dhb_ref, l_ref, dg_ref):
        @pl.when(pl.program_id(0) == 0)
        def _():
            l_ref[...] = jnp.zeros_like(l_ref)
            dg_ref[...] = jnp.zeros_like(dg_ref)

        x = h_ref[...]
        gr = g_ref[...]
        rs = lax.rsqrt(jnp.mean(x * x, axis=1, keepdims=True) + EPS)
        xh = x * rs
        err = xh * gr - t_ref[...]
        l_ref[0:1, 0:1] += 0.5 * _colsum(jnp.mean(err * err, axis=1, keepdims=True))
        dy = err * (1.0 / D)
        dg_ref[0:1, :] += _colsum(dy * xh)
        dxh = dy * gr
        dh = rs * (dxh - xh * jnp.mean(dxh * xh, axis=1, keepdims=True))
        dh_ref[...] = dh
        dhb_ref[...] = dh.astype(BF16)

    return pl.pallas_call(
        body, name="loss_head", grid=(S // tm,),
        in_specs=[_row_spec(tm, D), _const_spec((1, D)), _row_spec(tm, D)],
        out_specs=[_row_spec(tm, D), _row_spec(tm, D), _const_spec((8, 128)), _const_spec((8, D))],
        out_shape=[jax.ShapeDtypeStruct((S, D), F32), jax.ShapeDtypeStruct((S, D), BF16),
                   jax.ShapeDtypeStruct((8, 128), F32), jax.ShapeDtypeStruct((8, D), F32)],
        compiler_params=_params("arbitrary"),
    )(h, g, tgt)


def _lb_rows(l_ref):
    l = l_ref[...]
    e = jnp.exp(l - jnp.max(l, axis=0, keepdims=True))
    p = e / _colsum(e)
    lbs, run = [], None
    for i in range(DEPTH):
        run = p[i:i + 1] if i == 0 else run + p[i:i + 1]
        lbs.append(run - p[0:1])
    return p, lbs


def lb_fwd(lb_logits):
    def body(l_ref, o_ref):
        _, lbs = _lb_rows(l_ref)
        o_ref[...] = jnp.zeros_like(o_ref)
        for i, lb in enumerate(lbs):
            o_ref[8 * i:8 * i + 1, :] = jnp.log(jnp.maximum(lb, LB_FLOOR))
            o_ref[8 * i + 1:8 * i + 2, :] = jnp.log1p(-lb)
            o_ref[8 * i + 2:8 * i + 3, :] = 1.0 - lb
            o_ref[8 * i + 3:8 * i + 4, :] = lb

    return pl.pallas_call(
        body, name="lb_fwd",
        out_shape=jax.ShapeDtypeStruct((DEPTH * 8, HG_WIDTH), F32),
    )(lb_logits)


def lb_bwd(dlbrows, lb_logits):
    def body(d_ref, l_ref, o_ref):
        p, lbs = _lb_rows(l_ref)
        dlb = []
        for i, lb in enumerate(lbs):
            da = d_ref[8 * i:8 * i + 1, :]
            dc = d_ref[8 * i + 1:8 * i + 2, :]
            do = d_ref[8 * i + 2:8 * i + 3, :]
            dlb.append(jnp.where(lb > LB_FLOOR, da / jnp.maximum(lb, LB_FLOOR), 0.0) - dc / (1.0 - lb) - do)
        dp = [jnp.zeros_like(dlb[0])]
        for j in range(1, DEPTH):
            acc = dlb[j]
            for i in range(j + 1, DEPTH):
                acc = acc + dlb[i]
            dp.append(acc)
        dot_ = p[0:1] * dp[0]
        for j in range(1, DEPTH):
            dot_ = dot_ + p[j:j + 1] * dp[j]
        o_ref[...] = jnp.zeros_like(o_ref)
        for j in range(DEPTH):
            o_ref[j:j + 1, :] = p[j:j + 1] * (dp[j] - dot_)

    return pl.pallas_call(
        body, name="lb_bwd",
        out_shape=jax.ShapeDtypeStruct((8, HG_WIDTH), F32),
    )(dlbrows, lb_logits)


def adamw(w, g, m, v, *, name):
    R, C = w.shape
    tr = next((t for t in (512, 256, 128, 64, 32, 16, 8) if R % t == 0), R)

    def body(w_ref, g_ref, m_ref, v_ref, d_ref, m2_ref, v2_ref):
        gv = g_ref[...]
        m2 = ADAM_B1 * m_ref[...] + (1.0 - ADAM_B1) * gv
        v2 = ADAM_B2 * v_ref[...] + (1.0 - ADAM_B2) * (gv * gv)
        mh = m2 / (1.0 - ADAM_B1 ** ADAM_STEP)
        vh = v2 / (1.0 - ADAM_B2 ** ADAM_STEP)
        d_ref[...] = -ADAM_LR * (mh / (jnp.sqrt(vh) + ADAM_EPS) + ADAM_WD * w_ref[...])
        m2_ref[...] = m2
        v2_ref[...] = v2

    spec = pl.BlockSpec((tr, C), lambda i: (i, 0))
    return pl.pallas_call(
        body, name=name, grid=(R // tr,),
        in_specs=[spec] * 4, out_specs=[spec] * 3,
        out_shape=[jax.ShapeDtypeStruct((R, C), F32)] * 3,
        compiler_params=_params("parallel"),
    )(w, g, m, v)


def _slot_rows(R):
    return R if R <= 1024 else next((t for t in (848, 768, 704, 672, 512, 448, 352, 256, 128, 64, 16) if R % t == 0), R)


def add_pair(a, b):
    R, C = a.shape
    tr = _slot_rows(R)

    def body(a_ref, b_ref, o_ref):
        o_ref[...] = (a_ref[...].astype(F32) + b_ref[...].astype(F32)).astype(o_ref.dtype)

    spec = pl.BlockSpec((tr, C), lambda i: (i, 0))
    return pl.pallas_call(
        body, name="sum_pair", grid=(R // tr,),
        in_specs=[spec, spec], out_specs=spec,
        out_shape=jax.ShapeDtypeStruct((R, C), a.dtype),
        compiler_params=_params("parallel"),
    )(a, b)


def sum_slots(x, *, out_dtype, name):
    n, R, C = x.shape
    tr = _slot_rows(R)

    def body(x_ref, o_ref):
        acc = x_ref[0].astype(F32)
        for k in range(1, n):
            acc = acc + x_ref[k].astype(F32)
        o_ref[...] = acc.astype(o_ref.dtype)

    return pl.pallas_call(
        body, name=name, grid=(R // tr,),
        in_specs=[pl.BlockSpec((n, tr, C), lambda i: (0, i, 0))],
        out_specs=pl.BlockSpec((tr, C), lambda i: (i, 0)),
        out_shape=jax.ShapeDtypeStruct((R, C), out_dtype),
        compiler_params=_params("parallel"),
    )(x)


MESH = pl.DeviceIdType.MESH
ANY = pl.BlockSpec(memory_space=pl.ANY)


def _place():
    return lax.axis_index("x"), lax.axis_index("y"), lax.axis_index("c")


def _other_chips(x, y):
    return [(1 - x, y), (x, 1 - y), (1 - x, 1 - y)]


def small_allgather(buf):
    R, C = buf.shape

    def body(x_ref, out_ref, send_sems, recv_sems, local_sem):
        x, y, c = _place()
        me, sibling = (x, y, c), (x, y, 1 - c)
        chips = _other_chips(x, y)

        def slot(px, py, pc):
            return out_ref.at[4 * px + 2 * py + pc]

        def copy(k, block, to, src=None):
            return pltpu.make_async_remote_copy(
                src_ref=slot(*block) if src is None else src, dst_ref=slot(*block),
                send_sem=send_sems.at[k], recv_sem=recv_sems.at[k],
                device_id=to, device_id_type=MESH)

        mine = pltpu.make_async_copy(x_ref, slot(*me), local_sem)
        mine.start()
        first = [copy(0, me, sibling, src=x_ref)]
        first += [copy(1 + r, me, (*chip, c), src=x_ref) for r, chip in enumerate(chips)]
        for cp in first:
            cp.start()
        passed = [copy(4 + r, (*chip, c), sibling) for r, chip in enumerate(chips)]
        for r, chip in enumerate(chips):
            copy(1 + r, (*chip, c), me).wait_recv()
            passed[r].start()
        copy(0, sibling, me).wait_recv()
        for r, chip in enumerate(chips):
            copy(4 + r, (*chip, 1 - c), me).wait_recv()
        for cp in first + passed:
            cp.wait_send()
        mine.wait()

    return pl.pallas_call(
        body, name="small_allgather",
        out_shape=jax.ShapeDtypeStruct((8, R, C), buf.dtype),
        in_specs=[pl.BlockSpec(memory_space=pltpu.VMEM)],
        out_specs=pl.BlockSpec(memory_space=pltpu.VMEM),
        scratch_shapes=[pltpu.SemaphoreType.DMA((7,)), pltpu.SemaphoreType.DMA((7,)),
                        pltpu.SemaphoreType.DMA],
    )(buf)


def weights_allgather(wp):
    def body(w_ref, g_ref, send_sems, recv_sems, local_sem):
        gather = _Gather(w_ref, g_ref, send_sems, recv_sems, local_sem)
        gather.start()
        gather.finish()

    return pl.pallas_call(
        body, name="weights_allgather",
        out_shape=jax.ShapeDtypeStruct((4,) + wp.shape, wp.dtype),
        in_specs=[ANY], out_specs=ANY,
        scratch_shapes=_Gather.SCRATCH,
    )(wp)


class _Gather:
    SCRATCH = [pltpu.SemaphoreType.DMA((6,)), pltpu.SemaphoreType.DMA((6,)), pltpu.SemaphoreType.DMA]

    def __init__(self, w_ref, g_ref, send_sems, recv_sems, local_sem):
        self.w_ref, self.g_ref, self.local_sem = w_ref, g_ref, local_sem
        self.send_sems, self.recv_sems = send_sems, recv_sems
        self.x, self.y, self.c = _place()
        self.chips = _other_chips(self.x, self.y)
        half = w_ref.shape[0] // 2
        self.mine = pl.ds(pl.multiple_of(self.c * half, 16), half)
        self.theirs = pl.ds(pl.multiple_of((1 - self.c) * half, 16), half)

    def _copy(self, k, chip_block, rows, to, src=None):
        dst = self.g_ref.at[chip_block, rows]
        return pltpu.make_async_remote_copy(
            src_ref=dst if src is None else src, dst_ref=dst,
            send_sem=self.send_sems.at[k], recv_sem=self.recv_sems.at[k],
            device_id=to, device_id_type=MESH)

    def _own(self):
        return pltpu.make_async_copy(self.w_ref, self.g_ref.at[2 * self.x + self.y], self.local_sem)

    def _first(self):
        return [self._copy(r, 2 * self.x + self.y, self.mine, (*chip, self.c), src=self.w_ref.at[self.mine])
                for r, chip in enumerate(self.chips)]

    def start(self):
        self._own().start()
        for cp in self._first():
            cp.start()

    def finish(self):
        sibling = (self.x, self.y, 1 - self.c)
        passed = [self._copy(3 + r, 2 * chip[0] + chip[1], self.mine, sibling) for r, chip in enumerate(self.chips)]
        for r, chip in enumerate(self.chips):
            self._copy(r, 2 * chip[0] + chip[1], self.mine, (*chip, self.c)).wait_recv()
            passed[r].start()
        for r, chip in enumerate(self.chips):
            self._copy(3 + r, 2 * chip[0] + chip[1], self.theirs, sibling).wait_recv()
        for cp in self._first() + passed:
            cp.wait_send()
        self._own().wait()


def sibling_swap(v, *, name):
    def body(v_ref, got_ref, send_sem, recv_sem):
        x, y, c = _place()
        cp = pltpu.make_async_remote_copy(
            src_ref=v_ref, dst_ref=got_ref, send_sem=send_sem, recv_sem=recv_sem,
            device_id=(x, y, 1 - c), device_id_type=MESH)
        cp.start()
        cp.wait()

    return pl.pallas_call(
        body, name=name,
        out_shape=jax.ShapeDtypeStruct(v.shape, v.dtype),
        in_specs=[ANY], out_specs=ANY,
        scratch_shapes=[pltpu.SemaphoreType.DMA, pltpu.SemaphoreType.DMA],
    )(v)


def chip_exchange(q):
    def body(q_ref, r_ref, send_sems, recv_sems, local_sem):
        exchange = _Exchange(q_ref, r_ref, send_sems, recv_sems, local_sem)
        exchange.start()
        exchange.finish()

    return pl.pallas_call(
        body, name="chip_exchange",
        out_shape=jax.ShapeDtypeStruct(q.shape, q.dtype),
        in_specs=[ANY], out_specs=ANY,
        scratch_shapes=_Exchange.SCRATCH,
    )(q)


class _Exchange:
    SCRATCH = [pltpu.SemaphoreType.DMA((3,)), pltpu.SemaphoreType.DMA((3,)), pltpu.SemaphoreType.DMA]

    def __init__(self, q_ref, r_ref, send_sems, recv_sems, local_sem):
        self.q_ref, self.r_ref, self.local_sem = q_ref, r_ref, local_sem
        self.send_sems, self.recv_sems = send_sems, recv_sems
        self.x, self.y, self.c = _place()
        self.j = 2 * self.x + self.y
        self.chips = _other_chips(self.x, self.y)

    def _copy(self, r, src_block, dst_block, chip):
        return pltpu.make_async_remote_copy(
            src_ref=self.q_ref.at[src_block], dst_ref=self.r_ref.at[dst_block],
            send_sem=self.send_sems.at[r], recv_sem=self.recv_sems.at[r],
            device_id=(*chip, self.c), device_id_type=MESH)

    def _own(self):
        return pltpu.make_async_copy(self.q_ref.at[self.j], self.r_ref.at[self.j], self.local_sem)

    def _sends(self):
        return [self._copy(r, 2 * chip[0] + chip[1], self.j, chip) for r, chip in enumerate(self.chips)]

    def start(self):
        self._own().start()
        for cp in self._sends():
            cp.start()

    def finish(self):
        for r, chip in enumerate(self.chips):
            jr = 2 * chip[0] + chip[1]
            self._copy(r, jr, jr, chip).wait_recv()
        for cp in self._sends():
            cp.wait_send()
        self._own().wait()


N_CHIPS = 4
_PACK = (("w_in", 704), ("w_out", 256), ("w_up", 1408), ("w_down", 704), ("w_ple_gate", 256), ("w_ple_up", 64))
LAYER_ROWS = sum(r for _, r in _PACK)
PACK_ROWS = DEPTH * LAYER_ROWS


def _pack_shards(sh):
    parts = []
    for i in range(DEPTH):
        for name, rows in _PACK:
            parts.append(sh[name][i].reshape(rows, D_MODEL))
    return jnp.concatenate(parts, axis=0)


def _unpack_shards(slab):
    shapes = {"w_in": (D_MODEL, IN_WIDTH // N_CHIPS), "w_out": (D_MODEL // N_CHIPS, D_MODEL),
              "w_up": (D_MODEL, 2 * D_FF // N_CHIPS), "w_down": (D_FF // N_CHIPS, D_MODEL),
              "w_ple_gate": (D_MODEL // N_CHIPS, D_MODEL), "w_ple_up": (PLE_DIM, D_MODEL // N_CHIPS)}
    out = {name: [] for name, _ in _PACK}
    off = 0
    for i in range(DEPTH):
        for name, rows in _PACK:
            out[name].append(slab[off:off + rows].reshape(shapes[name]))
            off += rows
    return {k: jnp.stack(v) for k, v in out.items()}


_COL_SHARDED = ("w_in", "w_up", "w_ple_up")


def _full_from_chips(g, pack=_PACK):
    per_chip = [_unpack_shards_layer(g[k], pack) for k in range(N_CHIPS)]
    return {name: jnp.concatenate([pc[name] for pc in per_chip], axis=1 if name in _COL_SHARDED else 0)
            for name, _ in pack}


def _unpack_shards_layer(slab, pack):
    shapes = {"w_in": (D_MODEL, IN_WIDTH // N_CHIPS), "w_out": (D_MODEL // N_CHIPS, D_MODEL),
              "w_up": (D_MODEL, 2 * D_FF // N_CHIPS), "w_down": (D_FF // N_CHIPS, D_MODEL),
              "w_ple_gate": (D_MODEL // N_CHIPS, D_MODEL), "w_ple_up": (PLE_DIM, D_MODEL // N_CHIPS)}
    out = {}
    off = 0
    for name, rows in pack:
        out[name] = slab[off:off + rows].reshape(shapes[name])
        off += rows
    return out


def _split_to_chips(full, name):
    r, c = full.shape
    if name in _COL_SHARDED:
        full = full.reshape(r, N_CHIPS, c // N_CHIPS).transpose(1, 0, 2)
    return full.reshape(N_CHIPS, -1, D_MODEL)


_SMALL = (("loss", 128), ("g_final", 1024), ("g_mix", 4096), ("lb_logits", 2048), ("hg_norm_g", 2048),
          ("attn_sinks", 128), ("g_ffn", 4096), ("conv_w", 4 * 3 * D_FF), ("conv_b", 4 * D_FF), ("g_ple", 4096))
SMALL_ROWS = 496


def _pack_small(d):
    parts = []
    for name, n in _SMALL:
        v = d[name].reshape(-1).astype(F32)
        parts.append(jnp.pad(v, (0, n - v.shape[0])))
    flat = jnp.concatenate(parts)
    return jnp.pad(flat, (0, SMALL_ROWS * 128 - flat.shape[0])).reshape(SMALL_ROWS, 128)


def _unpack_small(buf, shapes):
    flat = buf.reshape(-1)
    out, off = {}, 0
    for name, n in _SMALL:
        size = 1
        for s in shapes[name]:
            size *= s
        out[name] = flat[off:off + size].reshape(shapes[name])
        off += n
    return out


WEIGHT_ORDER = ('g_mix', 'w_in', 'lb_logits', 'hg_norm_g', 'attn_sinks', 'w_out', 'g_ffn', 'w_up', 'conv_w',
                'conv_b', 'w_down', 'g_ple', 'w_ple_gate', 'w_ple_up', 'g_final')


def kernel(x, p, g_mix, w_in, lb_logits, hg_norm_g, attn_sinks, w_out, g_ffn, w_up, conv_w, conv_b, w_down, g_ple, w_ple_gate, w_ple_up, g_final, loss_target, m_g_mix, m_w_in, m_lb_logits, m_hg_norm_g, m_attn_sinks, m_w_out, m_g_ffn, m_w_up, m_conv_w, m_conv_b, m_w_down, m_g_ple, m_w_ple_gate, m_w_ple_up, m_g_final, v_g_mix, v_w_in, v_lb_logits, v_hg_norm_g, v_attn_sinks, v_w_out, v_g_ffn, v_w_up, v_conv_w, v_conv_b, v_w_down, v_g_ple, v_w_ple_gate, v_w_ple_up, v_g_final):
    W = dict(g_mix=g_mix, w_in=w_in, lb_logits=lb_logits, hg_norm_g=hg_norm_g, attn_sinks=attn_sinks,
             w_out=w_out, g_ffn=g_ffn, w_up=w_up, conv_w=conv_w, conv_b=conv_b, w_down=w_down, g_ple=g_ple,
             w_ple_gate=w_ple_gate, w_ple_up=w_ple_up, g_final=g_final)
    M = dict(g_mix=m_g_mix, w_in=m_w_in, lb_logits=m_lb_logits, hg_norm_g=m_hg_norm_g, attn_sinks=m_attn_sinks,
             w_out=m_w_out, g_ffn=m_g_ffn, w_up=m_w_up, conv_w=m_conv_w, conv_b=m_conv_b, w_down=m_w_down,
             g_ple=m_g_ple, w_ple_gate=m_w_ple_gate, w_ple_up=m_w_ple_up, g_final=m_g_final)
    V = dict(g_mix=v_g_mix, w_in=v_w_in, lb_logits=v_lb_logits, hg_norm_g=v_hg_norm_g, attn_sinks=v_attn_sinks,
             w_out=v_w_out, g_ffn=v_g_ffn, w_up=v_w_up, conv_w=v_conv_w, conv_b=v_conv_b, w_down=v_w_down,
             g_ple=v_g_ple, w_ple_gate=v_w_ple_gate, w_ple_up=v_w_ple_up, g_final=v_g_final)
    S = x.shape[1]
    hg_rows = 2 * ROW_TILE if S % (2 * ROW_TILE) == 0 else min(ROW_TILE, S)
    xi, yi, ci = _place()
    chip = 2 * xi + yi

    slab = _pack_shards({n: W[n] for n, _ in _PACK}).astype(BF16).reshape(DEPTH, LAYER_ROWS, D_MODEL)
    first_rows = _PACK[0][1]
    gathered = (weights_allgather(slab[0, :first_rows]),)
    cw_shard = jnp.pad(conv_w.reshape(-1), (0, 72 * 128 - conv_w.size)).reshape(72, 128)
    cw_all = small_allgather(cw_shard)
    cw_full = jnp.concatenate(
        [cw_all[2 * k].reshape(-1)[:conv_w.size].reshape(conv_w.shape) for k in range(N_CHIPS)], axis=2)
    lbrows = lb_fwd(lb_logits)
    at_mask = swa_mask()

    h = x[0]
    saved = []
    for i in range(DEPTH):
        wf = _full_from_chips(gathered[0], _PACK[:1] if i == 0 else _PACK)
        lbr = lbrows[8 * i:8 * i + 8]
        ng = hg_norm_g[i][None]
        sinks_b = jnp.pad(jnp.repeat(attn_sinks[i].reshape(AT_KV_HEADS, 1, AT_GROUP), WINDOW, axis=2),
                          ((0, 0), (0, 7), (0, 0))).reshape(8 * AT_KV_HEADS, GROUP_LANES)
        cw8 = jnp.pad(cw_full[i], ((0, 5), (0, 0)))
        cb = conv_b[i][None]
        if i == 0:
            u = rmsnorm_fwd(h, g_mix[0][None], name="rmsnorm_fwd")
        proj = mm(u, wf["w_in"], name="mm_in")
        riders = ([slab[0, first_rows:]] if i == 0 else []) + ([slab[i + 1]] if i + 1 < DEPTH else [])
        y, o_raw, states, *got = hgrn_fwd(proj, lbr, ng, D_MODEL, rows=hg_rows, gather=riders)
        if i == 0:
            wf.update(_full_from_chips(got[0], _PACK[1:]))
        gathered = got[-1:]
        y = swa_fwd(proj, sinks_b, at_mask, y)
        h1, u2 = mm(y, wf["w_out"], res=h, norm_g=g_ffn[i][None], name="mm_out")
        hh = mm(u2, wf["w_up"], out_dtype=BF16, name="mm_up")
        act, conv = convffn_fwd(hh, cw8, cb)
        h2, u3 = mm(act, wf["w_down"], res=h1, norm_g=g_ple[i][None], name="mm_down")
        gpre = mm(u3, wf["w_ple_gate"], out_dtype=BF16, name="mm_gate")
        next_g = g_mix[i + 1] if i + 1 < DEPTH else g_final
        h3, u_next = ple_fwd(h2, gpre, p[i, 0], wf["w_ple_up"], next_g[None])
        saved.append(dict(wf=wf, lbr=lbr, ng=ng, sinks_b=sinks_b, cw8=cw8, cb=cb, h=h, u=u, proj=proj,
                          o_raw=o_raw, states=states, y=y, h1=h1, u2=u2, hh=hh, conv=conv, act=act, h2=h2, u3=u3,
                          gpre=gpre))
        h, u = h3, u_next

    dh, dhb, loss_acc, dg_final = loss_head(h, g_final[None], loss_target[0])

    gfull = {n: [None] * DEPTH for n, _ in _PACK}
    gsmall = {n: [None] * DEPTH for n in ("g_mix", "hg_norm_g", "attn_sinks", "g_ffn", "conv_w", "conv_b", "g_ple")}
    dlbrows = [None] * DEPTH
    def pair_sums(i, pack):
        pk = jnp.concatenate([_split_to_chips(gfull[name][i], name) for name, _ in pack], axis=1).astype(BF16)
        half = pk.shape[1] // 2
        pk = pk.reshape(N_CHIPS, 2, half, D_MODEL)
        p_mine = lax.dynamic_index_in_dim(pk, ci, axis=1, keepdims=False)
        p_other = lax.dynamic_index_in_dim(pk, 1 - ci, axis=1, keepdims=False)
        from_sib = sibling_swap(p_other, name="sibling_swap_partials")
        pair = add_pair(p_mine.reshape(-1, D_MODEL), from_sib.reshape(-1, D_MODEL))
        return pair.reshape(N_CHIPS, half, D_MODEL)

    from_chips = {}
    pending = []
    for i in reversed(range(DEPTH)):
        s = saved[i]
        wf = s["wf"]
        dpu, dgp = ple_bwd(dh, s["gpre"], p[i, 0], wf["w_ple_up"])
        gfull["w_ple_up"][i] = mm_tn(p[i, 0], dpu, name="mm_tn_pu")
        gfull["w_ple_gate"][i] = mm_tn(s["u3"], dgp, name="mm_tn_gate")
        dh2, dh2b, dg = mm(dgp, wf["w_ple_gate"], nt=True, rms_bwd=(s["h2"], g_ple[i][None], dh),
                           name="mm_nt_gate")
        gsmall["g_ple"][i] = dg[0]
        gfull["w_down"][i] = mm_tn(s["act"], dh2b, name="mm_tn_down")
        dact = mm(dh2b, wf["w_down"], nt=True, out_dtype=BF16, name="mm_nt_down")
        da, db, dcw = convffn_bwd(s["hh"], s["conv"], dact, s["cw8"])
        gsmall["conv_w"][i] = dcw[0:3]
        gsmall["conv_b"][i] = dcw[3]
        gfull["w_up"][i] = jnp.concatenate([mm_tn(s["u2"], da, name="mm_tn_up"),
                                            mm_tn(s["u2"], db, name="mm_tn_up")], axis=1)
        dh1, dh1b, dg = mm((da, db), wf["w_up"], nt=True, rms_bwd=(s["h1"], g_ffn[i][None], dh2),
                           name="mm_nt_up")
        gsmall["g_ffn"][i] = dg[0]
        gfull["w_out"][i] = mm_tn(s["y"], dh1b, name="mm_tn_out")
        dy = mm(dh1b, wf["w_out"], nt=True, name="mm_nt_out")
        dq_at, dko, dkp, dvo, dvp, dsk = swa_bwd(s["proj"], s["sinks_b"], at_mask, dy)
        gsmall["attn_sinks"][i] = dsk.reshape(AT_KV_HEADS, 8, GROUP_LANES)[:, 1:1 + AT_GROUP, 0].reshape(-1)
        hg_args = (s["proj"], s["o_raw"], s["states"], dy, s["lbr"], s["ng"])
        riders = pending + ([("0 rest", pair_sums(0, _PACK[1:]))] if i == 0 else [])
        hq, hz, hv, hgp, dlbr, dng, *got = hgrn_bwd(*hg_args, rows=hg_rows, exchange=[q for _, q in riders])
        from_chips.update({unit: g for (unit, _), g in zip(riders, got)})
        dlbrows[i] = dlbr
        gsmall["hg_norm_g"][i] = dng[0]
        dproj = assemble_dproj((hq, hz, hv, hgp), dq_at, dko, dkp, dvo, dvp, rows=hg_rows)
        gfull["w_in"][i] = mm_tn(s["u"], dproj, name="mm_tn_in")
        dh, dhb, dg = mm(dproj, wf["w_in"], nt=True, rms_bwd=(s["h"], g_mix[i][None], dh1), name="mm_nt_in")
        gsmall["g_mix"][i] = dg[0]
        pending = [("%d" % i, pair_sums(i, _PACK))] if i > 0 else [("0 w_in", pair_sums(0, _PACK[:1]))]
    from_chips["0 w_in"] = chip_exchange(pending[0][1])
    grad_x = dh[None]
    dlb_logits = lb_bwd(jnp.concatenate(dlbrows, axis=0), lb_logits)[0:DEPTH]

    units = ["0 w_in", "0 rest"] + ["%d" % i for i in range(1, DEPTH)]
    sums = [sum_slots(from_chips[unit], out_dtype=F32, name="sum_chips") for unit in units]
    mine_sum = jnp.concatenate(sums, axis=0)
    sib_sum = sibling_swap(mine_sum, name="sibling_swap_sums")
    lo = jnp.where(ci == 0, mine_sum, sib_sum)
    hi = jnp.where(ci == 0, sib_sum, mine_sum)
    parts, off = [], 0
    for part in sums:
        n = part.shape[0]
        parts += [lo[off:off + n], hi[off:off + n]]
        off += n
    gshard = _unpack_shards(jnp.concatenate(parts, axis=0))

    small = dict(loss=loss_acc[0, 0:1], g_final=dg_final[0], lb_logits=dlb_logits,
                 **{n: jnp.stack(v) for n, v in gsmall.items()})
    small_sum = sum_slots(small_allgather(_pack_small(small)), out_dtype=F32, name="sum_small")
    shapes = {n: W[n].shape for n in W}
    shapes["loss"] = (1,)
    shapes["conv_w"] = (DEPTH, 3, D_FF)
    gs = _unpack_small(small_sum, shapes)
    loss = gs["loss"][0]
    cshard = conv_w.shape[2]
    grads = dict(gshard)
    for n in ("g_mix", "lb_logits", "hg_norm_g", "attn_sinks", "g_ffn", "conv_b", "g_ple", "g_final"):
        grads[n] = gs[n]
    grads["conv_w"] = lax.dynamic_slice_in_dim(gs["conv_w"], chip * cshard, cshard, axis=2)

    delta, new_m, new_v = {}, {}, {}
    small_names = ("g_final", "g_mix", "lb_logits", "hg_norm_g", "attn_sinks", "g_ffn", "conv_b", "g_ple")
    sshapes = {n: W[n].shape for n in small_names}

    def pack_s(d):
        z = dict(d)
        z["loss"] = jnp.zeros((1,), F32)
        z["conv_w"] = jnp.zeros((1,), F32)
        return _pack_small(z)

    sd, sm, sv = adamw(pack_s(W), pack_s(grads), pack_s(M), pack_s(V), name="adamw_small")
    for out, buf in ((delta, sd), (new_m, sm), (new_v, sv)):
        un = _unpack_small(buf, {**sshapes, "loss": (1,), "conv_w": (1,)})
        for n in small_names:
            out[n] = un[n]
    for n in ("w_in", "w_out", "w_up", "w_down", "w_ple_gate", "w_ple_up", "conv_w"):
        shp = W[n].shape
        two_d = (-1, shp[-1])
        d_, m_, v_ = adamw(W[n].reshape(two_d), grads[n].reshape(two_d), M[n].reshape(two_d),
                           V[n].reshape(two_d), name="adamw_" + n)
        delta[n], new_m[n], new_v[n] = d_.reshape(shp), m_.reshape(shp), v_.reshape(shp)

    return (loss, grad_x, *[grads[n] for n in WEIGHT_ORDER], *[delta[n] for n in WEIGHT_ORDER],
            *[new_m[n] for n in WEIGHT_ORDER], *[new_v[n] for n in WEIGHT_ORDER])
```
